```python
import math
import jax
import jax.numpy as jnp
from jax import lax
import numpy as np

D_MODEL = 1024
BATCH = 2
SEQ = 8192
DEPTH = 1

N_MEM = 256
HEAD_DIM = 64
NSA_HEADS = 8
NSA_KV_HEADS = 2
NSA_GROUP = NSA_HEADS // NSA_KV_HEADS
NSA_WIDTH = NSA_HEADS * HEAD_DIM
KV_WIDTH = NSA_KV_HEADS * HEAD_DIM
RWKV_HEADS = 8
RWKV_WIDTH = RWKV_HEADS * HEAD_DIM
MIX_WIDTH = NSA_WIDTH + RWKV_WIDTH
CMP_LEN = 32
CMP_STRIDE = 16
CMP_HIDDEN = 256
SEL_BLOCK = 64
SEL_TOP = 16
WINDOW = 512
Q_BLOCK = 128
DECAY_LORA = 64
AAA_LORA = 64
GATE_LORA = 128
N_BUCKETS = 32
MAX_DISTANCE = 2048
XATTN_HEADS = 4
XATTN_HEAD = D_MODEL // XATTN_HEADS
D_FF = -(-(8 * D_MODEL) // (3 * 256)) * 256
RMS_EPS = 1e-6
LNX_EPS = 64e-5
FORCE_SCORE = 1e4
NEG_SCORE = -1e9
NSA_SIZES = (NSA_WIDTH,) + (KV_WIDTH,) * 6 + (NSA_HEADS * 3,)
RWKV_SIZES = (RWKV_WIDTH,) * 3 + (DECAY_LORA, AAA_LORA, GATE_LORA)
NSA_COLS = sum(NSA_SIZES)
RWKV_COLS = sum(RWKV_SIZES)
IN_COLS = NSA_COLS + RWKV_COLS

kernel_name = 'hybrid_nsa_rwkv7_block'


def _split(z, sizes):
    return jnp.split(z, np.cumsum(sizes)[:-1].tolist(), axis=-1)


def rms_norm(x, g):
    x32 = x.astype(jnp.float32)
    y = x32 * lax.rsqrt(jnp.mean(x32 * x32, axis=-1, keepdims=True) + RMS_EPS)
    return (y * g.astype(jnp.float32)).astype(x.dtype)


def masked_softmax(s, mask):
    s = jnp.where(mask, s.astype(jnp.float32), -1e30)
    m = jnp.max(s, axis=-1, keepdims=True)
    e = jnp.where(mask, jnp.exp(s - m), 0.0)
    return e / jnp.maximum(jnp.sum(e, axis=-1, keepdims=True), 1e-30)


def t5_bucket(dist):
    n = jnp.maximum(dist, 0)
    max_exact = N_BUCKETS // 2
    nf = jnp.maximum(n, 1).astype(jnp.float32)
    large = max_exact + (jnp.log(nf / max_exact) / math.log(MAX_DISTANCE / max_exact)
                         * (N_BUCKETS - max_exact)).astype(jnp.int32)
    return jnp.where(n < max_exact, n, jnp.minimum(large, N_BUCKETS - 1))


def nsa_compress(kv, pe, w1, b1, w2):
    B, T = kv.shape[:2]
    n_cmp = (T - CMP_LEN) // CMP_STRIDE + 1
    idx = np.arange(n_cmp)[:, None] * CMP_STRIDE + np.arange(CMP_LEN)[None, :]
    blk = kv[:, idx] + pe[None, None, :, None, :]
    blk = jnp.transpose(blk, (0, 3, 1, 2, 4)).reshape(B, NSA_KV_HEADS, n_cmp, CMP_LEN * HEAD_DIM)
    return jax.nn.gelu(blk @ w1 + b1) @ w2


def nsa_attention(q, k_cmp, v_cmp, k_sel, v_sel, k_win, v_win, gate_logits, rel_bias,
                  pe_k, pe_v, ck_w1, ck_b1, ck_w2, cv_w1, cv_b1, cv_w2):
    B, T = q.shape[:2]
    Hkv, G, dh = NSA_KV_HEADS, NSA_GROUP, HEAD_DIM
    qh = (q.reshape(B, T, Hkv, G, dh) * (dh ** -0.5)).transpose(0, 2, 3, 1, 4)
    kvh = lambda a: a.reshape(B, T, Hkv, dh)
    Kc = nsa_compress(kvh(k_cmp), pe_k, ck_w1, ck_b1, ck_w2)
    Vc = nsa_compress(kvh(v_cmp), pe_v, cv_w1, cv_b1, cv_w2)
    n_cmp = Kc.shape[2]
    cmp_start = np.arange(n_cmp) * CMP_STRIDE
    cmp_end = cmp_start + CMP_LEN - 1
    n_sel = T // SEL_BLOCK
    n_top = min(SEL_TOP, n_sel)
    sel_start = np.arange(n_sel) * SEL_BLOCK
    overlap = jnp.asarray(((cmp_start[:, None] <= sel_start[None, :] + SEL_BLOCK - 1)
                           & (cmp_end[:, None] >= sel_start[None, :])).astype(np.float32))
    Ks = kvh(k_sel).reshape(B, n_sel, SEL_BLOCK, Hkv, dh).transpose(0, 3, 1, 2, 4)
    Vs = kvh(v_sel).reshape(B, n_sel, SEL_BLOCK, Hkv, dh).transpose(0, 3, 1, 2, 4)
    pad = ((0, 0), (0, 0), (WINDOW, 0), (0, 0))
    Kw = jnp.pad(kvh(k_win).transpose(0, 2, 1, 3), pad)
    Vw = jnp.pad(kvh(v_win).transpose(0, 2, 1, 3), pad)
    gates = jax.nn.sigmoid(gate_logits).reshape(B, T, Hkv, G, 3).transpose(0, 2, 3, 1, 4)
    table = rel_bias.T.reshape(Hkv, G, N_BUCKETS)
    bi = jnp.arange(B)[:, None, None, None]
    hi = jnp.arange(Hkv)[None, :, None, None]
    h6 = jnp.arange(Hkv)[None, :, None, None, None, None]
    g6 = jnp.arange(G)[None, None, :, None, None, None]
    blk_id = jnp.arange(n_sel)

    def query_block(qb):
        q0 = qb * Q_BLOCK
        qblk = lax.dynamic_slice_in_dim(qh, q0, Q_BLOCK, axis=3)
        t = q0 + jnp.arange(Q_BLOCK)
        dist_c = t[:, None] - cmp_end[None, :]
        s = jnp.einsum('bhgqd,bhcd->bhgqc', qblk, Kc).astype(jnp.float32) + table[:, :, t5_bucket(dist_c)]
        p_c = masked_softmax(s, dist_c >= 0)
        o_c = jnp.einsum('bhgqc,bhcd->bhgqd', p_c.astype(Vc.dtype), Vc)
        imp = jnp.einsum('bhgqc,cj->bhqj', p_c, overlap)
        cur = t // SEL_BLOCK
        forced = (blk_id[None] == 0) | (blk_id[None] == cur[:, None]) | (blk_id[None] == cur[:, None] - 1)
        valid = blk_id[None] * SEL_BLOCK <= t[:, None]
        score = jnp.where(valid, jnp.where(forced, FORCE_SCORE, imp), NEG_SCORE)
        _, idx = lax.top_k(score, n_top)
        kg = Ks[bi, hi, idx]
        vg = Vs[bi, hi, idx]
        kpos = idx[..., None] * SEL_BLOCK + jnp.arange(SEL_BLOCK)
        dist_s = t[None, None, :, None, None] - kpos
        bias_s = table[h6, g6, t5_bucket(dist_s)[:, :, None]]
        s = jnp.einsum('bhgqd,bhqnkd->bhgqnk', qblk, kg).astype(jnp.float32) + bias_s
        s = s.reshape(B, Hkv, G, Q_BLOCK, -1)
        mask_s = (dist_s >= 0)[:, :, None].reshape(B, Hkv, 1, Q_BLOCK, -1)
        p_s = masked_softmax(s, mask_s)
        o_s = jnp.einsum('bhgqm,bhqmd->bhgqd', p_s.astype(vg.dtype), vg.reshape(B, Hkv, Q_BLOCK, -1, dh))
        kw = lax.dynamic_slice_in_dim(Kw, q0, Q_BLOCK + WINDOW, axis=2)
        vw = lax.dynamic_slice_in_dim(Vw, q0, Q_BLOCK + WINDOW, axis=2)
        spos = q0 - WINDOW + jnp.arange(Q_BLOCK + WINDOW)
        dist_w = t[:, None] - spos[None, :]
        mask_w = (dist_w >= 0) & (dist_w < WINDOW) & (spos[None, :] >= 0)
        s = jnp.einsum('bhgqd,bhkd->bhgqk', qblk, kw).astype(jnp.float32) + table[:, :, t5_bucket(dist_w)]
        p_w = masked_softmax(s, mask_w)
        o_w = jnp.einsum('bhgqk,bhkd->bhgqd', p_w.astype(vw.dtype), vw)
        g = lax.dynamic_slice_in_dim(gates, q0, Q_BLOCK, axis=3)
        return g[..., 0:1] * o_c + g[..., 1:2] * o_s + g[..., 2:3] * o_w

    out = lax.map(query_block, jnp.arange(T // Q_BLOCK))
    return out.transpose(1, 0, 4, 2, 3, 5).reshape(B, T, NSA_WIDTH)


def rwkv7_time_mix(r, k, v, wd, ad, gd, w0, w_up, a0, a_up, g_up, k_k, k_a, r_k, lnx_w, lnx_b):
    B, T, _ = r.shape
    f32 = jnp.float32
    heads = lambda z: z.reshape(B, T, RWKV_HEADS, HEAD_DIM)
    w_log = -jax.nn.softplus(-(w0 + jnp.tanh(wd) @ w_up)) - 0.5
    decay = jnp.exp(-jnp.exp(w_log.astype(f32)))
    a = jax.nn.sigmoid(a0 + ad @ a_up)
    g = jax.nn.sigmoid(gd) @ g_up
    kk = heads(k * k_k).astype(f32)
    kk = kk * lax.rsqrt(jnp.maximum(jnp.sum(kk * kk, axis=-1, keepdims=True), 1e-24))
    k = k * (1 + (a - 1) * k_a)
    tm = lambda z: jnp.swapaxes(heads(z).astype(f32), 0, 1)
    xs = (tm(r), tm(decay), tm(k), tm(v), jnp.swapaxes(-kk, 0, 1),
          jnp.swapaxes(kk * heads(a).astype(f32), 0, 1))

    def step(S, inp):
        r_t, w_t, k_t, v_t, a_t, b_t = inp
        sa = jnp.einsum('bhvk,bhk->bhv', S, a_t)
        S = S * w_t[:, :, None, :] + sa[..., None] * b_t[:, :, None, :] + v_t[..., None] * k_t[:, :, None, :]
        return S, jnp.einsum('bhvk,bhk->bhv', S, r_t)

    S0 = jnp.zeros((B, RWKV_HEADS, HEAD_DIM, HEAD_DIM), f32)
    _, y = lax.scan(step, S0, xs)
    y = jnp.swapaxes(y, 0, 1)
    mean = jnp.mean(y, axis=-1, keepdims=True)
    var = jnp.mean(jnp.square(y - mean), axis=-1, keepdims=True)
    y = ((y - mean) * lax.rsqrt(var + LNX_EPS)).reshape(B, T, RWKV_WIDTH) * lnx_w + lnx_b
    bonus = jnp.sum(heads(r) * heads(k) * r_k, axis=-1, keepdims=True) * heads(v)
    y = (y + bonus.reshape(B, T, RWKV_WIDTH).astype(f32)) * g.astype(f32)
    return y.astype(r.dtype)


def memory_cross_attention(h, mem, w_q, w_kv, w_o):
    B, T, _ = h.shape
    M = mem.shape[1]
    q = (h @ w_q).reshape(B, T, XATTN_HEADS, XATTN_HEAD)
    kv = mem @ w_kv
    k = kv[..., :D_MODEL].reshape(B, M, XATTN_HEADS, XATTN_HEAD)
    v = kv[..., D_MODEL:].reshape(B, M, XATTN_HEADS, XATTN_HEAD)
    s = jnp.einsum('bthd,bmhd->bhtm', q, k).astype(jnp.float32) * (XATTN_HEAD ** -0.5)
    p = jax.nn.softmax(s, axis=-1)
    o = jnp.einsum('bhtm,bmhd->bthd', p.astype(v.dtype), v).reshape(B, T, D_MODEL)
    return o @ w_o


def hybrid_layer(x, mem, rel_bias, norm_mix_g, w_in, nsa_gate_b, cmp_pe_k, cmp_pe_v,
                 cmp_k_w1, cmp_k_b1, cmp_k_w2, cmp_v_w1, cmp_v_b1, cmp_v_w2,
                 rwkv_mu, rwkv_w0, rwkv_w_up, rwkv_a0, rwkv_a_up, rwkv_g_up,
                 rwkv_k_k, rwkv_k_a, rwkv_r_k, rwkv_lnx_w, rwkv_lnx_b, w_out,
                 norm_x_g, norm_mem_g, w_q_x, w_kv_x, w_o_x,
                 norm_ffn_g, w_gate, w_up, w_down):
    proj = rms_norm(x, norm_mix_g) @ w_in
    nsa_part = proj[..., :NSA_COLS]
    rw = proj[..., NSA_COLS:]
    rw_prev = jnp.pad(rw[:, :-1], ((0, 0), (1, 0), (0, 0)))
    rw = rw + (rw_prev - rw) * rwkv_mu
    q, kc, vc, ks, vs, kw, vw, gl = _split(nsa_part, NSA_SIZES)
    r, k, v, wd, ad, gd = _split(rw, RWKV_SIZES)
    o_nsa = nsa_attention(q, kc, vc, ks, vs, kw, vw, gl + nsa_gate_b, rel_bias,
                          cmp_pe_k, cmp_pe_v, cmp_k_w1, cmp_k_b1, cmp_k_w2, cmp_v_w1, cmp_v_b1, cmp_v_w2)
    o_rwkv = rwkv7_time_mix(r, k, v, wd, ad, gd, rwkv_w0, rwkv_w_up, rwkv_a0, rwkv_a_up, rwkv_g_up,
                            rwkv_k_k, rwkv_k_a, rwkv_r_k, rwkv_lnx_w, rwkv_lnx_b)
    x = x + jnp.concatenate([o_nsa, o_rwkv], axis=-1) @ w_out
    x = x + memory_cross_attention(rms_norm(x, norm_x_g), rms_norm(mem, norm_mem_g), w_q_x, w_kv_x, w_o_x)
    h = rms_norm(x, norm_ffn_g)
    return x + (jax.nn.silu(h @ w_gate) * (h @ w_up)) @ w_down


def setup_inputs(seed: int = 0) -> dict:
    key = jax.random.key(seed)
    keys = iter(jax.random.split(key, 48))
    f32 = jnp.float32

    def nrm(shape, scale, stacked=True):
        shp = ((DEPTH,) + shape) if stacked else shape
        return scale * jax.random.normal(next(keys), shp, f32)

    def gain(n):
        return 1.0 + nrm((n,), 0.01)

    flat = CMP_LEN * HEAD_DIM
    return {
        'x': nrm((BATCH, SEQ, D_MODEL), 1.0, stacked=False),
        'mem': nrm((BATCH, N_MEM, D_MODEL), 1.0, stacked=False),
        'rel_bias': nrm((N_BUCKETS, NSA_HEADS), 0.2, stacked=False),
        'norm_f_g': 1.0 + nrm((D_MODEL,), 0.01, stacked=False),
        'norm_mix_g': gain(D_MODEL),
        'w_in': nrm((D_MODEL, IN_COLS), D_MODEL ** -0.5),
        'nsa_gate_b': nrm((NSA_HEADS * 3,), 0.01),
        'cmp_pe_k': nrm((CMP_LEN, HEAD_DIM), 0.02),
        'cmp_pe_v': nrm((CMP_LEN, HEAD_DIM), 0.02),
        'cmp_k_w1': nrm((flat, CMP_HIDDEN), flat ** -0.5),
        'cmp_k_b1': nrm((CMP_HIDDEN,), 0.01),
        'cmp_k_w2': nrm((CMP_HIDDEN, HEAD_DIM), CMP_HIDDEN ** -0.5),
        'cmp_v_w1': nrm((flat, CMP_HIDDEN), flat ** -0.5),
        'cmp_v_b1': nrm((CMP_HIDDEN,), 0.01),
        'cmp_v_w2': nrm((CMP_HIDDEN, HEAD_DIM), CMP_HIDDEN ** -0.5),
        'rwkv_mu': jax.random.uniform(next(keys), (DEPTH, RWKV_COLS), f32, 0.0, 1.0),
        'rwkv_w0': jax.random.uniform(next(keys), (DEPTH, RWKV_WIDTH), f32, -6.0, 0.0),
        'rwkv_w_up': nrm((DECAY_LORA, RWKV_WIDTH), 0.5 * DECAY_LORA ** -0.5),
        'rwkv_a0': nrm((RWKV_WIDTH,), 0.5),
        'rwkv_a_up': nrm((AAA_LORA, RWKV_WIDTH), 0.5 * AAA_LORA ** -0.5),
        'rwkv_g_up': nrm((GATE_LORA, RWKV_WIDTH), GATE_LORA ** -0.5),
        'rwkv_k_k': 0.85 + nrm((RWKV_WIDTH,), 0.05),
        'rwkv_k_a': 1.0 + nrm((RWKV_WIDTH,), 0.05),
        'rwkv_r_k': nrm((RWKV_HEADS, HEAD_DIM), 0.1),
        'rwkv_lnx_w': gain(RWKV_WIDTH),
        'rwkv_lnx_b': nrm((RWKV_WIDTH,), 0.01),
        'w_out': nrm((MIX_WIDTH, D_MODEL), MIX_WIDTH ** -0.5),
        'norm_x_g': gain(D_MODEL),
        'norm_mem_g': gain(D_MODEL),
        'w_q_x': nrm((D_MODEL, D_MODEL), D_MODEL ** -0.5),
        'w_kv_x': nrm((D_MODEL, 2 * D_MODEL), D_MODEL ** -0.5),
        'w_o_x': nrm((D_MODEL, D_MODEL), D_MODEL ** -0.5),
        'norm_ffn_g': gain(D_MODEL),
        'w_gate': nrm((D_MODEL, D_FF), D_MODEL ** -0.5),
        'w_up': nrm((D_MODEL, D_FF), D_MODEL ** -0.5),
        'w_down': nrm((D_FF, D_MODEL), D_FF ** -0.5),
    }


def reference(x, mem, rel_bias, norm_f_g, norm_mix_g, w_in, nsa_gate_b, cmp_pe_k, cmp_pe_v,
              cmp_k_w1, cmp_k_b1, cmp_k_w2, cmp_v_w1, cmp_v_b1, cmp_v_w2,
              rwkv_mu, rwkv_w0, rwkv_w_up, rwkv_a0, rwkv_a_up, rwkv_g_up,
              rwkv_k_k, rwkv_k_a, rwkv_r_k, rwkv_lnx_w, rwkv_lnx_b, w_out,
              norm_x_g, norm_mem_g, w_q_x, w_kv_x, w_o_x,
              norm_ffn_g, w_gate, w_up, w_down):
    for l in range(DEPTH):
        x = hybrid_layer(x, mem, rel_bias, norm_mix_g[l], w_in[l], nsa_gate_b[l], cmp_pe_k[l], cmp_pe_v[l],
                         cmp_k_w1[l], cmp_k_b1[l], cmp_k_w2[l], cmp_v_w1[l], cmp_v_b1[l], cmp_v_w2[l],
                         rwkv_mu[l], rwkv_w0[l], rwkv_w_up[l], rwkv_a0[l], rwkv_a_up[l], rwkv_g_up[l],
                         rwkv_k_k[l], rwkv_k_a[l], rwkv_r_k[l], rwkv_lnx_w[l], rwkv_lnx_b[l], w_out[l],
                         norm_x_g[l], norm_mem_g[l], w_q_x[l], w_kv_x[l], w_o_x[l],
                         norm_ffn_g[l], w_gate[l], w_up[l], w_down[l])
    return rms_norm(x, norm_f_g)
```

```python
import functools
import math

import numpy as np
import jax
import jax.numpy as jnp
from jax import lax
from jax.experimental import pallas as pl
from jax.experimental.pallas import tpu as pltpu

F32 = jnp.float32
BF16 = jnp.bfloat16
HIGHEST = lax.Precision.HIGHEST

LANES = 128
SUBLANES = 8
VMEM_LIMIT_BYTES = 56 * 1024 * 1024

HEAD_DIM = 64
NSA_HEADS = 8
NSA_KV_HEADS = 2
NSA_GROUP = NSA_HEADS // NSA_KV_HEADS
NSA_WIDTH = NSA_HEADS * HEAD_DIM
KV_WIDTH = NSA_KV_HEADS * HEAD_DIM
RWKV_HEADS = 8
RWKV_WIDTH = RWKV_HEADS * HEAD_DIM
CMP_LEN = 32
CMP_STRIDE = 16
SEL_BLOCK = 64
SEL_SHIFT = 6
SEL_TOP = 16
WINDOW = 512
Q_BLOCK = 128
DECAY_LORA = 64
AAA_LORA = 64
GATE_LORA = 128
N_BUCKETS = 32
MAX_DISTANCE = 2048
XATTN_HEADS = 4
RMS_EPS = 1e-6
LNX_EPS = 64e-5
FORCE_SCORE = 1e4
NEG_SCORE = -1e9
MASK_SCORE = -1e30
RWKV_COLS = 3 * RWKV_WIDTH + DECAY_LORA + AAA_LORA + GATE_LORA
NSA_COLS = NSA_WIDTH + 6 * KV_WIDTH + 3 * NSA_HEADS

KEY_CHUNK = 128
RWKV_CHUNK = 64
ROW_TILE = 512
FFN_ROW_TILE = 256


def _t5_thresholds():
    d = np.arange(0, 2 * MAX_DISTANCE, dtype=np.int64)
    max_exact = N_BUCKETS // 2
    nf = np.maximum(d, 1).astype(np.float32)
    large = max_exact + (np.log(nf / np.float32(max_exact)) / np.float32(math.log(MAX_DISTANCE / max_exact))
                         * np.float32(N_BUCKETS - max_exact)).astype(np.int32)
    bucket = np.where(d < max_exact, d, np.minimum(large, N_BUCKETS - 1))
    return [int(np.argmax(bucket >= k)) for k in range(N_BUCKETS)]


T5_THRESHOLDS = _t5_thresholds()
N_BIAS_TILES = -(-(T5_THRESHOLDS[-1] + KEY_CHUNK) // KEY_CHUNK) + 1


def _params(*semantics):
    return pltpu.CompilerParams(dimension_semantics=semantics, vmem_limit_bytes=VMEM_LIMIT_BYTES)


def _rms(x, g):
    return x * lax.rsqrt(jnp.mean(x * x, axis=-1, keepdims=True) + RMS_EPS) * g


def _dot(a, b, **kw):
    return jnp.dot(a, b, preferred_element_type=F32, **kw)


def _dot_nt(a, b, **kw):
    return lax.dot_general(a, b, (((1,), (1,)), ((), ())), preferred_element_type=F32, **kw)


def _norm_matmul_kernel(x_ref, g_ref, *refs, nseg, bias_flags):
    nb = sum(bias_flags)
    w_refs, b_refs, o_refs = refs[:nseg], refs[nseg:nseg + nb], refs[nseg + nb:]
    xn = _rms(x_ref[...], g_ref[...]).astype(BF16)
    bi = 0
    for s in range(nseg):
        y = _dot(xn, w_refs[s][...])
        if bias_flags[s]:
            y = y + b_refs[bi][...]
            bi += 1
        o_refs[s][...] = y.astype(o_refs[s].dtype)


def _norm_matmul(x, g, weights, biases, out_dtypes, row_tile):
    R, D = x.shape
    tm = min(row_tile, R)
    assert R % tm == 0
    nseg = len(weights)
    bias_flags = tuple(b is not None for b in biases)
    const = lambda i: (0, 0)
    in_specs = [pl.BlockSpec((tm, D), lambda i: (i, 0)), pl.BlockSpec((1, D), const)]
    in_specs += [pl.BlockSpec(w.shape, const) for w in weights]
    in_specs += [pl.BlockSpec((1, b.shape[-1]), const) for b in biases if b is not None]
    out_specs = [pl.BlockSpec((tm, w.shape[1]), lambda i: (i, 0)) for w in weights]
    out_shape = [jax.ShapeDtypeStruct((R, w.shape[1]), dt) for w, dt in zip(weights, out_dtypes)]
    return pl.pallas_call(
        functools.partial(_norm_matmul_kernel, nseg=nseg, bias_flags=bias_flags),
        grid=(R // tm,), in_specs=in_specs, out_specs=out_specs, out_shape=out_shape,
        compiler_params=_params("parallel"), name="norm_matmul",
    )(x, g.reshape(1, D), *weights, *[b.reshape(1, -1) for b in biases if b is not None])


def _compress_kernel(x_ref, pe_ref, w1_ref, b1_ref, w2_ref, o_ref, *, transpose_out):
    x = x_ref[0, 0]
    n16 = x.shape[0]
    lo = _dot((x + pe_ref[0:1, :]).astype(BF16), w1_ref[0])
    hi = _dot((x + pe_ref[1:2, :]).astype(BF16), w1_ref[1])
    h = lo + pltpu.roll(hi, n16 - 1, axis=0) + b1_ref[...]
    h = jax.nn.gelu(h).astype(BF16)
    if transpose_out:
        o_ref[0, 0] = _dot_nt(w2_ref[...], h).astype(o_ref.dtype)
    else:
        o_ref[0, 0] = _dot(h, w2_ref[...]).astype(o_ref.dtype)


def _compress(x16, pe, w1, b1, w2, transpose_out):
    B, H, n16, flat = x16.shape
    hidden = w1.shape[1]
    pe2 = pe.reshape(2, flat)
    w1s = w1.astype(BF16).reshape(2, flat, hidden)
    w2b = (w2.T if transpose_out else w2).astype(BF16)
    oshape = (B, H, HEAD_DIM, n16) if transpose_out else (B, H, n16, HEAD_DIM)
    return pl.pallas_call(
        functools.partial(_compress_kernel, transpose_out=transpose_out),
        grid=(B, H),
        in_specs=[pl.BlockSpec((1, 1, n16, flat), lambda b, h: (b, h, 0, 0)),
                  pl.BlockSpec((2, flat), lambda b, h: (0, 0)),
                  pl.BlockSpec((2, flat, hidden), lambda b, h: (0, 0, 0)),
                  pl.BlockSpec((1, hidden), lambda b, h: (0, 0)),
                  pl.BlockSpec(w2b.shape, lambda b, h: (0, 0))],
        out_specs=pl.BlockSpec((1, 1) + oshape[2:], lambda b, h: (b, h, 0, 0)),
        out_shape=jax.ShapeDtypeStruct(oshape, BF16),
        compiler_params=_params("parallel", "parallel"), name="nsa_compress",
    )(x16, pe2, w1s, b1.reshape(1, hidden), w2b)


def _bias_of_distance(tab_ref, h, d):
    val = jnp.full(d.shape, tab_ref[h, 0], F32)
    for k in range(1, N_BUCKETS):
        val = jnp.where(d >= T5_THRESHOLDS[k], tab_ref[h, k], val)
    return val


def _bias_tiles_kernel(tab_ref, bt_ref, cb_ref, *, n_cmp_pad):
    h = pl.program_id(0)
    i = lax.broadcasted_iota(jnp.int32, (Q_BLOCK, KEY_CHUNK), 0)
    j = lax.broadcasted_iota(jnp.int32, (Q_BLOCK, KEY_CHUNK), 1)
    for m in range(N_BIAS_TILES):
        bt_ref[0, m] = _bias_of_distance(tab_ref, h, m * KEY_CHUNK + i - j)
    i2 = lax.broadcasted_iota(jnp.int32, (Q_BLOCK, n_cmp_pad), 0)
    l2 = lax.broadcasted_iota(jnp.int32, (Q_BLOCK, n_cmp_pad), 1)
    d = i2 - CMP_STRIDE * l2 + (CMP_STRIDE * KEY_CHUNK - Q_BLOCK - (CMP_LEN - 1))
    d = jnp.where(l2 < KEY_CHUNK, d, 2 * MAX_DISTANCE)
    cb_ref[0] = _bias_of_distance(tab_ref, h, d)


def _bias_tiles(rel_bias, n_cmp_pad):
    assert CMP_STRIDE * KEY_CHUNK - Q_BLOCK - (CMP_LEN - 1) >= T5_THRESHOLDS[-1]
    return pl.pallas_call(
        functools.partial(_bias_tiles_kernel, n_cmp_pad=n_cmp_pad),
        grid=(NSA_HEADS,),
        in_specs=[pl.BlockSpec(memory_space=pltpu.SMEM)],
        out_specs=[pl.BlockSpec((1, N_BIAS_TILES, Q_BLOCK, KEY_CHUNK), lambda h: (h, 0, 0, 0)),
                   pl.BlockSpec((1, Q_BLOCK, n_cmp_pad), lambda h: (h, 0, 0))],
        out_shape=[jax.ShapeDtypeStruct((NSA_HEADS, N_BIAS_TILES, Q_BLOCK, KEY_CHUNK), F32),
                   jax.ShapeDtypeStruct((NSA_HEADS, Q_BLOCK, n_cmp_pad), F32)],
        compiler_params=_params("parallel"), name="t5_bias_tiles",
    )(rel_bias.T)


def _nsa_kernel(q_ref, kcT_ref, vc_ref, ksT_ref, vs_ref, kwT_ref, vw_ref, gate_ref, ov_ref, bt_ref, cb_ref,
                o_ref, m_sc, l_sc, acc_sc, *, n_sel, n_cmp_pad):
    G = NSA_GROUP
    qb = pl.program_id(2)
    q = q_ref[0]
    q4 = jnp.concatenate([q[:, g * HEAD_DIM:(g + 1) * HEAD_DIM] for g in range(G)], axis=0)
    qbf = (q4 * (HEAD_DIM ** -0.5)).astype(BF16)
    t = qb * Q_BLOCK + lax.broadcasted_iota(jnp.int32, (Q_BLOCK, 1), 0)

    s = _dot(qbf, kcT_ref[0, 0])
    shift = lax.rem((Q_BLOCK // CMP_STRIDE) * (qb + 1) - KEY_CHUNK + n_cmp_pad, n_cmp_pad)
    ci = lax.broadcasted_iota(jnp.int32, (Q_BLOCK, n_cmp_pad), 1)
    vis = (CMP_STRIDE * ci + (CMP_LEN - 1)) <= t
    psum = jnp.zeros((Q_BLOCK, n_cmp_pad), F32)
    o_c = []
    for g in range(G):
        sg = s[g * Q_BLOCK:(g + 1) * Q_BLOCK] + pltpu.roll(cb_ref[g], shift, axis=1)
        sg = jnp.where(vis, sg, MASK_SCORE)
        e = jnp.where(vis, jnp.exp(sg - jnp.max(sg, axis=-1, keepdims=True)), 0.0)
        p = e / jnp.maximum(jnp.sum(e, axis=-1, keepdims=True), 1e-30)
        psum = psum + p
        o_c.append(_dot(p.astype(BF16), vc_ref[0, 0]))
    o_c = jnp.concatenate(o_c, axis=0)

    imp = _dot(psum, ov_ref[...], precision=HIGHEST)
    bj = lax.broadcasted_iota(jnp.int32, (Q_BLOCK, n_sel), 1)
    cur = jnp.right_shift(t, SEL_SHIFT)
    forced = (bj == 0) | (bj == cur) | (bj == cur - 1)
    valid = bj * SEL_BLOCK <= t
    score = jnp.where(valid, jnp.where(forced, FORCE_SCORE, imp), NEG_SCORE)
    bjf = bj.astype(F32)
    sel = jnp.zeros((Q_BLOCK, n_sel), F32)
    for _ in range(min(SEL_TOP, n_sel)):
        mx = jnp.max(score, axis=-1, keepdims=True)
        first = jnp.min(jnp.where(score == mx, bjf, float(n_sel)), axis=-1, keepdims=True)
        pick = bjf == first
        sel = jnp.where(pick, 1.0, sel)
        score = jnp.where(pick, -jnp.inf, score)
    sel_bf = sel.astype(BF16)

    kcol = lax.broadcasted_iota(jnp.int32, (Q_BLOCK, KEY_CHUNK), 1)
    blk_row = lax.broadcasted_iota(jnp.int32, (n_sel, KEY_CHUNK), 0)
    blk_col = jnp.right_shift(lax.broadcasted_iota(jnp.int32, (n_sel, KEY_CHUNK), 1), SEL_SHIFT)

    def reset():
        m_sc[...] = jnp.full(m_sc.shape, MASK_SCORE, F32)
        l_sc[...] = jnp.zeros(l_sc.shape, F32)
        acc_sc[...] = jnp.zeros(acc_sc.shape, F32)

    def flash_step(kc, kT_ref, v_ref, mask):
        s = _dot(qbf, kT_ref[0, 0, kc])
        tile = jnp.minimum(qb - kc, N_BIAS_TILES - 1)
        p_all = []
        for g in range(G):
            rows = slice(g * Q_BLOCK, (g + 1) * Q_BLOCK)
            sg = jnp.where(mask, s[rows] + bt_ref[g, tile], MASK_SCORE)
            m_old = m_sc[rows]
            m_new = jnp.maximum(m_old, jnp.max(sg, axis=-1, keepdims=True))
            alpha = jnp.exp(m_old - m_new)
            p = jnp.where(mask, jnp.exp(sg - m_new), 0.0)
            l_sc[rows] = alpha * l_sc[rows] + jnp.sum(p, axis=-1, keepdims=True)
            acc_sc[rows] = alpha * acc_sc[rows]
            m_sc[rows] = m_new
            p_all.append(p)
        p_all = jnp.concatenate(p_all, axis=0).astype(BF16)
        k0 = pl.multiple_of(kc * KEY_CHUNK, KEY_CHUNK)
        acc_sc[...] += _dot(p_all, v_ref[0, 0, pl.ds(k0, KEY_CHUNK), :])

    def finish():
        return acc_sc[...] / jnp.maximum(l_sc[...], 1e-30)

    def sel_step(kc, carry):
        expand = (blk_row == blk_col + kc * (KEY_CHUNK // SEL_BLOCK)).astype(BF16)
        chosen = _dot(sel_bf, expand) > 0.5
        causal = (kc * KEY_CHUNK + kcol) <= t
        flash_step(kc, ksT_ref, vs_ref, chosen & causal)
        return carry

    reset()
    lax.fori_loop(0, qb + 1, sel_step, 0)
    o_s = finish()

    def win_step(kc, carry):
        kpos = kc * KEY_CHUNK + kcol
        flash_step(kc, kwT_ref, vw_ref, (kpos <= t) & (kpos > t - WINDOW))
        return carry

    reset()
    lax.fori_loop(jnp.maximum(qb - WINDOW // KEY_CHUNK, 0), qb + 1, win_step, 0)
    o_w = finish()

    gate = jax.nn.sigmoid(gate_ref[0, 0, 0])
    o = gate[:, 0:1] * o_c + gate[:, 1:2] * o_s + gate[:, 2:3] * o_w
    o_ref[0] = jnp.concatenate([o[g * Q_BLOCK:(g + 1) * Q_BLOCK] for g in range(G)], axis=1)


def _nsa_attention(q, kcT, vc, ksT, vs, kwT, vw, gates, overlap, bias_tiles, cmp_bias):
    B, T, _ = q.shape
    Hkv, G = NSA_KV_HEADS, NSA_GROUP
    nqb, nkc = T // Q_BLOCK, T // KEY_CHUNK
    n_cmp_pad = kcT.shape[-1]
    n_sel = T // SEL_BLOCK
    per_head = lambda b, h, i: (b, h, 0, 0)
    return pl.pallas_call(
        functools.partial(_nsa_kernel, n_sel=n_sel, n_cmp_pad=n_cmp_pad),
        grid=(B, Hkv, nqb),
        in_specs=[pl.BlockSpec((1, Q_BLOCK, G * HEAD_DIM), lambda b, h, i: (b, i, h)),
                  pl.BlockSpec((1, 1, HEAD_DIM, n_cmp_pad), per_head),
                  pl.BlockSpec((1, 1, n_cmp_pad, HEAD_DIM), per_head),
                  pl.BlockSpec((1, 1, nkc, HEAD_DIM, KEY_CHUNK), lambda b, h, i: (b, h, 0, 0, 0)),
                  pl.BlockSpec((1, 1, T, HEAD_DIM), per_head),
                  pl.BlockSpec((1, 1, nkc, HEAD_DIM, KEY_CHUNK), lambda b, h, i: (b, h, 0, 0, 0)),
                  pl.BlockSpec((1, 1, T, HEAD_DIM), per_head),
                  pl.BlockSpec((1, 1, 1, G * Q_BLOCK, 3), lambda b, h, i: (b, h, i, 0, 0)),
                  pl.BlockSpec(overlap.shape, lambda b, h, i: (0, 0)),
                  pl.BlockSpec((G, N_BIAS_TILES, Q_BLOCK, KEY_CHUNK), lambda b, h, i: (h, 0, 0, 0)),
                  pl.BlockSpec((G, Q_BLOCK, n_cmp_pad), lambda b, h, i: (h, 0, 0))],
        out_specs=pl.BlockSpec((1, Q_BLOCK, G * HEAD_DIM), lambda b, h, i: (b, i, h)),
        out_shape=jax.ShapeDtypeStruct((B, T, NSA_WIDTH), F32),
        scratch_shapes=[pltpu.VMEM((G * Q_BLOCK, 1), F32), pltpu.VMEM((G * Q_BLOCK, 1), F32),
                        pltpu.VMEM((G * Q_BLOCK, HEAD_DIM), F32)],
        compiler_params=_params("parallel", "parallel", "arbitrary"), name="nsa_attention",
    )(q, kcT, vc, ksT, vs, kwT, vw, gates, overlap, bias_tiles, cmp_bias)


def _rwkv_chunk_kernel(rw_ref, prev_ref, mu_ref, w0_ref, wup_ref, a0_ref, aup_ref, gup_ref, kk_ref, ka_ref,
                       rk_ref, seg_ref, tri_ref, q_ref, y0_ref, a_ref, d_ref, g_ref, bonus_ref):
    C, W, N = RWKV_CHUNK, RWKV_WIDTH, HEAD_DIM
    c = pl.program_id(1)
    x = rw_ref[0]
    row = lax.broadcasted_iota(jnp.int32, (C, 1), 0)
    last_prev = jnp.where(c == 0, 0.0, prev_ref[0, SUBLANES - 1:SUBLANES, :])
    x_prev = jnp.where(row == 0, last_prev, pltpu.roll(x, 1, axis=0))
    xs = x + (x_prev - x) * mu_ref[...]
    r, k, v = xs[:, 0:W], xs[:, W:2 * W], xs[:, 2 * W:3 * W]
    o = 3 * W
    wd, ad, gd = xs[:, o:o + DECAY_LORA], xs[:, o + DECAY_LORA:o + DECAY_LORA + AAA_LORA], \
        xs[:, o + DECAY_LORA + AAA_LORA:]
    w_log = -jax.nn.softplus(-(w0_ref[...] + _dot(jnp.tanh(wd).astype(BF16), wup_ref[...]))) - 0.5
    lw = -jnp.exp(w_log)
    lr = jax.nn.sigmoid(a0_ref[...] + _dot(ad.astype(BF16), aup_ref[...]))
    g_ref[0] = _dot(jax.nn.sigmoid(gd).astype(BF16), gup_ref[...])
    kk = k * kk_ref[...]
    kk = kk * lax.rsqrt(jnp.maximum(_dot(kk * kk, seg_ref[...], precision=HIGHEST), 1e-24))
    k = k * (1.0 + (lr - 1.0) * ka_ref[...])
    bonus_ref[0] = _dot(r * k * rk_ref[...], seg_ref[...], precision=HIGHEST) * v
    a_vec, b_vec = -kk, kk * lr

    L = _dot(tri_ref[...], lw, precision=HIGHEST)
    L_end = L[C - 1:C, :]
    e_neg = jnp.exp(-L)
    e_rem = jnp.exp(L_end - L)
    At, Bt, Kt, Rt = a_vec * jnp.exp(L - lw), b_vec * e_neg, k * e_neg, r * jnp.exp(L)
    Bg, Kg = b_vec * e_rem, k * e_rem
    decay_end = jnp.exp(L_end)

    ri = lax.broadcasted_iota(jnp.int32, (C, C), 0)
    cj = lax.broadcasted_iota(jnp.int32, (C, C), 1)
    strict, incl = ri > cj, ri >= cj
    eye_c = (ri == cj).astype(F32)
    eye_n = (lax.broadcasted_iota(jnp.int32, (N, N), 0) == lax.broadcasted_iota(jnp.int32, (N, N), 1)).astype(F32)
    for h in range(RWKV_HEADS):
        hs = slice(h * N, (h + 1) * N)
        at, bt, kt, rt, vh = At[:, hs], Bt[:, hs], Kt[:, hs], Rt[:, hs], v[:, hs]
        n_mat = jnp.where(strict, _dot_nt(at, bt), 0.0)
        m_mat = jnp.where(strict, _dot_nt(at, kt), 0.0)
        p_mat = jnp.where(incl, _dot_nt(rt, bt), 0.0)
        pk_mat = jnp.where(incl, _dot_nt(rt, kt), 0.0)
        t_inv, n_pow = eye_c + n_mat, n_mat
        for _ in range(int(math.log2(C)) - 1):
            n_pow = _dot(n_pow, n_pow)
            t_inv = t_inv + _dot(t_inv, n_pow)
        ta = _dot(t_inv, at)
        g0 = _dot(t_inv, _dot(m_mat, vh))
        q_ref[0, :, hs] = rt + _dot(p_mat, ta)
        y0_ref[0, :, hs] = _dot(p_mat, g0) + _dot(pk_mat, vh)
        bgT, kgT = Bg[:, hs].T, Kg[:, hs].T
        a_ref[0, 0, h] = _dot(bgT, ta) + eye_n * decay_end[:, hs]
        d_ref[0, 0, h] = _dot(bgT, g0) + _dot(kgT, vh)


def _rwkv_chunks(rw, mu, w0, w_up, a0, a_up, g_up, k_k, k_a, r_k):
    B, T, cols = rw.shape
    C, W, H, N = RWKV_CHUNK, RWKV_WIDTH, RWKV_HEADS, HEAD_DIM
    nc = T // C
    seg = jnp.asarray(np.kron(np.eye(H), np.ones((N, N))), F32)
    tri = jnp.asarray(np.tril(np.ones((C, C))), F32)
    row = lambda z: z.reshape(1, -1)
    const = lambda b, c: (0, 0)
    vec = pl.BlockSpec((1, W), const)
    tok = pl.BlockSpec((1, C, W), lambda b, c: (b, c, 0))
    mat = pl.BlockSpec((1, 1, H, N, N), lambda b, c: (b, c, 0, 0, 0))
    return pl.pallas_call(
        _rwkv_chunk_kernel,
        grid=(B, nc),
        in_specs=[pl.BlockSpec((1, C, cols), lambda b, c: (b, c, 0)),
                  pl.BlockSpec((1, SUBLANES, cols), lambda b, c: (b, jnp.maximum(c * (C // SUBLANES) - 1, 0), 0)),
                  pl.BlockSpec((1, cols), const), vec,
                  pl.BlockSpec((DECAY_LORA, W), const), vec,
                  pl.BlockSpec((AAA_LORA, W), const),
                  pl.BlockSpec((GATE_LORA, W), const), vec, vec, vec,
                  pl.BlockSpec((W, W), const), pl.BlockSpec((C, C), const)],
        out_specs=[tok, tok, mat, mat, tok, tok],
        out_shape=[jax.ShapeDtypeStruct((B, T, W), F32), jax.ShapeDtypeStruct((B, T, W), F32),
                   jax.ShapeDtypeStruct((B, nc, H, N, N), F32), jax.ShapeDtypeStruct((B, nc, H, N, N), F32),
                   jax.ShapeDtypeStruct((B, T, W), F32), jax.ShapeDtypeStruct((B, T, W), F32)],
        compiler_params=_params("parallel", "parallel"), name="rwkv_chunks",
    )(rw, rw, row(mu), row(w0), w_up.astype(BF16), row(a0), a_up.astype(BF16), g_up.astype(BF16),
      row(k_k), row(k_a), row(r_k), seg, tri)


def _rwkv_scan_kernel(a_ref, d_ref, q_ref, y0_ref, g_ref, bonus_ref, lw_ref, lb_ref, o_ref, h_sc):
    N = HEAD_DIM
    @pl.when(pl.program_id(0) == 0)
    def _():
        h_sc[...] = jnp.zeros(h_sc.shape, F32)

    for b in range(q_ref.shape[0]):
        for h in range(RWKV_HEADS):
            hs = slice(h * N, (h + 1) * N)
            state = h_sc[b, h]
            y = _dot(q_ref[b, :, hs], state, precision=HIGHEST) + y0_ref[b, :, hs]
            h_sc[b, h] = _dot(a_ref[b, 0, h], state, precision=HIGHEST) + d_ref[b, 0, h]
            mean = jnp.mean(y, axis=-1, keepdims=True)
            var = jnp.mean(jnp.square(y - mean), axis=-1, keepdims=True)
            yn = (y - mean) * lax.rsqrt(var + LNX_EPS)
            yn = yn * lw_ref[:, hs] + lb_ref[:, hs]
            o_ref[b, :, hs] = (yn + bonus_ref[b, :, hs]) * g_ref[b, :, hs]


def _rwkv_scan(A, D, Q, Y0, g, bonus, lnx_w, lnx_b):
    B, nc, H, N, _ = A.shape
    T, W, C = Q.shape[1], Q.shape[2], RWKV_CHUNK
    tok = pl.BlockSpec((B, C, W), lambda c: (0, c, 0))
    mat = pl.BlockSpec((B, 1, H, N, N), lambda c: (0, c, 0, 0, 0))
    vec = pl.BlockSpec((1, W), lambda c: (0, 0))
    return pl.pallas_call(
        _rwkv_scan_kernel,
        grid=(nc,),
        in_specs=[mat, mat, tok, tok, tok, tok, vec, vec],
        out_specs=tok,
        out_shape=jax.ShapeDtypeStruct((B, T, W), F32),
        scratch_shapes=[pltpu.VMEM((B, H, N, N), F32)],
        compiler_params=_params("arbitrary"), name="rwkv_scan",
    )(A, D, Q, Y0, g, bonus, lnx_w.reshape(1, W), lnx_b.reshape(1, W))


def _mix_xattn_kernel(x_ref, on_ref, or_ref, wo1_ref, wo2_ref, gx_ref, wq_ref, k_ref, v_ref, wo_ref, o_ref):
    x1 = x_ref[0] + _dot(on_ref[0].astype(BF16), wo1_ref[...]) + _dot(or_ref[0].astype(BF16), wo2_ref[...])
    q = _dot(_rms(x1, gx_ref[...]).astype(BF16), wq_ref[...])
    dh = q.shape[-1] // XATTN_HEADS
    qbf = (q * (dh ** -0.5)).astype(BF16)
    outs = []
    for h in range(XATTN_HEADS):
        hs = slice(h * dh, (h + 1) * dh)
        s = _dot_nt(qbf[:, hs], k_ref[0, :, hs])
        e = jnp.exp(s - jnp.max(s, axis=-1, keepdims=True))
        p = e / jnp.sum(e, axis=-1, keepdims=True)
        outs.append(_dot(p.astype(BF16), v_ref[0, :, hs]))
    o = jnp.concatenate(outs, axis=1).astype(BF16)
    o_ref[0] = x1 + _dot(o, wo_ref[...])


def _mix_xattn(x, o_nsa, o_rwkv, w_out, norm_x_g, w_q, mem_k, mem_v, w_o):
    B, T, D = x.shape
    M = mem_k.shape[1]
    tm = min(ROW_TILE, T)
    wo1, wo2 = w_out[:NSA_WIDTH].astype(BF16), w_out[NSA_WIDTH:].astype(BF16)
    const = lambda b, i: (0, 0)
    tile = lambda w: pl.BlockSpec((1, tm, w), lambda b, i: (b, i, 0))
    return pl.pallas_call(
        _mix_xattn_kernel,
        grid=(B, T // tm),
        in_specs=[tile(D), tile(NSA_WIDTH), tile(RWKV_WIDTH),
                  pl.BlockSpec(wo1.shape, const), pl.BlockSpec(wo2.shape, const),
                  pl.BlockSpec((1, D), const), pl.BlockSpec((D, D), const),
                  pl.BlockSpec((1, M, D), lambda b, i: (b, 0, 0)), pl.BlockSpec((1, M, D), lambda b, i: (b, 0, 0)),
                  pl.BlockSpec((D, D), const)],
        out_specs=tile(D),
        out_shape=jax.ShapeDtypeStruct((B, T, D), F32),
        compiler_params=_params("parallel", "parallel"), name="mix_xattn",
    )(x, o_nsa, o_rwkv, wo1, wo2, norm_x_g.reshape(1, D), w_q.astype(BF16), mem_k, mem_v, w_o.astype(BF16))


def _ffn_kernel(x_ref, g_ref, wg_ref, wu_ref, wd_ref, gf_ref, o_ref, *, final_norm):
    x = x_ref[...]
    h = _rms(x, g_ref[...]).astype(BF16)
    act = (jax.nn.silu(_dot(h, wg_ref[...])) * _dot(h, wu_ref[...])).astype(BF16)
    y = x + _dot(act, wd_ref[...])
    o_ref[...] = _rms(y, gf_ref[...]) if final_norm else y


def _ffn(x, norm_g, w_gate, w_up, w_down, final_g, final_norm):
    R, D = x.shape
    F = w_gate.shape[1]
    tm = min(FFN_ROW_TILE, R)
    const = lambda i: (0, 0)
    resident = lambda shape: pl.BlockSpec(shape, const, pipeline_mode=pl.Buffered(1))
    return pl.pallas_call(
        functools.partial(_ffn_kernel, final_norm=final_norm),
        grid=(R // tm,),
        in_specs=[pl.BlockSpec((tm, D), lambda i: (i, 0)), pl.BlockSpec((1, D), const),
                  resident((D, F)), resident((D, F)), resident((F, D)), pl.BlockSpec((1, D), const)],
        out_specs=pl.BlockSpec((tm, D), lambda i: (i, 0)),
        out_shape=jax.ShapeDtypeStruct((R, D), F32),
        compiler_params=_params("parallel"), name="ffn",
    )(x, norm_g.reshape(1, D), w_gate.astype(BF16), w_up.astype(BF16), w_down.astype(BF16),
      final_g.reshape(1, D))


def _overlap_matrix(n_cmp_pad, n_sel):
    c = np.arange(n_cmp_pad)[:, None] * CMP_STRIDE
    s = np.arange(n_sel)[None, :] * SEL_BLOCK
    return ((c <= s + SEL_BLOCK - 1) & (c + CMP_LEN - 1 >= s)).astype(np.float32)


def _layer(x, mem, rel_bias, final_g, is_last, norm_mix_g, w_in, nsa_gate_b, cmp_pe_k, cmp_pe_v,
           cmp_k_w1, cmp_k_b1, cmp_k_w2, cmp_v_w1, cmp_v_b1, cmp_v_w2,
           rwkv_mu, rwkv_w0, rwkv_w_up, rwkv_a0, rwkv_a_up, rwkv_g_up,
           rwkv_k_k, rwkv_k_a, rwkv_r_k, rwkv_lnx_w, rwkv_lnx_b, w_out,
           norm_x_g, norm_mem_g, w_q_x, w_kv_x, w_o_x, norm_ffn_g, w_gate, w_up, w_down):
    B, T, D = x.shape
    Hkv, G, dh = NSA_KV_HEADS, NSA_GROUP, HEAD_DIM
    n_gate = 3 * NSA_HEADS
    kv0 = NSA_WIDTH
    g0 = kv0 + 6 * KV_WIDTH
    w_q, w_kv, w_rw = w_in[:, :kv0], w_in[:, kv0:g0], w_in[:, NSA_COLS:]
    w_g = jnp.pad(w_in[:, g0:NSA_COLS], ((0, 0), (0, LANES - n_gate)))
    b_g = jnp.pad(nsa_gate_b, (0, LANES - n_gate))
    q, kv, gl, rw = _norm_matmul(
        x.reshape(B * T, D), norm_mix_g, [w.astype(BF16) for w in (w_q, w_kv, w_g, w_rw)],
        [None, None, b_g, None], [F32, F32, F32, F32], ROW_TILE)

    kv = kv.reshape(B, T, 6, Hkv, dh).transpose(2, 0, 3, 1, 4)
    n16 = T // CMP_STRIDE
    x16 = lambda a: a.reshape(B, Hkv, n16, CMP_STRIDE * dh)
    kcT = _compress(x16(kv[0]), cmp_pe_k, cmp_k_w1, cmp_k_b1, cmp_k_w2, True)
    vc = _compress(x16(kv[1]), cmp_pe_v, cmp_v_w1, cmp_v_b1, cmp_v_w2, False)
    nkc = T // KEY_CHUNK
    chunkT = lambda a: a.astype(BF16).reshape(B, Hkv, nkc, KEY_CHUNK, dh).transpose(0, 1, 2, 4, 3)
    gates = gl[:, :n_gate].reshape(B, T // Q_BLOCK, Q_BLOCK, Hkv, G, 3).transpose(0, 3, 1, 4, 2, 5)
    gates = gates.reshape(B, Hkv, T // Q_BLOCK, G * Q_BLOCK, 3)
    bias_tiles, cmp_bias = _bias_tiles(rel_bias, n16)
    o_nsa = _nsa_attention(q.reshape(B, T, NSA_WIDTH), kcT, vc, chunkT(kv[2]), kv[3].astype(BF16),
                           chunkT(kv[4]), kv[5].astype(BF16), gates,
                           jnp.asarray(_overlap_matrix(n16, T // SEL_BLOCK)), bias_tiles, cmp_bias)

    Q, Y0, A, Dm, g, bonus = _rwkv_chunks(rw.reshape(B, T, RWKV_COLS), rwkv_mu, rwkv_w0, rwkv_w_up, rwkv_a0,
                                          rwkv_a_up, rwkv_g_up, rwkv_k_k, rwkv_k_a, rwkv_r_k.reshape(-1))
    o_rwkv = _rwkv_scan(A, Dm, Q, Y0, g, bonus, rwkv_lnx_w, rwkv_lnx_b)

    M = mem.shape[1]
    (kv_mem,) = _norm_matmul(mem.reshape(B * M, D), norm_mem_g, [w_kv_x.astype(BF16)], [None], [BF16], ROW_TILE)
    kv_mem = kv_mem.reshape(B, M, 2 * D)
    x = _mix_xattn(x, o_nsa, o_rwkv, w_out, norm_x_g, w_q_x, kv_mem[..., :D], kv_mem[..., D:], w_o_x)
    x = _ffn(x.reshape(B * T, D), norm_ffn_g, w_gate, w_up, w_down, final_g, is_last)
    return x.reshape(B, T, D)


def kernel(x, mem, rel_bias, norm_f_g, norm_mix_g, w_in, nsa_gate_b, cmp_pe_k, cmp_pe_v, cmp_k_w1, cmp_k_b1, cmp_k_w2, cmp_v_w1, cmp_v_b1, cmp_v_w2, rwkv_mu, rwkv_w0, rwkv_w_up, rwkv_a0, rwkv_a_up, rwkv_g_up, rwkv_k_k, rwkv_k_a, rwkv_r_k, rwkv_lnx_w, rwkv_lnx_b, w_out, norm_x_g, norm_mem_g, w_q_x, w_kv_x, w_o_x, norm_ffn_g, w_gate, w_up, w_down):
    stacked = (norm_mix_g, w_in, nsa_gate_b, cmp_pe_k, cmp_pe_v, cmp_k_w1, cmp_k_b1, cmp_k_w2, cmp_v_w1,
               cmp_v_b1, cmp_v_w2, rwkv_mu, rwkv_w0, rwkv_w_up, rwkv_a0, rwkv_a_up, rwkv_g_up, rwkv_k_k,
               rwkv_k_a, rwkv_r_k, rwkv_lnx_w, rwkv_lnx_b, w_out, norm_x_g, norm_mem_g, w_q_x, w_kv_x, w_o_x,
               norm_ffn_g, w_gate, w_up, w_down)
    depth = w_in.shape[0]
    for l in range(depth):
        x = _layer(x, mem, rel_bias, norm_f_g, l == depth - 1, *[p[l] for p in stacked])
    return x
```

```python
import functools
import math

import numpy as np
import jax
import jax.numpy as jnp
from jax import lax
from jax.experimental import pallas as pl
from jax.experimental.pallas import tpu as pltpu

F32 = jnp.float32
BF16 = jnp.bfloat16
HIGHEST = lax.Precision.HIGHEST

LANES = 128
SUBLANES = 8
VMEM_LIMIT_BYTES = 56 * 1024 * 1024

HEAD_DIM = 64
NSA_HEADS = 8
NSA_KV_HEADS = 2
NSA_GROUP = NSA_HEADS // NSA_KV_HEADS
NSA_WIDTH = NSA_HEADS * HEAD_DIM
KV_WIDTH = NSA_KV_HEADS * HEAD_DIM
RWKV_HEADS = 8
RWKV_WIDTH = RWKV_HEADS * HEAD_DIM
CMP_LEN = 32
CMP_STRIDE = 16
SEL_BLOCK = 64
SEL_SHIFT = 6
SEL_TOP = 16
WINDOW = 512
Q_BLOCK = 128
DECAY_LORA = 64
AAA_LORA = 64
GATE_LORA = 128
N_BUCKETS = 32
MAX_DISTANCE = 2048
XATTN_HEADS = 4
RMS_EPS = 1e-6
LNX_EPS = 64e-5
FORCE_SCORE = 1e4
NEG_SCORE = -1e9
MASK_SCORE = -1e30
RWKV_COLS = 3 * RWKV_WIDTH + DECAY_LORA + AAA_LORA + GATE_LORA
NSA_COLS = NSA_WIDTH + 6 * KV_WIDTH + 3 * NSA_HEADS

KEY_CHUNK = 128
RWKV_CHUNK = 64
ROW_TILE = 512
FFN_ROW_TILE = 256


def _t5_thresholds():
    d = np.arange(0, 2 * MAX_DISTANCE, dtype=np.int64)
    max_exact = N_BUCKETS // 2
    nf = np.maximum(d, 1).astype(np.float32)
    large = max_exact + (np.log(nf / np.float32(max_exact)) / np.float32(math.log(MAX_DISTANCE / max_exact))
                         * np.float32(N_BUCKETS - max_exact)).astype(np.int32)
    bucket = np.where(d < max_exact, d, np.minimum(large, N_BUCKETS - 1))
    return [int(np.argmax(bucket >= k)) for k in range(N_BUCKETS)]


T5_THRESHOLDS = _t5_thresholds()
N_BIAS_TILES = -(-(T5_THRESHOLDS[-1] + KEY_CHUNK) // KEY_CHUNK) + 1
TILE_MASKED = N_BIAS_TILES
TILE_WINDOW_EDGE = N_BIAS_TILES + 1
N_ALL_TILES = N_BIAS_TILES + 2
SEL_STEP_TILES = 8
SEL_STEP_BLOCKS = SEL_STEP_TILES * KEY_CHUNK // SEL_BLOCK


def _params(*semantics):
    return pltpu.CompilerParams(dimension_semantics=semantics, vmem_limit_bytes=VMEM_LIMIT_BYTES)


def _rms(x, g):
    return x * lax.rsqrt(jnp.mean(x * x, axis=-1, keepdims=True) + RMS_EPS) * g


def _dot(a, b, **kw):
    return jnp.dot(a, b, preferred_element_type=F32, **kw)


def _dot_nt(a, b, **kw):
    return lax.dot_general(a, b, (((1,), (1,)), ((), ())), preferred_element_type=F32, **kw)


def _norm_matmul_kernel(x_ref, g_ref, *refs, nseg, bias_flags):
    nb = sum(bias_flags)
    w_refs, b_refs, o_refs = refs[:nseg], refs[nseg:nseg + nb], refs[nseg + nb:]
    xn = _rms(x_ref[...], g_ref[...]).astype(BF16)
    bi = 0
    for s in range(nseg):
        y = _dot(xn, w_refs[s][...])
        if bias_flags[s]:
            y = y + b_refs[bi][...]
            bi += 1
        o_refs[s][...] = y.astype(o_refs[s].dtype)


def _norm_matmul(x, g, weights, biases, out_dtypes, row_tile):
    R, D = x.shape
    tm = min(row_tile, R)
    assert R % tm == 0
    nseg = len(weights)
    bias_flags = tuple(b is not None for b in biases)
    const = lambda i: (0, 0)
    in_specs = [pl.BlockSpec((tm, D), lambda i: (i, 0)), pl.BlockSpec((1, D), const)]
    in_specs += [pl.BlockSpec(w.shape, const) for w in weights]
    in_specs += [pl.BlockSpec((1, b.shape[-1]), const) for b in biases if b is not None]
    out_specs = [pl.BlockSpec((tm, w.shape[1]), lambda i: (i, 0)) for w in weights]
    out_shape = [jax.ShapeDtypeStruct((R, w.shape[1]), dt) for w, dt in zip(weights, out_dtypes)]
    return pl.pallas_call(
        functools.partial(_norm_matmul_kernel, nseg=nseg, bias_flags=bias_flags),
        grid=(R // tm,), in_specs=in_specs, out_specs=out_specs, out_shape=out_shape,
        compiler_params=_params("parallel"), name="norm_matmul",
    )(x, g.reshape(1, D), *weights, *[b.reshape(1, -1) for b in biases if b is not None])


def _compress_kernel(x_ref, pe_ref, w1_ref, b1_ref, w2_ref, o_ref, *, transpose_out):
    x = x_ref[0, 0]
    n16 = x.shape[0]
    lo = _dot((x + pe_ref[0:1, :]).astype(BF16), w1_ref[0])
    hi = _dot((x + pe_ref[1:2, :]).astype(BF16), w1_ref[1])
    h = lo + pltpu.roll(hi, n16 - 1, axis=0) + b1_ref[...]
    h = jax.nn.gelu(h).astype(BF16)
    if transpose_out:
        o_ref[0, 0] = _dot_nt(w2_ref[...], h).astype(o_ref.dtype)
    else:
        o_ref[0, 0] = _dot(h, w2_ref[...]).astype(o_ref.dtype)


def _compress(x16, pe, w1, b1, w2, transpose_out):
    B, H, n16, flat = x16.shape
    hidden = w1.shape[1]
    pe2 = pe.reshape(2, flat)
    w1s = w1.astype(BF16).reshape(2, flat, hidden)
    w2b = (w2.T if transpose_out else w2).astype(BF16)
    oshape = (B, H, HEAD_DIM, n16) if transpose_out else (B, H, n16, HEAD_DIM)
    return pl.pallas_call(
        functools.partial(_compress_kernel, transpose_out=transpose_out),
        grid=(B, H),
        in_specs=[pl.BlockSpec((1, 1, n16, flat), lambda b, h: (b, h, 0, 0)),
                  pl.BlockSpec((2, flat), lambda b, h: (0, 0)),
                  pl.BlockSpec((2, flat, hidden), lambda b, h: (0, 0, 0)),
                  pl.BlockSpec((1, hidden), lambda b, h: (0, 0)),
                  pl.BlockSpec(w2b.shape, lambda b, h: (0, 0))],
        out_specs=pl.BlockSpec((1, 1) + oshape[2:], lambda b, h: (b, h, 0, 0)),
        out_shape=jax.ShapeDtypeStruct(oshape, BF16),
        compiler_params=_params("parallel", "parallel"), name="nsa_compress",
    )(x16, pe2, w1s, b1.reshape(1, hidden), w2b)


def _bias_of_distance(tab_ref, h, d):
    val = jnp.full(d.shape, tab_ref[h, 0], F32)
    for k in range(1, N_BUCKETS):
        val = jnp.where(d >= T5_THRESHOLDS[k], tab_ref[h, k], val)
    return val


def _bias_tiles_kernel(tab_ref, bt_ref, cb_ref, *, n_cmp_pad):
    hkv = pl.program_id(0)
    j = lax.broadcasted_iota(jnp.int32, (KEY_CHUNK, Q_BLOCK), 0)
    i = lax.broadcasted_iota(jnp.int32, (KEY_CHUNK, Q_BLOCK), 1)
    r2 = lax.broadcasted_iota(jnp.int32, (2 * n_cmp_pad, Q_BLOCK), 0)
    i2 = lax.broadcasted_iota(jnp.int32, (2 * n_cmp_pad, Q_BLOCK), 1)
    l2 = r2 - (n_cmp_pad - KEY_CHUNK)
    d2 = i2 - CMP_STRIDE * l2 + (CMP_STRIDE * KEY_CHUNK - Q_BLOCK - (CMP_LEN - 1))
    d2 = jnp.where((l2 >= 0) & (l2 < KEY_CHUNK), d2, 2 * MAX_DISTANCE)
    for g in range(NSA_GROUP):
        h = hkv * NSA_GROUP + g
        lanes = slice(g * Q_BLOCK, (g + 1) * Q_BLOCK)
        for m in range(N_BIAS_TILES):
            tile = _bias_of_distance(tab_ref, h, m * KEY_CHUNK + i - j)
            if m == 0:
                tile = jnp.where(j <= i, tile, MASK_SCORE)
            bt_ref[0, m, :, lanes] = tile
        bt_ref[0, TILE_MASKED, :, lanes] = jnp.full((KEY_CHUNK, Q_BLOCK), MASK_SCORE, F32)
        edge = _bias_of_distance(tab_ref, h, WINDOW + i - j)
        bt_ref[0, TILE_WINDOW_EDGE, :, lanes] = jnp.where(j > i, edge, MASK_SCORE)
        cb_ref[0, :, lanes] = _bias_of_distance(tab_ref, h, d2)


def _bias_tiles(rel_bias, n_cmp_pad):
    assert CMP_STRIDE * KEY_CHUNK - Q_BLOCK - (CMP_LEN - 1) >= T5_THRESHOLDS[-1]
    GQ = NSA_GROUP * Q_BLOCK
    return pl.pallas_call(
        functools.partial(_bias_tiles_kernel, n_cmp_pad=n_cmp_pad),
        grid=(NSA_KV_HEADS,),
        in_specs=[pl.BlockSpec(memory_space=pltpu.SMEM)],
        out_specs=[pl.BlockSpec((1, N_ALL_TILES, KEY_CHUNK, GQ), lambda h: (h, 0, 0, 0)),
                   pl.BlockSpec((1, 2 * n_cmp_pad, GQ), lambda h: (h, 0, 0))],
        out_shape=[jax.ShapeDtypeStruct((NSA_KV_HEADS, N_ALL_TILES, KEY_CHUNK, GQ), F32),
                   jax.ShapeDtypeStruct((NSA_KV_HEADS, 2 * n_cmp_pad, GQ), F32)],
        compiler_params=_params("parallel"), name="t5_bias_tiles",
    )(rel_bias.T)


def _nsa_kernel(q_ref, kc_ref, vcT_ref, ks_ref, vsT_ref, kw_ref, vwT_ref, gate_ref, ovT_ref, bt_ref, cb_ref,
                o_ref, acc_sc, qaug_sc, seladd_sc, *, n_sel, n_cmp_pad):
    G, GQ = NSA_GROUP, NSA_GROUP * Q_BLOCK
    qb = pl.program_id(2)
    q = q_ref[0]
    q4 = jnp.concatenate([q[:, g * HEAD_DIM:(g + 1) * HEAD_DIM] for g in range(G)], axis=0)
    qT = (q4 * (HEAD_DIM ** -0.5)).T.astype(BF16)
    qaug_sc[0:HEAD_DIM, :] = qT
    qaug_sc[HEAD_DIM:, :] = jnp.zeros((qaug_sc.shape[0] - HEAD_DIM, GQ), BF16)
    tile_g = lambda a: jnp.concatenate([a] * G, axis=1)
    t = qb * Q_BLOCK + lax.broadcasted_iota(jnp.int32, (1, Q_BLOCK), 1)

    start = pl.multiple_of(n_cmp_pad - (Q_BLOCK // CMP_STRIDE) * (qb + 1), SUBLANES)
    s = _dot(kc_ref[0, 0], qT) + cb_ref[0, pl.ds(start, n_cmp_pad), :]
    ci = lax.broadcasted_iota(jnp.int32, (n_cmp_pad, Q_BLOCK), 0)
    vis = tile_g(jnp.where((CMP_STRIDE * ci + (CMP_LEN - 1)) <= t, 1.0, 0.0)) > 0.5
    s = jnp.where(vis, s, MASK_SCORE)
    e = jnp.where(vis, jnp.exp(s - jnp.max(s, axis=0, keepdims=True)), 0.0)
    p = e / jnp.maximum(jnp.sum(e, axis=0, keepdims=True), 1e-30)
    o_c = _dot(vcT_ref[0, 0], p.astype(BF16))
    psum = p[:, 0:Q_BLOCK]
    for g in range(1, G):
        psum = psum + p[:, g * Q_BLOCK:(g + 1) * Q_BLOCK]

    imp = _dot(ovT_ref[...], psum, precision=HIGHEST)
    bj = lax.broadcasted_iota(jnp.int32, (n_sel, Q_BLOCK), 0)
    cur = jnp.right_shift(t, SEL_SHIFT)
    forced = (bj == 0) | (bj == cur) | (bj == cur - 1)
    valid = bj * SEL_BLOCK <= t
    score = jnp.where(valid, jnp.where(forced, FORCE_SCORE, imp), NEG_SCORE)
    bjf = bj.astype(F32)
    sel = jnp.zeros((n_sel, Q_BLOCK), F32)
    for _ in range(min(SEL_TOP, n_sel)):
        mx = jnp.max(score, axis=0, keepdims=True)
        first = jnp.min(jnp.where(score == mx, bjf, float(n_sel)), axis=0, keepdims=True)
        pick = bjf == first
        sel = jnp.where(pick, 1.0, sel)
        score = jnp.where(pick, -jnp.inf, score)
    seladd_sc[...] = tile_g((sel - 1.0) * (-MASK_SCORE))

    def bias_rows(first_chunk, n_chunks):
        tiles = []
        for k in range(n_chunks):
            dist = qb - (first_chunk + k)
            tiles.append(bt_ref[0, jnp.where(dist < 0, TILE_MASKED, jnp.minimum(dist, N_BIAS_TILES - 1))])
        return jnp.concatenate(tiles, axis=0)

    half = SEL_STEP_TILES // 2
    half_keys = half * KEY_CHUNK

    def sel_step(step, carry):
        m, l = carry
        blk0 = pl.multiple_of(step * SEL_STEP_BLOCKS, SEL_STEP_BLOCKS)
        qaug_sc[HEAD_DIM:HEAD_DIM + SEL_STEP_BLOCKS, :] = seladd_sc[pl.ds(blk0, SEL_STEP_BLOCKS), :].astype(BF16)
        q_aug = qaug_sc[...]
        for part in range(2):
            k0 = pl.multiple_of((step * 2 + part) * half_keys, half_keys)
            s = _dot(ks_ref[0, 0, pl.ds(k0, half_keys), :], q_aug) + bias_rows((step * 2 + part) * half, half)
            m_new = jnp.maximum(m, jnp.max(s, axis=0, keepdims=True))
            alpha = jnp.exp(m - m_new)
            p = jnp.exp(s - m_new)
            l = alpha * l + jnp.sum(p, axis=0, keepdims=True)
            acc_sc[...] = alpha * acc_sc[...] + _dot(vsT_ref[0, 0, step * 2 + part], p.astype(BF16))
            m = m_new
        return m, l

    acc_sc[...] = jnp.zeros(acc_sc.shape, F32)
    m_init = jnp.full((1, GQ), 0.1 * MASK_SCORE, F32)
    _, l = lax.fori_loop(0, qb // SEL_STEP_TILES + 1, sel_step, (m_init, jnp.zeros((1, GQ), F32)))
    o_s = acc_sc[...] / jnp.maximum(l, 1e-30)

    n_back = WINDOW // KEY_CHUNK
    s_parts, chunks = [], []
    for back in range(n_back, -1, -1):
        kc = jnp.maximum(qb - back, 0)
        edge = TILE_WINDOW_EDGE if back == n_back else back
        tile = jnp.where(qb >= back, edge, TILE_MASKED)
        k_chunk = kw_ref[0, 0, pl.ds(pl.multiple_of(kc * KEY_CHUNK, KEY_CHUNK), KEY_CHUNK), :]
        s_parts.append(_dot(k_chunk, qT) + bt_ref[0, tile])
        chunks.append(kc)
    s = jnp.concatenate(s_parts, axis=0)
    p = jnp.exp(s - jnp.max(s, axis=0, keepdims=True))
    l = jnp.sum(p, axis=0, keepdims=True)
    p = p.astype(BF16)
    o_w = jnp.zeros((HEAD_DIM, GQ), F32)
    for n, kc in enumerate(chunks):
        o_w = o_w + _dot(vwT_ref[0, 0, kc], p[n * KEY_CHUNK:(n + 1) * KEY_CHUNK])
    o_w = o_w / jnp.maximum(l, 1e-30)

    gate = jax.nn.sigmoid(gate_ref[0, 0, 0])
    o = gate[0:1] * o_c + gate[1:2] * o_s + gate[2:3] * o_w
    o_ref[0] = jnp.concatenate([o[:, g * Q_BLOCK:(g + 1) * Q_BLOCK].T for g in range(G)], axis=1)


def _nsa_attention(q, kc, vcT, ks, vsT, kw, vwT, gates, overlapT, bias_tiles, cmp_bias):
    B, T, _ = q.shape
    Hkv, G = NSA_KV_HEADS, NSA_GROUP
    GQ = G * Q_BLOCK
    nqb, nkc = T // Q_BLOCK, T // KEY_CHUNK
    n_cmp_pad = kc.shape[2]
    n_sel = T // SEL_BLOCK
    half_keys = SEL_STEP_TILES // 2 * KEY_CHUNK
    assert T % (2 * half_keys) == 0 and ks.shape[-1] == LANES
    per_head = lambda b, h, i: (b, h, 0, 0)
    chunked = lambda w: pl.BlockSpec((1, 1, T // w, HEAD_DIM, w), lambda b, h, i: (b, h, 0, 0, 0))
    return pl.pallas_call(
        functools.partial(_nsa_kernel, n_sel=n_sel, n_cmp_pad=n_cmp_pad),
        grid=(B, Hkv, nqb),
        in_specs=[pl.BlockSpec((1, Q_BLOCK, G * HEAD_DIM), lambda b, h, i: (b, i, h)),
                  pl.BlockSpec((1, 1, n_cmp_pad, HEAD_DIM), per_head),
                  pl.BlockSpec((1, 1, HEAD_DIM, n_cmp_pad), per_head),
                  pl.BlockSpec((1, 1, T, LANES), per_head), chunked(half_keys),
                  pl.BlockSpec((1, 1, T, HEAD_DIM), per_head), chunked(KEY_CHUNK),
                  pl.BlockSpec((1, 1, 1, 3, GQ), lambda b, h, i: (b, h, i, 0, 0)),
                  pl.BlockSpec(overlapT.shape, lambda b, h, i: (0, 0)),
                  pl.BlockSpec((1, N_ALL_TILES, KEY_CHUNK, GQ), lambda b, h, i: (h, 0, 0, 0)),
                  pl.BlockSpec((1, 2 * n_cmp_pad, GQ), lambda b, h, i: (h, 0, 0))],
        out_specs=pl.BlockSpec((1, Q_BLOCK, G * HEAD_DIM), lambda b, h, i: (b, i, h)),
        out_shape=jax.ShapeDtypeStruct((B, T, NSA_WIDTH), F32),
        scratch_shapes=[pltpu.VMEM((HEAD_DIM, GQ), F32), pltpu.VMEM((LANES, GQ), BF16),
                        pltpu.VMEM((n_sel, GQ), F32)],
        compiler_params=_params("parallel", "parallel", "arbitrary"), name="nsa_attention",
    )(q, kc, vcT, ks, vsT, kw, vwT, gates, overlapT, bias_tiles, cmp_bias)


def _rwkv_chunk_kernel(rw_ref, prev_ref, mu_ref, w0_ref, wup_ref, a0_ref, aup_ref, gup_ref, kk_ref, ka_ref,
                       rk_ref, seg_ref, tri_ref, q_ref, y0_ref, a_ref, d_ref, g_ref, bonus_ref):
    C, W, N = RWKV_CHUNK, RWKV_WIDTH, HEAD_DIM
    c = pl.program_id(1)
    x = rw_ref[0]
    row = lax.broadcasted_iota(jnp.int32, (C, 1), 0)
    last_prev = jnp.where(c == 0, 0.0, prev_ref[0, SUBLANES - 1:SUBLANES, :])
    x_prev = jnp.where(row == 0, last_prev, pltpu.roll(x, 1, axis=0))
    xs = x + (x_prev - x) * mu_ref[...]
    r, k, v = xs[:, 0:W], xs[:, W:2 * W], xs[:, 2 * W:3 * W]
    o = 3 * W
    wd, ad, gd = xs[:, o:o + DECAY_LORA], xs[:, o + DECAY_LORA:o + DECAY_LORA + AAA_LORA], \
        xs[:, o + DECAY_LORA + AAA_LORA:]
    w_log = -jax.nn.softplus(-(w0_ref[...] + _dot(jnp.tanh(wd).astype(BF16), wup_ref[...]))) - 0.5
    lw = -jnp.exp(w_log)
    lr = jax.nn.sigmoid(a0_ref[...] + _dot(ad.astype(BF16), aup_ref[...]))
    g_ref[0] = _dot(jax.nn.sigmoid(gd).astype(BF16), gup_ref[...])
    kk = k * kk_ref[...]
    kk = kk * lax.rsqrt(jnp.maximum(_dot(kk * kk, seg_ref[...], precision=HIGHEST), 1e-24))
    k = k * (1.0 + (lr - 1.0) * ka_ref[...])
    bonus_ref[0] = _dot(r * k * rk_ref[...], seg_ref[...], precision=HIGHEST) * v
    a_vec, b_vec = -kk, kk * lr

    L = _dot(tri_ref[...], lw, precision=HIGHEST)
    L_end = L[C - 1:C, :]
    e_neg = jnp.exp(-L)
    e_rem = jnp.exp(L_end - L)
    At, Bt, Kt, Rt = a_vec * jnp.exp(L - lw), b_vec * e_neg, k * e_neg, r * jnp.exp(L)
    Bg, Kg = b_vec * e_rem, k * e_rem
    decay_end = jnp.exp(L_end)

    ri = lax.broadcasted_iota(jnp.int32, (C, C), 0)
    cj = lax.broadcasted_iota(jnp.int32, (C, C), 1)
    strict, incl = ri > cj, ri >= cj
    eye_c = (ri == cj).astype(F32)
    eye_n = (lax.broadcasted_iota(jnp.int32, (N, N), 0) == lax.broadcasted_iota(jnp.int32, (N, N), 1)).astype(F32)
    for h in range(RWKV_HEADS):
        hs = slice(h * N, (h + 1) * N)
        at, bt, kt, rt, vh = At[:, hs], Bt[:, hs], Kt[:, hs], Rt[:, hs], v[:, hs]
        n_mat = jnp.where(strict, _dot_nt(at, bt), 0.0)
        m_mat = jnp.where(strict, _dot_nt(at, kt), 0.0)
        p_mat = jnp.where(incl, _dot_nt(rt, bt), 0.0)
        pk_mat = jnp.where(incl, _dot_nt(rt, kt), 0.0)
        t_inv, n_pow = eye_c + n_mat, n_mat
        for _ in range(int(math.log2(C)) - 1):
            n_pow = _dot(n_pow, n_pow)
            t_inv = t_inv + _dot(t_inv, n_pow)
        ta = _dot(t_inv, at)
        g0 = _dot(t_inv, _dot(m_mat, vh))
        q_ref[0, :, hs] = rt + _dot(p_mat, ta)
        y0_ref[0, :, hs] = _dot(p_mat, g0) + _dot(pk_mat, vh)
        bgT, kgT = Bg[:, hs].T, Kg[:, hs].T
        a_ref[0, 0, h] = _dot(bgT, ta) + eye_n * decay_end[:, hs]
        d_ref[0, 0, h] = _dot(bgT, g0) + _dot(kgT, vh)


def _rwkv_chunks(rw, mu, w0, w_up, a0, a_up, g_up, k_k, k_a, r_k):
    B, T, cols = rw.shape
    C, W, H, N = RWKV_CHUNK, RWKV_WIDTH, RWKV_HEADS, HEAD_DIM
    nc = T // C
    seg = jnp.asarray(np.kron(np.eye(H), np.ones((N, N))), F32)
    tri = jnp.asarray(np.tril(np.ones((C, C))), F32)
    row = lambda z: z.reshape(1, -1)
    const = lambda b, c: (0, 0)
    vec = pl.BlockSpec((1, W), const)
    tok = pl.BlockSpec((1, C, W), lambda b, c: (b, c, 0))
    mat = pl.BlockSpec((1, 1, H, N, N), lambda b, c: (b, c, 0, 0, 0))
    return pl.pallas_call(
        _rwkv_chunk_kernel,
        grid=(B, nc),
        in_specs=[pl.BlockSpec((1, C, cols), lambda b, c: (b, c, 0)),
                  pl.BlockSpec((1, SUBLANES, cols), lambda b, c: (b, jnp.maximum(c * (C // SUBLANES) - 1, 0), 0)),
                  pl.BlockSpec((1, cols), const), vec,
                  pl.BlockSpec((DECAY_LORA, W), const), vec,
                  pl.BlockSpec((AAA_LORA, W), const),
                  pl.BlockSpec((GATE_LORA, W), const), vec, vec, vec,
                  pl.BlockSpec((W, W), const), pl.BlockSpec((C, C), const)],
        out_specs=[tok, tok, mat, mat, tok, tok],
        out_shape=[jax.ShapeDtypeStruct((B, T, W), F32), jax.ShapeDtypeStruct((B, T, W), F32),
                   jax.ShapeDtypeStruct((B, nc, H, N, N), F32), jax.ShapeDtypeStruct((B, nc, H, N, N), F32),
                   jax.ShapeDtypeStruct((B, T, W), F32), jax.ShapeDtypeStruct((B, T, W), F32)],
        compiler_params=_params("parallel", "parallel"), name="rwkv_chunks",
    )(rw, rw, row(mu), row(w0), w_up.astype(BF16), row(a0), a_up.astype(BF16), g_up.astype(BF16),
      row(k_k), row(k_a), row(r_k), seg, tri)


def _rwkv_scan_kernel(a_ref, d_ref, q_ref, y0_ref, g_ref, bonus_ref, lw_ref, lb_ref, o_ref, h_sc):
    N = HEAD_DIM
    @pl.when(pl.program_id(0) == 0)
    def _():
        h_sc[...] = jnp.zeros(h_sc.shape, F32)

    for b in range(q_ref.shape[0]):
        for h in range(RWKV_HEADS):
            hs = slice(h * N, (h + 1) * N)
            state = h_sc[b, h]
            y = _dot(q_ref[b, :, hs], state, precision=HIGHEST) + y0_ref[b, :, hs]
            h_sc[b, h] = _dot(a_ref[b, 0, h], state, precision=HIGHEST) + d_ref[b, 0, h]
            mean = jnp.mean(y, axis=-1, keepdims=True)
            var = jnp.mean(jnp.square(y - mean), axis=-1, keepdims=True)
            yn = (y - mean) * lax.rsqrt(var + LNX_EPS)
            yn = yn * lw_ref[:, hs] + lb_ref[:, hs]
            o_ref[b, :, hs] = (yn + bonus_ref[b, :, hs]) * g_ref[b, :, hs]


def _rwkv_scan(A, D, Q, Y0, g, bonus, lnx_w, lnx_b):
    B, nc, H, N, _ = A.shape
    T, W, C = Q.shape[1], Q.shape[2], RWKV_CHUNK
    tok = pl.BlockSpec((B, C, W), lambda c: (0, c, 0))
    mat = pl.BlockSpec((B, 1, H, N, N), lambda c: (0, c, 0, 0, 0))
    vec = pl.BlockSpec((1, W), lambda c: (0, 0))
    return pl.pallas_call(
        _rwkv_scan_kernel,
        grid=(nc,),
        in_specs=[mat, mat, tok, tok, tok, tok, vec, vec],
        out_specs=tok,
        out_shape=jax.ShapeDtypeStruct((B, T, W), F32),
        scratch_shapes=[pltpu.VMEM((B, H, N, N), F32)],
        compiler_params=_params("arbitrary"), name="rwkv_scan",
    )(A, D, Q, Y0, g, bonus, lnx_w.reshape(1, W), lnx_b.reshape(1, W))


def _mix_xattn_kernel(x_ref, on_ref, or_ref, wo1_ref, wo2_ref, gx_ref, wq_ref, k_ref, v_ref, wo_ref, o_ref):
    x1 = x_ref[0] + _dot(on_ref[0].astype(BF16), wo1_ref[...]) + _dot(or_ref[0].astype(BF16), wo2_ref[...])
    q = _dot(_rms(x1, gx_ref[...]).astype(BF16), wq_ref[...])
    dh = q.shape[-1] // XATTN_HEADS
    qbf = (q * (dh ** -0.5)).astype(BF16)
    outs = []
    for h in range(XATTN_HEADS):
        hs = slice(h * dh, (h + 1) * dh)
        s = _dot_nt(qbf[:, hs], k_ref[0, :, hs])
        e = jnp.exp(s - jnp.max(s, axis=-1, keepdims=True))
        p = e / jnp.sum(e, axis=-1, keepdims=True)
        outs.append(_dot(p.astype(BF16), v_ref[0, :, hs]))
    o = jnp.concatenate(outs, axis=1).astype(BF16)
    o_ref[0] = x1 + _dot(o, wo_ref[...])


def _mix_xattn(x, o_nsa, o_rwkv, w_out, norm_x_g, w_q, mem_k, mem_v, w_o):
    B, T, D = x.shape
    M = mem_k.shape[1]
    tm = min(ROW_TILE, T)
    wo1, wo2 = w_out[:NSA_WIDTH].astype(BF16), w_out[NSA_WIDTH:].astype(BF16)
    const = lambda b, i: (0, 0)
    tile = lambda w: pl.BlockSpec((1, tm, w), lambda b, i: (b, i, 0))
    return pl.pallas_call(
        _mix_xattn_kernel,
        grid=(B, T // tm),
        in_specs=[tile(D), tile(NSA_WIDTH), tile(RWKV_WIDTH),
                  pl.BlockSpec(wo1.shape, const), pl.BlockSpec(wo2.shape, const),
                  pl.BlockSpec((1, D), const), pl.BlockSpec((D, D), const),
                  pl.BlockSpec((1, M, D), lambda b, i: (b, 0, 0)), pl.BlockSpec((1, M, D), lambda b, i: (b, 0, 0)),
                  pl.BlockSpec((D, D), const)],
        out_specs=tile(D),
        out_shape=jax.ShapeDtypeStruct((B, T, D), F32),
        compiler_params=_params("parallel", "parallel"), name="mix_xattn",
    )(x, o_nsa, o_rwkv, wo1, wo2, norm_x_g.reshape(1, D), w_q.astype(BF16), mem_k, mem_v, w_o.astype(BF16))


def _ffn_kernel(x_ref, g_ref, wg_ref, wu_ref, wd_ref, gf_ref, o_ref, *, final_norm):
    x = x_ref[...]
    h = _rms(x, g_ref[...]).astype(BF16)
    act = (jax.nn.silu(_dot(h, wg_ref[...])) * _dot(h, wu_ref[...])).astype(BF16)
    y = x + _dot(act, wd_ref[...])
    o_ref[...] = _rms(y, gf_ref[...]) if final_norm else y


def _ffn(x, norm_g, w_gate, w_up, w_down, final_g, final_norm):
    R, D = x.shape
    F = w_gate.shape[1]
    tm = min(FFN_ROW_TILE, R)
    const = lambda i: (0, 0)
    resident = lambda shape: pl.BlockSpec(shape, const, pipeline_mode=pl.Buffered(1))
    return pl.pallas_call(
        functools.partial(_ffn_kernel, final_norm=final_norm),
        grid=(R // tm,),
        in_specs=[pl.BlockSpec((tm, D), lambda i: (i, 0)), pl.BlockSpec((1, D), const),
                  resident((D, F)), resident((D, F)), resident((F, D)), pl.BlockSpec((1, D), const)],
        out_specs=pl.BlockSpec((tm, D), lambda i: (i, 0)),
        out_shape=jax.ShapeDtypeStruct((R, D), F32),
        compiler_params=_params("parallel"), name="ffn",
    )(x, norm_g.reshape(1, D), w_gate.astype(BF16), w_up.astype(BF16), w_down.astype(BF16),
      final_g.reshape(1, D))


def _overlap_matrix(n_cmp_pad, n_sel):
    c = np.arange(n_cmp_pad)[:, None] * CMP_STRIDE
    s = np.arange(n_sel)[None, :] * SEL_BLOCK
    return ((c <= s + SEL_BLOCK - 1) & (c + CMP_LEN - 1 >= s)).astype(np.float32)


def _layer(x, mem, rel_bias, final_g, is_last, norm_mix_g, w_in, nsa_gate_b, cmp_pe_k, cmp_pe_v,
           cmp_k_w1, cmp_k_b1, cmp_k_w2, cmp_v_w1, cmp_v_b1, cmp_v_w2,
           rwkv_mu, rwkv_w0, rwkv_w_up, rwkv_a0, rwkv_a_up, rwkv_g_up,
           rwkv_k_k, rwkv_k_a, rwkv_r_k, rwkv_lnx_w, rwkv_lnx_b, w_out,
           norm_x_g, norm_mem_g, w_q_x, w_kv_x, w_o_x, norm_ffn_g, w_gate, w_up, w_down):
    B, T, D = x.shape
    Hkv, G, dh = NSA_KV_HEADS, NSA_GROUP, HEAD_DIM
    n_gate = 3 * NSA_HEADS
    kv0 = NSA_WIDTH
    g0 = kv0 + 6 * KV_WIDTH
    w_q, w_kv, w_rw = w_in[:, :kv0], w_in[:, kv0:g0], w_in[:, NSA_COLS:]
    w_g = jnp.pad(w_in[:, g0:NSA_COLS], ((0, 0), (0, LANES - n_gate)))
    b_g = jnp.pad(nsa_gate_b, (0, LANES - n_gate))
    q, kv, gl, rw = _norm_matmul(
        x.reshape(B * T, D), norm_mix_g, [w.astype(BF16) for w in (w_q, w_kv, w_g, w_rw)],
        [None, None, b_g, None], [F32, F32, F32, F32], ROW_TILE)

    kv = kv.reshape(B, T, 6, Hkv, dh).transpose(2, 0, 3, 1, 4)
    n16 = T // CMP_STRIDE
    x16 = lambda a: a.reshape(B, Hkv, n16, CMP_STRIDE * dh)
    kc = _compress(x16(kv[0]), cmp_pe_k, cmp_k_w1, cmp_k_b1, cmp_k_w2, False)
    vcT = _compress(x16(kv[1]), cmp_pe_v, cmp_v_w1, cmp_v_b1, cmp_v_w2, True)
    chunkT = lambda a, w: a.astype(BF16).reshape(B, Hkv, T // w, w, dh).transpose(0, 1, 2, 4, 3)
    gates = gl[:, :n_gate].reshape(B, T // Q_BLOCK, Q_BLOCK, Hkv, G, 3).transpose(0, 3, 1, 5, 4, 2)
    gates = gates.reshape(B, Hkv, T // Q_BLOCK, 3, G * Q_BLOCK)
    bias_tiles, cmp_bias = _bias_tiles(rel_bias, n16)
    blk_in_step = (np.arange(T) // SEL_BLOCK) % SEL_STEP_BLOCKS
    onehot = jnp.asarray(blk_in_step[:, None] == np.arange(LANES - dh)[None, :], BF16)
    ks_aug = jnp.concatenate([kv[2].astype(BF16), jnp.broadcast_to(onehot, (B, Hkv, T, LANES - dh))], axis=-1)
    o_nsa = _nsa_attention(q.reshape(B, T, NSA_WIDTH), kc, vcT, ks_aug,
                           chunkT(kv[3], SEL_STEP_TILES // 2 * KEY_CHUNK),
                           kv[4].astype(BF16), chunkT(kv[5], KEY_CHUNK), gates,
                           jnp.asarray(_overlap_matrix(n16, T // SEL_BLOCK).T), bias_tiles, cmp_bias)

    Q, Y0, A, Dm, g, bonus = _rwkv_chunks(rw.reshape(B, T, RWKV_COLS), rwkv_mu, rwkv_w0, rwkv_w_up, rwkv_a0,
                                          rwkv_a_up, rwkv_g_up, rwkv_k_k, rwkv_k_a, rwkv_r_k.reshape(-1))
    o_rwkv = _rwkv_scan(A, Dm, Q, Y0, g, bonus, rwkv_lnx_w, rwkv_lnx_b)

    M = mem.shape[1]
    (kv_mem,) = _norm_matmul(mem.reshape(B * M, D), norm_mem_g, [w_kv_x.astype(BF16)], [None], [BF16], ROW_TILE)
    kv_mem = kv_mem.reshape(B, M, 2 * D)
    x = _mix_xattn(x, o_nsa, o_rwkv, w_out, norm_x_g, w_q_x, kv_mem[..., :D], kv_mem[..., D:], w_o_x)
    x = _ffn(x.reshape(B * T, D), norm_ffn_g, w_gate, w_up, w_down, final_g, is_last)
    return x.reshape(B, T, D)


def kernel(x, mem, rel_bias, norm_f_g, norm_mix_g, w_in, nsa_gate_b, cmp_pe_k, cmp_pe_v, cmp_k_w1, cmp_k_b1, cmp_k_w2, cmp_v_w1, cmp_v_b1, cmp_v_w2, rwkv_mu, rwkv_w0, rwkv_w_up, rwkv_a0, rwkv_a_up, rwkv_g_up, rwkv_k_k, rwkv_k_a, rwkv_r_k, rwkv_lnx_w, rwkv_lnx_b, w_out, norm_x_g, norm_mem_g, w_q_x, w_kv_x, w_o_x, norm_ffn_g, w_gate, w_up, w_down):
    stacked = (norm_mix_g, w_in, nsa_gate_b, cmp_pe_k, cmp_pe_v, cmp_k_w1, cmp_k_b1, cmp_k_w2, cmp_v_w1,
               cmp_v_b1, cmp_v_w2, rwkv_mu, rwkv_w0, rwkv_w_up, rwkv_a0, rwkv_a_up, rwkv_g_up, rwkv_k_k,
               rwkv_k_a, rwkv_r_k, rwkv_lnx_w, rwkv_lnx_b, w_out, norm_x_g, norm_mem_g, w_q_x, w_kv_x, w_o_x,
               norm_ffn_g, w_gate, w_up, w_down)
    depth = w_in.shape[0]
    for l in range(depth):
        x = _layer(x, mem, rel_bias, norm_f_g, l == depth - 1, *[p[l] for p in stacked])
    return x
```

```python
import functools
import math

import numpy as np
import jax
import jax.numpy as jnp
from jax import lax
from jax.experimental import pallas as pl
from jax.experimental.pallas import tpu as pltpu

F32 = jnp.float32
BF16 = jnp.bfloat16
HIGHEST = lax.Precision.HIGHEST

LANES = 128
SUBLANES = 8
VMEM_LIMIT_BYTES = 56 * 1024 * 1024

HEAD_DIM = 64
NSA_HEADS = 8
NSA_KV_HEADS = 2
NSA_GROUP = NSA_HEADS // NSA_KV_HEADS
NSA_WIDTH = NSA_HEADS * HEAD_DIM
KV_WIDTH = NSA_KV_HEADS * HEAD_DIM
RWKV_HEADS = 8
RWKV_WIDTH = RWKV_HEADS * HEAD_DIM
CMP_LEN = 32
CMP_STRIDE = 16
SEL_BLOCK = 64
SEL_SHIFT = 6
SEL_TOP = 16
WINDOW = 512
Q_BLOCK = 128
DECAY_LORA = 64
AAA_LORA = 64
GATE_LORA = 128
N_BUCKETS = 32
MAX_DISTANCE = 2048
XATTN_HEADS = 4
RMS_EPS = 1e-6
LNX_EPS = 64e-5
FORCE_SCORE = 1e4
NEG_SCORE = -1e9
MASK_SCORE = -1e30
RWKV_COLS = 3 * RWKV_WIDTH + DECAY_LORA + AAA_LORA + GATE_LORA
NSA_COLS = NSA_WIDTH + 6 * KV_WIDTH + 3 * NSA_HEADS

KEY_CHUNK = 128
RWKV_CHUNK = 64
ROW_TILE = 512
FFN_ROW_TILE = 256


def _t5_thresholds():
    d = np.arange(0, 2 * MAX_DISTANCE, dtype=np.int64)
    max_exact = N_BUCKETS // 2
    nf = np.maximum(d, 1).astype(np.float32)
    large = max_exact + (np.log(nf / np.float32(max_exact)) / np.float32(math.log(MAX_DISTANCE / max_exact))
                         * np.float32(N_BUCKETS - max_exact)).astype(np.int32)
    bucket = np.where(d < max_exact, d, np.minimum(large, N_BUCKETS - 1))
    return [int(np.argmax(bucket >= k)) for k in range(N_BUCKETS)]


T5_THRESHOLDS = _t5_thresholds()
N_BIAS_TILES = -(-(T5_THRESHOLDS[-1] + KEY_CHUNK) // KEY_CHUNK) + 1
TILE_MASKED = N_BIAS_TILES
TILE_WINDOW_EDGE = N_BIAS_TILES + 1
N_ALL_TILES = N_BIAS_TILES + 2
SEL_STEP_TILES = 8
SEL_STEP_BLOCKS = SEL_STEP_TILES * KEY_CHUNK // SEL_BLOCK


def _params(*semantics):
    return pltpu.CompilerParams(dimension_semantics=semantics, vmem_limit_bytes=VMEM_LIMIT_BYTES)


def _rms(x, g):
    return x * lax.rsqrt(jnp.mean(x * x, axis=-1, keepdims=True) + RMS_EPS) * g


def _dot(a, b, **kw):
    return jnp.dot(a, b, preferred_element_type=F32, **kw)


def _split3(x):
    hi = x.astype(BF16)
    r1 = x - hi.astype(F32)
    mid = r1.astype(BF16)
    lo = (r1 - mid.astype(F32)).astype(BF16)
    return hi, mid, lo


def _dot_f32_lhs(x, w01):
    w = w01.astype(BF16)
    hi, mid, lo = _split3(x)
    return _dot(hi, w) + (_dot(mid, w) + _dot(lo, w))


def _dot_f32_rhs(w01, x):
    w = w01.astype(BF16)
    hi, mid, lo = _split3(x)
    return _dot(w, hi) + (_dot(w, mid) + _dot(w, lo))


def _dot_nt(a, b, **kw):
    return lax.dot_general(a, b, (((1,), (1,)), ((), ())), preferred_element_type=F32, **kw)


def _norm_matmul_kernel(x_ref, g_ref, *refs, nseg, bias_flags):
    nb = sum(bias_flags)
    w_refs, b_refs, o_refs = refs[:nseg], refs[nseg:nseg + nb], refs[nseg + nb:]
    xn = _rms(x_ref[...], g_ref[...]).astype(BF16)
    bi = 0
    for s in range(nseg):
        y = _dot(xn, w_refs[s][...])
        if bias_flags[s]:
            y = y + b_refs[bi][...]
            bi += 1
        o_refs[s][...] = y.astype(o_refs[s].dtype)


def _norm_matmul(x, g, weights, biases, out_dtypes, row_tile):
    R, D = x.shape
    tm = min(row_tile, R)
    assert R % tm == 0
    nseg = len(weights)
    bias_flags = tuple(b is not None for b in biases)
    const = lambda i: (0, 0)
    in_specs = [pl.BlockSpec((tm, D), lambda i: (i, 0)), pl.BlockSpec((1, D), const)]
    in_specs += [pl.BlockSpec(w.shape, const) for w in weights]
    in_specs += [pl.BlockSpec((1, b.shape[-1]), const) for b in biases if b is not None]
    out_specs = [pl.BlockSpec((tm, w.shape[1]), lambda i: (i, 0)) for w in weights]
    out_shape = [jax.ShapeDtypeStruct((R, w.shape[1]), dt) for w, dt in zip(weights, out_dtypes)]
    return pl.pallas_call(
        functools.partial(_norm_matmul_kernel, nseg=nseg, bias_flags=bias_flags),
        grid=(R // tm,), in_specs=in_specs, out_specs=out_specs, out_shape=out_shape,
        compiler_params=_params("parallel"), name="norm_matmul",
    )(x, g.reshape(1, D), *weights, *[b.reshape(1, -1) for b in biases if b is not None])


def _compress_kernel(x_ref, pe_ref, w1_ref, b1_ref, w2_ref, o_ref, *, transpose_out):
    x = x_ref[0, 0]
    n16 = x.shape[0]
    lo = _dot((x + pe_ref[0:1, :]).astype(BF16), w1_ref[0])
    hi = _dot((x + pe_ref[1:2, :]).astype(BF16), w1_ref[1])
    h = lo + pltpu.roll(hi, n16 - 1, axis=0) + b1_ref[...]
    h = jax.nn.gelu(h).astype(BF16)
    if transpose_out:
        o_ref[0, 0] = _dot_nt(w2_ref[...], h).astype(o_ref.dtype)
    else:
        o_ref[0, 0] = _dot(h, w2_ref[...]).astype(o_ref.dtype)


def _compress(x16, pe, w1, b1, w2, transpose_out):
    B, H, n16, flat = x16.shape
    hidden = w1.shape[1]
    pe2 = pe.reshape(2, flat)
    w1s = w1.astype(BF16).reshape(2, flat, hidden)
    w2b = (w2.T if transpose_out else w2).astype(BF16)
    oshape = (B, H, HEAD_DIM, n16) if transpose_out else (B, H, n16, HEAD_DIM)
    return pl.pallas_call(
        functools.partial(_compress_kernel, transpose_out=transpose_out),
        grid=(B, H),
        in_specs=[pl.BlockSpec((1, 1, n16, flat), lambda b, h: (b, h, 0, 0)),
                  pl.BlockSpec((2, flat), lambda b, h: (0, 0)),
                  pl.BlockSpec((2, flat, hidden), lambda b, h: (0, 0, 0)),
                  pl.BlockSpec((1, hidden), lambda b, h: (0, 0)),
                  pl.BlockSpec(w2b.shape, lambda b, h: (0, 0))],
        out_specs=pl.BlockSpec((1, 1) + oshape[2:], lambda b, h: (b, h, 0, 0)),
        out_shape=jax.ShapeDtypeStruct(oshape, BF16),
        compiler_params=_params("parallel", "parallel"), name="nsa_compress",
    )(x16, pe2, w1s, b1.reshape(1, hidden), w2b)


def _bias_of_distance(tab_ref, h, d):
    val = jnp.full(d.shape, tab_ref[h, 0], F32)
    for k in range(1, N_BUCKETS):
        val = jnp.where(d >= T5_THRESHOLDS[k], tab_ref[h, k], val)
    return val


def _bias_tiles_kernel(tab_ref, bt_ref, cb_ref, *, n_cmp_pad):
    hkv = pl.program_id(0)
    j = lax.broadcasted_iota(jnp.int32, (KEY_CHUNK, Q_BLOCK), 0)
    i = lax.broadcasted_iota(jnp.int32, (KEY_CHUNK, Q_BLOCK), 1)
    r2 = lax.broadcasted_iota(jnp.int32, (2 * n_cmp_pad, Q_BLOCK), 0)
    i2 = lax.broadcasted_iota(jnp.int32, (2 * n_cmp_pad, Q_BLOCK), 1)
    l2 = r2 - (n_cmp_pad - KEY_CHUNK)
    d2 = i2 - CMP_STRIDE * l2 + (CMP_STRIDE * KEY_CHUNK - Q_BLOCK - (CMP_LEN - 1))
    d2 = jnp.where((l2 >= 0) & (l2 < KEY_CHUNK), d2, 2 * MAX_DISTANCE)
    for g in range(NSA_GROUP):
        h = hkv * NSA_GROUP + g
        lanes = slice(g * Q_BLOCK, (g + 1) * Q_BLOCK)
        for m in range(N_BIAS_TILES):
            tile = _bias_of_distance(tab_ref, h, m * KEY_CHUNK + i - j)
            if m == 0:
                tile = jnp.where(j <= i, tile, MASK_SCORE)
            bt_ref[0, m, :, lanes] = tile
        bt_ref[0, TILE_MASKED, :, lanes] = jnp.full((KEY_CHUNK, Q_BLOCK), MASK_SCORE, F32)
        edge = _bias_of_distance(tab_ref, h, WINDOW + i - j)
        bt_ref[0, TILE_WINDOW_EDGE, :, lanes] = jnp.where(j > i, edge, MASK_SCORE)
        cb_ref[0, :, lanes] = _bias_of_distance(tab_ref, h, d2)


def _bias_tiles(rel_bias, n_cmp_pad):
    assert CMP_STRIDE * KEY_CHUNK - Q_BLOCK - (CMP_LEN - 1) >= T5_THRESHOLDS[-1]
    GQ = NSA_GROUP * Q_BLOCK
    return pl.pallas_call(
        functools.partial(_bias_tiles_kernel, n_cmp_pad=n_cmp_pad),
        grid=(NSA_KV_HEADS,),
        in_specs=[pl.BlockSpec(memory_space=pltpu.SMEM)],
        out_specs=[pl.BlockSpec((1, N_ALL_TILES, KEY_CHUNK, GQ), lambda h: (h, 0, 0, 0)),
                   pl.BlockSpec((1, 2 * n_cmp_pad, GQ), lambda h: (h, 0, 0))],
        out_shape=[jax.ShapeDtypeStruct((NSA_KV_HEADS, N_ALL_TILES, KEY_CHUNK, GQ), F32),
                   jax.ShapeDtypeStruct((NSA_KV_HEADS, 2 * n_cmp_pad, GQ), F32)],
        compiler_params=_params("parallel"), name="t5_bias_tiles",
    )(rel_bias.T)


def _nsa_kernel(q_ref, kc_ref, vcT_ref, ks_ref, vsT_ref, kw_ref, vwT_ref, gate_ref, ovT_ref, bt_ref, cb_ref,
                o_ref, acc_sc, qaug_sc, seladd_sc, *, n_sel, n_cmp_pad):
    G, GQ = NSA_GROUP, NSA_GROUP * Q_BLOCK
    qb = pl.program_id(2)
    q = q_ref[0]
    q4 = jnp.concatenate([q[:, g * HEAD_DIM:(g + 1) * HEAD_DIM] for g in range(G)], axis=0)
    qT = (q4 * (HEAD_DIM ** -0.5)).T.astype(BF16)
    qaug_sc[0:HEAD_DIM, :] = qT
    qaug_sc[HEAD_DIM:, :] = jnp.zeros((qaug_sc.shape[0] - HEAD_DIM, GQ), BF16)
    tile_g = lambda a: jnp.concatenate([a] * G, axis=1)
    t = qb * Q_BLOCK + lax.broadcasted_iota(jnp.int32, (1, Q_BLOCK), 1)

    start = pl.multiple_of(n_cmp_pad - (Q_BLOCK // CMP_STRIDE) * (qb + 1), SUBLANES)
    s = _dot(kc_ref[0, 0], qT) + cb_ref[0, pl.ds(start, n_cmp_pad), :]
    ci = lax.broadcasted_iota(jnp.int32, (n_cmp_pad, Q_BLOCK), 0)
    vis = tile_g(jnp.where((CMP_STRIDE * ci + (CMP_LEN - 1)) <= t, 1.0, 0.0)) > 0.5
    s = jnp.where(vis, s, MASK_SCORE)
    e = jnp.where(vis, jnp.exp(s - jnp.max(s, axis=0, keepdims=True)), 0.0)
    p = e / jnp.maximum(jnp.sum(e, axis=0, keepdims=True), 1e-30)
    o_c = _dot(vcT_ref[0, 0], p.astype(BF16))
    psum = p[:, 0:Q_BLOCK]
    for g in range(1, G):
        psum = psum + p[:, g * Q_BLOCK:(g + 1) * Q_BLOCK]

    imp = _dot_f32_rhs(ovT_ref[...], psum)
    bj = lax.broadcasted_iota(jnp.int32, (n_sel, Q_BLOCK), 0)
    cur = jnp.right_shift(t, SEL_SHIFT)
    forced = (bj == 0) | (bj == cur) | (bj == cur - 1)
    valid = bj * SEL_BLOCK <= t
    score = jnp.where(valid, jnp.where(forced, FORCE_SCORE, imp), NEG_SCORE)
    bjf = bj.astype(F32)
    sel = jnp.zeros((n_sel, Q_BLOCK), F32)
    for _ in range(min(SEL_TOP, n_sel)):
        mx = jnp.max(score, axis=0, keepdims=True)
        first = jnp.min(jnp.where(score == mx, bjf, float(n_sel)), axis=0, keepdims=True)
        pick = bjf == first
        sel = jnp.where(pick, 1.0, sel)
        score = jnp.where(pick, -jnp.inf, score)
    seladd_sc[...] = tile_g((sel - 1.0) * (-MASK_SCORE))

    def bias_rows(first_chunk, n_chunks):
        tiles = []
        for k in range(n_chunks):
            dist = qb - (first_chunk + k)
            tiles.append(bt_ref[0, jnp.where(dist < 0, TILE_MASKED, jnp.minimum(dist, N_BIAS_TILES - 1))])
        return jnp.concatenate(tiles, axis=0)

    half = SEL_STEP_TILES // 2
    half_keys = half * KEY_CHUNK

    def sel_step(step, carry):
        m, l = carry
        blk0 = pl.multiple_of(step * SEL_STEP_BLOCKS, SEL_STEP_BLOCKS)
        qaug_sc[HEAD_DIM:HEAD_DIM + SEL_STEP_BLOCKS, :] = seladd_sc[pl.ds(blk0, SEL_STEP_BLOCKS), :].astype(BF16)
        q_aug = qaug_sc[...]
        for part in range(2):
            k0 = pl.multiple_of((step * 2 + part) * half_keys, half_keys)
            s = _dot(ks_ref[0, 0, pl.ds(k0, half_keys), :], q_aug) + bias_rows((step * 2 + part) * half, half)
            m_new = jnp.maximum(m, jnp.max(s, axis=0, keepdims=True))
            alpha = jnp.exp(m - m_new)
            p = jnp.exp(s - m_new)
            l = alpha * l + jnp.sum(p, axis=0, keepdims=True)
            acc_sc[...] = alpha * acc_sc[...] + _dot(vsT_ref[0, 0, step * 2 + part], p.astype(BF16))
            m = m_new
        return m, l

    acc_sc[...] = jnp.zeros(acc_sc.shape, F32)
    m_init = jnp.full((1, GQ), 0.1 * MASK_SCORE, F32)
    _, l = lax.fori_loop(0, qb // SEL_STEP_TILES + 1, sel_step, (m_init, jnp.zeros((1, GQ), F32)))
    o_s = acc_sc[...] / jnp.maximum(l, 1e-30)

    n_back = WINDOW // KEY_CHUNK
    s_parts, chunks = [], []
    for back in range(n_back, -1, -1):
        kc = jnp.maximum(qb - back, 0)
        edge = TILE_WINDOW_EDGE if back == n_back else back
        tile = jnp.where(qb >= back, edge, TILE_MASKED)
        k_chunk = kw_ref[0, 0, pl.ds(pl.multiple_of(kc * KEY_CHUNK, KEY_CHUNK), KEY_CHUNK), :]
        s_parts.append(_dot(k_chunk, qT) + bt_ref[0, tile])
        chunks.append(kc)
    s = jnp.concatenate(s_parts, axis=0)
    p = jnp.exp(s - jnp.max(s, axis=0, keepdims=True))
    l = jnp.sum(p, axis=0, keepdims=True)
    p = p.astype(BF16)
    o_w = jnp.zeros((HEAD_DIM, GQ), F32)
    for n, kc in enumerate(chunks):
        o_w = o_w + _dot(vwT_ref[0, 0, kc], p[n * KEY_CHUNK:(n + 1) * KEY_CHUNK])
    o_w = o_w / jnp.maximum(l, 1e-30)

    gate = jax.nn.sigmoid(gate_ref[0, 0, 0])
    o = gate[0:1] * o_c + gate[1:2] * o_s + gate[2:3] * o_w
    o_ref[0] = jnp.concatenate([o[:, g * Q_BLOCK:(g + 1) * Q_BLOCK].T for g in range(G)], axis=1)


def _nsa_attention(q, kc, vcT, ks, vsT, kw, vwT, gates, overlapT, bias_tiles, cmp_bias):
    B, T, _ = q.shape
    Hkv, G = NSA_KV_HEADS, NSA_GROUP
    GQ = G * Q_BLOCK
    nqb, nkc = T // Q_BLOCK, T // KEY_CHUNK
    n_cmp_pad = kc.shape[2]
    n_sel = T // SEL_BLOCK
    half_keys = SEL_STEP_TILES // 2 * KEY_CHUNK
    assert T % (2 * half_keys) == 0 and ks.shape[-1] == LANES
    per_head = lambda b, h, i: (b, h, 0, 0)
    chunked = lambda w: pl.BlockSpec((1, 1, T // w, HEAD_DIM, w), lambda b, h, i: (b, h, 0, 0, 0))
    return pl.pallas_call(
        functools.partial(_nsa_kernel, n_sel=n_sel, n_cmp_pad=n_cmp_pad),
        grid=(B, Hkv, nqb),
        in_specs=[pl.BlockSpec((1, Q_BLOCK, G * HEAD_DIM), lambda b, h, i: (b, i, h)),
                  pl.BlockSpec((1, 1, n_cmp_pad, HEAD_DIM), per_head),
                  pl.BlockSpec((1, 1, HEAD_DIM, n_cmp_pad), per_head),
                  pl.BlockSpec((1, 1, T, LANES), per_head), chunked(half_keys),
                  pl.BlockSpec((1, 1, T, HEAD_DIM), per_head), chunked(KEY_CHUNK),
                  pl.BlockSpec((1, 1, 1, 3, GQ), lambda b, h, i: (b, h, i, 0, 0)),
                  pl.BlockSpec(overlapT.shape, lambda b, h, i: (0, 0)),
                  pl.BlockSpec((1, N_ALL_TILES, KEY_CHUNK, GQ), lambda b, h, i: (h, 0, 0, 0)),
                  pl.BlockSpec((1, 2 * n_cmp_pad, GQ), lambda b, h, i: (h, 0, 0))],
        out_specs=pl.BlockSpec((1, Q_BLOCK, G * HEAD_DIM), lambda b, h, i: (b, i, h)),
        out_shape=jax.ShapeDtypeStruct((B, T, NSA_WIDTH), F32),
        scratch_shapes=[pltpu.VMEM((HEAD_DIM, GQ), F32), pltpu.VMEM((LANES, GQ), BF16),
                        pltpu.VMEM((n_sel, GQ), F32)],
        compiler_params=_params("parallel", "parallel", "arbitrary"), name="nsa_attention",
    )(q, kc, vcT, ks, vsT, kw, vwT, gates, overlapT, bias_tiles, cmp_bias)


def _rwkv_chunk_kernel(rw_ref, prev_ref, mu_ref, w0_ref, wup_ref, a0_ref, aup_ref, gup_ref, kk_ref, ka_ref,
                       rk_ref, seg_ref, tri_ref, q_ref, y0_ref, a_ref, d_ref, g_ref, bonus_ref):
    C, W, N = RWKV_CHUNK, RWKV_WIDTH, HEAD_DIM
    c = pl.program_id(1)
    x = rw_ref[0]
    row = lax.broadcasted_iota(jnp.int32, (C, 1), 0)
    last_prev = jnp.where(c == 0, 0.0, prev_ref[0, SUBLANES - 1:SUBLANES, :])
    x_prev = jnp.where(row == 0, last_prev, pltpu.roll(x, 1, axis=0))
    xs = x + (x_prev - x) * mu_ref[...]
    r, k, v = xs[:, 0:W], xs[:, W:2 * W], xs[:, 2 * W:3 * W]
    o = 3 * W
    wd, ad, gd = xs[:, o:o + DECAY_LORA], xs[:, o + DECAY_LORA:o + DECAY_LORA + AAA_LORA], \
        xs[:, o + DECAY_LORA + AAA_LORA:]
    w_log = -jax.nn.softplus(-(w0_ref[...] + _dot(jnp.tanh(wd).astype(BF16), wup_ref[...]))) - 0.5
    lw = -jnp.exp(w_log)
    lr = jax.nn.sigmoid(a0_ref[...] + _dot(ad.astype(BF16), aup_ref[...]))
    g_ref[0] = _dot(jax.nn.sigmoid(gd).astype(BF16), gup_ref[...])
    kk = k * kk_ref[...]
    kk = kk * lax.rsqrt(jnp.maximum(_dot_f32_lhs(kk * kk, seg_ref[...]), 1e-24))
    k = k * (1.0 + (lr - 1.0) * ka_ref[...])
    bonus_ref[0] = _dot_f32_lhs(r * k * rk_ref[...], seg_ref[...]) * v
    a_vec, b_vec = -kk, kk * lr

    L = _dot_f32_rhs(tri_ref[...], lw)
    L_end = L[C - 1:C, :]
    e_neg = jnp.exp(-L)
    e_rem = jnp.exp(L_end - L)
    At, Bt, Kt, Rt = a_vec * jnp.exp(L - lw), b_vec * e_neg, k * e_neg, r * jnp.exp(L)
    Bg, Kg = b_vec * e_rem, k * e_rem
    decay_end = jnp.exp(L_end)

    ri = lax.broadcasted_iota(jnp.int32, (C, C), 0)
    cj = lax.broadcasted_iota(jnp.int32, (C, C), 1)
    strict, incl = ri > cj, ri >= cj
    eye_c = (ri == cj).astype(F32)
    eye_n = (lax.broadcasted_iota(jnp.int32, (N, N), 0) == lax.broadcasted_iota(jnp.int32, (N, N), 1)).astype(F32)
    H = RWKV_HEADS
    hs = [slice(h * N, (h + 1) * N) for h in range(H)]
    ar = [jnp.concatenate([At[:, s], Rt[:, s]], axis=0) for s in hs]
    zb = [_dot_nt(ar[h], Bt[:, hs[h]]) for h in range(H)]
    zk = [_dot_nt(ar[h], Kt[:, hs[h]]) for h in range(H)]
    n_mat = [jnp.where(strict, z[0:C], 0.0) for z in zb]
    m_mat = [jnp.where(strict, z[0:C], 0.0) for z in zk]
    mv = [_dot(m_mat[h], v[:, hs[h]]) for h in range(H)]
    t_inv, n_pow = [eye_c + n for n in n_mat], n_mat
    for _ in range(int(math.log2(C)) - 1):
        n_pow = [_dot(n, n) for n in n_pow]
        t_inv = [t + _dot(t, n) for t, n in zip(t_inv, n_pow)]
    ta = [_dot(t_inv[h], At[:, hs[h]]) for h in range(H)]
    g0 = [_dot(t_inv[h], mv[h]) for h in range(H)]
    pb = [jnp.concatenate([jnp.where(incl, zb[h][C:], 0.0), Bg[:, hs[h]].T], axis=0) for h in range(H)]
    pkk = [jnp.concatenate([jnp.where(incl, zk[h][C:], 0.0), Kg[:, hs[h]].T], axis=0) for h in range(H)]
    x_ta = [_dot(pb[h], ta[h]) for h in range(H)]
    x_g0 = [_dot(pb[h], g0[h]) for h in range(H)]
    x_v = [_dot(pkk[h], v[:, hs[h]]) for h in range(H)]
    for h in range(H):
        q_ref[0, :, hs[h]] = Rt[:, hs[h]] + x_ta[h][0:C]
        y0_ref[0, :, hs[h]] = x_g0[h][0:C] + x_v[h][0:C]
        a_ref[0, 0, h] = x_ta[h][C:] + eye_n * decay_end[:, hs[h]]
        d_ref[0, 0, h] = x_g0[h][C:] + x_v[h][C:]


def _rwkv_chunks(rw, mu, w0, w_up, a0, a_up, g_up, k_k, k_a, r_k):
    B, T, cols = rw.shape
    C, W, H, N = RWKV_CHUNK, RWKV_WIDTH, RWKV_HEADS, HEAD_DIM
    nc = T // C
    seg = jnp.asarray(np.kron(np.eye(H), np.ones((N, N))), BF16)
    tri = jnp.asarray(np.tril(np.ones((C, C))), BF16)
    row = lambda z: z.reshape(1, -1)
    const = lambda b, c: (0, 0)
    vec = pl.BlockSpec((1, W), const)
    tok = pl.BlockSpec((1, C, W), lambda b, c: (b, c, 0))
    mat = pl.BlockSpec((1, 1, H, N, N), lambda b, c: (b, c, 0, 0, 0))
    return pl.pallas_call(
        _rwkv_chunk_kernel,
        grid=(B, nc),
        in_specs=[pl.BlockSpec((1, C, cols), lambda b, c: (b, c, 0)),
                  pl.BlockSpec((1, SUBLANES, cols), lambda b, c: (b, jnp.maximum(c * (C // SUBLANES) - 1, 0), 0)),
                  pl.BlockSpec((1, cols), const), vec,
                  pl.BlockSpec((DECAY_LORA, W), const), vec,
                  pl.BlockSpec((AAA_LORA, W), const),
                  pl.BlockSpec((GATE_LORA, W), const), vec, vec, vec,
                  pl.BlockSpec((W, W), const), pl.BlockSpec((C, C), const)],
        out_specs=[tok, tok, mat, mat, tok, tok],
        out_shape=[jax.ShapeDtypeStruct((B, T, W), F32), jax.ShapeDtypeStruct((B, T, W), F32),
                   jax.ShapeDtypeStruct((B, nc, H, N, N), F32), jax.ShapeDtypeStruct((B, nc, H, N, N), F32),
                   jax.ShapeDtypeStruct((B, T, W), F32), jax.ShapeDtypeStruct((B, T, W), F32)],
        compiler_params=_params("parallel", "parallel"), name="rwkv_chunks",
    )(rw, rw, row(mu), row(w0), w_up.astype(BF16), row(a0), a_up.astype(BF16), g_up.astype(BF16),
      row(k_k), row(k_a), row(r_k), seg, tri)


def _rwkv_scan_kernel(a_ref, d_ref, q_ref, y0_ref, g_ref, bonus_ref, lw_ref, lb_ref, o_ref, h_sc):
    N = HEAD_DIM
    @pl.when(pl.program_id(0) == 0)
    def _():
        h_sc[...] = jnp.zeros(h_sc.shape, F32)

    for b in range(q_ref.shape[0]):
        for h in range(RWKV_HEADS):
            hs = slice(h * N, (h + 1) * N)
            state = h_sc[b, h]
            y = _dot(q_ref[b, :, hs], state, precision=HIGHEST) + y0_ref[b, :, hs]
            h_sc[b, h] = _dot(a_ref[b, 0, h], state, precision=HIGHEST) + d_ref[b, 0, h]
            mean = jnp.mean(y, axis=-1, keepdims=True)
            var = jnp.mean(jnp.square(y - mean), axis=-1, keepdims=True)
            yn = (y - mean) * lax.rsqrt(var + LNX_EPS)
            yn = yn * lw_ref[:, hs] + lb_ref[:, hs]
            o_ref[b, :, hs] = (yn + bonus_ref[b, :, hs]) * g_ref[b, :, hs]


def _rwkv_scan(A, D, Q, Y0, g, bonus, lnx_w, lnx_b):
    B, nc, H, N, _ = A.shape
    T, W, C = Q.shape[1], Q.shape[2], RWKV_CHUNK
    tok = pl.BlockSpec((B, C, W), lambda c: (0, c, 0))
    mat = pl.BlockSpec((B, 1, H, N, N), lambda c: (0, c, 0, 0, 0))
    vec = pl.BlockSpec((1, W), lambda c: (0, 0))
    return pl.pallas_call(
        _rwkv_scan_kernel,
        grid=(nc,),
        in_specs=[mat, mat, tok, tok, tok, tok, vec, vec],
        out_specs=tok,
        out_shape=jax.ShapeDtypeStruct((B, T, W), F32),
        scratch_shapes=[pltpu.VMEM((B, H, N, N), F32)],
        compiler_params=_params("arbitrary"), name="rwkv_scan",
    )(A, D, Q, Y0, g, bonus, lnx_w.reshape(1, W), lnx_b.reshape(1, W))


def _mix_xattn_kernel(x_ref, on_ref, or_ref, wo1_ref, wo2_ref, gx_ref, wq_ref, k_ref, v_ref, wo_ref, o_ref):
    x1 = x_ref[0] + _dot(on_ref[0].astype(BF16), wo1_ref[...]) + _dot(or_ref[0].astype(BF16), wo2_ref[...])
    q = _dot(_rms(x1, gx_ref[...]).astype(BF16), wq_ref[...])
    dh = q.shape[-1] // XATTN_HEADS
    qbf = (q * (dh ** -0.5)).astype(BF16)
    outs = []
    for h in range(XATTN_HEADS):
        hs = slice(h * dh, (h + 1) * dh)
        s = _dot_nt(qbf[:, hs], k_ref[0, :, hs])
        e = jnp.exp(s - jnp.max(s, axis=-1, keepdims=True))
        p = e / jnp.sum(e, axis=-1, keepdims=True)
        outs.append(_dot(p.astype(BF16), v_ref[0, :, hs]))
    o = jnp.concatenate(outs, axis=1).astype(BF16)
    o_ref[0] = x1 + _dot(o, wo_ref[...])


def _mix_xattn(x, o_nsa, o_rwkv, w_out, norm_x_g, w_q, mem_k, mem_v, w_o):
    B, T, D = x.shape
    M = mem_k.shape[1]
    tm = min(ROW_TILE, T)
    wo1, wo2 = w_out[:NSA_WIDTH].astype(BF16), w_out[NSA_WIDTH:].astype(BF16)
    const = lambda b, i: (0, 0)
    tile = lambda w: pl.BlockSpec((1, tm, w), lambda b, i: (b, i, 0))
    return pl.pallas_call(
        _mix_xattn_kernel,
        grid=(B, T // tm),
        in_specs=[tile(D), tile(NSA_WIDTH), tile(RWKV_WIDTH),
                  pl.BlockSpec(wo1.shape, const), pl.BlockSpec(wo2.shape, const),
                  pl.BlockSpec((1, D), const), pl.BlockSpec((D, D), const),
                  pl.BlockSpec((1, M, D), lambda b, i: (b, 0, 0)), pl.BlockSpec((1, M, D), lambda b, i: (b, 0, 0)),
                  pl.BlockSpec((D, D), const)],
        out_specs=tile(D),
        out_shape=jax.ShapeDtypeStruct((B, T, D), F32),
        compiler_params=_params("parallel", "parallel"), name="mix_xattn",
    )(x, o_nsa, o_rwkv, wo1, wo2, norm_x_g.reshape(1, D), w_q.astype(BF16), mem_k, mem_v, w_o.astype(BF16))


def _ffn_kernel(x_ref, g_ref, wg_ref, wu_ref, wd_ref, gf_ref, o_ref, *, final_norm):
    x = x_ref[...]
    h = _rms(x, g_ref[...]).astype(BF16)
    act = (jax.nn.silu(_dot(h, wg_ref[...])) * _dot(h, wu_ref[...])).astype(BF16)
    y = x + _dot(act, wd_ref[...])
    o_ref[...] = _rms(y, gf_ref[...]) if final_norm else y


def _ffn(x, norm_g, w_gate, w_up, w_down, final_g, final_norm):
    R, D = x.shape
    F = w_gate.shape[1]
    tm = min(FFN_ROW_TILE, R)
    const = lambda i: (0, 0)
    resident = lambda shape: pl.BlockSpec(shape, const, pipeline_mode=pl.Buffered(1))
    return pl.pallas_call(
        functools.partial(_ffn_kernel, final_norm=final_norm),
        grid=(R // tm,),
        in_specs=[pl.BlockSpec((tm, D), lambda i: (i, 0)), pl.BlockSpec((1, D), const),
                  resident((D, F)), resident((D, F)), resident((F, D)), pl.BlockSpec((1, D), const)],
        out_specs=pl.BlockSpec((tm, D), lambda i: (i, 0)),
        out_shape=jax.ShapeDtypeStruct((R, D), F32),
        compiler_params=_params("parallel"), name="ffn",
    )(x, norm_g.reshape(1, D), w_gate.astype(BF16), w_up.astype(BF16), w_down.astype(BF16),
      final_g.reshape(1, D))


def _overlap_matrix(n_cmp_pad, n_sel):
    c = np.arange(n_cmp_pad)[:, None] * CMP_STRIDE
    s = np.arange(n_sel)[None, :] * SEL_BLOCK
    return ((c <= s + SEL_BLOCK - 1) & (c + CMP_LEN - 1 >= s)).astype(np.float32)


def _layer(x, mem, rel_bias, final_g, is_last, norm_mix_g, w_in, nsa_gate_b, cmp_pe_k, cmp_pe_v,
           cmp_k_w1, cmp_k_b1, cmp_k_w2, cmp_v_w1, cmp_v_b1, cmp_v_w2,
           rwkv_mu, rwkv_w0, rwkv_w_up, rwkv_a0, rwkv_a_up, rwkv_g_up,
           rwkv_k_k, rwkv_k_a, rwkv_r_k, rwkv_lnx_w, rwkv_lnx_b, w_out,
           norm_x_g, norm_mem_g, w_q_x, w_kv_x, w_o_x, norm_ffn_g, w_gate, w_up, w_down):
    B, T, D = x.shape
    Hkv, G, dh = NSA_KV_HEADS, NSA_GROUP, HEAD_DIM
    n_gate = 3 * NSA_HEADS
    kv0 = NSA_WIDTH
    g0 = kv0 + 6 * KV_WIDTH
    w_q, w_kv, w_rw = w_in[:, :kv0], w_in[:, kv0:g0], w_in[:, NSA_COLS:]
    w_g = jnp.pad(w_in[:, g0:NSA_COLS], ((0, 0), (0, LANES - n_gate)))
    b_g = jnp.pad(nsa_gate_b, (0, LANES - n_gate))
    q, kv, gl, rw = _norm_matmul(
        x.reshape(B * T, D), norm_mix_g, [w.astype(BF16) for w in (w_q, w_kv, w_g, w_rw)],
        [None, None, b_g, None], [F32, F32, F32, F32], ROW_TILE)

    kv = kv.reshape(B, T, 6, Hkv, dh).transpose(2, 0, 3, 1, 4)
    n16 = T // CMP_STRIDE
    x16 = lambda a: a.reshape(B, Hkv, n16, CMP_STRIDE * dh)
    kc = _compress(x16(kv[0]), cmp_pe_k, cmp_k_w1, cmp_k_b1, cmp_k_w2, False)
    vcT = _compress(x16(kv[1]), cmp_pe_v, cmp_v_w1, cmp_v_b1, cmp_v_w2, True)
    chunkT = lambda a, w: a.astype(BF16).reshape(B, Hkv, T // w, w, dh).transpose(0, 1, 2, 4, 3)
    gates = gl[:, :n_gate].reshape(B, T // Q_BLOCK, Q_BLOCK, Hkv, G, 3).transpose(0, 3, 1, 5, 4, 2)
    gates = gates.reshape(B, Hkv, T // Q_BLOCK, 3, G * Q_BLOCK)
    bias_tiles, cmp_bias = _bias_tiles(rel_bias, n16)
    blk_in_step = (np.arange(T) // SEL_BLOCK) % SEL_STEP_BLOCKS
    onehot = jnp.asarray(blk_in_step[:, None] == np.arange(LANES - dh)[None, :], BF16)
    ks_aug = jnp.concatenate([kv[2].astype(BF16), jnp.broadcast_to(onehot, (B, Hkv, T, LANES - dh))], axis=-1)
    o_nsa = _nsa_attention(q.reshape(B, T, NSA_WIDTH), kc, vcT, ks_aug,
                           chunkT(kv[3], SEL_STEP_TILES // 2 * KEY_CHUNK),
                           kv[4].astype(BF16), chunkT(kv[5], KEY_CHUNK), gates,
                           jnp.asarray(_overlap_matrix(n16, T // SEL_BLOCK).T, BF16), bias_tiles, cmp_bias)

    Q, Y0, A, Dm, g, bonus = _rwkv_chunks(rw.reshape(B, T, RWKV_COLS), rwkv_mu, rwkv_w0, rwkv_w_up, rwkv_a0,
                                          rwkv_a_up, rwkv_g_up, rwkv_k_k, rwkv_k_a, rwkv_r_k.reshape(-1))
    o_rwkv = _rwkv_scan(A, Dm, Q, Y0, g, bonus, rwkv_lnx_w, rwkv_lnx_b)

    M = mem.shape[1]
    (kv_mem,) = _norm_matmul(mem.reshape(B * M, D), norm_mem_g, [w_kv_x.astype(BF16)], [None], [BF16], ROW_TILE)
    kv_mem = kv_mem.reshape(B, M, 2 * D)
    x = _mix_xattn(x, o_nsa, o_rwkv, w_out, norm_x_g, w_q_x, kv_mem[..., :D], kv_mem[..., D:], w_o_x)
    x = _ffn(x.reshape(B * T, D), norm_ffn_g, w_gate, w_up, w_down, final_g, is_last)
    return x.reshape(B, T, D)


def kernel(x, mem, rel_bias, norm_f_g, norm_mix_g, w_in, nsa_gate_b, cmp_pe_k, cmp_pe_v, cmp_k_w1, cmp_k_b1, cmp_k_w2, cmp_v_w1, cmp_v_b1, cmp_v_w2, rwkv_mu, rwkv_w0, rwkv_w_up, rwkv_a0, rwkv_a_up, rwkv_g_up, rwkv_k_k, rwkv_k_a, rwkv_r_k, rwkv_lnx_w, rwkv_lnx_b, w_out, norm_x_g, norm_mem_g, w_q_x, w_kv_x, w_o_x, norm_ffn_g, w_gate, w_up, w_down):
    stacked = (norm_mix_g, w_in, nsa_gate_b, cmp_pe_k, cmp_pe_v, cmp_k_w1, cmp_k_b1, cmp_k_w2, cmp_v_w1,
               cmp_v_b1, cmp_v_w2, rwkv_mu, rwkv_w0, rwkv_w_up, rwkv_a0, rwkv_a_up, rwkv_g_up, rwkv_k_k,
               rwkv_k_a, rwkv_r_k, rwkv_lnx_w, rwkv_lnx_b, w_out, norm_x_g, norm_mem_g, w_q_x, w_kv_x, w_o_x,
               norm_ffn_g, w_gate, w_up, w_down)
    depth = w_in.shape[0]
    for l in range(depth):
        x = _layer(x, mem, rel_bias, norm_f_g, l == depth - 1, *[p[l] for p in stacked])
    return x
```

```python
import functools
import math

import numpy as np
import jax
import jax.numpy as jnp
from jax import lax
from jax.experimental import pallas as pl
from jax.experimental.pallas import tpu as pltpu

F32 = jnp.float32
BF16 = jnp.bfloat16
HIGHEST = lax.Precision.HIGHEST

LANES = 128
SUBLANES = 8
BF16_ROWS = 16
VMEM_LIMIT_BYTES = 56 * 1024 * 1024

HEAD_DIM = 64
NSA_HEADS = 8
NSA_KV_HEADS = 2
NSA_GROUP = NSA_HEADS // NSA_KV_HEADS
NSA_WIDTH = NSA_HEADS * HEAD_DIM
KV_WIDTH = NSA_KV_HEADS * HEAD_DIM
RWKV_HEADS = 8
RWKV_WIDTH = RWKV_HEADS * HEAD_DIM
CMP_LEN = 32
CMP_STRIDE = 16
SEL_BLOCK = 64
SEL_SHIFT = 6
SEL_TOP = 16
WINDOW = 512
Q_BLOCK = 128
DECAY_LORA = 64
AAA_LORA = 64
GATE_LORA = 128
N_BUCKETS = 32
MAX_DISTANCE = 2048
XATTN_HEADS = 4
RMS_EPS = 1e-6
LNX_EPS = 64e-5
FORCE_SCORE = 1e4
NEG_SCORE = -1e9
MASK_SCORE = -1e30
LOG2E = math.log2(math.e)
RWKV_COLS = 3 * RWKV_WIDTH + DECAY_LORA + AAA_LORA + GATE_LORA
NSA_COLS = NSA_WIDTH + 6 * KV_WIDTH + 3 * NSA_HEADS

KEY_CHUNK = 128
RWKV_CHUNK = 64
ROW_TILE = 512
FFN_ROW_TILE = 256


def _t5_thresholds():
    d = np.arange(0, 2 * MAX_DISTANCE, dtype=np.int64)
    max_exact = N_BUCKETS // 2
    nf = np.maximum(d, 1).astype(np.float32)
    large = max_exact + (np.log(nf / np.float32(max_exact)) / np.float32(math.log(MAX_DISTANCE / max_exact))
                         * np.float32(N_BUCKETS - max_exact)).astype(np.int32)
    bucket = np.where(d < max_exact, d, np.minimum(large, N_BUCKETS - 1))
    return [int(np.argmax(bucket >= k)) for k in range(N_BUCKETS)]


T5_THRESHOLDS = _t5_thresholds()
N_BIAS_TILES = -(-(T5_THRESHOLDS[-1] + KEY_CHUNK) // KEY_CHUNK) + 1
TILE_MASKED = N_BIAS_TILES
TILE_WINDOW_EDGE = N_BIAS_TILES + 1
N_ALL_TILES = N_BIAS_TILES + 2
SEL_STEP_BLOCKS = 16
SEL_STAGE_TILES = 4
SEL_GROUP_STAGES = SEL_STEP_BLOCKS * SEL_BLOCK // (SEL_STAGE_TILES * KEY_CHUNK)


def _params(*semantics):
    return pltpu.CompilerParams(dimension_semantics=semantics, vmem_limit_bytes=VMEM_LIMIT_BYTES)


def _rms(x, g):
    return x * lax.rsqrt(jnp.mean(x * x, axis=-1, keepdims=True) + RMS_EPS) * g


def _dot(a, b, **kw):
    return jnp.dot(a, b, preferred_element_type=F32, **kw)


def _split3(x):
    hi = x.astype(BF16)
    r1 = x - hi.astype(F32)
    mid = r1.astype(BF16)
    lo = (r1 - mid.astype(F32)).astype(BF16)
    return hi, mid, lo


def _dot_f32_lhs(x, w01):
    w = w01.astype(BF16)
    hi, mid, lo = _split3(x)
    return _dot(hi, w) + (_dot(mid, w) + _dot(lo, w))


def _dot_f32_rhs(w01, x):
    w = w01.astype(BF16)
    hi, mid, lo = _split3(x)
    return _dot(w, hi) + (_dot(w, mid) + _dot(w, lo))


def _dot_nt(a, b, **kw):
    return lax.dot_general(a, b, (((1,), (1,)), ((), ())), preferred_element_type=F32, **kw)


def _norm_matmul_kernel(x_ref, g_ref, *refs, nseg, bias_flags):
    nb = sum(bias_flags)
    w_refs, b_refs, o_refs = refs[:nseg], refs[nseg:nseg + nb], refs[nseg + nb:]
    xn = _rms(x_ref[...], g_ref[...]).astype(BF16)
    bi = 0
    for s in range(nseg):
        y = _dot(xn, w_refs[s][...])
        if bias_flags[s]:
            y = y + b_refs[bi][...]
            bi += 1
        o_refs[s][...] = y.astype(o_refs[s].dtype)


def _norm_matmul(x, g, weights, biases, out_dtypes, row_tile):
    R, D = x.shape
    tm = min(row_tile, R)
    assert R % tm == 0
    nseg = len(weights)
    bias_flags = tuple(b is not None for b in biases)
    const = lambda i: (0, 0)
    in_specs = [pl.BlockSpec((tm, D), lambda i: (i, 0)), pl.BlockSpec((1, D), const)]
    in_specs += [pl.BlockSpec(w.shape, const) for w in weights]
    in_specs += [pl.BlockSpec((1, b.shape[-1]), const) for b in biases if b is not None]
    out_specs = [pl.BlockSpec((tm, w.shape[1]), lambda i: (i, 0)) for w in weights]
    out_shape = [jax.ShapeDtypeStruct((R, w.shape[1]), dt) for w, dt in zip(weights, out_dtypes)]
    return pl.pallas_call(
        functools.partial(_norm_matmul_kernel, nseg=nseg, bias_flags=bias_flags),
        grid=(R // tm,), in_specs=in_specs, out_specs=out_specs, out_shape=out_shape,
        compiler_params=_params("parallel"), name="norm_matmul",
    )(x, g.reshape(1, D), *weights, *[b.reshape(1, -1) for b in biases if b is not None])


def _compress_kernel(x_ref, pe_ref, w1_ref, b1_ref, w2_ref, o_ref, *, transpose_out):
    x = x_ref[0, 0]
    n16 = x.shape[0]
    lo = _dot((x + pe_ref[0:1, :]).astype(BF16), w1_ref[0])
    hi = _dot((x + pe_ref[1:2, :]).astype(BF16), w1_ref[1])
    h = lo + pltpu.roll(hi, n16 - 1, axis=0) + b1_ref[...]
    h = jax.nn.gelu(h).astype(BF16)
    if transpose_out:
        o_ref[0, 0] = _dot_nt(w2_ref[...], h).astype(o_ref.dtype)
    else:
        o_ref[0, 0] = _dot(h, w2_ref[...]).astype(o_ref.dtype)


def _compress(x16, pe, w1, b1, w2, transpose_out):
    B, H, n16, flat = x16.shape
    hidden = w1.shape[1]
    pe2 = pe.reshape(2, flat)
    w1s = w1.astype(BF16).reshape(2, flat, hidden)
    w2b = (w2.T if transpose_out else w2).astype(BF16)
    oshape = (B, H, HEAD_DIM, n16) if transpose_out else (B, H, n16, HEAD_DIM)
    return pl.pallas_call(
        functools.partial(_compress_kernel, transpose_out=transpose_out),
        grid=(B, H),
        in_specs=[pl.BlockSpec((1, 1, n16, flat), lambda b, h: (b, h, 0, 0)),
                  pl.BlockSpec((2, flat), lambda b, h: (0, 0)),
                  pl.BlockSpec((2, flat, hidden), lambda b, h: (0, 0, 0)),
                  pl.BlockSpec((1, hidden), lambda b, h: (0, 0)),
                  pl.BlockSpec(w2b.shape, lambda b, h: (0, 0))],
        out_specs=pl.BlockSpec((1, 1) + oshape[2:], lambda b, h: (b, h, 0, 0)),
        out_shape=jax.ShapeDtypeStruct(oshape, BF16),
        compiler_params=_params("parallel", "parallel"), name="nsa_compress",
    )(x16, pe2, w1s, b1.reshape(1, hidden), w2b)


def _bias_of_distance(tab_ref, h, d):
    val = jnp.full(d.shape, tab_ref[h, 0], F32)
    for k in range(1, N_BUCKETS):
        val = jnp.where(d >= T5_THRESHOLDS[k], tab_ref[h, k], val)
    return val * LOG2E


def _bias_tiles_kernel(tab_ref, bt_ref, cb_ref, *, n_cmp_pad):
    hkv = pl.program_id(0)
    j = lax.broadcasted_iota(jnp.int32, (KEY_CHUNK, Q_BLOCK), 0)
    i = lax.broadcasted_iota(jnp.int32, (KEY_CHUNK, Q_BLOCK), 1)
    r2 = lax.broadcasted_iota(jnp.int32, (2 * n_cmp_pad, Q_BLOCK), 0)
    i2 = lax.broadcasted_iota(jnp.int32, (2 * n_cmp_pad, Q_BLOCK), 1)
    l2 = r2 - (n_cmp_pad - KEY_CHUNK)
    d2 = i2 - CMP_STRIDE * l2 + (CMP_STRIDE * KEY_CHUNK - Q_BLOCK - (CMP_LEN - 1))
    d2 = jnp.where((l2 >= 0) & (l2 < KEY_CHUNK), d2, 2 * MAX_DISTANCE)
    for g in range(NSA_GROUP):
        h = hkv * NSA_GROUP + g
        lanes = slice(g * Q_BLOCK, (g + 1) * Q_BLOCK)
        for m in range(N_BIAS_TILES):
            tile = _bias_of_distance(tab_ref, h, m * KEY_CHUNK + i - j)
            if m == 0:
                tile = jnp.where(j <= i, tile, MASK_SCORE)
            bt_ref[0, m, :, lanes] = tile
        bt_ref[0, TILE_MASKED, :, lanes] = jnp.full((KEY_CHUNK, Q_BLOCK), MASK_SCORE, F32)
        edge = _bias_of_distance(tab_ref, h, WINDOW + i - j)
        bt_ref[0, TILE_WINDOW_EDGE, :, lanes] = jnp.where(j > i, edge, MASK_SCORE)
        cb_ref[0, :, lanes] = _bias_of_distance(tab_ref, h, d2)


def _bias_tiles(rel_bias, n_cmp_pad):
    assert CMP_STRIDE * KEY_CHUNK - Q_BLOCK - (CMP_LEN - 1) >= T5_THRESHOLDS[-1]
    GQ = NSA_GROUP * Q_BLOCK
    return pl.pallas_call(
        functools.partial(_bias_tiles_kernel, n_cmp_pad=n_cmp_pad),
        grid=(NSA_KV_HEADS,),
        in_specs=[pl.BlockSpec(memory_space=pltpu.SMEM)],
        out_specs=[pl.BlockSpec((1, N_ALL_TILES, KEY_CHUNK, GQ), lambda h: (h, 0, 0, 0)),
                   pl.BlockSpec((1, 2 * n_cmp_pad, GQ), lambda h: (h, 0, 0))],
        out_shape=[jax.ShapeDtypeStruct((NSA_KV_HEADS, N_ALL_TILES, KEY_CHUNK, GQ), F32),
                   jax.ShapeDtypeStruct((NSA_KV_HEADS, 2 * n_cmp_pad, GQ), F32)],
        compiler_params=_params("parallel"), name="t5_bias_tiles",
    )(rel_bias.T)


def _nsa_kernel(q_ref, kc_ref, vcT_ref, ks_ref, vsT_ref, kw_ref, vwT_ref, gate_ref, ovT_ref, bt_ref, cb_ref,
                o_ref, acc_sc, qaug_sc, seladd_sc, s0_sc, s1_sc, p0_sc, p1_sc, *, n_sel, n_cmp_pad):
    G, GQ = NSA_GROUP, NSA_GROUP * Q_BLOCK
    qb = pl.program_id(2)
    q = q_ref[0]
    q4 = jnp.concatenate([q[:, g * HEAD_DIM:(g + 1) * HEAD_DIM] for g in range(G)], axis=0)
    qT = (q4 * (HEAD_DIM ** -0.5 * LOG2E)).T.astype(BF16)
    qaug_sc[0:HEAD_DIM, :] = qT
    qaug_sc[HEAD_DIM:, :] = jnp.zeros((qaug_sc.shape[0] - HEAD_DIM, GQ), BF16)
    tile_g = lambda a: jnp.concatenate([a] * G, axis=1)
    t = qb * Q_BLOCK + lax.broadcasted_iota(jnp.int32, (1, Q_BLOCK), 1)

    start = pl.multiple_of(n_cmp_pad - (Q_BLOCK // CMP_STRIDE) * (qb + 1), SUBLANES)
    s = _dot(kc_ref[0, 0], qT) + cb_ref[0, pl.ds(start, n_cmp_pad), :]
    ci = lax.broadcasted_iota(jnp.int32, (n_cmp_pad, Q_BLOCK), 0)
    vis = tile_g(jnp.where((CMP_STRIDE * ci + (CMP_LEN - 1)) <= t, 1.0, 0.0)) > 0.5
    s = jnp.where(vis, s, MASK_SCORE)
    e = jnp.where(vis, jnp.exp2(s - jnp.max(s, axis=0, keepdims=True)), 0.0)
    p = e / jnp.maximum(jnp.sum(e, axis=0, keepdims=True), 1e-30)
    o_c = _dot(vcT_ref[0, 0], p.astype(BF16))
    psum = p[:, 0:Q_BLOCK]
    for g in range(1, G):
        psum = psum + p[:, g * Q_BLOCK:(g + 1) * Q_BLOCK]

    imp = _dot_f32_rhs(ovT_ref[...], psum)
    bj = lax.broadcasted_iota(jnp.int32, (n_sel, Q_BLOCK), 0)
    cur = jnp.right_shift(t, SEL_SHIFT)
    forced = (bj == 0) | (bj == cur) | (bj == cur - 1)
    valid = bj * SEL_BLOCK <= t
    score = jnp.where(valid, jnp.where(forced, FORCE_SCORE, imp), NEG_SCORE)
    bjf = bj.astype(F32)
    sel = jnp.zeros((n_sel, Q_BLOCK), F32)
    for _ in range(min(SEL_TOP, n_sel)):
        mx = jnp.max(score, axis=0, keepdims=True)
        first = jnp.min(jnp.where(score == mx, bjf, float(n_sel)), axis=0, keepdims=True)
        pick = bjf == first
        sel = jnp.where(pick, 1.0, sel)
        score = jnp.where(pick, -jnp.inf, score)
    seladd_sc[...] = tile_g((sel - 1.0) * (-MASK_SCORE))

    def bias_rows(first_chunk, n_chunks):
        tiles = []
        for k in range(n_chunks):
            dist = qb - (first_chunk + k)
            tiles.append(bt_ref[0, jnp.where(dist < 0, TILE_MASKED, jnp.minimum(dist, N_BIAS_TILES - 1))])
        return jnp.concatenate(tiles, axis=0)

    stage_keys = SEL_STAGE_TILES * KEY_CHUNK
    n_stages = qb // SEL_STAGE_TILES + 1
    last_stage = ks_ref.shape[2] // stage_keys - 1

    def scores(k, s_buf):
        kk = jnp.minimum(k, last_stage)
        blk0 = pl.multiple_of(kk // SEL_GROUP_STAGES * SEL_STEP_BLOCKS, SEL_STEP_BLOCKS)
        qaug_sc[HEAD_DIM:HEAD_DIM + SEL_STEP_BLOCKS, :] = seladd_sc[pl.ds(blk0, SEL_STEP_BLOCKS), :].astype(BF16)
        k0 = pl.multiple_of(kk * stage_keys, stage_keys)
        s = _dot(ks_ref[0, 0, pl.ds(k0, stage_keys), :], qaug_sc[...]) + bias_rows(k * SEL_STAGE_TILES,
                                                                                  SEL_STAGE_TILES)
        s_buf[...] = s
        return jnp.max(s, axis=0, keepdims=True)

    def weights(m, col_max, s_buf, p_buf):
        m_new = jnp.maximum(m, col_max)
        p_buf[...] = jnp.exp2(s_buf[...] - m_new).astype(BF16)
        return m_new, jnp.exp2(m - m_new)

    def accumulate(k, alpha, p_buf):
        acc_sc[...] = alpha * acc_sc[...] + _dot(vsT_ref[0, 0, jnp.clip(k, 0, last_stage)], p_buf[...])

    def pair(j, carry):
        m, col_max, alpha = carry
        k = 2 * j
        col_max1 = scores(k + 1, s1_sc)
        m, alpha0 = weights(m, col_max, s0_sc, p0_sc)
        accumulate(k - 1, alpha, p1_sc)
        col_max2 = scores(k + 2, s0_sc)
        m, alpha1 = weights(m, col_max1, s1_sc, p1_sc)
        accumulate(k, alpha0, p0_sc)
        return m, col_max2, alpha1

    acc_sc[...] = jnp.zeros(acc_sc.shape, F32)
    p1_sc[...] = jnp.zeros(p1_sc.shape, BF16)
    m_init = jnp.full((1, GQ), 0.1 * MASK_SCORE, F32)
    carry = (m_init, scores(0, s0_sc), jnp.ones((1, GQ), F32))
    n_pairs = (n_stages + 1) // 2
    _, _, alpha = lax.fori_loop(0, n_pairs, pair, carry)
    accumulate(2 * n_pairs - 1, alpha, p1_sc)
    o_s = acc_sc[0:HEAD_DIM, :] / jnp.maximum(acc_sc[HEAD_DIM:HEAD_DIM + 1, :], 1e-30)

    n_back = WINDOW // KEY_CHUNK
    s_parts, chunks = [], []
    for back in range(n_back, -1, -1):
        kc = jnp.maximum(qb - back, 0)
        edge = TILE_WINDOW_EDGE if back == n_back else back
        tile = jnp.where(qb >= back, edge, TILE_MASKED)
        k_chunk = kw_ref[0, 0, pl.ds(pl.multiple_of(kc * KEY_CHUNK, KEY_CHUNK), KEY_CHUNK), :]
        s_parts.append(_dot(k_chunk, qT) + bt_ref[0, tile])
        chunks.append(kc)
    s = jnp.concatenate(s_parts, axis=0)
    p = jnp.exp2(s - jnp.max(s, axis=0, keepdims=True)).astype(BF16)
    o_w = jnp.zeros(acc_sc.shape, F32)
    for n, kc in enumerate(chunks):
        o_w = o_w + _dot(vwT_ref[0, 0, kc], p[n * KEY_CHUNK:(n + 1) * KEY_CHUNK])
    o_w = o_w[0:HEAD_DIM] / jnp.maximum(o_w[HEAD_DIM:HEAD_DIM + 1], 1e-30)

    gate = jax.nn.sigmoid(gate_ref[0, 0, 0])
    o = gate[0:1] * o_c + gate[1:2] * o_s + gate[2:3] * o_w
    o_ref[0] = jnp.concatenate([o[:, g * Q_BLOCK:(g + 1) * Q_BLOCK].T for g in range(G)], axis=1)


def _nsa_attention(q, kc, vcT, ks, vsT, kw, vwT, gates, overlapT, bias_tiles, cmp_bias):
    B, T, _ = q.shape
    Hkv, G = NSA_KV_HEADS, NSA_GROUP
    GQ = G * Q_BLOCK
    nqb, nkc = T // Q_BLOCK, T // KEY_CHUNK
    n_cmp_pad = kc.shape[2]
    n_sel = T // SEL_BLOCK
    half_keys = SEL_STAGE_TILES * KEY_CHUNK
    assert T % (SEL_GROUP_STAGES * half_keys) == 0 and ks.shape[-1] == LANES
    per_head = lambda b, h, i: (b, h, 0, 0)
    v_rows = vsT.shape[3]
    chunked = lambda w: pl.BlockSpec((1, 1, T // w, v_rows, w), lambda b, h, i: (b, h, 0, 0, 0))
    return pl.pallas_call(
        functools.partial(_nsa_kernel, n_sel=n_sel, n_cmp_pad=n_cmp_pad),
        grid=(B, Hkv, nqb),
        in_specs=[pl.BlockSpec((1, Q_BLOCK, G * HEAD_DIM), lambda b, h, i: (b, i, h)),
                  pl.BlockSpec((1, 1, n_cmp_pad, HEAD_DIM), per_head),
                  pl.BlockSpec((1, 1, HEAD_DIM, n_cmp_pad), per_head),
                  pl.BlockSpec((1, 1, T, LANES), per_head), chunked(half_keys),
                  pl.BlockSpec((1, 1, T, HEAD_DIM), per_head), chunked(KEY_CHUNK),
                  pl.BlockSpec((1, 1, 1, 3, GQ), lambda b, h, i: (b, h, i, 0, 0)),
                  pl.BlockSpec(overlapT.shape, lambda b, h, i: (0, 0)),
                  pl.BlockSpec((1, N_ALL_TILES, KEY_CHUNK, GQ), lambda b, h, i: (h, 0, 0, 0)),
                  pl.BlockSpec((1, 2 * n_cmp_pad, GQ), lambda b, h, i: (h, 0, 0))],
        out_specs=pl.BlockSpec((1, Q_BLOCK, G * HEAD_DIM), lambda b, h, i: (b, i, h)),
        out_shape=jax.ShapeDtypeStruct((B, T, NSA_WIDTH), F32),
        scratch_shapes=[pltpu.VMEM((v_rows, GQ), F32), pltpu.VMEM((LANES, GQ), BF16),
                        pltpu.VMEM((n_sel, GQ), F32),
                        pltpu.VMEM((half_keys, GQ), F32), pltpu.VMEM((half_keys, GQ), F32),
                        pltpu.VMEM((half_keys, GQ), BF16), pltpu.VMEM((half_keys, GQ), BF16)],
        compiler_params=_params("parallel", "parallel", "arbitrary"), name="nsa_attention",
    )(q, kc, vcT, ks, vsT, kw, vwT, gates, overlapT, bias_tiles, cmp_bias)


def _rwkv_chunk_kernel(rw_ref, prev_ref, mu_ref, w0_ref, wup_ref, a0_ref, aup_ref, gup_ref, kk_ref, ka_ref,
                       rk_ref, seg_ref, tri_ref, q_ref, y0_ref, a_ref, d_ref, g_ref, bonus_ref):
    C, W, N = RWKV_CHUNK, RWKV_WIDTH, HEAD_DIM
    c = pl.program_id(1)
    x = rw_ref[0]
    row = lax.broadcasted_iota(jnp.int32, (C, 1), 0)
    last_prev = jnp.where(c == 0, 0.0, prev_ref[0, SUBLANES - 1:SUBLANES, :])
    x_prev = jnp.where(row == 0, last_prev, pltpu.roll(x, 1, axis=0))
    xs = x + (x_prev - x) * mu_ref[...]
    r, k, v = xs[:, 0:W], xs[:, W:2 * W], xs[:, 2 * W:3 * W]
    o = 3 * W
    wd, ad, gd = xs[:, o:o + DECAY_LORA], xs[:, o + DECAY_LORA:o + DECAY_LORA + AAA_LORA], \
        xs[:, o + DECAY_LORA + AAA_LORA:]
    w_log = -jax.nn.softplus(-(w0_ref[...] + _dot(jnp.tanh(wd).astype(BF16), wup_ref[...]))) - 0.5
    lw = -jnp.exp(w_log)
    lr = jax.nn.sigmoid(a0_ref[...] + _dot(ad.astype(BF16), aup_ref[...]))
    g_ref[0] = _dot(jax.nn.sigmoid(gd).astype(BF16), gup_ref[...])
    kk = k * kk_ref[...]
    kk = kk * lax.rsqrt(jnp.maximum(_dot_f32_lhs(kk * kk, seg_ref[...]), 1e-24))
    k = k * (1.0 + (lr - 1.0) * ka_ref[...])
    bonus_ref[0] = _dot_f32_lhs(r * k * rk_ref[...], seg_ref[...]) * v
    a_vec, b_vec = -kk, kk * lr

    L = _dot_f32_rhs(tri_ref[...], lw)
    L_end = L[C - 1:C, :]
    e_neg = jnp.exp(-L)
    e_rem = jnp.exp(L_end - L)
    At, Bt, Kt, Rt = a_vec * jnp.exp(L - lw), b_vec * e_neg, k * e_neg, r * jnp.exp(L)
    Bg, Kg = b_vec * e_rem, k * e_rem
    decay_end = jnp.exp(L_end)

    ri = lax.broadcasted_iota(jnp.int32, (C, C), 0)
    cj = lax.broadcasted_iota(jnp.int32, (C, C), 1)
    strict, incl = ri > cj, ri >= cj
    eye_c = (ri == cj).astype(F32)
    eye_n = (lax.broadcasted_iota(jnp.int32, (N, N), 0) == lax.broadcasted_iota(jnp.int32, (N, N), 1)).astype(F32)
    H = RWKV_HEADS
    hs = [slice(h * N, (h + 1) * N) for h in range(H)]
    ar = [jnp.concatenate([At[:, s], Rt[:, s]], axis=0) for s in hs]
    zb = [_dot_nt(ar[h], Bt[:, hs[h]]) for h in range(H)]
    zk = [_dot_nt(ar[h], Kt[:, hs[h]]) for h in range(H)]
    n_mat = [jnp.where(strict, z[0:C], 0.0) for z in zb]
    m_mat = [jnp.where(strict, z[0:C], 0.0) for z in zk]
    mv = [_dot(m_mat[h], v[:, hs[h]]) for h in range(H)]
    t_inv, n_pow = [eye_c + n for n in n_mat], n_mat
    for _ in range(int(math.log2(C)) - 1):
        n_pow = [_dot(n, n) for n in n_pow]
        t_inv = [t + _dot(t, n) for t, n in zip(t_inv, n_pow)]
    ta = [_dot(t_inv[h], At[:, hs[h]]) for h in range(H)]
    g0 = [_dot(t_inv[h], mv[h]) for h in range(H)]
    pb = [jnp.concatenate([jnp.where(incl, zb[h][C:], 0.0), Bg[:, hs[h]].T], axis=0) for h in range(H)]
    pkk = [jnp.concatenate([jnp.where(incl, zk[h][C:], 0.0), Kg[:, hs[h]].T], axis=0) for h in range(H)]
    x_ta = [_dot(pb[h], ta[h]) for h in range(H)]
    x_g0 = [_dot(pb[h], g0[h]) for h in range(H)]
    x_v = [_dot(pkk[h], v[:, hs[h]]) for h in range(H)]
    for h in range(H):
        q_ref[0, :, hs[h]] = Rt[:, hs[h]] + x_ta[h][0:C]
        y0_ref[0, :, hs[h]] = x_g0[h][0:C] + x_v[h][0:C]
        a_ref[0, 0, h] = x_ta[h][C:] + eye_n * decay_end[:, hs[h]]
        d_ref[0, 0, h] = x_g0[h][C:] + x_v[h][C:]


def _rwkv_chunks(rw, mu, w0, w_up, a0, a_up, g_up, k_k, k_a, r_k):
    B, T, cols = rw.shape
    C, W, H, N = RWKV_CHUNK, RWKV_WIDTH, RWKV_HEADS, HEAD_DIM
    nc = T // C
    seg = jnp.asarray(np.kron(np.eye(H), np.ones((N, N))), BF16)
    tri = jnp.asarray(np.tril(np.ones((C, C))), BF16)
    row = lambda z: z.reshape(1, -1)
    const = lambda b, c: (0, 0)
    vec = pl.BlockSpec((1, W), const)
    tok = pl.BlockSpec((1, C, W), lambda b, c: (b, c, 0))
    mat = pl.BlockSpec((1, 1, H, N, N), lambda b, c: (b, c, 0, 0, 0))
    return pl.pallas_call(
        _rwkv_chunk_kernel,
        grid=(B, nc),
        in_specs=[pl.BlockSpec((1, C, cols), lambda b, c: (b, c, 0)),
                  pl.BlockSpec((1, SUBLANES, cols), lambda b, c: (b, jnp.maximum(c * (C // SUBLANES) - 1, 0), 0)),
                  pl.BlockSpec((1, cols), const), vec,
                  pl.BlockSpec((DECAY_LORA, W), const), vec,
                  pl.BlockSpec((AAA_LORA, W), const),
                  pl.BlockSpec((GATE_LORA, W), const), vec, vec, vec,
                  pl.BlockSpec((W, W), const), pl.BlockSpec((C, C), const)],
        out_specs=[tok, tok, mat, mat, tok, tok],
        out_shape=[jax.ShapeDtypeStruct((B, T, W), F32), jax.ShapeDtypeStruct((B, T, W), F32),
                   jax.ShapeDtypeStruct((B, nc, H, N, N), F32), jax.ShapeDtypeStruct((B, nc, H, N, N), F32),
                   jax.ShapeDtypeStruct((B, T, W), F32), jax.ShapeDtypeStruct((B, T, W), F32)],
        compiler_params=_params("parallel", "parallel"), name="rwkv_chunks",
    )(rw, rw, row(mu), row(w0), w_up.astype(BF16), row(a0), a_up.astype(BF16), g_up.astype(BF16),
      row(k_k), row(k_a), row(r_k), seg, tri)


def _rwkv_scan_kernel(a_ref, d_ref, q_ref, y0_ref, g_ref, bonus_ref, lw_ref, lb_ref, o_ref, h_sc):
    N = HEAD_DIM
    @pl.when(pl.program_id(0) == 0)
    def _():
        h_sc[...] = jnp.zeros(h_sc.shape, F32)

    for b in range(q_ref.shape[0]):
        for h in range(RWKV_HEADS):
            hs = slice(h * N, (h + 1) * N)
            state = h_sc[b, h]
            y = _dot(q_ref[b, :, hs], state, precision=HIGHEST) + y0_ref[b, :, hs]
            h_sc[b, h] = _dot(a_ref[b, 0, h], state, precision=HIGHEST) + d_ref[b, 0, h]
            mean = jnp.mean(y, axis=-1, keepdims=True)
            var = jnp.mean(jnp.square(y - mean), axis=-1, keepdims=True)
            yn = (y - mean) * lax.rsqrt(var + LNX_EPS)
            yn = yn * lw_ref[:, hs] + lb_ref[:, hs]
            o_ref[b, :, hs] = (yn + bonus_ref[b, :, hs]) * g_ref[b, :, hs]


def _rwkv_scan(A, D, Q, Y0, g, bonus, lnx_w, lnx_b):
    B, nc, H, N, _ = A.shape
    T, W, C = Q.shape[1], Q.shape[2], RWKV_CHUNK
    tok = pl.BlockSpec((B, C, W), lambda c: (0, c, 0))
    mat = pl.BlockSpec((B, 1, H, N, N), lambda c: (0, c, 0, 0, 0))
    vec = pl.BlockSpec((1, W), lambda c: (0, 0))
    return pl.pallas_call(
        _rwkv_scan_kernel,
        grid=(nc,),
        in_specs=[mat, mat, tok, tok, tok, tok, vec, vec],
        out_specs=tok,
        out_shape=jax.ShapeDtypeStruct((B, T, W), F32),
        scratch_shapes=[pltpu.VMEM((B, H, N, N), F32)],
        compiler_params=_params("arbitrary"), name="rwkv_scan",
    )(A, D, Q, Y0, g, bonus, lnx_w.reshape(1, W), lnx_b.reshape(1, W))


def _mix_xattn_kernel(x_ref, on_ref, or_ref, wo1_ref, wo2_ref, gx_ref, wq_ref, k_ref, v_ref, wo_ref, o_ref):
    x1 = x_ref[0] + _dot(on_ref[0].astype(BF16), wo1_ref[...]) + _dot(or_ref[0].astype(BF16), wo2_ref[...])
    q = _dot(_rms(x1, gx_ref[...]).astype(BF16), wq_ref[...])
    dh = q.shape[-1] // XATTN_HEADS
    qbf = (q * (dh ** -0.5)).astype(BF16)
    outs = []
    for h in range(XATTN_HEADS):
        hs = slice(h * dh, (h + 1) * dh)
        s = _dot_nt(qbf[:, hs], k_ref[0, :, hs])
        e = jnp.exp(s - jnp.max(s, axis=-1, keepdims=True))
        p = e / jnp.sum(e, axis=-1, keepdims=True)
        outs.append(_dot(p.astype(BF16), v_ref[0, :, hs]))
    o = jnp.concatenate(outs, axis=1).astype(BF16)
    o_ref[0] = x1 + _dot(o, wo_ref[...])


def _mix_xattn(x, o_nsa, o_rwkv, w_out, norm_x_g, w_q, mem_k, mem_v, w_o):
    B, T, D = x.shape
    M = mem_k.shape[1]
    tm = min(ROW_TILE, T)
    wo1, wo2 = w_out[:NSA_WIDTH].astype(BF16), w_out[NSA_WIDTH:].astype(BF16)
    const = lambda b, i: (0, 0)
    tile = lambda w: pl.BlockSpec((1, tm, w), lambda b, i: (b, i, 0))
    return pl.pallas_call(
        _mix_xattn_kernel,
        grid=(B, T // tm),
        in_specs=[tile(D), tile(NSA_WIDTH), tile(RWKV_WIDTH),
                  pl.BlockSpec(wo1.shape, const), pl.BlockSpec(wo2.shape, const),
                  pl.BlockSpec((1, D), const), pl.BlockSpec((D, D), const),
                  pl.BlockSpec((1, M, D), lambda b, i: (b, 0, 0)), pl.BlockSpec((1, M, D), lambda b, i: (b, 0, 0)),
                  pl.BlockSpec((D, D), const)],
        out_specs=tile(D),
        out_shape=jax.ShapeDtypeStruct((B, T, D), F32),
        compiler_params=_params("parallel", "parallel"), name="mix_xattn",
    )(x, o_nsa, o_rwkv, wo1, wo2, norm_x_g.reshape(1, D), w_q.astype(BF16), mem_k, mem_v, w_o.astype(BF16))


def _ffn_kernel(x_ref, g_ref, wg_ref, wu_ref, wd_ref, gf_ref, o_ref, *, final_norm):
    x = x_ref[...]
    h = _rms(x, g_ref[...]).astype(BF16)
    act = (jax.nn.silu(_dot(h, wg_ref[...])) * _dot(h, wu_ref[...])).astype(BF16)
    y = x + _dot(act, wd_ref[...])
    o_ref[...] = _rms(y, gf_ref[...]) if final_norm else y


def _ffn(x, norm_g, w_gate, w_up, w_down, final_g, final_norm):
    R, D = x.shape
    F = w_gate.shape[1]
    tm = min(FFN_ROW_TILE, R)
    const = lambda i: (0, 0)
    resident = lambda shape: pl.BlockSpec(shape, const, pipeline_mode=pl.Buffered(1))
    return pl.pallas_call(
        functools.partial(_ffn_kernel, final_norm=final_norm),
        grid=(R // tm,),
        in_specs=[pl.BlockSpec((tm, D), lambda i: (i, 0)), pl.BlockSpec((1, D), const),
                  resident((D, F)), resident((D, F)), resident((F, D)), pl.BlockSpec((1, D), const)],
        out_specs=pl.BlockSpec((tm, D), lambda i: (i, 0)),
        out_shape=jax.ShapeDtypeStruct((R, D), F32),
        compiler_params=_params("parallel"), name="ffn",
    )(x, norm_g.reshape(1, D), w_gate.astype(BF16), w_up.astype(BF16), w_down.astype(BF16),
      final_g.reshape(1, D))


def _overlap_matrix(n_cmp_pad, n_sel):
    c = np.arange(n_cmp_pad)[:, None] * CMP_STRIDE
    s = np.arange(n_sel)[None, :] * SEL_BLOCK
    return ((c <= s + SEL_BLOCK - 1) & (c + CMP_LEN - 1 >= s)).astype(np.float32)


def _layer(x, mem, rel_bias, final_g, is_last, norm_mix_g, w_in, nsa_gate_b, cmp_pe_k, cmp_pe_v,
           cmp_k_w1, cmp_k_b1, cmp_k_w2, cmp_v_w1, cmp_v_b1, cmp_v_w2,
           rwkv_mu, rwkv_w0, rwkv_w_up, rwkv_a0, rwkv_a_up, rwkv_g_up,
           rwkv_k_k, rwkv_k_a, rwkv_r_k, rwkv_lnx_w, rwkv_lnx_b, w_out,
           norm_x_g, norm_mem_g, w_q_x, w_kv_x, w_o_x, norm_ffn_g, w_gate, w_up, w_down):
    B, T, D = x.shape
    Hkv, G, dh = NSA_KV_HEADS, NSA_GROUP, HEAD_DIM
    n_gate = 3 * NSA_HEADS
    kv0 = NSA_WIDTH
    g0 = kv0 + 6 * KV_WIDTH
    w_q, w_kv, w_rw = w_in[:, :kv0], w_in[:, kv0:g0], w_in[:, NSA_COLS:]
    w_g = jnp.pad(w_in[:, g0:NSA_COLS], ((0, 0), (0, LANES - n_gate)))
    b_g = jnp.pad(nsa_gate_b, (0, LANES - n_gate))
    q, kv, gl, rw = _norm_matmul(
        x.reshape(B * T, D), norm_mix_g, [w.astype(BF16) for w in (w_q, w_kv, w_g, w_rw)],
        [None, None, b_g, None], [F32, F32, F32, F32], ROW_TILE)

    kv = kv.reshape(B, T, 6, Hkv, dh).transpose(2, 0, 3, 1, 4)
    n16 = T // CMP_STRIDE
    x16 = lambda a: a.reshape(B, Hkv, n16, CMP_STRIDE * dh)
    kc = _compress(x16(kv[0]), cmp_pe_k, cmp_k_w1, cmp_k_b1, cmp_k_w2, False)
    vcT = _compress(x16(kv[1]), cmp_pe_v, cmp_v_w1, cmp_v_b1, cmp_v_w2, True)
    ones_rows = (np.arange(BF16_ROWS)[:, None] == 0).astype(np.float32)

    def chunkT(a, w):
        aT = a.astype(BF16).reshape(B, Hkv, T // w, w, dh).transpose(0, 1, 2, 4, 3)
        extra = jnp.broadcast_to(jnp.asarray(ones_rows, BF16), (B, Hkv, T // w, BF16_ROWS, w))
        return jnp.concatenate([aT, extra], axis=3)
    gates = gl[:, :n_gate].reshape(B, T // Q_BLOCK, Q_BLOCK, Hkv, G, 3).transpose(0, 3, 1, 5, 4, 2)
    gates = gates.reshape(B, Hkv, T // Q_BLOCK, 3, G * Q_BLOCK)
    bias_tiles, cmp_bias = _bias_tiles(rel_bias, n16)
    blk_in_step = (np.arange(T) // SEL_BLOCK) % SEL_STEP_BLOCKS
    onehot = jnp.asarray(blk_in_step[:, None] == np.arange(LANES - dh)[None, :], BF16)
    ks_aug = jnp.concatenate([kv[2].astype(BF16), jnp.broadcast_to(onehot, (B, Hkv, T, LANES - dh))], axis=-1)
    o_nsa = _nsa_attention(q.reshape(B, T, NSA_WIDTH), kc, vcT, ks_aug,
                           chunkT(kv[3], SEL_STAGE_TILES * KEY_CHUNK),
                           kv[4].astype(BF16), chunkT(kv[5], KEY_CHUNK), gates,
                           jnp.asarray(_overlap_matrix(n16, T // SEL_BLOCK).T, BF16), bias_tiles, cmp_bias)

    Q, Y0, A, Dm, g, bonus = _rwkv_chunks(rw.reshape(B, T, RWKV_COLS), rwkv_mu, rwkv_w0, rwkv_w_up, rwkv_a0,
                                          rwkv_a_up, rwkv_g_up, rwkv_k_k, rwkv_k_a, rwkv_r_k.reshape(-1))
    o_rwkv = _rwkv_scan(A, Dm, Q, Y0, g, bonus, rwkv_lnx_w, rwkv_lnx_b)

    M = mem.shape[1]
    (kv_mem,) = _norm_matmul(mem.reshape(B * M, D), norm_mem_g, [w_kv_x.astype(BF16)], [None], [BF16], ROW_TILE)
    kv_mem = kv_mem.reshape(B, M, 2 * D)
    x = _mix_xattn(x, o_nsa, o_rwkv, w_out, norm_x_g, w_q_x, kv_mem[..., :D], kv_mem[..., D:], w_o_x)
    x = _ffn(x.reshape(B * T, D), norm_ffn_g, w_gate, w_up, w_down, final_g, is_last)
    return x.reshape(B, T, D)


def kernel(x, mem, rel_bias, norm_f_g, norm_mix_g, w_in, nsa_gate_b, cmp_pe_k, cmp_pe_v, cmp_k_w1, cmp_k_b1, cmp_k_w2, cmp_v_w1, cmp_v_b1, cmp_v_w2, rwkv_mu, rwkv_w0, rwkv_w_up, rwkv_a0, rwkv_a_up, rwkv_g_up, rwkv_k_k, rwkv_k_a, rwkv_r_k, rwkv_lnx_w, rwkv_lnx_b, w_out, norm_x_g, norm_mem_g, w_q_x, w_kv_x, w_o_x, norm_ffn_g, w_gate, w_up, w_down):
    stacked = (norm_mix_g, w_in, nsa_gate_b, cmp_pe_k, cmp_pe_v, cmp_k_w1, cmp_k_b1, cmp_k_w2, cmp_v_w1,
               cmp_v_b1, cmp_v_w2, rwkv_mu, rwkv_w0, rwkv_w_up, rwkv_a0, rwkv_a_up, rwkv_g_up, rwkv_k_k,
               rwkv_k_a, rwkv_r_k, rwkv_lnx_w, rwkv_lnx_b, w_out, norm_x_g, norm_mem_g, w_q_x, w_kv_x, w_o_x,
               norm_ffn_g, w_gate, w_up, w_down)
    depth = w_in.shape[0]
    for l in range(depth):
        x = _layer(x, mem, rel_bias, norm_f_g, l == depth - 1, *[p[l] for p in stacked])
    return x
```

```python
import functools
import math

import numpy as np
import jax
import jax.numpy as jnp
from jax import lax
from jax.experimental import pallas as pl
from jax.experimental.pallas import tpu as pltpu

F32 = jnp.float32
BF16 = jnp.bfloat16
HIGHEST = lax.Precision.HIGHEST

LANES = 128
SUBLANES = 8
BF16_ROWS = 16
VMEM_LIMIT_BYTES = 56 * 1024 * 1024

HEAD_DIM = 64
NSA_HEADS = 8
NSA_KV_HEADS = 2
NSA_GROUP = NSA_HEADS // NSA_KV_HEADS
NSA_WIDTH = NSA_HEADS * HEAD_DIM
KV_WIDTH = NSA_KV_HEADS * HEAD_DIM
RWKV_HEADS = 8
RWKV_WIDTH = RWKV_HEADS * HEAD_DIM
CMP_LEN = 32
CMP_STRIDE = 16
SEL_BLOCK = 64
SEL_SHIFT = 6
SEL_TOP = 16
WINDOW = 512
Q_BLOCK = 128
DECAY_LORA = 64
AAA_LORA = 64
GATE_LORA = 128
N_BUCKETS = 32
MAX_DISTANCE = 2048
XATTN_HEADS = 4
RMS_EPS = 1e-6
LNX_EPS = 64e-5
FORCE_SCORE = 1e4
NEG_SCORE = -1e9
MASK_SCORE = -1e30
LOG2E = math.log2(math.e)
RWKV_COLS = 3 * RWKV_WIDTH + DECAY_LORA + AAA_LORA + GATE_LORA
NSA_COLS = NSA_WIDTH + 6 * KV_WIDTH + 3 * NSA_HEADS

KEY_CHUNK = 128
RWKV_CHUNK = 64
ROW_TILE = 512
FFN_ROW_TILE = 256


def _t5_thresholds():
    d = np.arange(0, 2 * MAX_DISTANCE, dtype=np.int64)
    max_exact = N_BUCKETS // 2
    nf = np.maximum(d, 1).astype(np.float32)
    large = max_exact + (np.log(nf / np.float32(max_exact)) / np.float32(math.log(MAX_DISTANCE / max_exact))
                         * np.float32(N_BUCKETS - max_exact)).astype(np.int32)
    bucket = np.where(d < max_exact, d, np.minimum(large, N_BUCKETS - 1))
    return [int(np.argmax(bucket >= k)) for k in range(N_BUCKETS)]


T5_THRESHOLDS = _t5_thresholds()
N_BIAS_TILES = -(-(T5_THRESHOLDS[-1] + KEY_CHUNK) // KEY_CHUNK) + 1
TILE_MASKED = N_BIAS_TILES
TILE_WINDOW_EDGE = N_BIAS_TILES + 1
N_ALL_TILES = N_BIAS_TILES + 2
SEL_STEP_BLOCKS = 16
SEL_STAGE_TILES = 4
SEL_GROUP_STAGES = SEL_STEP_BLOCKS * SEL_BLOCK // (SEL_STAGE_TILES * KEY_CHUNK)
V_ROWS = HEAD_DIM + BF16_ROWS
GATE_ROWS = 16


def _params(*semantics):
    return pltpu.CompilerParams(dimension_semantics=semantics, vmem_limit_bytes=VMEM_LIMIT_BYTES)


def _rms(x, g):
    return x * lax.rsqrt(jnp.mean(x * x, axis=-1, keepdims=True) + RMS_EPS) * g


def _dot(a, b, **kw):
    return jnp.dot(a, b, preferred_element_type=F32, **kw)


def _split3(x):
    hi = x.astype(BF16)
    r1 = x - hi.astype(F32)
    mid = r1.astype(BF16)
    lo = (r1 - mid.astype(F32)).astype(BF16)
    return hi, mid, lo


def _dot_f32_lhs(x, w01):
    w = w01.astype(BF16)
    hi, mid, lo = _split3(x)
    return _dot(hi, w) + (_dot(mid, w) + _dot(lo, w))


def _dot_f32_rhs(w01, x):
    w = w01.astype(BF16)
    hi, mid, lo = _split3(x)
    return _dot(w, hi) + (_dot(w, mid) + _dot(w, lo))


def _dot_nt(a, b, **kw):
    return lax.dot_general(a, b, (((1,), (1,)), ((), ())), preferred_element_type=F32, **kw)


def _norm_matmul_kernel(x_ref, g_ref, *refs, nseg, bias_flags):
    nb = sum(bias_flags)
    w_refs, b_refs, o_refs = refs[:nseg], refs[nseg:nseg + nb], refs[nseg + nb:]
    xn = _rms(x_ref[...], g_ref[...]).astype(BF16)
    bi = 0
    for s in range(nseg):
        y = _dot(xn, w_refs[s][...])
        if bias_flags[s]:
            y = y + b_refs[bi][...]
            bi += 1
        o_refs[s][...] = y.astype(o_refs[s].dtype)


def _norm_matmul(x, g, weights, biases, out_dtypes, row_tile):
    R, D = x.shape
    tm = min(row_tile, R)
    assert R % tm == 0
    nseg = len(weights)
    bias_flags = tuple(b is not None for b in biases)
    const = lambda i: (0, 0)
    in_specs = [pl.BlockSpec((tm, D), lambda i: (i, 0)), pl.BlockSpec((1, D), const)]
    in_specs += [pl.BlockSpec(w.shape, const) for w in weights]
    in_specs += [pl.BlockSpec((1, b.shape[-1]), const) for b in biases if b is not None]
    out_specs = [pl.BlockSpec((tm, w.shape[1]), lambda i: (i, 0)) for w in weights]
    out_shape = [jax.ShapeDtypeStruct((R, w.shape[1]), dt) for w, dt in zip(weights, out_dtypes)]
    return pl.pallas_call(
        functools.partial(_norm_matmul_kernel, nseg=nseg, bias_flags=bias_flags),
        grid=(R // tm,), in_specs=in_specs, out_specs=out_specs, out_shape=out_shape,
        compiler_params=_params("parallel"), name="norm_matmul",
    )(x, g.reshape(1, D), *weights, *[b.reshape(1, -1) for b in biases if b is not None])


def _proj_in_kernel(x_ref, g_ref, wq_ref, wc_ref, wk_ref, wvT_ref, wgT_ref, bg_ref, wr_ref,
                    q_ref, kvc_ref, kaug_ref, vsT_ref, vwT_ref, gate_ref, rw_ref, *, seq_len):
    tm = x_ref.shape[0]
    xn = _rms(x_ref[...], g_ref[...]).astype(BF16)
    q_ref[...] = _dot(xn, wq_ref[...])
    kvc_ref[...] = _dot(xn, wc_ref[...])
    rw_ref[...] = _dot(xn, wr_ref[...])
    k_all = _dot(xn, wk_ref[...])
    tok = lax.rem(pl.program_id(0) * tm, seq_len) + lax.broadcasted_iota(jnp.int32, k_all.shape, 0)
    lane = lax.broadcasted_iota(jnp.int32, k_all.shape, 1)
    blk = jnp.bitwise_and(jnp.right_shift(tok, SEL_SHIFT), SEL_STEP_BLOCKS - 1)
    hot = (jnp.bitwise_and(lane, LANES - 1) == HEAD_DIM + blk) & (lane < NSA_KV_HEADS * LANES)
    kaug_ref[...] = jnp.where(hot, 1.0, k_all).astype(BF16)
    vT = _dot_nt(wvT_ref[...], xn)
    row = lax.broadcasted_iota(jnp.int32, vT.shape, 0)
    ones_row = row == HEAD_DIM
    for grp in range(1, 2 * NSA_KV_HEADS):
        ones_row = ones_row | (row == grp * V_ROWS + HEAD_DIM)
    vT = jnp.where(ones_row, 1.0, vT).astype(BF16)
    half = NSA_KV_HEADS * V_ROWS
    vsT_ref[0, 0] = vT[0:half]
    for c in range(tm // KEY_CHUNK):
        vwT_ref[0, c] = vT[half:, c * KEY_CHUNK:(c + 1) * KEY_CHUNK]
    gT = _dot_nt(wgT_ref[...], xn) + bg_ref[...]
    for j in range(tm // Q_BLOCK):
        for h in range(NSA_KV_HEADS):
            gate_ref[0, j, h] = gT[h * GATE_ROWS:(h + 1) * GATE_ROWS, j * Q_BLOCK:(j + 1) * Q_BLOCK]


def _proj_in(x, norm_g, w_in, gate_b):
    B, T, D = x.shape
    Hkv, G, dh = NSA_KV_HEADS, NSA_GROUP, HEAD_DIM
    tm = SEL_STAGE_TILES * KEY_CHUNK
    assert T % tm == 0 and tm % Q_BLOCK == 0
    kv0 = NSA_WIDTH
    g0 = kv0 + 6 * KV_WIDTH
    stream = lambda s: w_in[:, kv0 + s * KV_WIDTH:kv0 + (s + 1) * KV_WIDTH].reshape(D, Hkv, dh)
    pad_cols = lambda w: jnp.pad(w, ((0, 0), (0, 0), (0, LANES - dh))).reshape(D, Hkv * LANES)
    pad_rows = lambda w: jnp.pad(w.transpose(1, 2, 0), ((0, 0), (0, V_ROWS - dh), (0, 0))).reshape(Hkv * V_ROWS, D)
    w_k = jnp.concatenate([pad_cols(stream(2)), pad_cols(stream(4))], axis=1)
    w_vT = jnp.concatenate([pad_rows(stream(3)), pad_rows(stream(5))], axis=0)
    reorder = lambda a: a.reshape(-1, Hkv, G, 3).transpose(1, 3, 2, 0).reshape(Hkv, 3 * G, -1)
    pad_gate = lambda a: jnp.pad(a, ((0, 0), (0, GATE_ROWS - 3 * G), (0, 0))).reshape(Hkv * GATE_ROWS, -1)
    w_gT = pad_gate(reorder(w_in[:, g0:NSA_COLS]))
    b_g = pad_gate(reorder(gate_b.reshape(1, -1)))
    weights = [w_in[:, :kv0], w_in[:, kv0:kv0 + 2 * KV_WIDTH], w_k, w_vT, w_gT]
    weights = [w.astype(BF16) for w in weights] + [b_g, w_in[:, NSA_COLS:].astype(BF16)]
    nt = T // tm
    rows = lambda n: pl.BlockSpec((tm, n), lambda i: (i, 0))
    const = lambda i: (0, 0)
    return pl.pallas_call(
        functools.partial(_proj_in_kernel, seq_len=T),
        grid=(B * nt,),
        in_specs=[rows(D), pl.BlockSpec((1, D), const)] + [pl.BlockSpec(w.shape, const) for w in weights],
        out_specs=[rows(NSA_WIDTH), rows(2 * KV_WIDTH), rows(2 * Hkv * LANES),
                   pl.BlockSpec((1, 1, Hkv * V_ROWS, tm), lambda i: (i // nt, i % nt, 0, 0)),
                   pl.BlockSpec((1, tm // KEY_CHUNK, Hkv * V_ROWS, KEY_CHUNK), lambda i: (i // nt, i % nt, 0, 0)),
                   pl.BlockSpec((1, tm // Q_BLOCK, Hkv, GATE_ROWS, Q_BLOCK), lambda i: (i // nt, i % nt, 0, 0, 0)),
                   rows(RWKV_COLS)],
        out_shape=[jax.ShapeDtypeStruct((B * T, NSA_WIDTH), F32),
                   jax.ShapeDtypeStruct((B * T, 2 * KV_WIDTH), F32),
                   jax.ShapeDtypeStruct((B * T, 2 * Hkv * LANES), BF16),
                   jax.ShapeDtypeStruct((B, nt, Hkv * V_ROWS, tm), BF16),
                   jax.ShapeDtypeStruct((B, T // KEY_CHUNK, Hkv * V_ROWS, KEY_CHUNK), BF16),
                   jax.ShapeDtypeStruct((B, T // Q_BLOCK, Hkv, GATE_ROWS, Q_BLOCK), F32),
                   jax.ShapeDtypeStruct((B * T, RWKV_COLS), F32)],
        compiler_params=_params("parallel"), name="proj_in",
    )(x.reshape(B * T, D), norm_g.reshape(1, D), *weights)


def _compress_kernel(x_ref, pe_ref, w1_ref, b1_ref, w2_ref, o_ref, *, transpose_out):
    n16 = x_ref.shape[1] // CMP_STRIDE
    hidden = w1_ref.shape[2] // NSA_KV_HEADS
    lo = jnp.zeros((n16, w1_ref.shape[2]), F32)
    hi = jnp.zeros((n16, w1_ref.shape[2]), F32)
    for l in range(CMP_STRIDE):
        rows = x_ref[0, pl.ds(l, n16, stride=CMP_STRIDE), :]
        lo = lo + _dot((rows + pe_ref[l:l + 1, :]).astype(BF16), w1_ref[l])
        hi = hi + _dot((rows + pe_ref[CMP_STRIDE + l:CMP_STRIDE + l + 1, :]).astype(BF16), w1_ref[CMP_STRIDE + l])
    h = lo + pltpu.roll(hi, n16 - 1, axis=0) + b1_ref[...]
    h = jax.nn.gelu(h).astype(BF16)
    for hkv in range(NSA_KV_HEADS):
        hh = h[:, hkv * hidden:(hkv + 1) * hidden]
        if transpose_out:
            o_ref[0, hkv] = _dot_nt(w2_ref[...], hh).astype(o_ref.dtype)
        else:
            o_ref[0, hkv] = _dot(hh, w2_ref[...]).astype(o_ref.dtype)


def _compress(kvc, stream, pe, w1, b1, w2, transpose_out):
    B, T, _ = kvc.shape
    H, dh = NSA_KV_HEADS, HEAD_DIM
    n16 = T // CMP_STRIDE
    hidden = w1.shape[1]
    eye = jnp.eye(H, dtype=w1.dtype)
    w1_bd = jnp.einsum('ldn,hg->lhdgn', w1.reshape(CMP_LEN, dh, hidden), eye).reshape(CMP_LEN, H * dh, H * hidden)
    w2b = (w2.T if transpose_out else w2).astype(BF16)
    oshape = (B, H, dh, n16) if transpose_out else (B, H, n16, dh)
    return pl.pallas_call(
        functools.partial(_compress_kernel, transpose_out=transpose_out),
        grid=(B,),
        in_specs=[pl.BlockSpec((1, T, H * dh), lambda b: (b, 0, stream)),
                  pl.BlockSpec((CMP_LEN, H * dh), lambda b: (0, 0)),
                  pl.BlockSpec(w1_bd.shape, lambda b: (0, 0, 0)),
                  pl.BlockSpec((1, H * hidden), lambda b: (0, 0)),
                  pl.BlockSpec(w2b.shape, lambda b: (0, 0))],
        out_specs=pl.BlockSpec((1,) + oshape[1:], lambda b: (b, 0, 0, 0)),
        out_shape=jax.ShapeDtypeStruct(oshape, BF16),
        compiler_params=_params("parallel"), name="nsa_compress",
    )(kvc, jnp.tile(pe, (1, H)), w1_bd.astype(BF16), jnp.tile(b1.reshape(1, hidden), (1, H)), w2b)


def _bias_of_distance(tab_ref, h, d):
    val = jnp.full(d.shape, tab_ref[h, 0], F32)
    for k in range(1, N_BUCKETS):
        val = jnp.where(d >= T5_THRESHOLDS[k], tab_ref[h, k], val)
    return val * LOG2E


def _bias_tiles_kernel(tab_ref, bt_ref, cb_ref, *, n_cmp_pad):
    hkv = pl.program_id(0)
    j = lax.broadcasted_iota(jnp.int32, (KEY_CHUNK, Q_BLOCK), 0)
    i = lax.broadcasted_iota(jnp.int32, (KEY_CHUNK, Q_BLOCK), 1)
    r2 = lax.broadcasted_iota(jnp.int32, (2 * n_cmp_pad, Q_BLOCK), 0)
    i2 = lax.broadcasted_iota(jnp.int32, (2 * n_cmp_pad, Q_BLOCK), 1)
    l2 = r2 - (n_cmp_pad - KEY_CHUNK)
    d2 = i2 - CMP_STRIDE * l2 + (CMP_STRIDE * KEY_CHUNK - Q_BLOCK - (CMP_LEN - 1))
    d2 = jnp.where((l2 >= 0) & (l2 < KEY_CHUNK), d2, 2 * MAX_DISTANCE)
    for g in range(NSA_GROUP):
        h = hkv * NSA_GROUP + g
        lanes = slice(g * Q_BLOCK, (g + 1) * Q_BLOCK)
        for m in range(N_BIAS_TILES):
            tile = _bias_of_distance(tab_ref, h, m * KEY_CHUNK + i - j)
            if m == 0:
                tile = jnp.where(j <= i, tile, MASK_SCORE)
            bt_ref[0, m, :, lanes] = tile
        bt_ref[0, TILE_MASKED, :, lanes] = jnp.full((KEY_CHUNK, Q_BLOCK), MASK_SCORE, F32)
        edge = _bias_of_distance(tab_ref, h, WINDOW + i - j)
        bt_ref[0, TILE_WINDOW_EDGE, :, lanes] = jnp.where(j > i, edge, MASK_SCORE)
        cb_ref[0, :, lanes] = _bias_of_distance(tab_ref, h, d2)


def _bias_tiles(rel_bias, n_cmp_pad):
    assert CMP_STRIDE * KEY_CHUNK - Q_BLOCK - (CMP_LEN - 1) >= T5_THRESHOLDS[-1]
    GQ = NSA_GROUP * Q_BLOCK
    return pl.pallas_call(
        functools.partial(_bias_tiles_kernel, n_cmp_pad=n_cmp_pad),
        grid=(NSA_KV_HEADS,),
        in_specs=[pl.BlockSpec(memory_space=pltpu.SMEM)],
        out_specs=[pl.BlockSpec((1, N_ALL_TILES, KEY_CHUNK, GQ), lambda h: (h, 0, 0, 0)),
                   pl.BlockSpec((1, 2 * n_cmp_pad, GQ), lambda h: (h, 0, 0))],
        out_shape=[jax.ShapeDtypeStruct((NSA_KV_HEADS, N_ALL_TILES, KEY_CHUNK, GQ), F32),
                   jax.ShapeDtypeStruct((NSA_KV_HEADS, 2 * n_cmp_pad, GQ), F32)],
        compiler_params=_params("parallel"), name="t5_bias_tiles",
    )(rel_bias.T)


def _nsa_kernel(q_ref, kc_ref, vcT_ref, ks_ref, vsT_ref, kw_ref, vwT_ref, gate_ref, ovT_ref, bt_ref, cb_ref,
                o_ref, acc_sc, qaug_sc, seladd_sc, s0_sc, s1_sc, p0_sc, p1_sc, *, n_sel, n_cmp_pad):
    G, GQ = NSA_GROUP, NSA_GROUP * Q_BLOCK
    qb = pl.program_id(2)
    q = q_ref[0]
    q4 = jnp.concatenate([q[:, g * HEAD_DIM:(g + 1) * HEAD_DIM] for g in range(G)], axis=0)
    qT = (q4 * (HEAD_DIM ** -0.5 * LOG2E)).T.astype(BF16)
    qaug_sc[0:HEAD_DIM, :] = qT
    qaug_sc[HEAD_DIM:, :] = jnp.zeros((qaug_sc.shape[0] - HEAD_DIM, GQ), BF16)
    tile_g = lambda a: jnp.concatenate([a] * G, axis=1)
    t = qb * Q_BLOCK + lax.broadcasted_iota(jnp.int32, (1, Q_BLOCK), 1)

    start = pl.multiple_of(n_cmp_pad - (Q_BLOCK // CMP_STRIDE) * (qb + 1), SUBLANES)
    s = _dot(kc_ref[0, 0], qT) + cb_ref[0, pl.ds(start, n_cmp_pad), :]
    ci = lax.broadcasted_iota(jnp.int32, (n_cmp_pad, Q_BLOCK), 0)
    vis = tile_g(jnp.where((CMP_STRIDE * ci + (CMP_LEN - 1)) <= t, 1.0, 0.0)) > 0.5
    s = jnp.where(vis, s, MASK_SCORE)
    e = jnp.where(vis, jnp.exp2(s - jnp.max(s, axis=0, keepdims=True)), 0.0)
    p = e / jnp.maximum(jnp.sum(e, axis=0, keepdims=True), 1e-30)
    o_c = _dot(vcT_ref[0, 0], p.astype(BF16))
    psum = p[:, 0:Q_BLOCK]
    for g in range(1, G):
        psum = psum + p[:, g * Q_BLOCK:(g + 1) * Q_BLOCK]

    imp = _dot_f32_rhs(ovT_ref[...], psum)
    bj = lax.broadcasted_iota(jnp.int32, (n_sel, Q_BLOCK), 0)
    cur = jnp.right_shift(t, SEL_SHIFT)
    forced = (bj == 0) | (bj == cur) | (bj == cur - 1)
    valid = bj * SEL_BLOCK <= t
    score = jnp.where(valid, jnp.where(forced, FORCE_SCORE, imp), NEG_SCORE)
    bjf = bj.astype(F32)
    sel = jnp.zeros((n_sel, Q_BLOCK), F32)
    for _ in range(min(SEL_TOP, n_sel)):
        mx = jnp.max(score, axis=0, keepdims=True)
        first = jnp.min(jnp.where(score == mx, bjf, float(n_sel)), axis=0, keepdims=True)
        pick = bjf == first
        sel = jnp.where(pick, 1.0, sel)
        score = jnp.where(pick, -jnp.inf, score)
    seladd_sc[...] = tile_g((sel - 1.0) * (-MASK_SCORE))

    def bias_rows(first_chunk, n_chunks):
        tiles = []
        for k in range(n_chunks):
            dist = qb - (first_chunk + k)
            tiles.append(bt_ref[0, jnp.where(dist < 0, TILE_MASKED, jnp.minimum(dist, N_BIAS_TILES - 1))])
        return jnp.concatenate(tiles, axis=0)

    stage_keys = SEL_STAGE_TILES * KEY_CHUNK
    n_stages = qb // SEL_STAGE_TILES + 1
    last_stage = ks_ref.shape[1] // stage_keys - 1

    def scores(k, s_buf):
        kk = jnp.minimum(k, last_stage)
        blk0 = pl.multiple_of(kk // SEL_GROUP_STAGES * SEL_STEP_BLOCKS, SEL_STEP_BLOCKS)
        qaug_sc[HEAD_DIM:HEAD_DIM + SEL_STEP_BLOCKS, :] = seladd_sc[pl.ds(blk0, SEL_STEP_BLOCKS), :].astype(BF16)
        k0 = pl.multiple_of(kk * stage_keys, stage_keys)
        s = _dot(ks_ref[0, pl.ds(k0, stage_keys), :], qaug_sc[...]) + bias_rows(k * SEL_STAGE_TILES,
                                                                                  SEL_STAGE_TILES)
        s_buf[...] = s
        return jnp.max(s, axis=0, keepdims=True)

    def weights(m, col_max, s_buf, p_buf):
        m_new = jnp.maximum(m, col_max)
        p_buf[...] = jnp.exp2(s_buf[...] - m_new).astype(BF16)
        return m_new, jnp.exp2(m - m_new)

    def accumulate(k, alpha, p_buf):
        acc_sc[...] = alpha * acc_sc[...] + _dot(vsT_ref[0, jnp.clip(k, 0, last_stage)], p_buf[...])

    def pair(j, carry):
        m, col_max, alpha = carry
        k = 2 * j
        col_max1 = scores(k + 1, s1_sc)
        m, alpha0 = weights(m, col_max, s0_sc, p0_sc)
        accumulate(k - 1, alpha, p1_sc)
        col_max2 = scores(k + 2, s0_sc)
        m, alpha1 = weights(m, col_max1, s1_sc, p1_sc)
        accumulate(k, alpha0, p0_sc)
        return m, col_max2, alpha1

    acc_sc[...] = jnp.zeros(acc_sc.shape, F32)
    p1_sc[...] = jnp.zeros(p1_sc.shape, BF16)
    m_init = jnp.full((1, GQ), 0.1 * MASK_SCORE, F32)
    carry = (m_init, scores(0, s0_sc), jnp.ones((1, GQ), F32))
    n_pairs = (n_stages + 1) // 2
    _, _, alpha = lax.fori_loop(0, n_pairs, pair, carry)
    accumulate(2 * n_pairs - 1, alpha, p1_sc)
    o_s = acc_sc[0:HEAD_DIM, :] / jnp.maximum(acc_sc[HEAD_DIM:HEAD_DIM + 1, :], 1e-30)

    n_back = WINDOW // KEY_CHUNK
    q_win = qaug_sc[...]
    s_parts, chunks = [], []
    for back in range(n_back, -1, -1):
        kc = jnp.maximum(qb - back, 0)
        edge = TILE_WINDOW_EDGE if back == n_back else back
        tile = jnp.where(qb >= back, edge, TILE_MASKED)
        k_chunk = kw_ref[0, pl.ds(pl.multiple_of(kc * KEY_CHUNK, KEY_CHUNK), KEY_CHUNK), :]
        s_parts.append(_dot(k_chunk, q_win) + bt_ref[0, tile])
        chunks.append(kc)
    s = jnp.concatenate(s_parts, axis=0)
    p = jnp.exp2(s - jnp.max(s, axis=0, keepdims=True)).astype(BF16)
    o_w = jnp.zeros(acc_sc.shape, F32)
    for n, kc in enumerate(chunks):
        o_w = o_w + _dot(vwT_ref[0, kc], p[n * KEY_CHUNK:(n + 1) * KEY_CHUNK])
    o_w = o_w[0:HEAD_DIM] / jnp.maximum(o_w[HEAD_DIM:HEAD_DIM + 1], 1e-30)

    gt = jax.nn.sigmoid(gate_ref[0, 0, 0])
    gate = [jnp.concatenate([gt[br * G + g:br * G + g + 1, :] for g in range(G)], axis=1) for br in range(3)]
    o = gate[0] * o_c + gate[1] * o_s + gate[2] * o_w
    o_ref[0] = jnp.concatenate([o[:, g * Q_BLOCK:(g + 1) * Q_BLOCK].T for g in range(G)], axis=1)


def _nsa_attention(q, kc, vcT, kaug, vsT, vwT, gates, overlapT, bias_tiles, cmp_bias):
    B, T, _ = q.shape
    Hkv, G = NSA_KV_HEADS, NSA_GROUP
    GQ = G * Q_BLOCK
    nqb = T // Q_BLOCK
    n_cmp_pad = kc.shape[2]
    n_sel = T // SEL_BLOCK
    half_keys = SEL_STAGE_TILES * KEY_CHUNK
    assert T % (SEL_GROUP_STAGES * half_keys) == 0
    per_head = lambda b, h, i: (b, h, 0, 0)
    v_rows = V_ROWS
    chunked = lambda w: pl.BlockSpec((1, T // w, v_rows, w), lambda b, h, i: (b, 0, h, 0))
    return pl.pallas_call(
        functools.partial(_nsa_kernel, n_sel=n_sel, n_cmp_pad=n_cmp_pad),
        grid=(B, Hkv, nqb),
        in_specs=[pl.BlockSpec((1, Q_BLOCK, G * HEAD_DIM), lambda b, h, i: (b, i, h)),
                  pl.BlockSpec((1, 1, n_cmp_pad, HEAD_DIM), per_head),
                  pl.BlockSpec((1, 1, HEAD_DIM, n_cmp_pad), per_head),
                  pl.BlockSpec((1, T, LANES), lambda b, h, i: (b, 0, h)), chunked(half_keys),
                  pl.BlockSpec((1, T, LANES), lambda b, h, i: (b, 0, Hkv + h)), chunked(KEY_CHUNK),
                  pl.BlockSpec((1, 1, 1, GATE_ROWS, Q_BLOCK), lambda b, h, i: (b, i, h, 0, 0)),
                  pl.BlockSpec(overlapT.shape, lambda b, h, i: (0, 0)),
                  pl.BlockSpec((1, N_ALL_TILES, KEY_CHUNK, GQ), lambda b, h, i: (h, 0, 0, 0)),
                  pl.BlockSpec((1, 2 * n_cmp_pad, GQ), lambda b, h, i: (h, 0, 0))],
        out_specs=pl.BlockSpec((1, Q_BLOCK, G * HEAD_DIM), lambda b, h, i: (b, i, h)),
        out_shape=jax.ShapeDtypeStruct((B, T, NSA_WIDTH), F32),
        scratch_shapes=[pltpu.VMEM((v_rows, GQ), F32), pltpu.VMEM((LANES, GQ), BF16),
                        pltpu.VMEM((n_sel, GQ), F32),
                        pltpu.VMEM((half_keys, GQ), F32), pltpu.VMEM((half_keys, GQ), F32),
                        pltpu.VMEM((half_keys, GQ), BF16), pltpu.VMEM((half_keys, GQ), BF16)],
        compiler_params=_params("parallel", "parallel", "arbitrary"), name="nsa_attention",
    )(q, kc, vcT, kaug, vsT, kaug, vwT, gates, overlapT, bias_tiles, cmp_bias)


def _rwkv_chunk_kernel(rw_ref, prev_ref, mu_ref, w0_ref, wup_ref, a0_ref, aup_ref, gup_ref, kk_ref, ka_ref,
                       rk_ref, seg_ref, tri_ref, q_ref, y0_ref, a_ref, d_ref, g_ref, bonus_ref):
    C, W, N = RWKV_CHUNK, RWKV_WIDTH, HEAD_DIM
    c = pl.program_id(1)
    x = rw_ref[0]
    row = lax.broadcasted_iota(jnp.int32, (C, 1), 0)
    last_prev = jnp.where(c == 0, 0.0, prev_ref[0, SUBLANES - 1:SUBLANES, :])
    x_prev = jnp.where(row == 0, last_prev, pltpu.roll(x, 1, axis=0))
    xs = x + (x_prev - x) * mu_ref[...]
    r, k, v = xs[:, 0:W], xs[:, W:2 * W], xs[:, 2 * W:3 * W]
    o = 3 * W
    wd, ad, gd = xs[:, o:o + DECAY_LORA], xs[:, o + DECAY_LORA:o + DECAY_LORA + AAA_LORA], \
        xs[:, o + DECAY_LORA + AAA_LORA:]
    w_log = -jax.nn.softplus(-(w0_ref[...] + _dot(jnp.tanh(wd).astype(BF16), wup_ref[...]))) - 0.5
    lw = -jnp.exp(w_log)
    lr = jax.nn.sigmoid(a0_ref[...] + _dot(ad.astype(BF16), aup_ref[...]))
    g_ref[0] = _dot(jax.nn.sigmoid(gd).astype(BF16), gup_ref[...])
    kk = k * kk_ref[...]
    kk = kk * lax.rsqrt(jnp.maximum(_dot_f32_lhs(kk * kk, seg_ref[...]), 1e-24))
    k = k * (1.0 + (lr - 1.0) * ka_ref[...])
    bonus_ref[0] = _dot_f32_lhs(r * k * rk_ref[...], seg_ref[...]) * v
    a_vec, b_vec = -kk, kk * lr

    L = _dot_f32_rhs(tri_ref[...], lw)
    L_end = L[C - 1:C, :]
    e_neg = jnp.exp(-L)
    e_rem = jnp.exp(L_end - L)
    At, Bt, Kt, Rt = a_vec * jnp.exp(L - lw), b_vec * e_neg, k * e_neg, r * jnp.exp(L)
    Bg, Kg = b_vec * e_rem, k * e_rem
    decay_end = jnp.exp(L_end)

    ri = lax.broadcasted_iota(jnp.int32, (C, C), 0)
    cj = lax.broadcasted_iota(jnp.int32, (C, C), 1)
    strict, incl = ri > cj, ri >= cj
    eye_c = (ri == cj).astype(F32)
    eye_n = (lax.broadcasted_iota(jnp.int32, (N, N), 0) == lax.broadcasted_iota(jnp.int32, (N, N), 1)).astype(F32)
    H = RWKV_HEADS
    hs = [slice(h * N, (h + 1) * N) for h in range(H)]
    ar = [jnp.concatenate([At[:, s], Rt[:, s]], axis=0) for s in hs]
    zb = [_dot_nt(ar[h], Bt[:, hs[h]]) for h in range(H)]
    zk = [_dot_nt(ar[h], Kt[:, hs[h]]) for h in range(H)]
    n_mat = [jnp.where(strict, z[0:C], 0.0) for z in zb]
    m_mat = [jnp.where(strict, z[0:C], 0.0) for z in zk]
    mv = [_dot(m_mat[h], v[:, hs[h]]) for h in range(H)]
    t_inv, n_pow = [eye_c + n for n in n_mat], n_mat
    for _ in range(int(math.log2(C)) - 1):
        n_pow = [_dot(n, n) for n in n_pow]
        t_inv = [t + _dot(t, n) for t, n in zip(t_inv, n_pow)]
    ta = [_dot(t_inv[h], At[:, hs[h]]) for h in range(H)]
    g0 = [_dot(t_inv[h], mv[h]) for h in range(H)]
    pb = [jnp.concatenate([jnp.where(incl, zb[h][C:], 0.0), Bg[:, hs[h]].T], axis=0) for h in range(H)]
    pkk = [jnp.concatenate([jnp.where(incl, zk[h][C:], 0.0), Kg[:, hs[h]].T], axis=0) for h in range(H)]
    x_ta = [_dot(pb[h], ta[h]) for h in range(H)]
    x_g0 = [_dot(pb[h], g0[h]) for h in range(H)]
    x_v = [_dot(pkk[h], v[:, hs[h]]) for h in range(H)]
    for h in range(H):
        q_ref[0, :, hs[h]] = Rt[:, hs[h]] + x_ta[h][0:C]
        y0_ref[0, :, hs[h]] = x_g0[h][0:C] + x_v[h][0:C]
        a_ref[0, 0, h] = x_ta[h][C:] + eye_n * decay_end[:, hs[h]]
        d_ref[0, 0, h] = x_g0[h][C:] + x_v[h][C:]


def _rwkv_chunks(rw, mu, w0, w_up, a0, a_up, g_up, k_k, k_a, r_k):
    B, T, cols = rw.shape
    C, W, H, N = RWKV_CHUNK, RWKV_WIDTH, RWKV_HEADS, HEAD_DIM
    nc = T // C
    seg = jnp.asarray(np.kron(np.eye(H), np.ones((N, N))), BF16)
    tri = jnp.asarray(np.tril(np.ones((C, C))), BF16)
    row = lambda z: z.reshape(1, -1)
    const = lambda b, c: (0, 0)
    vec = pl.BlockSpec((1, W), const)
    tok = pl.BlockSpec((1, C, W), lambda b, c: (b, c, 0))
    mat = pl.BlockSpec((1, 1, H, N, N), lambda b, c: (b, c, 0, 0, 0))
    return pl.pallas_call(
        _rwkv_chunk_kernel,
        grid=(B, nc),
        in_specs=[pl.BlockSpec((1, C, cols), lambda b, c: (b, c, 0)),
                  pl.BlockSpec((1, SUBLANES, cols), lambda b, c: (b, jnp.maximum(c * (C // SUBLANES) - 1, 0), 0)),
                  pl.BlockSpec((1, cols), const), vec,
                  pl.BlockSpec((DECAY_LORA, W), const), vec,
                  pl.BlockSpec((AAA_LORA, W), const),
                  pl.BlockSpec((GATE_LORA, W), const), vec, vec, vec,
                  pl.BlockSpec((W, W), const), pl.BlockSpec((C, C), const)],
        out_specs=[tok, tok, mat, mat, tok, tok],
        out_shape=[jax.ShapeDtypeStruct((B, T, W), F32), jax.ShapeDtypeStruct((B, T, W), F32),
                   jax.ShapeDtypeStruct((B, nc, H, N, N), F32), jax.ShapeDtypeStruct((B, nc, H, N, N), F32),
                   jax.ShapeDtypeStruct((B, T, W), F32), jax.ShapeDtypeStruct((B, T, W), F32)],
        compiler_params=_params("parallel", "parallel"), name="rwkv_chunks",
    )(rw, rw, row(mu), row(w0), w_up.astype(BF16), row(a0), a_up.astype(BF16), g_up.astype(BF16),
      row(k_k), row(k_a), row(r_k), seg, tri)


def _rwkv_scan_kernel(a_ref, d_ref, q_ref, y0_ref, g_ref, bonus_ref, lw_ref, lb_ref, o_ref, h_sc):
    N = HEAD_DIM
    @pl.when(pl.program_id(0) == 0)
    def _():
        h_sc[...] = jnp.zeros(h_sc.shape, F32)

    for b in range(q_ref.shape[0]):
        for h in range(RWKV_HEADS):
            hs = slice(h * N, (h + 1) * N)
            state = h_sc[b, h]
            y = _dot(q_ref[b, :, hs], state, precision=HIGHEST) + y0_ref[b, :, hs]
            h_sc[b, h] = _dot(a_ref[b, 0, h], state, precision=HIGHEST) + d_ref[b, 0, h]
            mean = jnp.mean(y, axis=-1, keepdims=True)
            var = jnp.mean(jnp.square(y - mean), axis=-1, keepdims=True)
            yn = (y - mean) * lax.rsqrt(var + LNX_EPS)
            yn = yn * lw_ref[:, hs] + lb_ref[:, hs]
            o_ref[b, :, hs] = (yn + bonus_ref[b, :, hs]) * g_ref[b, :, hs]


def _rwkv_scan(A, D, Q, Y0, g, bonus, lnx_w, lnx_b):
    B, nc, H, N, _ = A.shape
    T, W, C = Q.shape[1], Q.shape[2], RWKV_CHUNK
    tok = pl.BlockSpec((B, C, W), lambda c: (0, c, 0))
    mat = pl.BlockSpec((B, 1, H, N, N), lambda c: (0, c, 0, 0, 0))
    vec = pl.BlockSpec((1, W), lambda c: (0, 0))
    return pl.pallas_call(
        _rwkv_scan_kernel,
        grid=(nc,),
        in_specs=[mat, mat, tok, tok, tok, tok, vec, vec],
        out_specs=tok,
        out_shape=jax.ShapeDtypeStruct((B, T, W), F32),
        scratch_shapes=[pltpu.VMEM((B, H, N, N), F32)],
        compiler_params=_params("arbitrary"), name="rwkv_scan",
    )(A, D, Q, Y0, g, bonus, lnx_w.reshape(1, W), lnx_b.reshape(1, W))


def _mix_xattn_kernel(x_ref, on_ref, or_ref, wo1_ref, wo2_ref, gx_ref, wq_ref, k_ref, v_ref, wo_ref, o_ref):
    x1 = x_ref[0] + _dot(on_ref[0].astype(BF16), wo1_ref[...]) + _dot(or_ref[0].astype(BF16), wo2_ref[...])
    q = _dot(_rms(x1, gx_ref[...]).astype(BF16), wq_ref[...])
    dh = q.shape[-1] // XATTN_HEADS
    qbf = (q * (dh ** -0.5)).astype(BF16)
    outs = []
    for h in range(XATTN_HEADS):
        hs = slice(h * dh, (h + 1) * dh)
        s = _dot_nt(qbf[:, hs], k_ref[0, :, hs])
        e = jnp.exp(s - jnp.max(s, axis=-1, keepdims=True))
        p = e / jnp.sum(e, axis=-1, keepdims=True)
        outs.append(_dot(p.astype(BF16), v_ref[0, :, hs]))
    o = jnp.concatenate(outs, axis=1).astype(BF16)
    o_ref[0] = x1 + _dot(o, wo_ref[...])


def _mix_xattn(x, o_nsa, o_rwkv, w_out, norm_x_g, w_q, mem_k, mem_v, w_o):
    B, T, D = x.shape
    M = mem_k.shape[1]
    tm = min(ROW_TILE, T)
    wo1, wo2 = w_out[:NSA_WIDTH].astype(BF16), w_out[NSA_WIDTH:].astype(BF16)
    const = lambda b, i: (0, 0)
    tile = lambda w: pl.BlockSpec((1, tm, w), lambda b, i: (b, i, 0))
    return pl.pallas_call(
        _mix_xattn_kernel,
        grid=(B, T // tm),
        in_specs=[tile(D), tile(NSA_WIDTH), tile(RWKV_WIDTH),
                  pl.BlockSpec(wo1.shape, const), pl.BlockSpec(wo2.shape, const),
                  pl.BlockSpec((1, D), const), pl.BlockSpec((D, D), const),
                  pl.BlockSpec((1, M, D), lambda b, i: (b, 0, 0)), pl.BlockSpec((1, M, D), lambda b, i: (b, 0, 0)),
                  pl.BlockSpec((D, D), const)],
        out_specs=tile(D),
        out_shape=jax.ShapeDtypeStruct((B, T, D), F32),
        compiler_params=_params("parallel", "parallel"), name="mix_xattn",
    )(x, o_nsa, o_rwkv, wo1, wo2, norm_x_g.reshape(1, D), w_q.astype(BF16), mem_k, mem_v, w_o.astype(BF16))


def _ffn_kernel(x_ref, g_ref, wg_ref, wu_ref, wd_ref, gf_ref, o_ref, *, final_norm):
    x = x_ref[...]
    h = _rms(x, g_ref[...]).astype(BF16)
    act = (jax.nn.silu(_dot(h, wg_ref[...])) * _dot(h, wu_ref[...])).astype(BF16)
    y = x + _dot(act, wd_ref[...])
    o_ref[...] = _rms(y, gf_ref[...]) if final_norm else y


def _ffn(x, norm_g, w_gate, w_up, w_down, final_g, final_norm):
    R, D = x.shape
    F = w_gate.shape[1]
    tm = min(FFN_ROW_TILE, R)
    const = lambda i: (0, 0)
    resident = lambda shape: pl.BlockSpec(shape, const, pipeline_mode=pl.Buffered(1))
    return pl.pallas_call(
        functools.partial(_ffn_kernel, final_norm=final_norm),
        grid=(R // tm,),
        in_specs=[pl.BlockSpec((tm, D), lambda i: (i, 0)), pl.BlockSpec((1, D), const),
                  resident((D, F)), resident((D, F)), resident((F, D)), pl.BlockSpec((1, D), const)],
        out_specs=pl.BlockSpec((tm, D), lambda i: (i, 0)),
        out_shape=jax.ShapeDtypeStruct((R, D), F32),
        compiler_params=_params("parallel"), name="ffn",
    )(x, norm_g.reshape(1, D), w_gate.astype(BF16), w_up.astype(BF16), w_down.astype(BF16),
      final_g.reshape(1, D))


def _overlap_matrix(n_cmp_pad, n_sel):
    c = np.arange(n_cmp_pad)[:, None] * CMP_STRIDE
    s = np.arange(n_sel)[None, :] * SEL_BLOCK
    return ((c <= s + SEL_BLOCK - 1) & (c + CMP_LEN - 1 >= s)).astype(np.float32)


def _layer(x, mem, rel_bias, final_g, is_last, norm_mix_g, w_in, nsa_gate_b, cmp_pe_k, cmp_pe_v,
           cmp_k_w1, cmp_k_b1, cmp_k_w2, cmp_v_w1, cmp_v_b1, cmp_v_w2,
           rwkv_mu, rwkv_w0, rwkv_w_up, rwkv_a0, rwkv_a_up, rwkv_g_up,
           rwkv_k_k, rwkv_k_a, rwkv_r_k, rwkv_lnx_w, rwkv_lnx_b, w_out,
           norm_x_g, norm_mem_g, w_q_x, w_kv_x, w_o_x, norm_ffn_g, w_gate, w_up, w_down):
    B, T, D = x.shape
    Hkv, G, dh = NSA_KV_HEADS, NSA_GROUP, HEAD_DIM
    q, kvc, kaug, vsT, vwT, gates, rw = _proj_in(x, norm_mix_g, w_in, nsa_gate_b)

    n16 = T // CMP_STRIDE
    kvc = kvc.reshape(B, T, 2 * KV_WIDTH)
    kc = _compress(kvc, 0, cmp_pe_k, cmp_k_w1, cmp_k_b1, cmp_k_w2, False)
    vcT = _compress(kvc, 1, cmp_pe_v, cmp_v_w1, cmp_v_b1, cmp_v_w2, True)
    bias_tiles, cmp_bias = _bias_tiles(rel_bias, n16)
    o_nsa = _nsa_attention(q.reshape(B, T, NSA_WIDTH), kc, vcT, kaug.reshape(B, T, 2 * Hkv * LANES), vsT, vwT,
                           gates, jnp.asarray(_overlap_matrix(n16, T // SEL_BLOCK).T, BF16), bias_tiles, cmp_bias)

    Q, Y0, A, Dm, g, bonus = _rwkv_chunks(rw.reshape(B, T, RWKV_COLS), rwkv_mu, rwkv_w0, rwkv_w_up, rwkv_a0,
                                          rwkv_a_up, rwkv_g_up, rwkv_k_k, rwkv_k_a, rwkv_r_k.reshape(-1))
    o_rwkv = _rwkv_scan(A, Dm, Q, Y0, g, bonus, rwkv_lnx_w, rwkv_lnx_b)

    M = mem.shape[1]
    (kv_mem,) = _norm_matmul(mem.reshape(B * M, D), norm_mem_g, [w_kv_x.astype(BF16)], [None], [BF16], ROW_TILE)
    kv_mem = kv_mem.reshape(B, M, 2 * D)
    x = _mix_xattn(x, o_nsa, o_rwkv, w_out, norm_x_g, w_q_x, kv_mem[..., :D], kv_mem[..., D:], w_o_x)
    x = _ffn(x.reshape(B * T, D), norm_ffn_g, w_gate, w_up, w_down, final_g, is_last)
    return x.reshape(B, T, D)


def kernel(x, mem, rel_bias, norm_f_g, norm_mix_g, w_in, nsa_gate_b, cmp_pe_k, cmp_pe_v, cmp_k_w1, cmp_k_b1, cmp_k_w2, cmp_v_w1, cmp_v_b1, cmp_v_w2, rwkv_mu, rwkv_w0, rwkv_w_up, rwkv_a0, rwkv_a_up, rwkv_g_up, rwkv_k_k, rwkv_k_a, rwkv_r_k, rwkv_lnx_w, rwkv_lnx_b, w_out, norm_x_g, norm_mem_g, w_q_x, w_kv_x, w_o_x, norm_ffn_g, w_gate, w_up, w_down):
    stacked = (norm_mix_g, w_in, nsa_gate_b, cmp_pe_k, cmp_pe_v, cmp_k_w1, cmp_k_b1, cmp_k_w2, cmp_v_w1,
               cmp_v_b1, cmp_v_w2, rwkv_mu, rwkv_w0, rwkv_w_up, rwkv_a0, rwkv_a_up, rwkv_g_up, rwkv_k_k,
               rwkv_k_a, rwkv_r_k, rwkv_lnx_w, rwkv_lnx_b, w_out, norm_x_g, norm_mem_g, w_q_x, w_kv_x, w_o_x,
               norm_ffn_g, w_gate, w_up, w_down)
    depth = w_in.shape[0]
    for l in range(depth):
        x = _layer(x, mem, rel_bias, norm_f_g, l == depth - 1, *[p[l] for p in stacked])
    return x
```

```python
import functools
import math

import numpy as np
import jax
import jax.numpy as jnp
from jax import lax
from jax.experimental import pallas as pl
from jax.experimental.pallas import tpu as pltpu

F32 = jnp.float32
BF16 = jnp.bfloat16
HIGHEST = lax.Precision.HIGHEST

LANES = 128
SUBLANES = 8
BF16_ROWS = 16
VMEM_LIMIT_BYTES = 56 * 1024 * 1024

HEAD_DIM = 64
NSA_HEADS = 8
NSA_KV_HEADS = 2
NSA_GROUP = NSA_HEADS // NSA_KV_HEADS
NSA_WIDTH = NSA_HEADS * HEAD_DIM
KV_WIDTH = NSA_KV_HEADS * HEAD_DIM
RWKV_HEADS = 8
RWKV_WIDTH = RWKV_HEADS * HEAD_DIM
CMP_LEN = 32
CMP_STRIDE = 16
SEL_BLOCK = 64
SEL_SHIFT = 6
SEL_TOP = 16
WINDOW = 512
Q_BLOCK = 128
DECAY_LORA = 64
AAA_LORA = 64
GATE_LORA = 128
N_BUCKETS = 32
MAX_DISTANCE = 2048
XATTN_HEADS = 4
RMS_EPS = 1e-6
LNX_EPS = 64e-5
FORCE_SCORE = 1e4
NEG_SCORE = -1e9
MASK_SCORE = -1e30
LOG2E = math.log2(math.e)
RWKV_COLS = 3 * RWKV_WIDTH + DECAY_LORA + AAA_LORA + GATE_LORA
NSA_COLS = NSA_WIDTH + 6 * KV_WIDTH + 3 * NSA_HEADS

KEY_CHUNK = 128
RWKV_CHUNK = 64
RWKV_STEP_CHUNKS = 4
ROW_TILE = 512
FFN_ROW_TILE = 256


def _t5_thresholds():
    d = np.arange(0, 2 * MAX_DISTANCE, dtype=np.int64)
    max_exact = N_BUCKETS // 2
    nf = np.maximum(d, 1).astype(np.float32)
    large = max_exact + (np.log(nf / np.float32(max_exact)) / np.float32(math.log(MAX_DISTANCE / max_exact))
                         * np.float32(N_BUCKETS - max_exact)).astype(np.int32)
    bucket = np.where(d < max_exact, d, np.minimum(large, N_BUCKETS - 1))
    return [int(np.argmax(bucket >= k)) for k in range(N_BUCKETS)]


T5_THRESHOLDS = _t5_thresholds()
N_BIAS_TILES = -(-(T5_THRESHOLDS[-1] + KEY_CHUNK) // KEY_CHUNK) + 1
TILE_MASKED = N_BIAS_TILES
TILE_WINDOW_EDGE = N_BIAS_TILES + 1
N_ALL_TILES = N_BIAS_TILES + 2
SEL_STEP_BLOCKS = 16
SEL_STAGE_TILES = 4
SEL_GROUP_STAGES = SEL_STEP_BLOCKS * SEL_BLOCK // (SEL_STAGE_TILES * KEY_CHUNK)
V_ROWS = HEAD_DIM + BF16_ROWS
GATE_ROWS = 16


def _params(*semantics):
    return pltpu.CompilerParams(dimension_semantics=semantics, vmem_limit_bytes=VMEM_LIMIT_BYTES)


def _rms(x, g):
    return x * lax.rsqrt(jnp.mean(x * x, axis=-1, keepdims=True) + RMS_EPS) * g


def _dot(a, b, **kw):
    return jnp.dot(a, b, preferred_element_type=F32, **kw)


def _split3(x):
    hi = x.astype(BF16)
    r1 = x - hi.astype(F32)
    mid = r1.astype(BF16)
    lo = (r1 - mid.astype(F32)).astype(BF16)
    return hi, mid, lo


def _dot_f32_lhs(x, w01):
    w = w01.astype(BF16)
    hi, mid, lo = _split3(x)
    return _dot(hi, w) + (_dot(mid, w) + _dot(lo, w))


def _dot_f32_rhs(w01, x):
    w = w01.astype(BF16)
    hi, mid, lo = _split3(x)
    return _dot(w, hi) + (_dot(w, mid) + _dot(w, lo))


def _dot_nt(a, b, **kw):
    return lax.dot_general(a, b, (((1,), (1,)), ((), ())), preferred_element_type=F32, **kw)


def _norm_matmul_kernel(x_ref, g_ref, *refs, nseg, bias_flags):
    nb = sum(bias_flags)
    w_refs, b_refs, o_refs = refs[:nseg], refs[nseg:nseg + nb], refs[nseg + nb:]
    xn = _rms(x_ref[...], g_ref[...]).astype(BF16)
    bi = 0
    for s in range(nseg):
        y = _dot(xn, w_refs[s][...])
        if bias_flags[s]:
            y = y + b_refs[bi][...]
            bi += 1
        o_refs[s][...] = y.astype(o_refs[s].dtype)


def _norm_matmul(x, g, weights, biases, out_dtypes, row_tile):
    R, D = x.shape
    tm = min(row_tile, R)
    assert R % tm == 0
    nseg = len(weights)
    bias_flags = tuple(b is not None for b in biases)
    const = lambda i: (0, 0)
    in_specs = [pl.BlockSpec((tm, D), lambda i: (i, 0)), pl.BlockSpec((1, D), const)]
    in_specs += [pl.BlockSpec(w.shape, const) for w in weights]
    in_specs += [pl.BlockSpec((1, b.shape[-1]), const) for b in biases if b is not None]
    out_specs = [pl.BlockSpec((tm, w.shape[1]), lambda i: (i, 0)) for w in weights]
    out_shape = [jax.ShapeDtypeStruct((R, w.shape[1]), dt) for w, dt in zip(weights, out_dtypes)]
    return pl.pallas_call(
        functools.partial(_norm_matmul_kernel, nseg=nseg, bias_flags=bias_flags),
        grid=(R // tm,), in_specs=in_specs, out_specs=out_specs, out_shape=out_shape,
        compiler_params=_params("parallel"), name="norm_matmul",
    )(x, g.reshape(1, D), *weights, *[b.reshape(1, -1) for b in biases if b is not None])


def _proj_in_kernel(x_ref, g_ref, wq_ref, wc_ref, wk_ref, wvT_ref, wgT_ref, bg_ref, wr_ref,
                    q_ref, kvc_ref, kaug_ref, vsT_ref, vwT_ref, gate_ref, rw_ref, *, seq_len):
    tm = x_ref.shape[0]
    xn = _rms(x_ref[...], g_ref[...]).astype(BF16)
    qT = (_dot_nt(wq_ref[...], xn) * (HEAD_DIM ** -0.5 * LOG2E)).astype(BF16)
    for j in range(tm // Q_BLOCK):
        for hg in range(NSA_HEADS):
            h, g = divmod(hg, NSA_GROUP)
            q_ref[0, j, h, :, g * Q_BLOCK:(g + 1) * Q_BLOCK] = qT[hg * HEAD_DIM:(hg + 1) * HEAD_DIM,
                                                                  j * Q_BLOCK:(j + 1) * Q_BLOCK]
    kvc_ref[...] = _dot(xn, wc_ref[...])
    rw_ref[...] = _dot(xn, wr_ref[...])
    k_all = _dot(xn, wk_ref[...])
    tok = lax.rem(pl.program_id(0) * tm, seq_len) + lax.broadcasted_iota(jnp.int32, k_all.shape, 0)
    lane = lax.broadcasted_iota(jnp.int32, k_all.shape, 1)
    blk = jnp.bitwise_and(jnp.right_shift(tok, SEL_SHIFT), SEL_STEP_BLOCKS - 1)
    hot = (jnp.bitwise_and(lane, LANES - 1) == HEAD_DIM + blk) & (lane < NSA_KV_HEADS * LANES)
    kaug_ref[...] = jnp.where(hot, 1.0, k_all).astype(BF16)
    vT = _dot_nt(wvT_ref[...], xn)
    row = lax.broadcasted_iota(jnp.int32, vT.shape, 0)
    ones_row = row == HEAD_DIM
    for grp in range(1, 2 * NSA_KV_HEADS):
        ones_row = ones_row | (row == grp * V_ROWS + HEAD_DIM)
    vT = jnp.where(ones_row, 1.0, vT).astype(BF16)
    half = NSA_KV_HEADS * V_ROWS
    vsT_ref[0, 0] = vT[0:half]
    for c in range(tm // KEY_CHUNK):
        vwT_ref[0, c] = vT[half:, c * KEY_CHUNK:(c + 1) * KEY_CHUNK]
    gT = _dot_nt(wgT_ref[...], xn) + bg_ref[...]
    for j in range(tm // Q_BLOCK):
        for h in range(NSA_KV_HEADS):
            gate_ref[0, j, h] = gT[h * GATE_ROWS:(h + 1) * GATE_ROWS, j * Q_BLOCK:(j + 1) * Q_BLOCK]


def _proj_in(x, norm_g, w_in, gate_b):
    B, T, D = x.shape
    Hkv, G, dh = NSA_KV_HEADS, NSA_GROUP, HEAD_DIM
    tm = SEL_STAGE_TILES * KEY_CHUNK
    assert T % tm == 0 and tm % Q_BLOCK == 0
    kv0 = NSA_WIDTH
    g0 = kv0 + 6 * KV_WIDTH
    stream = lambda s: w_in[:, kv0 + s * KV_WIDTH:kv0 + (s + 1) * KV_WIDTH].reshape(D, Hkv, dh)
    pad_cols = lambda w: jnp.pad(w, ((0, 0), (0, 0), (0, LANES - dh))).reshape(D, Hkv * LANES)
    pad_rows = lambda w: jnp.pad(w.transpose(1, 2, 0), ((0, 0), (0, V_ROWS - dh), (0, 0))).reshape(Hkv * V_ROWS, D)
    w_k = jnp.concatenate([pad_cols(stream(2)), pad_cols(stream(4))], axis=1)
    w_vT = jnp.concatenate([pad_rows(stream(3)), pad_rows(stream(5))], axis=0)
    reorder = lambda a: a.reshape(-1, Hkv, G, 3).transpose(1, 3, 2, 0).reshape(Hkv, 3 * G, -1)
    pad_gate = lambda a: jnp.pad(a, ((0, 0), (0, GATE_ROWS - 3 * G), (0, 0))).reshape(Hkv * GATE_ROWS, -1)
    w_gT = pad_gate(reorder(w_in[:, g0:NSA_COLS]))
    b_g = pad_gate(reorder(gate_b.reshape(1, -1)))
    weights = [w_in[:, :kv0].T, w_in[:, kv0:kv0 + 2 * KV_WIDTH], w_k, w_vT, w_gT]
    weights = [w.astype(BF16) for w in weights] + [b_g, w_in[:, NSA_COLS:].astype(BF16)]
    nt = T // tm
    rows = lambda n: pl.BlockSpec((tm, n), lambda i: (i, 0))
    const = lambda i: (0, 0)
    return pl.pallas_call(
        functools.partial(_proj_in_kernel, seq_len=T),
        grid=(B * nt,),
        in_specs=[rows(D), pl.BlockSpec((1, D), const)] + [pl.BlockSpec(w.shape, const) for w in weights],
        out_specs=[pl.BlockSpec((1, tm // Q_BLOCK, Hkv, dh, G * Q_BLOCK), lambda i: (i // nt, i % nt, 0, 0, 0)),
                   rows(2 * KV_WIDTH), rows(2 * Hkv * LANES),
                   pl.BlockSpec((1, 1, Hkv * V_ROWS, tm), lambda i: (i // nt, i % nt, 0, 0)),
                   pl.BlockSpec((1, tm // KEY_CHUNK, Hkv * V_ROWS, KEY_CHUNK), lambda i: (i // nt, i % nt, 0, 0)),
                   pl.BlockSpec((1, tm // Q_BLOCK, Hkv, GATE_ROWS, Q_BLOCK), lambda i: (i // nt, i % nt, 0, 0, 0)),
                   rows(RWKV_COLS)],
        out_shape=[jax.ShapeDtypeStruct((B, T // Q_BLOCK, Hkv, dh, G * Q_BLOCK), BF16),
                   jax.ShapeDtypeStruct((B * T, 2 * KV_WIDTH), F32),
                   jax.ShapeDtypeStruct((B * T, 2 * Hkv * LANES), BF16),
                   jax.ShapeDtypeStruct((B, nt, Hkv * V_ROWS, tm), BF16),
                   jax.ShapeDtypeStruct((B, T // KEY_CHUNK, Hkv * V_ROWS, KEY_CHUNK), BF16),
                   jax.ShapeDtypeStruct((B, T // Q_BLOCK, Hkv, GATE_ROWS, Q_BLOCK), F32),
                   jax.ShapeDtypeStruct((B * T, RWKV_COLS), F32)],
        compiler_params=_params("parallel"), name="proj_in",
    )(x.reshape(B * T, D), norm_g.reshape(1, D), *weights)


def _compress_kernel(x_ref, pe_ref, w1_ref, b1_ref, w2_ref, o_ref, *, transpose_out):
    n16 = x_ref.shape[1] // CMP_STRIDE
    hidden = w1_ref.shape[2] // NSA_KV_HEADS
    lo = jnp.zeros((n16, w1_ref.shape[2]), F32)
    hi = jnp.zeros((n16, w1_ref.shape[2]), F32)
    for l in range(CMP_STRIDE):
        rows = x_ref[0, pl.ds(l, n16, stride=CMP_STRIDE), :]
        lo = lo + _dot((rows + pe_ref[l:l + 1, :]).astype(BF16), w1_ref[l])
        hi = hi + _dot((rows + pe_ref[CMP_STRIDE + l:CMP_STRIDE + l + 1, :]).astype(BF16), w1_ref[CMP_STRIDE + l])
    h = lo + pltpu.roll(hi, n16 - 1, axis=0) + b1_ref[...]
    h = jax.nn.gelu(h).astype(BF16)
    for hkv in range(NSA_KV_HEADS):
        hh = h[:, hkv * hidden:(hkv + 1) * hidden]
        if transpose_out:
            o_ref[0, hkv] = _dot_nt(w2_ref[...], hh).astype(o_ref.dtype)
        else:
            o_ref[0, hkv] = _dot(hh, w2_ref[...]).astype(o_ref.dtype)


def _compress(kvc, stream, pe, w1, b1, w2, transpose_out):
    B, T, _ = kvc.shape
    H, dh = NSA_KV_HEADS, HEAD_DIM
    n16 = T // CMP_STRIDE
    hidden = w1.shape[1]
    eye = jnp.eye(H, dtype=w1.dtype)
    w1_bd = jnp.einsum('ldn,hg->lhdgn', w1.reshape(CMP_LEN, dh, hidden), eye).reshape(CMP_LEN, H * dh, H * hidden)
    w2b = (w2.T if transpose_out else w2).astype(BF16)
    oshape = (B, H, dh, n16) if transpose_out else (B, H, n16, dh)
    return pl.pallas_call(
        functools.partial(_compress_kernel, transpose_out=transpose_out),
        grid=(B,),
        in_specs=[pl.BlockSpec((1, T, H * dh), lambda b: (b, 0, stream)),
                  pl.BlockSpec((CMP_LEN, H * dh), lambda b: (0, 0)),
                  pl.BlockSpec(w1_bd.shape, lambda b: (0, 0, 0)),
                  pl.BlockSpec((1, H * hidden), lambda b: (0, 0)),
                  pl.BlockSpec(w2b.shape, lambda b: (0, 0))],
        out_specs=pl.BlockSpec((1,) + oshape[1:], lambda b: (b, 0, 0, 0)),
        out_shape=jax.ShapeDtypeStruct(oshape, BF16),
        compiler_params=_params("parallel"), name="nsa_compress",
    )(kvc, jnp.tile(pe, (1, H)), w1_bd.astype(BF16), jnp.tile(b1.reshape(1, hidden), (1, H)), w2b)


def _bias_of_distance(tab_ref, h, d):
    val = jnp.full(d.shape, tab_ref[h, 0], F32)
    for k in range(1, N_BUCKETS):
        val = jnp.where(d >= T5_THRESHOLDS[k], tab_ref[h, k], val)
    return val * LOG2E


def _bias_tiles_kernel(tab_ref, bt_ref, cb_ref, *, n_cmp_pad):
    hkv = pl.program_id(0)
    j = lax.broadcasted_iota(jnp.int32, (KEY_CHUNK, Q_BLOCK), 0)
    i = lax.broadcasted_iota(jnp.int32, (KEY_CHUNK, Q_BLOCK), 1)
    r2 = lax.broadcasted_iota(jnp.int32, (2 * n_cmp_pad, Q_BLOCK), 0)
    i2 = lax.broadcasted_iota(jnp.int32, (2 * n_cmp_pad, Q_BLOCK), 1)
    l2 = r2 - (n_cmp_pad - KEY_CHUNK)
    d2 = i2 - CMP_STRIDE * l2 + (CMP_STRIDE * KEY_CHUNK - Q_BLOCK - (CMP_LEN - 1))
    hidden2 = (l2 >= KEY_CHUNK) | ((l2 >= 0) & (d2 < 0))
    d2 = jnp.where((l2 >= 0) & (l2 < KEY_CHUNK), d2, 2 * MAX_DISTANCE)
    for g in range(NSA_GROUP):
        h = hkv * NSA_GROUP + g
        lanes = slice(g * Q_BLOCK, (g + 1) * Q_BLOCK)
        for m in range(N_BIAS_TILES):
            tile = _bias_of_distance(tab_ref, h, m * KEY_CHUNK + i - j)
            if m == 0:
                tile = jnp.where(j <= i, tile, MASK_SCORE)
            bt_ref[0, m, :, lanes] = tile
        bt_ref[0, TILE_MASKED, :, lanes] = jnp.full((KEY_CHUNK, Q_BLOCK), MASK_SCORE, F32)
        edge = _bias_of_distance(tab_ref, h, WINDOW + i - j)
        bt_ref[0, TILE_WINDOW_EDGE, :, lanes] = jnp.where(j > i, edge, MASK_SCORE)
        cb_ref[0, :, lanes] = jnp.where(hidden2, MASK_SCORE, _bias_of_distance(tab_ref, h, d2))


def _bias_tiles(rel_bias, n_cmp_pad):
    assert CMP_STRIDE * KEY_CHUNK - Q_BLOCK - (CMP_LEN - 1) >= T5_THRESHOLDS[-1]
    GQ = NSA_GROUP * Q_BLOCK
    return pl.pallas_call(
        functools.partial(_bias_tiles_kernel, n_cmp_pad=n_cmp_pad),
        grid=(NSA_KV_HEADS,),
        in_specs=[pl.BlockSpec(memory_space=pltpu.SMEM)],
        out_specs=[pl.BlockSpec((1, N_ALL_TILES, KEY_CHUNK, GQ), lambda h: (h, 0, 0, 0)),
                   pl.BlockSpec((1, 2 * n_cmp_pad, GQ), lambda h: (h, 0, 0))],
        out_shape=[jax.ShapeDtypeStruct((NSA_KV_HEADS, N_ALL_TILES, KEY_CHUNK, GQ), F32),
                   jax.ShapeDtypeStruct((NSA_KV_HEADS, 2 * n_cmp_pad, GQ), F32)],
        compiler_params=_params("parallel"), name="t5_bias_tiles",
    )(rel_bias.T)


def _nsa_kernel(q_ref, kc_ref, vcT_ref, ks_ref, vsT_ref, kw_ref, vwT_ref, gate_ref, ovT_ref, bt_ref, cb_ref,
                o_ref, acc_sc, qaug_sc, seladd_sc, s0_sc, s1_sc, p0_sc, p1_sc, *, n_sel, n_cmp_pad):
    G, GQ = NSA_GROUP, NSA_GROUP * Q_BLOCK
    qb = pl.program_id(2)
    qT = q_ref[0, 0, 0]
    qaug_sc[0:HEAD_DIM, :] = qT
    qaug_sc[HEAD_DIM:, :] = jnp.zeros((qaug_sc.shape[0] - HEAD_DIM, GQ), BF16)
    tile_g = lambda a: jnp.concatenate([a] * G, axis=1)
    t = qb * Q_BLOCK + lax.broadcasted_iota(jnp.int32, (1, Q_BLOCK), 1)

    n_back = WINDOW // KEY_CHUNK
    q_win = qaug_sc[...]
    win = {"acc": jnp.zeros(acc_sc.shape, F32)}

    def win_scores(back):
        kc = jnp.maximum(qb - back, 0)
        edge = TILE_WINDOW_EDGE if back == n_back else back
        tile = jnp.where(qb >= back, edge, TILE_MASKED)
        k_chunk = kw_ref[0, pl.ds(pl.multiple_of(kc * KEY_CHUNK, KEY_CHUNK), KEY_CHUNK), :]
        win["s", back] = _dot(k_chunk, q_win) + bt_ref[0, tile]
        col_max = jnp.max(win["s", back], axis=0, keepdims=True)
        win["m"] = jnp.maximum(win["m"], col_max) if "m" in win else col_max

    def win_weights(back):
        win["p", back] = jnp.exp2(win["s", back] - win["m"]).astype(BF16)

    def win_values(back):
        win["acc"] = win["acc"] + _dot(vwT_ref[0, jnp.maximum(qb - back, 0)], win["p", back])

    backs = list(range(n_back, -1, -1))
    window_work = ([functools.partial(win_scores, b) for b in backs]
                   + [functools.partial(win_weights, b) for b in backs]
                   + [functools.partial(win_values, b) for b in backs])

    start = pl.multiple_of(n_cmp_pad - (Q_BLOCK // CMP_STRIDE) * (qb + 1), SUBLANES)
    s = _dot(kc_ref[0, 0], qT) + cb_ref[0, pl.ds(start, n_cmp_pad), :]
    e = jnp.exp2(s - jnp.maximum(jnp.max(s, axis=0, keepdims=True), 0.1 * MASK_SCORE))
    p = e * (1.0 / jnp.maximum(jnp.sum(e, axis=0, keepdims=True), 1e-30))
    o_c = _dot(vcT_ref[0, 0], p.astype(BF16))
    psum = p[:, 0:Q_BLOCK]
    for g in range(1, G):
        psum = psum + p[:, g * Q_BLOCK:(g + 1) * Q_BLOCK]

    imp = _dot_f32_rhs(ovT_ref[...], psum)
    bj = lax.broadcasted_iota(jnp.int32, (n_sel, Q_BLOCK), 0)
    cur = jnp.right_shift(t, SEL_SHIFT)
    forced = (bj == 0) | (bj == cur) | (bj == cur - 1)
    valid = bj * SEL_BLOCK <= t
    score = jnp.where(forced, -jnp.inf, jnp.where(valid, imp, NEG_SCORE))
    bjf = bj.astype(F32)
    sel = jnp.where(forced, 1.0, 0.0)
    for rnd in range(max(min(SEL_TOP, n_sel) - 3, 0)):
        mx = jnp.max(score, axis=0, keepdims=True)
        first = jnp.min(jnp.where(score == mx, bjf, float(n_sel)), axis=0, keepdims=True)
        pick = bjf == first
        sel = jnp.where(pick, 1.0, sel)
        score = jnp.where(pick, -jnp.inf, score)
        if rnd < len(window_work):
            window_work[rnd]()
    for piece in window_work[max(min(SEL_TOP, n_sel) - 3, 0):]:
        piece()
    o_w = win["acc"][0:HEAD_DIM] / jnp.maximum(win["acc"][HEAD_DIM:HEAD_DIM + 1], 1e-30)
    seladd_sc[...] = tile_g((sel - 1.0) * (-MASK_SCORE))

    def bias_rows(first_chunk, n_chunks):
        tiles = []
        for k in range(n_chunks):
            dist = qb - (first_chunk + k)
            tiles.append(bt_ref[0, jnp.where(dist < 0, TILE_MASKED, jnp.minimum(dist, N_BIAS_TILES - 1))])
        return jnp.concatenate(tiles, axis=0)

    stage_keys = SEL_STAGE_TILES * KEY_CHUNK
    n_stages = qb // SEL_STAGE_TILES + 1
    last_stage = ks_ref.shape[1] // stage_keys - 1

    def scores(k, s_buf):
        kk = jnp.minimum(k, last_stage)
        blk0 = pl.multiple_of(kk // SEL_GROUP_STAGES * SEL_STEP_BLOCKS, SEL_STEP_BLOCKS)
        qaug_sc[HEAD_DIM:HEAD_DIM + SEL_STEP_BLOCKS, :] = seladd_sc[pl.ds(blk0, SEL_STEP_BLOCKS), :].astype(BF16)
        k0 = pl.multiple_of(kk * stage_keys, stage_keys)
        s = _dot(ks_ref[0, pl.ds(k0, stage_keys), :], qaug_sc[...]) + bias_rows(k * SEL_STAGE_TILES,
                                                                                  SEL_STAGE_TILES)
        s_buf[...] = s
        return jnp.max(s, axis=0, keepdims=True)

    def weights(m, col_max, s_buf, p_buf):
        m_new = jnp.maximum(m, col_max)
        p_buf[...] = jnp.exp2(s_buf[...] - m_new).astype(BF16)
        return m_new, jnp.exp2(m - m_new)

    def accumulate(k, alpha, p_buf):
        acc_sc[...] = alpha * acc_sc[...] + _dot(vsT_ref[0, jnp.clip(k, 0, last_stage)], p_buf[...])

    def pair(j, carry):
        m, col_max, alpha = carry
        k = 2 * j
        col_max1 = scores(k + 1, s1_sc)
        m, alpha0 = weights(m, col_max, s0_sc, p0_sc)
        accumulate(k - 1, alpha, p1_sc)
        col_max2 = scores(k + 2, s0_sc)
        m, alpha1 = weights(m, col_max1, s1_sc, p1_sc)
        accumulate(k, alpha0, p0_sc)
        return m, col_max2, alpha1

    acc_sc[...] = jnp.zeros(acc_sc.shape, F32)
    p1_sc[...] = jnp.zeros(p1_sc.shape, BF16)
    m_init = jnp.full((1, GQ), 0.1 * MASK_SCORE, F32)
    carry = (m_init, scores(0, s0_sc), jnp.ones((1, GQ), F32))
    n_pairs = (n_stages + 1) // 2
    _, _, alpha = lax.fori_loop(0, n_pairs, pair, carry)
    accumulate(2 * n_pairs - 1, alpha, p1_sc)
    o_s = acc_sc[0:HEAD_DIM, :] / jnp.maximum(acc_sc[HEAD_DIM:HEAD_DIM + 1, :], 1e-30)

    gt = jax.nn.sigmoid(gate_ref[0, 0, 0])
    gate = [jnp.concatenate([gt[br * G + g:br * G + g + 1, :] for g in range(G)], axis=1) for br in range(3)]
    o = gate[0] * o_c + gate[1] * o_s + gate[2] * o_w
    o_ref[0] = jnp.concatenate([o[:, g * Q_BLOCK:(g + 1) * Q_BLOCK].T for g in range(G)], axis=1)


def _nsa_attention(q, kc, vcT, kaug, vsT, vwT, gates, overlapT, bias_tiles, cmp_bias):
    B, T, _ = kaug.shape
    Hkv, G = NSA_KV_HEADS, NSA_GROUP
    GQ = G * Q_BLOCK
    nqb = T // Q_BLOCK
    n_cmp_pad = kc.shape[2]
    n_sel = T // SEL_BLOCK
    half_keys = SEL_STAGE_TILES * KEY_CHUNK
    assert T % (SEL_GROUP_STAGES * half_keys) == 0
    per_head = lambda b, h, i: (b, h, 0, 0)
    v_rows = V_ROWS
    chunked = lambda w: pl.BlockSpec((1, T // w, v_rows, w), lambda b, h, i: (b, 0, h, 0))
    return pl.pallas_call(
        functools.partial(_nsa_kernel, n_sel=n_sel, n_cmp_pad=n_cmp_pad),
        grid=(B, Hkv, nqb),
        in_specs=[pl.BlockSpec((1, 1, 1, HEAD_DIM, GQ), lambda b, h, i: (b, i, h, 0, 0)),
                  pl.BlockSpec((1, 1, n_cmp_pad, HEAD_DIM), per_head),
                  pl.BlockSpec((1, 1, HEAD_DIM, n_cmp_pad), per_head),
                  pl.BlockSpec((1, T, LANES), lambda b, h, i: (b, 0, h)), chunked(half_keys),
                  pl.BlockSpec((1, T, LANES), lambda b, h, i: (b, 0, Hkv + h)), chunked(KEY_CHUNK),
                  pl.BlockSpec((1, 1, 1, GATE_ROWS, Q_BLOCK), lambda b, h, i: (b, i, h, 0, 0)),
                  pl.BlockSpec(overlapT.shape, lambda b, h, i: (0, 0)),
                  pl.BlockSpec((1, N_ALL_TILES, KEY_CHUNK, GQ), lambda b, h, i: (h, 0, 0, 0)),
                  pl.BlockSpec((1, 2 * n_cmp_pad, GQ), lambda b, h, i: (h, 0, 0))],
        out_specs=pl.BlockSpec((1, Q_BLOCK, G * HEAD_DIM), lambda b, h, i: (b, i, h)),
        out_shape=jax.ShapeDtypeStruct((B, T, NSA_WIDTH), F32),
        scratch_shapes=[pltpu.VMEM((v_rows, GQ), F32), pltpu.VMEM((LANES, GQ), BF16),
                        pltpu.VMEM((n_sel, GQ), F32),
                        pltpu.VMEM((half_keys, GQ), F32), pltpu.VMEM((half_keys, GQ), F32),
                        pltpu.VMEM((half_keys, GQ), BF16), pltpu.VMEM((half_keys, GQ), BF16)],
        compiler_params=_params("parallel", "parallel", "arbitrary"), name="nsa_attention",
    )(q, kc, vcT, kaug, vsT, kaug, vwT, gates, overlapT, bias_tiles, cmp_bias)


def _rwkv_chunk_kernel(rw_ref, prev_ref, mu_ref, w0_ref, wup_ref, a0_ref, aup_ref, gup_ref, kk_ref, ka_ref,
                       rk_ref, seg_ref, tri_ref, q_ref, y0_ref, a_ref, d_ref, g_ref, bonus_ref):
    C, W, N = RWKV_CHUNK, RWKV_WIDTH, HEAD_DIM
    c = pl.program_id(1)
    x = rw_ref[0]
    R = x.shape[0]
    chunk_rows = [slice(ck * C, (ck + 1) * C) for ck in range(R // C)]
    row = lax.broadcasted_iota(jnp.int32, (R, 1), 0)
    last_prev = jnp.where(c == 0, 0.0, prev_ref[0, SUBLANES - 1:SUBLANES, :])
    x_prev = jnp.where(row == 0, last_prev, pltpu.roll(x, 1, axis=0))
    xs = x + (x_prev - x) * mu_ref[...]
    r, k, v = xs[:, 0:W], xs[:, W:2 * W], xs[:, 2 * W:3 * W]
    o = 3 * W
    wd, ad, gd = xs[:, o:o + DECAY_LORA], xs[:, o + DECAY_LORA:o + DECAY_LORA + AAA_LORA], \
        xs[:, o + DECAY_LORA + AAA_LORA:]
    w_log = -jax.nn.softplus(-(w0_ref[...] + _dot(jnp.tanh(wd).astype(BF16), wup_ref[...]))) - 0.5
    lw = -jnp.exp(w_log)
    lr = jax.nn.sigmoid(a0_ref[...] + _dot(ad.astype(BF16), aup_ref[...]))
    g_ref[0] = _dot(jax.nn.sigmoid(gd).astype(BF16), gup_ref[...])
    kk = k * kk_ref[...]
    kk = kk * lax.rsqrt(jnp.maximum(_dot_f32_lhs(kk * kk, seg_ref[...]), 1e-24))
    k = k * (1.0 + (lr - 1.0) * ka_ref[...])
    bonus_ref[0] = _dot_f32_lhs(r * k * rk_ref[...], seg_ref[...]) * v
    a_vec, b_vec = -kk, kk * lr

    L = _dot_f32_rhs(tri_ref[...], lw)
    L_end = jnp.concatenate([jnp.broadcast_to(L[rs.stop - 1:rs.stop, :], (C, W)) for rs in chunk_rows], axis=0)
    e_neg = jnp.exp(-L)
    e_rem = jnp.exp(L_end - L)
    At, Bt, Kt, Rt = a_vec * jnp.exp(L - lw), b_vec * e_neg, k * e_neg, r * jnp.exp(L)
    Bg, Kg = b_vec * e_rem, k * e_rem
    decay_end = [jnp.exp(L[rs.stop - 1:rs.stop, :]) for rs in chunk_rows]

    ri = lax.broadcasted_iota(jnp.int32, (C, C), 0)
    cj = lax.broadcasted_iota(jnp.int32, (C, C), 1)
    strict, incl = ri > cj, ri >= cj
    eye_c = (ri == cj).astype(F32)
    eye_n = (lax.broadcasted_iota(jnp.int32, (N, N), 0) == lax.broadcasted_iota(jnp.int32, (N, N), 1)).astype(F32)
    items = [(ck, h) for ck in range(len(chunk_rows)) for h in range(RWKV_HEADS)]
    sl = [(chunk_rows[ck], slice(h * N, (h + 1) * N)) for ck, h in items]
    ar = [jnp.concatenate([At[s], Rt[s]], axis=0) for s in sl]
    zb = [_dot_nt(a, Bt[s]) for a, s in zip(ar, sl)]
    zk = [_dot_nt(a, Kt[s]) for a, s in zip(ar, sl)]
    n_mat = [jnp.where(strict, z[0:C], 0.0) for z in zb]
    m_mat = [jnp.where(strict, z[0:C], 0.0) for z in zk]
    mv = [_dot(m, v[s]) for m, s in zip(m_mat, sl)]
    t_inv, n_pow = [eye_c + n for n in n_mat], n_mat
    for _ in range(int(math.log2(C)) - 1):
        n_pow = [_dot(n, n) for n in n_pow]
        t_inv = [t + _dot(t, n) for t, n in zip(t_inv, n_pow)]
    ta = [_dot(t, At[s]) for t, s in zip(t_inv, sl)]
    g0 = [_dot(t, m) for t, m in zip(t_inv, mv)]
    pb = [jnp.concatenate([jnp.where(incl, z[C:], 0.0), Bg[s].T], axis=0) for z, s in zip(zb, sl)]
    pkk = [jnp.concatenate([jnp.where(incl, z[C:], 0.0), Kg[s].T], axis=0) for z, s in zip(zk, sl)]
    x_ta = [_dot(p, t) for p, t in zip(pb, ta)]
    x_g0 = [_dot(p, g) for p, g in zip(pb, g0)]
    x_v = [_dot(p, v[s]) for p, s in zip(pkk, sl)]
    for n, (ck, h) in enumerate(items):
        rows, lanes = sl[n]
        q_ref[0, rows, lanes] = Rt[sl[n]] + x_ta[n][0:C]
        y0_ref[0, rows, lanes] = x_g0[n][0:C] + x_v[n][0:C]
        a_ref[0, ck, h] = x_ta[n][C:] + eye_n * decay_end[ck][:, lanes]
        d_ref[0, ck, h] = x_g0[n][C:] + x_v[n][C:]


def _rwkv_chunks(rw, mu, w0, w_up, a0, a_up, g_up, k_k, k_a, r_k):
    B, T, cols = rw.shape
    C, W, H, N = RWKV_CHUNK, RWKV_WIDTH, RWKV_HEADS, HEAD_DIM
    nc = T // C
    S = RWKV_STEP_CHUNKS
    R = S * C
    assert nc % S == 0
    seg = jnp.asarray(np.kron(np.eye(H), np.ones((N, N))), BF16)
    tri = jnp.asarray(np.kron(np.eye(S), np.tril(np.ones((C, C)))), BF16)
    row = lambda z: z.reshape(1, -1)
    const = lambda b, c: (0, 0)
    vec = pl.BlockSpec((1, W), const)
    tok = pl.BlockSpec((1, R, W), lambda b, c: (b, c, 0))
    mat = pl.BlockSpec((1, S, H, N, N), lambda b, c: (b, c, 0, 0, 0))
    return pl.pallas_call(
        _rwkv_chunk_kernel,
        grid=(B, nc // S),
        in_specs=[pl.BlockSpec((1, R, cols), lambda b, c: (b, c, 0)),
                  pl.BlockSpec((1, SUBLANES, cols), lambda b, c: (b, jnp.maximum(c * (R // SUBLANES) - 1, 0), 0)),
                  pl.BlockSpec((1, cols), const), vec,
                  pl.BlockSpec((DECAY_LORA, W), const), vec,
                  pl.BlockSpec((AAA_LORA, W), const),
                  pl.BlockSpec((GATE_LORA, W), const), vec, vec, vec,
                  pl.BlockSpec((W, W), const), pl.BlockSpec((R, R), const)],
        out_specs=[tok, tok, mat, mat, tok, tok],
        out_shape=[jax.ShapeDtypeStruct((B, T, W), F32), jax.ShapeDtypeStruct((B, T, W), F32),
                   jax.ShapeDtypeStruct((B, nc, H, N, N), F32), jax.ShapeDtypeStruct((B, nc, H, N, N), F32),
                   jax.ShapeDtypeStruct((B, T, W), F32), jax.ShapeDtypeStruct((B, T, W), F32)],
        compiler_params=_params("parallel", "parallel"), name="rwkv_chunks",
    )(rw, rw, row(mu), row(w0), w_up.astype(BF16), row(a0), a_up.astype(BF16), g_up.astype(BF16),
      row(k_k), row(k_a), row(r_k), seg, tri)


def _rwkv_scan_kernel(a_ref, d_ref, q_ref, y0_ref, g_ref, bonus_ref, lw_ref, lb_ref, o_ref, h_sc):
    N = HEAD_DIM
    @pl.when(pl.program_id(0) == 0)
    def _():
        h_sc[...] = jnp.zeros(h_sc.shape, F32)

    for b in range(q_ref.shape[0]):
        for h in range(RWKV_HEADS):
            hs = slice(h * N, (h + 1) * N)
            state = h_sc[b, h]
            y = _dot(q_ref[b, :, hs], state, precision=HIGHEST) + y0_ref[b, :, hs]
            h_sc[b, h] = _dot(a_ref[b, 0, h], state, precision=HIGHEST) + d_ref[b, 0, h]
            mean = jnp.mean(y, axis=-1, keepdims=True)
            var = jnp.mean(jnp.square(y - mean), axis=-1, keepdims=True)
            yn = (y - mean) * lax.rsqrt(var + LNX_EPS)
            yn = yn * lw_ref[:, hs] + lb_ref[:, hs]
            o_ref[b, :, hs] = (yn + bonus_ref[b, :, hs]) * g_ref[b, :, hs]


def _rwkv_scan(A, D, Q, Y0, g, bonus, lnx_w, lnx_b):
    B, nc, H, N, _ = A.shape
    T, W, C = Q.shape[1], Q.shape[2], RWKV_CHUNK
    tok = pl.BlockSpec((B, C, W), lambda c: (0, c, 0))
    mat = pl.BlockSpec((B, 1, H, N, N), lambda c: (0, c, 0, 0, 0))
    vec = pl.BlockSpec((1, W), lambda c: (0, 0))
    return pl.pallas_call(
        _rwkv_scan_kernel,
        grid=(nc,),
        in_specs=[mat, mat, tok, tok, tok, tok, vec, vec],
        out_specs=tok,
        out_shape=jax.ShapeDtypeStruct((B, T, W), F32),
        scratch_shapes=[pltpu.VMEM((B, H, N, N), F32)],
        compiler_params=_params("arbitrary"), name="rwkv_scan",
    )(A, D, Q, Y0, g, bonus, lnx_w.reshape(1, W), lnx_b.reshape(1, W))


def _mix_xattn_kernel(x_ref, on_ref, or_ref, wo1_ref, wo2_ref, gx_ref, wq_ref, k_ref, v_ref, wo_ref, o_ref):
    x1 = x_ref[0] + _dot(on_ref[0].astype(BF16), wo1_ref[...]) + _dot(or_ref[0].astype(BF16), wo2_ref[...])
    q = _dot(_rms(x1, gx_ref[...]).astype(BF16), wq_ref[...])
    dh = q.shape[-1] // XATTN_HEADS
    qbf = (q * (dh ** -0.5)).astype(BF16)
    outs = []
    for h in range(XATTN_HEADS):
        hs = slice(h * dh, (h + 1) * dh)
        s = _dot_nt(qbf[:, hs], k_ref[0, :, hs])
        e = jnp.exp(s - jnp.max(s, axis=-1, keepdims=True))
        p = e / jnp.sum(e, axis=-1, keepdims=True)
        outs.append(_dot(p.astype(BF16), v_ref[0, :, hs]))
    o = jnp.concatenate(outs, axis=1).astype(BF16)
    o_ref[0] = x1 + _dot(o, wo_ref[...])


def _mix_xattn(x, o_nsa, o_rwkv, w_out, norm_x_g, w_q, mem_k, mem_v, w_o):
    B, T, D = x.shape
    M = mem_k.shape[1]
    tm = min(ROW_TILE, T)
    wo1, wo2 = w_out[:NSA_WIDTH].astype(BF16), w_out[NSA_WIDTH:].astype(BF16)
    const = lambda b, i: (0, 0)
    tile = lambda w: pl.BlockSpec((1, tm, w), lambda b, i: (b, i, 0))
    return pl.pallas_call(
        _mix_xattn_kernel,
        grid=(B, T // tm),
        in_specs=[tile(D), tile(NSA_WIDTH), tile(RWKV_WIDTH),
                  pl.BlockSpec(wo1.shape, const), pl.BlockSpec(wo2.shape, const),
                  pl.BlockSpec((1, D), const), pl.BlockSpec((D, D), const),
                  pl.BlockSpec((1, M, D), lambda b, i: (b, 0, 0)), pl.BlockSpec((1, M, D), lambda b, i: (b, 0, 0)),
                  pl.BlockSpec((D, D), const)],
        out_specs=tile(D),
        out_shape=jax.ShapeDtypeStruct((B, T, D), F32),
        compiler_params=_params("parallel", "parallel"), name="mix_xattn",
    )(x, o_nsa, o_rwkv, wo1, wo2, norm_x_g.reshape(1, D), w_q.astype(BF16), mem_k, mem_v, w_o.astype(BF16))


def _ffn_kernel(x_ref, g_ref, wg_ref, wu_ref, wd_ref, gf_ref, o_ref, *, final_norm):
    x = x_ref[...]
    h = _rms(x, g_ref[...]).astype(BF16)
    act = (jax.nn.silu(_dot(h, wg_ref[...])) * _dot(h, wu_ref[...])).astype(BF16)
    y = x + _dot(act, wd_ref[...])
    o_ref[...] = _rms(y, gf_ref[...]) if final_norm else y


def _ffn(x, norm_g, w_gate, w_up, w_down, final_g, final_norm):
    R, D = x.shape
    F = w_gate.shape[1]
    tm = min(FFN_ROW_TILE, R)
    const = lambda i: (0, 0)
    resident = lambda shape: pl.BlockSpec(shape, const, pipeline_mode=pl.Buffered(1))
    return pl.pallas_call(
        functools.partial(_ffn_kernel, final_norm=final_norm),
        grid=(R // tm,),
        in_specs=[pl.BlockSpec((tm, D), lambda i: (i, 0)), pl.BlockSpec((1, D), const),
                  resident((D, F)), resident((D, F)), resident((F, D)), pl.BlockSpec((1, D), const)],
        out_specs=pl.BlockSpec((tm, D), lambda i: (i, 0)),
        out_shape=jax.ShapeDtypeStruct((R, D), F32),
        compiler_params=_params("parallel"), name="ffn",
    )(x, norm_g.reshape(1, D), w_gate.astype(BF16), w_up.astype(BF16), w_down.astype(BF16),
      final_g.reshape(1, D))


def _overlap_matrix(n_cmp_pad, n_sel):
    c = np.arange(n_cmp_pad)[:, None] * CMP_STRIDE
    s = np.arange(n_sel)[None, :] * SEL_BLOCK
    return ((c <= s + SEL_BLOCK - 1) & (c + CMP_LEN - 1 >= s)).astype(np.float32)


def _layer(x, mem, rel_bias, final_g, is_last, norm_mix_g, w_in, nsa_gate_b, cmp_pe_k, cmp_pe_v,
           cmp_k_w1, cmp_k_b1, cmp_k_w2, cmp_v_w1, cmp_v_b1, cmp_v_w2,
           rwkv_mu, rwkv_w0, rwkv_w_up, rwkv_a0, rwkv_a_up, rwkv_g_up,
           rwkv_k_k, rwkv_k_a, rwkv_r_k, rwkv_lnx_w, rwkv_lnx_b, w_out,
           norm_x_g, norm_mem_g, w_q_x, w_kv_x, w_o_x, norm_ffn_g, w_gate, w_up, w_down):
    B, T, D = x.shape
    Hkv, G, dh = NSA_KV_HEADS, NSA_GROUP, HEAD_DIM
    q, kvc, kaug, vsT, vwT, gates, rw = _proj_in(x, norm_mix_g, w_in, nsa_gate_b)

    n16 = T // CMP_STRIDE
    kvc = kvc.reshape(B, T, 2 * KV_WIDTH)
    kc = _compress(kvc, 0, cmp_pe_k, cmp_k_w1, cmp_k_b1, cmp_k_w2, False)
    vcT = _compress(kvc, 1, cmp_pe_v, cmp_v_w1, cmp_v_b1, cmp_v_w2, True)
    bias_tiles, cmp_bias = _bias_tiles(rel_bias, n16)
    o_nsa = _nsa_attention(q, kc, vcT, kaug.reshape(B, T, 2 * Hkv * LANES), vsT, vwT,
                           gates, jnp.asarray(_overlap_matrix(n16, T // SEL_BLOCK).T, BF16), bias_tiles, cmp_bias)

    Q, Y0, A, Dm, g, bonus = _rwkv_chunks(rw.reshape(B, T, RWKV_COLS), rwkv_mu, rwkv_w0, rwkv_w_up, rwkv_a0,
                                          rwkv_a_up, rwkv_g_up, rwkv_k_k, rwkv_k_a, rwkv_r_k.reshape(-1))
    o_rwkv = _rwkv_scan(A, Dm, Q, Y0, g, bonus, rwkv_lnx_w, rwkv_lnx_b)

    M = mem.shape[1]
    (kv_mem,) = _norm_matmul(mem.reshape(B * M, D), norm_mem_g, [w_kv_x.astype(BF16)], [None], [BF16], ROW_TILE)
    kv_mem = kv_mem.reshape(B, M, 2 * D)
    x = _mix_xattn(x, o_nsa, o_rwkv, w_out, norm_x_g, w_q_x, kv_mem[..., :D], kv_mem[..., D:], w_o_x)
    x = _ffn(x.reshape(B * T, D), norm_ffn_g, w_gate, w_up, w_down, final_g, is_last)
    return x.reshape(B, T, D)


def kernel(x, mem, rel_bias, norm_f_g, norm_mix_g, w_in, nsa_gate_b, cmp_pe_k, cmp_pe_v, cmp_k_w1, cmp_k_b1, cmp_k_w2, cmp_v_w1, cmp_v_b1, cmp_v_w2, rwkv_mu, rwkv_w0, rwkv_w_up, rwkv_a0, rwkv_a_up, rwkv_g_up, rwkv_k_k, rwkv_k_a, rwkv_r_k, rwkv_lnx_w, rwkv_lnx_b, w_out, norm_x_g, norm_mem_g, w_q_x, w_kv_x, w_o_x, norm_ffn_g, w_gate, w_up, w_down):
    stacked = (norm_mix_g, w_in, nsa_gate_b, cmp_pe_k, cmp_pe_v, cmp_k_w1, cmp_k_b1, cmp_k_w2, cmp_v_w1,
               cmp_v_b1, cmp_v_w2, rwkv_mu, rwkv_w0, rwkv_w_up, rwkv_a0, rwkv_a_up, rwkv_g_up, rwkv_k_k,
               rwkv_k_a, rwkv_r_k, rwkv_lnx_w, rwkv_lnx_b, w_out, norm_x_g, norm_mem_g, w_q_x, w_kv_x, w_o_x,
               norm_ffn_g, w_gate, w_up, w_down)
    depth = w_in.shape[0]
    for l in range(depth):
        x = _layer(x, mem, rel_bias, norm_f_g, l == depth - 1, *[p[l] for p in stacked])
    return x
```

```python
import functools
import math

import numpy as np
import jax
import jax.numpy as jnp
from jax import lax
from jax.experimental import pallas as pl
from jax.experimental.pallas import tpu as pltpu

F32 = jnp.float32
BF16 = jnp.bfloat16
HIGHEST = lax.Precision.HIGHEST

LANES = 128
SUBLANES = 8
BF16_ROWS = 16
VMEM_LIMIT_BYTES = 56 * 1024 * 1024

HEAD_DIM = 64
NSA_HEADS = 8
NSA_KV_HEADS = 2
NSA_GROUP = NSA_HEADS // NSA_KV_HEADS
NSA_WIDTH = NSA_HEADS * HEAD_DIM
KV_WIDTH = NSA_KV_HEADS * HEAD_DIM
RWKV_HEADS = 8
RWKV_WIDTH = RWKV_HEADS * HEAD_DIM
CMP_LEN = 32
CMP_STRIDE = 16
SEL_BLOCK = 64
SEL_SHIFT = 6
SEL_TOP = 16
WINDOW = 512
Q_BLOCK = 128
DECAY_LORA = 64
AAA_LORA = 64
GATE_LORA = 128
N_BUCKETS = 32
MAX_DISTANCE = 2048
XATTN_HEADS = 4
RMS_EPS = 1e-6
LNX_EPS = 64e-5
FORCE_SCORE = 1e4
NEG_SCORE = -1e9
MASK_SCORE = -1e30
LOG2E = math.log2(math.e)
RWKV_COLS = 3 * RWKV_WIDTH + DECAY_LORA + AAA_LORA + GATE_LORA
NSA_COLS = NSA_WIDTH + 6 * KV_WIDTH + 3 * NSA_HEADS

KEY_CHUNK = 128
RWKV_CHUNK = 64
RWKV_STEP_CHUNKS = 4
ROW_TILE = 512
FFN_ROW_TILE = 256


def _t5_thresholds():
    d = np.arange(0, 2 * MAX_DISTANCE, dtype=np.int64)
    max_exact = N_BUCKETS // 2
    nf = np.maximum(d, 1).astype(np.float32)
    large = max_exact + (np.log(nf / np.float32(max_exact)) / np.float32(math.log(MAX_DISTANCE / max_exact))
                         * np.float32(N_BUCKETS - max_exact)).astype(np.int32)
    bucket = np.where(d < max_exact, d, np.minimum(large, N_BUCKETS - 1))
    return [int(np.argmax(bucket >= k)) for k in range(N_BUCKETS)]


T5_THRESHOLDS = _t5_thresholds()
N_BIAS_TILES = -(-(T5_THRESHOLDS[-1] + KEY_CHUNK) // KEY_CHUNK) + 1
TILE_MASKED = N_BIAS_TILES
TILE_WINDOW_EDGE = N_BIAS_TILES + 1
N_ALL_TILES = N_BIAS_TILES + 2
SEL_STEP_BLOCKS = 16
SEL_STAGE_TILES = 4
SEL_GROUP_STAGES = SEL_STEP_BLOCKS * SEL_BLOCK // (SEL_STAGE_TILES * KEY_CHUNK)
NSA_STEP_QBLOCKS = 4
V_ROWS = HEAD_DIM + BF16_ROWS
GATE_ROWS = 16


def _params(*semantics):
    return pltpu.CompilerParams(dimension_semantics=semantics, vmem_limit_bytes=VMEM_LIMIT_BYTES)


def _rms(x, g):
    return x * lax.rsqrt(jnp.mean(x * x, axis=-1, keepdims=True) + RMS_EPS) * g


def _dot(a, b, **kw):
    return jnp.dot(a, b, preferred_element_type=F32, **kw)


def _split3(x):
    hi = x.astype(BF16)
    r1 = x - hi.astype(F32)
    mid = r1.astype(BF16)
    lo = (r1 - mid.astype(F32)).astype(BF16)
    return hi, mid, lo


def _dot_f32_lhs(x, w01):
    w = w01.astype(BF16)
    hi, mid, lo = _split3(x)
    return _dot(hi, w) + (_dot(mid, w) + _dot(lo, w))


def _dot_f32_rhs(w01, x):
    w = w01.astype(BF16)
    hi, mid, lo = _split3(x)
    return _dot(w, hi) + (_dot(w, mid) + _dot(w, lo))


def _dot_nt(a, b, **kw):
    return lax.dot_general(a, b, (((1,), (1,)), ((), ())), preferred_element_type=F32, **kw)


def _norm_matmul_kernel(x_ref, g_ref, *refs, nseg, bias_flags):
    nb = sum(bias_flags)
    w_refs, b_refs, o_refs = refs[:nseg], refs[nseg:nseg + nb], refs[nseg + nb:]
    xn = _rms(x_ref[...], g_ref[...]).astype(BF16)
    bi = 0
    for s in range(nseg):
        y = _dot(xn, w_refs[s][...])
        if bias_flags[s]:
            y = y + b_refs[bi][...]
            bi += 1
        o_refs[s][...] = y.astype(o_refs[s].dtype)


def _norm_matmul(x, g, weights, biases, out_dtypes, row_tile):
    R, D = x.shape
    tm = min(row_tile, R)
    assert R % tm == 0
    nseg = len(weights)
    bias_flags = tuple(b is not None for b in biases)
    const = lambda i: (0, 0)
    in_specs = [pl.BlockSpec((tm, D), lambda i: (i, 0)), pl.BlockSpec((1, D), const)]
    in_specs += [pl.BlockSpec(w.shape, const) for w in weights]
    in_specs += [pl.BlockSpec((1, b.shape[-1]), const) for b in biases if b is not None]
    out_specs = [pl.BlockSpec((tm, w.shape[1]), lambda i: (i, 0)) for w in weights]
    out_shape = [jax.ShapeDtypeStruct((R, w.shape[1]), dt) for w, dt in zip(weights, out_dtypes)]
    return pl.pallas_call(
        functools.partial(_norm_matmul_kernel, nseg=nseg, bias_flags=bias_flags),
        grid=(R // tm,), in_specs=in_specs, out_specs=out_specs, out_shape=out_shape,
        compiler_params=_params("parallel"), name="norm_matmul",
    )(x, g.reshape(1, D), *weights, *[b.reshape(1, -1) for b in biases if b is not None])


def _proj_in_kernel(x_ref, g_ref, wq_ref, wc_ref, wk_ref, wvT_ref, wgT_ref, bg_ref, wr_ref,
                    q_ref, kvc_ref, kaug_ref, vsT_ref, vwT_ref, gate_ref, rw_ref, *, seq_len):
    tm = x_ref.shape[0]
    xn = _rms(x_ref[...], g_ref[...]).astype(BF16)
    qT = (_dot_nt(wq_ref[...], xn) * (HEAD_DIM ** -0.5 * LOG2E)).astype(BF16)
    for j in range(tm // Q_BLOCK):
        for hg in range(NSA_HEADS):
            h, g = divmod(hg, NSA_GROUP)
            q_ref[0, j, h, :, g * Q_BLOCK:(g + 1) * Q_BLOCK] = qT[hg * HEAD_DIM:(hg + 1) * HEAD_DIM,
                                                                  j * Q_BLOCK:(j + 1) * Q_BLOCK]
    kvc_ref[...] = _dot(xn, wc_ref[...])
    rw_ref[...] = _dot(xn, wr_ref[...])
    k_all = _dot(xn, wk_ref[...])
    tok = lax.rem(pl.program_id(0) * tm, seq_len) + lax.broadcasted_iota(jnp.int32, k_all.shape, 0)
    lane = lax.broadcasted_iota(jnp.int32, k_all.shape, 1)
    blk = jnp.bitwise_and(jnp.right_shift(tok, SEL_SHIFT), SEL_STEP_BLOCKS - 1)
    hot = (jnp.bitwise_and(lane, LANES - 1) == HEAD_DIM + blk) & (lane < NSA_KV_HEADS * LANES)
    kaug_ref[...] = jnp.where(hot, 1.0, k_all).astype(BF16)
    vT = _dot_nt(wvT_ref[...], xn)
    row = lax.broadcasted_iota(jnp.int32, vT.shape, 0)
    ones_row = row == HEAD_DIM
    for grp in range(1, 2 * NSA_KV_HEADS):
        ones_row = ones_row | (row == grp * V_ROWS + HEAD_DIM)
    vT = jnp.where(ones_row, 1.0, vT).astype(BF16)
    half = NSA_KV_HEADS * V_ROWS
    vsT_ref[0, 0] = vT[0:half]
    for c in range(tm // KEY_CHUNK):
        vwT_ref[0, c] = vT[half:, c * KEY_CHUNK:(c + 1) * KEY_CHUNK]
    gT = _dot_nt(wgT_ref[...], xn) + bg_ref[...]
    for j in range(tm // Q_BLOCK):
        for h in range(NSA_KV_HEADS):
            gate_ref[0, j, h] = gT[h * GATE_ROWS:(h + 1) * GATE_ROWS, j * Q_BLOCK:(j + 1) * Q_BLOCK]


def _proj_in(x, norm_g, w_in, gate_b):
    B, T, D = x.shape
    Hkv, G, dh = NSA_KV_HEADS, NSA_GROUP, HEAD_DIM
    tm = SEL_STAGE_TILES * KEY_CHUNK
    assert T % tm == 0 and tm % Q_BLOCK == 0
    kv0 = NSA_WIDTH
    g0 = kv0 + 6 * KV_WIDTH
    stream = lambda s: w_in[:, kv0 + s * KV_WIDTH:kv0 + (s + 1) * KV_WIDTH].reshape(D, Hkv, dh)
    pad_cols = lambda w: jnp.pad(w, ((0, 0), (0, 0), (0, LANES - dh))).reshape(D, Hkv * LANES)
    pad_rows = lambda w: jnp.pad(w.transpose(1, 2, 0), ((0, 0), (0, V_ROWS - dh), (0, 0))).reshape(Hkv * V_ROWS, D)
    w_k = jnp.concatenate([pad_cols(stream(2)), pad_cols(stream(4))], axis=1)
    w_vT = jnp.concatenate([pad_rows(stream(3)), pad_rows(stream(5))], axis=0)
    reorder = lambda a: a.reshape(-1, Hkv, G, 3).transpose(1, 3, 2, 0).reshape(Hkv, 3 * G, -1)
    pad_gate = lambda a: jnp.pad(a, ((0, 0), (0, GATE_ROWS - 3 * G), (0, 0))).reshape(Hkv * GATE_ROWS, -1)
    w_gT = pad_gate(reorder(w_in[:, g0:NSA_COLS]))
    b_g = pad_gate(reorder(gate_b.reshape(1, -1)))
    weights = [w_in[:, :kv0].T, w_in[:, kv0:kv0 + 2 * KV_WIDTH], w_k, w_vT, w_gT]
    weights = [w.astype(BF16) for w in weights] + [b_g, w_in[:, NSA_COLS:].astype(BF16)]
    nt = T // tm
    rows = lambda n: pl.BlockSpec((tm, n), lambda i: (i, 0))
    const = lambda i: (0, 0)
    return pl.pallas_call(
        functools.partial(_proj_in_kernel, seq_len=T),
        grid=(B * nt,),
        in_specs=[rows(D), pl.BlockSpec((1, D), const)] + [pl.BlockSpec(w.shape, const) for w in weights],
        out_specs=[pl.BlockSpec((1, tm // Q_BLOCK, Hkv, dh, G * Q_BLOCK), lambda i: (i // nt, i % nt, 0, 0, 0)),
                   rows(2 * KV_WIDTH), rows(2 * Hkv * LANES),
                   pl.BlockSpec((1, 1, Hkv * V_ROWS, tm), lambda i: (i // nt, i % nt, 0, 0)),
                   pl.BlockSpec((1, tm // KEY_CHUNK, Hkv * V_ROWS, KEY_CHUNK), lambda i: (i // nt, i % nt, 0, 0)),
                   pl.BlockSpec((1, tm // Q_BLOCK, Hkv, GATE_ROWS, Q_BLOCK), lambda i: (i // nt, i % nt, 0, 0, 0)),
                   rows(RWKV_COLS)],
        out_shape=[jax.ShapeDtypeStruct((B, T // Q_BLOCK, Hkv, dh, G * Q_BLOCK), BF16),
                   jax.ShapeDtypeStruct((B * T, 2 * KV_WIDTH), F32),
                   jax.ShapeDtypeStruct((B * T, 2 * Hkv * LANES), BF16),
                   jax.ShapeDtypeStruct((B, nt, Hkv * V_ROWS, tm), BF16),
                   jax.ShapeDtypeStruct((B, T // KEY_CHUNK, Hkv * V_ROWS, KEY_CHUNK), BF16),
                   jax.ShapeDtypeStruct((B, T // Q_BLOCK, Hkv, GATE_ROWS, Q_BLOCK), F32),
                   jax.ShapeDtypeStruct((B * T, RWKV_COLS), F32)],
        compiler_params=_params("parallel"), name="proj_in",
    )(x.reshape(B * T, D), norm_g.reshape(1, D), *weights)


def _compress_kernel(x_ref, pe_ref, w1_ref, b1_ref, w2_ref, o_ref, *, transpose_out):
    n16 = x_ref.shape[1] // CMP_STRIDE
    hidden = w1_ref.shape[2] // NSA_KV_HEADS
    lo = jnp.zeros((n16, w1_ref.shape[2]), F32)
    hi = jnp.zeros((n16, w1_ref.shape[2]), F32)
    for l in range(CMP_STRIDE):
        rows = x_ref[0, pl.ds(l, n16, stride=CMP_STRIDE), :]
        lo = lo + _dot((rows + pe_ref[l:l + 1, :]).astype(BF16), w1_ref[l])
        hi = hi + _dot((rows + pe_ref[CMP_STRIDE + l:CMP_STRIDE + l + 1, :]).astype(BF16), w1_ref[CMP_STRIDE + l])
    h = lo + pltpu.roll(hi, n16 - 1, axis=0) + b1_ref[...]
    h = jax.nn.gelu(h).astype(BF16)
    for hkv in range(NSA_KV_HEADS):
        hh = h[:, hkv * hidden:(hkv + 1) * hidden]
        if transpose_out:
            o_ref[0, hkv] = _dot_nt(w2_ref[...], hh).astype(o_ref.dtype)
        else:
            o_ref[0, hkv] = _dot(hh, w2_ref[...]).astype(o_ref.dtype)


def _compress(kvc, stream, pe, w1, b1, w2, transpose_out):
    B, T, _ = kvc.shape
    H, dh = NSA_KV_HEADS, HEAD_DIM
    n16 = T // CMP_STRIDE
    hidden = w1.shape[1]
    eye = jnp.eye(H, dtype=w1.dtype)
    w1_bd = jnp.einsum('ldn,hg->lhdgn', w1.reshape(CMP_LEN, dh, hidden), eye).reshape(CMP_LEN, H * dh, H * hidden)
    w2b = (w2.T if transpose_out else w2).astype(BF16)
    oshape = (B, H, dh, n16) if transpose_out else (B, H, n16, dh)
    return pl.pallas_call(
        functools.partial(_compress_kernel, transpose_out=transpose_out),
        grid=(B,),
        in_specs=[pl.BlockSpec((1, T, H * dh), lambda b: (b, 0, stream)),
                  pl.BlockSpec((CMP_LEN, H * dh), lambda b: (0, 0)),
                  pl.BlockSpec(w1_bd.shape, lambda b: (0, 0, 0)),
                  pl.BlockSpec((1, H * hidden), lambda b: (0, 0)),
                  pl.BlockSpec(w2b.shape, lambda b: (0, 0))],
        out_specs=pl.BlockSpec((1,) + oshape[1:], lambda b: (b, 0, 0, 0)),
        out_shape=jax.ShapeDtypeStruct(oshape, BF16),
        compiler_params=_params("parallel"), name="nsa_compress",
    )(kvc, jnp.tile(pe, (1, H)), w1_bd.astype(BF16), jnp.tile(b1.reshape(1, hidden), (1, H)), w2b)


def _bias_of_distance(tab_ref, h, d):
    val = jnp.full(d.shape, tab_ref[h, 0], F32)
    for k in range(1, N_BUCKETS):
        val = jnp.where(d >= T5_THRESHOLDS[k], tab_ref[h, k], val)
    return val * LOG2E


def _bias_tiles_kernel(tab_ref, bt_ref, cb_ref, *, n_cmp_pad):
    hkv = pl.program_id(0)
    j = lax.broadcasted_iota(jnp.int32, (KEY_CHUNK, Q_BLOCK), 0)
    i = lax.broadcasted_iota(jnp.int32, (KEY_CHUNK, Q_BLOCK), 1)
    r2 = lax.broadcasted_iota(jnp.int32, (2 * n_cmp_pad, Q_BLOCK), 0)
    i2 = lax.broadcasted_iota(jnp.int32, (2 * n_cmp_pad, Q_BLOCK), 1)
    l2 = r2 - (n_cmp_pad - KEY_CHUNK)
    d2 = i2 - CMP_STRIDE * l2 + (CMP_STRIDE * KEY_CHUNK - Q_BLOCK - (CMP_LEN - 1))
    hidden2 = (l2 >= KEY_CHUNK) | ((l2 >= 0) & (d2 < 0))
    d2 = jnp.where((l2 >= 0) & (l2 < KEY_CHUNK), d2, 2 * MAX_DISTANCE)
    for g in range(NSA_GROUP):
        h = hkv * NSA_GROUP + g
        lanes = slice(g * Q_BLOCK, (g + 1) * Q_BLOCK)
        for m in range(N_BIAS_TILES):
            tile = _bias_of_distance(tab_ref, h, m * KEY_CHUNK + i - j)
            if m == 0:
                tile = jnp.where(j <= i, tile, MASK_SCORE)
            bt_ref[0, m, :, lanes] = tile
        bt_ref[0, TILE_MASKED, :, lanes] = jnp.full((KEY_CHUNK, Q_BLOCK), MASK_SCORE, F32)
        edge = _bias_of_distance(tab_ref, h, WINDOW + i - j)
        bt_ref[0, TILE_WINDOW_EDGE, :, lanes] = jnp.where(j > i, edge, MASK_SCORE)
        cb_ref[0, :, lanes] = jnp.where(hidden2, MASK_SCORE, _bias_of_distance(tab_ref, h, d2))


def _bias_tiles(rel_bias, n_cmp_pad):
    assert CMP_STRIDE * KEY_CHUNK - Q_BLOCK - (CMP_LEN - 1) >= T5_THRESHOLDS[-1]
    GQ = NSA_GROUP * Q_BLOCK
    return pl.pallas_call(
        functools.partial(_bias_tiles_kernel, n_cmp_pad=n_cmp_pad),
        grid=(NSA_KV_HEADS,),
        in_specs=[pl.BlockSpec(memory_space=pltpu.SMEM)],
        out_specs=[pl.BlockSpec((1, N_ALL_TILES, KEY_CHUNK, GQ), lambda h: (h, 0, 0, 0)),
                   pl.BlockSpec((1, 2 * n_cmp_pad, GQ), lambda h: (h, 0, 0))],
        out_shape=[jax.ShapeDtypeStruct((NSA_KV_HEADS, N_ALL_TILES, KEY_CHUNK, GQ), F32),
                   jax.ShapeDtypeStruct((NSA_KV_HEADS, 2 * n_cmp_pad, GQ), F32)],
        compiler_params=_params("parallel"), name="t5_bias_tiles",
    )(rel_bias.T)


def _nsa_kernel(q_ref, kc_ref, vcT_ref, ks_ref, vsT_ref, kw_ref, vwT_ref, gate_ref, ovT_ref, bt_ref, cb_ref,
                o_ref, acc_sc, qaug_sc, seladd_sc, s0_sc, s1_sc, p0_sc, p1_sc, *, n_sel, n_cmp_pad):
    G, NQ = NSA_GROUP, NSA_STEP_QBLOCKS
    GQ = G * Q_BLOCK
    qbs = [pl.program_id(2) * NQ + x for x in range(NQ)]
    per_block = lambda fn: jnp.concatenate([fn(x) for x in range(NQ)], axis=1)
    tile_g = lambda a: jnp.concatenate([a] * G, axis=1)
    qT = per_block(lambda x: q_ref[0, x, 0])
    qaug_sc[0:HEAD_DIM, :] = qT
    qaug_sc[HEAD_DIM:, :] = jnp.zeros((qaug_sc.shape[0] - HEAD_DIM, NQ * GQ), BF16)
    lane_q = lax.broadcasted_iota(jnp.int32, (1, Q_BLOCK), 1)
    t = per_block(lambda x: qbs[x] * Q_BLOCK + lane_q)

    def bias_tile(dist_of_block):
        def one(x):
            dist = dist_of_block(x)
            return bt_ref[0, jnp.where(dist < 0, TILE_MASKED, jnp.minimum(dist, N_BIAS_TILES - 1))]
        return per_block(one)

    n_back = WINDOW // KEY_CHUNK
    q_win = qaug_sc[...]
    win = {}

    def win_scores(x, back):
        kc = jnp.maximum(qbs[x] - back, 0)
        edge = TILE_WINDOW_EDGE if back == n_back else back
        tile = jnp.where(qbs[x] >= back, edge, TILE_MASKED)
        k_chunk = kw_ref[0, pl.ds(pl.multiple_of(kc * KEY_CHUNK, KEY_CHUNK), KEY_CHUNK), :]
        win["s", x, back] = _dot(k_chunk, q_win[:, x * GQ:(x + 1) * GQ]) + bt_ref[0, tile]
        col_max = jnp.max(win["s", x, back], axis=0, keepdims=True)
        win["m", x] = jnp.maximum(win["m", x], col_max) if ("m", x) in win else col_max

    def win_weights(x, back):
        win["p", x, back] = jnp.exp2(win["s", x, back] - win["m", x]).astype(BF16)

    def win_values(x, back):
        pv = _dot(vwT_ref[0, jnp.maximum(qbs[x] - back, 0)], win["p", x, back])
        win["acc", x] = win["acc", x] + pv if ("acc", x) in win else pv

    backs = list(range(n_back, -1, -1))
    window_work = [functools.partial(fn, x, b) for x in range(NQ)
                   for fn in (win_scores, win_weights, win_values) for b in backs]

    def cmp_bias(x):
        start = pl.multiple_of(n_cmp_pad - (Q_BLOCK // CMP_STRIDE) * (qbs[x] + 1), SUBLANES)
        return cb_ref[0, pl.ds(start, n_cmp_pad), :]
    s = _dot(kc_ref[0, 0], qT) + per_block(cmp_bias)
    e = jnp.exp2(s - jnp.maximum(jnp.max(s, axis=0, keepdims=True), 0.1 * MASK_SCORE))
    p = e * (1.0 / jnp.maximum(jnp.sum(e, axis=0, keepdims=True), 1e-30))
    o_c = _dot(vcT_ref[0, 0], p.astype(BF16))
    psum = per_block(lambda x: sum(p[:, x * GQ + g * Q_BLOCK:x * GQ + (g + 1) * Q_BLOCK] for g in range(G)))

    imp = _dot_f32_rhs(ovT_ref[...], psum)
    bj = lax.broadcasted_iota(jnp.int32, (n_sel, NQ * Q_BLOCK), 0)
    cur = jnp.right_shift(t, SEL_SHIFT)
    forced = (bj == 0) | (bj == cur) | (bj == cur - 1)
    valid = bj * SEL_BLOCK <= t
    score = jnp.where(forced, -jnp.inf, jnp.where(valid, imp, NEG_SCORE))
    bjf = bj.astype(F32)
    n_rounds = max(min(SEL_TOP, n_sel) - 3, 0)
    for rnd in range(n_rounds):
        mx = jnp.max(score, axis=0, keepdims=True)
        first = jnp.min(jnp.where(score == mx, bjf, float(n_sel)), axis=0, keepdims=True)
        score = jnp.where(bjf == first, -jnp.inf, score)
        take = -(-len(window_work) // (n_rounds - rnd))
        for piece in window_work[:take]:
            piece()
        window_work = window_work[take:]
    for piece in window_work:
        piece()
    o_w = per_block(lambda x: win["acc", x][0:HEAD_DIM] / jnp.maximum(win["acc", x][HEAD_DIM:HEAD_DIM + 1], 1e-30))
    sel_add = jnp.where(score == -jnp.inf, 0.0, MASK_SCORE)
    seladd_sc[...] = per_block(lambda x: tile_g(sel_add[:, x * Q_BLOCK:(x + 1) * Q_BLOCK]))

    stage_keys = SEL_STAGE_TILES * KEY_CHUNK
    n_stages = qbs[-1] // SEL_STAGE_TILES + 1
    last_stage = ks_ref.shape[1] // stage_keys - 1

    def scores(k, s_buf):
        kk = jnp.minimum(k, last_stage)
        blk0 = pl.multiple_of(kk // SEL_GROUP_STAGES * SEL_STEP_BLOCKS, SEL_STEP_BLOCKS)
        qaug_sc[HEAD_DIM:HEAD_DIM + SEL_STEP_BLOCKS, :] = seladd_sc[pl.ds(blk0, SEL_STEP_BLOCKS), :].astype(BF16)
        k0 = pl.multiple_of(kk * stage_keys, stage_keys)
        bias = jnp.concatenate([bias_tile(lambda x, c=c: qbs[x] - (k * SEL_STAGE_TILES + c))
                                for c in range(SEL_STAGE_TILES)], axis=0)
        s = _dot(ks_ref[0, pl.ds(k0, stage_keys), :], qaug_sc[...]) + bias
        s_buf[...] = s
        return jnp.max(s, axis=0, keepdims=True)

    def weights(m, col_max, s_buf, p_buf):
        m_new = jnp.maximum(m, col_max)
        p_buf[...] = jnp.exp2(s_buf[...] - m_new).astype(BF16)
        return m_new, jnp.exp2(m - m_new)

    def accumulate(k, alpha, p_buf):
        acc_sc[...] = alpha * acc_sc[...] + _dot(vsT_ref[0, jnp.clip(k, 0, last_stage)], p_buf[...])

    def pair(j, carry):
        m, col_max, alpha = carry
        k = 2 * j
        col_max1 = scores(k + 1, s1_sc)
        m, alpha0 = weights(m, col_max, s0_sc, p0_sc)
        accumulate(k - 1, alpha, p1_sc)
        col_max2 = scores(k + 2, s0_sc)
        m, alpha1 = weights(m, col_max1, s1_sc, p1_sc)
        accumulate(k, alpha0, p0_sc)
        return m, col_max2, alpha1

    acc_sc[...] = jnp.zeros(acc_sc.shape, F32)
    p1_sc[...] = jnp.zeros(p1_sc.shape, BF16)
    m_init = jnp.full((1, NQ * GQ), 0.1 * MASK_SCORE, F32)
    carry = (m_init, scores(0, s0_sc), jnp.ones((1, NQ * GQ), F32))
    n_pairs = (n_stages + 1) // 2
    _, _, alpha = lax.fori_loop(0, n_pairs, pair, carry)
    accumulate(2 * n_pairs - 1, alpha, p1_sc)
    o_s = acc_sc[0:HEAD_DIM, :] / jnp.maximum(acc_sc[HEAD_DIM:HEAD_DIM + 1, :], 1e-30)

    def gate_row(br):
        def one(x):
            gt = jax.nn.sigmoid(gate_ref[0, x, 0])
            return jnp.concatenate([gt[br * G + g:br * G + g + 1, :] for g in range(G)], axis=1)
        return per_block(one)
    o = gate_row(0) * o_c + gate_row(1) * o_s + gate_row(2) * o_w
    for x in range(NQ):
        o_ref[0, x * Q_BLOCK:(x + 1) * Q_BLOCK, :] = jnp.concatenate(
            [o[:, x * GQ + g * Q_BLOCK:x * GQ + (g + 1) * Q_BLOCK].T for g in range(G)], axis=1)


def _nsa_attention(q, kc, vcT, kaug, vsT, vwT, gates, overlapT, bias_tiles, cmp_bias):
    B, T, _ = kaug.shape
    Hkv, G = NSA_KV_HEADS, NSA_GROUP
    GQ = G * Q_BLOCK
    nqb = T // Q_BLOCK
    n_cmp_pad = kc.shape[2]
    n_sel = T // SEL_BLOCK
    half_keys = SEL_STAGE_TILES * KEY_CHUNK
    NQ = NSA_STEP_QBLOCKS
    assert T % (SEL_GROUP_STAGES * half_keys) == 0 and nqb % NQ == 0
    per_head = lambda b, h, i: (b, h, 0, 0)
    v_rows = V_ROWS
    chunked = lambda w: pl.BlockSpec((1, T // w, v_rows, w), lambda b, h, i: (b, 0, h, 0))
    step_lanes = NQ * GQ
    return pl.pallas_call(
        functools.partial(_nsa_kernel, n_sel=n_sel, n_cmp_pad=n_cmp_pad),
        grid=(B, Hkv, nqb // NQ),
        in_specs=[pl.BlockSpec((1, NQ, 1, HEAD_DIM, GQ), lambda b, h, i: (b, i, h, 0, 0)),
                  pl.BlockSpec((1, 1, n_cmp_pad, HEAD_DIM), per_head),
                  pl.BlockSpec((1, 1, HEAD_DIM, n_cmp_pad), per_head),
                  pl.BlockSpec((1, T, LANES), lambda b, h, i: (b, 0, h)), chunked(half_keys),
                  pl.BlockSpec((1, T, LANES), lambda b, h, i: (b, 0, Hkv + h)), chunked(KEY_CHUNK),
                  pl.BlockSpec((1, NQ, 1, GATE_ROWS, Q_BLOCK), lambda b, h, i: (b, i, h, 0, 0)),
                  pl.BlockSpec(overlapT.shape, lambda b, h, i: (0, 0)),
                  pl.BlockSpec((1, N_ALL_TILES, KEY_CHUNK, GQ), lambda b, h, i: (h, 0, 0, 0)),
                  pl.BlockSpec((1, 2 * n_cmp_pad, GQ), lambda b, h, i: (h, 0, 0))],
        out_specs=pl.BlockSpec((1, NQ * Q_BLOCK, G * HEAD_DIM), lambda b, h, i: (b, i, h)),
        out_shape=jax.ShapeDtypeStruct((B, T, NSA_WIDTH), F32),
        scratch_shapes=[pltpu.VMEM((v_rows, step_lanes), F32), pltpu.VMEM((LANES, step_lanes), BF16),
                        pltpu.VMEM((n_sel, step_lanes), F32),
                        pltpu.VMEM((half_keys, step_lanes), F32), pltpu.VMEM((half_keys, step_lanes), F32),
                        pltpu.VMEM((half_keys, step_lanes), BF16), pltpu.VMEM((half_keys, step_lanes), BF16)],
        compiler_params=_params("parallel", "parallel", "arbitrary"), name="nsa_attention",
    )(q, kc, vcT, kaug, vsT, kaug, vwT, gates, overlapT, bias_tiles, cmp_bias)


def _rwkv_chunk_kernel(rw_ref, prev_ref, mu_ref, w0_ref, wup_ref, a0_ref, aup_ref, gup_ref, kk_ref, ka_ref,
                       rk_ref, seg_ref, tri_ref, q_ref, y0_ref, a_ref, d_ref, g_ref, bonus_ref):
    C, W, N = RWKV_CHUNK, RWKV_WIDTH, HEAD_DIM
    c = pl.program_id(1)
    x = rw_ref[0]
    R = x.shape[0]
    chunk_rows = [slice(ck * C, (ck + 1) * C) for ck in range(R // C)]
    row = lax.broadcasted_iota(jnp.int32, (R, 1), 0)
    last_prev = jnp.where(c == 0, 0.0, prev_ref[0, SUBLANES - 1:SUBLANES, :])
    x_prev = jnp.where(row == 0, last_prev, pltpu.roll(x, 1, axis=0))
    xs = x + (x_prev - x) * mu_ref[...]
    r, k, v = xs[:, 0:W], xs[:, W:2 * W], xs[:, 2 * W:3 * W]
    o = 3 * W
    wd, ad, gd = xs[:, o:o + DECAY_LORA], xs[:, o + DECAY_LORA:o + DECAY_LORA + AAA_LORA], \
        xs[:, o + DECAY_LORA + AAA_LORA:]
    w_log = -jax.nn.softplus(-(w0_ref[...] + _dot(jnp.tanh(wd).astype(BF16), wup_ref[...]))) - 0.5
    lw = -jnp.exp(w_log)
    lr = jax.nn.sigmoid(a0_ref[...] + _dot(ad.astype(BF16), aup_ref[...]))
    g_ref[0] = _dot(jax.nn.sigmoid(gd).astype(BF16), gup_ref[...])
    kk = k * kk_ref[...]
    kk = kk * lax.rsqrt(jnp.maximum(_dot_f32_lhs(kk * kk, seg_ref[...]), 1e-24))
    k = k * (1.0 + (lr - 1.0) * ka_ref[...])
    bonus_ref[0] = _dot_f32_lhs(r * k * rk_ref[...], seg_ref[...]) * v
    a_vec, b_vec = -kk, kk * lr

    L = _dot_f32_rhs(tri_ref[...], lw)
    L_end = jnp.concatenate([jnp.broadcast_to(L[rs.stop - 1:rs.stop, :], (C, W)) for rs in chunk_rows], axis=0)
    e_neg = jnp.exp(-L)
    e_rem = jnp.exp(L_end - L)
    At, Bt, Kt, Rt = a_vec * jnp.exp(L - lw), b_vec * e_neg, k * e_neg, r * jnp.exp(L)
    Bg, Kg = b_vec * e_rem, k * e_rem
    decay_end = [jnp.exp(L[rs.stop - 1:rs.stop, :]) for rs in chunk_rows]

    ri = lax.broadcasted_iota(jnp.int32, (C, C), 0)
    cj = lax.broadcasted_iota(jnp.int32, (C, C), 1)
    strict, incl = ri > cj, ri >= cj
    eye_c = (ri == cj).astype(F32)
    eye_n = (lax.broadcasted_iota(jnp.int32, (N, N), 0) == lax.broadcasted_iota(jnp.int32, (N, N), 1)).astype(F32)
    items = [(ck, h) for ck in range(len(chunk_rows)) for h in range(RWKV_HEADS)]
    sl = [(chunk_rows[ck], slice(h * N, (h + 1) * N)) for ck, h in items]
    ar = [jnp.concatenate([At[s], Rt[s]], axis=0) for s in sl]
    zb = [_dot_nt(a, Bt[s]) for a, s in zip(ar, sl)]
    zk = [_dot_nt(a, Kt[s]) for a, s in zip(ar, sl)]
    n_mat = [jnp.where(strict, z[0:C], 0.0) for z in zb]
    m_mat = [jnp.where(strict, z[0:C], 0.0) for z in zk]
    mv = [_dot(m, v[s]) for m, s in zip(m_mat, sl)]
    t_inv, n_pow = [eye_c + n for n in n_mat], n_mat
    for _ in range(int(math.log2(C)) - 1):
        n_pow = [_dot(n, n) for n in n_pow]
        t_inv = [t + _dot(t, n) for t, n in zip(t_inv, n_pow)]
    ta = [_dot(t, At[s]) for t, s in zip(t_inv, sl)]
    g0 = [_dot(t, m) for t, m in zip(t_inv, mv)]
    pb = [jnp.concatenate([jnp.where(incl, z[C:], 0.0), Bg[s].T], axis=0) for z, s in zip(zb, sl)]
    pkk = [jnp.concatenate([jnp.where(incl, z[C:], 0.0), Kg[s].T], axis=0) for z, s in zip(zk, sl)]
    x_ta = [_dot(p, t) for p, t in zip(pb, ta)]
    x_g0 = [_dot(p, g) for p, g in zip(pb, g0)]
    x_v = [_dot(p, v[s]) for p, s in zip(pkk, sl)]
    for n, (ck, h) in enumerate(items):
        rows, lanes = sl[n]
        q_ref[0, rows, lanes] = Rt[sl[n]] + x_ta[n][0:C]
        y0_ref[0, rows, lanes] = x_g0[n][0:C] + x_v[n][0:C]
        a_ref[0, ck, h] = x_ta[n][C:] + eye_n * decay_end[ck][:, lanes]
        d_ref[0, ck, h] = x_g0[n][C:] + x_v[n][C:]


def _rwkv_chunks(rw, mu, w0, w_up, a0, a_up, g_up, k_k, k_a, r_k):
    B, T, cols = rw.shape
    C, W, H, N = RWKV_CHUNK, RWKV_WIDTH, RWKV_HEADS, HEAD_DIM
    nc = T // C
    S = RWKV_STEP_CHUNKS
    R = S * C
    assert nc % S == 0
    seg = jnp.asarray(np.kron(np.eye(H), np.ones((N, N))), BF16)
    tri = jnp.asarray(np.kron(np.eye(S), np.tril(np.ones((C, C)))), BF16)
    row = lambda z: z.reshape(1, -1)
    const = lambda b, c: (0, 0)
    vec = pl.BlockSpec((1, W), const)
    tok = pl.BlockSpec((1, R, W), lambda b, c: (b, c, 0))
    mat = pl.BlockSpec((1, S, H, N, N), lambda b, c: (b, c, 0, 0, 0))
    return pl.pallas_call(
        _rwkv_chunk_kernel,
        grid=(B, nc // S),
        in_specs=[pl.BlockSpec((1, R, cols), lambda b, c: (b, c, 0)),
                  pl.BlockSpec((1, SUBLANES, cols), lambda b, c: (b, jnp.maximum(c * (R // SUBLANES) - 1, 0), 0)),
                  pl.BlockSpec((1, cols), const), vec,
                  pl.BlockSpec((DECAY_LORA, W), const), vec,
                  pl.BlockSpec((AAA_LORA, W), const),
                  pl.BlockSpec((GATE_LORA, W), const), vec, vec, vec,
                  pl.BlockSpec((W, W), const), pl.BlockSpec((R, R), const)],
        out_specs=[tok, tok, mat, mat, tok, tok],
        out_shape=[jax.ShapeDtypeStruct((B, T, W), F32), jax.ShapeDtypeStruct((B, T, W), F32),
                   jax.ShapeDtypeStruct((B, nc, H, N, N), F32), jax.ShapeDtypeStruct((B, nc, H, N, N), F32),
                   jax.ShapeDtypeStruct((B, T, W), F32), jax.ShapeDtypeStruct((B, T, W), F32)],
        compiler_params=_params("parallel", "parallel"), name="rwkv_chunks",
    )(rw, rw, row(mu), row(w0), w_up.astype(BF16), row(a0), a_up.astype(BF16), g_up.astype(BF16),
      row(k_k), row(k_a), row(r_k), seg, tri)


def _rwkv_scan_kernel(a_ref, d_ref, q_ref, y0_ref, g_ref, bonus_ref, lw_ref, lb_ref, o_ref, h_sc):
    N = HEAD_DIM
    @pl.when(pl.program_id(0) == 0)
    def _():
        h_sc[...] = jnp.zeros(h_sc.shape, F32)

    for b in range(q_ref.shape[0]):
        for h in range(RWKV_HEADS):
            hs = slice(h * N, (h + 1) * N)
            state = h_sc[b, h]
            y = _dot(q_ref[b, :, hs], state, precision=HIGHEST) + y0_ref[b, :, hs]
            h_sc[b, h] = _dot(a_ref[b, 0, h], state, precision=HIGHEST) + d_ref[b, 0, h]
            mean = jnp.mean(y, axis=-1, keepdims=True)
            var = jnp.mean(jnp.square(y - mean), axis=-1, keepdims=True)
            yn = (y - mean) * lax.rsqrt(var + LNX_EPS)
            yn = yn * lw_ref[:, hs] + lb_ref[:, hs]
            o_ref[b, :, hs] = (yn + bonus_ref[b, :, hs]) * g_ref[b, :, hs]


def _rwkv_scan(A, D, Q, Y0, g, bonus, lnx_w, lnx_b):
    B, nc, H, N, _ = A.shape
    T, W, C = Q.shape[1], Q.shape[2], RWKV_CHUNK
    tok = pl.BlockSpec((B, C, W), lambda c: (0, c, 0))
    mat = pl.BlockSpec((B, 1, H, N, N), lambda c: (0, c, 0, 0, 0))
    vec = pl.BlockSpec((1, W), lambda c: (0, 0))
    return pl.pallas_call(
        _rwkv_scan_kernel,
        grid=(nc,),
        in_specs=[mat, mat, tok, tok, tok, tok, vec, vec],
        out_specs=tok,
        out_shape=jax.ShapeDtypeStruct((B, T, W), F32),
        scratch_shapes=[pltpu.VMEM((B, H, N, N), F32)],
        compiler_params=_params("arbitrary"), name="rwkv_scan",
    )(A, D, Q, Y0, g, bonus, lnx_w.reshape(1, W), lnx_b.reshape(1, W))


def _mix_xattn_kernel(x_ref, on_ref, or_ref, wo1_ref, wo2_ref, gx_ref, wq_ref, k_ref, v_ref, wo_ref, o_ref):
    x1 = x_ref[0] + _dot(on_ref[0].astype(BF16), wo1_ref[...]) + _dot(or_ref[0].astype(BF16), wo2_ref[...])
    q = _dot(_rms(x1, gx_ref[...]).astype(BF16), wq_ref[...])
    dh = q.shape[-1] // XATTN_HEADS
    qbf = (q * (dh ** -0.5)).astype(BF16)
    outs = []
    for h in range(XATTN_HEADS):
        hs = slice(h * dh, (h + 1) * dh)
        s = _dot_nt(qbf[:, hs], k_ref[0, :, hs])
        e = jnp.exp(s - jnp.max(s, axis=-1, keepdims=True))
        p = e / jnp.sum(e, axis=-1, keepdims=True)
        outs.append(_dot(p.astype(BF16), v_ref[0, :, hs]))
    o = jnp.concatenate(outs, axis=1).astype(BF16)
    o_ref[0] = x1 + _dot(o, wo_ref[...])


def _mix_xattn(x, o_nsa, o_rwkv, w_out, norm_x_g, w_q, mem_k, mem_v, w_o):
    B, T, D = x.shape
    M = mem_k.shape[1]
    tm = min(ROW_TILE, T)
    wo1, wo2 = w_out[:NSA_WIDTH].astype(BF16), w_out[NSA_WIDTH:].astype(BF16)
    const = lambda b, i: (0, 0)
    tile = lambda w: pl.BlockSpec((1, tm, w), lambda b, i: (b, i, 0))
    return pl.pallas_call(
        _mix_xattn_kernel,
        grid=(B, T // tm),
        in_specs=[tile(D), tile(NSA_WIDTH), tile(RWKV_WIDTH),
                  pl.BlockSpec(wo1.shape, const), pl.BlockSpec(wo2.shape, const),
                  pl.BlockSpec((1, D), const), pl.BlockSpec((D, D), const),
                  pl.BlockSpec((1, M, D), lambda b, i: (b, 0, 0)), pl.BlockSpec((1, M, D), lambda b, i: (b, 0, 0)),
                  pl.BlockSpec((D, D), const)],
        out_specs=tile(D),
        out_shape=jax.ShapeDtypeStruct((B, T, D), F32),
        compiler_params=_params("parallel", "parallel"), name="mix_xattn",
    )(x, o_nsa, o_rwkv, wo1, wo2, norm_x_g.reshape(1, D), w_q.astype(BF16), mem_k, mem_v, w_o.astype(BF16))


def _ffn_kernel(x_ref, g_ref, wg_ref, wu_ref, wd_ref, gf_ref, o_ref, *, final_norm):
    x = x_ref[...]
    h = _rms(x, g_ref[...]).astype(BF16)
    act = (jax.nn.silu(_dot(h, wg_ref[...])) * _dot(h, wu_ref[...])).astype(BF16)
    y = x + _dot(act, wd_ref[...])
    o_ref[...] = _rms(y, gf_ref[...]) if final_norm else y


def _ffn(x, norm_g, w_gate, w_up, w_down, final_g, final_norm):
    R, D = x.shape
    F = w_gate.shape[1]
    tm = min(FFN_ROW_TILE, R)
    const = lambda i: (0, 0)
    resident = lambda shape: pl.BlockSpec(shape, const, pipeline_mode=pl.Buffered(1))
    return pl.pallas_call(
        functools.partial(_ffn_kernel, final_norm=final_norm),
        grid=(R // tm,),
        in_specs=[pl.BlockSpec((tm, D), lambda i: (i, 0)), pl.BlockSpec((1, D), const),
                  resident((D, F)), resident((D, F)), resident((F, D)), pl.BlockSpec((1, D), const)],
        out_specs=pl.BlockSpec((tm, D), lambda i: (i, 0)),
        out_shape=jax.ShapeDtypeStruct((R, D), F32),
        compiler_params=_params("parallel"), name="ffn",
    )(x, norm_g.reshape(1, D), w_gate.astype(BF16), w_up.astype(BF16), w_down.astype(BF16),
      final_g.reshape(1, D))


def _overlap_matrix(n_cmp_pad, n_sel):
    c = np.arange(n_cmp_pad)[:, None] * CMP_STRIDE
    s = np.arange(n_sel)[None, :] * SEL_BLOCK
    return ((c <= s + SEL_BLOCK - 1) & (c + CMP_LEN - 1 >= s)).astype(np.float32)


def _layer(x, mem, rel_bias, final_g, is_last, norm_mix_g, w_in, nsa_gate_b, cmp_pe_k, cmp_pe_v,
           cmp_k_w1, cmp_k_b1, cmp_k_w2, cmp_v_w1, cmp_v_b1, cmp_v_w2,
           rwkv_mu, rwkv_w0, rwkv_w_up, rwkv_a0, rwkv_a_up, rwkv_g_up,
           rwkv_k_k, rwkv_k_a, rwkv_r_k, rwkv_lnx_w, rwkv_lnx_b, w_out,
           norm_x_g, norm_mem_g, w_q_x, w_kv_x, w_o_x, norm_ffn_g, w_gate, w_up, w_down):
    B, T, D = x.shape
    Hkv, G, dh = NSA_KV_HEADS, NSA_GROUP, HEAD_DIM
    q, kvc, kaug, vsT, vwT, gates, rw = _proj_in(x, norm_mix_g, w_in, nsa_gate_b)

    n16 = T // CMP_STRIDE
    kvc = kvc.reshape(B, T, 2 * KV_WIDTH)
    kc = _compress(kvc, 0, cmp_pe_k, cmp_k_w1, cmp_k_b1, cmp_k_w2, False)
    vcT = _compress(kvc, 1, cmp_pe_v, cmp_v_w1, cmp_v_b1, cmp_v_w2, True)
    bias_tiles, cmp_bias = _bias_tiles(rel_bias, n16)
    o_nsa = _nsa_attention(q, kc, vcT, kaug.reshape(B, T, 2 * Hkv * LANES), vsT, vwT,
                           gates, jnp.asarray(_overlap_matrix(n16, T // SEL_BLOCK).T, BF16), bias_tiles, cmp_bias)

    Q, Y0, A, Dm, g, bonus = _rwkv_chunks(rw.reshape(B, T, RWKV_COLS), rwkv_mu, rwkv_w0, rwkv_w_up, rwkv_a0,
                                          rwkv_a_up, rwkv_g_up, rwkv_k_k, rwkv_k_a, rwkv_r_k.reshape(-1))
    o_rwkv = _rwkv_scan(A, Dm, Q, Y0, g, bonus, rwkv_lnx_w, rwkv_lnx_b)

    M = mem.shape[1]
    (kv_mem,) = _norm_matmul(mem.reshape(B * M, D), norm_mem_g, [w_kv_x.astype(BF16)], [None], [BF16], ROW_TILE)
    kv_mem = kv_mem.reshape(B, M, 2 * D)
    x = _mix_xattn(x, o_nsa, o_rwkv, w_out, norm_x_g, w_q_x, kv_mem[..., :D], kv_mem[..., D:], w_o_x)
    x = _ffn(x.reshape(B * T, D), norm_ffn_g, w_gate, w_up, w_down, final_g, is_last)
    return x.reshape(B, T, D)


def kernel(x, mem, rel_bias, norm_f_g, norm_mix_g, w_in, nsa_gate_b, cmp_pe_k, cmp_pe_v, cmp_k_w1, cmp_k_b1, cmp_k_w2, cmp_v_w1, cmp_v_b1, cmp_v_w2, rwkv_mu, rwkv_w0, rwkv_w_up, rwkv_a0, rwkv_a_up, rwkv_g_up, rwkv_k_k, rwkv_k_a, rwkv_r_k, rwkv_lnx_w, rwkv_lnx_b, w_out, norm_x_g, norm_mem_g, w_q_x, w_kv_x, w_o_x, norm_ffn_g, w_gate, w_up, w_down):
    stacked = (norm_mix_g, w_in, nsa_gate_b, cmp_pe_k, cmp_pe_v, cmp_k_w1, cmp_k_b1, cmp_k_w2, cmp_v_w1,
               cmp_v_b1, cmp_v_w2, rwkv_mu, rwkv_w0, rwkv_w_up, rwkv_a0, rwkv_a_up, rwkv_g_up, rwkv_k_k,
               rwkv_k_a, rwkv_r_k, rwkv_lnx_w, rwkv_lnx_b, w_out, norm_x_g, norm_mem_g, w_q_x, w_kv_x, w_o_x,
               norm_ffn_g, w_gate, w_up, w_down)
    depth = w_in.shape[0]
    for l in range(depth):
        x = _layer(x, mem, rel_bias, norm_f_g, l == depth - 1, *[p[l] for p in stacked])
    return x
```

```python
import functools
import math

import numpy as np
import jax
import jax.numpy as jnp
from jax import lax
from jax.experimental import pallas as pl
from jax.experimental.pallas import tpu as pltpu

F32 = jnp.float32
BF16 = jnp.bfloat16
HIGHEST = lax.Precision.HIGHEST

LANES = 128
SUBLANES = 8
BF16_ROWS = 16
VMEM_LIMIT_BYTES = 56 * 1024 * 1024

HEAD_DIM = 64
NSA_HEADS = 8
NSA_KV_HEADS = 2
NSA_GROUP = NSA_HEADS // NSA_KV_HEADS
NSA_WIDTH = NSA_HEADS * HEAD_DIM
KV_WIDTH = NSA_KV_HEADS * HEAD_DIM
RWKV_HEADS = 8
RWKV_WIDTH = RWKV_HEADS * HEAD_DIM
CMP_LEN = 32
CMP_STRIDE = 16
SEL_BLOCK = 64
SEL_SHIFT = 6
SEL_TOP = 16
WINDOW = 512
Q_BLOCK = 128
DECAY_LORA = 64
AAA_LORA = 64
GATE_LORA = 128
N_BUCKETS = 32
MAX_DISTANCE = 2048
XATTN_HEADS = 4
RMS_EPS = 1e-6
LNX_EPS = 64e-5
FORCE_SCORE = 1e4
NEG_SCORE = -1e9
MASK_SCORE = -1e30
LOG2E = math.log2(math.e)
RWKV_COLS = 3 * RWKV_WIDTH + DECAY_LORA + AAA_LORA + GATE_LORA
NSA_COLS = NSA_WIDTH + 6 * KV_WIDTH + 3 * NSA_HEADS

KEY_CHUNK = 128
RWKV_CHUNK = 64
RWKV_STEP_CHUNKS = 4
ROW_TILE = 512
FFN_ROW_TILE = 256


def _t5_thresholds():
    d = np.arange(0, 2 * MAX_DISTANCE, dtype=np.int64)
    max_exact = N_BUCKETS // 2
    nf = np.maximum(d, 1).astype(np.float32)
    large = max_exact + (np.log(nf / np.float32(max_exact)) / np.float32(math.log(MAX_DISTANCE / max_exact))
                         * np.float32(N_BUCKETS - max_exact)).astype(np.int32)
    bucket = np.where(d < max_exact, d, np.minimum(large, N_BUCKETS - 1))
    return [int(np.argmax(bucket >= k)) for k in range(N_BUCKETS)]


T5_THRESHOLDS = _t5_thresholds()
N_BIAS_TILES = -(-(T5_THRESHOLDS[-1] + KEY_CHUNK) // KEY_CHUNK) + 1
TILE_MASKED = N_BIAS_TILES
TILE_WINDOW_EDGE = N_BIAS_TILES + 1
N_ALL_TILES = N_BIAS_TILES + 2
SEL_STEP_BLOCKS = 16
SEL_STAGE_TILES = 4
SEL_GROUP_STAGES = SEL_STEP_BLOCKS * SEL_BLOCK // (SEL_STAGE_TILES * KEY_CHUNK)
NSA_STEP_QBLOCKS = 4
V_ROWS = HEAD_DIM + BF16_ROWS
GATE_ROWS = 16


def _params(*semantics):
    return pltpu.CompilerParams(dimension_semantics=semantics, vmem_limit_bytes=VMEM_LIMIT_BYTES)


def _rms(x, g):
    return x * lax.rsqrt(jnp.mean(x * x, axis=-1, keepdims=True) + RMS_EPS) * g


def _dot(a, b, **kw):
    return jnp.dot(a, b, preferred_element_type=F32, **kw)


def _split3(x):
    hi = x.astype(BF16)
    r1 = x - hi.astype(F32)
    mid = r1.astype(BF16)
    lo = (r1 - mid.astype(F32)).astype(BF16)
    return hi, mid, lo


def _dot_f32_lhs(x, w01):
    w = w01.astype(BF16)
    hi, mid, lo = _split3(x)
    return _dot(hi, w) + (_dot(mid, w) + _dot(lo, w))


def _dot_f32_rhs(w01, x):
    w = w01.astype(BF16)
    hi, mid, lo = _split3(x)
    return _dot(w, hi) + (_dot(w, mid) + _dot(w, lo))


def _dot_nt(a, b, **kw):
    return lax.dot_general(a, b, (((1,), (1,)), ((), ())), preferred_element_type=F32, **kw)


def _norm_matmul_kernel(x_ref, g_ref, *refs, nseg, bias_flags):
    nb = sum(bias_flags)
    w_refs, b_refs, o_refs = refs[:nseg], refs[nseg:nseg + nb], refs[nseg + nb:]
    xn = _rms(x_ref[...], g_ref[...]).astype(BF16)
    bi = 0
    for s in range(nseg):
        y = _dot(xn, w_refs[s][...])
        if bias_flags[s]:
            y = y + b_refs[bi][...]
            bi += 1
        o_refs[s][...] = y.astype(o_refs[s].dtype)


def _norm_matmul(x, g, weights, biases, out_dtypes, row_tile):
    R, D = x.shape
    tm = min(row_tile, R)
    assert R % tm == 0
    nseg = len(weights)
    bias_flags = tuple(b is not None for b in biases)
    const = lambda i: (0, 0)
    in_specs = [pl.BlockSpec((tm, D), lambda i: (i, 0)), pl.BlockSpec((1, D), const)]
    in_specs += [pl.BlockSpec(w.shape, const) for w in weights]
    in_specs += [pl.BlockSpec((1, b.shape[-1]), const) for b in biases if b is not None]
    out_specs = [pl.BlockSpec((tm, w.shape[1]), lambda i: (i, 0)) for w in weights]
    out_shape = [jax.ShapeDtypeStruct((R, w.shape[1]), dt) for w, dt in zip(weights, out_dtypes)]
    return pl.pallas_call(
        functools.partial(_norm_matmul_kernel, nseg=nseg, bias_flags=bias_flags),
        grid=(R // tm,), in_specs=in_specs, out_specs=out_specs, out_shape=out_shape,
        compiler_params=_params("parallel"), name="norm_matmul",
    )(x, g.reshape(1, D), *weights, *[b.reshape(1, -1) for b in biases if b is not None])


def _proj_in_kernel(x_ref, g_ref, wq_ref, wc_ref, wk_ref, wvT_ref, wgT_ref, bg_ref, wr_ref,
                    q_ref, kvc_ref, kaug_ref, vsT_ref, vwT_ref, gate_ref, rw_ref, *, seq_len):
    tm = x_ref.shape[0]
    xn = _rms(x_ref[...], g_ref[...]).astype(BF16)
    qT = (_dot_nt(wq_ref[...], xn) * (HEAD_DIM ** -0.5 * LOG2E)).astype(BF16)
    for j in range(tm // Q_BLOCK):
        for hg in range(NSA_HEADS):
            h, g = divmod(hg, NSA_GROUP)
            q_ref[0, j, h, :, g * Q_BLOCK:(g + 1) * Q_BLOCK] = qT[hg * HEAD_DIM:(hg + 1) * HEAD_DIM,
                                                                  j * Q_BLOCK:(j + 1) * Q_BLOCK]
    kvc_ref[...] = _dot(xn, wc_ref[...])
    rw_ref[...] = _dot(xn, wr_ref[...])
    k_all = _dot(xn, wk_ref[...])
    tok = lax.rem(pl.program_id(0) * tm, seq_len) + lax.broadcasted_iota(jnp.int32, k_all.shape, 0)
    lane = lax.broadcasted_iota(jnp.int32, k_all.shape, 1)
    blk = jnp.bitwise_and(jnp.right_shift(tok, SEL_SHIFT), SEL_STEP_BLOCKS - 1)
    hot = (jnp.bitwise_and(lane, LANES - 1) == HEAD_DIM + blk) & (lane < NSA_KV_HEADS * LANES)
    kaug_ref[...] = jnp.where(hot, 1.0, k_all).astype(BF16)
    vT = _dot_nt(wvT_ref[...], xn)
    row = lax.broadcasted_iota(jnp.int32, vT.shape, 0)
    ones_row = row == HEAD_DIM
    for grp in range(1, 2 * NSA_KV_HEADS):
        ones_row = ones_row | (row == grp * V_ROWS + HEAD_DIM)
    vT = jnp.where(ones_row, 1.0, vT).astype(BF16)
    half = NSA_KV_HEADS * V_ROWS
    vsT_ref[0, 0] = vT[0:half]
    for c in range(tm // KEY_CHUNK):
        vwT_ref[0, c] = vT[half:, c * KEY_CHUNK:(c + 1) * KEY_CHUNK]
    gT = _dot_nt(wgT_ref[...], xn) + bg_ref[...]
    for j in range(tm // Q_BLOCK):
        for h in range(NSA_KV_HEADS):
            gate_ref[0, j, h] = gT[h * GATE_ROWS:(h + 1) * GATE_ROWS, j * Q_BLOCK:(j + 1) * Q_BLOCK]


def _proj_in(x, norm_g, w_in, gate_b):
    B, T, D = x.shape
    Hkv, G, dh = NSA_KV_HEADS, NSA_GROUP, HEAD_DIM
    tm = SEL_STAGE_TILES * KEY_CHUNK
    assert T % tm == 0 and tm % Q_BLOCK == 0
    kv0 = NSA_WIDTH
    g0 = kv0 + 6 * KV_WIDTH
    stream = lambda s: w_in[:, kv0 + s * KV_WIDTH:kv0 + (s + 1) * KV_WIDTH].reshape(D, Hkv, dh)
    pad_cols = lambda w: jnp.pad(w, ((0, 0), (0, 0), (0, LANES - dh))).reshape(D, Hkv * LANES)
    pad_rows = lambda w: jnp.pad(w.transpose(1, 2, 0), ((0, 0), (0, V_ROWS - dh), (0, 0))).reshape(Hkv * V_ROWS, D)
    w_k = jnp.concatenate([pad_cols(stream(2)), pad_cols(stream(4))], axis=1)
    w_vT = jnp.concatenate([pad_rows(stream(3)), pad_rows(stream(5))], axis=0)
    reorder = lambda a: a.reshape(-1, Hkv, G, 3).transpose(1, 3, 2, 0).reshape(Hkv, 3 * G, -1)
    pad_gate = lambda a: jnp.pad(a, ((0, 0), (0, GATE_ROWS - 3 * G), (0, 0))).reshape(Hkv * GATE_ROWS, -1)
    w_gT = pad_gate(reorder(w_in[:, g0:NSA_COLS]))
    b_g = pad_gate(reorder(gate_b.reshape(1, -1)))
    weights = [w_in[:, :kv0].T, w_in[:, kv0:kv0 + 2 * KV_WIDTH], w_k, w_vT, w_gT]
    weights = [w.astype(BF16) for w in weights] + [b_g, w_in[:, NSA_COLS:].astype(BF16)]
    nt = T // tm
    rows = lambda n: pl.BlockSpec((tm, n), lambda i: (i, 0))
    const = lambda i: (0, 0)
    return pl.pallas_call(
        functools.partial(_proj_in_kernel, seq_len=T),
        grid=(B * nt,),
        in_specs=[rows(D), pl.BlockSpec((1, D), const)] + [pl.BlockSpec(w.shape, const) for w in weights],
        out_specs=[pl.BlockSpec((1, tm // Q_BLOCK, Hkv, dh, G * Q_BLOCK), lambda i: (i // nt, i % nt, 0, 0, 0)),
                   rows(2 * KV_WIDTH), rows(2 * Hkv * LANES),
                   pl.BlockSpec((1, 1, Hkv * V_ROWS, tm), lambda i: (i // nt, i % nt, 0, 0)),
                   pl.BlockSpec((1, tm // KEY_CHUNK, Hkv * V_ROWS, KEY_CHUNK), lambda i: (i // nt, i % nt, 0, 0)),
                   pl.BlockSpec((1, tm // Q_BLOCK, Hkv, GATE_ROWS, Q_BLOCK), lambda i: (i // nt, i % nt, 0, 0, 0)),
                   rows(RWKV_COLS)],
        out_shape=[jax.ShapeDtypeStruct((B, T // Q_BLOCK, Hkv, dh, G * Q_BLOCK), BF16),
                   jax.ShapeDtypeStruct((B * T, 2 * KV_WIDTH), F32),
                   jax.ShapeDtypeStruct((B * T, 2 * Hkv * LANES), BF16),
                   jax.ShapeDtypeStruct((B, nt, Hkv * V_ROWS, tm), BF16),
                   jax.ShapeDtypeStruct((B, T // KEY_CHUNK, Hkv * V_ROWS, KEY_CHUNK), BF16),
                   jax.ShapeDtypeStruct((B, T // Q_BLOCK, Hkv, GATE_ROWS, Q_BLOCK), F32),
                   jax.ShapeDtypeStruct((B * T, RWKV_COLS), F32)],
        compiler_params=_params("parallel"), name="proj_in",
    )(x.reshape(B * T, D), norm_g.reshape(1, D), *weights)


def _compress_kernel(x_ref, pe_ref, w1_ref, b1_ref, w2_ref, o_ref, *, transpose_out):
    n16 = x_ref.shape[1] // CMP_STRIDE
    hidden = w1_ref.shape[2] // NSA_KV_HEADS
    lo = jnp.zeros((n16, w1_ref.shape[2]), F32)
    hi = jnp.zeros((n16, w1_ref.shape[2]), F32)
    for l in range(CMP_STRIDE):
        rows = x_ref[0, pl.ds(l, n16, stride=CMP_STRIDE), :]
        lo = lo + _dot((rows + pe_ref[l:l + 1, :]).astype(BF16), w1_ref[l])
        hi = hi + _dot((rows + pe_ref[CMP_STRIDE + l:CMP_STRIDE + l + 1, :]).astype(BF16), w1_ref[CMP_STRIDE + l])
    h = lo + pltpu.roll(hi, n16 - 1, axis=0) + b1_ref[...]
    h = jax.nn.gelu(h).astype(BF16)
    for hkv in range(NSA_KV_HEADS):
        hh = h[:, hkv * hidden:(hkv + 1) * hidden]
        if transpose_out:
            o_ref[0, hkv] = _dot_nt(w2_ref[...], hh).astype(o_ref.dtype)
        else:
            o_ref[0, hkv] = _dot(hh, w2_ref[...]).astype(o_ref.dtype)


def _compress(kvc, stream, pe, w1, b1, w2, transpose_out):
    B, T, _ = kvc.shape
    H, dh = NSA_KV_HEADS, HEAD_DIM
    n16 = T // CMP_STRIDE
    hidden = w1.shape[1]
    eye = jnp.eye(H, dtype=w1.dtype)
    w1_bd = jnp.einsum('ldn,hg->lhdgn', w1.reshape(CMP_LEN, dh, hidden), eye).reshape(CMP_LEN, H * dh, H * hidden)
    w2b = (w2.T if transpose_out else w2).astype(BF16)
    oshape = (B, H, dh, n16) if transpose_out else (B, H, n16, dh)
    return pl.pallas_call(
        functools.partial(_compress_kernel, transpose_out=transpose_out),
        grid=(B,),
        in_specs=[pl.BlockSpec((1, T, H * dh), lambda b: (b, 0, stream)),
                  pl.BlockSpec((CMP_LEN, H * dh), lambda b: (0, 0)),
                  pl.BlockSpec(w1_bd.shape, lambda b: (0, 0, 0)),
                  pl.BlockSpec((1, H * hidden), lambda b: (0, 0)),
                  pl.BlockSpec(w2b.shape, lambda b: (0, 0))],
        out_specs=pl.BlockSpec((1,) + oshape[1:], lambda b: (b, 0, 0, 0)),
        out_shape=jax.ShapeDtypeStruct(oshape, BF16),
        compiler_params=_params("parallel"), name="nsa_compress",
    )(kvc, jnp.tile(pe, (1, H)), w1_bd.astype(BF16), jnp.tile(b1.reshape(1, hidden), (1, H)), w2b)


def _bias_of_distance(tab_ref, h, d):
    val = jnp.full(d.shape, tab_ref[h, 0], F32)
    for k in range(1, N_BUCKETS):
        val = jnp.where(d >= T5_THRESHOLDS[k], tab_ref[h, k], val)
    return val * LOG2E


def _bias_tiles_kernel(tab_ref, bt_ref, cb_ref, *, n_cmp_pad):
    hkv = pl.program_id(0)
    j = lax.broadcasted_iota(jnp.int32, (KEY_CHUNK, Q_BLOCK), 0)
    i = lax.broadcasted_iota(jnp.int32, (KEY_CHUNK, Q_BLOCK), 1)
    r2 = lax.broadcasted_iota(jnp.int32, (2 * n_cmp_pad, Q_BLOCK), 0)
    i2 = lax.broadcasted_iota(jnp.int32, (2 * n_cmp_pad, Q_BLOCK), 1)
    l2 = r2 - (n_cmp_pad - KEY_CHUNK)
    d2 = i2 - CMP_STRIDE * l2 + (CMP_STRIDE * KEY_CHUNK - Q_BLOCK - (CMP_LEN - 1))
    hidden2 = (l2 >= KEY_CHUNK) | ((l2 >= 0) & (d2 < 0))
    d2 = jnp.where((l2 >= 0) & (l2 < KEY_CHUNK), d2, 2 * MAX_DISTANCE)
    for g in range(NSA_GROUP):
        h = hkv * NSA_GROUP + g
        lanes = slice(g * Q_BLOCK, (g + 1) * Q_BLOCK)
        for m in range(N_BIAS_TILES):
            tile = _bias_of_distance(tab_ref, h, m * KEY_CHUNK + i - j)
            if m == 0:
                tile = jnp.where(j <= i, tile, MASK_SCORE)
            bt_ref[0, m, :, lanes] = tile
        bt_ref[0, TILE_MASKED, :, lanes] = jnp.full((KEY_CHUNK, Q_BLOCK), MASK_SCORE, F32)
        edge = _bias_of_distance(tab_ref, h, WINDOW + i - j)
        bt_ref[0, TILE_WINDOW_EDGE, :, lanes] = jnp.where(j > i, edge, MASK_SCORE)
        cb_ref[0, :, lanes] = jnp.where(hidden2, MASK_SCORE, _bias_of_distance(tab_ref, h, d2))


def _bias_tiles(rel_bias, n_cmp_pad):
    assert CMP_STRIDE * KEY_CHUNK - Q_BLOCK - (CMP_LEN - 1) >= T5_THRESHOLDS[-1]
    GQ = NSA_GROUP * Q_BLOCK
    return pl.pallas_call(
        functools.partial(_bias_tiles_kernel, n_cmp_pad=n_cmp_pad),
        grid=(NSA_KV_HEADS,),
        in_specs=[pl.BlockSpec(memory_space=pltpu.SMEM)],
        out_specs=[pl.BlockSpec((1, N_ALL_TILES, KEY_CHUNK, GQ), lambda h: (h, 0, 0, 0)),
                   pl.BlockSpec((1, 2 * n_cmp_pad, GQ), lambda h: (h, 0, 0))],
        out_shape=[jax.ShapeDtypeStruct((NSA_KV_HEADS, N_ALL_TILES, KEY_CHUNK, GQ), F32),
                   jax.ShapeDtypeStruct((NSA_KV_HEADS, 2 * n_cmp_pad, GQ), F32)],
        compiler_params=_params("parallel"), name="t5_bias_tiles",
    )(rel_bias.T)


def _nsa_kernel(q_ref, kc_ref, vcT_ref, ks_ref, vsT_ref, kw_ref, vwT_ref, gate_ref, ovT_ref, bt_ref, cb_ref,
                o_ref, acc_sc, qaug_sc, seladd_sc, s0_sc, s1_sc, p0_sc, p1_sc, *, n_sel, n_cmp_pad):
    G, NQ = NSA_GROUP, NSA_STEP_QBLOCKS
    GQ = G * Q_BLOCK
    qbs = [pl.program_id(2) * NQ + x for x in range(NQ)]
    per_block = lambda fn: jnp.concatenate([fn(x) for x in range(NQ)], axis=1)
    tile_g = lambda a: jnp.concatenate([a] * G, axis=1)
    qT = per_block(lambda x: q_ref[0, x, 0])
    qaug_sc[0:HEAD_DIM, :] = qT
    qaug_sc[HEAD_DIM:, :] = jnp.zeros((qaug_sc.shape[0] - HEAD_DIM, NQ * GQ), BF16)
    lane_q = lax.broadcasted_iota(jnp.int32, (1, Q_BLOCK), 1)
    t = per_block(lambda x: qbs[x] * Q_BLOCK + lane_q)

    def bias_tile(dist_of_block):
        def one(x):
            dist = dist_of_block(x)
            return bt_ref[0, jnp.where(dist < 0, TILE_MASKED, jnp.minimum(dist, N_BIAS_TILES - 1))]
        return per_block(one)

    n_back = WINDOW // KEY_CHUNK
    q_win = qaug_sc[...]
    win = {}

    def win_scores(x, back):
        kc = jnp.maximum(qbs[x] - back, 0)
        edge = TILE_WINDOW_EDGE if back == n_back else back
        tile = jnp.where(qbs[x] >= back, edge, TILE_MASKED)
        k_chunk = kw_ref[0, pl.ds(pl.multiple_of(kc * KEY_CHUNK, KEY_CHUNK), KEY_CHUNK), :]
        win["s", x, back] = _dot(k_chunk, q_win[:, x * GQ:(x + 1) * GQ]) + bt_ref[0, tile]
        col_max = jnp.max(win["s", x, back], axis=0, keepdims=True)
        win["m", x] = jnp.maximum(win["m", x], col_max) if ("m", x) in win else col_max

    def win_weights(x, back):
        win["p", x, back] = jnp.exp2(win["s", x, back] - win["m", x]).astype(BF16)

    def win_values(x, back):
        pv = _dot(vwT_ref[0, jnp.maximum(qbs[x] - back, 0)], win["p", x, back])
        win["acc", x] = win["acc", x] + pv if ("acc", x) in win else pv

    backs = list(range(n_back, -1, -1))
    window_work = [functools.partial(fn, x, b) for x in range(NQ)
                   for fn in (win_scores, win_weights, win_values) for b in backs]

    def cmp_bias(x):
        start = pl.multiple_of(n_cmp_pad - (Q_BLOCK // CMP_STRIDE) * (qbs[x] + 1), SUBLANES)
        return cb_ref[0, pl.ds(start, n_cmp_pad), :]
    s = _dot(kc_ref[0, 0], qT) + per_block(cmp_bias)
    e = jnp.exp2(s - jnp.maximum(jnp.max(s, axis=0, keepdims=True), 0.1 * MASK_SCORE))
    p = e * (1.0 / jnp.maximum(jnp.sum(e, axis=0, keepdims=True), 1e-30))
    o_c = _dot(vcT_ref[0, 0], p.astype(BF16))
    psum = per_block(lambda x: sum(p[:, x * GQ + g * Q_BLOCK:x * GQ + (g + 1) * Q_BLOCK] for g in range(G)))

    imp = _dot_f32_rhs(ovT_ref[...], psum)
    bj = lax.broadcasted_iota(jnp.int32, (n_sel, NQ * Q_BLOCK), 0)
    cur = jnp.right_shift(t, SEL_SHIFT)
    forced = (bj == 0) | (bj == cur) | (bj == cur - 1)
    valid = bj * SEL_BLOCK <= t
    score = jnp.where(forced, -jnp.inf, jnp.where(valid, imp, NEG_SCORE))
    bjf = bj.astype(F32)
    n_rounds = max(min(SEL_TOP, n_sel) - 3, 0)
    for rnd in range(n_rounds):
        mx = jnp.max(score, axis=0, keepdims=True)
        first = jnp.min(jnp.where(score == mx, bjf, float(n_sel)), axis=0, keepdims=True)
        score = jnp.where(bjf == first, -jnp.inf, score)
        take = -(-len(window_work) // (n_rounds - rnd))
        for piece in window_work[:take]:
            piece()
        window_work = window_work[take:]
    for piece in window_work:
        piece()
    o_w = per_block(lambda x: win["acc", x][0:HEAD_DIM] / jnp.maximum(win["acc", x][HEAD_DIM:HEAD_DIM + 1], 1e-30))
    sel_add = jnp.where(score == -jnp.inf, 0.0, MASK_SCORE)
    seladd_sc[...] = per_block(lambda x: tile_g(sel_add[:, x * Q_BLOCK:(x + 1) * Q_BLOCK]))

    stage_keys = SEL_STAGE_TILES * KEY_CHUNK
    n_stages = qbs[-1] // SEL_STAGE_TILES + 1
    last_stage = ks_ref.shape[1] // stage_keys - 1

    def scores(k, s_buf):
        kk = jnp.minimum(k, last_stage)
        blk0 = pl.multiple_of(kk // SEL_GROUP_STAGES * SEL_STEP_BLOCKS, SEL_STEP_BLOCKS)
        qaug_sc[HEAD_DIM:HEAD_DIM + SEL_STEP_BLOCKS, :] = seladd_sc[pl.ds(blk0, SEL_STEP_BLOCKS), :].astype(BF16)
        k0 = pl.multiple_of(kk * stage_keys, stage_keys)
        bias = jnp.concatenate([bias_tile(lambda x, c=c: qbs[x] - (k * SEL_STAGE_TILES + c))
                                for c in range(SEL_STAGE_TILES)], axis=0)
        s = _dot(ks_ref[0, pl.ds(k0, stage_keys), :], qaug_sc[...]) + bias
        s_buf[...] = s
        return jnp.max(s, axis=0, keepdims=True)

    def weights(m, col_max, s_buf, p_buf):
        m_new = jnp.maximum(m, col_max)
        p_buf[...] = jnp.exp2(s_buf[...] - m_new).astype(BF16)
        return m_new, jnp.exp2(m - m_new)

    def accumulate(k, alpha, p_buf):
        acc_sc[...] = alpha * acc_sc[...] + _dot(vsT_ref[0, jnp.clip(k, 0, last_stage)], p_buf[...])

    def pair(j, carry):
        m, col_max, alpha = carry
        k = 2 * j
        col_max1 = scores(k + 1, s1_sc)
        m, alpha0 = weights(m, col_max, s0_sc, p0_sc)
        accumulate(k - 1, alpha, p1_sc)
        col_max2 = scores(k + 2, s0_sc)
        m, alpha1 = weights(m, col_max1, s1_sc, p1_sc)
        accumulate(k, alpha0, p0_sc)
        return m, col_max2, alpha1

    acc_sc[...] = jnp.zeros(acc_sc.shape, F32)
    p1_sc[...] = jnp.zeros(p1_sc.shape, BF16)
    m_init = jnp.full((1, NQ * GQ), 0.1 * MASK_SCORE, F32)
    carry = (m_init, scores(0, s0_sc), jnp.ones((1, NQ * GQ), F32))
    n_pairs = (n_stages + 1) // 2
    _, _, alpha = lax.fori_loop(0, n_pairs, pair, carry)
    accumulate(2 * n_pairs - 1, alpha, p1_sc)
    o_s = acc_sc[0:HEAD_DIM, :] / jnp.maximum(acc_sc[HEAD_DIM:HEAD_DIM + 1, :], 1e-30)

    def gate_row(br):
        def one(x):
            gt = jax.nn.sigmoid(gate_ref[0, x, 0])
            return jnp.concatenate([gt[br * G + g:br * G + g + 1, :] for g in range(G)], axis=1)
        return per_block(one)
    o = gate_row(0) * o_c + gate_row(1) * o_s + gate_row(2) * o_w
    for x in range(NQ):
        o_ref[0, x * Q_BLOCK:(x + 1) * Q_BLOCK, :] = jnp.concatenate(
            [o[:, x * GQ + g * Q_BLOCK:x * GQ + (g + 1) * Q_BLOCK].T for g in range(G)], axis=1)


def _nsa_attention(q, kc, vcT, kaug, vsT, vwT, gates, overlapT, bias_tiles, cmp_bias):
    B, T, _ = kaug.shape
    Hkv, G = NSA_KV_HEADS, NSA_GROUP
    GQ = G * Q_BLOCK
    nqb = T // Q_BLOCK
    n_cmp_pad = kc.shape[2]
    n_sel = T // SEL_BLOCK
    half_keys = SEL_STAGE_TILES * KEY_CHUNK
    NQ = NSA_STEP_QBLOCKS
    assert T % (SEL_GROUP_STAGES * half_keys) == 0 and nqb % NQ == 0
    per_head = lambda b, h, i: (b, h, 0, 0)
    v_rows = V_ROWS
    chunked = lambda w: pl.BlockSpec((1, T // w, v_rows, w), lambda b, h, i: (b, 0, h, 0))
    step_lanes = NQ * GQ
    return pl.pallas_call(
        functools.partial(_nsa_kernel, n_sel=n_sel, n_cmp_pad=n_cmp_pad),
        grid=(B, Hkv, nqb // NQ),
        in_specs=[pl.BlockSpec((1, NQ, 1, HEAD_DIM, GQ), lambda b, h, i: (b, i, h, 0, 0)),
                  pl.BlockSpec((1, 1, n_cmp_pad, HEAD_DIM), per_head),
                  pl.BlockSpec((1, 1, HEAD_DIM, n_cmp_pad), per_head),
                  pl.BlockSpec((1, T, LANES), lambda b, h, i: (b, 0, h)), chunked(half_keys),
                  pl.BlockSpec((1, T, LANES), lambda b, h, i: (b, 0, Hkv + h)), chunked(KEY_CHUNK),
                  pl.BlockSpec((1, NQ, 1, GATE_ROWS, Q_BLOCK), lambda b, h, i: (b, i, h, 0, 0)),
                  pl.BlockSpec(overlapT.shape, lambda b, h, i: (0, 0)),
                  pl.BlockSpec((1, N_ALL_TILES, KEY_CHUNK, GQ), lambda b, h, i: (h, 0, 0, 0)),
                  pl.BlockSpec((1, 2 * n_cmp_pad, GQ), lambda b, h, i: (h, 0, 0))],
        out_specs=pl.BlockSpec((1, NQ * Q_BLOCK, G * HEAD_DIM), lambda b, h, i: (b, i, h)),
        out_shape=jax.ShapeDtypeStruct((B, T, NSA_WIDTH), F32),
        scratch_shapes=[pltpu.VMEM((v_rows, step_lanes), F32), pltpu.VMEM((LANES, step_lanes), BF16),
                        pltpu.VMEM((n_sel, step_lanes), F32),
                        pltpu.VMEM((half_keys, step_lanes), F32), pltpu.VMEM((half_keys, step_lanes), F32),
                        pltpu.VMEM((half_keys, step_lanes), BF16), pltpu.VMEM((half_keys, step_lanes), BF16)],
        compiler_params=_params("parallel", "parallel", "arbitrary"), name="nsa_attention",
    )(q, kc, vcT, kaug, vsT, kaug, vwT, gates, overlapT, bias_tiles, cmp_bias)


def _rwkv_chunk_kernel(rw_ref, prev_ref, mu_ref, w0_ref, wup_ref, a0_ref, aup_ref, gup_ref, kk_ref, ka_ref,
                       rk_ref, seg_ref, tri_ref, q_ref, y0_ref, a_ref, d_ref, g_ref, bonus_ref):
    C, W, N = RWKV_CHUNK, RWKV_WIDTH, HEAD_DIM
    c = pl.program_id(1)
    x = rw_ref[0]
    R = x.shape[0]
    chunk_rows = [slice(ck * C, (ck + 1) * C) for ck in range(R // C)]
    row = lax.broadcasted_iota(jnp.int32, (R, 1), 0)
    last_prev = jnp.where(c == 0, 0.0, prev_ref[0, SUBLANES - 1:SUBLANES, :])
    x_prev = jnp.where(row == 0, last_prev, pltpu.roll(x, 1, axis=0))
    xs = x + (x_prev - x) * mu_ref[...]
    r, k, v = xs[:, 0:W], xs[:, W:2 * W], xs[:, 2 * W:3 * W]
    o = 3 * W
    wd, ad, gd = xs[:, o:o + DECAY_LORA], xs[:, o + DECAY_LORA:o + DECAY_LORA + AAA_LORA], \
        xs[:, o + DECAY_LORA + AAA_LORA:]
    w_log = -jax.nn.softplus(-(w0_ref[...] + _dot(jnp.tanh(wd).astype(BF16), wup_ref[...]))) - 0.5
    lw = -jnp.exp(w_log)
    lr = jax.nn.sigmoid(a0_ref[...] + _dot(ad.astype(BF16), aup_ref[...]))
    g_ref[0] = _dot(jax.nn.sigmoid(gd).astype(BF16), gup_ref[...])
    kk = k * kk_ref[...]
    kk = kk * lax.rsqrt(jnp.maximum(_dot_f32_lhs(kk * kk, seg_ref[...]), 1e-24))
    k = k * (1.0 + (lr - 1.0) * ka_ref[...])
    bonus_ref[0] = _dot_f32_lhs(r * k * rk_ref[...], seg_ref[...]) * v
    a_vec, b_vec = -kk, kk * lr

    L = _dot_f32_rhs(tri_ref[...], lw)
    L_end = jnp.concatenate([jnp.broadcast_to(L[rs.stop - 1:rs.stop, :], (C, W)) for rs in chunk_rows], axis=0)
    e_neg = jnp.exp(-L)
    e_rem = jnp.exp(L_end - L)
    At, Bt, Kt, Rt = a_vec * jnp.exp(L - lw), b_vec * e_neg, k * e_neg, r * jnp.exp(L)
    Bg, Kg = b_vec * e_rem, k * e_rem
    decay_end = [jnp.exp(L[rs.stop - 1:rs.stop, :]) for rs in chunk_rows]

    ri = lax.broadcasted_iota(jnp.int32, (C, C), 0)
    cj = lax.broadcasted_iota(jnp.int32, (C, C), 1)
    strict, incl = ri > cj, ri >= cj
    eye_c = (ri == cj).astype(F32)
    eye_n = (lax.broadcasted_iota(jnp.int32, (N, N), 0) == lax.broadcasted_iota(jnp.int32, (N, N), 1)).astype(F32)
    items = [(ck, h) for ck in range(len(chunk_rows)) for h in range(RWKV_HEADS)]
    sl = [(chunk_rows[ck], slice(h * N, (h + 1) * N)) for ck, h in items]
    ar = [jnp.concatenate([At[s], Rt[s]], axis=0) for s in sl]
    zb = [_dot_nt(a, Bt[s]) for a, s in zip(ar, sl)]
    zk = [_dot_nt(a, Kt[s]) for a, s in zip(ar, sl)]
    n_mat = [jnp.where(strict, z[0:C], 0.0) for z in zb]
    m_mat = [jnp.where(strict, z[0:C], 0.0) for z in zk]
    mv = [_dot(m, v[s]) for m, s in zip(m_mat, sl)]
    t_inv, n_pow = [eye_c + n for n in n_mat], n_mat
    for _ in range(int(math.log2(C)) - 1):
        n_pow = [_dot(n, n) for n in n_pow]
        t_inv = [t + _dot(t, n) for t, n in zip(t_inv, n_pow)]
    ta = [_dot(t, At[s]) for t, s in zip(t_inv, sl)]
    g0 = [_dot(t, m) for t, m in zip(t_inv, mv)]
    pb = [jnp.concatenate([jnp.where(incl, z[C:], 0.0), Bg[s].T], axis=0) for z, s in zip(zb, sl)]
    pkk = [jnp.concatenate([jnp.where(incl, z[C:], 0.0), Kg[s].T], axis=0) for z, s in zip(zk, sl)]
    x_ta = [_dot(p, t) for p, t in zip(pb, ta)]
    x_g0 = [_dot(p, g) for p, g in zip(pb, g0)]
    x_v = [_dot(p, v[s]) for p, s in zip(pkk, sl)]
    for n, (ck, h) in enumerate(items):
        rows, lanes = sl[n]
        q_ref[0, rows, lanes] = Rt[sl[n]] + x_ta[n][0:C]
        y0_ref[0, rows, lanes] = x_g0[n][0:C] + x_v[n][0:C]
        blk = slice((h % 2) * N, (h % 2 + 1) * N)
        off = slice((1 - h % 2) * N, (2 - h % 2) * N)
        a_ref[0, ck, h // 2, blk, blk] = x_ta[n][C:] + eye_n * decay_end[ck][:, lanes]
        d_ref[0, ck, h // 2, blk, blk] = x_g0[n][C:] + x_v[n][C:]
        a_ref[0, ck, h // 2, blk, off] = jnp.zeros((N, N), F32)
        d_ref[0, ck, h // 2, blk, off] = jnp.zeros((N, N), F32)


def _rwkv_chunks(rw, mu, w0, w_up, a0, a_up, g_up, k_k, k_a, r_k):
    B, T, cols = rw.shape
    C, W, H, N = RWKV_CHUNK, RWKV_WIDTH, RWKV_HEADS, HEAD_DIM
    nc = T // C
    S = RWKV_STEP_CHUNKS
    R = S * C
    assert nc % S == 0
    seg = jnp.asarray(np.kron(np.eye(H), np.ones((N, N))), BF16)
    tri = jnp.asarray(np.kron(np.eye(S), np.tril(np.ones((C, C)))), BF16)
    row = lambda z: z.reshape(1, -1)
    const = lambda b, c: (0, 0)
    vec = pl.BlockSpec((1, W), const)
    tok = pl.BlockSpec((1, R, W), lambda b, c: (b, c, 0))
    mat = pl.BlockSpec((1, S, H // 2, 2 * N, 2 * N), lambda b, c: (b, c, 0, 0, 0))
    return pl.pallas_call(
        _rwkv_chunk_kernel,
        grid=(B, nc // S),
        in_specs=[pl.BlockSpec((1, R, cols), lambda b, c: (b, c, 0)),
                  pl.BlockSpec((1, SUBLANES, cols), lambda b, c: (b, jnp.maximum(c * (R // SUBLANES) - 1, 0), 0)),
                  pl.BlockSpec((1, cols), const), vec,
                  pl.BlockSpec((DECAY_LORA, W), const), vec,
                  pl.BlockSpec((AAA_LORA, W), const),
                  pl.BlockSpec((GATE_LORA, W), const), vec, vec, vec,
                  pl.BlockSpec((W, W), const), pl.BlockSpec((R, R), const)],
        out_specs=[tok, tok, mat, mat, tok, tok],
        out_shape=[jax.ShapeDtypeStruct((B, T, W), F32), jax.ShapeDtypeStruct((B, T, W), F32),
                   jax.ShapeDtypeStruct((B, nc, H // 2, 2 * N, 2 * N), F32),
                   jax.ShapeDtypeStruct((B, nc, H // 2, 2 * N, 2 * N), F32),
                   jax.ShapeDtypeStruct((B, T, W), F32), jax.ShapeDtypeStruct((B, T, W), F32)],
        compiler_params=_params("parallel", "parallel"), name="rwkv_chunks",
    )(rw, rw, row(mu), row(w0), w_up.astype(BF16), row(a0), a_up.astype(BF16), g_up.astype(BF16),
      row(k_k), row(k_a), row(r_k), seg, tri)


def _rwkv_scan_kernel(a_ref, d_ref, q_ref, y0_ref, g_ref, bonus_ref, lw_ref, lb_ref, seg_ref, o_ref, h_sc):
    @pl.when(pl.program_id(0) == 0)
    def _():
        h_sc[...] = jnp.zeros(h_sc.shape, F32)

    inv_n = 1.0 / HEAD_DIM
    items = [(b, pair, slice(pair * LANES, (pair + 1) * LANES))
             for b in range(q_ref.shape[0]) for pair in range(RWKV_HEADS // 2)]
    states = [h_sc[b, pair] for b, pair, _ in items]
    ys = [_dot(q_ref[b, :, lanes].astype(BF16), st.astype(BF16)) + y0_ref[b, :, lanes]
          for (b, _, lanes), st in zip(items, states)]
    for (b, pair, _), st in zip(items, states):
        h_sc[b, pair] = _dot(a_ref[b, 0, pair], st, precision=HIGHEST) + d_ref[b, 0, pair]
    means = [_dot_f32_lhs(y, seg_ref[...]) * inv_n for y in ys]
    cen = [y - mean for y, mean in zip(ys, means)]
    var = [_dot_f32_lhs(jnp.square(c), seg_ref[...]) * inv_n for c in cen]
    for (b, _, lanes), c, v in zip(items, cen, var):
        yn = c * lax.rsqrt(v + LNX_EPS) * lw_ref[:, lanes] + lb_ref[:, lanes]
        o_ref[b, :, lanes] = (yn + bonus_ref[b, :, lanes]) * g_ref[b, :, lanes]


def _rwkv_scan(A, D, Q, Y0, g, bonus, lnx_w, lnx_b):
    B, nc, P, N2, _ = A.shape
    T, W, C = Q.shape[1], Q.shape[2], RWKV_CHUNK
    tok = pl.BlockSpec((B, C, W), lambda c: (0, c, 0))
    mat = pl.BlockSpec((B, 1, P, N2, N2), lambda c: (0, c, 0, 0, 0))
    vec = pl.BlockSpec((1, W), lambda c: (0, 0))
    seg = jnp.asarray(np.kron(np.eye(2), np.ones((HEAD_DIM, HEAD_DIM))), BF16)
    return pl.pallas_call(
        _rwkv_scan_kernel,
        grid=(nc,),
        in_specs=[mat, mat, tok, tok, tok, tok, vec, vec, pl.BlockSpec((N2, N2), lambda c: (0, 0))],
        out_specs=tok,
        out_shape=jax.ShapeDtypeStruct((B, T, W), F32),
        scratch_shapes=[pltpu.VMEM((B, P, N2, N2), F32)],
        compiler_params=_params("arbitrary"), name="rwkv_scan",
    )(A, D, Q, Y0, g, bonus, lnx_w.reshape(1, W), lnx_b.reshape(1, W), seg)


def _mix_xattn_kernel(x_ref, on_ref, or_ref, wo1_ref, wo2_ref, gx_ref, wq_ref, k_ref, v_ref, wo_ref, o_ref):
    x1 = x_ref[0] + _dot(on_ref[0].astype(BF16), wo1_ref[...]) + _dot(or_ref[0].astype(BF16), wo2_ref[...])
    q = _dot(_rms(x1, gx_ref[...]).astype(BF16), wq_ref[...])
    dh = q.shape[-1] // XATTN_HEADS
    qbf = (q * (dh ** -0.5)).astype(BF16)
    outs = []
    for h in range(XATTN_HEADS):
        hs = slice(h * dh, (h + 1) * dh)
        s = _dot_nt(qbf[:, hs], k_ref[0, :, hs])
        e = jnp.exp(s - jnp.max(s, axis=-1, keepdims=True))
        p = e / jnp.sum(e, axis=-1, keepdims=True)
        outs.append(_dot(p.astype(BF16), v_ref[0, :, hs]))
    o = jnp.concatenate(outs, axis=1).astype(BF16)
    o_ref[0] = x1 + _dot(o, wo_ref[...])


def _mix_xattn(x, o_nsa, o_rwkv, w_out, norm_x_g, w_q, mem_k, mem_v, w_o):
    B, T, D = x.shape
    M = mem_k.shape[1]
    tm = min(ROW_TILE, T)
    wo1, wo2 = w_out[:NSA_WIDTH].astype(BF16), w_out[NSA_WIDTH:].astype(BF16)
    const = lambda b, i: (0, 0)
    tile = lambda w: pl.BlockSpec((1, tm, w), lambda b, i: (b, i, 0))
    return pl.pallas_call(
        _mix_xattn_kernel,
        grid=(B, T // tm),
        in_specs=[tile(D), tile(NSA_WIDTH), tile(RWKV_WIDTH),
                  pl.BlockSpec(wo1.shape, const), pl.BlockSpec(wo2.shape, const),
                  pl.BlockSpec((1, D), const), pl.BlockSpec((D, D), const),
                  pl.BlockSpec((1, M, D), lambda b, i: (b, 0, 0)), pl.BlockSpec((1, M, D), lambda b, i: (b, 0, 0)),
                  pl.BlockSpec((D, D), const)],
        out_specs=tile(D),
        out_shape=jax.ShapeDtypeStruct((B, T, D), F32),
        compiler_params=_params("parallel", "parallel"), name="mix_xattn",
    )(x, o_nsa, o_rwkv, wo1, wo2, norm_x_g.reshape(1, D), w_q.astype(BF16), mem_k, mem_v, w_o.astype(BF16))


def _ffn_kernel(x_ref, g_ref, wg_ref, wu_ref, wd_ref, gf_ref, o_ref, *, final_norm):
    x = x_ref[...]
    h = _rms(x, g_ref[...]).astype(BF16)
    act = (jax.nn.silu(_dot(h, wg_ref[...])) * _dot(h, wu_ref[...])).astype(BF16)
    y = x + _dot(act, wd_ref[...])
    o_ref[...] = _rms(y, gf_ref[...]) if final_norm else y


def _ffn(x, norm_g, w_gate, w_up, w_down, final_g, final_norm):
    R, D = x.shape
    F = w_gate.shape[1]
    tm = min(FFN_ROW_TILE, R)
    const = lambda i: (0, 0)
    resident = lambda shape: pl.BlockSpec(shape, const, pipeline_mode=pl.Buffered(1))
    return pl.pallas_call(
        functools.partial(_ffn_kernel, final_norm=final_norm),
        grid=(R // tm,),
        in_specs=[pl.BlockSpec((tm, D), lambda i: (i, 0)), pl.BlockSpec((1, D), const),
                  resident((D, F)), resident((D, F)), resident((F, D)), pl.BlockSpec((1, D), const)],
        out_specs=pl.BlockSpec((tm, D), lambda i: (i, 0)),
        out_shape=jax.ShapeDtypeStruct((R, D), F32),
        compiler_params=_params("parallel"), name="ffn",
    )(x, norm_g.reshape(1, D), w_gate.astype(BF16), w_up.astype(BF16), w_down.astype(BF16),
      final_g.reshape(1, D))


def _overlap_matrix(n_cmp_pad, n_sel):
    c = np.arange(n_cmp_pad)[:, None] * CMP_STRIDE
    s = np.arange(n_sel)[None, :] * SEL_BLOCK
    return ((c <= s + SEL_BLOCK - 1) & (c + CMP_LEN - 1 >= s)).astype(np.float32)


def _layer(x, mem, rel_bias, final_g, is_last, norm_mix_g, w_in, nsa_gate_b, cmp_pe_k, cmp_pe_v,
           cmp_k_w1, cmp_k_b1, cmp_k_w2, cmp_v_w1, cmp_v_b1, cmp_v_w2,
           rwkv_mu, rwkv_w0, rwkv_w_up, rwkv_a0, rwkv_a_up, rwkv_g_up,
           rwkv_k_k, rwkv_k_a, rwkv_r_k, rwkv_lnx_w, rwkv_lnx_b, w_out,
           norm_x_g, norm_mem_g, w_q_x, w_kv_x, w_o_x, norm_ffn_g, w_gate, w_up, w_down):
    B, T, D = x.shape
    Hkv, G, dh = NSA_KV_HEADS, NSA_GROUP, HEAD_DIM
    q, kvc, kaug, vsT, vwT, gates, rw = _proj_in(x, norm_mix_g, w_in, nsa_gate_b)

    n16 = T // CMP_STRIDE
    kvc = kvc.reshape(B, T, 2 * KV_WIDTH)
    kc = _compress(kvc, 0, cmp_pe_k, cmp_k_w1, cmp_k_b1, cmp_k_w2, False)
    vcT = _compress(kvc, 1, cmp_pe_v, cmp_v_w1, cmp_v_b1, cmp_v_w2, True)
    bias_tiles, cmp_bias = _bias_tiles(rel_bias, n16)
    o_nsa = _nsa_attention(q, kc, vcT, kaug.reshape(B, T, 2 * Hkv * LANES), vsT, vwT,
                           gates, jnp.asarray(_overlap_matrix(n16, T // SEL_BLOCK).T, BF16), bias_tiles, cmp_bias)

    Q, Y0, A, Dm, g, bonus = _rwkv_chunks(rw.reshape(B, T, RWKV_COLS), rwkv_mu, rwkv_w0, rwkv_w_up, rwkv_a0,
                                          rwkv_a_up, rwkv_g_up, rwkv_k_k, rwkv_k_a, rwkv_r_k.reshape(-1))
    o_rwkv = _rwkv_scan(A, Dm, Q, Y0, g, bonus, rwkv_lnx_w, rwkv_lnx_b)

    M = mem.shape[1]
    (kv_mem,) = _norm_matmul(mem.reshape(B * M, D), norm_mem_g, [w_kv_x.astype(BF16)], [None], [BF16], ROW_TILE)
    kv_mem = kv_mem.reshape(B, M, 2 * D)
    x = _mix_xattn(x, o_nsa, o_rwkv, w_out, norm_x_g, w_q_x, kv_mem[..., :D], kv_mem[..., D:], w_o_x)
    x = _ffn(x.reshape(B * T, D), norm_ffn_g, w_gate, w_up, w_down, final_g, is_last)
    return x.reshape(B, T, D)


def kernel(x, mem, rel_bias, norm_f_g, norm_mix_g, w_in, nsa_gate_b, cmp_pe_k, cmp_pe_v, cmp_k_w1, cmp_k_b1, cmp_k_w2, cmp_v_w1, cmp_v_b1, cmp_v_w2, rwkv_mu, rwkv_w0, rwkv_w_up, rwkv_a0, rwkv_a_up, rwkv_g_up, rwkv_k_k, rwkv_k_a, rwkv_r_k, rwkv_lnx_w, rwkv_lnx_b, w_out, norm_x_g, norm_mem_g, w_q_x, w_kv_x, w_o_x, norm_ffn_g, w_gate, w_up, w_down):
    stacked = (norm_mix_g, w_in, nsa_gate_b, cmp_pe_k, cmp_pe_v, cmp_k_w1, cmp_k_b1, cmp_k_w2, cmp_v_w1,
               cmp_v_b1, cmp_v_w2, rwkv_mu, rwkv_w0, rwkv_w_up, rwkv_a0, rwkv_a_up, rwkv_g_up, rwkv_k_k,
               rwkv_k_a, rwkv_r_k, rwkv_lnx_w, rwkv_lnx_b, w_out, norm_x_g, norm_mem_g, w_q_x, w_kv_x, w_o_x,
               norm_ffn_g, w_gate, w_up, w_down)
    depth = w_in.shape[0]
    for l in range(depth):
        x = _layer(x, mem, rel_bias, norm_f_g, l == depth - 1, *[p[l] for p in stacked])
    return x
```

```python
import functools
import math

import numpy as np
import jax
import jax.numpy as jnp
from jax import lax
from jax.experimental import pallas as pl
from jax.experimental.pallas import tpu as pltpu

F32 = jnp.float32
BF16 = jnp.bfloat16
HIGHEST = lax.Precision.HIGHEST

LANES = 128
SUBLANES = 8
BF16_ROWS = 16
VMEM_LIMIT_BYTES = 56 * 1024 * 1024

HEAD_DIM = 64
NSA_HEADS = 8
NSA_KV_HEADS = 2
NSA_GROUP = NSA_HEADS // NSA_KV_HEADS
NSA_WIDTH = NSA_HEADS * HEAD_DIM
KV_WIDTH = NSA_KV_HEADS * HEAD_DIM
RWKV_HEADS = 8
RWKV_WIDTH = RWKV_HEADS * HEAD_DIM
CMP_LEN = 32
CMP_STRIDE = 16
SEL_BLOCK = 64
SEL_SHIFT = 6
SEL_TOP = 16
WINDOW = 512
Q_BLOCK = 128
DECAY_LORA = 64
AAA_LORA = 64
GATE_LORA = 128
N_BUCKETS = 32
MAX_DISTANCE = 2048
XATTN_HEADS = 4
RMS_EPS = 1e-6
LNX_EPS = 64e-5
FORCE_SCORE = 1e4
NEG_SCORE = -1e9
MASK_SCORE = -1e30
LOG2E = math.log2(math.e)
RWKV_COLS = 3 * RWKV_WIDTH + DECAY_LORA + AAA_LORA + GATE_LORA
NSA_COLS = NSA_WIDTH + 6 * KV_WIDTH + 3 * NSA_HEADS

KEY_CHUNK = 128
RWKV_CHUNK = 64
RWKV_GROUP_HEADS = 2
RWKV_STEP_CHUNKS = 8
ROW_TILE = 512
FFN_ROW_TILE = 256


def _t5_thresholds():
    d = np.arange(0, 2 * MAX_DISTANCE, dtype=np.int64)
    max_exact = N_BUCKETS // 2
    nf = np.maximum(d, 1).astype(np.float32)
    large = max_exact + (np.log(nf / np.float32(max_exact)) / np.float32(math.log(MAX_DISTANCE / max_exact))
                         * np.float32(N_BUCKETS - max_exact)).astype(np.int32)
    bucket = np.where(d < max_exact, d, np.minimum(large, N_BUCKETS - 1))
    return [int(np.argmax(bucket >= k)) for k in range(N_BUCKETS)]


T5_THRESHOLDS = _t5_thresholds()
N_BIAS_TILES = -(-(T5_THRESHOLDS[-1] + KEY_CHUNK) // KEY_CHUNK) + 1
TILE_MASKED = N_BIAS_TILES
TILE_WINDOW_EDGE = N_BIAS_TILES + 1
N_ALL_TILES = N_BIAS_TILES + 2
SEL_STEP_BLOCKS = 16
SEL_STAGE_TILES = 4
SEL_GROUP_STAGES = SEL_STEP_BLOCKS * SEL_BLOCK // (SEL_STAGE_TILES * KEY_CHUNK)
NSA_STEP_QBLOCKS = 4
V_ROWS = HEAD_DIM + BF16_ROWS
GATE_ROWS = 16


def _params(*semantics):
    return pltpu.CompilerParams(dimension_semantics=semantics, vmem_limit_bytes=VMEM_LIMIT_BYTES)


def _rms(x, g):
    return x * lax.rsqrt(jnp.mean(x * x, axis=-1, keepdims=True) + RMS_EPS) * g


def _dot(a, b, **kw):
    return jnp.dot(a, b, preferred_element_type=F32, **kw)


def _split3(x):
    hi = x.astype(BF16)
    r1 = x - hi.astype(F32)
    mid = r1.astype(BF16)
    lo = (r1 - mid.astype(F32)).astype(BF16)
    return hi, mid, lo


def _dot_f32_lhs(x, w01):
    w = w01.astype(BF16)
    hi, mid, lo = _split3(x)
    return _dot(hi, w) + (_dot(mid, w) + _dot(lo, w))


def _dot_f32_rhs(w01, x):
    w = w01.astype(BF16)
    hi, mid, lo = _split3(x)
    return _dot(w, hi) + (_dot(w, mid) + _dot(w, lo))


def _dot_nt(a, b, **kw):
    return lax.dot_general(a, b, (((1,), (1,)), ((), ())), preferred_element_type=F32, **kw)


def _norm_matmul_kernel(x_ref, g_ref, *refs, nseg, bias_flags):
    nb = sum(bias_flags)
    w_refs, b_refs, o_refs = refs[:nseg], refs[nseg:nseg + nb], refs[nseg + nb:]
    xn = _rms(x_ref[...], g_ref[...]).astype(BF16)
    bi = 0
    for s in range(nseg):
        y = _dot(xn, w_refs[s][...])
        if bias_flags[s]:
            y = y + b_refs[bi][...]
            bi += 1
        o_refs[s][...] = y.astype(o_refs[s].dtype)


def _norm_matmul(x, g, weights, biases, out_dtypes, row_tile):
    R, D = x.shape
    tm = min(row_tile, R)
    assert R % tm == 0
    nseg = len(weights)
    bias_flags = tuple(b is not None for b in biases)
    const = lambda i: (0, 0)
    in_specs = [pl.BlockSpec((tm, D), lambda i: (i, 0)), pl.BlockSpec((1, D), const)]
    in_specs += [pl.BlockSpec(w.shape, const) for w in weights]
    in_specs += [pl.BlockSpec((1, b.shape[-1]), const) for b in biases if b is not None]
    out_specs = [pl.BlockSpec((tm, w.shape[1]), lambda i: (i, 0)) for w in weights]
    out_shape = [jax.ShapeDtypeStruct((R, w.shape[1]), dt) for w, dt in zip(weights, out_dtypes)]
    return pl.pallas_call(
        functools.partial(_norm_matmul_kernel, nseg=nseg, bias_flags=bias_flags),
        grid=(R // tm,), in_specs=in_specs, out_specs=out_specs, out_shape=out_shape,
        compiler_params=_params("parallel"), name="norm_matmul",
    )(x, g.reshape(1, D), *weights, *[b.reshape(1, -1) for b in biases if b is not None])


def _proj_in_kernel(x_ref, g_ref, wq_ref, wc_ref, wk_ref, wvT_ref, wgT_ref, bg_ref, wr_ref,
                    q_ref, kvc_ref, kaug_ref, vsT_ref, vwT_ref, gate_ref, rw_ref, *, seq_len):
    tm = x_ref.shape[0]
    xn = _rms(x_ref[...], g_ref[...]).astype(BF16)
    qT = (_dot_nt(wq_ref[...], xn) * (HEAD_DIM ** -0.5 * LOG2E)).astype(BF16)
    for j in range(tm // Q_BLOCK):
        for hg in range(NSA_HEADS):
            h, g = divmod(hg, NSA_GROUP)
            q_ref[0, j, h, :, g * Q_BLOCK:(g + 1) * Q_BLOCK] = qT[hg * HEAD_DIM:(hg + 1) * HEAD_DIM,
                                                                  j * Q_BLOCK:(j + 1) * Q_BLOCK]
    kvc_ref[...] = _dot(xn, wc_ref[...])
    rw_ref[...] = _dot(xn, wr_ref[...])
    k_all = _dot(xn, wk_ref[...])
    tok = lax.rem(pl.program_id(0) * tm, seq_len) + lax.broadcasted_iota(jnp.int32, k_all.shape, 0)
    lane = lax.broadcasted_iota(jnp.int32, k_all.shape, 1)
    blk = jnp.bitwise_and(jnp.right_shift(tok, SEL_SHIFT), SEL_STEP_BLOCKS - 1)
    hot = (jnp.bitwise_and(lane, LANES - 1) == HEAD_DIM + blk) & (lane < NSA_KV_HEADS * LANES)
    kaug_ref[...] = jnp.where(hot, 1.0, k_all).astype(BF16)
    vT = _dot_nt(wvT_ref[...], xn)
    row = lax.broadcasted_iota(jnp.int32, vT.shape, 0)
    ones_row = row == HEAD_DIM
    for grp in range(1, 2 * NSA_KV_HEADS):
        ones_row = ones_row | (row == grp * V_ROWS + HEAD_DIM)
    vT = jnp.where(ones_row, 1.0, vT).astype(BF16)
    half = NSA_KV_HEADS * V_ROWS
    stage_keys = vsT_ref.shape[3]
    for c in range(tm // stage_keys):
        vsT_ref[0, c] = vT[0:half, c * stage_keys:(c + 1) * stage_keys]
    for c in range(tm // KEY_CHUNK):
        vwT_ref[0, c] = vT[half:, c * KEY_CHUNK:(c + 1) * KEY_CHUNK]
    gT = _dot_nt(wgT_ref[...], xn) + bg_ref[...]
    for j in range(tm // Q_BLOCK):
        for h in range(NSA_KV_HEADS):
            gate_ref[0, j, h] = gT[h * GATE_ROWS:(h + 1) * GATE_ROWS, j * Q_BLOCK:(j + 1) * Q_BLOCK]


def _proj_in(x, norm_g, w_in, gate_b):
    B, T, D = x.shape
    Hkv, G, dh = NSA_KV_HEADS, NSA_GROUP, HEAD_DIM
    tm = ROW_TILE
    stage_keys = SEL_STAGE_TILES * KEY_CHUNK
    assert T % tm == 0 and tm % Q_BLOCK == 0 and tm % stage_keys == 0
    kv0 = NSA_WIDTH
    g0 = kv0 + 6 * KV_WIDTH
    stream = lambda s: w_in[:, kv0 + s * KV_WIDTH:kv0 + (s + 1) * KV_WIDTH].reshape(D, Hkv, dh)
    pad_cols = lambda w: jnp.pad(w, ((0, 0), (0, 0), (0, LANES - dh))).reshape(D, Hkv * LANES)
    pad_rows = lambda w: jnp.pad(w.transpose(1, 2, 0), ((0, 0), (0, V_ROWS - dh), (0, 0))).reshape(Hkv * V_ROWS, D)
    w_k = jnp.concatenate([pad_cols(stream(2)), pad_cols(stream(4))], axis=1)
    w_vT = jnp.concatenate([pad_rows(stream(3)), pad_rows(stream(5))], axis=0)
    reorder = lambda a: a.reshape(-1, Hkv, G, 3).transpose(1, 3, 2, 0).reshape(Hkv, 3 * G, -1)
    pad_gate = lambda a: jnp.pad(a, ((0, 0), (0, GATE_ROWS - 3 * G), (0, 0))).reshape(Hkv * GATE_ROWS, -1)
    w_gT = pad_gate(reorder(w_in[:, g0:NSA_COLS]))
    b_g = pad_gate(reorder(gate_b.reshape(1, -1)))
    weights = [w_in[:, :kv0].T, w_in[:, kv0:kv0 + 2 * KV_WIDTH], w_k, w_vT, w_gT]
    weights = [w.astype(BF16) for w in weights] + [b_g, w_in[:, NSA_COLS:].astype(BF16)]
    nt = T // tm
    rows = lambda n: pl.BlockSpec((tm, n), lambda i: (i, 0))
    const = lambda i: (0, 0)
    return pl.pallas_call(
        functools.partial(_proj_in_kernel, seq_len=T),
        grid=(B * nt,),
        in_specs=[rows(D), pl.BlockSpec((1, D), const)] + [pl.BlockSpec(w.shape, const) for w in weights],
        out_specs=[pl.BlockSpec((1, tm // Q_BLOCK, Hkv, dh, G * Q_BLOCK), lambda i: (i // nt, i % nt, 0, 0, 0)),
                   rows(2 * KV_WIDTH), rows(2 * Hkv * LANES),
                   pl.BlockSpec((1, tm // stage_keys, Hkv * V_ROWS, stage_keys), lambda i: (i // nt, i % nt, 0, 0)),
                   pl.BlockSpec((1, tm // KEY_CHUNK, Hkv * V_ROWS, KEY_CHUNK), lambda i: (i // nt, i % nt, 0, 0)),
                   pl.BlockSpec((1, tm // Q_BLOCK, Hkv, GATE_ROWS, Q_BLOCK), lambda i: (i // nt, i % nt, 0, 0, 0)),
                   rows(RWKV_COLS)],
        out_shape=[jax.ShapeDtypeStruct((B, T // Q_BLOCK, Hkv, dh, G * Q_BLOCK), BF16),
                   jax.ShapeDtypeStruct((B * T, 2 * KV_WIDTH), F32),
                   jax.ShapeDtypeStruct((B * T, 2 * Hkv * LANES), BF16),
                   jax.ShapeDtypeStruct((B, T // stage_keys, Hkv * V_ROWS, stage_keys), BF16),
                   jax.ShapeDtypeStruct((B, T // KEY_CHUNK, Hkv * V_ROWS, KEY_CHUNK), BF16),
                   jax.ShapeDtypeStruct((B, T // Q_BLOCK, Hkv, GATE_ROWS, Q_BLOCK), F32),
                   jax.ShapeDtypeStruct((B * T, RWKV_COLS), F32)],
        compiler_params=_params("parallel"), name="proj_in",
    )(x.reshape(B * T, D), norm_g.reshape(1, D), *weights)


def _compress_kernel(x_ref, pe_ref, w1_ref, b1_ref, w2_ref, o_ref, *, transpose_out):
    n16 = x_ref.shape[1] // CMP_STRIDE
    hidden = w1_ref.shape[2] // NSA_KV_HEADS
    lo = jnp.zeros((n16, w1_ref.shape[2]), F32)
    hi = jnp.zeros((n16, w1_ref.shape[2]), F32)
    for l in range(CMP_STRIDE):
        rows = x_ref[0, pl.ds(l, n16, stride=CMP_STRIDE), :]
        lo = lo + _dot((rows + pe_ref[l:l + 1, :]).astype(BF16), w1_ref[l])
        hi = hi + _dot((rows + pe_ref[CMP_STRIDE + l:CMP_STRIDE + l + 1, :]).astype(BF16), w1_ref[CMP_STRIDE + l])
    h = lo + pltpu.roll(hi, n16 - 1, axis=0) + b1_ref[...]
    h = jax.nn.gelu(h).astype(BF16)
    for hkv in range(NSA_KV_HEADS):
        hh = h[:, hkv * hidden:(hkv + 1) * hidden]
        if transpose_out:
            o_ref[0, hkv] = _dot_nt(w2_ref[...], hh).astype(o_ref.dtype)
        else:
            o_ref[0, hkv] = _dot(hh, w2_ref[...]).astype(o_ref.dtype)


def _compress(kvc, stream, pe, w1, b1, w2, transpose_out):
    B, T, _ = kvc.shape
    H, dh = NSA_KV_HEADS, HEAD_DIM
    n16 = T // CMP_STRIDE
    hidden = w1.shape[1]
    eye = jnp.eye(H, dtype=w1.dtype)
    w1_bd = jnp.einsum('ldn,hg->lhdgn', w1.reshape(CMP_LEN, dh, hidden), eye).reshape(CMP_LEN, H * dh, H * hidden)
    w2b = (w2.T if transpose_out else w2).astype(BF16)
    oshape = (B, H, dh, n16) if transpose_out else (B, H, n16, dh)
    return pl.pallas_call(
        functools.partial(_compress_kernel, transpose_out=transpose_out),
        grid=(B,),
        in_specs=[pl.BlockSpec((1, T, H * dh), lambda b: (b, 0, stream)),
                  pl.BlockSpec((CMP_LEN, H * dh), lambda b: (0, 0)),
                  pl.BlockSpec(w1_bd.shape, lambda b: (0, 0, 0)),
                  pl.BlockSpec((1, H * hidden), lambda b: (0, 0)),
                  pl.BlockSpec(w2b.shape, lambda b: (0, 0))],
        out_specs=pl.BlockSpec((1,) + oshape[1:], lambda b: (b, 0, 0, 0)),
        out_shape=jax.ShapeDtypeStruct(oshape, BF16),
        compiler_params=_params("parallel"), name="nsa_compress",
    )(kvc, jnp.tile(pe, (1, H)), w1_bd.astype(BF16), jnp.tile(b1.reshape(1, hidden), (1, H)), w2b)


def _bias_of_distance(tab_ref, h, d):
    val = jnp.full(d.shape, tab_ref[h, 0], F32)
    for k in range(1, N_BUCKETS):
        val = jnp.where(d >= T5_THRESHOLDS[k], tab_ref[h, k], val)
    return val * LOG2E


def _bias_tiles_kernel(tab_ref, bt_ref, cb_ref, *, n_cmp_pad):
    hkv = pl.program_id(0)
    j = lax.broadcasted_iota(jnp.int32, (KEY_CHUNK, Q_BLOCK), 0)
    i = lax.broadcasted_iota(jnp.int32, (KEY_CHUNK, Q_BLOCK), 1)
    r2 = lax.broadcasted_iota(jnp.int32, (2 * n_cmp_pad, Q_BLOCK), 0)
    i2 = lax.broadcasted_iota(jnp.int32, (2 * n_cmp_pad, Q_BLOCK), 1)
    l2 = r2 - (n_cmp_pad - KEY_CHUNK)
    d2 = i2 - CMP_STRIDE * l2 + (CMP_STRIDE * KEY_CHUNK - Q_BLOCK - (CMP_LEN - 1))
    hidden2 = (l2 >= KEY_CHUNK) | ((l2 >= 0) & (d2 < 0))
    d2 = jnp.where((l2 >= 0) & (l2 < KEY_CHUNK), d2, 2 * MAX_DISTANCE)
    for g in range(NSA_GROUP):
        h = hkv * NSA_GROUP + g
        lanes = slice(g * Q_BLOCK, (g + 1) * Q_BLOCK)
        for m in range(N_BIAS_TILES):
            tile = _bias_of_distance(tab_ref, h, m * KEY_CHUNK + i - j)
            if m == 0:
                tile = jnp.where(j <= i, tile, MASK_SCORE)
            bt_ref[0, m, :, lanes] = tile
        bt_ref[0, TILE_MASKED, :, lanes] = jnp.full((KEY_CHUNK, Q_BLOCK), MASK_SCORE, F32)
        edge = _bias_of_distance(tab_ref, h, WINDOW + i - j)
        bt_ref[0, TILE_WINDOW_EDGE, :, lanes] = jnp.where(j > i, edge, MASK_SCORE)
        cb_ref[0, :, lanes] = jnp.where(hidden2, MASK_SCORE, _bias_of_distance(tab_ref, h, d2))


def _bias_tiles(rel_bias, n_cmp_pad):
    assert CMP_STRIDE * KEY_CHUNK - Q_BLOCK - (CMP_LEN - 1) >= T5_THRESHOLDS[-1]
    GQ = NSA_GROUP * Q_BLOCK
    return pl.pallas_call(
        functools.partial(_bias_tiles_kernel, n_cmp_pad=n_cmp_pad),
        grid=(NSA_KV_HEADS,),
        in_specs=[pl.BlockSpec(memory_space=pltpu.SMEM)],
        out_specs=[pl.BlockSpec((1, N_ALL_TILES, KEY_CHUNK, GQ), lambda h: (h, 0, 0, 0)),
                   pl.BlockSpec((1, 2 * n_cmp_pad, GQ), lambda h: (h, 0, 0))],
        out_shape=[jax.ShapeDtypeStruct((NSA_KV_HEADS, N_ALL_TILES, KEY_CHUNK, GQ), F32),
                   jax.ShapeDtypeStruct((NSA_KV_HEADS, 2 * n_cmp_pad, GQ), F32)],
        compiler_params=_params("parallel"), name="t5_bias_tiles",
    )(rel_bias.T)


def _nsa_kernel(q_ref, kc_ref, vcT_ref, ks_ref, vsT_ref, kw_ref, vwT_ref, gate_ref, ovT_ref, bt_ref, cb_ref,
                o_ref, acc_sc, qaug_sc, seladd_sc, s0_sc, s1_sc, p0_sc, p1_sc, *, n_sel, n_cmp_pad):
    G, NQ = NSA_GROUP, NSA_STEP_QBLOCKS
    GQ = G * Q_BLOCK
    qbs = [pl.program_id(2) * NQ + x for x in range(NQ)]
    per_block = lambda fn: jnp.concatenate([fn(x) for x in range(NQ)], axis=1)
    tile_g = lambda a: jnp.concatenate([a] * G, axis=1)
    qT = per_block(lambda x: q_ref[0, x, 0])
    qaug_sc[0:HEAD_DIM, :] = qT
    qaug_sc[HEAD_DIM:, :] = jnp.zeros((qaug_sc.shape[0] - HEAD_DIM, NQ * GQ), BF16)
    lane_q = lax.broadcasted_iota(jnp.int32, (1, Q_BLOCK), 1)
    t = per_block(lambda x: qbs[x] * Q_BLOCK + lane_q)

    def bias_tile(dist_of_block):
        def one(x):
            dist = dist_of_block(x)
            return bt_ref[0, jnp.where(dist < 0, TILE_MASKED, jnp.minimum(dist, N_BIAS_TILES - 1))]
        return per_block(one)

    n_back = WINDOW // KEY_CHUNK
    q_win = qaug_sc[...]
    win = {}

    def win_scores(x, back):
        kc = jnp.maximum(qbs[x] - back, 0)
        edge = TILE_WINDOW_EDGE if back == n_back else back
        tile = jnp.where(qbs[x] >= back, edge, TILE_MASKED)
        k_chunk = kw_ref[0, pl.ds(pl.multiple_of(kc * KEY_CHUNK, KEY_CHUNK), KEY_CHUNK), :]
        win["s", x, back] = _dot(k_chunk, q_win[:, x * GQ:(x + 1) * GQ]) + bt_ref[0, tile]
        col_max = jnp.max(win["s", x, back], axis=0, keepdims=True)
        win["m", x] = jnp.maximum(win["m", x], col_max) if ("m", x) in win else col_max

    def win_weights(x, back):
        win["p", x, back] = jnp.exp2(win["s", x, back] - win["m", x]).astype(BF16)

    def win_values(x, back):
        pv = _dot(vwT_ref[0, jnp.maximum(qbs[x] - back, 0)], win["p", x, back])
        win["acc", x] = win["acc", x] + pv if ("acc", x) in win else pv

    backs = list(range(n_back, -1, -1))
    window_work = [functools.partial(fn, x, b) for x in range(NQ)
                   for fn in (win_scores, win_weights, win_values) for b in backs]

    def cmp_bias(x):
        start = pl.multiple_of(n_cmp_pad - (Q_BLOCK // CMP_STRIDE) * (qbs[x] + 1), SUBLANES)
        return cb_ref[0, pl.ds(start, n_cmp_pad), :]
    s = _dot(kc_ref[0, 0], qT) + per_block(cmp_bias)
    e = jnp.exp2(s - jnp.maximum(jnp.max(s, axis=0, keepdims=True), 0.1 * MASK_SCORE))
    p = e * (1.0 / jnp.maximum(jnp.sum(e, axis=0, keepdims=True), 1e-30))
    o_c = _dot(vcT_ref[0, 0], p.astype(BF16))
    psum = per_block(lambda x: sum(p[:, x * GQ + g * Q_BLOCK:x * GQ + (g + 1) * Q_BLOCK] for g in range(G)))

    imp = _dot_f32_rhs(ovT_ref[...], psum)
    bj = lax.broadcasted_iota(jnp.int32, (n_sel, NQ * Q_BLOCK), 0)
    cur = jnp.right_shift(t, SEL_SHIFT)
    forced = (bj == 0) | (bj == cur) | (bj == cur - 1)
    valid = bj * SEL_BLOCK <= t
    score = jnp.where(forced, -jnp.inf, jnp.where(valid, imp, NEG_SCORE))
    bjf = bj.astype(F32)
    n_rounds = max(min(SEL_TOP, n_sel) - 3, 0)
    for rnd in range(n_rounds):
        mx = jnp.max(score, axis=0, keepdims=True)
        first = jnp.min(jnp.where(score == mx, bjf, float(n_sel)), axis=0, keepdims=True)
        score = jnp.where(bjf == first, -jnp.inf, score)
        take = -(-len(window_work) // (n_rounds - rnd))
        for piece in window_work[:take]:
            piece()
        window_work = window_work[take:]
    for piece in window_work:
        piece()
    o_w = per_block(lambda x: win["acc", x][0:HEAD_DIM] / jnp.maximum(win["acc", x][HEAD_DIM:HEAD_DIM + 1], 1e-30))
    sel_add = jnp.where(score == -jnp.inf, 0.0, MASK_SCORE)
    seladd_sc[...] = per_block(lambda x: tile_g(sel_add[:, x * Q_BLOCK:(x + 1) * Q_BLOCK]))

    stage_keys = SEL_STAGE_TILES * KEY_CHUNK
    n_stages = qbs[-1] // SEL_STAGE_TILES + 1
    last_stage = ks_ref.shape[1] // stage_keys - 1

    def scores(k, s_buf):
        kk = jnp.minimum(k, last_stage)
        blk0 = pl.multiple_of(kk // SEL_GROUP_STAGES * SEL_STEP_BLOCKS, SEL_STEP_BLOCKS)
        qaug_sc[HEAD_DIM:HEAD_DIM + SEL_STEP_BLOCKS, :] = seladd_sc[pl.ds(blk0, SEL_STEP_BLOCKS), :].astype(BF16)
        k0 = pl.multiple_of(kk * stage_keys, stage_keys)
        bias = jnp.concatenate([bias_tile(lambda x, c=c: qbs[x] - (k * SEL_STAGE_TILES + c))
                                for c in range(SEL_STAGE_TILES)], axis=0)
        s = _dot(ks_ref[0, pl.ds(k0, stage_keys), :], qaug_sc[...]) + bias
        s_buf[...] = s
        return jnp.max(s, axis=0, keepdims=True)

    def weights(m, col_max, s_buf, p_buf):
        m_new = jnp.maximum(m, col_max)
        p_buf[...] = jnp.exp2(s_buf[...] - m_new).astype(BF16)
        return m_new, jnp.exp2(m - m_new)

    def accumulate(k, alpha, p_buf):
        acc_sc[...] = alpha * acc_sc[...] + _dot(vsT_ref[0, jnp.clip(k, 0, last_stage)], p_buf[...])

    def pair(j, carry):
        m, col_max, alpha = carry
        k = 2 * j
        col_max1 = scores(k + 1, s1_sc)
        m, alpha0 = weights(m, col_max, s0_sc, p0_sc)
        accumulate(k - 1, alpha, p1_sc)
        col_max2 = scores(k + 2, s0_sc)
        m, alpha1 = weights(m, col_max1, s1_sc, p1_sc)
        accumulate(k, alpha0, p0_sc)
        return m, col_max2, alpha1

    acc_sc[...] = jnp.zeros(acc_sc.shape, F32)
    p1_sc[...] = jnp.zeros(p1_sc.shape, BF16)
    m_init = jnp.full((1, NQ * GQ), 0.1 * MASK_SCORE, F32)
    carry = (m_init, scores(0, s0_sc), jnp.ones((1, NQ * GQ), F32))
    n_pairs = (n_stages + 1) // 2
    _, _, alpha = lax.fori_loop(0, n_pairs, pair, carry)
    accumulate(2 * n_pairs - 1, alpha, p1_sc)
    o_s = acc_sc[0:HEAD_DIM, :] / jnp.maximum(acc_sc[HEAD_DIM:HEAD_DIM + 1, :], 1e-30)

    def gate_row(br):
        def one(x):
            gt = jax.nn.sigmoid(gate_ref[0, x, 0])
            return jnp.concatenate([gt[br * G + g:br * G + g + 1, :] for g in range(G)], axis=1)
        return per_block(one)
    o = gate_row(0) * o_c + gate_row(1) * o_s + gate_row(2) * o_w
    for x in range(NQ):
        o_ref[0, x * Q_BLOCK:(x + 1) * Q_BLOCK, :] = jnp.concatenate(
            [o[:, x * GQ + g * Q_BLOCK:x * GQ + (g + 1) * Q_BLOCK].T for g in range(G)], axis=1)


def _nsa_attention(q, kc, vcT, kaug, vsT, vwT, gates, overlapT, bias_tiles, cmp_bias):
    B, T, _ = kaug.shape
    Hkv, G = NSA_KV_HEADS, NSA_GROUP
    GQ = G * Q_BLOCK
    nqb = T // Q_BLOCK
    n_cmp_pad = kc.shape[2]
    n_sel = T // SEL_BLOCK
    half_keys = SEL_STAGE_TILES * KEY_CHUNK
    NQ = NSA_STEP_QBLOCKS
    assert T % (SEL_GROUP_STAGES * half_keys) == 0 and nqb % NQ == 0
    per_head = lambda b, h, i: (b, h, 0, 0)
    v_rows = V_ROWS
    chunked = lambda w: pl.BlockSpec((1, T // w, v_rows, w), lambda b, h, i: (b, 0, h, 0))
    step_lanes = NQ * GQ
    return pl.pallas_call(
        functools.partial(_nsa_kernel, n_sel=n_sel, n_cmp_pad=n_cmp_pad),
        grid=(B, Hkv, nqb // NQ),
        in_specs=[pl.BlockSpec((1, NQ, 1, HEAD_DIM, GQ), lambda b, h, i: (b, i, h, 0, 0)),
                  pl.BlockSpec((1, 1, n_cmp_pad, HEAD_DIM), per_head),
                  pl.BlockSpec((1, 1, HEAD_DIM, n_cmp_pad), per_head),
                  pl.BlockSpec((1, T, LANES), lambda b, h, i: (b, 0, h)), chunked(half_keys),
                  pl.BlockSpec((1, T, LANES), lambda b, h, i: (b, 0, Hkv + h)), chunked(KEY_CHUNK),
                  pl.BlockSpec((1, NQ, 1, GATE_ROWS, Q_BLOCK), lambda b, h, i: (b, i, h, 0, 0)),
                  pl.BlockSpec(overlapT.shape, lambda b, h, i: (0, 0)),
                  pl.BlockSpec((1, N_ALL_TILES, KEY_CHUNK, GQ), lambda b, h, i: (h, 0, 0, 0)),
                  pl.BlockSpec((1, 2 * n_cmp_pad, GQ), lambda b, h, i: (h, 0, 0))],
        out_specs=pl.BlockSpec((1, NQ * Q_BLOCK, G * HEAD_DIM), lambda b, h, i: (b, i, h)),
        out_shape=jax.ShapeDtypeStruct((B, T, NSA_WIDTH), F32),
        scratch_shapes=[pltpu.VMEM((v_rows, step_lanes), F32), pltpu.VMEM((LANES, step_lanes), BF16),
                        pltpu.VMEM((n_sel, step_lanes), F32),
                        pltpu.VMEM((half_keys, step_lanes), F32), pltpu.VMEM((half_keys, step_lanes), F32),
                        pltpu.VMEM((half_keys, step_lanes), BF16), pltpu.VMEM((half_keys, step_lanes), BF16)],
        compiler_params=_params("parallel", "parallel", "arbitrary"), name="nsa_attention",
    )(q, kc, vcT, kaug, vsT, kaug, vwT, gates, overlapT, bias_tiles, cmp_bias)


def _rwkv_chunk_kernel(rw_ref, prev_ref, mu_ref, w0_ref, wup_ref, a0_ref, aup_ref, gup_ref, kk_ref, ka_ref,
                       rk_ref, seg_ref, tri_ref, q_ref, y0_ref, a_ref, d_ref, g_ref, bonus_ref):
    C, W, N = RWKV_CHUNK, RWKV_WIDTH, HEAD_DIM
    c = pl.program_id(1)
    x = rw_ref[0]
    R = x.shape[0]
    chunk_rows = [slice(ck * C, (ck + 1) * C) for ck in range(R // C)]
    row = lax.broadcasted_iota(jnp.int32, (R, 1), 0)
    last_prev = jnp.where(c == 0, 0.0, prev_ref[0, SUBLANES - 1:SUBLANES, :])
    x_prev = jnp.where(row == 0, last_prev, pltpu.roll(x, 1, axis=0))
    xs = x + (x_prev - x) * mu_ref[...]
    r, k, v = xs[:, 0:W], xs[:, W:2 * W], xs[:, 2 * W:3 * W]
    o = 3 * W
    wd, ad, gd = xs[:, o:o + DECAY_LORA], xs[:, o + DECAY_LORA:o + DECAY_LORA + AAA_LORA], \
        xs[:, o + DECAY_LORA + AAA_LORA:]
    w_log = -jax.nn.softplus(-(w0_ref[...] + _dot(jnp.tanh(wd).astype(BF16), wup_ref[...]))) - 0.5
    lw = -jnp.exp(w_log)
    lr = jax.nn.sigmoid(a0_ref[...] + _dot(ad.astype(BF16), aup_ref[...]))
    g_ref[0] = _dot(jax.nn.sigmoid(gd).astype(BF16), gup_ref[...])
    kk = k * kk_ref[...]
    def head_sums(z):
        return jnp.concatenate([_dot_f32_lhs(z[:, t * LANES:(t + 1) * LANES], seg_ref[...])
                                for t in range(W // LANES)], axis=1)
    kk = kk * lax.rsqrt(jnp.maximum(head_sums(kk * kk), 1e-24))
    k = k * (1.0 + (lr - 1.0) * ka_ref[...])
    bonus_ref[0] = head_sums(r * k * rk_ref[...]) * v
    a_vec, b_vec = -kk, kk * lr

    L = jnp.concatenate([_dot_f32_rhs(tri_ref[...], lw[rs]) for rs in chunk_rows], axis=0)
    L_end = jnp.concatenate([jnp.broadcast_to(L[rs.stop - 1:rs.stop, :], (C, W)) for rs in chunk_rows], axis=0)
    e_neg = jnp.exp(-L)
    e_rem = jnp.exp(L_end - L)
    At, Bt, Kt, Rt = a_vec * jnp.exp(L - lw), b_vec * e_neg, k * e_neg, r * jnp.exp(L)
    Bg, Kg = b_vec * e_rem, k * e_rem
    decay_end = [jnp.exp(L[rs.stop - 1:rs.stop, :]) for rs in chunk_rows]

    HG = RWKV_GROUP_HEADS
    GL = HG * N
    assert C == N and C & (C - 1) == 0
    blk_of = lambda idx: jnp.right_shift(idx, int(math.log2(C)))
    same_head = (blk_of(lax.broadcasted_iota(jnp.int32, (HG * C, GL), 0))
                 == blk_of(lax.broadcasted_iota(jnp.int32, (HG * C, GL), 1)))
    bf = lambda z: z.astype(BF16)
    block_diag = lambda y: jnp.where(same_head, jnp.concatenate([bf(y)] * HG, axis=0), 0.0)
    ri = lax.broadcasted_iota(jnp.int32, (C, GL), 0)
    cj = jnp.bitwise_and(lax.broadcasted_iota(jnp.int32, (C, GL), 1), C - 1)
    strict, incl = ri > cj, ri >= cj
    eye_c = (ri == cj).astype(F32)
    pr = lax.broadcasted_iota(jnp.int32, (2 * N, 2 * N), 0)
    pc = lax.broadcasted_iota(jnp.int32, (2 * N, 2 * N), 1)
    pair_same_head, pair_eye = blk_of(pr) == blk_of(pc), (pr == pc).astype(F32)
    items = [(ck, gp) for ck in range(len(chunk_rows)) for gp in range(RWKV_HEADS // HG)]
    sl = [(chunk_rows[ck], slice(gp * GL, (gp + 1) * GL)) for ck, gp in items]
    ar = [bf(jnp.concatenate([At[s], Rt[s]], axis=0)) for s in sl]
    zb = [_dot_nt(a, block_diag(Bt[s])) for a, s in zip(ar, sl)]
    zk = [_dot_nt(a, block_diag(Kt[s])) for a, s in zip(ar, sl)]
    n_mat = [jnp.where(strict, z[0:C], 0.0) for z in zb]
    m_mat = [jnp.where(strict, z[0:C], 0.0) for z in zk]
    v_bd = [block_diag(v[s]) for s in sl]
    mv = [_dot(bf(m), vb) for m, vb in zip(m_mat, v_bd)]
    t_inv, n_pow = [eye_c + n for n in n_mat], n_mat
    for _ in range(int(math.log2(C)) - 1):
        n_pow = [_dot(bf(n), block_diag(n)) for n in n_pow]
        t_inv = [t + _dot(bf(t), block_diag(n)) for t, n in zip(t_inv, n_pow)]
    t_bf = [bf(t) for t in t_inv]
    ta = [_dot(t, block_diag(At[s])) for t, s in zip(t_bf, sl)]
    g0 = [_dot(t, block_diag(m)) for t, m in zip(t_bf, mv)]
    p_mat = [bf(jnp.where(incl, z[C:], 0.0)) for z in zb]
    pk_mat = [bf(jnp.where(incl, z[C:], 0.0)) for z in zk]
    q_out = [Rt[s] + _dot(p, block_diag(t)) for p, t, s in zip(p_mat, ta, sl)]
    y0_out = [_dot(p, block_diag(g)) + _dot(pk, vb) for p, g, pk, vb in zip(p_mat, g0, pk_mat, v_bd)]
    bgT = [bf(Bg[s].T) for s in sl]
    a_full = [_dot(b, bf(t)) for b, t in zip(bgT, ta)]
    d_full = [_dot(b, bf(g)) + _dot(bf(Kg[s].T), bf(v[s])) for b, g, s in zip(bgT, g0, sl)]
    for n, (ck, gp) in enumerate(items):
        rows, lanes = sl[n]
        q_ref[0, rows, lanes] = q_out[n]
        y0_ref[0, rows, lanes] = y0_out[n]
        for pp in range(HG // 2):
            blk = slice(pp * 2 * N, (pp + 1) * 2 * N)
            pair_lanes = slice(lanes.start + blk.start, lanes.start + blk.stop)
            a_ref[0, ck, gp * (HG // 2) + pp] = (jnp.where(pair_same_head, a_full[n][blk, blk], 0.0)
                                                 + pair_eye * decay_end[ck][:, pair_lanes])
            d_ref[0, ck, gp * (HG // 2) + pp] = jnp.where(pair_same_head, d_full[n][blk, blk], 0.0)


def _rwkv_chunks(rw, mu, w0, w_up, a0, a_up, g_up, k_k, k_a, r_k):
    B, T, cols = rw.shape
    C, W, H, N = RWKV_CHUNK, RWKV_WIDTH, RWKV_HEADS, HEAD_DIM
    nc = T // C
    S = RWKV_STEP_CHUNKS
    R = S * C
    assert nc % S == 0
    seg = jnp.asarray(np.kron(np.eye(LANES // N), np.ones((N, N))), BF16)
    tri = jnp.asarray(np.tril(np.ones((C, C))), BF16)
    row = lambda z: z.reshape(1, -1)
    const = lambda b, c: (0, 0)
    vec = pl.BlockSpec((1, W), const)
    tok = pl.BlockSpec((1, R, W), lambda b, c: (b, c, 0))
    mat = pl.BlockSpec((1, S, H // 2, 2 * N, 2 * N), lambda b, c: (b, c, 0, 0, 0))
    return pl.pallas_call(
        _rwkv_chunk_kernel,
        grid=(B, nc // S),
        in_specs=[pl.BlockSpec((1, R, cols), lambda b, c: (b, c, 0)),
                  pl.BlockSpec((1, SUBLANES, cols), lambda b, c: (b, jnp.maximum(c * (R // SUBLANES) - 1, 0), 0)),
                  pl.BlockSpec((1, cols), const), vec,
                  pl.BlockSpec((DECAY_LORA, W), const), vec,
                  pl.BlockSpec((AAA_LORA, W), const),
                  pl.BlockSpec((GATE_LORA, W), const), vec, vec, vec,
                  pl.BlockSpec((LANES, LANES), const), pl.BlockSpec((C, C), const)],
        out_specs=[tok, tok, mat, mat, tok, tok],
        out_shape=[jax.ShapeDtypeStruct((B, T, W), F32), jax.ShapeDtypeStruct((B, T, W), F32),
                   jax.ShapeDtypeStruct((B, nc, H // 2, 2 * N, 2 * N), F32),
                   jax.ShapeDtypeStruct((B, nc, H // 2, 2 * N, 2 * N), F32),
                   jax.ShapeDtypeStruct((B, T, W), F32), jax.ShapeDtypeStruct((B, T, W), F32)],
        compiler_params=_params("parallel", "parallel"), name="rwkv_chunks",
    )(rw, rw, row(mu), row(w0), w_up.astype(BF16), row(a0), a_up.astype(BF16), g_up.astype(BF16),
      row(k_k), row(k_a), row(r_k), seg, tri)


def _rwkv_scan_kernel(a_ref, d_ref, q_ref, y0_ref, g_ref, bonus_ref, lw_ref, lb_ref, seg_ref, o_ref, h_sc):
    @pl.when(pl.program_id(0) == 0)
    def _():
        h_sc[...] = jnp.zeros(h_sc.shape, F32)

    inv_n = 1.0 / HEAD_DIM
    items = [(b, pair, slice(pair * LANES, (pair + 1) * LANES))
             for b in range(q_ref.shape[0]) for pair in range(RWKV_HEADS // 2)]
    states = [h_sc[b, pair] for b, pair, _ in items]
    ys = [_dot(q_ref[b, :, lanes].astype(BF16), st.astype(BF16)) + y0_ref[b, :, lanes]
          for (b, _, lanes), st in zip(items, states)]
    for (b, pair, _), st in zip(items, states):
        h_sc[b, pair] = _dot(a_ref[b, 0, pair], st, precision=HIGHEST) + d_ref[b, 0, pair]
    means = [_dot_f32_lhs(y, seg_ref[...]) * inv_n for y in ys]
    cen = [y - mean for y, mean in zip(ys, means)]
    var = [_dot_f32_lhs(jnp.square(c), seg_ref[...]) * inv_n for c in cen]
    for (b, _, lanes), c, v in zip(items, cen, var):
        yn = c * lax.rsqrt(v + LNX_EPS) * lw_ref[:, lanes] + lb_ref[:, lanes]
        o_ref[b, :, lanes] = (yn + bonus_ref[b, :, lanes]) * g_ref[b, :, lanes]


def _rwkv_scan(A, D, Q, Y0, g, bonus, lnx_w, lnx_b):
    B, nc, P, N2, _ = A.shape
    T, W, C = Q.shape[1], Q.shape[2], RWKV_CHUNK
    tok = pl.BlockSpec((B, C, W), lambda c: (0, c, 0))
    mat = pl.BlockSpec((B, 1, P, N2, N2), lambda c: (0, c, 0, 0, 0))
    vec = pl.BlockSpec((1, W), lambda c: (0, 0))
    seg = jnp.asarray(np.kron(np.eye(2), np.ones((HEAD_DIM, HEAD_DIM))), BF16)
    return pl.pallas_call(
        _rwkv_scan_kernel,
        grid=(nc,),
        in_specs=[mat, mat, tok, tok, tok, tok, vec, vec, pl.BlockSpec((N2, N2), lambda c: (0, 0))],
        out_specs=tok,
        out_shape=jax.ShapeDtypeStruct((B, T, W), F32),
        scratch_shapes=[pltpu.VMEM((B, P, N2, N2), F32)],
        compiler_params=_params("arbitrary"), name="rwkv_scan",
    )(A, D, Q, Y0, g, bonus, lnx_w.reshape(1, W), lnx_b.reshape(1, W), seg)


def _mix_xattn_kernel(x_ref, on_ref, or_ref, wo1_ref, wo2_ref, gx_ref, wq_ref, k_ref, v_ref, wo_ref, o_ref):
    x1 = x_ref[0] + _dot(on_ref[0].astype(BF16), wo1_ref[...]) + _dot(or_ref[0].astype(BF16), wo2_ref[...])
    q = _dot(_rms(x1, gx_ref[...]).astype(BF16), wq_ref[...])
    dh = q.shape[-1] // XATTN_HEADS
    qbf = (q * (dh ** -0.5)).astype(BF16)
    outs = []
    for h in range(XATTN_HEADS):
        hs = slice(h * dh, (h + 1) * dh)
        s = _dot_nt(qbf[:, hs], k_ref[0, :, hs])
        e = jnp.exp(s - jnp.max(s, axis=-1, keepdims=True))
        p = e / jnp.sum(e, axis=-1, keepdims=True)
        outs.append(_dot(p.astype(BF16), v_ref[0, :, hs]))
    o = jnp.concatenate(outs, axis=1).astype(BF16)
    o_ref[0] = x1 + _dot(o, wo_ref[...])


def _mix_xattn(x, o_nsa, o_rwkv, w_out, norm_x_g, w_q, mem_k, mem_v, w_o):
    B, T, D = x.shape
    M = mem_k.shape[1]
    tm = min(ROW_TILE, T)
    wo1, wo2 = w_out[:NSA_WIDTH].astype(BF16), w_out[NSA_WIDTH:].astype(BF16)
    const = lambda b, i: (0, 0)
    tile = lambda w: pl.BlockSpec((1, tm, w), lambda b, i: (b, i, 0))
    return pl.pallas_call(
        _mix_xattn_kernel,
        grid=(B, T // tm),
        in_specs=[tile(D), tile(NSA_WIDTH), tile(RWKV_WIDTH),
                  pl.BlockSpec(wo1.shape, const), pl.BlockSpec(wo2.shape, const),
                  pl.BlockSpec((1, D), const), pl.BlockSpec((D, D), const),
                  pl.BlockSpec((1, M, D), lambda b, i: (b, 0, 0)), pl.BlockSpec((1, M, D), lambda b, i: (b, 0, 0)),
                  pl.BlockSpec((D, D), const)],
        out_specs=tile(D),
        out_shape=jax.ShapeDtypeStruct((B, T, D), F32),
        compiler_params=_params("parallel", "parallel"), name="mix_xattn",
    )(x, o_nsa, o_rwkv, wo1, wo2, norm_x_g.reshape(1, D), w_q.astype(BF16), mem_k, mem_v, w_o.astype(BF16))


def _ffn_kernel(x_ref, g_ref, wg_ref, wu_ref, wd_ref, gf_ref, o_ref, *, final_norm):
    x = x_ref[...]
    h = _rms(x, g_ref[...]).astype(BF16)
    act = (jax.nn.silu(_dot(h, wg_ref[...])) * _dot(h, wu_ref[...])).astype(BF16)
    y = x + _dot(act, wd_ref[...])
    o_ref[...] = _rms(y, gf_ref[...]) if final_norm else y


def _ffn(x, norm_g, w_gate, w_up, w_down, final_g, final_norm):
    R, D = x.shape
    F = w_gate.shape[1]
    tm = min(FFN_ROW_TILE, R)
    const = lambda i: (0, 0)
    resident = lambda shape: pl.BlockSpec(shape, const, pipeline_mode=pl.Buffered(1))
    return pl.pallas_call(
        functools.partial(_ffn_kernel, final_norm=final_norm),
        grid=(R // tm,),
        in_specs=[pl.BlockSpec((tm, D), lambda i: (i, 0)), pl.BlockSpec((1, D), const),
                  resident((D, F)), resident((D, F)), resident((F, D)), pl.BlockSpec((1, D), const)],
        out_specs=pl.BlockSpec((tm, D), lambda i: (i, 0)),
        out_shape=jax.ShapeDtypeStruct((R, D), F32),
        compiler_params=_params("parallel"), name="ffn",
    )(x, norm_g.reshape(1, D), w_gate.astype(BF16), w_up.astype(BF16), w_down.astype(BF16),
      final_g.reshape(1, D))


def _overlap_matrix(n_cmp_pad, n_sel):
    c = np.arange(n_cmp_pad)[:, None] * CMP_STRIDE
    s = np.arange(n_sel)[None, :] * SEL_BLOCK
    return ((c <= s + SEL_BLOCK - 1) & (c + CMP_LEN - 1 >= s)).astype(np.float32)


def _layer(x, mem, rel_bias, final_g, is_last, norm_mix_g, w_in, nsa_gate_b, cmp_pe_k, cmp_pe_v,
           cmp_k_w1, cmp_k_b1, cmp_k_w2, cmp_v_w1, cmp_v_b1, cmp_v_w2,
           rwkv_mu, rwkv_w0, rwkv_w_up, rwkv_a0, rwkv_a_up, rwkv_g_up,
           rwkv_k_k, rwkv_k_a, rwkv_r_k, rwkv_lnx_w, rwkv_lnx_b, w_out,
           norm_x_g, norm_mem_g, w_q_x, w_kv_x, w_o_x, norm_ffn_g, w_gate, w_up, w_down):
    B, T, D = x.shape
    Hkv, G, dh = NSA_KV_HEADS, NSA_GROUP, HEAD_DIM
    q, kvc, kaug, vsT, vwT, gates, rw = _proj_in(x, norm_mix_g, w_in, nsa_gate_b)

    n16 = T // CMP_STRIDE
    kvc = kvc.reshape(B, T, 2 * KV_WIDTH)
    kc = _compress(kvc, 0, cmp_pe_k, cmp_k_w1, cmp_k_b1, cmp_k_w2, False)
    vcT = _compress(kvc, 1, cmp_pe_v, cmp_v_w1, cmp_v_b1, cmp_v_w2, True)
    bias_tiles, cmp_bias = _bias_tiles(rel_bias, n16)
    o_nsa = _nsa_attention(q, kc, vcT, kaug.reshape(B, T, 2 * Hkv * LANES), vsT, vwT,
                           gates, jnp.asarray(_overlap_matrix(n16, T // SEL_BLOCK).T, BF16), bias_tiles, cmp_bias)

    Q, Y0, A, Dm, g, bonus = _rwkv_chunks(rw.reshape(B, T, RWKV_COLS), rwkv_mu, rwkv_w0, rwkv_w_up, rwkv_a0,
                                          rwkv_a_up, rwkv_g_up, rwkv_k_k, rwkv_k_a, rwkv_r_k.reshape(-1))
    o_rwkv = _rwkv_scan(A, Dm, Q, Y0, g, bonus, rwkv_lnx_w, rwkv_lnx_b)

    M = mem.shape[1]
    (kv_mem,) = _norm_matmul(mem.reshape(B * M, D), norm_mem_g, [w_kv_x.astype(BF16)], [None], [BF16], ROW_TILE)
    kv_mem = kv_mem.reshape(B, M, 2 * D)
    x = _mix_xattn(x, o_nsa, o_rwkv, w_out, norm_x_g, w_q_x, kv_mem[..., :D], kv_mem[..., D:], w_o_x)
    x = _ffn(x.reshape(B * T, D), norm_ffn_g, w_gate, w_up, w_down, final_g, is_last)
    return x.reshape(B, T, D)


def kernel(x, mem, rel_bias, norm_f_g, norm_mix_g, w_in, nsa_gate_b, cmp_pe_k, cmp_pe_v, cmp_k_w1, cmp_k_b1, cmp_k_w2, cmp_v_w1, cmp_v_b1, cmp_v_w2, rwkv_mu, rwkv_w0, rwkv_w_up, rwkv_a0, rwkv_a_up, rwkv_g_up, rwkv_k_k, rwkv_k_a, rwkv_r_k, rwkv_lnx_w, rwkv_lnx_b, w_out, norm_x_g, norm_mem_g, w_q_x, w_kv_x, w_o_x, norm_ffn_g, w_gate, w_up, w_down):
    stacked = (norm_mix_g, w_in, nsa_gate_b, cmp_pe_k, cmp_pe_v, cmp_k_w1, cmp_k_b1, cmp_k_w2, cmp_v_w1,
               cmp_v_b1, cmp_v_w2, rwkv_mu, rwkv_w0, rwkv_w_up, rwkv_a0, rwkv_a_up, rwkv_g_up, rwkv_k_k,
               rwkv_k_a, rwkv_r_k, rwkv_lnx_w, rwkv_lnx_b, w_out, norm_x_g, norm_mem_g, w_q_x, w_kv_x, w_o_x,
               norm_ffn_g, w_gate, w_up, w_down)
    depth = w_in.shape[0]
    for l in range(depth):
        x = _layer(x, mem, rel_bias, norm_f_g, l == depth - 1, *[p[l] for p in stacked])
    return x
```

```python
import functools
import math

import numpy as np
import jax
import jax.numpy as jnp
from jax import lax
from jax.experimental import pallas as pl
from jax.experimental.pallas import tpu as pltpu

F32 = jnp.float32
BF16 = jnp.bfloat16

LANES = 128
SUBLANES = 8
BF16_ROWS = 16
VMEM_LIMIT_BYTES = 56 * 1024 * 1024

HEAD_DIM = 64
NSA_HEADS = 8
NSA_KV_HEADS = 2
NSA_GROUP = NSA_HEADS // NSA_KV_HEADS
NSA_WIDTH = NSA_HEADS * HEAD_DIM
KV_WIDTH = NSA_KV_HEADS * HEAD_DIM
RWKV_HEADS = 8
RWKV_WIDTH = RWKV_HEADS * HEAD_DIM
CMP_LEN = 32
CMP_STRIDE = 16
SEL_BLOCK = 64
SEL_SHIFT = 6
SEL_TOP = 16
WINDOW = 512
Q_BLOCK = 128
DECAY_LORA = 64
AAA_LORA = 64
GATE_LORA = 128
N_BUCKETS = 32
MAX_DISTANCE = 2048
XATTN_HEADS = 4
RMS_EPS = 1e-6
LNX_EPS = 64e-5
FORCE_SCORE = 1e4
NEG_SCORE = -1e9
MASK_SCORE = -1e30
LOG2E = math.log2(math.e)
RWKV_COLS = 3 * RWKV_WIDTH + DECAY_LORA + AAA_LORA + GATE_LORA
NSA_COLS = NSA_WIDTH + 6 * KV_WIDTH + 3 * NSA_HEADS

KEY_CHUNK = 128
RWKV_CHUNK = 64
RWKV_SCAN_STEP_CHUNKS = 8
RWKV_READOUT_STEP_CHUNKS = 4
RWKV_GROUP_HEADS = 2
RWKV_STEP_CHUNKS = 8
ROW_TILE = 512
FFN_ROW_TILE = 512


def _t5_thresholds():
    d = np.arange(0, 2 * MAX_DISTANCE, dtype=np.int64)
    max_exact = N_BUCKETS // 2
    nf = np.maximum(d, 1).astype(np.float32)
    large = max_exact + (np.log(nf / np.float32(max_exact)) / np.float32(math.log(MAX_DISTANCE / max_exact))
                         * np.float32(N_BUCKETS - max_exact)).astype(np.int32)
    bucket = np.where(d < max_exact, d, np.minimum(large, N_BUCKETS - 1))
    return [int(np.argmax(bucket >= k)) for k in range(N_BUCKETS)]


T5_THRESHOLDS = _t5_thresholds()
N_BIAS_TILES = -(-(T5_THRESHOLDS[-1] + KEY_CHUNK) // KEY_CHUNK) + 1
TILE_MASKED = N_BIAS_TILES
TILE_WINDOW_EDGE = N_BIAS_TILES + 1
N_ALL_TILES = N_BIAS_TILES + 2
SEL_STEP_BLOCKS = 16
SEL_STAGE_TILES = 4
SEL_GROUP_STAGES = SEL_STEP_BLOCKS * SEL_BLOCK // (SEL_STAGE_TILES * KEY_CHUNK)
NSA_STEP_QBLOCKS = 4
V_ROWS = HEAD_DIM + BF16_ROWS
GATE_ROWS = 16


def _params(*semantics):
    return pltpu.CompilerParams(dimension_semantics=semantics, vmem_limit_bytes=VMEM_LIMIT_BYTES)


def _rms(x, g):
    return x * lax.rsqrt(jnp.mean(x * x, axis=-1, keepdims=True) + RMS_EPS) * g


def _dot(a, b, **kw):
    return jnp.dot(a, b, preferred_element_type=F32, **kw)


def _split3(x):
    hi = x.astype(BF16)
    r1 = x - hi.astype(F32)
    mid = r1.astype(BF16)
    lo = (r1 - mid.astype(F32)).astype(BF16)
    return hi, mid, lo


def _dot_f32_lhs(x, w01, pieces=3):
    w = w01.astype(BF16)
    hi, mid, lo = _split3(x)
    return _dot(hi, w) + (_dot(mid, w) + _dot(lo, w) if pieces == 3 else _dot(mid, w))


def _dot_f32_rhs(w01, x):
    w = w01.astype(BF16)
    hi, mid, lo = _split3(x)
    return _dot(w, hi) + (_dot(w, mid) + _dot(w, lo))


def _dot_hi_lo(a, b):
    a_hi = a.astype(BF16)
    a_lo = (a - a_hi.astype(F32)).astype(BF16)
    b_hi = b.astype(BF16)
    b_lo = (b - b_hi.astype(F32)).astype(BF16)
    return _dot(a_hi, b_hi) + (_dot(a_hi, b_lo) + _dot(a_lo, b_hi))


def _dot_nt(a, b, **kw):
    return lax.dot_general(a, b, (((1,), (1,)), ((), ())), preferred_element_type=F32, **kw)


def _norm_matmul_kernel(x_ref, g_ref, *refs, nseg, bias_flags):
    nb = sum(bias_flags)
    w_refs, b_refs, o_refs = refs[:nseg], refs[nseg:nseg + nb], refs[nseg + nb:]
    xn = _rms(x_ref[...], g_ref[...]).astype(BF16)
    bi = 0
    for s in range(nseg):
        y = _dot(xn, w_refs[s][...])
        if bias_flags[s]:
            y = y + b_refs[bi][...]
            bi += 1
        o_refs[s][...] = y.astype(o_refs[s].dtype)


def _norm_matmul(x, g, weights, biases, out_dtypes, row_tile):
    R, D = x.shape
    tm = min(row_tile, R)
    assert R % tm == 0
    nseg = len(weights)
    bias_flags = tuple(b is not None for b in biases)
    const = lambda i: (0, 0)
    in_specs = [pl.BlockSpec((tm, D), lambda i: (i, 0)), pl.BlockSpec((1, D), const)]
    in_specs += [pl.BlockSpec(w.shape, const) for w in weights]
    in_specs += [pl.BlockSpec((1, b.shape[-1]), const) for b in biases if b is not None]
    out_specs = [pl.BlockSpec((tm, w.shape[1]), lambda i: (i, 0)) for w in weights]
    out_shape = [jax.ShapeDtypeStruct((R, w.shape[1]), dt) for w, dt in zip(weights, out_dtypes)]
    return pl.pallas_call(
        functools.partial(_norm_matmul_kernel, nseg=nseg, bias_flags=bias_flags),
        grid=(R // tm,), in_specs=in_specs, out_specs=out_specs, out_shape=out_shape,
        compiler_params=_params("parallel"), name="norm_matmul",
    )(x, g.reshape(1, D), *weights, *[b.reshape(1, -1) for b in biases if b is not None])


def _proj_in_kernel(x_ref, g_ref, wq_ref, wc_ref, wk_ref, wvT_ref, wgT_ref, bg_ref, wr_ref,
                    q_ref, kvc_ref, kaug_ref, vsT_ref, vwT_ref, gate_ref, rw_ref, *, seq_len):
    tm = x_ref.shape[0]
    xn = _rms(x_ref[...], g_ref[...]).astype(BF16)
    qT = (_dot_nt(wq_ref[...], xn) * (HEAD_DIM ** -0.5 * LOG2E)).astype(BF16)
    for j in range(tm // Q_BLOCK):
        for hg in range(NSA_HEADS):
            h, g = divmod(hg, NSA_GROUP)
            q_ref[0, j, h, :, g * Q_BLOCK:(g + 1) * Q_BLOCK] = qT[hg * HEAD_DIM:(hg + 1) * HEAD_DIM,
                                                                  j * Q_BLOCK:(j + 1) * Q_BLOCK]
    kvc_ref[...] = _dot(xn, wc_ref[...])
    rw_ref[...] = _dot(xn, wr_ref[...])
    k_all = _dot(xn, wk_ref[...])
    tok = lax.rem(pl.program_id(0) * tm, seq_len) + lax.broadcasted_iota(jnp.int32, k_all.shape, 0)
    lane = lax.broadcasted_iota(jnp.int32, k_all.shape, 1)
    blk = jnp.bitwise_and(jnp.right_shift(tok, SEL_SHIFT), SEL_STEP_BLOCKS - 1)
    hot = (jnp.bitwise_and(lane, LANES - 1) == HEAD_DIM + blk) & (lane < NSA_KV_HEADS * LANES)
    kaug_ref[...] = jnp.where(hot, 1.0, k_all).astype(BF16)
    vT = _dot_nt(wvT_ref[...], xn)
    row = lax.broadcasted_iota(jnp.int32, vT.shape, 0)
    ones_row = row == HEAD_DIM
    for grp in range(1, 2 * NSA_KV_HEADS):
        ones_row = ones_row | (row == grp * V_ROWS + HEAD_DIM)
    vT = jnp.where(ones_row, 1.0, vT).astype(BF16)
    half = NSA_KV_HEADS * V_ROWS
    stage_keys = vsT_ref.shape[3]
    for c in range(tm // stage_keys):
        vsT_ref[0, c] = vT[0:half, c * stage_keys:(c + 1) * stage_keys]
    for c in range(tm // KEY_CHUNK):
        vwT_ref[0, c] = vT[half:, c * KEY_CHUNK:(c + 1) * KEY_CHUNK]
    gT = _dot_nt(wgT_ref[...], xn) + bg_ref[...]
    for j in range(tm // Q_BLOCK):
        for h in range(NSA_KV_HEADS):
            gate_ref[0, j, h] = gT[h * GATE_ROWS:(h + 1) * GATE_ROWS, j * Q_BLOCK:(j + 1) * Q_BLOCK]


def _proj_in(x, norm_g, w_in, gate_b):
    B, T, D = x.shape
    Hkv, G, dh = NSA_KV_HEADS, NSA_GROUP, HEAD_DIM
    tm = ROW_TILE
    stage_keys = SEL_STAGE_TILES * KEY_CHUNK
    assert T % tm == 0 and tm % Q_BLOCK == 0 and tm % stage_keys == 0
    kv0 = NSA_WIDTH
    g0 = kv0 + 6 * KV_WIDTH
    stream = lambda s: w_in[:, kv0 + s * KV_WIDTH:kv0 + (s + 1) * KV_WIDTH].reshape(D, Hkv, dh)
    pad_cols = lambda w: jnp.pad(w, ((0, 0), (0, 0), (0, LANES - dh))).reshape(D, Hkv * LANES)
    pad_rows = lambda w: jnp.pad(w.transpose(1, 2, 0), ((0, 0), (0, V_ROWS - dh), (0, 0))).reshape(Hkv * V_ROWS, D)
    w_k = jnp.concatenate([pad_cols(stream(2)), pad_cols(stream(4))], axis=1)
    w_vT = jnp.concatenate([pad_rows(stream(3)), pad_rows(stream(5))], axis=0)
    reorder = lambda a: a.reshape(-1, Hkv, G, 3).transpose(1, 3, 2, 0).reshape(Hkv, 3 * G, -1)
    pad_gate = lambda a: jnp.pad(a, ((0, 0), (0, GATE_ROWS - 3 * G), (0, 0))).reshape(Hkv * GATE_ROWS, -1)
    w_gT = pad_gate(reorder(w_in[:, g0:NSA_COLS]))
    b_g = pad_gate(reorder(gate_b.reshape(1, -1)))
    weights = [w_in[:, :kv0].T, w_in[:, kv0:kv0 + 2 * KV_WIDTH], w_k, w_vT, w_gT]
    weights = [w.astype(BF16) for w in weights] + [b_g, w_in[:, NSA_COLS:].astype(BF16)]
    nt = T // tm
    rows = lambda n: pl.BlockSpec((tm, n), lambda i: (i, 0))
    const = lambda i: (0, 0)
    return pl.pallas_call(
        functools.partial(_proj_in_kernel, seq_len=T),
        grid=(B * nt,),
        in_specs=[rows(D), pl.BlockSpec((1, D), const)] + [pl.BlockSpec(w.shape, const) for w in weights],
        out_specs=[pl.BlockSpec((1, tm // Q_BLOCK, Hkv, dh, G * Q_BLOCK), lambda i: (i // nt, i % nt, 0, 0, 0)),
                   rows(2 * KV_WIDTH), rows(2 * Hkv * LANES),
                   pl.BlockSpec((1, tm // stage_keys, Hkv * V_ROWS, stage_keys), lambda i: (i // nt, i % nt, 0, 0)),
                   pl.BlockSpec((1, tm // KEY_CHUNK, Hkv * V_ROWS, KEY_CHUNK), lambda i: (i // nt, i % nt, 0, 0)),
                   pl.BlockSpec((1, tm // Q_BLOCK, Hkv, GATE_ROWS, Q_BLOCK), lambda i: (i // nt, i % nt, 0, 0, 0)),
                   rows(RWKV_COLS)],
        out_shape=[jax.ShapeDtypeStruct((B, T // Q_BLOCK, Hkv, dh, G * Q_BLOCK), BF16),
                   jax.ShapeDtypeStruct((B * T, 2 * KV_WIDTH), F32),
                   jax.ShapeDtypeStruct((B * T, 2 * Hkv * LANES), BF16),
                   jax.ShapeDtypeStruct((B, T // stage_keys, Hkv * V_ROWS, stage_keys), BF16),
                   jax.ShapeDtypeStruct((B, T // KEY_CHUNK, Hkv * V_ROWS, KEY_CHUNK), BF16),
                   jax.ShapeDtypeStruct((B, T // Q_BLOCK, Hkv, GATE_ROWS, Q_BLOCK), F32),
                   jax.ShapeDtypeStruct((B * T, RWKV_COLS), F32)],
        compiler_params=_params("parallel"), name="proj_in",
    )(x.reshape(B * T, D), norm_g.reshape(1, D), *weights)


def _compress_kernel(x_ref, pe_ref, w1_ref, b1_ref, w2_ref, o_ref, *, transpose_out):
    n16 = x_ref.shape[1] // CMP_STRIDE
    hidden = w1_ref.shape[2] // NSA_KV_HEADS
    lo = jnp.zeros((n16, w1_ref.shape[2]), F32)
    hi = jnp.zeros((n16, w1_ref.shape[2]), F32)
    for l in range(CMP_STRIDE):
        rows = x_ref[0, pl.ds(l, n16, stride=CMP_STRIDE), :]
        lo = lo + _dot((rows + pe_ref[l:l + 1, :]).astype(BF16), w1_ref[l])
        hi = hi + _dot((rows + pe_ref[CMP_STRIDE + l:CMP_STRIDE + l + 1, :]).astype(BF16), w1_ref[CMP_STRIDE + l])
    h = lo + pltpu.roll(hi, n16 - 1, axis=0) + b1_ref[...]
    h = jax.nn.gelu(h).astype(BF16)
    for hkv in range(NSA_KV_HEADS):
        hh = h[:, hkv * hidden:(hkv + 1) * hidden]
        if transpose_out:
            o_ref[0, hkv] = _dot_nt(w2_ref[...], hh).astype(o_ref.dtype)
        else:
            o_ref[0, hkv] = _dot(hh, w2_ref[...]).astype(o_ref.dtype)


def _compress(kvc, stream, pe, w1, b1, w2, transpose_out):
    B, T, _ = kvc.shape
    H, dh = NSA_KV_HEADS, HEAD_DIM
    n16 = T // CMP_STRIDE
    hidden = w1.shape[1]
    eye = jnp.eye(H, dtype=w1.dtype)
    w1_bd = jnp.einsum('ldn,hg->lhdgn', w1.reshape(CMP_LEN, dh, hidden), eye).reshape(CMP_LEN, H * dh, H * hidden)
    w2b = (w2.T if transpose_out else w2).astype(BF16)
    oshape = (B, H, dh, n16) if transpose_out else (B, H, n16, dh)
    return pl.pallas_call(
        functools.partial(_compress_kernel, transpose_out=transpose_out),
        grid=(B,),
        in_specs=[pl.BlockSpec((1, T, H * dh), lambda b: (b, 0, stream)),
                  pl.BlockSpec((CMP_LEN, H * dh), lambda b: (0, 0)),
                  pl.BlockSpec(w1_bd.shape, lambda b: (0, 0, 0)),
                  pl.BlockSpec((1, H * hidden), lambda b: (0, 0)),
                  pl.BlockSpec(w2b.shape, lambda b: (0, 0))],
        out_specs=pl.BlockSpec((1,) + oshape[1:], lambda b: (b, 0, 0, 0)),
        out_shape=jax.ShapeDtypeStruct(oshape, BF16),
        compiler_params=_params("parallel"), name="nsa_compress",
    )(kvc, jnp.tile(pe, (1, H)), w1_bd.astype(BF16), jnp.tile(b1.reshape(1, hidden), (1, H)), w2b)


def _bias_of_distance(tab_ref, h, d):
    val = jnp.full(d.shape, tab_ref[h, 0], F32)
    for k in range(1, N_BUCKETS):
        val = jnp.where(d >= T5_THRESHOLDS[k], tab_ref[h, k], val)
    return val * LOG2E


def _bias_tiles_kernel(tab_ref, bt_ref, cb_ref, *, n_cmp_pad):
    hkv = pl.program_id(0)
    j = lax.broadcasted_iota(jnp.int32, (KEY_CHUNK, Q_BLOCK), 0)
    i = lax.broadcasted_iota(jnp.int32, (KEY_CHUNK, Q_BLOCK), 1)
    r2 = lax.broadcasted_iota(jnp.int32, (2 * n_cmp_pad, Q_BLOCK), 0)
    i2 = lax.broadcasted_iota(jnp.int32, (2 * n_cmp_pad, Q_BLOCK), 1)
    l2 = r2 - (n_cmp_pad - KEY_CHUNK)
    d2 = i2 - CMP_STRIDE * l2 + (CMP_STRIDE * KEY_CHUNK - Q_BLOCK - (CMP_LEN - 1))
    hidden2 = (l2 >= KEY_CHUNK) | ((l2 >= 0) & (d2 < 0))
    d2 = jnp.where((l2 >= 0) & (l2 < KEY_CHUNK), d2, 2 * MAX_DISTANCE)
    for g in range(NSA_GROUP):
        h = hkv * NSA_GROUP + g
        lanes = slice(g * Q_BLOCK, (g + 1) * Q_BLOCK)
        for m in range(N_BIAS_TILES):
            tile = _bias_of_distance(tab_ref, h, m * KEY_CHUNK + i - j)
            if m == 0:
                tile = jnp.where(j <= i, tile, MASK_SCORE)
            bt_ref[0, m, :, lanes] = tile
        bt_ref[0, TILE_MASKED, :, lanes] = jnp.full((KEY_CHUNK, Q_BLOCK), MASK_SCORE, F32)
        edge = _bias_of_distance(tab_ref, h, WINDOW + i - j)
        bt_ref[0, TILE_WINDOW_EDGE, :, lanes] = jnp.where(j > i, edge, MASK_SCORE)
        cb_ref[0, :, lanes] = jnp.where(hidden2, MASK_SCORE, _bias_of_distance(tab_ref, h, d2))


def _bias_tiles(rel_bias, n_cmp_pad):
    assert CMP_STRIDE * KEY_CHUNK - Q_BLOCK - (CMP_LEN - 1) >= T5_THRESHOLDS[-1]
    GQ = NSA_GROUP * Q_BLOCK
    return pl.pallas_call(
        functools.partial(_bias_tiles_kernel, n_cmp_pad=n_cmp_pad),
        grid=(NSA_KV_HEADS,),
        in_specs=[pl.BlockSpec(memory_space=pltpu.SMEM)],
        out_specs=[pl.BlockSpec((1, N_ALL_TILES, KEY_CHUNK, GQ), lambda h: (h, 0, 0, 0)),
                   pl.BlockSpec((1, 2 * n_cmp_pad, GQ), lambda h: (h, 0, 0))],
        out_shape=[jax.ShapeDtypeStruct((NSA_KV_HEADS, N_ALL_TILES, KEY_CHUNK, GQ), F32),
                   jax.ShapeDtypeStruct((NSA_KV_HEADS, 2 * n_cmp_pad, GQ), F32)],
        compiler_params=_params("parallel"), name="t5_bias_tiles",
    )(rel_bias.T)


def _nsa_kernel(q_ref, kc_ref, vcT_ref, ks_ref, vsT_ref, kw_ref, vwT_ref, gate_ref, ovT_ref, bt_ref, cb_ref,
                o_ref, acc_sc, qaug_sc, seladd_sc, s0_sc, s1_sc, p0_sc, p1_sc, *, n_sel, n_cmp_pad):
    G, NQ = NSA_GROUP, NSA_STEP_QBLOCKS
    GQ = G * Q_BLOCK
    qbs = [pl.program_id(2) * NQ + x for x in range(NQ)]
    per_block = lambda fn: jnp.concatenate([fn(x) for x in range(NQ)], axis=1)
    tile_g = lambda a: jnp.concatenate([a] * G, axis=1)
    qT = per_block(lambda x: q_ref[0, x, 0])
    qaug_sc[0:HEAD_DIM, :] = qT
    qaug_sc[HEAD_DIM:, :] = jnp.zeros((qaug_sc.shape[0] - HEAD_DIM, NQ * GQ), BF16)
    lane_q = lax.broadcasted_iota(jnp.int32, (1, Q_BLOCK), 1)
    t = per_block(lambda x: qbs[x] * Q_BLOCK + lane_q)

    def bias_tile(dist_of_block):
        def one(x):
            dist = dist_of_block(x)
            return bt_ref[0, jnp.where(dist < 0, TILE_MASKED, jnp.minimum(dist, N_BIAS_TILES - 1))]
        return per_block(one)

    n_back = WINDOW // KEY_CHUNK
    q_win = qaug_sc[...]
    win = {}

    def win_scores(x, back):
        kc = jnp.maximum(qbs[x] - back, 0)
        edge = TILE_WINDOW_EDGE if back == n_back else back
        tile = jnp.where(qbs[x] >= back, edge, TILE_MASKED)
        k_chunk = kw_ref[0, pl.ds(pl.multiple_of(kc * KEY_CHUNK, KEY_CHUNK), KEY_CHUNK), :]
        win["s", x, back] = _dot(k_chunk, q_win[:, x * GQ:(x + 1) * GQ]) + bt_ref[0, tile]
        col_max = jnp.max(win["s", x, back], axis=0, keepdims=True)
        win["m", x] = jnp.maximum(win["m", x], col_max) if ("m", x) in win else col_max

    def win_weights(x, back):
        win["p", x, back] = jnp.exp2(win["s", x, back] - win["m", x]).astype(BF16)

    def win_values(x, back):
        pv = _dot(vwT_ref[0, jnp.maximum(qbs[x] - back, 0)], win["p", x, back])
        win["acc", x] = win["acc", x] + pv if ("acc", x) in win else pv

    backs = list(range(n_back, -1, -1))
    window_work = [functools.partial(fn, x, b) for x in range(NQ)
                   for fn in (win_scores, win_weights, win_values) for b in backs]

    def cmp_bias(x):
        start = pl.multiple_of(n_cmp_pad - (Q_BLOCK // CMP_STRIDE) * (qbs[x] + 1), SUBLANES)
        return cb_ref[0, pl.ds(start, n_cmp_pad), :]
    s = _dot(kc_ref[0, 0], qT) + per_block(cmp_bias)
    e = jnp.exp2(s - jnp.maximum(jnp.max(s, axis=0, keepdims=True), 0.1 * MASK_SCORE))
    p = e * (1.0 / jnp.maximum(jnp.sum(e, axis=0, keepdims=True), 1e-30))
    o_c = _dot(vcT_ref[0, 0], p.astype(BF16))
    psum = per_block(lambda x: sum(p[:, x * GQ + g * Q_BLOCK:x * GQ + (g + 1) * Q_BLOCK] for g in range(G)))

    imp = _dot_f32_rhs(ovT_ref[...], psum)
    bj = lax.broadcasted_iota(jnp.int32, (n_sel, NQ * Q_BLOCK), 0)
    cur = jnp.right_shift(t, SEL_SHIFT)
    forced = (bj == 0) | (bj == cur) | (bj == cur - 1)
    valid = bj * SEL_BLOCK <= t
    score = jnp.where(forced, -jnp.inf, jnp.where(valid, imp, NEG_SCORE))
    bjf = bj.astype(F32)
    n_rounds = max(min(SEL_TOP, n_sel) - 3, 0)
    for rnd in range(n_rounds):
        mx = jnp.max(score, axis=0, keepdims=True)
        first = jnp.min(jnp.where(score == mx, bjf, float(n_sel)), axis=0, keepdims=True)
        score = jnp.where(bjf == first, -jnp.inf, score)
        take = -(-len(window_work) // (n_rounds - rnd))
        for piece in window_work[:take]:
            piece()
        window_work = window_work[take:]
    for piece in window_work:
        piece()
    o_w = per_block(lambda x: win["acc", x][0:HEAD_DIM] / jnp.maximum(win["acc", x][HEAD_DIM:HEAD_DIM + 1], 1e-30))
    sel_add = jnp.where(score == -jnp.inf, 0.0, MASK_SCORE)
    seladd_sc[...] = per_block(lambda x: tile_g(sel_add[:, x * Q_BLOCK:(x + 1) * Q_BLOCK]))

    stage_keys = SEL_STAGE_TILES * KEY_CHUNK
    n_stages = qbs[-1] // SEL_STAGE_TILES + 1
    last_stage = ks_ref.shape[1] // stage_keys - 1

    def scores(k, s_buf):
        kk = jnp.minimum(k, last_stage)
        blk0 = pl.multiple_of(kk // SEL_GROUP_STAGES * SEL_STEP_BLOCKS, SEL_STEP_BLOCKS)
        qaug_sc[HEAD_DIM:HEAD_DIM + SEL_STEP_BLOCKS, :] = seladd_sc[pl.ds(blk0, SEL_STEP_BLOCKS), :].astype(BF16)
        k0 = pl.multiple_of(kk * stage_keys, stage_keys)
        bias = jnp.concatenate([bias_tile(lambda x, c=c: qbs[x] - (k * SEL_STAGE_TILES + c))
                                for c in range(SEL_STAGE_TILES)], axis=0)
        s = _dot(ks_ref[0, pl.ds(k0, stage_keys), :], qaug_sc[...]) + bias
        s_buf[...] = s
        return jnp.max(s, axis=0, keepdims=True)

    def weights(m, col_max, s_buf, p_buf):
        m_new = jnp.maximum(m, col_max)
        p_buf[...] = jnp.exp2(s_buf[...] - m_new).astype(BF16)
        return m_new, jnp.exp2(m - m_new)

    def accumulate(k, alpha, p_buf):
        acc_sc[...] = alpha * acc_sc[...] + _dot(vsT_ref[0, jnp.clip(k, 0, last_stage)], p_buf[...])

    def pair(j, carry):
        m, col_max, alpha = carry
        k = 2 * j
        col_max1 = scores(k + 1, s1_sc)
        m, alpha0 = weights(m, col_max, s0_sc, p0_sc)
        accumulate(k - 1, alpha, p1_sc)
        col_max2 = scores(k + 2, s0_sc)
        m, alpha1 = weights(m, col_max1, s1_sc, p1_sc)
        accumulate(k, alpha0, p0_sc)
        return m, col_max2, alpha1

    acc_sc[...] = jnp.zeros(acc_sc.shape, F32)
    p1_sc[...] = jnp.zeros(p1_sc.shape, BF16)
    m_init = jnp.full((1, NQ * GQ), 0.1 * MASK_SCORE, F32)
    carry = (m_init, scores(0, s0_sc), jnp.ones((1, NQ * GQ), F32))
    n_pairs = (n_stages + 1) // 2
    _, _, alpha = lax.fori_loop(0, n_pairs, pair, carry)
    accumulate(2 * n_pairs - 1, alpha, p1_sc)
    o_s = acc_sc[0:HEAD_DIM, :] / jnp.maximum(acc_sc[HEAD_DIM:HEAD_DIM + 1, :], 1e-30)

    def gate_row(br):
        def one(x):
            gt = jax.nn.sigmoid(gate_ref[0, x, 0])
            return jnp.concatenate([gt[br * G + g:br * G + g + 1, :] for g in range(G)], axis=1)
        return per_block(one)
    o = gate_row(0) * o_c + gate_row(1) * o_s + gate_row(2) * o_w
    for x in range(NQ):
        o_ref[0, x * Q_BLOCK:(x + 1) * Q_BLOCK, :] = jnp.concatenate(
            [o[:, x * GQ + g * Q_BLOCK:x * GQ + (g + 1) * Q_BLOCK].T for g in range(G)], axis=1)


def _nsa_attention(q, kc, vcT, kaug, vsT, vwT, gates, overlapT, bias_tiles, cmp_bias):
    B, T, _ = kaug.shape
    Hkv, G = NSA_KV_HEADS, NSA_GROUP
    GQ = G * Q_BLOCK
    nqb = T // Q_BLOCK
    n_cmp_pad = kc.shape[2]
    n_sel = T // SEL_BLOCK
    half_keys = SEL_STAGE_TILES * KEY_CHUNK
    NQ = NSA_STEP_QBLOCKS
    assert T % (SEL_GROUP_STAGES * half_keys) == 0 and nqb % NQ == 0
    per_head = lambda b, h, i: (b, h, 0, 0)
    v_rows = V_ROWS
    chunked = lambda w: pl.BlockSpec((1, T // w, v_rows, w), lambda b, h, i: (b, 0, h, 0))
    step_lanes = NQ * GQ
    return pl.pallas_call(
        functools.partial(_nsa_kernel, n_sel=n_sel, n_cmp_pad=n_cmp_pad),
        grid=(B, Hkv, nqb // NQ),
        in_specs=[pl.BlockSpec((1, NQ, 1, HEAD_DIM, GQ), lambda b, h, i: (b, i, h, 0, 0)),
                  pl.BlockSpec((1, 1, n_cmp_pad, HEAD_DIM), per_head),
                  pl.BlockSpec((1, 1, HEAD_DIM, n_cmp_pad), per_head),
                  pl.BlockSpec((1, T, LANES), lambda b, h, i: (b, 0, h)), chunked(half_keys),
                  pl.BlockSpec((1, T, LANES), lambda b, h, i: (b, 0, Hkv + h)), chunked(KEY_CHUNK),
                  pl.BlockSpec((1, NQ, 1, GATE_ROWS, Q_BLOCK), lambda b, h, i: (b, i, h, 0, 0)),
                  pl.BlockSpec(overlapT.shape, lambda b, h, i: (0, 0)),
                  pl.BlockSpec((1, N_ALL_TILES, KEY_CHUNK, GQ), lambda b, h, i: (h, 0, 0, 0)),
                  pl.BlockSpec((1, 2 * n_cmp_pad, GQ), lambda b, h, i: (h, 0, 0))],
        out_specs=pl.BlockSpec((1, NQ * Q_BLOCK, G * HEAD_DIM), lambda b, h, i: (b, i, h)),
        out_shape=jax.ShapeDtypeStruct((B, T, NSA_WIDTH), F32),
        scratch_shapes=[pltpu.VMEM((v_rows, step_lanes), F32), pltpu.VMEM((LANES, step_lanes), BF16),
                        pltpu.VMEM((n_sel, step_lanes), F32),
                        pltpu.VMEM((half_keys, step_lanes), F32), pltpu.VMEM((half_keys, step_lanes), F32),
                        pltpu.VMEM((half_keys, step_lanes), BF16), pltpu.VMEM((half_keys, step_lanes), BF16)],
        compiler_params=_params("parallel", "parallel", "arbitrary"), name="nsa_attention",
    )(q, kc, vcT, kaug, vsT, kaug, vwT, gates, overlapT, bias_tiles, cmp_bias)


def _rwkv_chunk_kernel(rw_ref, prev_ref, mu_ref, w0_ref, wup_ref, a0_ref, aup_ref, gup_ref, kk_ref, ka_ref,
                       rk_ref, seg_ref, tri_ref, q_ref, y0_ref, a_ref, d_ref, g_ref, bonus_ref):
    C, W, N = RWKV_CHUNK, RWKV_WIDTH, HEAD_DIM
    c = pl.program_id(1)
    x = rw_ref[0]
    R = x.shape[0]
    chunk_rows = [slice(ck * C, (ck + 1) * C) for ck in range(R // C)]
    row = lax.broadcasted_iota(jnp.int32, (R, 1), 0)
    last_prev = jnp.where(c == 0, 0.0, prev_ref[0, SUBLANES - 1:SUBLANES, :])
    x_prev = jnp.where(row == 0, last_prev, pltpu.roll(x, 1, axis=0))
    xs = x + (x_prev - x) * mu_ref[...]
    r, k, v = xs[:, 0:W], xs[:, W:2 * W], xs[:, 2 * W:3 * W]
    o = 3 * W
    wd, ad, gd = xs[:, o:o + DECAY_LORA], xs[:, o + DECAY_LORA:o + DECAY_LORA + AAA_LORA], \
        xs[:, o + DECAY_LORA + AAA_LORA:]
    w_log = -jax.nn.softplus(-(w0_ref[...] + _dot(jnp.tanh(wd).astype(BF16), wup_ref[...]))) - 0.5
    lw = -jnp.exp(w_log)
    lr = jax.nn.sigmoid(a0_ref[...] + _dot(ad.astype(BF16), aup_ref[...]))
    g_ref[0] = _dot(jax.nn.sigmoid(gd).astype(BF16), gup_ref[...])
    kk = k * kk_ref[...]
    def head_sums(z):
        return jnp.concatenate([_dot_f32_lhs(z[:, t * LANES:(t + 1) * LANES], seg_ref[...])
                                for t in range(W // LANES)], axis=1)
    kk = kk * lax.rsqrt(jnp.maximum(head_sums(kk * kk), 1e-24))
    k = k * (1.0 + (lr - 1.0) * ka_ref[...])
    bonus_ref[0] = head_sums(r * k * rk_ref[...]) * v
    a_vec, b_vec = -kk, kk * lr

    L = jnp.concatenate([_dot_f32_rhs(tri_ref[...], lw[rs]) for rs in chunk_rows], axis=0)
    L_end = jnp.concatenate([jnp.broadcast_to(L[rs.stop - 1:rs.stop, :], (C, W)) for rs in chunk_rows], axis=0)
    e_neg = jnp.exp(-L)
    e_rem = jnp.exp(L_end - L)
    At, Bt, Kt, Rt = a_vec * jnp.exp(L - lw), b_vec * e_neg, k * e_neg, r * jnp.exp(L)
    Bg, Kg = b_vec * e_rem, k * e_rem
    decay_end = [jnp.exp(L[rs.stop - 1:rs.stop, :]) for rs in chunk_rows]

    HG = RWKV_GROUP_HEADS
    GL = HG * N
    assert C == N and C & (C - 1) == 0
    blk_of = lambda idx: jnp.right_shift(idx, int(math.log2(C)))
    same_head = (blk_of(lax.broadcasted_iota(jnp.int32, (HG * C, GL), 0))
                 == blk_of(lax.broadcasted_iota(jnp.int32, (HG * C, GL), 1)))
    bf = lambda z: z.astype(BF16)
    block_diag = lambda y: jnp.where(same_head, jnp.concatenate([bf(y)] * HG, axis=0), 0.0)
    ri = lax.broadcasted_iota(jnp.int32, (C, GL), 0)
    cj = jnp.bitwise_and(lax.broadcasted_iota(jnp.int32, (C, GL), 1), C - 1)
    strict, incl = ri > cj, ri >= cj
    eye_c = (ri == cj).astype(F32)
    pr = lax.broadcasted_iota(jnp.int32, (2 * N, 2 * N), 0)
    pc = lax.broadcasted_iota(jnp.int32, (2 * N, 2 * N), 1)
    pair_same_head, pair_eye = blk_of(pr) == blk_of(pc), (pr == pc).astype(F32)
    items = [(ck, gp) for ck in range(len(chunk_rows)) for gp in range(RWKV_HEADS // HG)]
    sl = [(chunk_rows[ck], slice(gp * GL, (gp + 1) * GL)) for ck, gp in items]
    ar = [bf(jnp.concatenate([At[s], Rt[s]], axis=0)) for s in sl]
    zb = [_dot_nt(a, block_diag(Bt[s])) for a, s in zip(ar, sl)]
    zk = [_dot_nt(a, block_diag(Kt[s])) for a, s in zip(ar, sl)]
    n_mat = [jnp.where(strict, z[0:C], 0.0) for z in zb]
    m_mat = [jnp.where(strict, z[0:C], 0.0) for z in zk]
    v_bd = [block_diag(v[s]) for s in sl]
    mv = [_dot(bf(m), vb) for m, vb in zip(m_mat, v_bd)]
    t_inv, n_pow = [eye_c + n for n in n_mat], n_mat
    for _ in range(int(math.log2(C)) - 1):
        n_pow = [_dot(bf(n), block_diag(n)) for n in n_pow]
        t_inv = [t + _dot(bf(t), block_diag(n)) for t, n in zip(t_inv, n_pow)]
    t_bf = [bf(t) for t in t_inv]
    ta = [_dot(t, block_diag(At[s])) for t, s in zip(t_bf, sl)]
    g0 = [_dot(t, block_diag(m)) for t, m in zip(t_bf, mv)]
    p_mat = [bf(jnp.where(incl, z[C:], 0.0)) for z in zb]
    pk_mat = [bf(jnp.where(incl, z[C:], 0.0)) for z in zk]
    q_out = [Rt[s] + _dot(p, block_diag(t)) for p, t, s in zip(p_mat, ta, sl)]
    y0_out = [_dot(p, block_diag(g)) + _dot(pk, vb) for p, g, pk, vb in zip(p_mat, g0, pk_mat, v_bd)]
    bgT = [bf(Bg[s].T) for s in sl]
    a_full = [_dot(b, bf(t)) for b, t in zip(bgT, ta)]
    d_full = [_dot(b, bf(g)) + _dot(bf(Kg[s].T), bf(v[s])) for b, g, s in zip(bgT, g0, sl)]
    for n, (ck, gp) in enumerate(items):
        rows, lanes = sl[n]
        q_ref[0, rows, lanes] = q_out[n]
        y0_ref[0, rows, lanes] = y0_out[n]
        for pp in range(HG // 2):
            blk = slice(pp * 2 * N, (pp + 1) * 2 * N)
            pair_lanes = slice(lanes.start + blk.start, lanes.start + blk.stop)
            a_ref[0, ck, gp * (HG // 2) + pp] = (jnp.where(pair_same_head, a_full[n][blk, blk], 0.0)
                                                 + pair_eye * decay_end[ck][:, pair_lanes])
            d_ref[0, ck, gp * (HG // 2) + pp] = jnp.where(pair_same_head, d_full[n][blk, blk], 0.0)


def _rwkv_chunks(rw, mu, w0, w_up, a0, a_up, g_up, k_k, k_a, r_k):
    B, T, cols = rw.shape
    C, W, H, N = RWKV_CHUNK, RWKV_WIDTH, RWKV_HEADS, HEAD_DIM
    nc = T // C
    S = RWKV_STEP_CHUNKS
    R = S * C
    assert nc % S == 0
    seg = jnp.asarray(np.kron(np.eye(LANES // N), np.ones((N, N))), BF16)
    tri = jnp.asarray(np.tril(np.ones((C, C))), BF16)
    row = lambda z: z.reshape(1, -1)
    const = lambda b, c: (0, 0)
    vec = pl.BlockSpec((1, W), const)
    tok = pl.BlockSpec((1, R, W), lambda b, c: (b, c, 0))
    mat = pl.BlockSpec((1, S, H // 2, 2 * N, 2 * N), lambda b, c: (b, c, 0, 0, 0))
    return pl.pallas_call(
        _rwkv_chunk_kernel,
        grid=(B, nc // S),
        in_specs=[pl.BlockSpec((1, R, cols), lambda b, c: (b, c, 0)),
                  pl.BlockSpec((1, SUBLANES, cols), lambda b, c: (b, jnp.maximum(c * (R // SUBLANES) - 1, 0), 0)),
                  pl.BlockSpec((1, cols), const), vec,
                  pl.BlockSpec((DECAY_LORA, W), const), vec,
                  pl.BlockSpec((AAA_LORA, W), const),
                  pl.BlockSpec((GATE_LORA, W), const), vec, vec, vec,
                  pl.BlockSpec((LANES, LANES), const), pl.BlockSpec((C, C), const)],
        out_specs=[tok, tok, mat, mat, tok, tok],
        out_shape=[jax.ShapeDtypeStruct((B, T, W), F32), jax.ShapeDtypeStruct((B, T, W), F32),
                   jax.ShapeDtypeStruct((B, nc, H // 2, 2 * N, 2 * N), F32),
                   jax.ShapeDtypeStruct((B, nc, H // 2, 2 * N, 2 * N), F32),
                   jax.ShapeDtypeStruct((B, T, W), F32), jax.ShapeDtypeStruct((B, T, W), F32)],
        compiler_params=_params("parallel", "parallel"), name="rwkv_chunks",
    )(rw, rw, row(mu), row(w0), w_up.astype(BF16), row(a0), a_up.astype(BF16), g_up.astype(BF16),
      row(k_k), row(k_a), row(r_k), seg, tri)


def _rwkv_state_kernel(a_ref, d_ref, h_ref, h_sc):
    @pl.when(pl.program_id(0) == 0)
    def _():
        h_sc[...] = jnp.zeros(h_sc.shape, F32)

    items = [(b, pair) for b in range(a_ref.shape[0]) for pair in range(a_ref.shape[2])]
    states = [h_sc[b, pair] for b, pair in items]
    for ck in range(a_ref.shape[1]):
        for (b, pair), st in zip(items, states):
            h_ref[b, ck, pair] = st
        states = [_dot_hi_lo(a_ref[b, ck, pair], st) + d_ref[b, ck, pair]
                  for (b, pair), st in zip(items, states)]
    for (b, pair), st in zip(items, states):
        h_sc[b, pair] = st


def _rwkv_readout_kernel(h_ref, q_ref, y0_ref, g_ref, bonus_ref, lw_ref, lb_ref, seg_ref, o_ref):
    C = RWKV_CHUNK
    inv_n = 1.0 / HEAD_DIM
    items = [(b, ck, pair, slice(ck * C, (ck + 1) * C), slice(pair * LANES, (pair + 1) * LANES))
             for b in range(h_ref.shape[0]) for ck in range(h_ref.shape[1]) for pair in range(h_ref.shape[2])]
    ys = [_dot(q_ref[b, rows, lanes].astype(BF16), h_ref[b, ck, pair].astype(BF16)) + y0_ref[b, rows, lanes]
          for b, ck, pair, rows, lanes in items]
    means = [_dot_f32_lhs(y, seg_ref[...], 2) * inv_n for y in ys]
    cen = [y - mean for y, mean in zip(ys, means)]
    var = [_dot_f32_lhs(jnp.square(c), seg_ref[...], 2) * inv_n for c in cen]
    for (b, _, _, rows, lanes), c, v in zip(items, cen, var):
        yn = c * lax.rsqrt(v + LNX_EPS) * lw_ref[:, lanes] + lb_ref[:, lanes]
        o_ref[b, rows, lanes] = (yn + bonus_ref[b, rows, lanes]) * g_ref[b, rows, lanes]


def _rwkv_scan(A, D, Q, Y0, g, bonus, lnx_w, lnx_b):
    B, nc, P, N2, _ = A.shape
    T, W, C = Q.shape[1], Q.shape[2], RWKV_CHUNK
    S, SR = RWKV_SCAN_STEP_CHUNKS, RWKV_READOUT_STEP_CHUNKS
    assert nc % S == 0 and nc % SR == 0
    mat = lambda s: pl.BlockSpec((B, s, P, N2, N2), lambda c: (0, c, 0, 0, 0))
    h_start = pl.pallas_call(
        _rwkv_state_kernel,
        grid=(nc // S,),
        in_specs=[mat(S), mat(S)],
        out_specs=mat(S),
        out_shape=jax.ShapeDtypeStruct((B, nc, P, N2, N2), F32),
        scratch_shapes=[pltpu.VMEM((B, P, N2, N2), F32)],
        compiler_params=_params("arbitrary"), name="rwkv_state",
    )(A, D)
    tok = pl.BlockSpec((B, SR * C, W), lambda c: (0, c, 0))
    vec = pl.BlockSpec((1, W), lambda c: (0, 0))
    seg = jnp.asarray(np.kron(np.eye(2), np.ones((HEAD_DIM, HEAD_DIM))), BF16)
    return pl.pallas_call(
        _rwkv_readout_kernel,
        grid=(nc // SR,),
        in_specs=[mat(SR), tok, tok, tok, tok, vec, vec, pl.BlockSpec((N2, N2), lambda c: (0, 0))],
        out_specs=tok,
        out_shape=jax.ShapeDtypeStruct((B, T, W), F32),
        compiler_params=_params("parallel"), name="rwkv_readout",
    )(h_start, Q, Y0, g, bonus, lnx_w.reshape(1, W), lnx_b.reshape(1, W), seg)


def _mix_xattn_kernel(x_ref, on_ref, or_ref, wo1_ref, wo2_ref, gx_ref, wq_ref, k_ref, v_ref, wo_ref, o_ref):
    x1 = x_ref[0] + _dot(on_ref[0].astype(BF16), wo1_ref[...]) + _dot(or_ref[0].astype(BF16), wo2_ref[...])
    q = _dot(_rms(x1, gx_ref[...]).astype(BF16), wq_ref[...])
    dh = q.shape[-1] // XATTN_HEADS
    qbf = (q * (dh ** -0.5)).astype(BF16)
    outs = []
    for h in range(XATTN_HEADS):
        hs = slice(h * dh, (h + 1) * dh)
        s = _dot_nt(qbf[:, hs], k_ref[0, :, hs])
        e = jnp.exp(s - jnp.max(s, axis=-1, keepdims=True))
        p = e / jnp.sum(e, axis=-1, keepdims=True)
        outs.append(_dot(p.astype(BF16), v_ref[0, :, hs]))
    o = jnp.concatenate(outs, axis=1).astype(BF16)
    o_ref[0] = x1 + _dot(o, wo_ref[...])


def _mix_xattn(x, o_nsa, o_rwkv, w_out, norm_x_g, w_q, mem_k, mem_v, w_o):
    B, T, D = x.shape
    M = mem_k.shape[1]
    tm = min(ROW_TILE, T)
    wo1, wo2 = w_out[:NSA_WIDTH].astype(BF16), w_out[NSA_WIDTH:].astype(BF16)
    const = lambda b, i: (0, 0)
    tile = lambda w: pl.BlockSpec((1, tm, w), lambda b, i: (b, i, 0))
    return pl.pallas_call(
        _mix_xattn_kernel,
        grid=(B, T // tm),
        in_specs=[tile(D), tile(NSA_WIDTH), tile(RWKV_WIDTH),
                  pl.BlockSpec(wo1.shape, const), pl.BlockSpec(wo2.shape, const),
                  pl.BlockSpec((1, D), const), pl.BlockSpec((D, D), const),
                  pl.BlockSpec((1, M, D), lambda b, i: (b, 0, 0)), pl.BlockSpec((1, M, D), lambda b, i: (b, 0, 0)),
                  pl.BlockSpec((D, D), const)],
        out_specs=tile(D),
        out_shape=jax.ShapeDtypeStruct((B, T, D), F32),
        compiler_params=_params("parallel", "parallel"), name="mix_xattn",
    )(x, o_nsa, o_rwkv, wo1, wo2, norm_x_g.reshape(1, D), w_q.astype(BF16), mem_k, mem_v, w_o.astype(BF16))


def _ffn_kernel(x_ref, g_ref, wg_ref, wu_ref, wd_ref, gf_ref, o_ref, *, final_norm):
    x = x_ref[...]
    h = _rms(x, g_ref[...]).astype(BF16)
    act = (jax.nn.silu(_dot(h, wg_ref[...])) * _dot(h, wu_ref[...])).astype(BF16)
    y = x + _dot(act, wd_ref[...])
    o_ref[...] = _rms(y, gf_ref[...]) if final_norm else y


def _ffn(x, norm_g, w_gate, w_up, w_down, final_g, final_norm):
    R, D = x.shape
    F = w_gate.shape[1]
    tm = min(FFN_ROW_TILE, R)
    const = lambda i: (0, 0)
    resident = lambda shape: pl.BlockSpec(shape, const, pipeline_mode=pl.Buffered(1))
    return pl.pallas_call(
        functools.partial(_ffn_kernel, final_norm=final_norm),
        grid=(R // tm,),
        in_specs=[pl.BlockSpec((tm, D), lambda i: (i, 0)), pl.BlockSpec((1, D), const),
                  resident((D, F)), resident((D, F)), resident((F, D)), pl.BlockSpec((1, D), const)],
        out_specs=pl.BlockSpec((tm, D), lambda i: (i, 0)),
        out_shape=jax.ShapeDtypeStruct((R, D), F32),
        compiler_params=_params("parallel"), name="ffn",
    )(x, norm_g.reshape(1, D), w_gate.astype(BF16), w_up.astype(BF16), w_down.astype(BF16),
      final_g.reshape(1, D))


def _overlap_matrix(n_cmp_pad, n_sel):
    c = np.arange(n_cmp_pad)[:, None] * CMP_STRIDE
    s = np.arange(n_sel)[None, :] * SEL_BLOCK
    return ((c <= s + SEL_BLOCK - 1) & (c + CMP_LEN - 1 >= s)).astype(np.float32)


def _layer(x, mem, rel_bias, final_g, is_last, norm_mix_g, w_in, nsa_gate_b, cmp_pe_k, cmp_pe_v,
           cmp_k_w1, cmp_k_b1, cmp_k_w2, cmp_v_w1, cmp_v_b1, cmp_v_w2,
           rwkv_mu, rwkv_w0, rwkv_w_up, rwkv_a0, rwkv_a_up, rwkv_g_up,
           rwkv_k_k, rwkv_k_a, rwkv_r_k, rwkv_lnx_w, rwkv_lnx_b, w_out,
           norm_x_g, norm_mem_g, w_q_x, w_kv_x, w_o_x, norm_ffn_g, w_gate, w_up, w_down):
    B, T, D = x.shape
    Hkv, G, dh = NSA_KV_HEADS, NSA_GROUP, HEAD_DIM
    q, kvc, kaug, vsT, vwT, gates, rw = _proj_in(x, norm_mix_g, w_in, nsa_gate_b)

    n16 = T // CMP_STRIDE
    kvc = kvc.reshape(B, T, 2 * KV_WIDTH)
    kc = _compress(kvc, 0, cmp_pe_k, cmp_k_w1, cmp_k_b1, cmp_k_w2, False)
    vcT = _compress(kvc, 1, cmp_pe_v, cmp_v_w1, cmp_v_b1, cmp_v_w2, True)
    bias_tiles, cmp_bias = _bias_tiles(rel_bias, n16)
    o_nsa = _nsa_attention(q, kc, vcT, kaug.reshape(B, T, 2 * Hkv * LANES), vsT, vwT,
                           gates, jnp.asarray(_overlap_matrix(n16, T // SEL_BLOCK).T, BF16), bias_tiles, cmp_bias)

    Q, Y0, A, Dm, g, bonus = _rwkv_chunks(rw.reshape(B, T, RWKV_COLS), rwkv_mu, rwkv_w0, rwkv_w_up, rwkv_a0,
                                          rwkv_a_up, rwkv_g_up, rwkv_k_k, rwkv_k_a, rwkv_r_k.reshape(-1))
    o_rwkv = _rwkv_scan(A, Dm, Q, Y0, g, bonus, rwkv_lnx_w, rwkv_lnx_b)

    M = mem.shape[1]
    (kv_mem,) = _norm_matmul(mem.reshape(B * M, D), norm_mem_g, [w_kv_x.astype(BF16)], [None], [BF16], ROW_TILE)
    kv_mem = kv_mem.reshape(B, M, 2 * D)
    x = _mix_xattn(x, o_nsa, o_rwkv, w_out, norm_x_g, w_q_x, kv_mem[..., :D], kv_mem[..., D:], w_o_x)
    x = _ffn(x.reshape(B * T, D), norm_ffn_g, w_gate, w_up, w_down, final_g, is_last)
    return x.reshape(B, T, D)


def kernel(x, mem, rel_bias, norm_f_g, norm_mix_g, w_in, nsa_gate_b, cmp_pe_k, cmp_pe_v, cmp_k_w1, cmp_k_b1, cmp_k_w2, cmp_v_w1, cmp_v_b1, cmp_v_w2, rwkv_mu, rwkv_w0, rwkv_w_up, rwkv_a0, rwkv_a_up, rwkv_g_up, rwkv_k_k, rwkv_k_a, rwkv_r_k, rwkv_lnx_w, rwkv_lnx_b, w_out, norm_x_g, norm_mem_g, w_q_x, w_kv_x, w_o_x, norm_ffn_g, w_gate, w_up, w_down):
    stacked = (norm_mix_g, w_in, nsa_gate_b, cmp_pe_k, cmp_pe_v, cmp_k_w1, cmp_k_b1, cmp_k_w2, cmp_v_w1,
               cmp_v_b1, cmp_v_w2, rwkv_mu, rwkv_w0, rwkv_w_up, rwkv_a0, rwkv_a_up, rwkv_g_up, rwkv_k_k,
               rwkv_k_a, rwkv_r_k, rwkv_lnx_w, rwkv_lnx_b, w_out, norm_x_g, norm_mem_g, w_q_x, w_kv_x, w_o_x,
               norm_ffn_g, w_gate, w_up, w_down)
    depth = w_in.shape[0]
    for l in range(depth):
        x = _layer(x, mem, rel_bias, norm_f_g, l == depth - 1, *[p[l] for p in stacked])
    return x
```

```python
import functools
import math

import numpy as np
import jax
import jax.numpy as jnp
from jax import lax
from jax.experimental import pallas as pl
from jax.experimental.pallas import tpu as pltpu

F32 = jnp.float32
BF16 = jnp.bfloat16

LANES = 128
SUBLANES = 8
BF16_ROWS = 16
VMEM_LIMIT_BYTES = 56 * 1024 * 1024

HEAD_DIM = 64
NSA_HEADS = 8
NSA_KV_HEADS = 2
NSA_GROUP = NSA_HEADS // NSA_KV_HEADS
NSA_WIDTH = NSA_HEADS * HEAD_DIM
KV_WIDTH = NSA_KV_HEADS * HEAD_DIM
RWKV_HEADS = 8
RWKV_WIDTH = RWKV_HEADS * HEAD_DIM
CMP_LEN = 32
CMP_STRIDE = 16
SEL_BLOCK = 64
SEL_SHIFT = 6
SEL_TOP = 16
WINDOW = 512
Q_BLOCK = 128
DECAY_LORA = 64
AAA_LORA = 64
GATE_LORA = 128
N_BUCKETS = 32
MAX_DISTANCE = 2048
XATTN_HEADS = 4
RMS_EPS = 1e-6
LNX_EPS = 64e-5
FORCE_SCORE = 1e4
NEG_SCORE = -1e9
MASK_SCORE = -1e30
LOG2E = math.log2(math.e)
RWKV_COLS = 3 * RWKV_WIDTH + DECAY_LORA + AAA_LORA + GATE_LORA
NSA_COLS = NSA_WIDTH + 6 * KV_WIDTH + 3 * NSA_HEADS

KEY_CHUNK = 128
RWKV_CHUNK = 64
RWKV_SCAN_STEP_CHUNKS = 8
RWKV_READOUT_STEP_CHUNKS = 4
RWKV_GROUP_HEADS = 2
RWKV_STEP_CHUNKS = 8
ROW_TILE = 512
FFN_ROW_TILE = 512


def _t5_thresholds():
    d = np.arange(0, 2 * MAX_DISTANCE, dtype=np.int64)
    max_exact = N_BUCKETS // 2
    nf = np.maximum(d, 1).astype(np.float32)
    large = max_exact + (np.log(nf / np.float32(max_exact)) / np.float32(math.log(MAX_DISTANCE / max_exact))
                         * np.float32(N_BUCKETS - max_exact)).astype(np.int32)
    bucket = np.where(d < max_exact, d, np.minimum(large, N_BUCKETS - 1))
    return [int(np.argmax(bucket >= k)) for k in range(N_BUCKETS)]


T5_THRESHOLDS = _t5_thresholds()
N_BIAS_TILES = -(-(T5_THRESHOLDS[-1] + KEY_CHUNK) // KEY_CHUNK) + 1
TILE_MASKED = N_BIAS_TILES
TILE_WINDOW_EDGE = N_BIAS_TILES + 1
N_ALL_TILES = N_BIAS_TILES + 2
SEL_STEP_BLOCKS = 16
SEL_STAGE_TILES = 4
SEL_GROUP_STAGES = SEL_STEP_BLOCKS * SEL_BLOCK // (SEL_STAGE_TILES * KEY_CHUNK)
NSA_STEP_QBLOCKS = 4
V_ROWS = HEAD_DIM + BF16_ROWS
GATE_ROWS = 16


def _params(*semantics):
    return pltpu.CompilerParams(dimension_semantics=semantics, vmem_limit_bytes=VMEM_LIMIT_BYTES)


def _rms(x, g):
    return x * lax.rsqrt(jnp.mean(x * x, axis=-1, keepdims=True) + RMS_EPS) * g


def _dot(a, b, **kw):
    return jnp.dot(a, b, preferred_element_type=F32, **kw)


def _split3(x):
    hi = x.astype(BF16)
    r1 = x - hi.astype(F32)
    mid = r1.astype(BF16)
    lo = (r1 - mid.astype(F32)).astype(BF16)
    return hi, mid, lo


def _dot_f32_lhs(x, w01, pieces=3):
    w = w01.astype(BF16)
    hi, mid, lo = _split3(x)
    return _dot(hi, w) + (_dot(mid, w) + _dot(lo, w) if pieces == 3 else _dot(mid, w))


def _dot_f32_rhs(w01, x):
    w = w01.astype(BF16)
    hi, mid, lo = _split3(x)
    return _dot(w, hi) + (_dot(w, mid) + _dot(w, lo))


def _dot_hi_lo(a, b):
    a_hi = a.astype(BF16)
    a_lo = (a - a_hi.astype(F32)).astype(BF16)
    b_hi = b.astype(BF16)
    b_lo = (b - b_hi.astype(F32)).astype(BF16)
    return _dot(a_hi, b_hi) + (_dot(a_hi, b_lo) + _dot(a_lo, b_hi))


def _dot_nt(a, b, **kw):
    return lax.dot_general(a, b, (((1,), (1,)), ((), ())), preferred_element_type=F32, **kw)


def _norm_matmul_kernel(x_ref, g_ref, *refs, nseg, bias_flags):
    nb = sum(bias_flags)
    w_refs, b_refs, o_refs = refs[:nseg], refs[nseg:nseg + nb], refs[nseg + nb:]
    xn = _rms(x_ref[...], g_ref[...]).astype(BF16)
    bi = 0
    for s in range(nseg):
        y = _dot(xn, w_refs[s][...])
        if bias_flags[s]:
            y = y + b_refs[bi][...]
            bi += 1
        o_refs[s][...] = y.astype(o_refs[s].dtype)


def _norm_matmul(x, g, weights, biases, out_dtypes, row_tile):
    R, D = x.shape
    tm = min(row_tile, R)
    assert R % tm == 0
    nseg = len(weights)
    bias_flags = tuple(b is not None for b in biases)
    const = lambda i: (0, 0)
    in_specs = [pl.BlockSpec((tm, D), lambda i: (i, 0)), pl.BlockSpec((1, D), const)]
    in_specs += [pl.BlockSpec(w.shape, const) for w in weights]
    in_specs += [pl.BlockSpec((1, b.shape[-1]), const) for b in biases if b is not None]
    out_specs = [pl.BlockSpec((tm, w.shape[1]), lambda i: (i, 0)) for w in weights]
    out_shape = [jax.ShapeDtypeStruct((R, w.shape[1]), dt) for w, dt in zip(weights, out_dtypes)]
    return pl.pallas_call(
        functools.partial(_norm_matmul_kernel, nseg=nseg, bias_flags=bias_flags),
        grid=(R // tm,), in_specs=in_specs, out_specs=out_specs, out_shape=out_shape,
        compiler_params=_params("parallel"), name="norm_matmul",
    )(x, g.reshape(1, D), *weights, *[b.reshape(1, -1) for b in biases if b is not None])


def _proj_in_kernel(x_ref, g_ref, wq_ref, wc_ref, wk_ref, wvT_ref, wgT_ref, bg_ref, wr_ref,
                    q_ref, kvc_ref, kaug_ref, vsT_ref, vwT_ref, gate_ref, rw_ref, *, seq_len):
    tm = x_ref.shape[0]
    xn = _rms(x_ref[...], g_ref[...]).astype(BF16)
    qT = (_dot_nt(wq_ref[...], xn) * (HEAD_DIM ** -0.5 * LOG2E)).astype(BF16)
    for j in range(tm // Q_BLOCK):
        for hg in range(NSA_HEADS):
            h, g = divmod(hg, NSA_GROUP)
            q_ref[0, j, h, :, g * Q_BLOCK:(g + 1) * Q_BLOCK] = qT[hg * HEAD_DIM:(hg + 1) * HEAD_DIM,
                                                                  j * Q_BLOCK:(j + 1) * Q_BLOCK]
    kvc_ref[...] = _dot(xn, wc_ref[...])
    rw_ref[...] = _dot(xn, wr_ref[...])
    k_all = _dot(xn, wk_ref[...])
    tok = lax.rem(pl.program_id(0) * tm, seq_len) + lax.broadcasted_iota(jnp.int32, k_all.shape, 0)
    lane = lax.broadcasted_iota(jnp.int32, k_all.shape, 1)
    blk = jnp.bitwise_and(jnp.right_shift(tok, SEL_SHIFT), SEL_STEP_BLOCKS - 1)
    hot = (jnp.bitwise_and(lane, LANES - 1) == HEAD_DIM + blk) & (lane < NSA_KV_HEADS * LANES)
    kaug_ref[...] = jnp.where(hot, 1.0, k_all).astype(BF16)
    vT = _dot_nt(wvT_ref[...], xn)
    row = lax.broadcasted_iota(jnp.int32, vT.shape, 0)
    ones_row = row == HEAD_DIM
    for grp in range(1, 2 * NSA_KV_HEADS):
        ones_row = ones_row | (row == grp * V_ROWS + HEAD_DIM)
    vT = jnp.where(ones_row, 1.0, vT).astype(BF16)
    half = NSA_KV_HEADS * V_ROWS
    stage_keys = vsT_ref.shape[3]
    for c in range(tm // stage_keys):
        vsT_ref[0, c] = vT[0:half, c * stage_keys:(c + 1) * stage_keys]
    for c in range(tm // KEY_CHUNK):
        vwT_ref[0, c] = vT[half:, c * KEY_CHUNK:(c + 1) * KEY_CHUNK]
    gT = _dot_nt(wgT_ref[...], xn) + bg_ref[...]
    for j in range(tm // Q_BLOCK):
        for h in range(NSA_KV_HEADS):
            gate_ref[0, j, h] = gT[h * GATE_ROWS:(h + 1) * GATE_ROWS, j * Q_BLOCK:(j + 1) * Q_BLOCK]


def _proj_in(x, norm_g, w_in, gate_b):
    B, T, D = x.shape
    Hkv, G, dh = NSA_KV_HEADS, NSA_GROUP, HEAD_DIM
    tm = ROW_TILE
    stage_keys = SEL_STAGE_TILES * KEY_CHUNK
    assert T % tm == 0 and tm % Q_BLOCK == 0 and tm % stage_keys == 0
    kv0 = NSA_WIDTH
    g0 = kv0 + 6 * KV_WIDTH
    stream = lambda s: w_in[:, kv0 + s * KV_WIDTH:kv0 + (s + 1) * KV_WIDTH].reshape(D, Hkv, dh)
    pad_cols = lambda w: jnp.pad(w, ((0, 0), (0, 0), (0, LANES - dh))).reshape(D, Hkv * LANES)
    pad_rows = lambda w: jnp.pad(w.transpose(1, 2, 0), ((0, 0), (0, V_ROWS - dh), (0, 0))).reshape(Hkv * V_ROWS, D)
    w_k = jnp.concatenate([pad_cols(stream(2)), pad_cols(stream(4))], axis=1)
    w_vT = jnp.concatenate([pad_rows(stream(3)), pad_rows(stream(5))], axis=0)
    reorder = lambda a: a.reshape(-1, Hkv, G, 3).transpose(1, 3, 2, 0).reshape(Hkv, 3 * G, -1)
    pad_gate = lambda a: jnp.pad(a, ((0, 0), (0, GATE_ROWS - 3 * G), (0, 0))).reshape(Hkv * GATE_ROWS, -1)
    w_gT = pad_gate(reorder(w_in[:, g0:NSA_COLS]))
    b_g = pad_gate(reorder(gate_b.reshape(1, -1)))
    weights = [w_in[:, :kv0].T, w_in[:, kv0:kv0 + 2 * KV_WIDTH], w_k, w_vT, w_gT]
    weights = [w.astype(BF16) for w in weights] + [b_g, w_in[:, NSA_COLS:].astype(BF16)]
    nt = T // tm
    rows = lambda n: pl.BlockSpec((tm, n), lambda i: (i, 0))
    const = lambda i: (0, 0)
    return pl.pallas_call(
        functools.partial(_proj_in_kernel, seq_len=T),
        grid=(B * nt,),
        in_specs=[rows(D), pl.BlockSpec((1, D), const)] + [pl.BlockSpec(w.shape, const) for w in weights],
        out_specs=[pl.BlockSpec((1, tm // Q_BLOCK, Hkv, dh, G * Q_BLOCK), lambda i: (i // nt, i % nt, 0, 0, 0)),
                   rows(2 * KV_WIDTH), rows(2 * Hkv * LANES),
                   pl.BlockSpec((1, tm // stage_keys, Hkv * V_ROWS, stage_keys), lambda i: (i // nt, i % nt, 0, 0)),
                   pl.BlockSpec((1, tm // KEY_CHUNK, Hkv * V_ROWS, KEY_CHUNK), lambda i: (i // nt, i % nt, 0, 0)),
                   pl.BlockSpec((1, tm // Q_BLOCK, Hkv, GATE_ROWS, Q_BLOCK), lambda i: (i // nt, i % nt, 0, 0, 0)),
                   rows(RWKV_COLS)],
        out_shape=[jax.ShapeDtypeStruct((B, T // Q_BLOCK, Hkv, dh, G * Q_BLOCK), BF16),
                   jax.ShapeDtypeStruct((B * T, 2 * KV_WIDTH), F32),
                   jax.ShapeDtypeStruct((B * T, 2 * Hkv * LANES), BF16),
                   jax.ShapeDtypeStruct((B, T // stage_keys, Hkv * V_ROWS, stage_keys), BF16),
                   jax.ShapeDtypeStruct((B, T // KEY_CHUNK, Hkv * V_ROWS, KEY_CHUNK), BF16),
                   jax.ShapeDtypeStruct((B, T // Q_BLOCK, Hkv, GATE_ROWS, Q_BLOCK), F32),
                   jax.ShapeDtypeStruct((B * T, RWKV_COLS), F32)],
        compiler_params=_params("parallel"), name="proj_in",
    )(x.reshape(B * T, D), norm_g.reshape(1, D), *weights)


def _compress_kernel(x_ref, pe_ref, w1_ref, b1_ref, w2_ref, o_ref, *, transpose_out):
    n16 = x_ref.shape[1] // CMP_STRIDE
    hidden = w1_ref.shape[2] // NSA_KV_HEADS
    lo = jnp.zeros((n16, w1_ref.shape[2]), F32)
    hi = jnp.zeros((n16, w1_ref.shape[2]), F32)
    for l in range(CMP_STRIDE):
        rows = x_ref[0, pl.ds(l, n16, stride=CMP_STRIDE), :]
        lo = lo + _dot((rows + pe_ref[l:l + 1, :]).astype(BF16), w1_ref[l])
        hi = hi + _dot((rows + pe_ref[CMP_STRIDE + l:CMP_STRIDE + l + 1, :]).astype(BF16), w1_ref[CMP_STRIDE + l])
    h = lo + pltpu.roll(hi, n16 - 1, axis=0) + b1_ref[...]
    h = jax.nn.gelu(h).astype(BF16)
    for hkv in range(NSA_KV_HEADS):
        hh = h[:, hkv * hidden:(hkv + 1) * hidden]
        if transpose_out:
            o_ref[0, hkv] = _dot_nt(w2_ref[...], hh).astype(o_ref.dtype)
        else:
            o_ref[0, hkv] = _dot(hh, w2_ref[...]).astype(o_ref.dtype)


def _compress(kvc, stream, pe, w1, b1, w2, transpose_out):
    B, T, _ = kvc.shape
    H, dh = NSA_KV_HEADS, HEAD_DIM
    n16 = T // CMP_STRIDE
    hidden = w1.shape[1]
    eye = jnp.eye(H, dtype=w1.dtype)
    w1_bd = jnp.einsum('ldn,hg->lhdgn', w1.reshape(CMP_LEN, dh, hidden), eye).reshape(CMP_LEN, H * dh, H * hidden)
    w2b = (w2.T if transpose_out else w2).astype(BF16)
    oshape = (B, H, dh, n16) if transpose_out else (B, H, n16, dh)
    return pl.pallas_call(
        functools.partial(_compress_kernel, transpose_out=transpose_out),
        grid=(B,),
        in_specs=[pl.BlockSpec((1, T, H * dh), lambda b: (b, 0, stream)),
                  pl.BlockSpec((CMP_LEN, H * dh), lambda b: (0, 0)),
                  pl.BlockSpec(w1_bd.shape, lambda b: (0, 0, 0)),
                  pl.BlockSpec((1, H * hidden), lambda b: (0, 0)),
                  pl.BlockSpec(w2b.shape, lambda b: (0, 0))],
        out_specs=pl.BlockSpec((1,) + oshape[1:], lambda b: (b, 0, 0, 0)),
        out_shape=jax.ShapeDtypeStruct(oshape, BF16),
        compiler_params=_params("parallel"), name="nsa_compress",
    )(kvc, jnp.tile(pe, (1, H)), w1_bd.astype(BF16), jnp.tile(b1.reshape(1, hidden), (1, H)), w2b)


def _bias_of_distance(tab_ref, h, d):
    val = jnp.full(d.shape, tab_ref[h, 0], F32)
    for k in range(1, N_BUCKETS):
        val = jnp.where(d >= T5_THRESHOLDS[k], tab_ref[h, k], val)
    return val * LOG2E


def _bias_tiles_kernel(tab_ref, bt_ref, cb_ref, *, n_cmp_pad):
    hkv = pl.program_id(0)
    j = lax.broadcasted_iota(jnp.int32, (KEY_CHUNK, Q_BLOCK), 0)
    i = lax.broadcasted_iota(jnp.int32, (KEY_CHUNK, Q_BLOCK), 1)
    r2 = lax.broadcasted_iota(jnp.int32, (2 * n_cmp_pad, Q_BLOCK), 0)
    i2 = lax.broadcasted_iota(jnp.int32, (2 * n_cmp_pad, Q_BLOCK), 1)
    l2 = r2 - (n_cmp_pad - KEY_CHUNK)
    d2 = i2 - CMP_STRIDE * l2 + (CMP_STRIDE * KEY_CHUNK - Q_BLOCK - (CMP_LEN - 1))
    hidden2 = (l2 >= KEY_CHUNK) | ((l2 >= 0) & (d2 < 0))
    d2 = jnp.where((l2 >= 0) & (l2 < KEY_CHUNK), d2, 2 * MAX_DISTANCE)
    for g in range(NSA_GROUP):
        h = hkv * NSA_GROUP + g
        lanes = slice(g * Q_BLOCK, (g + 1) * Q_BLOCK)
        for m in range(N_BIAS_TILES):
            tile = _bias_of_distance(tab_ref, h, m * KEY_CHUNK + i - j)
            if m == 0:
                tile = jnp.where(j <= i, tile, MASK_SCORE)
            bt_ref[0, m, :, lanes] = tile
        bt_ref[0, TILE_MASKED, :, lanes] = jnp.full((KEY_CHUNK, Q_BLOCK), MASK_SCORE, F32)
        edge = _bias_of_distance(tab_ref, h, WINDOW + i - j)
        bt_ref[0, TILE_WINDOW_EDGE, :, lanes] = jnp.where(j > i, edge, MASK_SCORE)
        cb_ref[0, :, lanes] = jnp.where(hidden2, MASK_SCORE, _bias_of_distance(tab_ref, h, d2))


def _bias_tiles(rel_bias, n_cmp_pad):
    assert CMP_STRIDE * KEY_CHUNK - Q_BLOCK - (CMP_LEN - 1) >= T5_THRESHOLDS[-1]
    GQ = NSA_GROUP * Q_BLOCK
    return pl.pallas_call(
        functools.partial(_bias_tiles_kernel, n_cmp_pad=n_cmp_pad),
        grid=(NSA_KV_HEADS,),
        in_specs=[pl.BlockSpec(memory_space=pltpu.SMEM)],
        out_specs=[pl.BlockSpec((1, N_ALL_TILES, KEY_CHUNK, GQ), lambda h: (h, 0, 0, 0)),
                   pl.BlockSpec((1, 2 * n_cmp_pad, GQ), lambda h: (h, 0, 0))],
        out_shape=[jax.ShapeDtypeStruct((NSA_KV_HEADS, N_ALL_TILES, KEY_CHUNK, GQ), F32),
                   jax.ShapeDtypeStruct((NSA_KV_HEADS, 2 * n_cmp_pad, GQ), F32)],
        compiler_params=_params("parallel"), name="t5_bias_tiles",
    )(rel_bias.T)


def _nsa_kernel(q_ref, kc_ref, vcT_ref, ks_ref, vsT_ref, kw_ref, vwT_ref, gate_ref, ovT_ref, bt_ref, cb_ref,
                o_ref, acc_sc, qaug_sc, seladd_sc, s0_sc, s1_sc, p0_sc, p1_sc, *, n_sel, n_cmp_pad):
    G, NQ = NSA_GROUP, NSA_STEP_QBLOCKS
    GQ = G * Q_BLOCK
    qbs = [pl.program_id(2) * NQ + x for x in range(NQ)]
    per_block = lambda fn: jnp.concatenate([fn(x) for x in range(NQ)], axis=1)
    tile_g = lambda a: jnp.concatenate([a] * G, axis=1)
    qT = per_block(lambda x: q_ref[0, x, 0])
    qaug_sc[0:HEAD_DIM, :] = qT
    qaug_sc[HEAD_DIM:, :] = jnp.zeros((qaug_sc.shape[0] - HEAD_DIM, NQ * GQ), BF16)
    lane_q = lax.broadcasted_iota(jnp.int32, (1, Q_BLOCK), 1)
    t = per_block(lambda x: qbs[x] * Q_BLOCK + lane_q)

    def bias_tile(dist_of_block):
        def one(x):
            dist = dist_of_block(x)
            return bt_ref[0, jnp.where(dist < 0, TILE_MASKED, jnp.minimum(dist, N_BIAS_TILES - 1))]
        return per_block(one)

    n_back = WINDOW // KEY_CHUNK
    q_win = qaug_sc[...]
    win = {}

    def win_scores(x, back):
        kc = jnp.maximum(qbs[x] - back, 0)
        edge = TILE_WINDOW_EDGE if back == n_back else back
        tile = jnp.where(qbs[x] >= back, edge, TILE_MASKED)
        k_chunk = kw_ref[0, pl.ds(pl.multiple_of(kc * KEY_CHUNK, KEY_CHUNK), KEY_CHUNK), :]
        win["s", x, back] = _dot(k_chunk, q_win[:, x * GQ:(x + 1) * GQ]) + bt_ref[0, tile]
        col_max = jnp.max(win["s", x, back], axis=0, keepdims=True)
        win["m", x] = jnp.maximum(win["m", x], col_max) if ("m", x) in win else col_max

    def win_weights(x, back):
        win["p", x, back] = jnp.exp2(win["s", x, back] - win["m", x]).astype(BF16)

    def win_values(x, back):
        pv = _dot(vwT_ref[0, jnp.maximum(qbs[x] - back, 0)], win["p", x, back])
        win["acc", x] = win["acc", x] + pv if ("acc", x) in win else pv

    backs = list(range(n_back, -1, -1))
    window_work = [functools.partial(fn, x, b) for x in range(NQ)
                   for fn in (win_scores, win_weights, win_values) for b in backs]

    def cmp_bias(x):
        start = pl.multiple_of(n_cmp_pad - (Q_BLOCK // CMP_STRIDE) * (qbs[x] + 1), SUBLANES)
        return cb_ref[0, pl.ds(start, n_cmp_pad), :]
    s = _dot(kc_ref[0, 0], qT) + per_block(cmp_bias)
    e = jnp.exp2(s - jnp.maximum(jnp.max(s, axis=0, keepdims=True), 0.1 * MASK_SCORE))
    p = e * (1.0 / jnp.maximum(jnp.sum(e, axis=0, keepdims=True), 1e-30))
    o_c = _dot(vcT_ref[0, 0], p.astype(BF16))
    psum = per_block(lambda x: sum(p[:, x * GQ + g * Q_BLOCK:x * GQ + (g + 1) * Q_BLOCK] for g in range(G)))

    imp = _dot_f32_rhs(ovT_ref[...], psum)
    bj = lax.broadcasted_iota(jnp.int32, (n_sel, NQ * Q_BLOCK), 0)
    cur = jnp.right_shift(t, SEL_SHIFT)
    forced = (bj == 0) | (bj == cur) | (bj == cur - 1)
    valid = bj * SEL_BLOCK <= t
    score = jnp.where(forced, -jnp.inf, jnp.where(valid, imp, NEG_SCORE))
    bjf = bj.astype(F32)
    n_rounds = max(min(SEL_TOP, n_sel) - 3, 0)
    for rnd in range(n_rounds):
        mx = jnp.max(score, axis=0, keepdims=True)
        first = jnp.min(jnp.where(score == mx, bjf, float(n_sel)), axis=0, keepdims=True)
        score = jnp.where(bjf == first, -jnp.inf, score)
        take = -(-len(window_work) // (n_rounds - rnd))
        for piece in window_work[:take]:
            piece()
        window_work = window_work[take:]
    for piece in window_work:
        piece()
    o_w = per_block(lambda x: win["acc", x][0:HEAD_DIM] / jnp.maximum(win["acc", x][HEAD_DIM:HEAD_DIM + 1], 1e-30))
    sel_add = jnp.where(score == -jnp.inf, 0.0, MASK_SCORE)
    seladd_sc[...] = per_block(lambda x: tile_g(sel_add[:, x * Q_BLOCK:(x + 1) * Q_BLOCK]))

    stage_keys = SEL_STAGE_TILES * KEY_CHUNK
    n_stages = qbs[-1] // SEL_STAGE_TILES + 1
    last_stage = ks_ref.shape[1] // stage_keys - 1

    def scores(k, s_buf):
        kk = jnp.minimum(k, last_stage)
        blk0 = pl.multiple_of(kk // SEL_GROUP_STAGES * SEL_STEP_BLOCKS, SEL_STEP_BLOCKS)
        qaug_sc[HEAD_DIM:HEAD_DIM + SEL_STEP_BLOCKS, :] = seladd_sc[pl.ds(blk0, SEL_STEP_BLOCKS), :].astype(BF16)
        k0 = pl.multiple_of(kk * stage_keys, stage_keys)
        bias = jnp.concatenate([bias_tile(lambda x, c=c: qbs[x] - (k * SEL_STAGE_TILES + c))
                                for c in range(SEL_STAGE_TILES)], axis=0)
        s = _dot(ks_ref[0, pl.ds(k0, stage_keys), :], qaug_sc[...]) + bias
        s_buf[...] = s
        return jnp.max(s, axis=0, keepdims=True)

    def weights(m, col_max, s_buf, p_buf):
        m_new = jnp.maximum(m, col_max)
        p_buf[...] = jnp.exp2(s_buf[...] - m_new).astype(BF16)
        return m_new, jnp.exp2(m - m_new)

    def accumulate(k, alpha, p_buf):
        acc_sc[...] = alpha * acc_sc[...] + _dot(vsT_ref[0, jnp.clip(k, 0, last_stage)], p_buf[...])

    def pair(j, carry):
        m, col_max, alpha = carry
        k = 2 * j
        col_max1 = scores(k + 1, s1_sc)
        m, alpha0 = weights(m, col_max, s0_sc, p0_sc)
        accumulate(k - 1, alpha, p1_sc)
        col_max2 = scores(k + 2, s0_sc)
        m, alpha1 = weights(m, col_max1, s1_sc, p1_sc)
        accumulate(k, alpha0, p0_sc)
        return m, col_max2, alpha1

    acc_sc[...] = jnp.zeros(acc_sc.shape, F32)
    p1_sc[...] = jnp.zeros(p1_sc.shape, BF16)
    m_init = jnp.full((1, NQ * GQ), 0.1 * MASK_SCORE, F32)
    carry = (m_init, scores(0, s0_sc), jnp.ones((1, NQ * GQ), F32))
    n_pairs = (n_stages + 1) // 2
    _, _, alpha = lax.fori_loop(0, n_pairs, pair, carry)
    accumulate(2 * n_pairs - 1, alpha, p1_sc)
    o_s = acc_sc[0:HEAD_DIM, :] / jnp.maximum(acc_sc[HEAD_DIM:HEAD_DIM + 1, :], 1e-30)

    def gate_row(br):
        def one(x):
            gt = jax.nn.sigmoid(gate_ref[0, x, 0])
            return jnp.concatenate([gt[br * G + g:br * G + g + 1, :] for g in range(G)], axis=1)
        return per_block(one)
    o = gate_row(0) * o_c + gate_row(1) * o_s + gate_row(2) * o_w
    for x in range(NQ):
        o_ref[0, x * Q_BLOCK:(x + 1) * Q_BLOCK, :] = jnp.concatenate(
            [o[:, x * GQ + g * Q_BLOCK:x * GQ + (g + 1) * Q_BLOCK].T for g in range(G)], axis=1)


def _nsa_attention(q, kc, vcT, kaug, vsT, vwT, gates, overlapT, bias_tiles, cmp_bias):
    B, T, _ = kaug.shape
    Hkv, G = NSA_KV_HEADS, NSA_GROUP
    GQ = G * Q_BLOCK
    nqb = T // Q_BLOCK
    n_cmp_pad = kc.shape[2]
    n_sel = T // SEL_BLOCK
    half_keys = SEL_STAGE_TILES * KEY_CHUNK
    NQ = NSA_STEP_QBLOCKS
    assert T % (SEL_GROUP_STAGES * half_keys) == 0 and nqb % NQ == 0
    per_head = lambda b, h, i: (b, h, 0, 0)
    v_rows = V_ROWS
    chunked = lambda w: pl.BlockSpec((1, T // w, v_rows, w), lambda b, h, i: (b, 0, h, 0))
    step_lanes = NQ * GQ
    return pl.pallas_call(
        functools.partial(_nsa_kernel, n_sel=n_sel, n_cmp_pad=n_cmp_pad),
        grid=(B, Hkv, nqb // NQ),
        in_specs=[pl.BlockSpec((1, NQ, 1, HEAD_DIM, GQ), lambda b, h, i: (b, i, h, 0, 0)),
                  pl.BlockSpec((1, 1, n_cmp_pad, HEAD_DIM), per_head),
                  pl.BlockSpec((1, 1, HEAD_DIM, n_cmp_pad), per_head),
                  pl.BlockSpec((1, T, LANES), lambda b, h, i: (b, 0, h)), chunked(half_keys),
                  pl.BlockSpec((1, T, LANES), lambda b, h, i: (b, 0, Hkv + h)), chunked(KEY_CHUNK),
                  pl.BlockSpec((1, NQ, 1, GATE_ROWS, Q_BLOCK), lambda b, h, i: (b, i, h, 0, 0)),
                  pl.BlockSpec(overlapT.shape, lambda b, h, i: (0, 0)),
                  pl.BlockSpec((1, N_ALL_TILES, KEY_CHUNK, GQ), lambda b, h, i: (h, 0, 0, 0)),
                  pl.BlockSpec((1, 2 * n_cmp_pad, GQ), lambda b, h, i: (h, 0, 0))],
        out_specs=pl.BlockSpec((1, NQ * Q_BLOCK, G * HEAD_DIM), lambda b, h, i: (b, i, h)),
        out_shape=jax.ShapeDtypeStruct((B, T, NSA_WIDTH), F32),
        scratch_shapes=[pltpu.VMEM((v_rows, step_lanes), F32), pltpu.VMEM((LANES, step_lanes), BF16),
                        pltpu.VMEM((n_sel, step_lanes), F32),
                        pltpu.VMEM((half_keys, step_lanes), F32), pltpu.VMEM((half_keys, step_lanes), F32),
                        pltpu.VMEM((half_keys, step_lanes), BF16), pltpu.VMEM((half_keys, step_lanes), BF16)],
        compiler_params=_params("parallel", "parallel", "arbitrary"), name="nsa_attention",
    )(q, kc, vcT, kaug, vsT, kaug, vwT, gates, overlapT, bias_tiles, cmp_bias)


def _rwkv_chunk_kernel(rw_ref, prev_ref, mu_ref, w0_ref, wup_ref, a0_ref, aup_ref, gup_ref, kk_ref, ka_ref,
                       rk_ref, seg_ref, tri_ref, q_ref, y0_ref, a_ref, d_ref, g_ref, bonus_ref):
    C, W, N = RWKV_CHUNK, RWKV_WIDTH, HEAD_DIM
    c = pl.program_id(1)
    x = rw_ref[0]
    R = x.shape[0]
    chunk_rows = [slice(ck * C, (ck + 1) * C) for ck in range(R // C)]
    row = lax.broadcasted_iota(jnp.int32, (R, 1), 0)
    last_prev = jnp.where(c == 0, 0.0, prev_ref[0, SUBLANES - 1:SUBLANES, :])
    x_prev = jnp.where(row == 0, last_prev, pltpu.roll(x, 1, axis=0))
    xs = x + (x_prev - x) * mu_ref[...]
    r, k, v = xs[:, 0:W], xs[:, W:2 * W], xs[:, 2 * W:3 * W]
    o = 3 * W
    wd, ad, gd = xs[:, o:o + DECAY_LORA], xs[:, o + DECAY_LORA:o + DECAY_LORA + AAA_LORA], \
        xs[:, o + DECAY_LORA + AAA_LORA:]
    w_log = -jax.nn.softplus(-(w0_ref[...] + _dot(jnp.tanh(wd).astype(BF16), wup_ref[...]))) - 0.5
    lw = -jnp.exp(w_log)
    lr = jax.nn.sigmoid(a0_ref[...] + _dot(ad.astype(BF16), aup_ref[...]))
    g_ref[0] = _dot(jax.nn.sigmoid(gd).astype(BF16), gup_ref[...])
    kk = k * kk_ref[...]
    def head_sums(z):
        return jnp.concatenate([_dot_f32_lhs(z[:, t * LANES:(t + 1) * LANES], seg_ref[...])
                                for t in range(W // LANES)], axis=1)
    kk = kk * lax.rsqrt(jnp.maximum(head_sums(kk * kk), 1e-24))
    k = k * (1.0 + (lr - 1.0) * ka_ref[...])
    bonus_ref[0] = head_sums(r * k * rk_ref[...]) * v
    a_vec, b_vec = -kk, kk * lr

    L = jnp.concatenate([_dot_f32_rhs(tri_ref[...], lw[rs]) for rs in chunk_rows], axis=0)
    L_end = jnp.concatenate([jnp.broadcast_to(L[rs.stop - 1:rs.stop, :], (C, W)) for rs in chunk_rows], axis=0)
    e_neg = jnp.exp(-L)
    e_rem = jnp.exp(L_end - L)
    At, Bt, Kt, Rt = a_vec * jnp.exp(L - lw), b_vec * e_neg, k * e_neg, r * jnp.exp(L)
    Bg, Kg = b_vec * e_rem, k * e_rem
    decay_end = [jnp.exp(L[rs.stop - 1:rs.stop, :]) for rs in chunk_rows]

    HG = RWKV_GROUP_HEADS
    GL = HG * N
    assert C == N and C & (C - 1) == 0
    blk_of = lambda idx: jnp.right_shift(idx, int(math.log2(C)))
    same_head = (blk_of(lax.broadcasted_iota(jnp.int32, (HG * C, GL), 0))
                 == blk_of(lax.broadcasted_iota(jnp.int32, (HG * C, GL), 1)))
    bf = lambda z: z.astype(BF16)
    block_diag = lambda y: jnp.where(same_head, jnp.concatenate([bf(y)] * HG, axis=0), 0.0)
    ri = lax.broadcasted_iota(jnp.int32, (C, GL), 0)
    cj = jnp.bitwise_and(lax.broadcasted_iota(jnp.int32, (C, GL), 1), C - 1)
    strict, incl = ri > cj, ri >= cj
    eye_c = (ri == cj).astype(F32)
    pr = lax.broadcasted_iota(jnp.int32, (N, 2 * N), 0)
    pc = lax.broadcasted_iota(jnp.int32, (N, 2 * N), 1)
    first_head, pair_eye = pc < N, (pr == jnp.bitwise_and(pc, N - 1)).astype(F32)

    def same_head_blocks(z):
        return jnp.where(first_head, z[0:N], z[N:2 * N])
    items = [(ck, gp) for ck in range(len(chunk_rows)) for gp in range(RWKV_HEADS // HG)]
    sl = [(chunk_rows[ck], slice(gp * GL, (gp + 1) * GL)) for ck, gp in items]
    ar = [bf(jnp.concatenate([At[s], Rt[s]], axis=0)) for s in sl]
    zb = [_dot_nt(a, block_diag(Bt[s])) for a, s in zip(ar, sl)]
    zk = [_dot_nt(a, block_diag(Kt[s])) for a, s in zip(ar, sl)]
    n_mat = [jnp.where(strict, z[0:C], 0.0) for z in zb]
    m_mat = [jnp.where(strict, z[0:C], 0.0) for z in zk]
    v_bd = [block_diag(v[s]) for s in sl]
    mv = [_dot(bf(m), vb) for m, vb in zip(m_mat, v_bd)]
    t_inv, n_pow = [eye_c + n for n in n_mat], n_mat
    for _ in range(int(math.log2(C)) - 1):
        n_pow = [_dot(bf(n), block_diag(n)) for n in n_pow]
        t_inv = [t + _dot(bf(t), block_diag(n)) for t, n in zip(t_inv, n_pow)]
    t_bf = [bf(t) for t in t_inv]
    ta = [_dot(t, block_diag(At[s])) for t, s in zip(t_bf, sl)]
    g0 = [_dot(t, block_diag(m)) for t, m in zip(t_bf, mv)]
    p_mat = [bf(jnp.where(incl, z[C:], 0.0)) for z in zb]
    pk_mat = [bf(jnp.where(incl, z[C:], 0.0)) for z in zk]
    q_out = [Rt[s] + _dot(p, block_diag(t)) for p, t, s in zip(p_mat, ta, sl)]
    y0_out = [_dot(p, block_diag(g)) + _dot(pk, vb) for p, g, pk, vb in zip(p_mat, g0, pk_mat, v_bd)]
    bgT = [bf(Bg[s].T) for s in sl]
    a_full = [_dot(b, bf(t)) for b, t in zip(bgT, ta)]
    d_full = [_dot(b, bf(g)) + _dot(bf(Kg[s].T), bf(v[s])) for b, g, s in zip(bgT, g0, sl)]
    for n, (ck, gp) in enumerate(items):
        rows, lanes = sl[n]
        q_ref[0, rows, lanes] = q_out[n].astype(q_ref.dtype)
        y0_ref[0, rows, lanes] = y0_out[n]
        for pp in range(HG // 2):
            blk = slice(pp * 2 * N, (pp + 1) * 2 * N)
            pair_lanes = slice(lanes.start + blk.start, lanes.start + blk.stop)
            a_ref[0, ck, gp * (HG // 2) + pp] = (same_head_blocks(a_full[n][blk, blk])
                                                 + pair_eye * decay_end[ck][:, pair_lanes])
            d_ref[0, ck, gp * (HG // 2) + pp] = same_head_blocks(d_full[n][blk, blk])


def _rwkv_chunks(rw, mu, w0, w_up, a0, a_up, g_up, k_k, k_a, r_k):
    B, T, cols = rw.shape
    C, W, H, N = RWKV_CHUNK, RWKV_WIDTH, RWKV_HEADS, HEAD_DIM
    nc = T // C
    S = RWKV_STEP_CHUNKS
    R = S * C
    assert nc % S == 0
    seg = jnp.asarray(np.kron(np.eye(LANES // N), np.ones((N, N))), BF16)
    tri = jnp.asarray(np.tril(np.ones((C, C))), BF16)
    row = lambda z: z.reshape(1, -1)
    const = lambda b, c: (0, 0)
    vec = pl.BlockSpec((1, W), const)
    tok = pl.BlockSpec((1, R, W), lambda b, c: (b, c, 0))
    mat = pl.BlockSpec((1, S, H // 2, N, 2 * N), lambda b, c: (b, c, 0, 0, 0))
    return pl.pallas_call(
        _rwkv_chunk_kernel,
        grid=(B, nc // S),
        in_specs=[pl.BlockSpec((1, R, cols), lambda b, c: (b, c, 0)),
                  pl.BlockSpec((1, SUBLANES, cols), lambda b, c: (b, jnp.maximum(c * (R // SUBLANES) - 1, 0), 0)),
                  pl.BlockSpec((1, cols), const), vec,
                  pl.BlockSpec((DECAY_LORA, W), const), vec,
                  pl.BlockSpec((AAA_LORA, W), const),
                  pl.BlockSpec((GATE_LORA, W), const), vec, vec, vec,
                  pl.BlockSpec((LANES, LANES), const), pl.BlockSpec((C, C), const)],
        out_specs=[tok, tok, mat, mat, tok, tok],
        out_shape=[jax.ShapeDtypeStruct((B, T, W), BF16), jax.ShapeDtypeStruct((B, T, W), F32),
                   jax.ShapeDtypeStruct((B, nc, H // 2, N, 2 * N), F32),
                   jax.ShapeDtypeStruct((B, nc, H // 2, N, 2 * N), F32),
                   jax.ShapeDtypeStruct((B, T, W), F32), jax.ShapeDtypeStruct((B, T, W), F32)],
        compiler_params=_params("parallel", "parallel"), name="rwkv_chunks",
    )(rw, rw, row(mu), row(w0), w_up.astype(BF16), row(a0), a_up.astype(BF16), g_up.astype(BF16),
      row(k_k), row(k_a), row(r_k), seg, tri)


def _pair_block_diag(x):
    first_head = lax.broadcasted_iota(jnp.int32, x.shape, 1) < x.shape[0]
    zero = jnp.zeros_like(x)
    return jnp.concatenate([jnp.where(first_head, x, zero), jnp.where(first_head, zero, x)], axis=0)


def _pair_side_by_side(z):
    n = z.shape[0] // 2
    return jnp.where(lax.broadcasted_iota(jnp.int32, (n, 2 * n), 1) < n, z[0:n], z[n:])


def _rwkv_state_kernel(a_ref, d_ref, h_ref, h_sc):
    @pl.when(pl.program_id(0) == 0)
    def _():
        h_sc[...] = jnp.zeros(h_sc.shape, F32)

    items = [(b, pair) for b in range(a_ref.shape[0]) for pair in range(a_ref.shape[2])]
    states = [h_sc[b, pair] for b, pair in items]
    for ck in range(a_ref.shape[1]):
        for (b, pair), st in zip(items, states):
            h_ref[b, ck, pair] = _pair_side_by_side(st).astype(h_ref.dtype)
        states = [_dot_hi_lo(_pair_block_diag(a_ref[b, ck, pair]), st) + _pair_block_diag(d_ref[b, ck, pair])
                  for (b, pair), st in zip(items, states)]
    for (b, pair), st in zip(items, states):
        h_sc[b, pair] = st


def _rwkv_readout_kernel(h_ref, q_ref, y0_ref, g_ref, bonus_ref, lw_ref, lb_ref, seg_ref, o_ref):
    C = RWKV_CHUNK
    inv_n = 1.0 / HEAD_DIM
    items = [(b, ck, pair, slice(ck * C, (ck + 1) * C), slice(pair * LANES, (pair + 1) * LANES))
             for b in range(h_ref.shape[0]) for ck in range(h_ref.shape[1]) for pair in range(h_ref.shape[2])]
    ys = [_dot(q_ref[b, rows, lanes], _pair_block_diag(h_ref[b, ck, pair])) + y0_ref[b, rows, lanes]
          for b, ck, pair, rows, lanes in items]
    means = [_dot_f32_lhs(y, seg_ref[...], 2) * inv_n for y in ys]
    cen = [y - mean for y, mean in zip(ys, means)]
    var = [_dot_f32_lhs(jnp.square(c), seg_ref[...], 2) * inv_n for c in cen]
    for (b, _, _, rows, lanes), c, v in zip(items, cen, var):
        yn = c * lax.rsqrt(v + LNX_EPS) * lw_ref[:, lanes] + lb_ref[:, lanes]
        o_ref[b, rows, lanes] = (yn + bonus_ref[b, rows, lanes]) * g_ref[b, rows, lanes]


def _rwkv_scan(A, D, Q, Y0, g, bonus, lnx_w, lnx_b):
    B, nc, P, N, N2 = A.shape
    T, W, C = Q.shape[1], Q.shape[2], RWKV_CHUNK
    S, SR = RWKV_SCAN_STEP_CHUNKS, RWKV_READOUT_STEP_CHUNKS
    assert nc % S == 0 and nc % SR == 0
    mat = lambda s: pl.BlockSpec((B, s, P, N, N2), lambda c: (0, c, 0, 0, 0))
    h_start = pl.pallas_call(
        _rwkv_state_kernel,
        grid=(nc // S,),
        in_specs=[mat(S), mat(S)],
        out_specs=mat(S),
        out_shape=jax.ShapeDtypeStruct((B, nc, P, N, N2), BF16),
        scratch_shapes=[pltpu.VMEM((B, P, N2, N2), F32)],
        compiler_params=_params("arbitrary"), name="rwkv_state",
    )(A, D)
    tok = pl.BlockSpec((B, SR * C, W), lambda c: (0, c, 0))
    vec = pl.BlockSpec((1, W), lambda c: (0, 0))
    seg = jnp.asarray(np.kron(np.eye(2), np.ones((HEAD_DIM, HEAD_DIM))), BF16)
    return pl.pallas_call(
        _rwkv_readout_kernel,
        grid=(nc // SR,),
        in_specs=[mat(SR), tok, tok, tok, tok, vec, vec, pl.BlockSpec((N2, N2), lambda c: (0, 0))],
        out_specs=tok,
        out_shape=jax.ShapeDtypeStruct((B, T, W), F32),
        compiler_params=_params("parallel"), name="rwkv_readout",
    )(h_start, Q, Y0, g, bonus, lnx_w.reshape(1, W), lnx_b.reshape(1, W), seg)


def _mix_xattn_kernel(x_ref, on_ref, or_ref, wo1_ref, wo2_ref, gx_ref, wq_ref, k_ref, v_ref, wo_ref, o_ref):
    x1 = x_ref[0] + _dot(on_ref[0].astype(BF16), wo1_ref[...]) + _dot(or_ref[0].astype(BF16), wo2_ref[...])
    q = _dot(_rms(x1, gx_ref[...]).astype(BF16), wq_ref[...])
    dh = q.shape[-1] // XATTN_HEADS
    qbf = (q * (dh ** -0.5)).astype(BF16)
    outs = []
    for h in range(XATTN_HEADS):
        hs = slice(h * dh, (h + 1) * dh)
        s = _dot_nt(qbf[:, hs], k_ref[0, :, hs])
        e = jnp.exp(s - jnp.max(s, axis=-1, keepdims=True))
        p = e / jnp.sum(e, axis=-1, keepdims=True)
        outs.append(_dot(p.astype(BF16), v_ref[0, :, hs]))
    o = jnp.concatenate(outs, axis=1).astype(BF16)
    o_ref[0] = x1 + _dot(o, wo_ref[...])


def _mix_xattn(x, o_nsa, o_rwkv, w_out, norm_x_g, w_q, mem_k, mem_v, w_o):
    B, T, D = x.shape
    M = mem_k.shape[1]
    tm = min(ROW_TILE, T)
    wo1, wo2 = w_out[:NSA_WIDTH].astype(BF16), w_out[NSA_WIDTH:].astype(BF16)
    const = lambda b, i: (0, 0)
    tile = lambda w: pl.BlockSpec((1, tm, w), lambda b, i: (b, i, 0))
    return pl.pallas_call(
        _mix_xattn_kernel,
        grid=(B, T // tm),
        in_specs=[tile(D), tile(NSA_WIDTH), tile(RWKV_WIDTH),
                  pl.BlockSpec(wo1.shape, const), pl.BlockSpec(wo2.shape, const),
                  pl.BlockSpec((1, D), const), pl.BlockSpec((D, D), const),
                  pl.BlockSpec((1, M, D), lambda b, i: (b, 0, 0)), pl.BlockSpec((1, M, D), lambda b, i: (b, 0, 0)),
                  pl.BlockSpec((D, D), const)],
        out_specs=tile(D),
        out_shape=jax.ShapeDtypeStruct((B, T, D), F32),
        compiler_params=_params("parallel", "parallel"), name="mix_xattn",
    )(x, o_nsa, o_rwkv, wo1, wo2, norm_x_g.reshape(1, D), w_q.astype(BF16), mem_k, mem_v, w_o.astype(BF16))


def _ffn_kernel(x_ref, g_ref, wg_ref, wu_ref, wd_ref, gf_ref, o_ref, *, final_norm):
    x = x_ref[...]
    h = _rms(x, g_ref[...]).astype(BF16)
    act = (jax.nn.silu(_dot(h, wg_ref[...])) * _dot(h, wu_ref[...])).astype(BF16)
    y = x + _dot(act, wd_ref[...])
    o_ref[...] = _rms(y, gf_ref[...]) if final_norm else y


def _ffn(x, norm_g, w_gate, w_up, w_down, final_g, final_norm):
    R, D = x.shape
    F = w_gate.shape[1]
    tm = min(FFN_ROW_TILE, R)
    const = lambda i: (0, 0)
    resident = lambda shape: pl.BlockSpec(shape, const, pipeline_mode=pl.Buffered(1))
    return pl.pallas_call(
        functools.partial(_ffn_kernel, final_norm=final_norm),
        grid=(R // tm,),
        in_specs=[pl.BlockSpec((tm, D), lambda i: (i, 0)), pl.BlockSpec((1, D), const),
                  resident((D, F)), resident((D, F)), resident((F, D)), pl.BlockSpec((1, D), const)],
        out_specs=pl.BlockSpec((tm, D), lambda i: (i, 0)),
        out_shape=jax.ShapeDtypeStruct((R, D), F32),
        compiler_params=_params("parallel"), name="ffn",
    )(x, norm_g.reshape(1, D), w_gate.astype(BF16), w_up.astype(BF16), w_down.astype(BF16),
      final_g.reshape(1, D))


def _overlap_matrix(n_cmp_pad, n_sel):
    c = np.arange(n_cmp_pad)[:, None] * CMP_STRIDE
    s = np.arange(n_sel)[None, :] * SEL_BLOCK
    return ((c <= s + SEL_BLOCK - 1) & (c + CMP_LEN - 1 >= s)).astype(np.float32)


def _layer(x, mem, rel_bias, final_g, is_last, norm_mix_g, w_in, nsa_gate_b, cmp_pe_k, cmp_pe_v,
           cmp_k_w1, cmp_k_b1, cmp_k_w2, cmp_v_w1, cmp_v_b1, cmp_v_w2,
           rwkv_mu, rwkv_w0, rwkv_w_up, rwkv_a0, rwkv_a_up, rwkv_g_up,
           rwkv_k_k, rwkv_k_a, rwkv_r_k, rwkv_lnx_w, rwkv_lnx_b, w_out,
           norm_x_g, norm_mem_g, w_q_x, w_kv_x, w_o_x, norm_ffn_g, w_gate, w_up, w_down):
    B, T, D = x.shape
    Hkv, G, dh = NSA_KV_HEADS, NSA_GROUP, HEAD_DIM
    q, kvc, kaug, vsT, vwT, gates, rw = _proj_in(x, norm_mix_g, w_in, nsa_gate_b)

    n16 = T // CMP_STRIDE
    kvc = kvc.reshape(B, T, 2 * KV_WIDTH)
    kc = _compress(kvc, 0, cmp_pe_k, cmp_k_w1, cmp_k_b1, cmp_k_w2, False)
    vcT = _compress(kvc, 1, cmp_pe_v, cmp_v_w1, cmp_v_b1, cmp_v_w2, True)
    bias_tiles, cmp_bias = _bias_tiles(rel_bias, n16)
    o_nsa = _nsa_attention(q, kc, vcT, kaug.reshape(B, T, 2 * Hkv * LANES), vsT, vwT,
                           gates, jnp.asarray(_overlap_matrix(n16, T // SEL_BLOCK).T, BF16), bias_tiles, cmp_bias)

    Q, Y0, A, Dm, g, bonus = _rwkv_chunks(rw.reshape(B, T, RWKV_COLS), rwkv_mu, rwkv_w0, rwkv_w_up, rwkv_a0,
                                          rwkv_a_up, rwkv_g_up, rwkv_k_k, rwkv_k_a, rwkv_r_k.reshape(-1))
    o_rwkv = _rwkv_scan(A, Dm, Q, Y0, g, bonus, rwkv_lnx_w, rwkv_lnx_b)

    M = mem.shape[1]
    (kv_mem,) = _norm_matmul(mem.reshape(B * M, D), norm_mem_g, [w_kv_x.astype(BF16)], [None], [BF16], ROW_TILE)
    kv_mem = kv_mem.reshape(B, M, 2 * D)
    x = _mix_xattn(x, o_nsa, o_rwkv, w_out, norm_x_g, w_q_x, kv_mem[..., :D], kv_mem[..., D:], w_o_x)
    x = _ffn(x.reshape(B * T, D), norm_ffn_g, w_gate, w_up, w_down, final_g, is_last)
    return x.reshape(B, T, D)


def kernel(x, mem, rel_bias, norm_f_g, norm_mix_g, w_in, nsa_gate_b, cmp_pe_k, cmp_pe_v, cmp_k_w1, cmp_k_b1, cmp_k_w2, cmp_v_w1, cmp_v_b1, cmp_v_w2, rwkv_mu, rwkv_w0, rwkv_w_up, rwkv_a0, rwkv_a_up, rwkv_g_up, rwkv_k_k, rwkv_k_a, rwkv_r_k, rwkv_lnx_w, rwkv_lnx_b, w_out, norm_x_g, norm_mem_g, w_q_x, w_kv_x, w_o_x, norm_ffn_g, w_gate, w_up, w_down):
    stacked = (norm_mix_g, w_in, nsa_gate_b, cmp_pe_k, cmp_pe_v, cmp_k_w1, cmp_k_b1, cmp_k_w2, cmp_v_w1,
               cmp_v_b1, cmp_v_w2, rwkv_mu, rwkv_w0, rwkv_w_up, rwkv_a0, rwkv_a_up, rwkv_g_up, rwkv_k_k,
               rwkv_k_a, rwkv_r_k, rwkv_lnx_w, rwkv_lnx_b, w_out, norm_x_g, norm_mem_g, w_q_x, w_kv_x, w_o_x,
               norm_ffn_g, w_gate, w_up, w_down)
    depth = w_in.shape[0]
    for l in range(depth):
        x = _layer(x, mem, rel_bias, norm_f_g, l == depth - 1, *[p[l] for p in stacked])
    return x
```

```python
import functools
import math

import numpy as np
import jax
import jax.numpy as jnp
from jax import lax
from jax.experimental import pallas as pl
from jax.experimental.pallas import tpu as pltpu

F32 = jnp.float32
BF16 = jnp.bfloat16

LANES = 128
SUBLANES = 8
BF16_ROWS = 16
VMEM_LIMIT_BYTES = 56 * 1024 * 1024

HEAD_DIM = 64
NSA_HEADS = 8
NSA_KV_HEADS = 2
NSA_GROUP = NSA_HEADS // NSA_KV_HEADS
NSA_WIDTH = NSA_HEADS * HEAD_DIM
KV_WIDTH = NSA_KV_HEADS * HEAD_DIM
RWKV_HEADS = 8
RWKV_WIDTH = RWKV_HEADS * HEAD_DIM
CMP_LEN = 32
CMP_STRIDE = 16
SEL_BLOCK = 64
SEL_SHIFT = 6
SEL_TOP = 16
WINDOW = 512
Q_BLOCK = 128
DECAY_LORA = 64
AAA_LORA = 64
GATE_LORA = 128
N_BUCKETS = 32
MAX_DISTANCE = 2048
XATTN_HEADS = 4
RMS_EPS = 1e-6
LNX_EPS = 64e-5
FORCE_SCORE = 1e4
NEG_SCORE = -1e9
MASK_SCORE = -1e30
LOG2E = math.log2(math.e)
RWKV_COLS = 3 * RWKV_WIDTH + DECAY_LORA + AAA_LORA + GATE_LORA
NSA_COLS = NSA_WIDTH + 6 * KV_WIDTH + 3 * NSA_HEADS

KEY_CHUNK = 128
RWKV_CHUNK = 64
RWKV_SCAN_STEP_CHUNKS = 8
RWKV_READOUT_STEP_CHUNKS = 4
RWKV_GROUP_HEADS = 2
RWKV_STEP_CHUNKS = 8
ROW_TILE = 512
FFN_ROW_TILE = 512


def _t5_thresholds():
    d = np.arange(0, 2 * MAX_DISTANCE, dtype=np.int64)
    max_exact = N_BUCKETS // 2
    nf = np.maximum(d, 1).astype(np.float32)
    large = max_exact + (np.log(nf / np.float32(max_exact)) / np.float32(math.log(MAX_DISTANCE / max_exact))
                         * np.float32(N_BUCKETS - max_exact)).astype(np.int32)
    bucket = np.where(d < max_exact, d, np.minimum(large, N_BUCKETS - 1))
    return [int(np.argmax(bucket >= k)) for k in range(N_BUCKETS)]


T5_THRESHOLDS = _t5_thresholds()
N_BIAS_TILES = -(-(T5_THRESHOLDS[-1] + KEY_CHUNK) // KEY_CHUNK) + 1
TILE_MASKED = N_BIAS_TILES
TILE_WINDOW_EDGE = N_BIAS_TILES + 1
N_ALL_TILES = N_BIAS_TILES + 2
SEL_STEP_BLOCKS = 16
SEL_STAGE_TILES = 4
SEL_GROUP_STAGES = SEL_STEP_BLOCKS * SEL_BLOCK // (SEL_STAGE_TILES * KEY_CHUNK)
NSA_STEP_QBLOCKS = 4
V_ROWS = HEAD_DIM + BF16_ROWS
GATE_ROWS = 16


def _params(*semantics):
    return pltpu.CompilerParams(dimension_semantics=semantics, vmem_limit_bytes=VMEM_LIMIT_BYTES)


def _rms(x, g):
    return x * lax.rsqrt(jnp.mean(x * x, axis=-1, keepdims=True) + RMS_EPS) * g


def _dot(a, b, **kw):
    return jnp.dot(a, b, preferred_element_type=F32, **kw)


def _split3(x):
    hi = x.astype(BF16)
    r1 = x - hi.astype(F32)
    mid = r1.astype(BF16)
    lo = (r1 - mid.astype(F32)).astype(BF16)
    return hi, mid, lo


def _dot_f32_lhs(x, w01, pieces=3):
    w = w01.astype(BF16)
    hi, mid, lo = _split3(x)
    return _dot(hi, w) + (_dot(mid, w) + _dot(lo, w) if pieces == 3 else _dot(mid, w))


def _dot_f32_rhs(w01, x):
    w = w01.astype(BF16)
    hi, mid, lo = _split3(x)
    return _dot(w, hi) + (_dot(w, mid) + _dot(w, lo))


def _dot_hi_lo(a, b):
    a_hi = a.astype(BF16)
    a_lo = (a - a_hi.astype(F32)).astype(BF16)
    b_hi = b.astype(BF16)
    b_lo = (b - b_hi.astype(F32)).astype(BF16)
    return _dot(a_hi, b_hi) + (_dot(a_hi, b_lo) + _dot(a_lo, b_hi))


def _dot_nt(a, b, **kw):
    return lax.dot_general(a, b, (((1,), (1,)), ((), ())), preferred_element_type=F32, **kw)


def _norm_matmul_kernel(x_ref, g_ref, *refs, nseg, bias_flags):
    nb = sum(bias_flags)
    w_refs, b_refs, o_refs = refs[:nseg], refs[nseg:nseg + nb], refs[nseg + nb:]
    xn = _rms(x_ref[...], g_ref[...]).astype(BF16)
    bi = 0
    for s in range(nseg):
        y = _dot(xn, w_refs[s][...])
        if bias_flags[s]:
            y = y + b_refs[bi][...]
            bi += 1
        o_refs[s][...] = y.astype(o_refs[s].dtype)


def _norm_matmul(x, g, weights, biases, out_dtypes, row_tile):
    R, D = x.shape
    tm = min(row_tile, R)
    assert R % tm == 0
    nseg = len(weights)
    bias_flags = tuple(b is not None for b in biases)
    const = lambda i: (0, 0)
    in_specs = [pl.BlockSpec((tm, D), lambda i: (i, 0)), pl.BlockSpec((1, D), const)]
    in_specs += [pl.BlockSpec(w.shape, const) for w in weights]
    in_specs += [pl.BlockSpec((1, b.shape[-1]), const) for b in biases if b is not None]
    out_specs = [pl.BlockSpec((tm, w.shape[1]), lambda i: (i, 0)) for w in weights]
    out_shape = [jax.ShapeDtypeStruct((R, w.shape[1]), dt) for w, dt in zip(weights, out_dtypes)]
    return pl.pallas_call(
        functools.partial(_norm_matmul_kernel, nseg=nseg, bias_flags=bias_flags),
        grid=(R // tm,), in_specs=in_specs, out_specs=out_specs, out_shape=out_shape,
        compiler_params=_params("parallel"), name="norm_matmul",
    )(x, g.reshape(1, D), *weights, *[b.reshape(1, -1) for b in biases if b is not None])


def _proj_in_kernel(x_ref, g_ref, wq_ref, wc_ref, wk_ref, wvT_ref, wgT_ref, bg_ref, wr_ref,
                    q_ref, kvc_ref, kaug_ref, vsT_ref, vwT_ref, gate_ref, rw_ref, *, seq_len):
    tm = x_ref.shape[0]
    xn = _rms(x_ref[...], g_ref[...]).astype(BF16)
    qT = (_dot_nt(wq_ref[...], xn) * (HEAD_DIM ** -0.5 * LOG2E)).astype(BF16)
    for j in range(tm // Q_BLOCK):
        for hg in range(NSA_HEADS):
            h, g = divmod(hg, NSA_GROUP)
            q_ref[0, j, h, :, g * Q_BLOCK:(g + 1) * Q_BLOCK] = qT[hg * HEAD_DIM:(hg + 1) * HEAD_DIM,
                                                                  j * Q_BLOCK:(j + 1) * Q_BLOCK]
    kvc_ref[...] = _dot(xn, wc_ref[...])
    rw_ref[...] = _dot(xn, wr_ref[...])
    k_all = _dot(xn, wk_ref[...])
    tok = lax.rem(pl.program_id(0) * tm, seq_len) + lax.broadcasted_iota(jnp.int32, k_all.shape, 0)
    lane = lax.broadcasted_iota(jnp.int32, k_all.shape, 1)
    blk = jnp.bitwise_and(jnp.right_shift(tok, SEL_SHIFT), SEL_STEP_BLOCKS - 1)
    hot = (jnp.bitwise_and(lane, LANES - 1) == HEAD_DIM + blk) & (lane < NSA_KV_HEADS * LANES)
    kaug_ref[...] = jnp.where(hot, 1.0, k_all).astype(BF16)
    vT = _dot_nt(wvT_ref[...], xn)
    row = lax.broadcasted_iota(jnp.int32, vT.shape, 0)
    ones_row = row == HEAD_DIM
    for grp in range(1, 2 * NSA_KV_HEADS):
        ones_row = ones_row | (row == grp * V_ROWS + HEAD_DIM)
    vT = jnp.where(ones_row, 1.0, vT).astype(BF16)
    half = NSA_KV_HEADS * V_ROWS
    stage_keys = vsT_ref.shape[3]
    for c in range(tm // stage_keys):
        vsT_ref[0, c] = vT[0:half, c * stage_keys:(c + 1) * stage_keys]
    for c in range(tm // KEY_CHUNK):
        vwT_ref[0, c] = vT[half:, c * KEY_CHUNK:(c + 1) * KEY_CHUNK]
    gT = _dot_nt(wgT_ref[...], xn) + bg_ref[...]
    for j in range(tm // Q_BLOCK):
        for h in range(NSA_KV_HEADS):
            gate_ref[0, j, h] = gT[h * GATE_ROWS:(h + 1) * GATE_ROWS, j * Q_BLOCK:(j + 1) * Q_BLOCK]


def _proj_in(x, norm_g, w_in, gate_b):
    B, T, D = x.shape
    Hkv, G, dh = NSA_KV_HEADS, NSA_GROUP, HEAD_DIM
    tm = ROW_TILE
    stage_keys = SEL_STAGE_TILES * KEY_CHUNK
    assert T % tm == 0 and tm % Q_BLOCK == 0 and tm % stage_keys == 0
    kv0 = NSA_WIDTH
    g0 = kv0 + 6 * KV_WIDTH
    stream = lambda s: w_in[:, kv0 + s * KV_WIDTH:kv0 + (s + 1) * KV_WIDTH].reshape(D, Hkv, dh)
    pad_cols = lambda w: jnp.pad(w, ((0, 0), (0, 0), (0, LANES - dh))).reshape(D, Hkv * LANES)
    pad_rows = lambda w: jnp.pad(w.transpose(1, 2, 0), ((0, 0), (0, V_ROWS - dh), (0, 0))).reshape(Hkv * V_ROWS, D)
    w_k = jnp.concatenate([pad_cols(stream(2)), pad_cols(stream(4))], axis=1)
    w_vT = jnp.concatenate([pad_rows(stream(3)), pad_rows(stream(5))], axis=0)
    reorder = lambda a: a.reshape(-1, Hkv, G, 3).transpose(1, 3, 2, 0).reshape(Hkv, 3 * G, -1)
    pad_gate = lambda a: jnp.pad(a, ((0, 0), (0, GATE_ROWS - 3 * G), (0, 0))).reshape(Hkv * GATE_ROWS, -1)
    w_gT = pad_gate(reorder(w_in[:, g0:NSA_COLS]))
    b_g = pad_gate(reorder(gate_b.reshape(1, -1)))
    weights = [w_in[:, :kv0].T, w_in[:, kv0:kv0 + 2 * KV_WIDTH], w_k, w_vT, w_gT]
    weights = [w.astype(BF16) for w in weights] + [b_g, w_in[:, NSA_COLS:].astype(BF16)]
    nt = T // tm
    rows = lambda n: pl.BlockSpec((tm, n), lambda i: (i, 0))
    const = lambda i: (0, 0)
    return pl.pallas_call(
        functools.partial(_proj_in_kernel, seq_len=T),
        grid=(B * nt,),
        in_specs=[rows(D), pl.BlockSpec((1, D), const)] + [pl.BlockSpec(w.shape, const) for w in weights],
        out_specs=[pl.BlockSpec((1, tm // Q_BLOCK, Hkv, dh, G * Q_BLOCK), lambda i: (i // nt, i % nt, 0, 0, 0)),
                   rows(2 * KV_WIDTH), rows(2 * Hkv * LANES),
                   pl.BlockSpec((1, tm // stage_keys, Hkv * V_ROWS, stage_keys), lambda i: (i // nt, i % nt, 0, 0)),
                   pl.BlockSpec((1, tm // KEY_CHUNK, Hkv * V_ROWS, KEY_CHUNK), lambda i: (i // nt, i % nt, 0, 0)),
                   pl.BlockSpec((1, tm // Q_BLOCK, Hkv, GATE_ROWS, Q_BLOCK), lambda i: (i // nt, i % nt, 0, 0, 0)),
                   rows(RWKV_COLS)],
        out_shape=[jax.ShapeDtypeStruct((B, T // Q_BLOCK, Hkv, dh, G * Q_BLOCK), BF16),
                   jax.ShapeDtypeStruct((B * T, 2 * KV_WIDTH), F32),
                   jax.ShapeDtypeStruct((B * T, 2 * Hkv * LANES), BF16),
                   jax.ShapeDtypeStruct((B, T // stage_keys, Hkv * V_ROWS, stage_keys), BF16),
                   jax.ShapeDtypeStruct((B, T // KEY_CHUNK, Hkv * V_ROWS, KEY_CHUNK), BF16),
                   jax.ShapeDtypeStruct((B, T // Q_BLOCK, Hkv, GATE_ROWS, Q_BLOCK), F32),
                   jax.ShapeDtypeStruct((B * T, RWKV_COLS), F32)],
        compiler_params=_params("parallel"), name="proj_in",
    )(x.reshape(B * T, D), norm_g.reshape(1, D), *weights)


def _compress_kernel(x_ref, pe_ref, w1_ref, b1_ref, w2_ref, o_ref, *, transpose_out):
    n16 = x_ref.shape[1] // CMP_STRIDE
    hidden = w1_ref.shape[2] // NSA_KV_HEADS
    lo = jnp.zeros((n16, w1_ref.shape[2]), F32)
    hi = jnp.zeros((n16, w1_ref.shape[2]), F32)
    for l in range(CMP_STRIDE):
        rows = x_ref[0, pl.ds(l, n16, stride=CMP_STRIDE), :]
        lo = lo + _dot((rows + pe_ref[l:l + 1, :]).astype(BF16), w1_ref[l])
        hi = hi + _dot((rows + pe_ref[CMP_STRIDE + l:CMP_STRIDE + l + 1, :]).astype(BF16), w1_ref[CMP_STRIDE + l])
    h = lo + pltpu.roll(hi, n16 - 1, axis=0) + b1_ref[...]
    h = jax.nn.gelu(h).astype(BF16)
    for hkv in range(NSA_KV_HEADS):
        hh = h[:, hkv * hidden:(hkv + 1) * hidden]
        if transpose_out:
            o_ref[0, hkv] = _dot_nt(w2_ref[...], hh).astype(o_ref.dtype)
        else:
            o_ref[0, hkv] = _dot(hh, w2_ref[...]).astype(o_ref.dtype)


def _compress(kvc, stream, pe, w1, b1, w2, transpose_out):
    B, T, _ = kvc.shape
    H, dh = NSA_KV_HEADS, HEAD_DIM
    n16 = T // CMP_STRIDE
    hidden = w1.shape[1]
    eye = jnp.eye(H, dtype=w1.dtype)
    w1_bd = jnp.einsum('ldn,hg->lhdgn', w1.reshape(CMP_LEN, dh, hidden), eye).reshape(CMP_LEN, H * dh, H * hidden)
    w2b = (w2.T if transpose_out else w2).astype(BF16)
    oshape = (B, H, dh, n16) if transpose_out else (B, H, n16, dh)
    return pl.pallas_call(
        functools.partial(_compress_kernel, transpose_out=transpose_out),
        grid=(B,),
        in_specs=[pl.BlockSpec((1, T, H * dh), lambda b: (b, 0, stream)),
                  pl.BlockSpec((CMP_LEN, H * dh), lambda b: (0, 0)),
                  pl.BlockSpec(w1_bd.shape, lambda b: (0, 0, 0)),
                  pl.BlockSpec((1, H * hidden), lambda b: (0, 0)),
                  pl.BlockSpec(w2b.shape, lambda b: (0, 0))],
        out_specs=pl.BlockSpec((1,) + oshape[1:], lambda b: (b, 0, 0, 0)),
        out_shape=jax.ShapeDtypeStruct(oshape, BF16),
        compiler_params=_params("parallel"), name="nsa_compress",
    )(kvc, jnp.tile(pe, (1, H)), w1_bd.astype(BF16), jnp.tile(b1.reshape(1, hidden), (1, H)), w2b)


def _bias_of_distance(tab_ref, h, d):
    val = jnp.full(d.shape, tab_ref[h, 0], F32)
    for k in range(1, N_BUCKETS):
        val = jnp.where(d >= T5_THRESHOLDS[k], tab_ref[h, k], val)
    return val * LOG2E


def _bias_tiles_kernel(tab_ref, bt_ref, cb_ref, *, n_cmp_pad):
    hkv = pl.program_id(0)
    j = lax.broadcasted_iota(jnp.int32, (KEY_CHUNK, Q_BLOCK), 0)
    i = lax.broadcasted_iota(jnp.int32, (KEY_CHUNK, Q_BLOCK), 1)
    r2 = lax.broadcasted_iota(jnp.int32, (2 * n_cmp_pad, Q_BLOCK), 0)
    i2 = lax.broadcasted_iota(jnp.int32, (2 * n_cmp_pad, Q_BLOCK), 1)
    l2 = r2 - (n_cmp_pad - KEY_CHUNK)
    d2 = i2 - CMP_STRIDE * l2 + (CMP_STRIDE * KEY_CHUNK - Q_BLOCK - (CMP_LEN - 1))
    hidden2 = (l2 >= KEY_CHUNK) | ((l2 >= 0) & (d2 < 0))
    d2 = jnp.where((l2 >= 0) & (l2 < KEY_CHUNK), d2, 2 * MAX_DISTANCE)
    for g in range(NSA_GROUP):
        h = hkv * NSA_GROUP + g
        lanes = slice(g * Q_BLOCK, (g + 1) * Q_BLOCK)
        for m in range(N_BIAS_TILES):
            tile = _bias_of_distance(tab_ref, h, m * KEY_CHUNK + i - j)
            if m == 0:
                tile = jnp.where(j <= i, tile, MASK_SCORE)
            bt_ref[0, m, :, lanes] = tile
        bt_ref[0, TILE_MASKED, :, lanes] = jnp.full((KEY_CHUNK, Q_BLOCK), MASK_SCORE, F32)
        edge = _bias_of_distance(tab_ref, h, WINDOW + i - j)
        bt_ref[0, TILE_WINDOW_EDGE, :, lanes] = jnp.where(j > i, edge, MASK_SCORE)
        cb_ref[0, :, lanes] = jnp.where(hidden2, MASK_SCORE, _bias_of_distance(tab_ref, h, d2))


def _bias_tiles(rel_bias, n_cmp_pad):
    assert CMP_STRIDE * KEY_CHUNK - Q_BLOCK - (CMP_LEN - 1) >= T5_THRESHOLDS[-1]
    GQ = NSA_GROUP * Q_BLOCK
    return pl.pallas_call(
        functools.partial(_bias_tiles_kernel, n_cmp_pad=n_cmp_pad),
        grid=(NSA_KV_HEADS,),
        in_specs=[pl.BlockSpec(memory_space=pltpu.SMEM)],
        out_specs=[pl.BlockSpec((1, N_ALL_TILES, KEY_CHUNK, GQ), lambda h: (h, 0, 0, 0)),
                   pl.BlockSpec((1, 2 * n_cmp_pad, GQ), lambda h: (h, 0, 0))],
        out_shape=[jax.ShapeDtypeStruct((NSA_KV_HEADS, N_ALL_TILES, KEY_CHUNK, GQ), F32),
                   jax.ShapeDtypeStruct((NSA_KV_HEADS, 2 * n_cmp_pad, GQ), F32)],
        compiler_params=_params("parallel"), name="t5_bias_tiles",
    )(rel_bias.T)


def _nsa_kernel(q_ref, kc_ref, vcT_ref, ks_ref, vsT_ref, kw_ref, vwT_ref, gate_ref, ovT_ref, bt_ref, cb_ref,
                o_ref, acc_sc, qaug_sc, seladd_sc, s0_sc, s1_sc, p0_sc, p1_sc, *, n_sel, n_cmp_pad):
    G, NQ = NSA_GROUP, NSA_STEP_QBLOCKS
    GQ = G * Q_BLOCK
    qbs = [pl.program_id(2) * NQ + x for x in range(NQ)]
    per_block = lambda fn: jnp.concatenate([fn(x) for x in range(NQ)], axis=1)
    tile_g = lambda a: jnp.concatenate([a] * G, axis=1)
    qT = per_block(lambda x: q_ref[0, x, 0])
    qaug_sc[0:HEAD_DIM, :] = qT
    qaug_sc[HEAD_DIM:, :] = jnp.zeros((qaug_sc.shape[0] - HEAD_DIM, NQ * GQ), BF16)
    lane_q = lax.broadcasted_iota(jnp.int32, (1, Q_BLOCK), 1)
    t = per_block(lambda x: qbs[x] * Q_BLOCK + lane_q)

    def bias_tile(dist_of_block):
        def one(x):
            dist = dist_of_block(x)
            return bt_ref[0, jnp.where(dist < 0, TILE_MASKED, jnp.minimum(dist, N_BIAS_TILES - 1))]
        return per_block(one)

    n_back = WINDOW // KEY_CHUNK
    q_win = qaug_sc[...]
    win = {}

    def win_scores(x, back):
        kc = jnp.maximum(qbs[x] - back, 0)
        edge = TILE_WINDOW_EDGE if back == n_back else back
        tile = jnp.where(qbs[x] >= back, edge, TILE_MASKED)
        k_chunk = kw_ref[0, pl.ds(pl.multiple_of(kc * KEY_CHUNK, KEY_CHUNK), KEY_CHUNK), :]
        win["s", x, back] = _dot(k_chunk, q_win[:, x * GQ:(x + 1) * GQ]) + bt_ref[0, tile]
        col_max = jnp.max(win["s", x, back], axis=0, keepdims=True)
        win["m", x] = jnp.maximum(win["m", x], col_max) if ("m", x) in win else col_max

    def win_weights(x, back):
        win["p", x, back] = jnp.exp2(win["s", x, back] - win["m", x]).astype(BF16)

    def win_values(x, back):
        pv = _dot(vwT_ref[0, jnp.maximum(qbs[x] - back, 0)], win["p", x, back])
        win["acc", x] = win["acc", x] + pv if ("acc", x) in win else pv

    backs = list(range(n_back, -1, -1))
    window_work = [functools.partial(fn, x, b) for x in range(NQ)
                   for fn in (win_scores, win_weights, win_values) for b in backs]

    def cmp_bias(x):
        start = pl.multiple_of(n_cmp_pad - (Q_BLOCK // CMP_STRIDE) * (qbs[x] + 1), SUBLANES)
        return cb_ref[0, pl.ds(start, n_cmp_pad), :]
    s = _dot(kc_ref[0, 0], qT) + per_block(cmp_bias)
    e = jnp.exp2(s - jnp.maximum(jnp.max(s, axis=0, keepdims=True), 0.1 * MASK_SCORE))
    p = e * (1.0 / jnp.maximum(jnp.sum(e, axis=0, keepdims=True), 1e-30))
    o_c = _dot(vcT_ref[0, 0], p.astype(BF16))
    psum = per_block(lambda x: sum(p[:, x * GQ + g * Q_BLOCK:x * GQ + (g + 1) * Q_BLOCK] for g in range(G)))

    imp = _dot_f32_rhs(ovT_ref[...], psum)
    bj = lax.broadcasted_iota(jnp.int32, (n_sel, NQ * Q_BLOCK), 0)
    cur = jnp.right_shift(t, SEL_SHIFT)
    forced = (bj == 0) | (bj == cur) | (bj == cur - 1)
    valid = bj * SEL_BLOCK <= t
    score = jnp.where(forced, -jnp.inf, jnp.where(valid, imp, NEG_SCORE))
    bjf = bj.astype(F32)
    n_rounds = max(min(SEL_TOP, n_sel) - 3, 0)
    for rnd in range(n_rounds):
        mx = jnp.max(score, axis=0, keepdims=True)
        first = jnp.min(jnp.where(score == mx, bjf, float(n_sel)), axis=0, keepdims=True)
        score = jnp.where(bjf == first, -jnp.inf, score)
        take = -(-len(window_work) // (n_rounds - rnd))
        for piece in window_work[:take]:
            piece()
        window_work = window_work[take:]
    for piece in window_work:
        piece()
    o_w = per_block(lambda x: win["acc", x][0:HEAD_DIM] / jnp.maximum(win["acc", x][HEAD_DIM:HEAD_DIM + 1], 1e-30))
    sel_add = jnp.where(score == -jnp.inf, 0.0, MASK_SCORE)
    seladd_sc[...] = per_block(lambda x: tile_g(sel_add[:, x * Q_BLOCK:(x + 1) * Q_BLOCK]))

    stage_keys = SEL_STAGE_TILES * KEY_CHUNK
    n_stages = qbs[-1] // SEL_STAGE_TILES + 1
    last_stage = ks_ref.shape[1] // stage_keys - 1

    def scores(k, s_buf):
        kk = jnp.minimum(k, last_stage)
        blk0 = pl.multiple_of(kk // SEL_GROUP_STAGES * SEL_STEP_BLOCKS, SEL_STEP_BLOCKS)
        qaug_sc[HEAD_DIM:HEAD_DIM + SEL_STEP_BLOCKS, :] = seladd_sc[pl.ds(blk0, SEL_STEP_BLOCKS), :].astype(BF16)
        k0 = pl.multiple_of(kk * stage_keys, stage_keys)
        bias = jnp.concatenate([bias_tile(lambda x, c=c: qbs[x] - (k * SEL_STAGE_TILES + c))
                                for c in range(SEL_STAGE_TILES)], axis=0)
        s = _dot(ks_ref[0, pl.ds(k0, stage_keys), :], qaug_sc[...]) + bias
        s_buf[...] = s
        return jnp.max(s, axis=0, keepdims=True)

    def weights(m, col_max, s_buf, p_buf):
        m_new = jnp.maximum(m, col_max)
        p_buf[...] = jnp.exp2(s_buf[...] - m_new).astype(BF16)
        return m_new, jnp.exp2(m - m_new)

    def accumulate(k, alpha, p_buf):
        acc_sc[...] = alpha * acc_sc[...] + _dot(vsT_ref[0, jnp.clip(k, 0, last_stage)], p_buf[...])

    def pair(j, carry):
        m, col_max, alpha = carry
        k = 2 * j
        col_max1 = scores(k + 1, s1_sc)
        m, alpha0 = weights(m, col_max, s0_sc, p0_sc)
        accumulate(k - 1, alpha, p1_sc)
        col_max2 = scores(k + 2, s0_sc)
        m, alpha1 = weights(m, col_max1, s1_sc, p1_sc)
        accumulate(k, alpha0, p0_sc)
        return m, col_max2, alpha1

    acc_sc[...] = jnp.zeros(acc_sc.shape, F32)
    p1_sc[...] = jnp.zeros(p1_sc.shape, BF16)
    m_init = jnp.full((1, NQ * GQ), 0.1 * MASK_SCORE, F32)
    carry = (m_init, scores(0, s0_sc), jnp.ones((1, NQ * GQ), F32))
    n_pairs = n_stages // 2
    m, col_max, alpha = lax.fori_loop(0, n_pairs, pair, carry)
    last = 2 * n_pairs

    @pl.when(n_stages % 2 == 1)
    def _():
        _, alpha_last = weights(m, col_max, s0_sc, p0_sc)
        accumulate(last - 1, alpha, p1_sc)
        accumulate(last, alpha_last, p0_sc)

    @pl.when(n_stages % 2 == 0)
    def _():
        accumulate(last - 1, alpha, p1_sc)
    o_s = acc_sc[0:HEAD_DIM, :] / jnp.maximum(acc_sc[HEAD_DIM:HEAD_DIM + 1, :], 1e-30)

    def gate_row(br):
        def one(x):
            gt = jax.nn.sigmoid(gate_ref[0, x, 0])
            return jnp.concatenate([gt[br * G + g:br * G + g + 1, :] for g in range(G)], axis=1)
        return per_block(one)
    o = gate_row(0) * o_c + gate_row(1) * o_s + gate_row(2) * o_w
    for x in range(NQ):
        o_ref[0, x * Q_BLOCK:(x + 1) * Q_BLOCK, :] = jnp.concatenate(
            [o[:, x * GQ + g * Q_BLOCK:x * GQ + (g + 1) * Q_BLOCK].T for g in range(G)], axis=1)


def _nsa_attention(q, kc, vcT, kaug, vsT, vwT, gates, overlapT, bias_tiles, cmp_bias):
    B, T, _ = kaug.shape
    Hkv, G = NSA_KV_HEADS, NSA_GROUP
    GQ = G * Q_BLOCK
    nqb = T // Q_BLOCK
    n_cmp_pad = kc.shape[2]
    n_sel = T // SEL_BLOCK
    half_keys = SEL_STAGE_TILES * KEY_CHUNK
    NQ = NSA_STEP_QBLOCKS
    assert T % (SEL_GROUP_STAGES * half_keys) == 0 and nqb % NQ == 0
    per_head = lambda b, h, i: (b, h, 0, 0)
    v_rows = V_ROWS
    chunked = lambda w: pl.BlockSpec((1, T // w, v_rows, w), lambda b, h, i: (b, 0, h, 0))
    step_lanes = NQ * GQ
    return pl.pallas_call(
        functools.partial(_nsa_kernel, n_sel=n_sel, n_cmp_pad=n_cmp_pad),
        grid=(B, Hkv, nqb // NQ),
        in_specs=[pl.BlockSpec((1, NQ, 1, HEAD_DIM, GQ), lambda b, h, i: (b, i, h, 0, 0)),
                  pl.BlockSpec((1, 1, n_cmp_pad, HEAD_DIM), per_head),
                  pl.BlockSpec((1, 1, HEAD_DIM, n_cmp_pad), per_head),
                  pl.BlockSpec((1, T, LANES), lambda b, h, i: (b, 0, h)), chunked(half_keys),
                  pl.BlockSpec((1, T, LANES), lambda b, h, i: (b, 0, Hkv + h)), chunked(KEY_CHUNK),
                  pl.BlockSpec((1, NQ, 1, GATE_ROWS, Q_BLOCK), lambda b, h, i: (b, i, h, 0, 0)),
                  pl.BlockSpec(overlapT.shape, lambda b, h, i: (0, 0)),
                  pl.BlockSpec((1, N_ALL_TILES, KEY_CHUNK, GQ), lambda b, h, i: (h, 0, 0, 0)),
                  pl.BlockSpec((1, 2 * n_cmp_pad, GQ), lambda b, h, i: (h, 0, 0))],
        out_specs=pl.BlockSpec((1, NQ * Q_BLOCK, G * HEAD_DIM), lambda b, h, i: (b, i, h)),
        out_shape=jax.ShapeDtypeStruct((B, T, NSA_WIDTH), F32),
        scratch_shapes=[pltpu.VMEM((v_rows, step_lanes), F32), pltpu.VMEM((LANES, step_lanes), BF16),
                        pltpu.VMEM((n_sel, step_lanes), F32),
                        pltpu.VMEM((half_keys, step_lanes), F32), pltpu.VMEM((half_keys, step_lanes), F32),
                        pltpu.VMEM((half_keys, step_lanes), BF16), pltpu.VMEM((half_keys, step_lanes), BF16)],
        compiler_params=_params("parallel", "parallel", "arbitrary"), name="nsa_attention",
    )(q, kc, vcT, kaug, vsT, kaug, vwT, gates, overlapT, bias_tiles, cmp_bias)


def _rwkv_chunk_kernel(rw_ref, prev_ref, mu_ref, w0_ref, wup_ref, a0_ref, aup_ref, gup_ref, kk_ref, ka_ref,
                       rk_ref, seg_ref, tri_ref, q_ref, y0_ref, a_ref, d_ref, g_ref, bonus_ref):
    C, W, N = RWKV_CHUNK, RWKV_WIDTH, HEAD_DIM
    c = pl.program_id(1)
    x = rw_ref[0]
    R = x.shape[0]
    chunk_rows = [slice(ck * C, (ck + 1) * C) for ck in range(R // C)]
    row = lax.broadcasted_iota(jnp.int32, (R, 1), 0)
    last_prev = jnp.where(c == 0, 0.0, prev_ref[0, SUBLANES - 1:SUBLANES, :])
    x_prev = jnp.where(row == 0, last_prev, pltpu.roll(x, 1, axis=0))
    xs = x + (x_prev - x) * mu_ref[...]
    r, k, v = xs[:, 0:W], xs[:, W:2 * W], xs[:, 2 * W:3 * W]
    o = 3 * W
    wd, ad, gd = xs[:, o:o + DECAY_LORA], xs[:, o + DECAY_LORA:o + DECAY_LORA + AAA_LORA], \
        xs[:, o + DECAY_LORA + AAA_LORA:]
    w_log = -jax.nn.softplus(-(w0_ref[...] + _dot(jnp.tanh(wd).astype(BF16), wup_ref[...]))) - 0.5
    lw = -jnp.exp(w_log)
    lr = jax.nn.sigmoid(a0_ref[...] + _dot(ad.astype(BF16), aup_ref[...]))
    g_ref[0] = _dot(jax.nn.sigmoid(gd).astype(BF16), gup_ref[...])
    kk = k * kk_ref[...]
    def head_sums(z):
        return jnp.concatenate([_dot_f32_lhs(z[:, t * LANES:(t + 1) * LANES], seg_ref[...])
                                for t in range(W // LANES)], axis=1)
    kk = kk * lax.rsqrt(jnp.maximum(head_sums(kk * kk), 1e-24))
    k = k * (1.0 + (lr - 1.0) * ka_ref[...])
    bonus_ref[0] = head_sums(r * k * rk_ref[...]) * v
    a_vec, b_vec = -kk, kk * lr

    L = jnp.concatenate([_dot_f32_rhs(tri_ref[...], lw[rs]) for rs in chunk_rows], axis=0)
    L_end = jnp.concatenate([jnp.broadcast_to(L[rs.stop - 1:rs.stop, :], (C, W)) for rs in chunk_rows], axis=0)
    e_neg = jnp.exp(-L)
    e_rem = jnp.exp(L_end - L)
    At, Bt, Kt, Rt = a_vec * jnp.exp(L - lw), b_vec * e_neg, k * e_neg, r * jnp.exp(L)
    Bg, Kg = b_vec * e_rem, k * e_rem
    decay_end = [jnp.exp(L[rs.stop - 1:rs.stop, :]) for rs in chunk_rows]

    HG = RWKV_GROUP_HEADS
    GL = HG * N
    assert C == N and C & (C - 1) == 0
    blk_of = lambda idx: jnp.right_shift(idx, int(math.log2(C)))
    same_head = (blk_of(lax.broadcasted_iota(jnp.int32, (HG * C, GL), 0))
                 == blk_of(lax.broadcasted_iota(jnp.int32, (HG * C, GL), 1)))
    bf = lambda z: z.astype(BF16)
    block_diag = lambda y: jnp.where(same_head, jnp.concatenate([bf(y)] * HG, axis=0), 0.0)
    ri = lax.broadcasted_iota(jnp.int32, (C, GL), 0)
    cj = jnp.bitwise_and(lax.broadcasted_iota(jnp.int32, (C, GL), 1), C - 1)
    strict, incl = ri > cj, ri >= cj
    eye_c = (ri == cj).astype(F32)
    pr = lax.broadcasted_iota(jnp.int32, (N, 2 * N), 0)
    pc = lax.broadcasted_iota(jnp.int32, (N, 2 * N), 1)
    first_head, pair_eye = pc < N, (pr == jnp.bitwise_and(pc, N - 1)).astype(F32)

    def same_head_blocks(z):
        return jnp.where(first_head, z[0:N], z[N:2 * N])
    items = [(ck, gp) for ck in range(len(chunk_rows)) for gp in range(RWKV_HEADS // HG)]
    sl = [(chunk_rows[ck], slice(gp * GL, (gp + 1) * GL)) for ck, gp in items]
    ar = [bf(jnp.concatenate([At[s], Rt[s]], axis=0)) for s in sl]
    zb = [_dot_nt(a, block_diag(Bt[s])) for a, s in zip(ar, sl)]
    zk = [_dot_nt(a, block_diag(Kt[s])) for a, s in zip(ar, sl)]
    n_mat = [jnp.where(strict, z[0:C], 0.0) for z in zb]
    m_mat = [jnp.where(strict, z[0:C], 0.0) for z in zk]
    v_bd = [block_diag(v[s]) for s in sl]
    mv = [_dot(bf(m), vb) for m, vb in zip(m_mat, v_bd)]
    t_inv, n_pow = [eye_c + n for n in n_mat], n_mat
    for _ in range(int(math.log2(C)) - 1):
        n_pow = [_dot(bf(n), block_diag(n)) for n in n_pow]
        t_inv = [t + _dot(bf(t), block_diag(n)) for t, n in zip(t_inv, n_pow)]
    t_bf = [bf(t) for t in t_inv]
    ta = [_dot(t, block_diag(At[s])) for t, s in zip(t_bf, sl)]
    g0 = [_dot(t, block_diag(m)) for t, m in zip(t_bf, mv)]
    p_mat = [bf(jnp.where(incl, z[C:], 0.0)) for z in zb]
    pk_mat = [bf(jnp.where(incl, z[C:], 0.0)) for z in zk]
    q_out = [Rt[s] + _dot(p, block_diag(t)) for p, t, s in zip(p_mat, ta, sl)]
    y0_out = [_dot(p, block_diag(g)) + _dot(pk, vb) for p, g, pk, vb in zip(p_mat, g0, pk_mat, v_bd)]
    bgT = [bf(Bg[s].T) for s in sl]
    a_full = [_dot(b, bf(t)) for b, t in zip(bgT, ta)]
    d_full = [_dot(b, bf(g)) + _dot(bf(Kg[s].T), bf(v[s])) for b, g, s in zip(bgT, g0, sl)]
    for n, (ck, gp) in enumerate(items):
        rows, lanes = sl[n]
        q_ref[0, rows, lanes] = q_out[n].astype(q_ref.dtype)
        y0_ref[0, rows, lanes] = y0_out[n]
        for pp in range(HG // 2):
            blk = slice(pp * 2 * N, (pp + 1) * 2 * N)
            pair_lanes = slice(lanes.start + blk.start, lanes.start + blk.stop)
            a_ref[0, ck, gp * (HG // 2) + pp] = (same_head_blocks(a_full[n][blk, blk])
                                                 + pair_eye * decay_end[ck][:, pair_lanes])
            d_ref[0, ck, gp * (HG // 2) + pp] = same_head_blocks(d_full[n][blk, blk])


def _rwkv_chunks(rw, mu, w0, w_up, a0, a_up, g_up, k_k, k_a, r_k):
    B, T, cols = rw.shape
    C, W, H, N = RWKV_CHUNK, RWKV_WIDTH, RWKV_HEADS, HEAD_DIM
    nc = T // C
    S = RWKV_STEP_CHUNKS
    R = S * C
    assert nc % S == 0
    seg = jnp.asarray(np.kron(np.eye(LANES // N), np.ones((N, N))), BF16)
    tri = jnp.asarray(np.tril(np.ones((C, C))), BF16)
    row = lambda z: z.reshape(1, -1)
    const = lambda b, c: (0, 0)
    vec = pl.BlockSpec((1, W), const)
    tok = pl.BlockSpec((1, R, W), lambda b, c: (b, c, 0))
    mat = pl.BlockSpec((1, S, H // 2, N, 2 * N), lambda b, c: (b, c, 0, 0, 0))
    return pl.pallas_call(
        _rwkv_chunk_kernel,
        grid=(B, nc // S),
        in_specs=[pl.BlockSpec((1, R, cols), lambda b, c: (b, c, 0)),
                  pl.BlockSpec((1, SUBLANES, cols), lambda b, c: (b, jnp.maximum(c * (R // SUBLANES) - 1, 0), 0)),
                  pl.BlockSpec((1, cols), const), vec,
                  pl.BlockSpec((DECAY_LORA, W), const), vec,
                  pl.BlockSpec((AAA_LORA, W), const),
                  pl.BlockSpec((GATE_LORA, W), const), vec, vec, vec,
                  pl.BlockSpec((LANES, LANES), const), pl.BlockSpec((C, C), const)],
        out_specs=[tok, tok, mat, mat, tok, tok],
        out_shape=[jax.ShapeDtypeStruct((B, T, W), BF16), jax.ShapeDtypeStruct((B, T, W), F32),
                   jax.ShapeDtypeStruct((B, nc, H // 2, N, 2 * N), F32),
                   jax.ShapeDtypeStruct((B, nc, H // 2, N, 2 * N), F32),
                   jax.ShapeDtypeStruct((B, T, W), F32), jax.ShapeDtypeStruct((B, T, W), F32)],
        compiler_params=_params("parallel", "parallel"), name="rwkv_chunks",
    )(rw, rw, row(mu), row(w0), w_up.astype(BF16), row(a0), a_up.astype(BF16), g_up.astype(BF16),
      row(k_k), row(k_a), row(r_k), seg, tri)


def _pair_block_diag(x):
    first_head = lax.broadcasted_iota(jnp.int32, x.shape, 1) < x.shape[0]
    zero = jnp.zeros_like(x)
    return jnp.concatenate([jnp.where(first_head, x, zero), jnp.where(first_head, zero, x)], axis=0)


def _pair_side_by_side(z):
    n = z.shape[0] // 2
    return jnp.where(lax.broadcasted_iota(jnp.int32, (n, 2 * n), 1) < n, z[0:n], z[n:])


def _rwkv_state_kernel(a_ref, d_ref, h_ref, h_sc):
    @pl.when(pl.program_id(0) == 0)
    def _():
        h_sc[...] = jnp.zeros(h_sc.shape, F32)

    items = [(b, pair) for b in range(a_ref.shape[0]) for pair in range(a_ref.shape[2])]
    states = [h_sc[b, pair] for b, pair in items]
    for ck in range(a_ref.shape[1]):
        for (b, pair), st in zip(items, states):
            h_ref[b, ck, pair] = _pair_side_by_side(st).astype(h_ref.dtype)
        states = [_dot_hi_lo(_pair_block_diag(a_ref[b, ck, pair]), st) + _pair_block_diag(d_ref[b, ck, pair])
                  for (b, pair), st in zip(items, states)]
    for (b, pair), st in zip(items, states):
        h_sc[b, pair] = st


def _rwkv_readout_kernel(h_ref, q_ref, y0_ref, g_ref, bonus_ref, lw_ref, lb_ref, seg_ref, o_ref):
    C = RWKV_CHUNK
    inv_n = 1.0 / HEAD_DIM
    items = [(b, ck, pair, slice(ck * C, (ck + 1) * C), slice(pair * LANES, (pair + 1) * LANES))
             for b in range(h_ref.shape[0]) for ck in range(h_ref.shape[1]) for pair in range(h_ref.shape[2])]
    ys = [_dot(q_ref[b, rows, lanes], _pair_block_diag(h_ref[b, ck, pair])) + y0_ref[b, rows, lanes]
          for b, ck, pair, rows, lanes in items]
    means = [_dot_f32_lhs(y, seg_ref[...], 2) * inv_n for y in ys]
    cen = [y - mean for y, mean in zip(ys, means)]
    var = [_dot_f32_lhs(jnp.square(c), seg_ref[...], 2) * inv_n for c in cen]
    for (b, _, _, rows, lanes), c, v in zip(items, cen, var):
        yn = c * lax.rsqrt(v + LNX_EPS) * lw_ref[:, lanes] + lb_ref[:, lanes]
        o_ref[b, rows, lanes] = (yn + bonus_ref[b, rows, lanes]) * g_ref[b, rows, lanes]


def _rwkv_scan(A, D, Q, Y0, g, bonus, lnx_w, lnx_b):
    B, nc, P, N, N2 = A.shape
    T, W, C = Q.shape[1], Q.shape[2], RWKV_CHUNK
    S, SR = RWKV_SCAN_STEP_CHUNKS, RWKV_READOUT_STEP_CHUNKS
    assert nc % S == 0 and nc % SR == 0
    mat = lambda s: pl.BlockSpec((B, s, P, N, N2), lambda c: (0, c, 0, 0, 0))
    h_start = pl.pallas_call(
        _rwkv_state_kernel,
        grid=(nc // S,),
        in_specs=[mat(S), mat(S)],
        out_specs=mat(S),
        out_shape=jax.ShapeDtypeStruct((B, nc, P, N, N2), BF16),
        scratch_shapes=[pltpu.VMEM((B, P, N2, N2), F32)],
        compiler_params=_params("arbitrary"), name="rwkv_state",
    )(A, D)
    tok = pl.BlockSpec((B, SR * C, W), lambda c: (0, c, 0))
    vec = pl.BlockSpec((1, W), lambda c: (0, 0))
    seg = jnp.asarray(np.kron(np.eye(2), np.ones((HEAD_DIM, HEAD_DIM))), BF16)
    return pl.pallas_call(
        _rwkv_readout_kernel,
        grid=(nc // SR,),
        in_specs=[mat(SR), tok, tok, tok, tok, vec, vec, pl.BlockSpec((N2, N2), lambda c: (0, 0))],
        out_specs=tok,
        out_shape=jax.ShapeDtypeStruct((B, T, W), F32),
        compiler_params=_params("parallel"), name="rwkv_readout",
    )(h_start, Q, Y0, g, bonus, lnx_w.reshape(1, W), lnx_b.reshape(1, W), seg)


def _mix_xattn_kernel(x_ref, on_ref, or_ref, wo1_ref, wo2_ref, gx_ref, wq_ref, k_ref, v_ref, wo_ref, o_ref):
    x1 = x_ref[0] + _dot(on_ref[0].astype(BF16), wo1_ref[...]) + _dot(or_ref[0].astype(BF16), wo2_ref[...])
    q = _dot(_rms(x1, gx_ref[...]).astype(BF16), wq_ref[...])
    dh = q.shape[-1] // XATTN_HEADS
    qbf = (q * (dh ** -0.5)).astype(BF16)
    outs = []
    for h in range(XATTN_HEADS):
        hs = slice(h * dh, (h + 1) * dh)
        s = _dot_nt(qbf[:, hs], k_ref[0, :, hs])
        e = jnp.exp(s - jnp.max(s, axis=-1, keepdims=True))
        p = e / jnp.sum(e, axis=-1, keepdims=True)
        outs.append(_dot(p.astype(BF16), v_ref[0, :, hs]))
    o = jnp.concatenate(outs, axis=1).astype(BF16)
    o_ref[0] = x1 + _dot(o, wo_ref[...])


def _mix_xattn(x, o_nsa, o_rwkv, w_out, norm_x_g, w_q, mem_k, mem_v, w_o):
    B, T, D = x.shape
    M = mem_k.shape[1]
    tm = min(ROW_TILE, T)
    wo1, wo2 = w_out[:NSA_WIDTH].astype(BF16), w_out[NSA_WIDTH:].astype(BF16)
    const = lambda b, i: (0, 0)
    tile = lambda w: pl.BlockSpec((1, tm, w), lambda b, i: (b, i, 0))
    return pl.pallas_call(
        _mix_xattn_kernel,
        grid=(B, T // tm),
        in_specs=[tile(D), tile(NSA_WIDTH), tile(RWKV_WIDTH),
                  pl.BlockSpec(wo1.shape, const), pl.BlockSpec(wo2.shape, const),
                  pl.BlockSpec((1, D), const), pl.BlockSpec((D, D), const),
                  pl.BlockSpec((1, M, D), lambda b, i: (b, 0, 0)), pl.BlockSpec((1, M, D), lambda b, i: (b, 0, 0)),
                  pl.BlockSpec((D, D), const)],
        out_specs=tile(D),
        out_shape=jax.ShapeDtypeStruct((B, T, D), F32),
        compiler_params=_params("parallel", "parallel"), name="mix_xattn",
    )(x, o_nsa, o_rwkv, wo1, wo2, norm_x_g.reshape(1, D), w_q.astype(BF16), mem_k, mem_v, w_o.astype(BF16))


def _ffn_kernel(x_ref, g_ref, wg_ref, wu_ref, wd_ref, gf_ref, o_ref, *, final_norm):
    x = x_ref[...]
    h = _rms(x, g_ref[...]).astype(BF16)
    act = (jax.nn.silu(_dot(h, wg_ref[...])) * _dot(h, wu_ref[...])).astype(BF16)
    y = x + _dot(act, wd_ref[...])
    o_ref[...] = _rms(y, gf_ref[...]) if final_norm else y


def _ffn(x, norm_g, w_gate, w_up, w_down, final_g, final_norm):
    R, D = x.shape
    F = w_gate.shape[1]
    tm = min(FFN_ROW_TILE, R)
    const = lambda i: (0, 0)
    resident = lambda shape: pl.BlockSpec(shape, const, pipeline_mode=pl.Buffered(1))
    return pl.pallas_call(
        functools.partial(_ffn_kernel, final_norm=final_norm),
        grid=(R // tm,),
        in_specs=[pl.BlockSpec((tm, D), lambda i: (i, 0)), pl.BlockSpec((1, D), const),
                  resident((D, F)), resident((D, F)), resident((F, D)), pl.BlockSpec((1, D), const)],
        out_specs=pl.BlockSpec((tm, D), lambda i: (i, 0)),
        out_shape=jax.ShapeDtypeStruct((R, D), F32),
        compiler_params=_params("parallel"), name="ffn",
    )(x, norm_g.reshape(1, D), w_gate.astype(BF16), w_up.astype(BF16), w_down.astype(BF16),
      final_g.reshape(1, D))


def _overlap_matrix(n_cmp_pad, n_sel):
    c = np.arange(n_cmp_pad)[:, None] * CMP_STRIDE
    s = np.arange(n_sel)[None, :] * SEL_BLOCK
    return ((c <= s + SEL_BLOCK - 1) & (c + CMP_LEN - 1 >= s)).astype(np.float32)


def _layer(x, mem, rel_bias, final_g, is_last, norm_mix_g, w_in, nsa_gate_b, cmp_pe_k, cmp_pe_v,
           cmp_k_w1, cmp_k_b1, cmp_k_w2, cmp_v_w1, cmp_v_b1, cmp_v_w2,
           rwkv_mu, rwkv_w0, rwkv_w_up, rwkv_a0, rwkv_a_up, rwkv_g_up,
           rwkv_k_k, rwkv_k_a, rwkv_r_k, rwkv_lnx_w, rwkv_lnx_b, w_out,
           norm_x_g, norm_mem_g, w_q_x, w_kv_x, w_o_x, norm_ffn_g, w_gate, w_up, w_down):
    B, T, D = x.shape
    Hkv, G, dh = NSA_KV_HEADS, NSA_GROUP, HEAD_DIM
    q, kvc, kaug, vsT, vwT, gates, rw = _proj_in(x, norm_mix_g, w_in, nsa_gate_b)

    n16 = T // CMP_STRIDE
    kvc = kvc.reshape(B, T, 2 * KV_WIDTH)
    kc = _compress(kvc, 0, cmp_pe_k, cmp_k_w1, cmp_k_b1, cmp_k_w2, False)
    vcT = _compress(kvc, 1, cmp_pe_v, cmp_v_w1, cmp_v_b1, cmp_v_w2, True)
    bias_tiles, cmp_bias = _bias_tiles(rel_bias, n16)
    o_nsa = _nsa_attention(q, kc, vcT, kaug.reshape(B, T, 2 * Hkv * LANES), vsT, vwT,
                           gates, jnp.asarray(_overlap_matrix(n16, T // SEL_BLOCK).T, BF16), bias_tiles, cmp_bias)

    Q, Y0, A, Dm, g, bonus = _rwkv_chunks(rw.reshape(B, T, RWKV_COLS), rwkv_mu, rwkv_w0, rwkv_w_up, rwkv_a0,
                                          rwkv_a_up, rwkv_g_up, rwkv_k_k, rwkv_k_a, rwkv_r_k.reshape(-1))
    o_rwkv = _rwkv_scan(A, Dm, Q, Y0, g, bonus, rwkv_lnx_w, rwkv_lnx_b)

    M = mem.shape[1]
    (kv_mem,) = _norm_matmul(mem.reshape(B * M, D), norm_mem_g, [w_kv_x.astype(BF16)], [None], [BF16], ROW_TILE)
    kv_mem = kv_mem.reshape(B, M, 2 * D)
    x = _mix_xattn(x, o_nsa, o_rwkv, w_out, norm_x_g, w_q_x, kv_mem[..., :D], kv_mem[..., D:], w_o_x)
    x = _ffn(x.reshape(B * T, D), norm_ffn_g, w_gate, w_up, w_down, final_g, is_last)
    return x.reshape(B, T, D)


def kernel(x, mem, rel_bias, norm_f_g, norm_mix_g, w_in, nsa_gate_b, cmp_pe_k, cmp_pe_v, cmp_k_w1, cmp_k_b1, cmp_k_w2, cmp_v_w1, cmp_v_b1, cmp_v_w2, rwkv_mu, rwkv_w0, rwkv_w_up, rwkv_a0, rwkv_a_up, rwkv_g_up, rwkv_k_k, rwkv_k_a, rwkv_r_k, rwkv_lnx_w, rwkv_lnx_b, w_out, norm_x_g, norm_mem_g, w_q_x, w_kv_x, w_o_x, norm_ffn_g, w_gate, w_up, w_down):
    stacked = (norm_mix_g, w_in, nsa_gate_b, cmp_pe_k, cmp_pe_v, cmp_k_w1, cmp_k_b1, cmp_k_w2, cmp_v_w1,
               cmp_v_b1, cmp_v_w2, rwkv_mu, rwkv_w0, rwkv_w_up, rwkv_a0, rwkv_a_up, rwkv_g_up, rwkv_k_k,
               rwkv_k_a, rwkv_r_k, rwkv_lnx_w, rwkv_lnx_b, w_out, norm_x_g, norm_mem_g, w_q_x, w_kv_x, w_o_x,
               norm_ffn_g, w_gate, w_up, w_down)
    depth = w_in.shape[0]
    for l in range(depth):
        x = _layer(x, mem, rel_bias, norm_f_g, l == depth - 1, *[p[l] for p in stacked])
    return x
```

```python
import functools
import math

import numpy as np
import jax
import jax.numpy as jnp
from jax import lax
from jax.experimental import pallas as pl
from jax.experimental.pallas import tpu as pltpu

F32 = jnp.float32
BF16 = jnp.bfloat16

LANES = 128
SUBLANES = 8
BF16_ROWS = 16
VMEM_LIMIT_BYTES = 56 * 1024 * 1024

HEAD_DIM = 64
NSA_HEADS = 8
NSA_KV_HEADS = 2
NSA_GROUP = NSA_HEADS // NSA_KV_HEADS
NSA_WIDTH = NSA_HEADS * HEAD_DIM
KV_WIDTH = NSA_KV_HEADS * HEAD_DIM
RWKV_HEADS = 8
RWKV_WIDTH = RWKV_HEADS * HEAD_DIM
CMP_LEN = 32
CMP_STRIDE = 16
SEL_BLOCK = 64
SEL_SHIFT = 6
SEL_TOP = 16
WINDOW = 512
Q_BLOCK = 128
DECAY_LORA = 64
AAA_LORA = 64
GATE_LORA = 128
N_BUCKETS = 32
MAX_DISTANCE = 2048
XATTN_HEADS = 4
RMS_EPS = 1e-6
LNX_EPS = 64e-5
FORCE_SCORE = 1e4
NEG_SCORE = -1e9
MASK_SCORE = -1e30
LOG2E = math.log2(math.e)
RWKV_COLS = 3 * RWKV_WIDTH + DECAY_LORA + AAA_LORA + GATE_LORA
NSA_COLS = NSA_WIDTH + 6 * KV_WIDTH + 3 * NSA_HEADS

KEY_CHUNK = 128
RWKV_CHUNK = 64
RWKV_SCAN_STEP_CHUNKS = 8
RWKV_READOUT_STEP_CHUNKS = 4
RWKV_GROUP_HEADS = 2
RWKV_STEP_CHUNKS = 8
ROW_TILE = 512
FFN_ROW_TILE = 512


def _t5_thresholds():
    d = np.arange(0, 2 * MAX_DISTANCE, dtype=np.int64)
    max_exact = N_BUCKETS // 2
    nf = np.maximum(d, 1).astype(np.float32)
    large = max_exact + (np.log(nf / np.float32(max_exact)) / np.float32(math.log(MAX_DISTANCE / max_exact))
                         * np.float32(N_BUCKETS - max_exact)).astype(np.int32)
    bucket = np.where(d < max_exact, d, np.minimum(large, N_BUCKETS - 1))
    return [int(np.argmax(bucket >= k)) for k in range(N_BUCKETS)]


T5_THRESHOLDS = _t5_thresholds()
N_BIAS_TILES = -(-(T5_THRESHOLDS[-1] + KEY_CHUNK) // KEY_CHUNK) + 1
TILE_MASKED = N_BIAS_TILES
TILE_WINDOW_EDGE = N_BIAS_TILES + 1
N_ALL_TILES = N_BIAS_TILES + 2
SEL_STEP_BLOCKS = 16
SEL_STAGE_TILES = 4
SEL_GROUP_STAGES = SEL_STEP_BLOCKS * SEL_BLOCK // (SEL_STAGE_TILES * KEY_CHUNK)
NSA_STEP_QBLOCKS = 4
V_ROWS = HEAD_DIM + BF16_ROWS
GATE_ROWS = 16


def _params(*semantics):
    return pltpu.CompilerParams(dimension_semantics=semantics, vmem_limit_bytes=VMEM_LIMIT_BYTES)


def _rms(x, g):
    return x * lax.rsqrt(jnp.mean(x * x, axis=-1, keepdims=True) + RMS_EPS) * g


def _dot(a, b, **kw):
    return jnp.dot(a, b, preferred_element_type=F32, **kw)


def _split3(x):
    hi = x.astype(BF16)
    r1 = x - hi.astype(F32)
    mid = r1.astype(BF16)
    lo = (r1 - mid.astype(F32)).astype(BF16)
    return hi, mid, lo


def _dot_f32_lhs(x, w01, pieces=3):
    w = w01.astype(BF16)
    hi, mid, lo = _split3(x)
    return _dot(hi, w) + (_dot(mid, w) + _dot(lo, w) if pieces == 3 else _dot(mid, w))


def _dot_f32_rhs(w01, x):
    w = w01.astype(BF16)
    hi, mid, lo = _split3(x)
    return _dot(w, hi) + (_dot(w, mid) + _dot(w, lo))


def _dot_hi_lo(a, b):
    a_hi = a.astype(BF16)
    a_lo = (a - a_hi.astype(F32)).astype(BF16)
    b_hi = b.astype(BF16)
    b_lo = (b - b_hi.astype(F32)).astype(BF16)
    return _dot(a_hi, b_hi) + (_dot(a_hi, b_lo) + _dot(a_lo, b_hi))


def _dot_nt(a, b, **kw):
    return lax.dot_general(a, b, (((1,), (1,)), ((), ())), preferred_element_type=F32, **kw)


def _norm_matmul_kernel(x_ref, g_ref, *refs, nseg, bias_flags):
    nb = sum(bias_flags)
    w_refs, b_refs, o_refs = refs[:nseg], refs[nseg:nseg + nb], refs[nseg + nb:]
    xn = _rms(x_ref[...], g_ref[...]).astype(BF16)
    bi = 0
    for s in range(nseg):
        y = _dot(xn, w_refs[s][...])
        if bias_flags[s]:
            y = y + b_refs[bi][...]
            bi += 1
        o_refs[s][...] = y.astype(o_refs[s].dtype)


def _norm_matmul(x, g, weights, biases, out_dtypes, row_tile):
    R, D = x.shape
    tm = min(row_tile, R)
    assert R % tm == 0
    nseg = len(weights)
    bias_flags = tuple(b is not None for b in biases)
    const = lambda i: (0, 0)
    in_specs = [pl.BlockSpec((tm, D), lambda i: (i, 0)), pl.BlockSpec((1, D), const)]
    in_specs += [pl.BlockSpec(w.shape, const) for w in weights]
    in_specs += [pl.BlockSpec((1, b.shape[-1]), const) for b in biases if b is not None]
    out_specs = [pl.BlockSpec((tm, w.shape[1]), lambda i: (i, 0)) for w in weights]
    out_shape = [jax.ShapeDtypeStruct((R, w.shape[1]), dt) for w, dt in zip(weights, out_dtypes)]
    return pl.pallas_call(
        functools.partial(_norm_matmul_kernel, nseg=nseg, bias_flags=bias_flags),
        grid=(R // tm,), in_specs=in_specs, out_specs=out_specs, out_shape=out_shape,
        compiler_params=_params("parallel"), name="norm_matmul",
    )(x, g.reshape(1, D), *weights, *[b.reshape(1, -1) for b in biases if b is not None])


def _proj_in_kernel(x_ref, g_ref, wq_ref, wc_ref, wk_ref, wvT_ref, wgT_ref, bg_ref, wr_ref,
                    q_ref, kvc_ref, kaug_ref, vsT_ref, vwT_ref, gate_ref, rw_ref, *, seq_len):
    tm = x_ref.shape[0]
    xn = _rms(x_ref[...], g_ref[...]).astype(BF16)
    qT = (_dot_nt(wq_ref[...], xn) * (HEAD_DIM ** -0.5 * LOG2E)).astype(BF16)
    for j in range(tm // Q_BLOCK):
        for hg in range(NSA_HEADS):
            h, g = divmod(hg, NSA_GROUP)
            q_ref[0, j, h, :, g * Q_BLOCK:(g + 1) * Q_BLOCK] = qT[hg * HEAD_DIM:(hg + 1) * HEAD_DIM,
                                                                  j * Q_BLOCK:(j + 1) * Q_BLOCK]
    kvc_ref[...] = _dot(xn, wc_ref[...])
    rw_ref[...] = _dot(xn, wr_ref[...])
    k_all = _dot(xn, wk_ref[...])
    tok = lax.rem(pl.program_id(0) * tm, seq_len) + lax.broadcasted_iota(jnp.int32, k_all.shape, 0)
    lane = lax.broadcasted_iota(jnp.int32, k_all.shape, 1)
    blk = jnp.bitwise_and(jnp.right_shift(tok, SEL_SHIFT), SEL_STEP_BLOCKS - 1)
    hot = (jnp.bitwise_and(lane, LANES - 1) == HEAD_DIM + blk) & (lane < NSA_KV_HEADS * LANES)
    kaug_ref[...] = jnp.where(hot, 1.0, k_all).astype(BF16)
    vT = _dot_nt(wvT_ref[...], xn)
    row = lax.broadcasted_iota(jnp.int32, vT.shape, 0)
    ones_row = row == HEAD_DIM
    for grp in range(1, 2 * NSA_KV_HEADS):
        ones_row = ones_row | (row == grp * V_ROWS + HEAD_DIM)
    vT = jnp.where(ones_row, 1.0, vT).astype(BF16)
    half = NSA_KV_HEADS * V_ROWS
    stage_keys = vsT_ref.shape[3]
    for c in range(tm // stage_keys):
        vsT_ref[0, c] = vT[0:half, c * stage_keys:(c + 1) * stage_keys]
    for c in range(tm // KEY_CHUNK):
        vwT_ref[0, c] = vT[half:, c * KEY_CHUNK:(c + 1) * KEY_CHUNK]
    gT = _dot_nt(wgT_ref[...], xn) + bg_ref[...]
    for j in range(tm // Q_BLOCK):
        for h in range(NSA_KV_HEADS):
            gate_ref[0, j, h] = gT[h * GATE_ROWS:(h + 1) * GATE_ROWS, j * Q_BLOCK:(j + 1) * Q_BLOCK]


def _proj_in(x, norm_g, w_in, gate_b):
    B, T, D = x.shape
    Hkv, G, dh = NSA_KV_HEADS, NSA_GROUP, HEAD_DIM
    tm = ROW_TILE
    stage_keys = SEL_STAGE_TILES * KEY_CHUNK
    assert T % tm == 0 and tm % Q_BLOCK == 0 and tm % stage_keys == 0
    kv0 = NSA_WIDTH
    g0 = kv0 + 6 * KV_WIDTH
    stream = lambda s: w_in[:, kv0 + s * KV_WIDTH:kv0 + (s + 1) * KV_WIDTH].reshape(D, Hkv, dh)
    pad_cols = lambda w: jnp.pad(w, ((0, 0), (0, 0), (0, LANES - dh))).reshape(D, Hkv * LANES)
    pad_rows = lambda w: jnp.pad(w.transpose(1, 2, 0), ((0, 0), (0, V_ROWS - dh), (0, 0))).reshape(Hkv * V_ROWS, D)
    w_k = jnp.concatenate([pad_cols(stream(2)), pad_cols(stream(4))], axis=1)
    w_vT = jnp.concatenate([pad_rows(stream(3)), pad_rows(stream(5))], axis=0)
    reorder = lambda a: a.reshape(-1, Hkv, G, 3).transpose(1, 3, 2, 0).reshape(Hkv, 3 * G, -1)
    pad_gate = lambda a: jnp.pad(a, ((0, 0), (0, GATE_ROWS - 3 * G), (0, 0))).reshape(Hkv * GATE_ROWS, -1)
    w_gT = pad_gate(reorder(w_in[:, g0:NSA_COLS]))
    b_g = pad_gate(reorder(gate_b.reshape(1, -1)))
    weights = [w_in[:, :kv0].T, w_in[:, kv0:kv0 + 2 * KV_WIDTH], w_k, w_vT, w_gT]
    weights = [w.astype(BF16) for w in weights] + [b_g, w_in[:, NSA_COLS:].astype(BF16)]
    nt = T // tm
    rows = lambda n: pl.BlockSpec((tm, n), lambda i: (i, 0))
    const = lambda i: (0, 0)
    return pl.pallas_call(
        functools.partial(_proj_in_kernel, seq_len=T),
        grid=(B * nt,),
        in_specs=[rows(D), pl.BlockSpec((1, D), const)] + [pl.BlockSpec(w.shape, const) for w in weights],
        out_specs=[pl.BlockSpec((1, tm // Q_BLOCK, Hkv, dh, G * Q_BLOCK), lambda i: (i // nt, i % nt, 0, 0, 0)),
                   rows(2 * KV_WIDTH), rows(2 * Hkv * LANES),
                   pl.BlockSpec((1, tm // stage_keys, Hkv * V_ROWS, stage_keys), lambda i: (i // nt, i % nt, 0, 0)),
                   pl.BlockSpec((1, tm // KEY_CHUNK, Hkv * V_ROWS, KEY_CHUNK), lambda i: (i // nt, i % nt, 0, 0)),
                   pl.BlockSpec((1, tm // Q_BLOCK, Hkv, GATE_ROWS, Q_BLOCK), lambda i: (i // nt, i % nt, 0, 0, 0)),
                   rows(RWKV_COLS)],
        out_shape=[jax.ShapeDtypeStruct((B, T // Q_BLOCK, Hkv, dh, G * Q_BLOCK), BF16),
                   jax.ShapeDtypeStruct((B * T, 2 * KV_WIDTH), F32),
                   jax.ShapeDtypeStruct((B * T, 2 * Hkv * LANES), BF16),
                   jax.ShapeDtypeStruct((B, T // stage_keys, Hkv * V_ROWS, stage_keys), BF16),
                   jax.ShapeDtypeStruct((B, T // KEY_CHUNK, Hkv * V_ROWS, KEY_CHUNK), BF16),
                   jax.ShapeDtypeStruct((B, T // Q_BLOCK, Hkv, GATE_ROWS, Q_BLOCK), F32),
                   jax.ShapeDtypeStruct((B * T, RWKV_COLS), F32)],
        compiler_params=_params("parallel"), name="proj_in",
    )(x.reshape(B * T, D), norm_g.reshape(1, D), *weights)


def _compress_kernel(x_ref, pe_ref, w1_ref, b1_ref, w2_ref, o_ref, *, transpose_out):
    n16 = x_ref.shape[1] // CMP_STRIDE
    hidden = w1_ref.shape[2] // NSA_KV_HEADS
    lo = jnp.zeros((n16, w1_ref.shape[2]), F32)
    hi = jnp.zeros((n16, w1_ref.shape[2]), F32)
    for l in range(CMP_STRIDE):
        rows = x_ref[0, pl.ds(l, n16, stride=CMP_STRIDE), :]
        lo = lo + _dot((rows + pe_ref[l:l + 1, :]).astype(BF16), w1_ref[l])
        hi = hi + _dot((rows + pe_ref[CMP_STRIDE + l:CMP_STRIDE + l + 1, :]).astype(BF16), w1_ref[CMP_STRIDE + l])
    h = lo + pltpu.roll(hi, n16 - 1, axis=0) + b1_ref[...]
    h = jax.nn.gelu(h).astype(BF16)
    for hkv in range(NSA_KV_HEADS):
        hh = h[:, hkv * hidden:(hkv + 1) * hidden]
        if transpose_out:
            o_ref[0, hkv] = _dot_nt(w2_ref[...], hh).astype(o_ref.dtype)
        else:
            o_ref[0, hkv] = _dot(hh, w2_ref[...]).astype(o_ref.dtype)


def _compress(kvc, stream, pe, w1, b1, w2, transpose_out):
    B, T, _ = kvc.shape
    H, dh = NSA_KV_HEADS, HEAD_DIM
    n16 = T // CMP_STRIDE
    hidden = w1.shape[1]
    eye = jnp.eye(H, dtype=w1.dtype)
    w1_bd = jnp.einsum('ldn,hg->lhdgn', w1.reshape(CMP_LEN, dh, hidden), eye).reshape(CMP_LEN, H * dh, H * hidden)
    w2b = (w2.T if transpose_out else w2).astype(BF16)
    oshape = (B, H, dh, n16) if transpose_out else (B, H, n16, dh)
    return pl.pallas_call(
        functools.partial(_compress_kernel, transpose_out=transpose_out),
        grid=(B,),
        in_specs=[pl.BlockSpec((1, T, H * dh), lambda b: (b, 0, stream)),
                  pl.BlockSpec((CMP_LEN, H * dh), lambda b: (0, 0)),
                  pl.BlockSpec(w1_bd.shape, lambda b: (0, 0, 0)),
                  pl.BlockSpec((1, H * hidden), lambda b: (0, 0)),
                  pl.BlockSpec(w2b.shape, lambda b: (0, 0))],
        out_specs=pl.BlockSpec((1,) + oshape[1:], lambda b: (b, 0, 0, 0)),
        out_shape=jax.ShapeDtypeStruct(oshape, BF16),
        compiler_params=_params("parallel"), name="nsa_compress",
    )(kvc, jnp.tile(pe, (1, H)), w1_bd.astype(BF16), jnp.tile(b1.reshape(1, hidden), (1, H)), w2b)


def _bias_of_distance(tab_ref, h, d):
    val = jnp.full(d.shape, tab_ref[h, 0], F32)
    for k in range(1, N_BUCKETS):
        val = jnp.where(d >= T5_THRESHOLDS[k], tab_ref[h, k], val)
    return val * LOG2E


def _bias_tiles_kernel(tab_ref, bt_ref, cb_ref, *, n_cmp_pad):
    hkv = pl.program_id(0)
    j = lax.broadcasted_iota(jnp.int32, (KEY_CHUNK, Q_BLOCK), 0)
    i = lax.broadcasted_iota(jnp.int32, (KEY_CHUNK, Q_BLOCK), 1)
    r2 = lax.broadcasted_iota(jnp.int32, (2 * n_cmp_pad, Q_BLOCK), 0)
    i2 = lax.broadcasted_iota(jnp.int32, (2 * n_cmp_pad, Q_BLOCK), 1)
    l2 = r2 - (n_cmp_pad - KEY_CHUNK)
    d2 = i2 - CMP_STRIDE * l2 + (CMP_STRIDE * KEY_CHUNK - Q_BLOCK - (CMP_LEN - 1))
    hidden2 = (l2 >= KEY_CHUNK) | ((l2 >= 0) & (d2 < 0))
    d2 = jnp.where((l2 >= 0) & (l2 < KEY_CHUNK), d2, 2 * MAX_DISTANCE)
    for g in range(NSA_GROUP):
        h = hkv * NSA_GROUP + g
        lanes = slice(g * Q_BLOCK, (g + 1) * Q_BLOCK)
        for m in range(N_BIAS_TILES):
            tile = _bias_of_distance(tab_ref, h, m * KEY_CHUNK + i - j)
            if m == 0:
                tile = jnp.where(j <= i, tile, MASK_SCORE)
            bt_ref[0, m, :, lanes] = tile
        bt_ref[0, TILE_MASKED, :, lanes] = jnp.full((KEY_CHUNK, Q_BLOCK), MASK_SCORE, F32)
        edge = _bias_of_distance(tab_ref, h, WINDOW + i - j)
        bt_ref[0, TILE_WINDOW_EDGE, :, lanes] = jnp.where(j > i, edge, MASK_SCORE)
        cb_ref[0, :, lanes] = jnp.where(hidden2, MASK_SCORE, _bias_of_distance(tab_ref, h, d2))


def _bias_tiles(rel_bias, n_cmp_pad):
    assert CMP_STRIDE * KEY_CHUNK - Q_BLOCK - (CMP_LEN - 1) >= T5_THRESHOLDS[-1]
    GQ = NSA_GROUP * Q_BLOCK
    return pl.pallas_call(
        functools.partial(_bias_tiles_kernel, n_cmp_pad=n_cmp_pad),
        grid=(NSA_KV_HEADS,),
        in_specs=[pl.BlockSpec(memory_space=pltpu.SMEM)],
        out_specs=[pl.BlockSpec((1, N_ALL_TILES, KEY_CHUNK, GQ), lambda h: (h, 0, 0, 0)),
                   pl.BlockSpec((1, 2 * n_cmp_pad, GQ), lambda h: (h, 0, 0))],
        out_shape=[jax.ShapeDtypeStruct((NSA_KV_HEADS, N_ALL_TILES, KEY_CHUNK, GQ), F32),
                   jax.ShapeDtypeStruct((NSA_KV_HEADS, 2 * n_cmp_pad, GQ), F32)],
        compiler_params=_params("parallel"), name="t5_bias_tiles",
    )(rel_bias.T)


def _nsa_kernel(q_ref, kc_ref, vcT_ref, ks_ref, vsT_ref, kw_ref, vwT_ref, gate_ref, ovT_ref, bt_ref, cb_ref,
                o_ref, acc_sc, qaug_sc, seladd_sc, s0_sc, s1_sc, p0_sc, p1_sc, *, n_sel, n_cmp_pad):
    G, NQ = NSA_GROUP, NSA_STEP_QBLOCKS
    GQ = G * Q_BLOCK
    qbs = [pl.program_id(2) * NQ + x for x in range(NQ)]
    per_block = lambda fn: jnp.concatenate([fn(x) for x in range(NQ)], axis=1)
    tile_g = lambda a: jnp.concatenate([a] * G, axis=1)
    qT = per_block(lambda x: q_ref[0, x, 0])
    qaug_sc[0:HEAD_DIM, :] = qT
    qaug_sc[HEAD_DIM:, :] = jnp.zeros((qaug_sc.shape[0] - HEAD_DIM, NQ * GQ), BF16)
    lane_q = lax.broadcasted_iota(jnp.int32, (1, Q_BLOCK), 1)
    t = per_block(lambda x: qbs[x] * Q_BLOCK + lane_q)

    def bias_tile(dist_of_block):
        def one(x):
            dist = dist_of_block(x)
            return bt_ref[0, jnp.where(dist < 0, TILE_MASKED, jnp.minimum(dist, N_BIAS_TILES - 1))]
        return per_block(one)

    n_back = WINDOW // KEY_CHUNK
    q_win = qaug_sc[...]
    win = {}

    def win_scores(x, back):
        kc = jnp.maximum(qbs[x] - back, 0)
        edge = TILE_WINDOW_EDGE if back == n_back else back
        tile = jnp.where(qbs[x] >= back, edge, TILE_MASKED)
        k_chunk = kw_ref[0, pl.ds(pl.multiple_of(kc * KEY_CHUNK, KEY_CHUNK), KEY_CHUNK), :]
        win["s", x, back] = _dot(k_chunk, q_win[:, x * GQ:(x + 1) * GQ]) + bt_ref[0, tile]
        col_max = jnp.max(win["s", x, back], axis=0, keepdims=True)
        win["m", x] = jnp.maximum(win["m", x], col_max) if ("m", x) in win else col_max

    def win_weights(x, back):
        win["p", x, back] = jnp.exp2(win["s", x, back] - win["m", x]).astype(BF16)

    def win_values(x, back):
        pv = _dot(vwT_ref[0, jnp.maximum(qbs[x] - back, 0)], win["p", x, back])
        win["acc", x] = win["acc", x] + pv if ("acc", x) in win else pv

    backs = list(range(n_back, -1, -1))
    window_work = [functools.partial(fn, x, b) for x in range(NQ)
                   for fn in (win_scores, win_weights, win_values) for b in backs]

    def cmp_branch(n_rows):
        def cmp_bias(x):
            start = pl.multiple_of(n_cmp_pad - (Q_BLOCK // CMP_STRIDE) * (qbs[x] + 1), SUBLANES)
            return cb_ref[0, pl.ds(start, n_rows), :]
        s = _dot(kc_ref[0, 0, 0:n_rows, :], qT) + per_block(cmp_bias)
        e = jnp.exp2(s - jnp.maximum(jnp.max(s, axis=0, keepdims=True), 0.1 * MASK_SCORE))
        p = e * (1.0 / jnp.maximum(jnp.sum(e, axis=0, keepdims=True), 1e-30))
        o_cmp = _dot(vcT_ref[0, 0, :, 0:n_rows], p.astype(BF16))
        psum = per_block(lambda x: sum(p[:, x * GQ + g * Q_BLOCK:x * GQ + (g + 1) * Q_BLOCK] for g in range(G)))
        return o_cmp, _dot_f32_rhs(ovT_ref[:, 0:n_rows], psum)

    half_rows = n_cmp_pad // 2
    n_visible = (Q_BLOCK // CMP_STRIDE) * (qbs[-1] + 1)
    if half_rows % LANES == 0:
        o_c, imp = lax.cond(n_visible <= half_rows, lambda: cmp_branch(half_rows), lambda: cmp_branch(n_cmp_pad))
    else:
        o_c, imp = cmp_branch(n_cmp_pad)
    bj = lax.broadcasted_iota(jnp.int32, (n_sel, NQ * Q_BLOCK), 0)
    cur = jnp.right_shift(t, SEL_SHIFT)
    forced = (bj == 0) | (bj == cur) | (bj == cur - 1)
    valid = bj * SEL_BLOCK <= t
    score = jnp.where(forced, -jnp.inf, jnp.where(valid, imp, NEG_SCORE))
    bjf = bj.astype(F32)
    n_rounds = max(min(SEL_TOP, n_sel) - 3, 0)
    for rnd in range(n_rounds):
        mx = jnp.max(score, axis=0, keepdims=True)
        first = jnp.min(jnp.where(score == mx, bjf, float(n_sel)), axis=0, keepdims=True)
        score = jnp.where(bjf == first, -jnp.inf, score)
        take = -(-len(window_work) // (n_rounds - rnd))
        for piece in window_work[:take]:
            piece()
        window_work = window_work[take:]
    for piece in window_work:
        piece()
    o_w = per_block(lambda x: win["acc", x][0:HEAD_DIM] / jnp.maximum(win["acc", x][HEAD_DIM:HEAD_DIM + 1], 1e-30))
    sel_add = jnp.where(score == -jnp.inf, 0.0, MASK_SCORE)
    seladd_sc[...] = per_block(lambda x: tile_g(sel_add[:, x * Q_BLOCK:(x + 1) * Q_BLOCK]))

    stage_keys = SEL_STAGE_TILES * KEY_CHUNK
    n_stages = qbs[-1] // SEL_STAGE_TILES + 1
    last_stage = ks_ref.shape[1] // stage_keys - 1

    def scores(k, s_buf):
        kk = jnp.minimum(k, last_stage)
        blk0 = pl.multiple_of(kk // SEL_GROUP_STAGES * SEL_STEP_BLOCKS, SEL_STEP_BLOCKS)
        qaug_sc[HEAD_DIM:HEAD_DIM + SEL_STEP_BLOCKS, :] = seladd_sc[pl.ds(blk0, SEL_STEP_BLOCKS), :].astype(BF16)
        k0 = pl.multiple_of(kk * stage_keys, stage_keys)
        bias = jnp.concatenate([bias_tile(lambda x, c=c: qbs[x] - (k * SEL_STAGE_TILES + c))
                                for c in range(SEL_STAGE_TILES)], axis=0)
        s = _dot(ks_ref[0, pl.ds(k0, stage_keys), :], qaug_sc[...]) + bias
        s_buf[...] = s
        return jnp.max(s, axis=0, keepdims=True)

    def weights(m, col_max, s_buf, p_buf):
        m_new = jnp.maximum(m, col_max)
        p_buf[...] = jnp.exp2(s_buf[...] - m_new).astype(BF16)
        return m_new, jnp.exp2(m - m_new)

    def accumulate(k, alpha, p_buf):
        acc_sc[...] = alpha * acc_sc[...] + _dot(vsT_ref[0, jnp.clip(k, 0, last_stage)], p_buf[...])

    def pair(j, carry):
        m, col_max, alpha = carry
        k = 2 * j
        col_max1 = scores(k + 1, s1_sc)
        m, alpha0 = weights(m, col_max, s0_sc, p0_sc)
        accumulate(k - 1, alpha, p1_sc)
        col_max2 = scores(k + 2, s0_sc)
        m, alpha1 = weights(m, col_max1, s1_sc, p1_sc)
        accumulate(k, alpha0, p0_sc)
        return m, col_max2, alpha1

    acc_sc[...] = jnp.zeros(acc_sc.shape, F32)
    p1_sc[...] = jnp.zeros(p1_sc.shape, BF16)
    m_init = jnp.full((1, NQ * GQ), 0.1 * MASK_SCORE, F32)
    carry = (m_init, scores(0, s0_sc), jnp.ones((1, NQ * GQ), F32))
    n_pairs = n_stages // 2
    m, col_max, alpha = lax.fori_loop(0, n_pairs, pair, carry)
    last = 2 * n_pairs

    @pl.when(n_stages % 2 == 1)
    def _():
        _, alpha_last = weights(m, col_max, s0_sc, p0_sc)
        accumulate(last - 1, alpha, p1_sc)
        accumulate(last, alpha_last, p0_sc)

    @pl.when(n_stages % 2 == 0)
    def _():
        accumulate(last - 1, alpha, p1_sc)
    o_s = acc_sc[0:HEAD_DIM, :] / jnp.maximum(acc_sc[HEAD_DIM:HEAD_DIM + 1, :], 1e-30)

    def gate_row(br):
        def one(x):
            gt = jax.nn.sigmoid(gate_ref[0, x, 0])
            return jnp.concatenate([gt[br * G + g:br * G + g + 1, :] for g in range(G)], axis=1)
        return per_block(one)
    o = gate_row(0) * o_c + gate_row(1) * o_s + gate_row(2) * o_w
    for x in range(NQ):
        o_ref[0, x * Q_BLOCK:(x + 1) * Q_BLOCK, :] = jnp.concatenate(
            [o[:, x * GQ + g * Q_BLOCK:x * GQ + (g + 1) * Q_BLOCK].T for g in range(G)], axis=1)


def _nsa_attention(q, kc, vcT, kaug, vsT, vwT, gates, overlapT, bias_tiles, cmp_bias):
    B, T, _ = kaug.shape
    Hkv, G = NSA_KV_HEADS, NSA_GROUP
    GQ = G * Q_BLOCK
    nqb = T // Q_BLOCK
    n_cmp_pad = kc.shape[2]
    n_sel = T // SEL_BLOCK
    half_keys = SEL_STAGE_TILES * KEY_CHUNK
    NQ = NSA_STEP_QBLOCKS
    assert T % (SEL_GROUP_STAGES * half_keys) == 0 and nqb % NQ == 0
    per_head = lambda b, h, i: (b, h, 0, 0)
    v_rows = V_ROWS
    chunked = lambda w: pl.BlockSpec((1, T // w, v_rows, w), lambda b, h, i: (b, 0, h, 0))
    step_lanes = NQ * GQ
    return pl.pallas_call(
        functools.partial(_nsa_kernel, n_sel=n_sel, n_cmp_pad=n_cmp_pad),
        grid=(B, Hkv, nqb // NQ),
        in_specs=[pl.BlockSpec((1, NQ, 1, HEAD_DIM, GQ), lambda b, h, i: (b, i, h, 0, 0)),
                  pl.BlockSpec((1, 1, n_cmp_pad, HEAD_DIM), per_head),
                  pl.BlockSpec((1, 1, HEAD_DIM, n_cmp_pad), per_head),
                  pl.BlockSpec((1, T, LANES), lambda b, h, i: (b, 0, h)), chunked(half_keys),
                  pl.BlockSpec((1, T, LANES), lambda b, h, i: (b, 0, Hkv + h)), chunked(KEY_CHUNK),
                  pl.BlockSpec((1, NQ, 1, GATE_ROWS, Q_BLOCK), lambda b, h, i: (b, i, h, 0, 0)),
                  pl.BlockSpec(overlapT.shape, lambda b, h, i: (0, 0)),
                  pl.BlockSpec((1, N_ALL_TILES, KEY_CHUNK, GQ), lambda b, h, i: (h, 0, 0, 0)),
                  pl.BlockSpec((1, 2 * n_cmp_pad, GQ), lambda b, h, i: (h, 0, 0))],
        out_specs=pl.BlockSpec((1, NQ * Q_BLOCK, G * HEAD_DIM), lambda b, h, i: (b, i, h)),
        out_shape=jax.ShapeDtypeStruct((B, T, NSA_WIDTH), F32),
        scratch_shapes=[pltpu.VMEM((v_rows, step_lanes), F32), pltpu.VMEM((LANES, step_lanes), BF16),
                        pltpu.VMEM((n_sel, step_lanes), F32),
                        pltpu.VMEM((half_keys, step_lanes), F32), pltpu.VMEM((half_keys, step_lanes), F32),
                        pltpu.VMEM((half_keys, step_lanes), BF16), pltpu.VMEM((half_keys, step_lanes), BF16)],
        compiler_params=_params("parallel", "parallel", "arbitrary"), name="nsa_attention",
    )(q, kc, vcT, kaug, vsT, kaug, vwT, gates, overlapT, bias_tiles, cmp_bias)


def _rwkv_chunk_kernel(rw_ref, prev_ref, mu_ref, w0_ref, wup_ref, a0_ref, aup_ref, gup_ref, kk_ref, ka_ref,
                       rk_ref, seg_ref, tri_ref, q_ref, y0_ref, a_ref, d_ref, g_ref, bonus_ref):
    C, W, N = RWKV_CHUNK, RWKV_WIDTH, HEAD_DIM
    c = pl.program_id(1)
    x = rw_ref[0]
    R = x.shape[0]
    chunk_rows = [slice(ck * C, (ck + 1) * C) for ck in range(R // C)]
    row = lax.broadcasted_iota(jnp.int32, (R, 1), 0)
    last_prev = jnp.where(c == 0, 0.0, prev_ref[0, SUBLANES - 1:SUBLANES, :])
    x_prev = jnp.where(row == 0, last_prev, pltpu.roll(x, 1, axis=0))
    xs = x + (x_prev - x) * mu_ref[...]
    r, k, v = xs[:, 0:W], xs[:, W:2 * W], xs[:, 2 * W:3 * W]
    o = 3 * W
    wd, ad, gd = xs[:, o:o + DECAY_LORA], xs[:, o + DECAY_LORA:o + DECAY_LORA + AAA_LORA], \
        xs[:, o + DECAY_LORA + AAA_LORA:]
    w_log = -jax.nn.softplus(-(w0_ref[...] + _dot(jnp.tanh(wd).astype(BF16), wup_ref[...]))) - 0.5
    lw = -jnp.exp(w_log)
    lr = jax.nn.sigmoid(a0_ref[...] + _dot(ad.astype(BF16), aup_ref[...]))
    g_ref[0] = _dot(jax.nn.sigmoid(gd).astype(BF16), gup_ref[...])
    kk = k * kk_ref[...]
    def head_sums(z):
        return jnp.concatenate([_dot_f32_lhs(z[:, t * LANES:(t + 1) * LANES], seg_ref[...])
                                for t in range(W // LANES)], axis=1)
    kk = kk * lax.rsqrt(jnp.maximum(head_sums(kk * kk), 1e-24))
    k = k * (1.0 + (lr - 1.0) * ka_ref[...])
    bonus_ref[0] = head_sums(r * k * rk_ref[...]) * v
    a_vec, b_vec = -kk, kk * lr

    L = jnp.concatenate([_dot_f32_rhs(tri_ref[...], lw[rs]) for rs in chunk_rows], axis=0)
    L_end = jnp.concatenate([jnp.broadcast_to(L[rs.stop - 1:rs.stop, :], (C, W)) for rs in chunk_rows], axis=0)
    e_neg = jnp.exp(-L)
    e_rem = jnp.exp(L_end - L)
    At, Bt, Kt, Rt = a_vec * jnp.exp(L - lw), b_vec * e_neg, k * e_neg, r * jnp.exp(L)
    Bg, Kg = b_vec * e_rem, k * e_rem
    decay_end = [jnp.exp(L[rs.stop - 1:rs.stop, :]) for rs in chunk_rows]

    HG = RWKV_GROUP_HEADS
    GL = HG * N
    assert C == N and C & (C - 1) == 0
    blk_of = lambda idx: jnp.right_shift(idx, int(math.log2(C)))
    same_head = (blk_of(lax.broadcasted_iota(jnp.int32, (HG * C, GL), 0))
                 == blk_of(lax.broadcasted_iota(jnp.int32, (HG * C, GL), 1)))
    bf = lambda z: z.astype(BF16)
    block_diag = lambda y: jnp.where(same_head, jnp.concatenate([bf(y)] * HG, axis=0), 0.0)
    ri = lax.broadcasted_iota(jnp.int32, (C, GL), 0)
    cj = jnp.bitwise_and(lax.broadcasted_iota(jnp.int32, (C, GL), 1), C - 1)
    strict, incl = ri > cj, ri >= cj
    eye_c = (ri == cj).astype(F32)
    pr = lax.broadcasted_iota(jnp.int32, (N, 2 * N), 0)
    pc = lax.broadcasted_iota(jnp.int32, (N, 2 * N), 1)
    first_head, pair_eye = pc < N, (pr == jnp.bitwise_and(pc, N - 1)).astype(F32)

    def same_head_blocks(z):
        return jnp.where(first_head, z[0:N], z[N:2 * N])
    items = [(ck, gp) for ck in range(len(chunk_rows)) for gp in range(RWKV_HEADS // HG)]
    sl = [(chunk_rows[ck], slice(gp * GL, (gp + 1) * GL)) for ck, gp in items]
    ar = [bf(jnp.concatenate([At[s], Rt[s]], axis=0)) for s in sl]
    zb = [_dot_nt(a, block_diag(Bt[s])) for a, s in zip(ar, sl)]
    zk = [_dot_nt(a, block_diag(Kt[s])) for a, s in zip(ar, sl)]
    n_mat = [jnp.where(strict, z[0:C], 0.0) for z in zb]
    m_mat = [jnp.where(strict, z[0:C], 0.0) for z in zk]
    v_bd = [block_diag(v[s]) for s in sl]
    mv = [_dot(bf(m), vb) for m, vb in zip(m_mat, v_bd)]
    t_inv, n_pow = [eye_c + n for n in n_mat], n_mat
    for _ in range(int(math.log2(C)) - 1):
        n_pow = [_dot(bf(n), block_diag(n)) for n in n_pow]
        t_inv = [t + _dot(bf(t), block_diag(n)) for t, n in zip(t_inv, n_pow)]
    t_bf = [bf(t) for t in t_inv]
    ta = [_dot(t, block_diag(At[s])) for t, s in zip(t_bf, sl)]
    g0 = [_dot(t, block_diag(m)) for t, m in zip(t_bf, mv)]
    p_mat = [bf(jnp.where(incl, z[C:], 0.0)) for z in zb]
    pk_mat = [bf(jnp.where(incl, z[C:], 0.0)) for z in zk]
    q_out = [Rt[s] + _dot(p, block_diag(t)) for p, t, s in zip(p_mat, ta, sl)]
    y0_out = [_dot(p, block_diag(g)) + _dot(pk, vb) for p, g, pk, vb in zip(p_mat, g0, pk_mat, v_bd)]
    bgT = [bf(Bg[s].T) for s in sl]
    a_full = [_dot(b, bf(t)) for b, t in zip(bgT, ta)]
    d_full = [_dot(b, bf(g)) + _dot(bf(Kg[s].T), bf(v[s])) for b, g, s in zip(bgT, g0, sl)]
    for n, (ck, gp) in enumerate(items):
        rows, lanes = sl[n]
        q_ref[0, rows, lanes] = q_out[n].astype(q_ref.dtype)
        y0_ref[0, rows, lanes] = y0_out[n]
        for pp in range(HG // 2):
            blk = slice(pp * 2 * N, (pp + 1) * 2 * N)
            pair_lanes = slice(lanes.start + blk.start, lanes.start + blk.stop)
            a_ref[0, ck, gp * (HG // 2) + pp] = (same_head_blocks(a_full[n][blk, blk])
                                                 + pair_eye * decay_end[ck][:, pair_lanes])
            d_ref[0, ck, gp * (HG // 2) + pp] = same_head_blocks(d_full[n][blk, blk])


def _rwkv_chunks(rw, mu, w0, w_up, a0, a_up, g_up, k_k, k_a, r_k):
    B, T, cols = rw.shape
    C, W, H, N = RWKV_CHUNK, RWKV_WIDTH, RWKV_HEADS, HEAD_DIM
    nc = T // C
    S = RWKV_STEP_CHUNKS
    R = S * C
    assert nc % S == 0
    seg = jnp.asarray(np.kron(np.eye(LANES // N), np.ones((N, N))), BF16)
    tri = jnp.asarray(np.tril(np.ones((C, C))), BF16)
    row = lambda z: z.reshape(1, -1)
    const = lambda b, c: (0, 0)
    vec = pl.BlockSpec((1, W), const)
    tok = pl.BlockSpec((1, R, W), lambda b, c: (b, c, 0))
    mat = pl.BlockSpec((1, S, H // 2, N, 2 * N), lambda b, c: (b, c, 0, 0, 0))
    return pl.pallas_call(
        _rwkv_chunk_kernel,
        grid=(B, nc // S),
        in_specs=[pl.BlockSpec((1, R, cols), lambda b, c: (b, c, 0)),
                  pl.BlockSpec((1, SUBLANES, cols), lambda b, c: (b, jnp.maximum(c * (R // SUBLANES) - 1, 0), 0)),
                  pl.BlockSpec((1, cols), const), vec,
                  pl.BlockSpec((DECAY_LORA, W), const), vec,
                  pl.BlockSpec((AAA_LORA, W), const),
                  pl.BlockSpec((GATE_LORA, W), const), vec, vec, vec,
                  pl.BlockSpec((LANES, LANES), const), pl.BlockSpec((C, C), const)],
        out_specs=[tok, tok, mat, mat, tok, tok],
        out_shape=[jax.ShapeDtypeStruct((B, T, W), BF16), jax.ShapeDtypeStruct((B, T, W), F32),
                   jax.ShapeDtypeStruct((B, nc, H // 2, N, 2 * N), F32),
                   jax.ShapeDtypeStruct((B, nc, H // 2, N, 2 * N), F32),
                   jax.ShapeDtypeStruct((B, T, W), F32), jax.ShapeDtypeStruct((B, T, W), F32)],
        compiler_params=_params("parallel", "parallel"), name="rwkv_chunks",
    )(rw, rw, row(mu), row(w0), w_up.astype(BF16), row(a0), a_up.astype(BF16), g_up.astype(BF16),
      row(k_k), row(k_a), row(r_k), seg, tri)


def _pair_block_diag(x):
    first_head = lax.broadcasted_iota(jnp.int32, x.shape, 1) < x.shape[0]
    zero = jnp.zeros_like(x)
    return jnp.concatenate([jnp.where(first_head, x, zero), jnp.where(first_head, zero, x)], axis=0)


def _pair_side_by_side(z):
    n = z.shape[0] // 2
    return jnp.where(lax.broadcasted_iota(jnp.int32, (n, 2 * n), 1) < n, z[0:n], z[n:])


def _rwkv_state_kernel(a_ref, d_ref, h_ref, h_sc):
    @pl.when(pl.program_id(0) == 0)
    def _():
        h_sc[...] = jnp.zeros(h_sc.shape, F32)

    items = [(b, pair) for b in range(a_ref.shape[0]) for pair in range(a_ref.shape[2])]
    states = [h_sc[b, pair] for b, pair in items]
    for ck in range(a_ref.shape[1]):
        for (b, pair), st in zip(items, states):
            h_ref[b, ck, pair] = _pair_side_by_side(st).astype(h_ref.dtype)
        states = [_dot_hi_lo(_pair_block_diag(a_ref[b, ck, pair]), st) + _pair_block_diag(d_ref[b, ck, pair])
                  for (b, pair), st in zip(items, states)]
    for (b, pair), st in zip(items, states):
        h_sc[b, pair] = st


def _rwkv_readout_kernel(h_ref, q_ref, y0_ref, g_ref, bonus_ref, lw_ref, lb_ref, seg_ref, o_ref):
    C = RWKV_CHUNK
    inv_n = 1.0 / HEAD_DIM
    items = [(b, ck, pair, slice(ck * C, (ck + 1) * C), slice(pair * LANES, (pair + 1) * LANES))
             for b in range(h_ref.shape[0]) for ck in range(h_ref.shape[1]) for pair in range(h_ref.shape[2])]
    ys = [_dot(q_ref[b, rows, lanes], _pair_block_diag(h_ref[b, ck, pair])) + y0_ref[b, rows, lanes]
          for b, ck, pair, rows, lanes in items]
    means = [_dot_f32_lhs(y, seg_ref[...], 2) * inv_n for y in ys]
    cen = [y - mean for y, mean in zip(ys, means)]
    var = [_dot_f32_lhs(jnp.square(c), seg_ref[...], 2) * inv_n for c in cen]
    for (b, _, _, rows, lanes), c, v in zip(items, cen, var):
        yn = c * lax.rsqrt(v + LNX_EPS) * lw_ref[:, lanes] + lb_ref[:, lanes]
        o_ref[b, rows, lanes] = (yn + bonus_ref[b, rows, lanes]) * g_ref[b, rows, lanes]


def _rwkv_scan(A, D, Q, Y0, g, bonus, lnx_w, lnx_b):
    B, nc, P, N, N2 = A.shape
    T, W, C = Q.shape[1], Q.shape[2], RWKV_CHUNK
    S, SR = RWKV_SCAN_STEP_CHUNKS, RWKV_READOUT_STEP_CHUNKS
    assert nc % S == 0 and nc % SR == 0
    mat = lambda s: pl.BlockSpec((B, s, P, N, N2), lambda c: (0, c, 0, 0, 0))
    h_start = pl.pallas_call(
        _rwkv_state_kernel,
        grid=(nc // S,),
        in_specs=[mat(S), mat(S)],
        out_specs=mat(S),
        out_shape=jax.ShapeDtypeStruct((B, nc, P, N, N2), BF16),
        scratch_shapes=[pltpu.VMEM((B, P, N2, N2), F32)],
        compiler_params=_params("arbitrary"), name="rwkv_state",
    )(A, D)
    tok = pl.BlockSpec((B, SR * C, W), lambda c: (0, c, 0))
    vec = pl.BlockSpec((1, W), lambda c: (0, 0))
    seg = jnp.asarray(np.kron(np.eye(2), np.ones((HEAD_DIM, HEAD_DIM))), BF16)
    return pl.pallas_call(
        _rwkv_readout_kernel,
        grid=(nc // SR,),
        in_specs=[mat(SR), tok, tok, tok, tok, vec, vec, pl.BlockSpec((N2, N2), lambda c: (0, 0))],
        out_specs=tok,
        out_shape=jax.ShapeDtypeStruct((B, T, W), F32),
        compiler_params=_params("parallel"), name="rwkv_readout",
    )(h_start, Q, Y0, g, bonus, lnx_w.reshape(1, W), lnx_b.reshape(1, W), seg)


def _mix_xattn_kernel(x_ref, on_ref, or_ref, wo1_ref, wo2_ref, gx_ref, wq_ref, k_ref, v_ref, wo_ref, o_ref):
    x1 = x_ref[0] + _dot(on_ref[0].astype(BF16), wo1_ref[...]) + _dot(or_ref[0].astype(BF16), wo2_ref[...])
    q = _dot(_rms(x1, gx_ref[...]).astype(BF16), wq_ref[...])
    dh = q.shape[-1] // XATTN_HEADS
    qbf = (q * (dh ** -0.5)).astype(BF16)
    outs = []
    for h in range(XATTN_HEADS):
        hs = slice(h * dh, (h + 1) * dh)
        s = _dot_nt(qbf[:, hs], k_ref[0, :, hs])
        e = jnp.exp(s - jnp.max(s, axis=-1, keepdims=True))
        p = e / jnp.sum(e, axis=-1, keepdims=True)
        outs.append(_dot(p.astype(BF16), v_ref[0, :, hs]))
    o = jnp.concatenate(outs, axis=1).astype(BF16)
    o_ref[0] = x1 + _dot(o, wo_ref[...])


def _mix_xattn(x, o_nsa, o_rwkv, w_out, norm_x_g, w_q, mem_k, mem_v, w_o):
    B, T, D = x.shape
    M = mem_k.shape[1]
    tm = min(ROW_TILE, T)
    wo1, wo2 = w_out[:NSA_WIDTH].astype(BF16), w_out[NSA_WIDTH:].astype(BF16)
    const = lambda b, i: (0, 0)
    tile = lambda w: pl.BlockSpec((1, tm, w), lambda b, i: (b, i, 0))
    return pl.pallas_call(
        _mix_xattn_kernel,
        grid=(B, T // tm),
        in_specs=[tile(D), tile(NSA_WIDTH), tile(RWKV_WIDTH),
                  pl.BlockSpec(wo1.shape, const), pl.BlockSpec(wo2.shape, const),
                  pl.BlockSpec((1, D), const), pl.BlockSpec((D, D), const),
                  pl.BlockSpec((1, M, D), lambda b, i: (b, 0, 0)), pl.BlockSpec((1, M, D), lambda b, i: (b, 0, 0)),
                  pl.BlockSpec((D, D), const)],
        out_specs=tile(D),
        out_shape=jax.ShapeDtypeStruct((B, T, D), F32),
        compiler_params=_params("parallel", "parallel"), name="mix_xattn",
    )(x, o_nsa, o_rwkv, wo1, wo2, norm_x_g.reshape(1, D), w_q.astype(BF16), mem_k, mem_v, w_o.astype(BF16))


def _ffn_kernel(x_ref, g_ref, wg_ref, wu_ref, wd_ref, gf_ref, o_ref, *, final_norm):
    x = x_ref[...]
    h = _rms(x, g_ref[...]).astype(BF16)
    act = (jax.nn.silu(_dot(h, wg_ref[...])) * _dot(h, wu_ref[...])).astype(BF16)
    y = x + _dot(act, wd_ref[...])
    o_ref[...] = _rms(y, gf_ref[...]) if final_norm else y


def _ffn(x, norm_g, w_gate, w_up, w_down, final_g, final_norm):
    R, D = x.shape
    F = w_gate.shape[1]
    tm = min(FFN_ROW_TILE, R)
    const = lambda i: (0, 0)
    resident = lambda shape: pl.BlockSpec(shape, const, pipeline_mode=pl.Buffered(1))
    return pl.pallas_call(
        functools.partial(_ffn_kernel, final_norm=final_norm),
        grid=(R // tm,),
        in_specs=[pl.BlockSpec((tm, D), lambda i: (i, 0)), pl.BlockSpec((1, D), const),
                  resident((D, F)), resident((D, F)), resident((F, D)), pl.BlockSpec((1, D), const)],
        out_specs=pl.BlockSpec((tm, D), lambda i: (i, 0)),
        out_shape=jax.ShapeDtypeStruct((R, D), F32),
        compiler_params=_params("parallel"), name="ffn",
    )(x, norm_g.reshape(1, D), w_gate.astype(BF16), w_up.astype(BF16), w_down.astype(BF16),
      final_g.reshape(1, D))


def _overlap_matrix(n_cmp_pad, n_sel):
    c = np.arange(n_cmp_pad)[:, None] * CMP_STRIDE
    s = np.arange(n_sel)[None, :] * SEL_BLOCK
    return ((c <= s + SEL_BLOCK - 1) & (c + CMP_LEN - 1 >= s)).astype(np.float32)


def _layer(x, mem, rel_bias, final_g, is_last, norm_mix_g, w_in, nsa_gate_b, cmp_pe_k, cmp_pe_v,
           cmp_k_w1, cmp_k_b1, cmp_k_w2, cmp_v_w1, cmp_v_b1, cmp_v_w2,
           rwkv_mu, rwkv_w0, rwkv_w_up, rwkv_a0, rwkv_a_up, rwkv_g_up,
           rwkv_k_k, rwkv_k_a, rwkv_r_k, rwkv_lnx_w, rwkv_lnx_b, w_out,
           norm_x_g, norm_mem_g, w_q_x, w_kv_x, w_o_x, norm_ffn_g, w_gate, w_up, w_down):
    B, T, D = x.shape
    Hkv, G, dh = NSA_KV_HEADS, NSA_GROUP, HEAD_DIM
    q, kvc, kaug, vsT, vwT, gates, rw = _proj_in(x, norm_mix_g, w_in, nsa_gate_b)

    n16 = T // CMP_STRIDE
    kvc = kvc.reshape(B, T, 2 * KV_WIDTH)
    kc = _compress(kvc, 0, cmp_pe_k, cmp_k_w1, cmp_k_b1, cmp_k_w2, False)
    vcT = _compress(kvc, 1, cmp_pe_v, cmp_v_w1, cmp_v_b1, cmp_v_w2, True)
    bias_tiles, cmp_bias = _bias_tiles(rel_bias, n16)
    o_nsa = _nsa_attention(q, kc, vcT, kaug.reshape(B, T, 2 * Hkv * LANES), vsT, vwT,
                           gates, jnp.asarray(_overlap_matrix(n16, T // SEL_BLOCK).T, BF16), bias_tiles, cmp_bias)

    Q, Y0, A, Dm, g, bonus = _rwkv_chunks(rw.reshape(B, T, RWKV_COLS), rwkv_mu, rwkv_w0, rwkv_w_up, rwkv_a0,
                                          rwkv_a_up, rwkv_g_up, rwkv_k_k, rwkv_k_a, rwkv_r_k.reshape(-1))
    o_rwkv = _rwkv_scan(A, Dm, Q, Y0, g, bonus, rwkv_lnx_w, rwkv_lnx_b)

    M = mem.shape[1]
    (kv_mem,) = _norm_matmul(mem.reshape(B * M, D), norm_mem_g, [w_kv_x.astype(BF16)], [None], [BF16], ROW_TILE)
    kv_mem = kv_mem.reshape(B, M, 2 * D)
    x = _mix_xattn(x, o_nsa, o_rwkv, w_out, norm_x_g, w_q_x, kv_mem[..., :D], kv_mem[..., D:], w_o_x)
    x = _ffn(x.reshape(B * T, D), norm_ffn_g, w_gate, w_up, w_down, final_g, is_last)
    return x.reshape(B, T, D)


def kernel(x, mem, rel_bias, norm_f_g, norm_mix_g, w_in, nsa_gate_b, cmp_pe_k, cmp_pe_v, cmp_k_w1, cmp_k_b1, cmp_k_w2, cmp_v_w1, cmp_v_b1, cmp_v_w2, rwkv_mu, rwkv_w0, rwkv_w_up, rwkv_a0, rwkv_a_up, rwkv_g_up, rwkv_k_k, rwkv_k_a, rwkv_r_k, rwkv_lnx_w, rwkv_lnx_b, w_out, norm_x_g, norm_mem_g, w_q_x, w_kv_x, w_o_x, norm_ffn_g, w_gate, w_up, w_down):
    stacked = (norm_mix_g, w_in, nsa_gate_b, cmp_pe_k, cmp_pe_v, cmp_k_w1, cmp_k_b1, cmp_k_w2, cmp_v_w1,
               cmp_v_b1, cmp_v_w2, rwkv_mu, rwkv_w0, rwkv_w_up, rwkv_a0, rwkv_a_up, rwkv_g_up, rwkv_k_k,
               rwkv_k_a, rwkv_r_k, rwkv_lnx_w, rwkv_lnx_b, w_out, norm_x_g, norm_mem_g, w_q_x, w_kv_x, w_o_x,
               norm_ffn_g, w_gate, w_up, w_down)
    depth = w_in.shape[0]
    for l in range(depth):
        x = _layer(x, mem, rel_bias, norm_f_g, l == depth - 1, *[p[l] for p in stacked])
    return x
```

```python
import functools
import math

import numpy as np
import jax
import jax.numpy as jnp
from jax import lax
from jax.experimental import pallas as pl
from jax.experimental.pallas import tpu as pltpu

F32 = jnp.float32
BF16 = jnp.bfloat16

LANES = 128
SUBLANES = 8
BF16_ROWS = 16
VMEM_LIMIT_BYTES = 56 * 1024 * 1024

HEAD_DIM = 64
NSA_HEADS = 8
NSA_KV_HEADS = 2
NSA_GROUP = NSA_HEADS // NSA_KV_HEADS
NSA_WIDTH = NSA_HEADS * HEAD_DIM
KV_WIDTH = NSA_KV_HEADS * HEAD_DIM
RWKV_HEADS = 8
RWKV_WIDTH = RWKV_HEADS * HEAD_DIM
CMP_LEN = 32
CMP_STRIDE = 16
SEL_BLOCK = 64
SEL_SHIFT = 6
SEL_TOP = 16
WINDOW = 512
Q_BLOCK = 128
DECAY_LORA = 64
AAA_LORA = 64
GATE_LORA = 128
N_BUCKETS = 32
MAX_DISTANCE = 2048
XATTN_HEADS = 4
RMS_EPS = 1e-6
LNX_EPS = 64e-5
FORCE_SCORE = 1e4
NEG_SCORE = -1e9
MASK_SCORE = -1e30
LOG2E = math.log2(math.e)
RWKV_COLS = 3 * RWKV_WIDTH + DECAY_LORA + AAA_LORA + GATE_LORA
NSA_COLS = NSA_WIDTH + 6 * KV_WIDTH + 3 * NSA_HEADS

KEY_CHUNK = 128
RWKV_CHUNK = 64
RWKV_SCAN_STEP_CHUNKS = 8
RWKV_READOUT_STEP_CHUNKS = 4
RWKV_GROUP_HEADS = 2
RWKV_STEP_CHUNKS = 8
ROW_TILE = 512
FFN_ROW_TILE = 512


def _t5_thresholds():
    d = np.arange(0, 2 * MAX_DISTANCE, dtype=np.int64)
    max_exact = N_BUCKETS // 2
    nf = np.maximum(d, 1).astype(np.float32)
    large = max_exact + (np.log(nf / np.float32(max_exact)) / np.float32(math.log(MAX_DISTANCE / max_exact))
                         * np.float32(N_BUCKETS - max_exact)).astype(np.int32)
    bucket = np.where(d < max_exact, d, np.minimum(large, N_BUCKETS - 1))
    return [int(np.argmax(bucket >= k)) for k in range(N_BUCKETS)]


T5_THRESHOLDS = _t5_thresholds()
N_BIAS_TILES = -(-(T5_THRESHOLDS[-1] + KEY_CHUNK) // KEY_CHUNK) + 1
TILE_MASKED = N_BIAS_TILES
TILE_WINDOW_EDGE = N_BIAS_TILES + 1
N_ALL_TILES = N_BIAS_TILES + 2
SEL_STEP_BLOCKS = 16
SEL_STAGE_TILES = 4
SEL_GROUP_STAGES = SEL_STEP_BLOCKS * SEL_BLOCK // (SEL_STAGE_TILES * KEY_CHUNK)
NSA_STEP_QBLOCKS = 4
V_ROWS = HEAD_DIM + BF16_ROWS
GATE_ROWS = 16


def _params(*semantics):
    return pltpu.CompilerParams(dimension_semantics=semantics, vmem_limit_bytes=VMEM_LIMIT_BYTES)


def _rms(x, g):
    return x * lax.rsqrt(jnp.mean(x * x, axis=-1, keepdims=True) + RMS_EPS) * g


def _dot(a, b, **kw):
    return jnp.dot(a, b, preferred_element_type=F32, **kw)


def _split3(x):
    hi = x.astype(BF16)
    r1 = x - hi.astype(F32)
    mid = r1.astype(BF16)
    lo = (r1 - mid.astype(F32)).astype(BF16)
    return hi, mid, lo


def _dot_f32_lhs(x, w01, pieces=3):
    w = w01.astype(BF16)
    hi, mid, lo = _split3(x)
    return _dot(hi, w) + (_dot(mid, w) + _dot(lo, w) if pieces == 3 else _dot(mid, w))


def _dot_f32_rhs(w01, x):
    w = w01.astype(BF16)
    hi, mid, lo = _split3(x)
    return _dot(w, hi) + (_dot(w, mid) + _dot(w, lo))


def _dot_hi_lo(a, b):
    a_hi = a.astype(BF16)
    a_lo = (a - a_hi.astype(F32)).astype(BF16)
    b_hi = b.astype(BF16)
    b_lo = (b - b_hi.astype(F32)).astype(BF16)
    return _dot(a_hi, b_hi) + (_dot(a_hi, b_lo) + _dot(a_lo, b_hi))


def _dot_nt(a, b, **kw):
    return lax.dot_general(a, b, (((1,), (1,)), ((), ())), preferred_element_type=F32, **kw)


def _norm_matmul_kernel(x_ref, g_ref, *refs, nseg, bias_flags):
    nb = sum(bias_flags)
    w_refs, b_refs, o_refs = refs[:nseg], refs[nseg:nseg + nb], refs[nseg + nb:]
    xn = _rms(x_ref[...], g_ref[...]).astype(BF16)
    bi = 0
    for s in range(nseg):
        y = _dot(xn, w_refs[s][...])
        if bias_flags[s]:
            y = y + b_refs[bi][...]
            bi += 1
        o_refs[s][...] = y.astype(o_refs[s].dtype)


def _norm_matmul(x, g, weights, biases, out_dtypes, row_tile):
    R, D = x.shape
    tm = min(row_tile, R)
    assert R % tm == 0
    nseg = len(weights)
    bias_flags = tuple(b is not None for b in biases)
    const = lambda i: (0, 0)
    in_specs = [pl.BlockSpec((tm, D), lambda i: (i, 0)), pl.BlockSpec((1, D), const)]
    in_specs += [pl.BlockSpec(w.shape, const) for w in weights]
    in_specs += [pl.BlockSpec((1, b.shape[-1]), const) for b in biases if b is not None]
    out_specs = [pl.BlockSpec((tm, w.shape[1]), lambda i: (i, 0)) for w in weights]
    out_shape = [jax.ShapeDtypeStruct((R, w.shape[1]), dt) for w, dt in zip(weights, out_dtypes)]
    return pl.pallas_call(
        functools.partial(_norm_matmul_kernel, nseg=nseg, bias_flags=bias_flags),
        grid=(R // tm,), in_specs=in_specs, out_specs=out_specs, out_shape=out_shape,
        compiler_params=_params("parallel"), name="norm_matmul",
    )(x, g.reshape(1, D), *weights, *[b.reshape(1, -1) for b in biases if b is not None])


def _proj_in_kernel(x_ref, g_ref, wq_ref, wc_ref, wk_ref, wvT_ref, wgT_ref, bg_ref, wr_ref,
                    q_ref, kvc_ref, kaug_ref, vsT_ref, vwT_ref, gate_ref, rw_ref, *, seq_len):
    tm = x_ref.shape[0]
    xn = _rms(x_ref[...], g_ref[...]).astype(BF16)
    qT = (_dot_nt(wq_ref[...], xn) * (HEAD_DIM ** -0.5 * LOG2E)).astype(BF16)
    for j in range(tm // Q_BLOCK):
        for hg in range(NSA_HEADS):
            h, g = divmod(hg, NSA_GROUP)
            q_ref[0, j, h, :, g * Q_BLOCK:(g + 1) * Q_BLOCK] = qT[hg * HEAD_DIM:(hg + 1) * HEAD_DIM,
                                                                  j * Q_BLOCK:(j + 1) * Q_BLOCK]
    kvc_ref[...] = _dot(xn, wc_ref[...])
    rw_ref[...] = _dot(xn, wr_ref[...])
    k_all = _dot(xn, wk_ref[...])
    tok = lax.rem(pl.program_id(0) * tm, seq_len) + lax.broadcasted_iota(jnp.int32, k_all.shape, 0)
    lane = lax.broadcasted_iota(jnp.int32, k_all.shape, 1)
    blk = jnp.bitwise_and(jnp.right_shift(tok, SEL_SHIFT), SEL_STEP_BLOCKS - 1)
    hot = (jnp.bitwise_and(lane, LANES - 1) == HEAD_DIM + blk) & (lane < NSA_KV_HEADS * LANES)
    kaug_ref[...] = jnp.where(hot, 1.0, k_all).astype(BF16)
    vT = _dot_nt(wvT_ref[...], xn)
    row = lax.broadcasted_iota(jnp.int32, vT.shape, 0)
    ones_row = row == HEAD_DIM
    for grp in range(1, 2 * NSA_KV_HEADS):
        ones_row = ones_row | (row == grp * V_ROWS + HEAD_DIM)
    vT = jnp.where(ones_row, 1.0, vT).astype(BF16)
    half = NSA_KV_HEADS * V_ROWS
    stage_keys = vsT_ref.shape[3]
    for c in range(tm // stage_keys):
        vsT_ref[0, c] = vT[0:half, c * stage_keys:(c + 1) * stage_keys]
    for c in range(tm // KEY_CHUNK):
        vwT_ref[0, c] = vT[half:, c * KEY_CHUNK:(c + 1) * KEY_CHUNK]
    gT = _dot_nt(wgT_ref[...], xn) + bg_ref[...]
    for j in range(tm // Q_BLOCK):
        for h in range(NSA_KV_HEADS):
            gate_ref[0, j, h] = gT[h * GATE_ROWS:(h + 1) * GATE_ROWS, j * Q_BLOCK:(j + 1) * Q_BLOCK]


def _proj_in(x, norm_g, w_in, gate_b):
    B, T, D = x.shape
    Hkv, G, dh = NSA_KV_HEADS, NSA_GROUP, HEAD_DIM
    tm = ROW_TILE
    stage_keys = SEL_STAGE_TILES * KEY_CHUNK
    assert T % tm == 0 and tm % Q_BLOCK == 0 and tm % stage_keys == 0
    kv0 = NSA_WIDTH
    g0 = kv0 + 6 * KV_WIDTH
    stream = lambda s: w_in[:, kv0 + s * KV_WIDTH:kv0 + (s + 1) * KV_WIDTH].reshape(D, Hkv, dh)
    pad_cols = lambda w: jnp.pad(w, ((0, 0), (0, 0), (0, LANES - dh))).reshape(D, Hkv * LANES)
    pad_rows = lambda w: jnp.pad(w.transpose(1, 2, 0), ((0, 0), (0, V_ROWS - dh), (0, 0))).reshape(Hkv * V_ROWS, D)
    w_k = jnp.concatenate([pad_cols(stream(2)), pad_cols(stream(4))], axis=1)
    w_vT = jnp.concatenate([pad_rows(stream(3)), pad_rows(stream(5))], axis=0)
    reorder = lambda a: a.reshape(-1, Hkv, G, 3).transpose(1, 3, 2, 0).reshape(Hkv, 3 * G, -1)
    pad_gate = lambda a: jnp.pad(a, ((0, 0), (0, GATE_ROWS - 3 * G), (0, 0))).reshape(Hkv * GATE_ROWS, -1)
    w_gT = pad_gate(reorder(w_in[:, g0:NSA_COLS]))
    b_g = pad_gate(reorder(gate_b.reshape(1, -1)))
    weights = [w_in[:, :kv0].T, w_in[:, kv0:kv0 + 2 * KV_WIDTH], w_k, w_vT, w_gT]
    weights = [w.astype(BF16) for w in weights] + [b_g, w_in[:, NSA_COLS:].astype(BF16)]
    nt = T // tm
    rows = lambda n: pl.BlockSpec((tm, n), lambda i: (i, 0))
    const = lambda i: (0, 0)
    return pl.pallas_call(
        functools.partial(_proj_in_kernel, seq_len=T),
        grid=(B * nt,),
        in_specs=[rows(D), pl.BlockSpec((1, D), const)] + [pl.BlockSpec(w.shape, const) for w in weights],
        out_specs=[pl.BlockSpec((1, tm // Q_BLOCK, Hkv, dh, G * Q_BLOCK), lambda i: (i // nt, i % nt, 0, 0, 0)),
                   rows(2 * KV_WIDTH), rows(2 * Hkv * LANES),
                   pl.BlockSpec((1, tm // stage_keys, Hkv * V_ROWS, stage_keys), lambda i: (i // nt, i % nt, 0, 0)),
                   pl.BlockSpec((1, tm // KEY_CHUNK, Hkv * V_ROWS, KEY_CHUNK), lambda i: (i // nt, i % nt, 0, 0)),
                   pl.BlockSpec((1, tm // Q_BLOCK, Hkv, GATE_ROWS, Q_BLOCK), lambda i: (i // nt, i % nt, 0, 0, 0)),
                   rows(RWKV_COLS)],
        out_shape=[jax.ShapeDtypeStruct((B, T // Q_BLOCK, Hkv, dh, G * Q_BLOCK), BF16),
                   jax.ShapeDtypeStruct((B * T, 2 * KV_WIDTH), F32),
                   jax.ShapeDtypeStruct((B * T, 2 * Hkv * LANES), BF16),
                   jax.ShapeDtypeStruct((B, T // stage_keys, Hkv * V_ROWS, stage_keys), BF16),
                   jax.ShapeDtypeStruct((B, T // KEY_CHUNK, Hkv * V_ROWS, KEY_CHUNK), BF16),
                   jax.ShapeDtypeStruct((B, T // Q_BLOCK, Hkv, GATE_ROWS, Q_BLOCK), F32),
                   jax.ShapeDtypeStruct((B * T, RWKV_COLS), F32)],
        compiler_params=_params("parallel"), name="proj_in",
    )(x.reshape(B * T, D), norm_g.reshape(1, D), *weights)


def _compress_kernel(x_ref, pe_ref, w1_ref, b1_ref, w2_ref, o_ref, *, transpose_out):
    n16 = x_ref.shape[1] // CMP_STRIDE
    hidden = w1_ref.shape[2] // NSA_KV_HEADS
    lo = jnp.zeros((n16, w1_ref.shape[2]), F32)
    hi = jnp.zeros((n16, w1_ref.shape[2]), F32)
    for l in range(CMP_STRIDE):
        rows = x_ref[0, pl.ds(l, n16, stride=CMP_STRIDE), :]
        lo = lo + _dot((rows + pe_ref[l:l + 1, :]).astype(BF16), w1_ref[l])
        hi = hi + _dot((rows + pe_ref[CMP_STRIDE + l:CMP_STRIDE + l + 1, :]).astype(BF16), w1_ref[CMP_STRIDE + l])
    h = lo + pltpu.roll(hi, n16 - 1, axis=0) + b1_ref[...]
    h = jax.nn.gelu(h).astype(BF16)
    for hkv in range(NSA_KV_HEADS):
        hh = h[:, hkv * hidden:(hkv + 1) * hidden]
        if transpose_out:
            o_ref[0, hkv] = _dot_nt(w2_ref[...], hh).astype(o_ref.dtype)
        else:
            o_ref[0, hkv] = _dot(hh, w2_ref[...]).astype(o_ref.dtype)


def _compress(kvc, stream, pe, w1, b1, w2, transpose_out):
    B, T, _ = kvc.shape
    H, dh = NSA_KV_HEADS, HEAD_DIM
    n16 = T // CMP_STRIDE
    hidden = w1.shape[1]
    eye = jnp.eye(H, dtype=w1.dtype)
    w1_bd = jnp.einsum('ldn,hg->lhdgn', w1.reshape(CMP_LEN, dh, hidden), eye).reshape(CMP_LEN, H * dh, H * hidden)
    w2b = (w2.T if transpose_out else w2).astype(BF16)
    oshape = (B, H, dh, n16) if transpose_out else (B, H, n16, dh)
    return pl.pallas_call(
        functools.partial(_compress_kernel, transpose_out=transpose_out),
        grid=(B,),
        in_specs=[pl.BlockSpec((1, T, H * dh), lambda b: (b, 0, stream)),
                  pl.BlockSpec((CMP_LEN, H * dh), lambda b: (0, 0)),
                  pl.BlockSpec(w1_bd.shape, lambda b: (0, 0, 0)),
                  pl.BlockSpec((1, H * hidden), lambda b: (0, 0)),
                  pl.BlockSpec(w2b.shape, lambda b: (0, 0))],
        out_specs=pl.BlockSpec((1,) + oshape[1:], lambda b: (b, 0, 0, 0)),
        out_shape=jax.ShapeDtypeStruct(oshape, BF16),
        compiler_params=_params("parallel"), name="nsa_compress",
    )(kvc, jnp.tile(pe, (1, H)), w1_bd.astype(BF16), jnp.tile(b1.reshape(1, hidden), (1, H)), w2b)


def _bias_of_distance(tab_ref, h, d):
    val = jnp.full(d.shape, tab_ref[h, 0], F32)
    for k in range(1, N_BUCKETS):
        val = jnp.where(d >= T5_THRESHOLDS[k], tab_ref[h, k], val)
    return val * LOG2E


def _bias_tiles_kernel(tab_ref, bt_ref, cb_ref, *, n_cmp_pad):
    hkv = pl.program_id(0)
    j = lax.broadcasted_iota(jnp.int32, (KEY_CHUNK, Q_BLOCK), 0)
    i = lax.broadcasted_iota(jnp.int32, (KEY_CHUNK, Q_BLOCK), 1)
    r2 = lax.broadcasted_iota(jnp.int32, (2 * n_cmp_pad, Q_BLOCK), 0)
    i2 = lax.broadcasted_iota(jnp.int32, (2 * n_cmp_pad, Q_BLOCK), 1)
    l2 = r2 - (n_cmp_pad - KEY_CHUNK)
    d2 = i2 - CMP_STRIDE * l2 + (CMP_STRIDE * KEY_CHUNK - Q_BLOCK - (CMP_LEN - 1))
    hidden2 = (l2 >= KEY_CHUNK) | ((l2 >= 0) & (d2 < 0))
    d2 = jnp.where((l2 >= 0) & (l2 < KEY_CHUNK), d2, 2 * MAX_DISTANCE)
    for g in range(NSA_GROUP):
        h = hkv * NSA_GROUP + g
        lanes = slice(g * Q_BLOCK, (g + 1) * Q_BLOCK)
        for m in range(N_BIAS_TILES):
            tile = _bias_of_distance(tab_ref, h, m * KEY_CHUNK + i - j)
            if m == 0:
                tile = jnp.where(j <= i, tile, MASK_SCORE)
            bt_ref[0, m, :, lanes] = tile
        bt_ref[0, TILE_MASKED, :, lanes] = jnp.full((KEY_CHUNK, Q_BLOCK), MASK_SCORE, F32)
        edge = _bias_of_distance(tab_ref, h, WINDOW + i - j)
        bt_ref[0, TILE_WINDOW_EDGE, :, lanes] = jnp.where(j > i, edge, MASK_SCORE)
        cb_ref[0, :, lanes] = jnp.where(hidden2, MASK_SCORE, _bias_of_distance(tab_ref, h, d2))


def _bias_tiles(rel_bias, n_cmp_pad):
    assert CMP_STRIDE * KEY_CHUNK - Q_BLOCK - (CMP_LEN - 1) >= T5_THRESHOLDS[-1]
    GQ = NSA_GROUP * Q_BLOCK
    return pl.pallas_call(
        functools.partial(_bias_tiles_kernel, n_cmp_pad=n_cmp_pad),
        grid=(NSA_KV_HEADS,),
        in_specs=[pl.BlockSpec(memory_space=pltpu.SMEM)],
        out_specs=[pl.BlockSpec((1, N_ALL_TILES, KEY_CHUNK, GQ), lambda h: (h, 0, 0, 0)),
                   pl.BlockSpec((1, 2 * n_cmp_pad, GQ), lambda h: (h, 0, 0))],
        out_shape=[jax.ShapeDtypeStruct((NSA_KV_HEADS, N_ALL_TILES, KEY_CHUNK, GQ), F32),
                   jax.ShapeDtypeStruct((NSA_KV_HEADS, 2 * n_cmp_pad, GQ), F32)],
        compiler_params=_params("parallel"), name="t5_bias_tiles",
    )(rel_bias.T)


def _nsa_kernel(q_ref, kc_ref, vcT_ref, ks_ref, vsT_ref, kw_ref, vwT_ref, gate_ref, ovT_ref, bt_ref, cb_ref,
                o_ref, acc_sc, qaug_sc, seladd_sc, s0_sc, s1_sc, p0_sc, p1_sc, *, n_sel, n_cmp_pad):
    G, NQ = NSA_GROUP, NSA_STEP_QBLOCKS
    GQ = G * Q_BLOCK
    qbs = [pl.program_id(2) * NQ + x for x in range(NQ)]
    per_block = lambda fn: jnp.concatenate([fn(x) for x in range(NQ)], axis=1)
    tile_g = lambda a: jnp.concatenate([a] * G, axis=1)
    qT = per_block(lambda x: q_ref[0, x, 0])
    qaug_sc[0:HEAD_DIM, :] = qT
    qaug_sc[HEAD_DIM:, :] = jnp.zeros((qaug_sc.shape[0] - HEAD_DIM, NQ * GQ), BF16)
    lane_q = lax.broadcasted_iota(jnp.int32, (1, Q_BLOCK), 1)
    t = per_block(lambda x: qbs[x] * Q_BLOCK + lane_q)

    def bias_tile(dist_of_block):
        def one(x):
            dist = dist_of_block(x)
            return bt_ref[0, jnp.where(dist < 0, TILE_MASKED, jnp.minimum(dist, N_BIAS_TILES - 1))]
        return per_block(one)

    n_back = WINDOW // KEY_CHUNK
    q_win = qaug_sc[...]
    win = {}

    def win_scores(x, back):
        kc = jnp.maximum(qbs[x] - back, 0)
        edge = TILE_WINDOW_EDGE if back == n_back else back
        tile = jnp.where(qbs[x] >= back, edge, TILE_MASKED)
        k_chunk = kw_ref[0, pl.ds(pl.multiple_of(kc * KEY_CHUNK, KEY_CHUNK), KEY_CHUNK), :]
        win["s", x, back] = _dot(k_chunk, q_win[:, x * GQ:(x + 1) * GQ]) + bt_ref[0, tile]
        col_max = jnp.max(win["s", x, back], axis=0, keepdims=True)
        win["m", x] = jnp.maximum(win["m", x], col_max) if ("m", x) in win else col_max

    def win_weights(x, back):
        win["p", x, back] = jnp.exp2(win["s", x, back] - win["m", x]).astype(BF16)

    def win_values(x, back):
        pv = _dot(vwT_ref[0, jnp.maximum(qbs[x] - back, 0)], win["p", x, back])
        win["acc", x] = win["acc", x] + pv if ("acc", x) in win else pv

    backs = list(range(n_back, -1, -1))
    window_work = [functools.partial(fn, x, b) for x in range(NQ)
                   for fn in (win_scores, win_weights, win_values) for b in backs]

    def cmp_branch(n_rows):
        def cmp_bias(x):
            start = pl.multiple_of(n_cmp_pad - (Q_BLOCK // CMP_STRIDE) * (qbs[x] + 1), SUBLANES)
            return cb_ref[0, pl.ds(start, n_rows), :]
        s = _dot(kc_ref[0, 0, 0:n_rows, :], qT) + per_block(cmp_bias)
        e = jnp.exp2(s - jnp.maximum(jnp.max(s, axis=0, keepdims=True), 0.1 * MASK_SCORE))
        p = e * (1.0 / jnp.maximum(jnp.sum(e, axis=0, keepdims=True), 1e-30))
        o_cmp = _dot(vcT_ref[0, 0, :, 0:n_rows], p.astype(BF16))
        psum = per_block(lambda x: sum(p[:, x * GQ + g * Q_BLOCK:x * GQ + (g + 1) * Q_BLOCK] for g in range(G)))
        return o_cmp, _dot_f32_rhs(ovT_ref[:, 0:n_rows], psum)

    half_rows = n_cmp_pad // 2
    n_visible = (Q_BLOCK // CMP_STRIDE) * (qbs[-1] + 1)
    if half_rows % LANES == 0:
        o_c, imp = lax.cond(n_visible <= half_rows, lambda: cmp_branch(half_rows), lambda: cmp_branch(n_cmp_pad))
    else:
        o_c, imp = cmp_branch(n_cmp_pad)
    bj = lax.broadcasted_iota(jnp.int32, (n_sel, NQ * Q_BLOCK), 0)
    cur = jnp.right_shift(t, SEL_SHIFT)
    forced = (bj == 0) | (bj == cur) | (bj == cur - 1)
    valid = bj * SEL_BLOCK <= t
    score = jnp.where(forced, -jnp.inf, jnp.where(valid, imp, NEG_SCORE))
    bjf = bj.astype(F32)
    n_rounds = max(min(SEL_TOP, n_sel) - 3, 0)
    for rnd in range(n_rounds):
        mx = jnp.max(score, axis=0, keepdims=True)
        first = jnp.min(jnp.where(score == mx, bjf, float(n_sel)), axis=0, keepdims=True)
        score = jnp.where(bjf == first, -jnp.inf, score)
        take = -(-len(window_work) // (n_rounds - rnd))
        for piece in window_work[:take]:
            piece()
        window_work = window_work[take:]
    for piece in window_work:
        piece()
    o_w = per_block(lambda x: win["acc", x][0:HEAD_DIM] / jnp.maximum(win["acc", x][HEAD_DIM:HEAD_DIM + 1], 1e-30))
    sel_add = jnp.where(score == -jnp.inf, 0.0, MASK_SCORE)
    seladd_sc[...] = per_block(lambda x: tile_g(sel_add[:, x * Q_BLOCK:(x + 1) * Q_BLOCK]))

    stage_keys = SEL_STAGE_TILES * KEY_CHUNK
    n_stages = qbs[-1] // SEL_STAGE_TILES + 1
    last_stage = ks_ref.shape[1] // stage_keys - 1

    def scores(k, s_buf):
        kk = jnp.minimum(k, last_stage)
        blk0 = pl.multiple_of(kk // SEL_GROUP_STAGES * SEL_STEP_BLOCKS, SEL_STEP_BLOCKS)
        qaug_sc[HEAD_DIM:HEAD_DIM + SEL_STEP_BLOCKS, :] = seladd_sc[pl.ds(blk0, SEL_STEP_BLOCKS), :].astype(BF16)
        k0 = pl.multiple_of(kk * stage_keys, stage_keys)
        bias = jnp.concatenate([bias_tile(lambda x, c=c: qbs[x] - (k * SEL_STAGE_TILES + c))
                                for c in range(SEL_STAGE_TILES)], axis=0)
        s = _dot(ks_ref[0, pl.ds(k0, stage_keys), :], qaug_sc[...]) + bias
        s_buf[...] = s
        return jnp.max(s, axis=0, keepdims=True)

    def weights(m, col_max, s_buf, p_buf):
        m_new = jnp.maximum(m, col_max)
        p_buf[...] = jnp.exp2(s_buf[...] - m_new).astype(BF16)
        return m_new, jnp.exp2(m - m_new)

    def accumulate(k, alpha, p_buf):
        acc_sc[...] = alpha * acc_sc[...] + _dot(vsT_ref[0, jnp.clip(k, 0, last_stage)], p_buf[...])

    def pair(j, carry):
        m, col_max, alpha = carry
        k = 2 * j
        col_max1 = scores(k + 1, s1_sc)
        m, alpha0 = weights(m, col_max, s0_sc, p0_sc)
        accumulate(k - 1, alpha, p1_sc)
        col_max2 = scores(k + 2, s0_sc)
        m, alpha1 = weights(m, col_max1, s1_sc, p1_sc)
        accumulate(k, alpha0, p0_sc)
        return m, col_max2, alpha1

    acc_sc[...] = jnp.zeros(acc_sc.shape, F32)
    p1_sc[...] = jnp.zeros(p1_sc.shape, BF16)
    m_init = jnp.full((1, NQ * GQ), 0.1 * MASK_SCORE, F32)
    carry = (m_init, scores(0, s0_sc), jnp.ones((1, NQ * GQ), F32))
    n_pairs = n_stages // 2
    m, col_max, alpha = lax.fori_loop(0, n_pairs, pair, carry)
    last = 2 * n_pairs

    @pl.when(n_stages % 2 == 1)
    def _():
        _, alpha_last = weights(m, col_max, s0_sc, p0_sc)
        accumulate(last - 1, alpha, p1_sc)
        accumulate(last, alpha_last, p0_sc)

    @pl.when(n_stages % 2 == 0)
    def _():
        accumulate(last - 1, alpha, p1_sc)
    o_s = acc_sc[0:HEAD_DIM, :] / jnp.maximum(acc_sc[HEAD_DIM:HEAD_DIM + 1, :], 1e-30)

    def gate_row(br):
        def one(x):
            gt = jax.nn.sigmoid(gate_ref[0, x, 0])
            return jnp.concatenate([gt[br * G + g:br * G + g + 1, :] for g in range(G)], axis=1)
        return per_block(one)
    o = gate_row(0) * o_c + gate_row(1) * o_s + gate_row(2) * o_w
    for x in range(NQ):
        o_ref[0, x * Q_BLOCK:(x + 1) * Q_BLOCK, :] = jnp.concatenate(
            [o[:, x * GQ + g * Q_BLOCK:x * GQ + (g + 1) * Q_BLOCK].T for g in range(G)], axis=1)


def _nsa_attention(q, kc, vcT, kaug, vsT, vwT, gates, overlapT, bias_tiles, cmp_bias):
    B, T, _ = kaug.shape
    Hkv, G = NSA_KV_HEADS, NSA_GROUP
    GQ = G * Q_BLOCK
    nqb = T // Q_BLOCK
    n_cmp_pad = kc.shape[2]
    n_sel = T // SEL_BLOCK
    half_keys = SEL_STAGE_TILES * KEY_CHUNK
    NQ = NSA_STEP_QBLOCKS
    assert T % (SEL_GROUP_STAGES * half_keys) == 0 and nqb % NQ == 0
    assert FORCE_SCORE > NSA_GROUP
    per_head = lambda b, h, i: (b, h, 0, 0)
    v_rows = V_ROWS
    chunked = lambda w: pl.BlockSpec((1, T // w, v_rows, w), lambda b, h, i: (b, 0, h, 0))
    step_lanes = NQ * GQ
    return pl.pallas_call(
        functools.partial(_nsa_kernel, n_sel=n_sel, n_cmp_pad=n_cmp_pad),
        grid=(B, Hkv, nqb // NQ),
        in_specs=[pl.BlockSpec((1, NQ, 1, HEAD_DIM, GQ), lambda b, h, i: (b, i, h, 0, 0)),
                  pl.BlockSpec((1, 1, n_cmp_pad, HEAD_DIM), per_head),
                  pl.BlockSpec((1, 1, HEAD_DIM, n_cmp_pad), per_head),
                  pl.BlockSpec((1, T, LANES), lambda b, h, i: (b, 0, h)), chunked(half_keys),
                  pl.BlockSpec((1, T, LANES), lambda b, h, i: (b, 0, Hkv + h)), chunked(KEY_CHUNK),
                  pl.BlockSpec((1, NQ, 1, GATE_ROWS, Q_BLOCK), lambda b, h, i: (b, i, h, 0, 0)),
                  pl.BlockSpec(overlapT.shape, lambda b, h, i: (0, 0)),
                  pl.BlockSpec((1, N_ALL_TILES, KEY_CHUNK, GQ), lambda b, h, i: (h, 0, 0, 0)),
                  pl.BlockSpec((1, 2 * n_cmp_pad, GQ), lambda b, h, i: (h, 0, 0))],
        out_specs=pl.BlockSpec((1, NQ * Q_BLOCK, G * HEAD_DIM), lambda b, h, i: (b, i, h)),
        out_shape=jax.ShapeDtypeStruct((B, T, NSA_WIDTH), F32),
        scratch_shapes=[pltpu.VMEM((v_rows, step_lanes), F32), pltpu.VMEM((LANES, step_lanes), BF16),
                        pltpu.VMEM((n_sel, step_lanes), F32),
                        pltpu.VMEM((half_keys, step_lanes), F32), pltpu.VMEM((half_keys, step_lanes), F32),
                        pltpu.VMEM((half_keys, step_lanes), BF16), pltpu.VMEM((half_keys, step_lanes), BF16)],
        compiler_params=_params("parallel", "parallel", "arbitrary"), name="nsa_attention",
    )(q, kc, vcT, kaug, vsT, kaug, vwT, gates, overlapT, bias_tiles, cmp_bias)


def _rwkv_chunk_kernel(rw_ref, prev_ref, mu_ref, w0_ref, wup_ref, a0_ref, aup_ref, gup_ref, kk_ref, ka_ref,
                       rk_ref, seg_ref, tri_ref, q_ref, y0_ref, a_ref, d_ref, g_ref, bonus_ref):
    C, W, N = RWKV_CHUNK, RWKV_WIDTH, HEAD_DIM
    c = pl.program_id(1)
    x = rw_ref[0]
    R = x.shape[0]
    chunk_rows = [slice(ck * C, (ck + 1) * C) for ck in range(R // C)]
    row = lax.broadcasted_iota(jnp.int32, (R, 1), 0)
    last_prev = jnp.where(c == 0, 0.0, prev_ref[0, SUBLANES - 1:SUBLANES, :])
    x_prev = jnp.where(row == 0, last_prev, pltpu.roll(x, 1, axis=0))
    xs = x + (x_prev - x) * mu_ref[...]
    r, k, v = xs[:, 0:W], xs[:, W:2 * W], xs[:, 2 * W:3 * W]
    o = 3 * W
    wd, ad, gd = xs[:, o:o + DECAY_LORA], xs[:, o + DECAY_LORA:o + DECAY_LORA + AAA_LORA], \
        xs[:, o + DECAY_LORA + AAA_LORA:]
    w_log = -jax.nn.softplus(-(w0_ref[...] + _dot(jnp.tanh(wd).astype(BF16), wup_ref[...]))) - 0.5
    lw = -jnp.exp(w_log)
    lr = jax.nn.sigmoid(a0_ref[...] + _dot(ad.astype(BF16), aup_ref[...]))
    g_ref[0] = _dot(jax.nn.sigmoid(gd).astype(BF16), gup_ref[...])
    kk = k * kk_ref[...]
    def head_sums(z):
        return jnp.concatenate([_dot_f32_lhs(z[:, t * LANES:(t + 1) * LANES], seg_ref[...])
                                for t in range(W // LANES)], axis=1)
    kk = kk * lax.rsqrt(jnp.maximum(head_sums(kk * kk), 1e-24))
    k = k * (1.0 + (lr - 1.0) * ka_ref[...])
    bonus_ref[0] = head_sums(r * k * rk_ref[...]) * v
    a_vec, b_vec = -kk, kk * lr

    L = jnp.concatenate([_dot_f32_rhs(tri_ref[...], lw[rs]) for rs in chunk_rows], axis=0)
    L_end = jnp.concatenate([jnp.broadcast_to(L[rs.stop - 1:rs.stop, :], (C, W)) for rs in chunk_rows], axis=0)
    e_neg = jnp.exp(-L)
    e_rem = jnp.exp(L_end - L)
    At, Bt, Kt, Rt = a_vec * jnp.exp(L - lw), b_vec * e_neg, k * e_neg, r * jnp.exp(L)
    Bg, Kg = b_vec * e_rem, k * e_rem
    decay_end = [jnp.exp(L[rs.stop - 1:rs.stop, :]) for rs in chunk_rows]

    HG = RWKV_GROUP_HEADS
    GL = HG * N
    assert C == N and C & (C - 1) == 0
    blk_of = lambda idx: jnp.right_shift(idx, int(math.log2(C)))
    same_head = (blk_of(lax.broadcasted_iota(jnp.int32, (HG * C, GL), 0))
                 == blk_of(lax.broadcasted_iota(jnp.int32, (HG * C, GL), 1)))
    bf = lambda z: z.astype(BF16)
    block_diag = lambda y: jnp.where(same_head, jnp.concatenate([bf(y)] * HG, axis=0), 0.0)
    ri = lax.broadcasted_iota(jnp.int32, (C, GL), 0)
    cj = jnp.bitwise_and(lax.broadcasted_iota(jnp.int32, (C, GL), 1), C - 1)
    strict, incl = ri > cj, ri >= cj
    eye_c = (ri == cj).astype(F32)
    pr = lax.broadcasted_iota(jnp.int32, (N, 2 * N), 0)
    pc = lax.broadcasted_iota(jnp.int32, (N, 2 * N), 1)
    first_head, pair_eye = pc < N, (pr == jnp.bitwise_and(pc, N - 1)).astype(F32)

    def same_head_blocks(z):
        return jnp.where(first_head, z[0:N], z[N:2 * N])
    items = [(ck, gp) for ck in range(len(chunk_rows)) for gp in range(RWKV_HEADS // HG)]
    sl = [(chunk_rows[ck], slice(gp * GL, (gp + 1) * GL)) for ck, gp in items]
    ar = [bf(jnp.concatenate([At[s], Rt[s]], axis=0)) for s in sl]
    zb = [_dot_nt(a, block_diag(Bt[s])) for a, s in zip(ar, sl)]
    zk = [_dot_nt(a, block_diag(Kt[s])) for a, s in zip(ar, sl)]
    n_mat = [jnp.where(strict, z[0:C], 0.0) for z in zb]
    m_mat = [jnp.where(strict, z[0:C], 0.0) for z in zk]
    v_bd = [block_diag(v[s]) for s in sl]
    mv = [_dot(bf(m), vb) for m, vb in zip(m_mat, v_bd)]
    t_inv, n_pow = [eye_c + n for n in n_mat], n_mat
    for _ in range(int(math.log2(C)) - 1):
        n_pow = [_dot(bf(n), block_diag(n)) for n in n_pow]
        t_inv = [t + _dot(bf(t), block_diag(n)) for t, n in zip(t_inv, n_pow)]
    t_bf = [bf(t) for t in t_inv]
    ta = [_dot(t, block_diag(At[s])) for t, s in zip(t_bf, sl)]
    g0 = [_dot(t, block_diag(m)) for t, m in zip(t_bf, mv)]
    p_mat = [bf(jnp.where(incl, z[C:], 0.0)) for z in zb]
    pk_mat = [bf(jnp.where(incl, z[C:], 0.0)) for z in zk]
    q_out = [Rt[s] + _dot(p, block_diag(t)) for p, t, s in zip(p_mat, ta, sl)]
    y0_out = [_dot(p, block_diag(g)) + _dot(pk, vb) for p, g, pk, vb in zip(p_mat, g0, pk_mat, v_bd)]
    bgT = [bf(Bg[s].T) for s in sl]
    a_full = [_dot(b, bf(t)) for b, t in zip(bgT, ta)]
    d_full = [_dot(b, bf(g)) + _dot(bf(Kg[s].T), bf(v[s])) for b, g, s in zip(bgT, g0, sl)]
    for n, (ck, gp) in enumerate(items):
        rows, lanes = sl[n]
        q_ref[0, rows, lanes] = q_out[n].astype(q_ref.dtype)
        y0_ref[0, rows, lanes] = y0_out[n]
        for pp in range(HG // 2):
            blk = slice(pp * 2 * N, (pp + 1) * 2 * N)
            pair_lanes = slice(lanes.start + blk.start, lanes.start + blk.stop)
            a_ref[0, ck, gp * (HG // 2) + pp] = (same_head_blocks(a_full[n][blk, blk])
                                                 + pair_eye * decay_end[ck][:, pair_lanes])
            d_ref[0, ck, gp * (HG // 2) + pp] = same_head_blocks(d_full[n][blk, blk])


def _rwkv_chunks(rw, mu, w0, w_up, a0, a_up, g_up, k_k, k_a, r_k):
    B, T, cols = rw.shape
    C, W, H, N = RWKV_CHUNK, RWKV_WIDTH, RWKV_HEADS, HEAD_DIM
    nc = T // C
    S = RWKV_STEP_CHUNKS
    R = S * C
    assert nc % S == 0
    seg = jnp.asarray(np.kron(np.eye(LANES // N), np.ones((N, N))), BF16)
    tri = jnp.asarray(np.tril(np.ones((C, C))), BF16)
    row = lambda z: z.reshape(1, -1)
    const = lambda b, c: (0, 0)
    vec = pl.BlockSpec((1, W), const)
    tok = pl.BlockSpec((1, R, W), lambda b, c: (b, c, 0))
    mat = pl.BlockSpec((1, S, H // 2, N, 2 * N), lambda b, c: (b, c, 0, 0, 0))
    return pl.pallas_call(
        _rwkv_chunk_kernel,
        grid=(B, nc // S),
        in_specs=[pl.BlockSpec((1, R, cols), lambda b, c: (b, c, 0)),
                  pl.BlockSpec((1, SUBLANES, cols), lambda b, c: (b, jnp.maximum(c * (R // SUBLANES) - 1, 0), 0)),
                  pl.BlockSpec((1, cols), const), vec,
                  pl.BlockSpec((DECAY_LORA, W), const), vec,
                  pl.BlockSpec((AAA_LORA, W), const),
                  pl.BlockSpec((GATE_LORA, W), const), vec, vec, vec,
                  pl.BlockSpec((LANES, LANES), const), pl.BlockSpec((C, C), const)],
        out_specs=[tok, tok, mat, mat, tok, tok],
        out_shape=[jax.ShapeDtypeStruct((B, T, W), BF16), jax.ShapeDtypeStruct((B, T, W), F32),
                   jax.ShapeDtypeStruct((B, nc, H // 2, N, 2 * N), F32),
                   jax.ShapeDtypeStruct((B, nc, H // 2, N, 2 * N), F32),
                   jax.ShapeDtypeStruct((B, T, W), F32), jax.ShapeDtypeStruct((B, T, W), F32)],
        compiler_params=_params("parallel", "parallel"), name="rwkv_chunks",
    )(rw, rw, row(mu), row(w0), w_up.astype(BF16), row(a0), a_up.astype(BF16), g_up.astype(BF16),
      row(k_k), row(k_a), row(r_k), seg, tri)


def _pair_block_diag(x):
    first_head = lax.broadcasted_iota(jnp.int32, x.shape, 1) < x.shape[0]
    zero = jnp.zeros_like(x)
    return jnp.concatenate([jnp.where(first_head, x, zero), jnp.where(first_head, zero, x)], axis=0)


def _pair_side_by_side(z):
    n = z.shape[0] // 2
    return jnp.where(lax.broadcasted_iota(jnp.int32, (n, 2 * n), 1) < n, z[0:n], z[n:])


def _rwkv_state_kernel(a_ref, d_ref, h_ref, h_sc):
    @pl.when(pl.program_id(0) == 0)
    def _():
        h_sc[...] = jnp.zeros(h_sc.shape, F32)

    items = [(b, pair) for b in range(a_ref.shape[0]) for pair in range(a_ref.shape[2])]
    states = [h_sc[b, pair] for b, pair in items]
    for ck in range(a_ref.shape[1]):
        for (b, pair), st in zip(items, states):
            h_ref[b, ck, pair] = _pair_side_by_side(st).astype(h_ref.dtype)
        states = [_dot_hi_lo(_pair_block_diag(a_ref[b, ck, pair]), st) + _pair_block_diag(d_ref[b, ck, pair])
                  for (b, pair), st in zip(items, states)]
    for (b, pair), st in zip(items, states):
        h_sc[b, pair] = st


def _rwkv_readout_kernel(h_ref, q_ref, y0_ref, g_ref, bonus_ref, lw_ref, lb_ref, seg_ref, o_ref):
    C = RWKV_CHUNK
    inv_n = 1.0 / HEAD_DIM
    items = [(b, ck, pair, slice(ck * C, (ck + 1) * C), slice(pair * LANES, (pair + 1) * LANES))
             for b in range(h_ref.shape[0]) for ck in range(h_ref.shape[1]) for pair in range(h_ref.shape[2])]
    ys = [_dot(q_ref[b, rows, lanes], _pair_block_diag(h_ref[b, ck, pair])) + y0_ref[b, rows, lanes]
          for b, ck, pair, rows, lanes in items]
    means = [_dot_f32_lhs(y, seg_ref[...], 2) * inv_n for y in ys]
    cen = [y - mean for y, mean in zip(ys, means)]
    var = [_dot_f32_lhs(jnp.square(c), seg_ref[...], 2) * inv_n for c in cen]
    for (b, _, _, rows, lanes), c, v in zip(items, cen, var):
        yn = c * lax.rsqrt(v + LNX_EPS) * lw_ref[:, lanes] + lb_ref[:, lanes]
        o_ref[b, rows, lanes] = (yn + bonus_ref[b, rows, lanes]) * g_ref[b, rows, lanes]


def _rwkv_scan(A, D, Q, Y0, g, bonus, lnx_w, lnx_b):
    B, nc, P, N, N2 = A.shape
    T, W, C = Q.shape[1], Q.shape[2], RWKV_CHUNK
    S, SR = RWKV_SCAN_STEP_CHUNKS, RWKV_READOUT_STEP_CHUNKS
    assert nc % S == 0 and nc % SR == 0
    mat = lambda s: pl.BlockSpec((B, s, P, N, N2), lambda c: (0, c, 0, 0, 0))
    h_start = pl.pallas_call(
        _rwkv_state_kernel,
        grid=(nc // S,),
        in_specs=[mat(S), mat(S)],
        out_specs=mat(S),
        out_shape=jax.ShapeDtypeStruct((B, nc, P, N, N2), BF16),
        scratch_shapes=[pltpu.VMEM((B, P, N2, N2), F32)],
        compiler_params=_params("arbitrary"), name="rwkv_state",
    )(A, D)
    tok = pl.BlockSpec((B, SR * C, W), lambda c: (0, c, 0))
    vec = pl.BlockSpec((1, W), lambda c: (0, 0))
    seg = jnp.asarray(np.kron(np.eye(2), np.ones((HEAD_DIM, HEAD_DIM))), BF16)
    return pl.pallas_call(
        _rwkv_readout_kernel,
        grid=(nc // SR,),
        in_specs=[mat(SR), tok, tok, tok, tok, vec, vec, pl.BlockSpec((N2, N2), lambda c: (0, 0))],
        out_specs=tok,
        out_shape=jax.ShapeDtypeStruct((B, T, W), F32),
        compiler_params=_params("parallel"), name="rwkv_readout",
    )(h_start, Q, Y0, g, bonus, lnx_w.reshape(1, W), lnx_b.reshape(1, W), seg)


def _mix_xattn_kernel(x_ref, on_ref, or_ref, wo1_ref, wo2_ref, gx_ref, wq_ref, k_ref, v_ref, wo_ref, o_ref):
    x1 = x_ref[0] + _dot(on_ref[0].astype(BF16), wo1_ref[...]) + _dot(or_ref[0].astype(BF16), wo2_ref[...])
    q = _dot(_rms(x1, gx_ref[...]).astype(BF16), wq_ref[...])
    dh = q.shape[-1] // XATTN_HEADS
    qbf = (q * (dh ** -0.5)).astype(BF16)
    outs = []
    for h in range(XATTN_HEADS):
        hs = slice(h * dh, (h + 1) * dh)
        s = _dot_nt(qbf[:, hs], k_ref[0, :, hs])
        e = jnp.exp(s - jnp.max(s, axis=-1, keepdims=True))
        p = e / jnp.sum(e, axis=-1, keepdims=True)
        outs.append(_dot(p.astype(BF16), v_ref[0, :, hs]))
    o = jnp.concatenate(outs, axis=1).astype(BF16)
    o_ref[0] = x1 + _dot(o, wo_ref[...])


def _mix_xattn(x, o_nsa, o_rwkv, w_out, norm_x_g, w_q, mem_k, mem_v, w_o):
    B, T, D = x.shape
    M = mem_k.shape[1]
    tm = min(ROW_TILE, T)
    wo1, wo2 = w_out[:NSA_WIDTH].astype(BF16), w_out[NSA_WIDTH:].astype(BF16)
    const = lambda b, i: (0, 0)
    tile = lambda w: pl.BlockSpec((1, tm, w), lambda b, i: (b, i, 0))
    return pl.pallas_call(
        _mix_xattn_kernel,
        grid=(B, T // tm),
        in_specs=[tile(D), tile(NSA_WIDTH), tile(RWKV_WIDTH),
                  pl.BlockSpec(wo1.shape, const), pl.BlockSpec(wo2.shape, const),
                  pl.BlockSpec((1, D), const), pl.BlockSpec((D, D), const),
                  pl.BlockSpec((1, M, D), lambda b, i: (b, 0, 0)), pl.BlockSpec((1, M, D), lambda b, i: (b, 0, 0)),
                  pl.BlockSpec((D, D), const)],
        out_specs=tile(D),
        out_shape=jax.ShapeDtypeStruct((B, T, D), F32),
        compiler_params=_params("parallel", "parallel"), name="mix_xattn",
    )(x, o_nsa, o_rwkv, wo1, wo2, norm_x_g.reshape(1, D), w_q.astype(BF16), mem_k, mem_v, w_o.astype(BF16))


def _ffn_kernel(x_ref, g_ref, wg_ref, wu_ref, wd_ref, gf_ref, o_ref, *, final_norm):
    x = x_ref[...]
    h = _rms(x, g_ref[...]).astype(BF16)
    act = (jax.nn.silu(_dot(h, wg_ref[...])) * _dot(h, wu_ref[...])).astype(BF16)
    y = x + _dot(act, wd_ref[...])
    o_ref[...] = _rms(y, gf_ref[...]) if final_norm else y


def _ffn(x, norm_g, w_gate, w_up, w_down, final_g, final_norm):
    R, D = x.shape
    F = w_gate.shape[1]
    tm = min(FFN_ROW_TILE, R)
    const = lambda i: (0, 0)
    resident = lambda shape: pl.BlockSpec(shape, const, pipeline_mode=pl.Buffered(1))
    return pl.pallas_call(
        functools.partial(_ffn_kernel, final_norm=final_norm),
        grid=(R // tm,),
        in_specs=[pl.BlockSpec((tm, D), lambda i: (i, 0)), pl.BlockSpec((1, D), const),
                  resident((D, F)), resident((D, F)), resident((F, D)), pl.BlockSpec((1, D), const)],
        out_specs=pl.BlockSpec((tm, D), lambda i: (i, 0)),
        out_shape=jax.ShapeDtypeStruct((R, D), F32),
        compiler_params=_params("parallel"), name="ffn",
    )(x, norm_g.reshape(1, D), w_gate.astype(BF16), w_up.astype(BF16), w_down.astype(BF16),
      final_g.reshape(1, D))


def _overlap_matrix(n_cmp_pad, n_sel):
    c = np.arange(n_cmp_pad)[:, None] * CMP_STRIDE
    s = np.arange(n_sel)[None, :] * SEL_BLOCK
    return ((c <= s + SEL_BLOCK - 1) & (c + CMP_LEN - 1 >= s)).astype(np.float32)


def _layer(x, mem, rel_bias, final_g, is_last, norm_mix_g, w_in, nsa_gate_b, cmp_pe_k, cmp_pe_v,
           cmp_k_w1, cmp_k_b1, cmp_k_w2, cmp_v_w1, cmp_v_b1, cmp_v_w2,
           rwkv_mu, rwkv_w0, rwkv_w_up, rwkv_a0, rwkv_a_up, rwkv_g_up,
           rwkv_k_k, rwkv_k_a, rwkv_r_k, rwkv_lnx_w, rwkv_lnx_b, w_out,
           norm_x_g, norm_mem_g, w_q_x, w_kv_x, w_o_x, norm_ffn_g, w_gate, w_up, w_down):
    B, T, D = x.shape
    Hkv, G, dh = NSA_KV_HEADS, NSA_GROUP, HEAD_DIM
    q, kvc, kaug, vsT, vwT, gates, rw = _proj_in(x, norm_mix_g, w_in, nsa_gate_b)

    n16 = T // CMP_STRIDE
    kvc = kvc.reshape(B, T, 2 * KV_WIDTH)
    kc = _compress(kvc, 0, cmp_pe_k, cmp_k_w1, cmp_k_b1, cmp_k_w2, False)
    vcT = _compress(kvc, 1, cmp_pe_v, cmp_v_w1, cmp_v_b1, cmp_v_w2, True)
    bias_tiles, cmp_bias = _bias_tiles(rel_bias, n16)
    o_nsa = _nsa_attention(q, kc, vcT, kaug.reshape(B, T, 2 * Hkv * LANES), vsT, vwT,
                           gates, jnp.asarray(_overlap_matrix(n16, T // SEL_BLOCK).T, BF16), bias_tiles, cmp_bias)

    Q, Y0, A, Dm, g, bonus = _rwkv_chunks(rw.reshape(B, T, RWKV_COLS), rwkv_mu, rwkv_w0, rwkv_w_up, rwkv_a0,
                                          rwkv_a_up, rwkv_g_up, rwkv_k_k, rwkv_k_a, rwkv_r_k.reshape(-1))
    o_rwkv = _rwkv_scan(A, Dm, Q, Y0, g, bonus, rwkv_lnx_w, rwkv_lnx_b)

    M = mem.shape[1]
    (kv_mem,) = _norm_matmul(mem.reshape(B * M, D), norm_mem_g, [w_kv_x.astype(BF16)], [None], [BF16], ROW_TILE)
    kv_mem = kv_mem.reshape(B, M, 2 * D)
    x = _mix_xattn(x, o_nsa, o_rwkv, w_out, norm_x_g, w_q_x, kv_mem[..., :D], kv_mem[..., D:], w_o_x)
    x = _ffn(x.reshape(B * T, D), norm_ffn_g, w_gate, w_up, w_down, final_g, is_last)
    return x.reshape(B, T, D)


def kernel(x, mem, rel_bias, norm_f_g, norm_mix_g, w_in, nsa_gate_b, cmp_pe_k, cmp_pe_v, cmp_k_w1, cmp_k_b1, cmp_k_w2, cmp_v_w1, cmp_v_b1, cmp_v_w2, rwkv_mu, rwkv_w0, rwkv_w_up, rwkv_a0, rwkv_a_up, rwkv_g_up, rwkv_k_k, rwkv_k_a, rwkv_r_k, rwkv_lnx_w, rwkv_lnx_b, w_out, norm_x_g, norm_mem_g, w_q_x, w_kv_x, w_o_x, norm_ffn_g, w_gate, w_up, w_down):
    stacked = (norm_mix_g, w_in, nsa_gate_b, cmp_pe_k, cmp_pe_v, cmp_k_w1, cmp_k_b1, cmp_k_w2, cmp_v_w1,
               cmp_v_b1, cmp_v_w2, rwkv_mu, rwkv_w0, rwkv_w_up, rwkv_a0, rwkv_a_up, rwkv_g_up, rwkv_k_k,
               rwkv_k_a, rwkv_r_k, rwkv_lnx_w, rwkv_lnx_b, w_out, norm_x_g, norm_mem_g, w_q_x, w_kv_x, w_o_x,
               norm_ffn_g, w_gate, w_up, w_down)
    depth = w_in.shape[0]
    for l in range(depth):
        x = _layer(x, mem, rel_bias, norm_f_g, l == depth - 1, *[p[l] for p in stacked])
    return x
```

```python
import functools
import math

import numpy as np
import jax
import jax.numpy as jnp
from jax import lax
from jax.experimental import pallas as pl
from jax.experimental.pallas import tpu as pltpu

F32 = jnp.float32
BF16 = jnp.bfloat16

LANES = 128
SUBLANES = 8
BF16_ROWS = 16
VMEM_LIMIT_BYTES = 56 * 1024 * 1024

HEAD_DIM = 64
NSA_HEADS = 8
NSA_KV_HEADS = 2
NSA_GROUP = NSA_HEADS // NSA_KV_HEADS
NSA_WIDTH = NSA_HEADS * HEAD_DIM
KV_WIDTH = NSA_KV_HEADS * HEAD_DIM
RWKV_HEADS = 8
RWKV_WIDTH = RWKV_HEADS * HEAD_DIM
CMP_LEN = 32
CMP_STRIDE = 16
SEL_BLOCK = 64
SEL_SHIFT = 6
SEL_TOP = 16
WINDOW = 512
Q_BLOCK = 128
DECAY_LORA = 64
AAA_LORA = 64
GATE_LORA = 128
N_BUCKETS = 32
MAX_DISTANCE = 2048
XATTN_HEADS = 4
RMS_EPS = 1e-6
LNX_EPS = 64e-5
FORCE_SCORE = 1e4
NEG_SCORE = -1e9
MASK_SCORE = -1e30
LOG2E = math.log2(math.e)
RWKV_COLS = 3 * RWKV_WIDTH + DECAY_LORA + AAA_LORA + GATE_LORA
NSA_COLS = NSA_WIDTH + 6 * KV_WIDTH + 3 * NSA_HEADS

KEY_CHUNK = 128
RWKV_CHUNK = 64
RWKV_SCAN_STEP_CHUNKS = 8
RWKV_READOUT_STEP_CHUNKS = 4
RWKV_GROUP_HEADS = 2
RWKV_STEP_CHUNKS = 8
ROW_TILE = 512
FFN_ROW_TILE = 512


def _t5_thresholds():
    d = np.arange(0, 2 * MAX_DISTANCE, dtype=np.int64)
    max_exact = N_BUCKETS // 2
    nf = np.maximum(d, 1).astype(np.float32)
    large = max_exact + (np.log(nf / np.float32(max_exact)) / np.float32(math.log(MAX_DISTANCE / max_exact))
                         * np.float32(N_BUCKETS - max_exact)).astype(np.int32)
    bucket = np.where(d < max_exact, d, np.minimum(large, N_BUCKETS - 1))
    return [int(np.argmax(bucket >= k)) for k in range(N_BUCKETS)]


T5_THRESHOLDS = _t5_thresholds()
N_BIAS_TILES = -(-(T5_THRESHOLDS[-1] + KEY_CHUNK) // KEY_CHUNK) + 1
TILE_MASKED = N_BIAS_TILES
TILE_WINDOW_EDGE = N_BIAS_TILES + 1
N_ALL_TILES = N_BIAS_TILES + 2
SEL_STEP_BLOCKS = 16
SEL_STAGE_TILES = 4
SEL_GROUP_STAGES = SEL_STEP_BLOCKS * SEL_BLOCK // (SEL_STAGE_TILES * KEY_CHUNK)
NSA_STEP_QBLOCKS = 4
V_ROWS = HEAD_DIM + BF16_ROWS
GATE_ROWS = 16


def _params(*semantics):
    return pltpu.CompilerParams(dimension_semantics=semantics, vmem_limit_bytes=VMEM_LIMIT_BYTES)


def _rms(x, g):
    return x * lax.rsqrt(jnp.mean(x * x, axis=-1, keepdims=True) + RMS_EPS) * g


def _dot(a, b, **kw):
    return jnp.dot(a, b, preferred_element_type=F32, **kw)


def _split3(x):
    hi = x.astype(BF16)
    r1 = x - hi.astype(F32)
    mid = r1.astype(BF16)
    lo = (r1 - mid.astype(F32)).astype(BF16)
    return hi, mid, lo


def _dot_f32_lhs(x, w01, pieces=3):
    w = w01.astype(BF16)
    hi, mid, lo = _split3(x)
    return _dot(hi, w) + (_dot(mid, w) + _dot(lo, w) if pieces == 3 else _dot(mid, w))


def _dot_f32_rhs(w01, x):
    w = w01.astype(BF16)
    hi, mid, lo = _split3(x)
    return _dot(w, hi) + (_dot(w, mid) + _dot(w, lo))


def _dot_hi_lo(a, b):
    a_hi = a.astype(BF16)
    a_lo = (a - a_hi.astype(F32)).astype(BF16)
    b_hi = b.astype(BF16)
    b_lo = (b - b_hi.astype(F32)).astype(BF16)
    return _dot(a_hi, b_hi) + (_dot(a_hi, b_lo) + _dot(a_lo, b_hi))


def _dot_nt(a, b, **kw):
    return lax.dot_general(a, b, (((1,), (1,)), ((), ())), preferred_element_type=F32, **kw)


def _norm_matmul_kernel(x_ref, g_ref, *refs, nseg, bias_flags):
    nb = sum(bias_flags)
    w_refs, b_refs, o_refs = refs[:nseg], refs[nseg:nseg + nb], refs[nseg + nb:]
    xn = _rms(x_ref[...], g_ref[...]).astype(BF16)
    bi = 0
    for s in range(nseg):
        y = _dot(xn, w_refs[s][...])
        if bias_flags[s]:
            y = y + b_refs[bi][...]
            bi += 1
        o_refs[s][...] = y.astype(o_refs[s].dtype)


def _norm_matmul(x, g, weights, biases, out_dtypes, row_tile):
    R, D = x.shape
    tm = min(row_tile, R)
    assert R % tm == 0
    nseg = len(weights)
    bias_flags = tuple(b is not None for b in biases)
    const = lambda i: (0, 0)
    in_specs = [pl.BlockSpec((tm, D), lambda i: (i, 0)), pl.BlockSpec((1, D), const)]
    in_specs += [pl.BlockSpec(w.shape, const) for w in weights]
    in_specs += [pl.BlockSpec((1, b.shape[-1]), const) for b in biases if b is not None]
    out_specs = [pl.BlockSpec((tm, w.shape[1]), lambda i: (i, 0)) for w in weights]
    out_shape = [jax.ShapeDtypeStruct((R, w.shape[1]), dt) for w, dt in zip(weights, out_dtypes)]
    return pl.pallas_call(
        functools.partial(_norm_matmul_kernel, nseg=nseg, bias_flags=bias_flags),
        grid=(R // tm,), in_specs=in_specs, out_specs=out_specs, out_shape=out_shape,
        compiler_params=_params("parallel"), name="norm_matmul",
    )(x, g.reshape(1, D), *weights, *[b.reshape(1, -1) for b in biases if b is not None])


def _proj_in_kernel(x_ref, g_ref, wq_ref, wc_ref, wk_ref, wvT_ref, wgT_ref, bg_ref, wr_ref,
                    q_ref, kvc_ref, kaug_ref, vsT_ref, vwT_ref, gate_ref, rw_ref, *, seq_len):
    tm = x_ref.shape[0]
    xn = _rms(x_ref[...], g_ref[...]).astype(BF16)
    qT = (_dot_nt(wq_ref[...], xn) * (HEAD_DIM ** -0.5 * LOG2E)).astype(BF16)
    for j in range(tm // Q_BLOCK):
        for hg in range(NSA_HEADS):
            h, g = divmod(hg, NSA_GROUP)
            q_ref[0, j, h, :, g * Q_BLOCK:(g + 1) * Q_BLOCK] = qT[hg * HEAD_DIM:(hg + 1) * HEAD_DIM,
                                                                  j * Q_BLOCK:(j + 1) * Q_BLOCK]
    kvc_ref[...] = _dot(xn, wc_ref[...])
    rw_ref[...] = _dot(xn, wr_ref[...])
    k_all = _dot(xn, wk_ref[...])
    tok = lax.rem(pl.program_id(0) * tm, seq_len) + lax.broadcasted_iota(jnp.int32, k_all.shape, 0)
    lane = lax.broadcasted_iota(jnp.int32, k_all.shape, 1)
    blk = jnp.bitwise_and(jnp.right_shift(tok, SEL_SHIFT), SEL_STEP_BLOCKS - 1)
    hot = (jnp.bitwise_and(lane, LANES - 1) == HEAD_DIM + blk) & (lane < NSA_KV_HEADS * LANES)
    kaug_ref[...] = jnp.where(hot, 1.0, k_all).astype(BF16)
    vT = _dot_nt(wvT_ref[...], xn)
    row = lax.broadcasted_iota(jnp.int32, vT.shape, 0)
    ones_row = row == HEAD_DIM
    for grp in range(1, 2 * NSA_KV_HEADS):
        ones_row = ones_row | (row == grp * V_ROWS + HEAD_DIM)
    vT = jnp.where(ones_row, 1.0, vT).astype(BF16)
    half = NSA_KV_HEADS * V_ROWS
    stage_keys = vsT_ref.shape[3]
    for c in range(tm // stage_keys):
        vsT_ref[0, c] = vT[0:half, c * stage_keys:(c + 1) * stage_keys]
    for c in range(tm // KEY_CHUNK):
        vwT_ref[0, c] = vT[half:, c * KEY_CHUNK:(c + 1) * KEY_CHUNK]
    gT = _dot_nt(wgT_ref[...], xn) + bg_ref[...]
    for j in range(tm // Q_BLOCK):
        for h in range(NSA_KV_HEADS):
            gate_ref[0, j, h] = gT[h * GATE_ROWS:(h + 1) * GATE_ROWS, j * Q_BLOCK:(j + 1) * Q_BLOCK]


def _proj_in(x, norm_g, w_in, gate_b):
    B, T, D = x.shape
    Hkv, G, dh = NSA_KV_HEADS, NSA_GROUP, HEAD_DIM
    tm = ROW_TILE
    stage_keys = SEL_STAGE_TILES * KEY_CHUNK
    assert T % tm == 0 and tm % Q_BLOCK == 0 and tm % stage_keys == 0
    kv0 = NSA_WIDTH
    g0 = kv0 + 6 * KV_WIDTH
    stream = lambda s: w_in[:, kv0 + s * KV_WIDTH:kv0 + (s + 1) * KV_WIDTH].reshape(D, Hkv, dh)
    pad_cols = lambda w: jnp.pad(w, ((0, 0), (0, 0), (0, LANES - dh))).reshape(D, Hkv * LANES)
    pad_rows = lambda w: jnp.pad(w.transpose(1, 2, 0), ((0, 0), (0, V_ROWS - dh), (0, 0))).reshape(Hkv * V_ROWS, D)
    w_k = jnp.concatenate([pad_cols(stream(2)), pad_cols(stream(4))], axis=1)
    w_vT = jnp.concatenate([pad_rows(stream(3)), pad_rows(stream(5))], axis=0)
    reorder = lambda a: a.reshape(-1, Hkv, G, 3).transpose(1, 3, 2, 0).reshape(Hkv, 3 * G, -1)
    pad_gate = lambda a: jnp.pad(a, ((0, 0), (0, GATE_ROWS - 3 * G), (0, 0))).reshape(Hkv * GATE_ROWS, -1)
    w_gT = pad_gate(reorder(w_in[:, g0:NSA_COLS]))
    b_g = pad_gate(reorder(gate_b.reshape(1, -1)))
    weights = [w_in[:, :kv0].T, w_in[:, kv0:kv0 + 2 * KV_WIDTH], w_k, w_vT, w_gT]
    weights = [w.astype(BF16) for w in weights] + [b_g, w_in[:, NSA_COLS:].astype(BF16)]
    nt = T // tm
    rows = lambda n: pl.BlockSpec((tm, n), lambda i: (i, 0))
    const = lambda i: (0, 0)
    return pl.pallas_call(
        functools.partial(_proj_in_kernel, seq_len=T),
        grid=(B * nt,),
        in_specs=[rows(D), pl.BlockSpec((1, D), const)] + [pl.BlockSpec(w.shape, const) for w in weights],
        out_specs=[pl.BlockSpec((1, tm // Q_BLOCK, Hkv, dh, G * Q_BLOCK), lambda i: (i // nt, i % nt, 0, 0, 0)),
                   rows(2 * KV_WIDTH), rows(2 * Hkv * LANES),
                   pl.BlockSpec((1, tm // stage_keys, Hkv * V_ROWS, stage_keys), lambda i: (i // nt, i % nt, 0, 0)),
                   pl.BlockSpec((1, tm // KEY_CHUNK, Hkv * V_ROWS, KEY_CHUNK), lambda i: (i // nt, i % nt, 0, 0)),
                   pl.BlockSpec((1, tm // Q_BLOCK, Hkv, GATE_ROWS, Q_BLOCK), lambda i: (i // nt, i % nt, 0, 0, 0)),
                   rows(RWKV_COLS)],
        out_shape=[jax.ShapeDtypeStruct((B, T // Q_BLOCK, Hkv, dh, G * Q_BLOCK), BF16),
                   jax.ShapeDtypeStruct((B * T, 2 * KV_WIDTH), F32),
                   jax.ShapeDtypeStruct((B * T, 2 * Hkv * LANES), BF16),
                   jax.ShapeDtypeStruct((B, T // stage_keys, Hkv * V_ROWS, stage_keys), BF16),
                   jax.ShapeDtypeStruct((B, T // KEY_CHUNK, Hkv * V_ROWS, KEY_CHUNK), BF16),
                   jax.ShapeDtypeStruct((B, T // Q_BLOCK, Hkv, GATE_ROWS, Q_BLOCK), F32),
                   jax.ShapeDtypeStruct((B * T, RWKV_COLS), F32)],
        compiler_params=_params("parallel"), name="proj_in",
    )(x.reshape(B * T, D), norm_g.reshape(1, D), *weights)


def _compress_kernel(x_ref, pe_ref, w1_ref, b1_ref, w2_ref, o_ref, *, transpose_out):
    n16 = x_ref.shape[1] // CMP_STRIDE
    hidden = w1_ref.shape[2] // NSA_KV_HEADS
    lo = jnp.zeros((n16, w1_ref.shape[2]), F32)
    hi = jnp.zeros((n16, w1_ref.shape[2]), F32)
    for l in range(CMP_STRIDE):
        rows = x_ref[0, pl.ds(l, n16, stride=CMP_STRIDE), :]
        lo = lo + _dot((rows + pe_ref[l:l + 1, :]).astype(BF16), w1_ref[l])
        hi = hi + _dot((rows + pe_ref[CMP_STRIDE + l:CMP_STRIDE + l + 1, :]).astype(BF16), w1_ref[CMP_STRIDE + l])
    h = lo + pltpu.roll(hi, n16 - 1, axis=0) + b1_ref[...]
    h = jax.nn.gelu(h).astype(BF16)
    for hkv in range(NSA_KV_HEADS):
        hh = h[:, hkv * hidden:(hkv + 1) * hidden]
        if transpose_out:
            o_ref[0, hkv] = _dot_nt(w2_ref[...], hh).astype(o_ref.dtype)
        else:
            o_ref[0, hkv] = _dot(hh, w2_ref[...]).astype(o_ref.dtype)


def _compress(kvc, stream, pe, w1, b1, w2, transpose_out):
    B, T, _ = kvc.shape
    H, dh = NSA_KV_HEADS, HEAD_DIM
    n16 = T // CMP_STRIDE
    hidden = w1.shape[1]
    w1r = w1.astype(BF16).reshape(CMP_LEN, dh, hidden)
    w1_bd = jnp.concatenate([jnp.pad(w1r, ((0, 0), (0, 0), (h * hidden, (H - 1 - h) * hidden))) for h in range(H)],
                            axis=1)
    w2b = (w2.T if transpose_out else w2).astype(BF16)
    oshape = (B, H, dh, n16) if transpose_out else (B, H, n16, dh)
    return pl.pallas_call(
        functools.partial(_compress_kernel, transpose_out=transpose_out),
        grid=(B,),
        in_specs=[pl.BlockSpec((1, T, H * dh), lambda b: (b, 0, stream)),
                  pl.BlockSpec((CMP_LEN, H * dh), lambda b: (0, 0)),
                  pl.BlockSpec(w1_bd.shape, lambda b: (0, 0, 0)),
                  pl.BlockSpec((1, H * hidden), lambda b: (0, 0)),
                  pl.BlockSpec(w2b.shape, lambda b: (0, 0))],
        out_specs=pl.BlockSpec((1,) + oshape[1:], lambda b: (b, 0, 0, 0)),
        out_shape=jax.ShapeDtypeStruct(oshape, BF16),
        compiler_params=_params("parallel"), name="nsa_compress",
    )(kvc, jnp.tile(pe, (1, H)), w1_bd, jnp.tile(b1.reshape(1, hidden), (1, H)), w2b)


def _bias_of_distance(tab_ref, h, d):
    val = jnp.full(d.shape, tab_ref[h, 0], F32)
    for k in range(1, N_BUCKETS):
        val = jnp.where(d >= T5_THRESHOLDS[k], tab_ref[h, k], val)
    return val * LOG2E


def _bias_tiles_kernel(tab_ref, bt_ref, cb_ref, *, n_cmp_pad):
    hkv = pl.program_id(0)
    j = lax.broadcasted_iota(jnp.int32, (KEY_CHUNK, Q_BLOCK), 0)
    i = lax.broadcasted_iota(jnp.int32, (KEY_CHUNK, Q_BLOCK), 1)
    r2 = lax.broadcasted_iota(jnp.int32, (2 * n_cmp_pad, Q_BLOCK), 0)
    i2 = lax.broadcasted_iota(jnp.int32, (2 * n_cmp_pad, Q_BLOCK), 1)
    l2 = r2 - (n_cmp_pad - KEY_CHUNK)
    d2 = i2 - CMP_STRIDE * l2 + (CMP_STRIDE * KEY_CHUNK - Q_BLOCK - (CMP_LEN - 1))
    hidden2 = (l2 >= KEY_CHUNK) | ((l2 >= 0) & (d2 < 0))
    d2 = jnp.where((l2 >= 0) & (l2 < KEY_CHUNK), d2, 2 * MAX_DISTANCE)
    for g in range(NSA_GROUP):
        h = hkv * NSA_GROUP + g
        lanes = slice(g * Q_BLOCK, (g + 1) * Q_BLOCK)
        for m in range(N_BIAS_TILES):
            tile = _bias_of_distance(tab_ref, h, m * KEY_CHUNK + i - j)
            if m == 0:
                tile = jnp.where(j <= i, tile, MASK_SCORE)
            bt_ref[0, m, :, lanes] = tile
        bt_ref[0, TILE_MASKED, :, lanes] = jnp.full((KEY_CHUNK, Q_BLOCK), MASK_SCORE, F32)
        edge = _bias_of_distance(tab_ref, h, WINDOW + i - j)
        bt_ref[0, TILE_WINDOW_EDGE, :, lanes] = jnp.where(j > i, edge, MASK_SCORE)
        cb_ref[0, :, lanes] = jnp.where(hidden2, MASK_SCORE, _bias_of_distance(tab_ref, h, d2))


def _bias_tiles(rel_bias, n_cmp_pad):
    assert CMP_STRIDE * KEY_CHUNK - Q_BLOCK - (CMP_LEN - 1) >= T5_THRESHOLDS[-1]
    GQ = NSA_GROUP * Q_BLOCK
    return pl.pallas_call(
        functools.partial(_bias_tiles_kernel, n_cmp_pad=n_cmp_pad),
        grid=(NSA_KV_HEADS,),
        in_specs=[pl.BlockSpec(memory_space=pltpu.SMEM)],
        out_specs=[pl.BlockSpec((1, N_ALL_TILES, KEY_CHUNK, GQ), lambda h: (h, 0, 0, 0)),
                   pl.BlockSpec((1, 2 * n_cmp_pad, GQ), lambda h: (h, 0, 0))],
        out_shape=[jax.ShapeDtypeStruct((NSA_KV_HEADS, N_ALL_TILES, KEY_CHUNK, GQ), F32),
                   jax.ShapeDtypeStruct((NSA_KV_HEADS, 2 * n_cmp_pad, GQ), F32)],
        compiler_params=_params("parallel"), name="t5_bias_tiles",
    )(rel_bias.T)


def _nsa_kernel(q_ref, kc_ref, vcT_ref, ks_ref, vsT_ref, kw_ref, vwT_ref, gate_ref, ovT_ref, bt_ref, cb_ref,
                o_ref, acc_sc, qaug_sc, seladd_sc, s0_sc, s1_sc, p0_sc, p1_sc, *, n_sel, n_cmp_pad):
    G, NQ = NSA_GROUP, NSA_STEP_QBLOCKS
    GQ = G * Q_BLOCK
    qbs = [pl.program_id(2) * NQ + x for x in range(NQ)]
    per_block = lambda fn: jnp.concatenate([fn(x) for x in range(NQ)], axis=1)
    tile_g = lambda a: jnp.concatenate([a] * G, axis=1)
    qT = per_block(lambda x: q_ref[0, x, 0])
    qaug_sc[0:HEAD_DIM, :] = qT
    qaug_sc[HEAD_DIM:, :] = jnp.zeros((qaug_sc.shape[0] - HEAD_DIM, NQ * GQ), BF16)
    lane_q = lax.broadcasted_iota(jnp.int32, (1, Q_BLOCK), 1)
    t = per_block(lambda x: qbs[x] * Q_BLOCK + lane_q)

    def bias_tile(dist_of_block):
        def one(x):
            dist = dist_of_block(x)
            return bt_ref[0, jnp.where(dist < 0, TILE_MASKED, jnp.minimum(dist, N_BIAS_TILES - 1))]
        return per_block(one)

    n_back = WINDOW // KEY_CHUNK
    q_win = qaug_sc[...]
    win = {}

    def win_scores(x, back):
        kc = jnp.maximum(qbs[x] - back, 0)
        edge = TILE_WINDOW_EDGE if back == n_back else back
        tile = jnp.where(qbs[x] >= back, edge, TILE_MASKED)
        k_chunk = kw_ref[0, pl.ds(pl.multiple_of(kc * KEY_CHUNK, KEY_CHUNK), KEY_CHUNK), :]
        win["s", x, back] = _dot(k_chunk, q_win[:, x * GQ:(x + 1) * GQ]) + bt_ref[0, tile]
        col_max = jnp.max(win["s", x, back], axis=0, keepdims=True)
        win["m", x] = jnp.maximum(win["m", x], col_max) if ("m", x) in win else col_max

    def win_weights(x, back):
        win["p", x, back] = jnp.exp2(win["s", x, back] - win["m", x]).astype(BF16)

    def win_values(x, back):
        pv = _dot(vwT_ref[0, jnp.maximum(qbs[x] - back, 0)], win["p", x, back])
        win["acc", x] = win["acc", x] + pv if ("acc", x) in win else pv

    backs = list(range(n_back, -1, -1))
    window_work = [functools.partial(fn, x, b) for x in range(NQ)
                   for fn in (win_scores, win_weights, win_values) for b in backs]

    def cmp_branch(n_rows):
        def cmp_bias(x):
            start = pl.multiple_of(n_cmp_pad - (Q_BLOCK // CMP_STRIDE) * (qbs[x] + 1), SUBLANES)
            return cb_ref[0, pl.ds(start, n_rows), :]
        s = _dot(kc_ref[0, 0, 0:n_rows, :], qT) + per_block(cmp_bias)
        e = jnp.exp2(s - jnp.maximum(jnp.max(s, axis=0, keepdims=True), 0.1 * MASK_SCORE))
        p = e * (1.0 / jnp.maximum(jnp.sum(e, axis=0, keepdims=True), 1e-30))
        o_cmp = _dot(vcT_ref[0, 0, :, 0:n_rows], p.astype(BF16))
        psum = per_block(lambda x: sum(p[:, x * GQ + g * Q_BLOCK:x * GQ + (g + 1) * Q_BLOCK] for g in range(G)))
        return o_cmp, _dot_f32_rhs(ovT_ref[:, 0:n_rows], psum)

    half_rows = n_cmp_pad // 2
    n_visible = (Q_BLOCK // CMP_STRIDE) * (qbs[-1] + 1)
    if half_rows % LANES == 0:
        o_c, imp = lax.cond(n_visible <= half_rows, lambda: cmp_branch(half_rows), lambda: cmp_branch(n_cmp_pad))
    else:
        o_c, imp = cmp_branch(n_cmp_pad)
    bj = lax.broadcasted_iota(jnp.int32, (n_sel, NQ * Q_BLOCK), 0)
    cur = jnp.right_shift(t, SEL_SHIFT)
    forced = (bj == 0) | (bj == cur) | (bj == cur - 1)
    valid = bj * SEL_BLOCK <= t
    score = jnp.where(forced, -jnp.inf, jnp.where(valid, imp, NEG_SCORE))
    bjf = bj.astype(F32)
    n_rounds = max(min(SEL_TOP, n_sel) - 3, 0)
    for rnd in range(n_rounds):
        mx = jnp.max(score, axis=0, keepdims=True)
        first = jnp.min(jnp.where(score == mx, bjf, float(n_sel)), axis=0, keepdims=True)
        score = jnp.where(bjf == first, -jnp.inf, score)
        take = -(-len(window_work) // (n_rounds - rnd))
        for piece in window_work[:take]:
            piece()
        window_work = window_work[take:]
    for piece in window_work:
        piece()
    o_w = per_block(lambda x: win["acc", x][0:HEAD_DIM] / jnp.maximum(win["acc", x][HEAD_DIM:HEAD_DIM + 1], 1e-30))
    sel_add = jnp.where(score == -jnp.inf, 0.0, MASK_SCORE)
    seladd_sc[...] = per_block(lambda x: tile_g(sel_add[:, x * Q_BLOCK:(x + 1) * Q_BLOCK]))

    stage_keys = SEL_STAGE_TILES * KEY_CHUNK
    n_stages = qbs[-1] // SEL_STAGE_TILES + 1
    last_stage = ks_ref.shape[1] // stage_keys - 1

    def scores(k, s_buf):
        kk = jnp.minimum(k, last_stage)
        blk0 = pl.multiple_of(kk // SEL_GROUP_STAGES * SEL_STEP_BLOCKS, SEL_STEP_BLOCKS)
        qaug_sc[HEAD_DIM:HEAD_DIM + SEL_STEP_BLOCKS, :] = seladd_sc[pl.ds(blk0, SEL_STEP_BLOCKS), :].astype(BF16)
        k0 = pl.multiple_of(kk * stage_keys, stage_keys)
        bias = jnp.concatenate([bias_tile(lambda x, c=c: qbs[x] - (k * SEL_STAGE_TILES + c))
                                for c in range(SEL_STAGE_TILES)], axis=0)
        s = _dot(ks_ref[0, pl.ds(k0, stage_keys), :], qaug_sc[...]) + bias
        s_buf[...] = s
        return jnp.max(s, axis=0, keepdims=True)

    def weights(m, col_max, s_buf, p_buf):
        m_new = jnp.maximum(m, col_max)
        p_buf[...] = jnp.exp2(s_buf[...] - m_new).astype(BF16)
        return m_new, jnp.exp2(m - m_new)

    def accumulate(k, alpha, p_buf):
        acc_sc[...] = alpha * acc_sc[...] + _dot(vsT_ref[0, jnp.clip(k, 0, last_stage)], p_buf[...])

    def pair(j, carry):
        m, col_max, alpha = carry
        k = 2 * j
        col_max1 = scores(k + 1, s1_sc)
        m, alpha0 = weights(m, col_max, s0_sc, p0_sc)
        accumulate(k - 1, alpha, p1_sc)
        col_max2 = scores(k + 2, s0_sc)
        m, alpha1 = weights(m, col_max1, s1_sc, p1_sc)
        accumulate(k, alpha0, p0_sc)
        return m, col_max2, alpha1

    acc_sc[...] = jnp.zeros(acc_sc.shape, F32)
    p1_sc[...] = jnp.zeros(p1_sc.shape, BF16)
    m_init = jnp.full((1, NQ * GQ), 0.1 * MASK_SCORE, F32)
    carry = (m_init, scores(0, s0_sc), jnp.ones((1, NQ * GQ), F32))
    n_pairs = n_stages // 2
    m, col_max, alpha = lax.fori_loop(0, n_pairs, pair, carry)
    last = 2 * n_pairs

    @pl.when(n_stages % 2 == 1)
    def _():
        _, alpha_last = weights(m, col_max, s0_sc, p0_sc)
        accumulate(last - 1, alpha, p1_sc)
        accumulate(last, alpha_last, p0_sc)

    @pl.when(n_stages % 2 == 0)
    def _():
        accumulate(last - 1, alpha, p1_sc)
    o_s = acc_sc[0:HEAD_DIM, :] / jnp.maximum(acc_sc[HEAD_DIM:HEAD_DIM + 1, :], 1e-30)

    def gate_row(br):
        def one(x):
            gt = jax.nn.sigmoid(gate_ref[0, x, 0])
            return jnp.concatenate([gt[br * G + g:br * G + g + 1, :] for g in range(G)], axis=1)
        return per_block(one)
    o = gate_row(0) * o_c + gate_row(1) * o_s + gate_row(2) * o_w
    for x in range(NQ):
        o_ref[0, x * Q_BLOCK:(x + 1) * Q_BLOCK, :] = jnp.concatenate(
            [o[:, x * GQ + g * Q_BLOCK:x * GQ + (g + 1) * Q_BLOCK].T for g in range(G)], axis=1)


def _nsa_attention(q, kc, vcT, kaug, vsT, vwT, gates, overlapT, bias_tiles, cmp_bias):
    B, T, _ = kaug.shape
    Hkv, G = NSA_KV_HEADS, NSA_GROUP
    GQ = G * Q_BLOCK
    nqb = T // Q_BLOCK
    n_cmp_pad = kc.shape[2]
    n_sel = T // SEL_BLOCK
    half_keys = SEL_STAGE_TILES * KEY_CHUNK
    NQ = NSA_STEP_QBLOCKS
    assert T % (SEL_GROUP_STAGES * half_keys) == 0 and nqb % NQ == 0
    assert FORCE_SCORE > NSA_GROUP
    per_head = lambda b, h, i: (b, h, 0, 0)
    v_rows = V_ROWS
    chunked = lambda w: pl.BlockSpec((1, T // w, v_rows, w), lambda b, h, i: (b, 0, h, 0))
    step_lanes = NQ * GQ
    return pl.pallas_call(
        functools.partial(_nsa_kernel, n_sel=n_sel, n_cmp_pad=n_cmp_pad),
        grid=(B, Hkv, nqb // NQ),
        in_specs=[pl.BlockSpec((1, NQ, 1, HEAD_DIM, GQ), lambda b, h, i: (b, i, h, 0, 0)),
                  pl.BlockSpec((1, 1, n_cmp_pad, HEAD_DIM), per_head),
                  pl.BlockSpec((1, 1, HEAD_DIM, n_cmp_pad), per_head),
                  pl.BlockSpec((1, T, LANES), lambda b, h, i: (b, 0, h)), chunked(half_keys),
                  pl.BlockSpec((1, T, LANES), lambda b, h, i: (b, 0, Hkv + h)), chunked(KEY_CHUNK),
                  pl.BlockSpec((1, NQ, 1, GATE_ROWS, Q_BLOCK), lambda b, h, i: (b, i, h, 0, 0)),
                  pl.BlockSpec(overlapT.shape, lambda b, h, i: (0, 0)),
                  pl.BlockSpec((1, N_ALL_TILES, KEY_CHUNK, GQ), lambda b, h, i: (h, 0, 0, 0)),
                  pl.BlockSpec((1, 2 * n_cmp_pad, GQ), lambda b, h, i: (h, 0, 0))],
        out_specs=pl.BlockSpec((1, NQ * Q_BLOCK, G * HEAD_DIM), lambda b, h, i: (b, i, h)),
        out_shape=jax.ShapeDtypeStruct((B, T, NSA_WIDTH), F32),
        scratch_shapes=[pltpu.VMEM((v_rows, step_lanes), F32), pltpu.VMEM((LANES, step_lanes), BF16),
                        pltpu.VMEM((n_sel, step_lanes), F32),
                        pltpu.VMEM((half_keys, step_lanes), F32), pltpu.VMEM((half_keys, step_lanes), F32),
                        pltpu.VMEM((half_keys, step_lanes), BF16), pltpu.VMEM((half_keys, step_lanes), BF16)],
        compiler_params=_params("parallel", "parallel", "arbitrary"), name="nsa_attention",
    )(q, kc, vcT, kaug, vsT, kaug, vwT, gates, overlapT, bias_tiles, cmp_bias)


def _rwkv_chunk_kernel(rw_ref, prev_ref, mu_ref, w0_ref, wup_ref, a0_ref, aup_ref, gup_ref, kk_ref, ka_ref,
                       rk_ref, seg_ref, tri_ref, q_ref, y0_ref, a_ref, d_ref, g_ref, bonus_ref):
    C, W, N = RWKV_CHUNK, RWKV_WIDTH, HEAD_DIM
    c = pl.program_id(1)
    x = rw_ref[0]
    R = x.shape[0]
    chunk_rows = [slice(ck * C, (ck + 1) * C) for ck in range(R // C)]
    row = lax.broadcasted_iota(jnp.int32, (R, 1), 0)
    last_prev = jnp.where(c == 0, 0.0, prev_ref[0, SUBLANES - 1:SUBLANES, :])
    x_prev = jnp.where(row == 0, last_prev, pltpu.roll(x, 1, axis=0))
    xs = x + (x_prev - x) * mu_ref[...]
    r, k, v = xs[:, 0:W], xs[:, W:2 * W], xs[:, 2 * W:3 * W]
    o = 3 * W
    wd, ad, gd = xs[:, o:o + DECAY_LORA], xs[:, o + DECAY_LORA:o + DECAY_LORA + AAA_LORA], \
        xs[:, o + DECAY_LORA + AAA_LORA:]
    w_log = -jax.nn.softplus(-(w0_ref[...] + _dot(jnp.tanh(wd).astype(BF16), wup_ref[...]))) - 0.5
    lw = -jnp.exp(w_log)
    lr = jax.nn.sigmoid(a0_ref[...] + _dot(ad.astype(BF16), aup_ref[...]))
    g_ref[0] = _dot(jax.nn.sigmoid(gd).astype(BF16), gup_ref[...])
    kk = k * kk_ref[...]
    def head_sums(z):
        return jnp.concatenate([_dot_f32_lhs(z[:, t * LANES:(t + 1) * LANES], seg_ref[...])
                                for t in range(W // LANES)], axis=1)
    kk = kk * lax.rsqrt(jnp.maximum(head_sums(kk * kk), 1e-24))
    k = k * (1.0 + (lr - 1.0) * ka_ref[...])
    bonus_ref[0] = head_sums(r * k * rk_ref[...]) * v
    a_vec, b_vec = -kk, kk * lr

    L = jnp.concatenate([_dot_f32_rhs(tri_ref[...], lw[rs]) for rs in chunk_rows], axis=0)
    L_end = jnp.concatenate([jnp.broadcast_to(L[rs.stop - 1:rs.stop, :], (C, W)) for rs in chunk_rows], axis=0)
    e_neg = jnp.exp(-L)
    e_rem = jnp.exp(L_end - L)
    At, Bt, Kt, Rt = a_vec * jnp.exp(L - lw), b_vec * e_neg, k * e_neg, r * jnp.exp(L)
    Bg, Kg = b_vec * e_rem, k * e_rem
    decay_end = [jnp.exp(L[rs.stop - 1:rs.stop, :]) for rs in chunk_rows]

    HG = RWKV_GROUP_HEADS
    GL = HG * N
    assert C == N and C & (C - 1) == 0
    blk_of = lambda idx: jnp.right_shift(idx, int(math.log2(C)))
    same_head = (blk_of(lax.broadcasted_iota(jnp.int32, (HG * C, GL), 0))
                 == blk_of(lax.broadcasted_iota(jnp.int32, (HG * C, GL), 1)))
    bf = lambda z: z.astype(BF16)
    block_diag = lambda y: jnp.where(same_head, jnp.concatenate([bf(y)] * HG, axis=0), 0.0)
    ri = lax.broadcasted_iota(jnp.int32, (C, GL), 0)
    cj = jnp.bitwise_and(lax.broadcasted_iota(jnp.int32, (C, GL), 1), C - 1)
    strict, incl = ri > cj, ri >= cj
    eye_c = (ri == cj).astype(F32)
    pr = lax.broadcasted_iota(jnp.int32, (N, 2 * N), 0)
    pc = lax.broadcasted_iota(jnp.int32, (N, 2 * N), 1)
    first_head, pair_eye = pc < N, (pr == jnp.bitwise_and(pc, N - 1)).astype(F32)

    def same_head_blocks(z):
        return jnp.where(first_head, z[0:N], z[N:2 * N])
    items = [(ck, gp) for ck in range(len(chunk_rows)) for gp in range(RWKV_HEADS // HG)]
    sl = [(chunk_rows[ck], slice(gp * GL, (gp + 1) * GL)) for ck, gp in items]
    ar = [bf(jnp.concatenate([At[s], Rt[s]], axis=0)) for s in sl]
    zb = [_dot_nt(a, block_diag(Bt[s])) for a, s in zip(ar, sl)]
    zk = [_dot_nt(a, block_diag(Kt[s])) for a, s in zip(ar, sl)]
    n_mat = [jnp.where(strict, z[0:C], 0.0) for z in zb]
    m_mat = [jnp.where(strict, z[0:C], 0.0) for z in zk]
    v_bd = [block_diag(v[s]) for s in sl]
    mv = [_dot(bf(m), vb) for m, vb in zip(m_mat, v_bd)]
    t_inv, n_pow = [eye_c + n for n in n_mat], n_mat
    for _ in range(int(math.log2(C)) - 1):
        n_pow = [_dot(bf(n), block_diag(n)) for n in n_pow]
        t_inv = [t + _dot(bf(t), block_diag(n)) for t, n in zip(t_inv, n_pow)]
    t_bf = [bf(t) for t in t_inv]
    ta = [_dot(t, block_diag(At[s])) for t, s in zip(t_bf, sl)]
    g0 = [_dot(t, block_diag(m)) for t, m in zip(t_bf, mv)]
    p_mat = [bf(jnp.where(incl, z[C:], 0.0)) for z in zb]
    pk_mat = [bf(jnp.where(incl, z[C:], 0.0)) for z in zk]
    q_out = [Rt[s] + _dot(p, block_diag(t)) for p, t, s in zip(p_mat, ta, sl)]
    y0_out = [_dot(p, block_diag(g)) + _dot(pk, vb) for p, g, pk, vb in zip(p_mat, g0, pk_mat, v_bd)]
    bgT = [bf(Bg[s].T) for s in sl]
    a_full = [_dot(b, bf(t)) for b, t in zip(bgT, ta)]
    d_full = [_dot(b, bf(g)) + _dot(bf(Kg[s].T), bf(v[s])) for b, g, s in zip(bgT, g0, sl)]
    for n, (ck, gp) in enumerate(items):
        rows, lanes = sl[n]
        q_ref[0, rows, lanes] = q_out[n].astype(q_ref.dtype)
        y0_ref[0, rows, lanes] = y0_out[n]
        for pp in range(HG // 2):
            blk = slice(pp * 2 * N, (pp + 1) * 2 * N)
            pair_lanes = slice(lanes.start + blk.start, lanes.start + blk.stop)
            a_ref[0, ck, gp * (HG // 2) + pp] = (same_head_blocks(a_full[n][blk, blk])
                                                 + pair_eye * decay_end[ck][:, pair_lanes])
            d_ref[0, ck, gp * (HG // 2) + pp] = same_head_blocks(d_full[n][blk, blk])


def _rwkv_chunks(rw, mu, w0, w_up, a0, a_up, g_up, k_k, k_a, r_k):
    B, T, cols = rw.shape
    C, W, H, N = RWKV_CHUNK, RWKV_WIDTH, RWKV_HEADS, HEAD_DIM
    nc = T // C
    S = RWKV_STEP_CHUNKS
    R = S * C
    assert nc % S == 0
    seg = jnp.asarray(np.kron(np.eye(LANES // N), np.ones((N, N))), BF16)
    tri = jnp.asarray(np.tril(np.ones((C, C))), BF16)
    row = lambda z: z.reshape(1, -1)
    const = lambda b, c: (0, 0)
    vec = pl.BlockSpec((1, W), const)
    tok = pl.BlockSpec((1, R, W), lambda b, c: (b, c, 0))
    mat = pl.BlockSpec((1, S, H // 2, N, 2 * N), lambda b, c: (b, c, 0, 0, 0))
    return pl.pallas_call(
        _rwkv_chunk_kernel,
        grid=(B, nc // S),
        in_specs=[pl.BlockSpec((1, R, cols), lambda b, c: (b, c, 0)),
                  pl.BlockSpec((1, SUBLANES, cols), lambda b, c: (b, jnp.maximum(c * (R // SUBLANES) - 1, 0), 0)),
                  pl.BlockSpec((1, cols), const), vec,
                  pl.BlockSpec((DECAY_LORA, W), const), vec,
                  pl.BlockSpec((AAA_LORA, W), const),
                  pl.BlockSpec((GATE_LORA, W), const), vec, vec, vec,
                  pl.BlockSpec((LANES, LANES), const), pl.BlockSpec((C, C), const)],
        out_specs=[tok, tok, mat, mat, tok, tok],
        out_shape=[jax.ShapeDtypeStruct((B, T, W), BF16), jax.ShapeDtypeStruct((B, T, W), F32),
                   jax.ShapeDtypeStruct((B, nc, H // 2, N, 2 * N), F32),
                   jax.ShapeDtypeStruct((B, nc, H // 2, N, 2 * N), F32),
                   jax.ShapeDtypeStruct((B, T, W), F32), jax.ShapeDtypeStruct((B, T, W), F32)],
        compiler_params=_params("parallel", "parallel"), name="rwkv_chunks",
    )(rw, rw, row(mu), row(w0), w_up.astype(BF16), row(a0), a_up.astype(BF16), g_up.astype(BF16),
      row(k_k), row(k_a), row(r_k), seg, tri)


def _pair_block_diag(x):
    first_head = lax.broadcasted_iota(jnp.int32, x.shape, 1) < x.shape[0]
    zero = jnp.zeros_like(x)
    return jnp.concatenate([jnp.where(first_head, x, zero), jnp.where(first_head, zero, x)], axis=0)


def _pair_side_by_side(z):
    n = z.shape[0] // 2
    return jnp.where(lax.broadcasted_iota(jnp.int32, (n, 2 * n), 1) < n, z[0:n], z[n:])


def _rwkv_state_kernel(a_ref, d_ref, h_ref, h_sc):
    @pl.when(pl.program_id(0) == 0)
    def _():
        h_sc[...] = jnp.zeros(h_sc.shape, F32)

    items = [(b, pair) for b in range(a_ref.shape[0]) for pair in range(a_ref.shape[2])]
    states = [h_sc[b, pair] for b, pair in items]
    for ck in range(a_ref.shape[1]):
        for (b, pair), st in zip(items, states):
            h_ref[b, ck, pair] = _pair_side_by_side(st).astype(h_ref.dtype)
        states = [_dot_hi_lo(_pair_block_diag(a_ref[b, ck, pair]), st) + _pair_block_diag(d_ref[b, ck, pair])
                  for (b, pair), st in zip(items, states)]
    for (b, pair), st in zip(items, states):
        h_sc[b, pair] = st


def _rwkv_readout_kernel(h_ref, q_ref, y0_ref, g_ref, bonus_ref, lw_ref, lb_ref, seg_ref, o_ref):
    C = RWKV_CHUNK
    inv_n = 1.0 / HEAD_DIM
    items = [(b, ck, pair, slice(ck * C, (ck + 1) * C), slice(pair * LANES, (pair + 1) * LANES))
             for b in range(h_ref.shape[0]) for ck in range(h_ref.shape[1]) for pair in range(h_ref.shape[2])]
    ys = [_dot(q_ref[b, rows, lanes], _pair_block_diag(h_ref[b, ck, pair])) + y0_ref[b, rows, lanes]
          for b, ck, pair, rows, lanes in items]
    means = [_dot_f32_lhs(y, seg_ref[...], 2) * inv_n for y in ys]
    cen = [y - mean for y, mean in zip(ys, means)]
    var = [_dot_f32_lhs(jnp.square(c), seg_ref[...], 2) * inv_n for c in cen]
    for (b, _, _, rows, lanes), c, v in zip(items, cen, var):
        yn = c * lax.rsqrt(v + LNX_EPS) * lw_ref[:, lanes] + lb_ref[:, lanes]
        o_ref[b, rows, lanes] = (yn + bonus_ref[b, rows, lanes]) * g_ref[b, rows, lanes]


def _rwkv_scan(A, D, Q, Y0, g, bonus, lnx_w, lnx_b):
    B, nc, P, N, N2 = A.shape
    T, W, C = Q.shape[1], Q.shape[2], RWKV_CHUNK
    S, SR = RWKV_SCAN_STEP_CHUNKS, RWKV_READOUT_STEP_CHUNKS
    assert nc % S == 0 and nc % SR == 0
    mat = lambda s: pl.BlockSpec((B, s, P, N, N2), lambda c: (0, c, 0, 0, 0))
    h_start = pl.pallas_call(
        _rwkv_state_kernel,
        grid=(nc // S,),
        in_specs=[mat(S), mat(S)],
        out_specs=mat(S),
        out_shape=jax.ShapeDtypeStruct((B, nc, P, N, N2), BF16),
        scratch_shapes=[pltpu.VMEM((B, P, N2, N2), F32)],
        compiler_params=_params("arbitrary"), name="rwkv_state",
    )(A, D)
    tok = pl.BlockSpec((B, SR * C, W), lambda c: (0, c, 0))
    vec = pl.BlockSpec((1, W), lambda c: (0, 0))
    seg = jnp.asarray(np.kron(np.eye(2), np.ones((HEAD_DIM, HEAD_DIM))), BF16)
    return pl.pallas_call(
        _rwkv_readout_kernel,
        grid=(nc // SR,),
        in_specs=[mat(SR), tok, tok, tok, tok, vec, vec, pl.BlockSpec((N2, N2), lambda c: (0, 0))],
        out_specs=tok,
        out_shape=jax.ShapeDtypeStruct((B, T, W), F32),
        compiler_params=_params("parallel"), name="rwkv_readout",
    )(h_start, Q, Y0, g, bonus, lnx_w.reshape(1, W), lnx_b.reshape(1, W), seg)


def _mix_xattn_kernel(x_ref, on_ref, or_ref, wo1_ref, wo2_ref, gx_ref, wq_ref, k_ref, v_ref, wo_ref, o_ref):
    x1 = x_ref[0] + _dot(on_ref[0].astype(BF16), wo1_ref[...]) + _dot(or_ref[0].astype(BF16), wo2_ref[...])
    q = _dot(_rms(x1, gx_ref[...]).astype(BF16), wq_ref[...])
    dh = q.shape[-1] // XATTN_HEADS
    qbf = (q * (dh ** -0.5)).astype(BF16)
    outs = []
    for h in range(XATTN_HEADS):
        hs = slice(h * dh, (h + 1) * dh)
        s = _dot_nt(qbf[:, hs], k_ref[0, :, hs])
        e = jnp.exp(s - jnp.max(s, axis=-1, keepdims=True))
        p = e / jnp.sum(e, axis=-1, keepdims=True)
        outs.append(_dot(p.astype(BF16), v_ref[0, :, hs]))
    o = jnp.concatenate(outs, axis=1).astype(BF16)
    o_ref[0] = x1 + _dot(o, wo_ref[...])


def _mix_xattn(x, o_nsa, o_rwkv, w_out, norm_x_g, w_q, mem_k, mem_v, w_o):
    B, T, D = x.shape
    M = mem_k.shape[1]
    tm = min(ROW_TILE, T)
    wo1, wo2 = w_out[:NSA_WIDTH].astype(BF16), w_out[NSA_WIDTH:].astype(BF16)
    const = lambda b, i: (0, 0)
    tile = lambda w: pl.BlockSpec((1, tm, w), lambda b, i: (b, i, 0))
    return pl.pallas_call(
        _mix_xattn_kernel,
        grid=(B, T // tm),
        in_specs=[tile(D), tile(NSA_WIDTH), tile(RWKV_WIDTH),
                  pl.BlockSpec(wo1.shape, const), pl.BlockSpec(wo2.shape, const),
                  pl.BlockSpec((1, D), const), pl.BlockSpec((D, D), const),
                  pl.BlockSpec((1, M, D), lambda b, i: (b, 0, 0)), pl.BlockSpec((1, M, D), lambda b, i: (b, 0, 0)),
                  pl.BlockSpec((D, D), const)],
        out_specs=tile(D),
        out_shape=jax.ShapeDtypeStruct((B, T, D), F32),
        compiler_params=_params("parallel", "parallel"), name="mix_xattn",
    )(x, o_nsa, o_rwkv, wo1, wo2, norm_x_g.reshape(1, D), w_q.astype(BF16), mem_k, mem_v, w_o.astype(BF16))


def _ffn_kernel(x_ref, g_ref, wg_ref, wu_ref, wd_ref, gf_ref, o_ref, *, final_norm):
    x = x_ref[...]
    h = _rms(x, g_ref[...]).astype(BF16)
    act = (jax.nn.silu(_dot(h, wg_ref[...])) * _dot(h, wu_ref[...])).astype(BF16)
    y = x + _dot(act, wd_ref[...])
    o_ref[...] = _rms(y, gf_ref[...]) if final_norm else y


def _ffn(x, norm_g, w_gate, w_up, w_down, final_g, final_norm):
    R, D = x.shape
    F = w_gate.shape[1]
    tm = min(FFN_ROW_TILE, R)
    const = lambda i: (0, 0)
    resident = lambda shape: pl.BlockSpec(shape, const, pipeline_mode=pl.Buffered(1))
    return pl.pallas_call(
        functools.partial(_ffn_kernel, final_norm=final_norm),
        grid=(R // tm,),
        in_specs=[pl.BlockSpec((tm, D), lambda i: (i, 0)), pl.BlockSpec((1, D), const),
                  resident((D, F)), resident((D, F)), resident((F, D)), pl.BlockSpec((1, D), const)],
        out_specs=pl.BlockSpec((tm, D), lambda i: (i, 0)),
        out_shape=jax.ShapeDtypeStruct((R, D), F32),
        compiler_params=_params("parallel"), name="ffn",
    )(x, norm_g.reshape(1, D), w_gate.astype(BF16), w_up.astype(BF16), w_down.astype(BF16),
      final_g.reshape(1, D))


def _overlap_matrix(n_cmp_pad, n_sel):
    c = np.arange(n_cmp_pad)[:, None] * CMP_STRIDE
    s = np.arange(n_sel)[None, :] * SEL_BLOCK
    return ((c <= s + SEL_BLOCK - 1) & (c + CMP_LEN - 1 >= s)).astype(np.float32)


def _layer(x, mem, rel_bias, final_g, is_last, norm_mix_g, w_in, nsa_gate_b, cmp_pe_k, cmp_pe_v,
           cmp_k_w1, cmp_k_b1, cmp_k_w2, cmp_v_w1, cmp_v_b1, cmp_v_w2,
           rwkv_mu, rwkv_w0, rwkv_w_up, rwkv_a0, rwkv_a_up, rwkv_g_up,
           rwkv_k_k, rwkv_k_a, rwkv_r_k, rwkv_lnx_w, rwkv_lnx_b, w_out,
           norm_x_g, norm_mem_g, w_q_x, w_kv_x, w_o_x, norm_ffn_g, w_gate, w_up, w_down):
    B, T, D = x.shape
    Hkv, G, dh = NSA_KV_HEADS, NSA_GROUP, HEAD_DIM
    q, kvc, kaug, vsT, vwT, gates, rw = _proj_in(x, norm_mix_g, w_in, nsa_gate_b)

    n16 = T // CMP_STRIDE
    kvc = kvc.reshape(B, T, 2 * KV_WIDTH)
    kc = _compress(kvc, 0, cmp_pe_k, cmp_k_w1, cmp_k_b1, cmp_k_w2, False)
    vcT = _compress(kvc, 1, cmp_pe_v, cmp_v_w1, cmp_v_b1, cmp_v_w2, True)
    bias_tiles, cmp_bias = _bias_tiles(rel_bias, n16)
    o_nsa = _nsa_attention(q, kc, vcT, kaug.reshape(B, T, 2 * Hkv * LANES), vsT, vwT,
                           gates, jnp.asarray(_overlap_matrix(n16, T // SEL_BLOCK).T, BF16), bias_tiles, cmp_bias)

    Q, Y0, A, Dm, g, bonus = _rwkv_chunks(rw.reshape(B, T, RWKV_COLS), rwkv_mu, rwkv_w0, rwkv_w_up, rwkv_a0,
                                          rwkv_a_up, rwkv_g_up, rwkv_k_k, rwkv_k_a, rwkv_r_k.reshape(-1))
    o_rwkv = _rwkv_scan(A, Dm, Q, Y0, g, bonus, rwkv_lnx_w, rwkv_lnx_b)

    M = mem.shape[1]
    (kv_mem,) = _norm_matmul(mem.reshape(B * M, D), norm_mem_g, [w_kv_x.astype(BF16)], [None], [BF16], ROW_TILE)
    kv_mem = kv_mem.reshape(B, M, 2 * D)
    x = _mix_xattn(x, o_nsa, o_rwkv, w_out, norm_x_g, w_q_x, kv_mem[..., :D], kv_mem[..., D:], w_o_x)
    x = _ffn(x.reshape(B * T, D), norm_ffn_g, w_gate, w_up, w_down, final_g, is_last)
    return x.reshape(B, T, D)


def kernel(x, mem, rel_bias, norm_f_g, norm_mix_g, w_in, nsa_gate_b, cmp_pe_k, cmp_pe_v, cmp_k_w1, cmp_k_b1, cmp_k_w2, cmp_v_w1, cmp_v_b1, cmp_v_w2, rwkv_mu, rwkv_w0, rwkv_w_up, rwkv_a0, rwkv_a_up, rwkv_g_up, rwkv_k_k, rwkv_k_a, rwkv_r_k, rwkv_lnx_w, rwkv_lnx_b, w_out, norm_x_g, norm_mem_g, w_q_x, w_kv_x, w_o_x, norm_ffn_g, w_gate, w_up, w_down):
    stacked = (norm_mix_g, w_in, nsa_gate_b, cmp_pe_k, cmp_pe_v, cmp_k_w1, cmp_k_b1, cmp_k_w2, cmp_v_w1,
               cmp_v_b1, cmp_v_w2, rwkv_mu, rwkv_w0, rwkv_w_up, rwkv_a0, rwkv_a_up, rwkv_g_up, rwkv_k_k,
               rwkv_k_a, rwkv_r_k, rwkv_lnx_w, rwkv_lnx_b, w_out, norm_x_g, norm_mem_g, w_q_x, w_kv_x, w_o_x,
               norm_ffn_g, w_gate, w_up, w_down)
    depth = w_in.shape[0]
    for l in range(depth):
        x = _layer(x, mem, rel_bias, norm_f_g, l == depth - 1, *[p[l] for p in stacked])
    return x
```

```python
import functools
import math

import numpy as np
import jax
import jax.numpy as jnp
from jax import lax
from jax.experimental import pallas as pl
from jax.experimental.pallas import tpu as pltpu

F32 = jnp.float32
BF16 = jnp.bfloat16

LANES = 128
SUBLANES = 8
BF16_ROWS = 16
VMEM_LIMIT_BYTES = 56 * 1024 * 1024

HEAD_DIM = 64
NSA_HEADS = 8
NSA_KV_HEADS = 2
NSA_GROUP = NSA_HEADS // NSA_KV_HEADS
NSA_WIDTH = NSA_HEADS * HEAD_DIM
KV_WIDTH = NSA_KV_HEADS * HEAD_DIM
RWKV_HEADS = 8
RWKV_WIDTH = RWKV_HEADS * HEAD_DIM
CMP_LEN = 32
CMP_STRIDE = 16
SEL_BLOCK = 64
SEL_SHIFT = 6
SEL_TOP = 16
WINDOW = 512
Q_BLOCK = 128
DECAY_LORA = 64
AAA_LORA = 64
GATE_LORA = 128
N_BUCKETS = 32
MAX_DISTANCE = 2048
XATTN_HEADS = 4
RMS_EPS = 1e-6
LNX_EPS = 64e-5
FORCE_SCORE = 1e4
NEG_SCORE = -1e9
MASK_SCORE = -1e30
LOG2E = math.log2(math.e)
RWKV_COLS = 3 * RWKV_WIDTH + DECAY_LORA + AAA_LORA + GATE_LORA
NSA_COLS = NSA_WIDTH + 6 * KV_WIDTH + 3 * NSA_HEADS

KEY_CHUNK = 128
RWKV_CHUNK = 64
RWKV_SCAN_STEP_CHUNKS = 8
RWKV_READOUT_STEP_CHUNKS = 4
RWKV_GROUP_HEADS = 2
RWKV_STEP_CHUNKS = 8
ROW_TILE = 512
FFN_ROW_TILE = 512


def _t5_thresholds():
    d = np.arange(0, 2 * MAX_DISTANCE, dtype=np.int64)
    max_exact = N_BUCKETS // 2
    nf = np.maximum(d, 1).astype(np.float32)
    large = max_exact + (np.log(nf / np.float32(max_exact)) / np.float32(math.log(MAX_DISTANCE / max_exact))
                         * np.float32(N_BUCKETS - max_exact)).astype(np.int32)
    bucket = np.where(d < max_exact, d, np.minimum(large, N_BUCKETS - 1))
    return [int(np.argmax(bucket >= k)) for k in range(N_BUCKETS)]


T5_THRESHOLDS = _t5_thresholds()
N_BIAS_TILES = -(-(T5_THRESHOLDS[-1] + KEY_CHUNK) // KEY_CHUNK) + 1
TILE_MASKED = N_BIAS_TILES
TILE_WINDOW_EDGE = N_BIAS_TILES + 1
N_ALL_TILES = N_BIAS_TILES + 2
SEL_STEP_BLOCKS = 16
SEL_STAGE_TILES = 4
SEL_GROUP_STAGES = SEL_STEP_BLOCKS * SEL_BLOCK // (SEL_STAGE_TILES * KEY_CHUNK)
NSA_STEP_QBLOCKS = 4
V_ROWS = HEAD_DIM + BF16_ROWS
GATE_ROWS = 16


def _params(*semantics):
    return pltpu.CompilerParams(dimension_semantics=semantics, vmem_limit_bytes=VMEM_LIMIT_BYTES)


def _rms(x, g):
    return x * lax.rsqrt(jnp.mean(x * x, axis=-1, keepdims=True) + RMS_EPS) * g


def _dot(a, b, **kw):
    return jnp.dot(a, b, preferred_element_type=F32, **kw)


def _split3(x):
    hi = x.astype(BF16)
    r1 = x - hi.astype(F32)
    mid = r1.astype(BF16)
    lo = (r1 - mid.astype(F32)).astype(BF16)
    return hi, mid, lo


def _dot_f32_lhs(x, w01, pieces=3):
    w = w01.astype(BF16)
    hi, mid, lo = _split3(x)
    return _dot(hi, w) + (_dot(mid, w) + _dot(lo, w) if pieces == 3 else _dot(mid, w))


def _dot_f32_rhs(w01, x):
    w = w01.astype(BF16)
    hi, mid, lo = _split3(x)
    return _dot(w, hi) + (_dot(w, mid) + _dot(w, lo))


def _dot_hi_lo(a, b):
    a_hi = a.astype(BF16)
    a_lo = (a - a_hi.astype(F32)).astype(BF16)
    b_hi = b.astype(BF16)
    b_lo = (b - b_hi.astype(F32)).astype(BF16)
    return _dot(a_hi, b_hi) + (_dot(a_hi, b_lo) + _dot(a_lo, b_hi))


def _dot_nt(a, b, **kw):
    return lax.dot_general(a, b, (((1,), (1,)), ((), ())), preferred_element_type=F32, **kw)


def _norm_matmul_kernel(x_ref, g_ref, *refs, nseg, bias_flags):
    nb = sum(bias_flags)
    w_refs, b_refs, o_refs = refs[:nseg], refs[nseg:nseg + nb], refs[nseg + nb:]
    xn = _rms(x_ref[...], g_ref[...]).astype(BF16)
    bi = 0
    for s in range(nseg):
        y = _dot(xn, w_refs[s][...])
        if bias_flags[s]:
            y = y + b_refs[bi][...]
            bi += 1
        o_refs[s][...] = y.astype(o_refs[s].dtype)


def _norm_matmul(x, g, weights, biases, out_dtypes, row_tile):
    R, D = x.shape
    tm = min(row_tile, R)
    assert R % tm == 0
    nseg = len(weights)
    bias_flags = tuple(b is not None for b in biases)
    const = lambda i: (0, 0)
    in_specs = [pl.BlockSpec((tm, D), lambda i: (i, 0)), pl.BlockSpec((1, D), const)]
    in_specs += [pl.BlockSpec(w.shape, const) for w in weights]
    in_specs += [pl.BlockSpec((1, b.shape[-1]), const) for b in biases if b is not None]
    out_specs = [pl.BlockSpec((tm, w.shape[1]), lambda i: (i, 0)) for w in weights]
    out_shape = [jax.ShapeDtypeStruct((R, w.shape[1]), dt) for w, dt in zip(weights, out_dtypes)]
    return pl.pallas_call(
        functools.partial(_norm_matmul_kernel, nseg=nseg, bias_flags=bias_flags),
        grid=(R // tm,), in_specs=in_specs, out_specs=out_specs, out_shape=out_shape,
        compiler_params=_params("parallel"), name="norm_matmul",
    )(x, g.reshape(1, D), *weights, *[b.reshape(1, -1) for b in biases if b is not None])


def _proj_in_kernel(x_ref, g_ref, wq_ref, wc_ref, wk_ref, wvT_ref, wgT_ref, bg_ref, wr_ref,
                    q_ref, kvc_ref, kaug_ref, vsT_ref, vwT_ref, gate_ref, rw_ref, *, seq_len):
    tm = x_ref.shape[0]
    xn = _rms(x_ref[...], g_ref[...]).astype(BF16)
    qT = (_dot_nt(wq_ref[...], xn) * (HEAD_DIM ** -0.5 * LOG2E)).astype(BF16)
    for j in range(tm // Q_BLOCK):
        for hg in range(NSA_HEADS):
            h, g = divmod(hg, NSA_GROUP)
            q_ref[0, j, h, :, g * Q_BLOCK:(g + 1) * Q_BLOCK] = qT[hg * HEAD_DIM:(hg + 1) * HEAD_DIM,
                                                                  j * Q_BLOCK:(j + 1) * Q_BLOCK]
    kvc_ref[0] = _dot(xn, wc_ref[...])
    rw_ref[0] = _dot(xn, wr_ref[...])
    k_all = _dot(xn, wk_ref[...])
    tok = lax.rem(pl.program_id(0) * tm, seq_len) + lax.broadcasted_iota(jnp.int32, k_all.shape, 0)
    lane = lax.broadcasted_iota(jnp.int32, k_all.shape, 1)
    blk = jnp.bitwise_and(jnp.right_shift(tok, SEL_SHIFT), SEL_STEP_BLOCKS - 1)
    hot = (jnp.bitwise_and(lane, LANES - 1) == HEAD_DIM + blk) & (lane < NSA_KV_HEADS * LANES)
    kaug_ref[0] = jnp.where(hot, 1.0, k_all).astype(BF16)
    vT = _dot_nt(wvT_ref[...], xn)
    row = lax.broadcasted_iota(jnp.int32, vT.shape, 0)
    ones_row = row == HEAD_DIM
    for grp in range(1, 2 * NSA_KV_HEADS):
        ones_row = ones_row | (row == grp * V_ROWS + HEAD_DIM)
    vT = jnp.where(ones_row, 1.0, vT).astype(BF16)
    half = NSA_KV_HEADS * V_ROWS
    stage_keys = vsT_ref.shape[3]
    for c in range(tm // stage_keys):
        vsT_ref[0, c] = vT[0:half, c * stage_keys:(c + 1) * stage_keys]
    for c in range(tm // KEY_CHUNK):
        vwT_ref[0, c] = vT[half:, c * KEY_CHUNK:(c + 1) * KEY_CHUNK]
    gT = _dot_nt(wgT_ref[...], xn) + bg_ref[...]
    for j in range(tm // Q_BLOCK):
        for h in range(NSA_KV_HEADS):
            gate_ref[0, j, h] = gT[h * GATE_ROWS:(h + 1) * GATE_ROWS, j * Q_BLOCK:(j + 1) * Q_BLOCK]


def _proj_in(x, norm_g, w_in, gate_b):
    B, T, D = x.shape
    Hkv, G, dh = NSA_KV_HEADS, NSA_GROUP, HEAD_DIM
    tm = ROW_TILE
    stage_keys = SEL_STAGE_TILES * KEY_CHUNK
    assert T % tm == 0 and tm % Q_BLOCK == 0 and tm % stage_keys == 0
    kv0 = NSA_WIDTH
    g0 = kv0 + 6 * KV_WIDTH
    stream = lambda s: w_in[:, kv0 + s * KV_WIDTH:kv0 + (s + 1) * KV_WIDTH].reshape(D, Hkv, dh)
    pad_cols = lambda w: jnp.pad(w, ((0, 0), (0, 0), (0, LANES - dh))).reshape(D, Hkv * LANES)
    pad_rows = lambda w: jnp.pad(w.transpose(1, 2, 0), ((0, 0), (0, V_ROWS - dh), (0, 0))).reshape(Hkv * V_ROWS, D)
    w_k = jnp.concatenate([pad_cols(stream(2)), pad_cols(stream(4))], axis=1)
    w_vT = jnp.concatenate([pad_rows(stream(3)), pad_rows(stream(5))], axis=0)
    reorder = lambda a: a.reshape(-1, Hkv, G, 3).transpose(1, 3, 2, 0).reshape(Hkv, 3 * G, -1)
    pad_gate = lambda a: jnp.pad(a, ((0, 0), (0, GATE_ROWS - 3 * G), (0, 0))).reshape(Hkv * GATE_ROWS, -1)
    w_gT = pad_gate(reorder(w_in[:, g0:NSA_COLS]))
    b_g = pad_gate(reorder(gate_b.reshape(1, -1)))
    weights = [w_in[:, :kv0].T, w_in[:, kv0:kv0 + 2 * KV_WIDTH], w_k, w_vT, w_gT]
    weights = [w.astype(BF16) for w in weights] + [b_g, w_in[:, NSA_COLS:].astype(BF16)]
    nt = T // tm
    rows = lambda n: pl.BlockSpec((1, tm, n), lambda i: (i // nt, i % nt, 0))
    const = lambda i: (0, 0)
    return pl.pallas_call(
        functools.partial(_proj_in_kernel, seq_len=T),
        grid=(B * nt,),
        in_specs=[pl.BlockSpec((tm, D), lambda i: (i, 0)), pl.BlockSpec((1, D), const)]
        + [pl.BlockSpec(w.shape, const) for w in weights],
        out_specs=[pl.BlockSpec((1, tm // Q_BLOCK, Hkv, dh, G * Q_BLOCK), lambda i: (i // nt, i % nt, 0, 0, 0)),
                   rows(2 * KV_WIDTH), rows(2 * Hkv * LANES),
                   pl.BlockSpec((1, tm // stage_keys, Hkv * V_ROWS, stage_keys), lambda i: (i // nt, i % nt, 0, 0)),
                   pl.BlockSpec((1, tm // KEY_CHUNK, Hkv * V_ROWS, KEY_CHUNK), lambda i: (i // nt, i % nt, 0, 0)),
                   pl.BlockSpec((1, tm // Q_BLOCK, Hkv, GATE_ROWS, Q_BLOCK), lambda i: (i // nt, i % nt, 0, 0, 0)),
                   rows(RWKV_COLS)],
        out_shape=[jax.ShapeDtypeStruct((B, T // Q_BLOCK, Hkv, dh, G * Q_BLOCK), BF16),
                   jax.ShapeDtypeStruct((B, T, 2 * KV_WIDTH), F32),
                   jax.ShapeDtypeStruct((B, T, 2 * Hkv * LANES), BF16),
                   jax.ShapeDtypeStruct((B, T // stage_keys, Hkv * V_ROWS, stage_keys), BF16),
                   jax.ShapeDtypeStruct((B, T // KEY_CHUNK, Hkv * V_ROWS, KEY_CHUNK), BF16),
                   jax.ShapeDtypeStruct((B, T // Q_BLOCK, Hkv, GATE_ROWS, Q_BLOCK), F32),
                   jax.ShapeDtypeStruct((B, T, RWKV_COLS), F32)],
        compiler_params=_params("parallel"), name="proj_in",
    )(x.reshape(B * T, D), norm_g.reshape(1, D), *weights)


def _compress_kernel(x_ref, pe_ref, w1_ref, b1_ref, w2_ref, o_ref, *, transpose_out):
    n16 = x_ref.shape[1] // CMP_STRIDE
    hidden = w1_ref.shape[2] // NSA_KV_HEADS
    lo = jnp.zeros((n16, w1_ref.shape[2]), F32)
    hi = jnp.zeros((n16, w1_ref.shape[2]), F32)
    for l in range(CMP_STRIDE):
        rows = x_ref[0, pl.ds(l, n16, stride=CMP_STRIDE), :]
        lo = lo + _dot((rows + pe_ref[l:l + 1, :]).astype(BF16), w1_ref[l])
        hi = hi + _dot((rows + pe_ref[CMP_STRIDE + l:CMP_STRIDE + l + 1, :]).astype(BF16), w1_ref[CMP_STRIDE + l])
    h = lo + pltpu.roll(hi, n16 - 1, axis=0) + b1_ref[...]
    h = jax.nn.gelu(h).astype(BF16)
    for hkv in range(NSA_KV_HEADS):
        hh = h[:, hkv * hidden:(hkv + 1) * hidden]
        if transpose_out:
            o_ref[0, hkv] = _dot_nt(w2_ref[...], hh).astype(o_ref.dtype)
        else:
            o_ref[0, hkv] = _dot(hh, w2_ref[...]).astype(o_ref.dtype)


def _compress(kvc, stream, pe, w1, b1, w2, transpose_out):
    B, T, _ = kvc.shape
    H, dh = NSA_KV_HEADS, HEAD_DIM
    n16 = T // CMP_STRIDE
    hidden = w1.shape[1]
    w1r = w1.astype(BF16).reshape(CMP_LEN, dh, hidden)
    w1_bd = jnp.concatenate([jnp.pad(w1r, ((0, 0), (0, 0), (h * hidden, (H - 1 - h) * hidden))) for h in range(H)],
                            axis=1)
    w2b = (w2.T if transpose_out else w2).astype(BF16)
    oshape = (B, H, dh, n16) if transpose_out else (B, H, n16, dh)
    return pl.pallas_call(
        functools.partial(_compress_kernel, transpose_out=transpose_out),
        grid=(B,),
        in_specs=[pl.BlockSpec((1, T, H * dh), lambda b: (b, 0, stream)),
                  pl.BlockSpec((CMP_LEN, H * dh), lambda b: (0, 0)),
                  pl.BlockSpec(w1_bd.shape, lambda b: (0, 0, 0)),
                  pl.BlockSpec((1, H * hidden), lambda b: (0, 0)),
                  pl.BlockSpec(w2b.shape, lambda b: (0, 0))],
        out_specs=pl.BlockSpec((1,) + oshape[1:], lambda b: (b, 0, 0, 0)),
        out_shape=jax.ShapeDtypeStruct(oshape, BF16),
        compiler_params=_params("parallel"), name="nsa_compress",
    )(kvc, jnp.tile(pe, (1, H)), w1_bd, jnp.tile(b1.reshape(1, hidden), (1, H)), w2b)


def _bias_of_distance(tab_ref, h, d):
    val = jnp.full(d.shape, tab_ref[h, 0], F32)
    for k in range(1, N_BUCKETS):
        val = jnp.where(d >= T5_THRESHOLDS[k], tab_ref[h, k], val)
    return val * LOG2E


def _bias_tiles_kernel(tab_ref, bt_ref, cb_ref, *, n_cmp_pad):
    hkv = pl.program_id(0)
    j = lax.broadcasted_iota(jnp.int32, (KEY_CHUNK, Q_BLOCK), 0)
    i = lax.broadcasted_iota(jnp.int32, (KEY_CHUNK, Q_BLOCK), 1)
    r2 = lax.broadcasted_iota(jnp.int32, (2 * n_cmp_pad, Q_BLOCK), 0)
    i2 = lax.broadcasted_iota(jnp.int32, (2 * n_cmp_pad, Q_BLOCK), 1)
    l2 = r2 - (n_cmp_pad - KEY_CHUNK)
    d2 = i2 - CMP_STRIDE * l2 + (CMP_STRIDE * KEY_CHUNK - Q_BLOCK - (CMP_LEN - 1))
    hidden2 = (l2 >= KEY_CHUNK) | ((l2 >= 0) & (d2 < 0))
    d2 = jnp.where((l2 >= 0) & (l2 < KEY_CHUNK), d2, 2 * MAX_DISTANCE)
    for g in range(NSA_GROUP):
        h = hkv * NSA_GROUP + g
        lanes = slice(g * Q_BLOCK, (g + 1) * Q_BLOCK)
        for m in range(N_BIAS_TILES):
            tile = _bias_of_distance(tab_ref, h, m * KEY_CHUNK + i - j)
            if m == 0:
                tile = jnp.where(j <= i, tile, MASK_SCORE)
            bt_ref[0, m, :, lanes] = tile
        bt_ref[0, TILE_MASKED, :, lanes] = jnp.full((KEY_CHUNK, Q_BLOCK), MASK_SCORE, F32)
        edge = _bias_of_distance(tab_ref, h, WINDOW + i - j)
        bt_ref[0, TILE_WINDOW_EDGE, :, lanes] = jnp.where(j > i, edge, MASK_SCORE)
        cb_ref[0, :, lanes] = jnp.where(hidden2, MASK_SCORE, _bias_of_distance(tab_ref, h, d2))


def _bias_tiles(rel_bias, n_cmp_pad):
    assert CMP_STRIDE * KEY_CHUNK - Q_BLOCK - (CMP_LEN - 1) >= T5_THRESHOLDS[-1]
    GQ = NSA_GROUP * Q_BLOCK
    return pl.pallas_call(
        functools.partial(_bias_tiles_kernel, n_cmp_pad=n_cmp_pad),
        grid=(NSA_KV_HEADS,),
        in_specs=[pl.BlockSpec(memory_space=pltpu.SMEM)],
        out_specs=[pl.BlockSpec((1, N_ALL_TILES, KEY_CHUNK, GQ), lambda h: (h, 0, 0, 0)),
                   pl.BlockSpec((1, 2 * n_cmp_pad, GQ), lambda h: (h, 0, 0))],
        out_shape=[jax.ShapeDtypeStruct((NSA_KV_HEADS, N_ALL_TILES, KEY_CHUNK, GQ), F32),
                   jax.ShapeDtypeStruct((NSA_KV_HEADS, 2 * n_cmp_pad, GQ), F32)],
        compiler_params=_params("parallel"), name="t5_bias_tiles",
    )(rel_bias.T)


def _nsa_kernel(q_ref, kc_ref, vcT_ref, ks_ref, vsT_ref, kw_ref, vwT_ref, gate_ref, ovT_ref, bt_ref, cb_ref,
                o_ref, acc_sc, qaug_sc, seladd_sc, s0_sc, s1_sc, p0_sc, p1_sc, *, n_sel, n_cmp_pad):
    G, NQ = NSA_GROUP, NSA_STEP_QBLOCKS
    GQ = G * Q_BLOCK
    qbs = [pl.program_id(2) * NQ + x for x in range(NQ)]
    per_block = lambda fn: jnp.concatenate([fn(x) for x in range(NQ)], axis=1)
    tile_g = lambda a: jnp.concatenate([a] * G, axis=1)
    qT = per_block(lambda x: q_ref[0, x, 0])
    qaug_sc[0:HEAD_DIM, :] = qT
    qaug_sc[HEAD_DIM:, :] = jnp.zeros((qaug_sc.shape[0] - HEAD_DIM, NQ * GQ), BF16)
    lane_q = lax.broadcasted_iota(jnp.int32, (1, Q_BLOCK), 1)
    t = per_block(lambda x: qbs[x] * Q_BLOCK + lane_q)

    def bias_tile(dist_of_block):
        def one(x):
            dist = dist_of_block(x)
            return bt_ref[0, jnp.where(dist < 0, TILE_MASKED, jnp.minimum(dist, N_BIAS_TILES - 1))]
        return per_block(one)

    n_back = WINDOW // KEY_CHUNK
    q_win = qaug_sc[...]
    win = {}

    def win_scores(x, back):
        kc = jnp.maximum(qbs[x] - back, 0)
        edge = TILE_WINDOW_EDGE if back == n_back else back
        tile = jnp.where(qbs[x] >= back, edge, TILE_MASKED)
        k_chunk = kw_ref[0, pl.ds(pl.multiple_of(kc * KEY_CHUNK, KEY_CHUNK), KEY_CHUNK), :]
        win["s", x, back] = _dot(k_chunk, q_win[:, x * GQ:(x + 1) * GQ]) + bt_ref[0, tile]
        col_max = jnp.max(win["s", x, back], axis=0, keepdims=True)
        win["m", x] = jnp.maximum(win["m", x], col_max) if ("m", x) in win else col_max

    def win_weights(x, back):
        win["p", x, back] = jnp.exp2(win["s", x, back] - win["m", x]).astype(BF16)

    def win_values(x, back):
        pv = _dot(vwT_ref[0, jnp.maximum(qbs[x] - back, 0)], win["p", x, back])
        win["acc", x] = win["acc", x] + pv if ("acc", x) in win else pv

    backs = list(range(n_back, -1, -1))
    window_work = [functools.partial(fn, x, b) for x in range(NQ)
                   for fn in (win_scores, win_weights, win_values) for b in backs]

    def cmp_branch(n_rows):
        def cmp_bias(x):
            start = pl.multiple_of(n_cmp_pad - (Q_BLOCK // CMP_STRIDE) * (qbs[x] + 1), SUBLANES)
            return cb_ref[0, pl.ds(start, n_rows), :]
        s = _dot(kc_ref[0, 0, 0:n_rows, :], qT) + per_block(cmp_bias)
        e = jnp.exp2(s - jnp.maximum(jnp.max(s, axis=0, keepdims=True), 0.1 * MASK_SCORE))
        p = e * (1.0 / jnp.maximum(jnp.sum(e, axis=0, keepdims=True), 1e-30))
        o_cmp = _dot(vcT_ref[0, 0, :, 0:n_rows], p.astype(BF16))
        psum = per_block(lambda x: sum(p[:, x * GQ + g * Q_BLOCK:x * GQ + (g + 1) * Q_BLOCK] for g in range(G)))
        return o_cmp, _dot_f32_rhs(ovT_ref[:, 0:n_rows], psum)

    half_rows = n_cmp_pad // 2
    n_visible = (Q_BLOCK // CMP_STRIDE) * (qbs[-1] + 1)
    if half_rows % LANES == 0:
        o_c, imp = lax.cond(n_visible <= half_rows, lambda: cmp_branch(half_rows), lambda: cmp_branch(n_cmp_pad))
    else:
        o_c, imp = cmp_branch(n_cmp_pad)
    bj = lax.broadcasted_iota(jnp.int32, (n_sel, NQ * Q_BLOCK), 0)
    cur = jnp.right_shift(t, SEL_SHIFT)
    forced = (bj == 0) | (bj == cur) | (bj == cur - 1)
    valid = bj * SEL_BLOCK <= t
    score = jnp.where(forced, -jnp.inf, jnp.where(valid, imp, NEG_SCORE))
    bjf = bj.astype(F32)
    n_rounds = max(min(SEL_TOP, n_sel) - 3, 0)
    for rnd in range(n_rounds):
        mx = jnp.max(score, axis=0, keepdims=True)
        first = jnp.min(jnp.where(score == mx, bjf, float(n_sel)), axis=0, keepdims=True)
        score = jnp.where(bjf == first, -jnp.inf, score)
        take = -(-len(window_work) // (n_rounds - rnd))
        for piece in window_work[:take]:
            piece()
        window_work = window_work[take:]
    for piece in window_work:
        piece()
    o_w = per_block(lambda x: win["acc", x][0:HEAD_DIM] / jnp.maximum(win["acc", x][HEAD_DIM:HEAD_DIM + 1], 1e-30))
    sel_add = jnp.where(score == -jnp.inf, 0.0, MASK_SCORE)
    seladd_sc[...] = per_block(lambda x: tile_g(sel_add[:, x * Q_BLOCK:(x + 1) * Q_BLOCK]))

    stage_keys = SEL_STAGE_TILES * KEY_CHUNK
    n_stages = qbs[-1] // SEL_STAGE_TILES + 1
    last_stage = ks_ref.shape[1] // stage_keys - 1

    def scores(k, s_buf):
        kk = jnp.minimum(k, last_stage)
        blk0 = pl.multiple_of(kk // SEL_GROUP_STAGES * SEL_STEP_BLOCKS, SEL_STEP_BLOCKS)
        qaug_sc[HEAD_DIM:HEAD_DIM + SEL_STEP_BLOCKS, :] = seladd_sc[pl.ds(blk0, SEL_STEP_BLOCKS), :].astype(BF16)
        k0 = pl.multiple_of(kk * stage_keys, stage_keys)
        bias = jnp.concatenate([bias_tile(lambda x, c=c: qbs[x] - (k * SEL_STAGE_TILES + c))
                                for c in range(SEL_STAGE_TILES)], axis=0)
        s = _dot(ks_ref[0, pl.ds(k0, stage_keys), :], qaug_sc[...]) + bias
        s_buf[...] = s
        return jnp.max(s, axis=0, keepdims=True)

    def weights(m, col_max, s_buf, p_buf):
        m_new = jnp.maximum(m, col_max)
        p_buf[...] = jnp.exp2(s_buf[...] - m_new).astype(BF16)
        return m_new, jnp.exp2(m - m_new)

    def accumulate(k, alpha, p_buf):
        acc_sc[...] = alpha * acc_sc[...] + _dot(vsT_ref[0, jnp.clip(k, 0, last_stage)], p_buf[...])

    def pair(j, carry):
        m, col_max, alpha = carry
        k = 2 * j
        col_max1 = scores(k + 1, s1_sc)
        m, alpha0 = weights(m, col_max, s0_sc, p0_sc)
        accumulate(k - 1, alpha, p1_sc)
        col_max2 = scores(k + 2, s0_sc)
        m, alpha1 = weights(m, col_max1, s1_sc, p1_sc)
        accumulate(k, alpha0, p0_sc)
        return m, col_max2, alpha1

    acc_sc[...] = jnp.zeros(acc_sc.shape, F32)
    p1_sc[...] = jnp.zeros(p1_sc.shape, BF16)
    m_init = jnp.full((1, NQ * GQ), 0.1 * MASK_SCORE, F32)
    carry = (m_init, scores(0, s0_sc), jnp.ones((1, NQ * GQ), F32))
    n_pairs = n_stages // 2
    m, col_max, alpha = lax.fori_loop(0, n_pairs, pair, carry)
    last = 2 * n_pairs

    @pl.when(n_stages % 2 == 1)
    def _():
        _, alpha_last = weights(m, col_max, s0_sc, p0_sc)
        accumulate(last - 1, alpha, p1_sc)
        accumulate(last, alpha_last, p0_sc)

    @pl.when(n_stages % 2 == 0)
    def _():
        accumulate(last - 1, alpha, p1_sc)
    o_s = acc_sc[0:HEAD_DIM, :] / jnp.maximum(acc_sc[HEAD_DIM:HEAD_DIM + 1, :], 1e-30)

    def gate_row(br):
        def one(x):
            gt = jax.nn.sigmoid(gate_ref[0, x, 0])
            return jnp.concatenate([gt[br * G + g:br * G + g + 1, :] for g in range(G)], axis=1)
        return per_block(one)
    o = gate_row(0) * o_c + gate_row(1) * o_s + gate_row(2) * o_w
    for x in range(NQ):
        o_ref[0, x * Q_BLOCK:(x + 1) * Q_BLOCK, :] = jnp.concatenate(
            [o[:, x * GQ + g * Q_BLOCK:x * GQ + (g + 1) * Q_BLOCK].T for g in range(G)], axis=1)


def _nsa_attention(q, kc, vcT, kaug, vsT, vwT, gates, overlapT, bias_tiles, cmp_bias):
    B, T, _ = kaug.shape
    Hkv, G = NSA_KV_HEADS, NSA_GROUP
    GQ = G * Q_BLOCK
    nqb = T // Q_BLOCK
    n_cmp_pad = kc.shape[2]
    n_sel = T // SEL_BLOCK
    half_keys = SEL_STAGE_TILES * KEY_CHUNK
    NQ = NSA_STEP_QBLOCKS
    assert T % (SEL_GROUP_STAGES * half_keys) == 0 and nqb % NQ == 0
    assert FORCE_SCORE > NSA_GROUP
    per_head = lambda b, h, i: (b, h, 0, 0)
    v_rows = V_ROWS
    chunked = lambda w: pl.BlockSpec((1, T // w, v_rows, w), lambda b, h, i: (b, 0, h, 0))
    step_lanes = NQ * GQ
    return pl.pallas_call(
        functools.partial(_nsa_kernel, n_sel=n_sel, n_cmp_pad=n_cmp_pad),
        grid=(B, Hkv, nqb // NQ),
        in_specs=[pl.BlockSpec((1, NQ, 1, HEAD_DIM, GQ), lambda b, h, i: (b, i, h, 0, 0)),
                  pl.BlockSpec((1, 1, n_cmp_pad, HEAD_DIM), per_head),
                  pl.BlockSpec((1, 1, HEAD_DIM, n_cmp_pad), per_head),
                  pl.BlockSpec((1, T, LANES), lambda b, h, i: (b, 0, h)), chunked(half_keys),
                  pl.BlockSpec((1, T, LANES), lambda b, h, i: (b, 0, Hkv + h)), chunked(KEY_CHUNK),
                  pl.BlockSpec((1, NQ, 1, GATE_ROWS, Q_BLOCK), lambda b, h, i: (b, i, h, 0, 0)),
                  pl.BlockSpec(overlapT.shape, lambda b, h, i: (0, 0)),
                  pl.BlockSpec((1, N_ALL_TILES, KEY_CHUNK, GQ), lambda b, h, i: (h, 0, 0, 0)),
                  pl.BlockSpec((1, 2 * n_cmp_pad, GQ), lambda b, h, i: (h, 0, 0))],
        out_specs=pl.BlockSpec((1, NQ * Q_BLOCK, G * HEAD_DIM), lambda b, h, i: (b, i, h)),
        out_shape=jax.ShapeDtypeStruct((B, T, NSA_WIDTH), F32),
        scratch_shapes=[pltpu.VMEM((v_rows, step_lanes), F32), pltpu.VMEM((LANES, step_lanes), BF16),
                        pltpu.VMEM((n_sel, step_lanes), F32),
                        pltpu.VMEM((half_keys, step_lanes), F32), pltpu.VMEM((half_keys, step_lanes), F32),
                        pltpu.VMEM((half_keys, step_lanes), BF16), pltpu.VMEM((half_keys, step_lanes), BF16)],
        compiler_params=_params("parallel", "parallel", "arbitrary"), name="nsa_attention",
    )(q, kc, vcT, kaug, vsT, kaug, vwT, gates, overlapT, bias_tiles, cmp_bias)


def _rwkv_chunk_kernel(rw_ref, prev_ref, mu_ref, w0_ref, wup_ref, a0_ref, aup_ref, gup_ref, kk_ref, ka_ref,
                       rk_ref, seg_ref, tri_ref, q_ref, y0_ref, a_ref, d_ref, g_ref, bonus_ref):
    C, W, N = RWKV_CHUNK, RWKV_WIDTH, HEAD_DIM
    c = pl.program_id(1)
    x = rw_ref[0]
    R = x.shape[0]
    chunk_rows = [slice(ck * C, (ck + 1) * C) for ck in range(R // C)]
    row = lax.broadcasted_iota(jnp.int32, (R, 1), 0)
    last_prev = jnp.where(c == 0, 0.0, prev_ref[0, SUBLANES - 1:SUBLANES, :])
    x_prev = jnp.where(row == 0, last_prev, pltpu.roll(x, 1, axis=0))
    xs = x + (x_prev - x) * mu_ref[...]
    r, k, v = xs[:, 0:W], xs[:, W:2 * W], xs[:, 2 * W:3 * W]
    o = 3 * W
    wd, ad, gd = xs[:, o:o + DECAY_LORA], xs[:, o + DECAY_LORA:o + DECAY_LORA + AAA_LORA], \
        xs[:, o + DECAY_LORA + AAA_LORA:]
    w_log = -jax.nn.softplus(-(w0_ref[...] + _dot(jnp.tanh(wd).astype(BF16), wup_ref[...]))) - 0.5
    lw = -jnp.exp(w_log)
    lr = jax.nn.sigmoid(a0_ref[...] + _dot(ad.astype(BF16), aup_ref[...]))
    g_ref[0] = _dot(jax.nn.sigmoid(gd).astype(BF16), gup_ref[...])
    kk = k * kk_ref[...]
    def head_sums(z):
        return jnp.concatenate([_dot_f32_lhs(z[:, t * LANES:(t + 1) * LANES], seg_ref[...])
                                for t in range(W // LANES)], axis=1)
    kk = kk * lax.rsqrt(jnp.maximum(head_sums(kk * kk), 1e-24))
    k = k * (1.0 + (lr - 1.0) * ka_ref[...])
    bonus_ref[0] = head_sums(r * k * rk_ref[...]) * v
    a_vec, b_vec = -kk, kk * lr

    L = jnp.concatenate([_dot_f32_rhs(tri_ref[...], lw[rs]) for rs in chunk_rows], axis=0)
    L_end = jnp.concatenate([jnp.broadcast_to(L[rs.stop - 1:rs.stop, :], (C, W)) for rs in chunk_rows], axis=0)
    e_neg = jnp.exp(-L)
    e_rem = jnp.exp(L_end - L)
    At, Bt, Kt, Rt = a_vec * jnp.exp(L - lw), b_vec * e_neg, k * e_neg, r * jnp.exp(L)
    Bg, Kg = b_vec * e_rem, k * e_rem
    decay_end = [jnp.exp(L[rs.stop - 1:rs.stop, :]) for rs in chunk_rows]

    HG = RWKV_GROUP_HEADS
    GL = HG * N
    assert C == N and C & (C - 1) == 0
    blk_of = lambda idx: jnp.right_shift(idx, int(math.log2(C)))
    same_head = (blk_of(lax.broadcasted_iota(jnp.int32, (HG * C, GL), 0))
                 == blk_of(lax.broadcasted_iota(jnp.int32, (HG * C, GL), 1)))
    bf = lambda z: z.astype(BF16)
    block_diag = lambda y: jnp.where(same_head, jnp.concatenate([bf(y)] * HG, axis=0), 0.0)
    ri = lax.broadcasted_iota(jnp.int32, (C, GL), 0)
    cj = jnp.bitwise_and(lax.broadcasted_iota(jnp.int32, (C, GL), 1), C - 1)
    strict, incl = ri > cj, ri >= cj
    eye_c = (ri == cj).astype(F32)
    pr = lax.broadcasted_iota(jnp.int32, (N, 2 * N), 0)
    pc = lax.broadcasted_iota(jnp.int32, (N, 2 * N), 1)
    first_head, pair_eye = pc < N, (pr == jnp.bitwise_and(pc, N - 1)).astype(F32)

    def same_head_blocks(z):
        return jnp.where(first_head, z[0:N], z[N:2 * N])
    items = [(ck, gp) for ck in range(len(chunk_rows)) for gp in range(RWKV_HEADS // HG)]
    sl = [(chunk_rows[ck], slice(gp * GL, (gp + 1) * GL)) for ck, gp in items]
    ar = [bf(jnp.concatenate([At[s], Rt[s]], axis=0)) for s in sl]
    zb = [_dot_nt(a, block_diag(Bt[s])) for a, s in zip(ar, sl)]
    zk = [_dot_nt(a, block_diag(Kt[s])) for a, s in zip(ar, sl)]
    n_mat = [jnp.where(strict, z[0:C], 0.0) for z in zb]
    m_mat = [jnp.where(strict, z[0:C], 0.0) for z in zk]
    v_bd = [block_diag(v[s]) for s in sl]
    mv = [_dot(bf(m), vb) for m, vb in zip(m_mat, v_bd)]
    t_inv, n_pow = [eye_c + n for n in n_mat], n_mat
    for _ in range(int(math.log2(C)) - 1):
        n_pow = [_dot(bf(n), block_diag(n)) for n in n_pow]
        t_inv = [t + _dot(bf(t), block_diag(n)) for t, n in zip(t_inv, n_pow)]
    t_bf = [bf(t) for t in t_inv]
    ta = [_dot(t, block_diag(At[s])) for t, s in zip(t_bf, sl)]
    g0 = [_dot(t, block_diag(m)) for t, m in zip(t_bf, mv)]
    p_mat = [bf(jnp.where(incl, z[C:], 0.0)) for z in zb]
    pk_mat = [bf(jnp.where(incl, z[C:], 0.0)) for z in zk]
    q_out = [Rt[s] + _dot(p, block_diag(t)) for p, t, s in zip(p_mat, ta, sl)]
    y0_out = [_dot(p, block_diag(g)) + _dot(pk, vb) for p, g, pk, vb in zip(p_mat, g0, pk_mat, v_bd)]
    bgT = [bf(Bg[s].T) for s in sl]
    a_full = [_dot(b, bf(t)) for b, t in zip(bgT, ta)]
    d_full = [_dot(b, bf(g)) + _dot(bf(Kg[s].T), bf(v[s])) for b, g, s in zip(bgT, g0, sl)]
    for n, (ck, gp) in enumerate(items):
        rows, lanes = sl[n]
        q_ref[0, rows, lanes] = q_out[n].astype(q_ref.dtype)
        y0_ref[0, rows, lanes] = y0_out[n]
        for pp in range(HG // 2):
            blk = slice(pp * 2 * N, (pp + 1) * 2 * N)
            pair_lanes = slice(lanes.start + blk.start, lanes.start + blk.stop)
            a_ref[0, ck, gp * (HG // 2) + pp] = (same_head_blocks(a_full[n][blk, blk])
                                                 + pair_eye * decay_end[ck][:, pair_lanes])
            d_ref[0, ck, gp * (HG // 2) + pp] = same_head_blocks(d_full[n][blk, blk])


def _rwkv_chunks(rw, mu, w0, w_up, a0, a_up, g_up, k_k, k_a, r_k):
    B, T, cols = rw.shape
    C, W, H, N = RWKV_CHUNK, RWKV_WIDTH, RWKV_HEADS, HEAD_DIM
    nc = T // C
    S = RWKV_STEP_CHUNKS
    R = S * C
    assert nc % S == 0
    seg = jnp.asarray(np.kron(np.eye(LANES // N), np.ones((N, N))), BF16)
    tri = jnp.asarray(np.tril(np.ones((C, C))), BF16)
    row = lambda z: z.reshape(1, -1)
    const = lambda b, c: (0, 0)
    vec = pl.BlockSpec((1, W), const)
    tok = pl.BlockSpec((1, R, W), lambda b, c: (b, c, 0))
    mat = pl.BlockSpec((1, S, H // 2, N, 2 * N), lambda b, c: (b, c, 0, 0, 0))
    return pl.pallas_call(
        _rwkv_chunk_kernel,
        grid=(B, nc // S),
        in_specs=[pl.BlockSpec((1, R, cols), lambda b, c: (b, c, 0)),
                  pl.BlockSpec((1, SUBLANES, cols), lambda b, c: (b, jnp.maximum(c * (R // SUBLANES) - 1, 0), 0)),
                  pl.BlockSpec((1, cols), const), vec,
                  pl.BlockSpec((DECAY_LORA, W), const), vec,
                  pl.BlockSpec((AAA_LORA, W), const),
                  pl.BlockSpec((GATE_LORA, W), const), vec, vec, vec,
                  pl.BlockSpec((LANES, LANES), const), pl.BlockSpec((C, C), const)],
        out_specs=[tok, tok, mat, mat, tok, tok],
        out_shape=[jax.ShapeDtypeStruct((B, T, W), BF16), jax.ShapeDtypeStruct((B, T, W), F32),
                   jax.ShapeDtypeStruct((B, nc, H // 2, N, 2 * N), F32),
                   jax.ShapeDtypeStruct((B, nc, H // 2, N, 2 * N), F32),
                   jax.ShapeDtypeStruct((B, T, W), F32), jax.ShapeDtypeStruct((B, T, W), F32)],
        compiler_params=_params("parallel", "parallel"), name="rwkv_chunks",
    )(rw, rw, row(mu), row(w0), w_up.astype(BF16), row(a0), a_up.astype(BF16), g_up.astype(BF16),
      row(k_k), row(k_a), row(r_k), seg, tri)


def _pair_block_diag(x):
    first_head = lax.broadcasted_iota(jnp.int32, x.shape, 1) < x.shape[0]
    zero = jnp.zeros_like(x)
    return jnp.concatenate([jnp.where(first_head, x, zero), jnp.where(first_head, zero, x)], axis=0)


def _pair_side_by_side(z):
    n = z.shape[0] // 2
    return jnp.where(lax.broadcasted_iota(jnp.int32, (n, 2 * n), 1) < n, z[0:n], z[n:])


def _rwkv_state_kernel(a_ref, d_ref, h_ref, h_sc):
    @pl.when(pl.program_id(0) == 0)
    def _():
        h_sc[...] = jnp.zeros(h_sc.shape, F32)

    items = [(b, pair) for b in range(a_ref.shape[0]) for pair in range(a_ref.shape[2])]
    states = [h_sc[b, pair] for b, pair in items]
    for ck in range(a_ref.shape[1]):
        for (b, pair), st in zip(items, states):
            h_ref[b, ck, pair] = _pair_side_by_side(st).astype(h_ref.dtype)
        states = [_dot_hi_lo(_pair_block_diag(a_ref[b, ck, pair]), st) + _pair_block_diag(d_ref[b, ck, pair])
                  for (b, pair), st in zip(items, states)]
    for (b, pair), st in zip(items, states):
        h_sc[b, pair] = st


def _rwkv_readout_kernel(h_ref, q_ref, y0_ref, g_ref, bonus_ref, lw_ref, lb_ref, seg_ref, o_ref):
    C = RWKV_CHUNK
    inv_n = 1.0 / HEAD_DIM
    items = [(b, ck, pair, slice(ck * C, (ck + 1) * C), slice(pair * LANES, (pair + 1) * LANES))
             for b in range(h_ref.shape[0]) for ck in range(h_ref.shape[1]) for pair in range(h_ref.shape[2])]
    ys = [_dot(q_ref[b, rows, lanes], _pair_block_diag(h_ref[b, ck, pair])) + y0_ref[b, rows, lanes]
          for b, ck, pair, rows, lanes in items]
    means = [_dot_f32_lhs(y, seg_ref[...], 2) * inv_n for y in ys]
    cen = [y - mean for y, mean in zip(ys, means)]
    var = [_dot_f32_lhs(jnp.square(c), seg_ref[...], 2) * inv_n for c in cen]
    for (b, _, _, rows, lanes), c, v in zip(items, cen, var):
        yn = c * lax.rsqrt(v + LNX_EPS) * lw_ref[:, lanes] + lb_ref[:, lanes]
        o_ref[b, rows, lanes] = (yn + bonus_ref[b, rows, lanes]) * g_ref[b, rows, lanes]


def _rwkv_scan(A, D, Q, Y0, g, bonus, lnx_w, lnx_b):
    B, nc, P, N, N2 = A.shape
    T, W, C = Q.shape[1], Q.shape[2], RWKV_CHUNK
    S, SR = RWKV_SCAN_STEP_CHUNKS, RWKV_READOUT_STEP_CHUNKS
    assert nc % S == 0 and nc % SR == 0
    mat = lambda s: pl.BlockSpec((B, s, P, N, N2), lambda c: (0, c, 0, 0, 0))
    h_start = pl.pallas_call(
        _rwkv_state_kernel,
        grid=(nc // S,),
        in_specs=[mat(S), mat(S)],
        out_specs=mat(S),
        out_shape=jax.ShapeDtypeStruct((B, nc, P, N, N2), BF16),
        scratch_shapes=[pltpu.VMEM((B, P, N2, N2), F32)],
        compiler_params=_params("arbitrary"), name="rwkv_state",
    )(A, D)
    tok = pl.BlockSpec((B, SR * C, W), lambda c: (0, c, 0))
    vec = pl.BlockSpec((1, W), lambda c: (0, 0))
    seg = jnp.asarray(np.kron(np.eye(2), np.ones((HEAD_DIM, HEAD_DIM))), BF16)
    return pl.pallas_call(
        _rwkv_readout_kernel,
        grid=(nc // SR,),
        in_specs=[mat(SR), tok, tok, tok, tok, vec, vec, pl.BlockSpec((N2, N2), lambda c: (0, 0))],
        out_specs=tok,
        out_shape=jax.ShapeDtypeStruct((B, T, W), F32),
        compiler_params=_params("parallel"), name="rwkv_readout",
    )(h_start, Q, Y0, g, bonus, lnx_w.reshape(1, W), lnx_b.reshape(1, W), seg)


def _mix_xattn_kernel(x_ref, on_ref, or_ref, wo1_ref, wo2_ref, gx_ref, wq_ref, k_ref, v_ref, wo_ref, o_ref):
    x1 = x_ref[0] + _dot(on_ref[0].astype(BF16), wo1_ref[...]) + _dot(or_ref[0].astype(BF16), wo2_ref[...])
    q = _dot(_rms(x1, gx_ref[...]).astype(BF16), wq_ref[...])
    dh = q.shape[-1] // XATTN_HEADS
    qbf = (q * (dh ** -0.5)).astype(BF16)
    outs = []
    for h in range(XATTN_HEADS):
        hs = slice(h * dh, (h + 1) * dh)
        s = _dot_nt(qbf[:, hs], k_ref[0, :, hs])
        e = jnp.exp(s - jnp.max(s, axis=-1, keepdims=True))
        p = e / jnp.sum(e, axis=-1, keepdims=True)
        outs.append(_dot(p.astype(BF16), v_ref[0, :, hs]))
    o = jnp.concatenate(outs, axis=1).astype(BF16)
    o_ref[0] = x1 + _dot(o, wo_ref[...])


def _mix_xattn(x, o_nsa, o_rwkv, w_out, norm_x_g, w_q, mem_k, mem_v, w_o):
    B, T, D = x.shape
    M = mem_k.shape[1]
    tm = min(ROW_TILE, T)
    wo1, wo2 = w_out[:NSA_WIDTH].astype(BF16), w_out[NSA_WIDTH:].astype(BF16)
    const = lambda b, i: (0, 0)
    tile = lambda w: pl.BlockSpec((1, tm, w), lambda b, i: (b, i, 0))
    return pl.pallas_call(
        _mix_xattn_kernel,
        grid=(B, T // tm),
        in_specs=[tile(D), tile(NSA_WIDTH), tile(RWKV_WIDTH),
                  pl.BlockSpec(wo1.shape, const), pl.BlockSpec(wo2.shape, const),
                  pl.BlockSpec((1, D), const), pl.BlockSpec((D, D), const),
                  pl.BlockSpec((1, M, D), lambda b, i: (b, 0, 0)), pl.BlockSpec((1, M, D), lambda b, i: (b, 0, 0)),
                  pl.BlockSpec((D, D), const)],
        out_specs=tile(D),
        out_shape=jax.ShapeDtypeStruct((B, T, D), F32),
        compiler_params=_params("parallel", "parallel"), name="mix_xattn",
    )(x, o_nsa, o_rwkv, wo1, wo2, norm_x_g.reshape(1, D), w_q.astype(BF16), mem_k, mem_v, w_o.astype(BF16))


def _ffn_kernel(x_ref, g_ref, wg_ref, wu_ref, wd_ref, gf_ref, o_ref, *, final_norm):
    x = x_ref[...]
    h = _rms(x, g_ref[...]).astype(BF16)
    act = (jax.nn.silu(_dot(h, wg_ref[...])) * _dot(h, wu_ref[...])).astype(BF16)
    y = x + _dot(act, wd_ref[...])
    o_ref[...] = _rms(y, gf_ref[...]) if final_norm else y


def _ffn(x, norm_g, w_gate, w_up, w_down, final_g, final_norm):
    R, D = x.shape
    F = w_gate.shape[1]
    tm = min(FFN_ROW_TILE, R)
    const = lambda i: (0, 0)
    resident = lambda shape: pl.BlockSpec(shape, const, pipeline_mode=pl.Buffered(1))
    return pl.pallas_call(
        functools.partial(_ffn_kernel, final_norm=final_norm),
        grid=(R // tm,),
        in_specs=[pl.BlockSpec((tm, D), lambda i: (i, 0)), pl.BlockSpec((1, D), const),
                  resident((D, F)), resident((D, F)), resident((F, D)), pl.BlockSpec((1, D), const)],
        out_specs=pl.BlockSpec((tm, D), lambda i: (i, 0)),
        out_shape=jax.ShapeDtypeStruct((R, D), F32),
        compiler_params=_params("parallel"), name="ffn",
    )(x, norm_g.reshape(1, D), w_gate.astype(BF16), w_up.astype(BF16), w_down.astype(BF16),
      final_g.reshape(1, D))


def _overlap_matrix(n_cmp_pad, n_sel):
    c = np.arange(n_cmp_pad)[:, None] * CMP_STRIDE
    s = np.arange(n_sel)[None, :] * SEL_BLOCK
    return ((c <= s + SEL_BLOCK - 1) & (c + CMP_LEN - 1 >= s)).astype(np.float32)


def _layer(x, mem, rel_bias, final_g, is_last, norm_mix_g, w_in, nsa_gate_b, cmp_pe_k, cmp_pe_v,
           cmp_k_w1, cmp_k_b1, cmp_k_w2, cmp_v_w1, cmp_v_b1, cmp_v_w2,
           rwkv_mu, rwkv_w0, rwkv_w_up, rwkv_a0, rwkv_a_up, rwkv_g_up,
           rwkv_k_k, rwkv_k_a, rwkv_r_k, rwkv_lnx_w, rwkv_lnx_b, w_out,
           norm_x_g, norm_mem_g, w_q_x, w_kv_x, w_o_x, norm_ffn_g, w_gate, w_up, w_down):
    B, T, D = x.shape
    q, kvc, kaug, vsT, vwT, gates, rw = _proj_in(x, norm_mix_g, w_in, nsa_gate_b)

    n16 = T // CMP_STRIDE
    kc = _compress(kvc, 0, cmp_pe_k, cmp_k_w1, cmp_k_b1, cmp_k_w2, False)
    vcT = _compress(kvc, 1, cmp_pe_v, cmp_v_w1, cmp_v_b1, cmp_v_w2, True)
    bias_tiles, cmp_bias = _bias_tiles(rel_bias, n16)
    o_nsa = _nsa_attention(q, kc, vcT, kaug, vsT, vwT,
                           gates, jnp.asarray(_overlap_matrix(n16, T // SEL_BLOCK).T, BF16), bias_tiles, cmp_bias)

    Q, Y0, A, Dm, g, bonus = _rwkv_chunks(rw, rwkv_mu, rwkv_w0, rwkv_w_up, rwkv_a0,
                                          rwkv_a_up, rwkv_g_up, rwkv_k_k, rwkv_k_a, rwkv_r_k.reshape(-1))
    o_rwkv = _rwkv_scan(A, Dm, Q, Y0, g, bonus, rwkv_lnx_w, rwkv_lnx_b)

    M = mem.shape[1]
    (kv_mem,) = _norm_matmul(mem.reshape(B * M, D), norm_mem_g, [w_kv_x.astype(BF16)], [None], [BF16], ROW_TILE)
    kv_mem = kv_mem.reshape(B, M, 2 * D)
    x = _mix_xattn(x, o_nsa, o_rwkv, w_out, norm_x_g, w_q_x, kv_mem[..., :D], kv_mem[..., D:], w_o_x)
    x = _ffn(x.reshape(B * T, D), norm_ffn_g, w_gate, w_up, w_down, final_g, is_last)
    return x.reshape(B, T, D)


def kernel(x, mem, rel_bias, norm_f_g, norm_mix_g, w_in, nsa_gate_b, cmp_pe_k, cmp_pe_v, cmp_k_w1, cmp_k_b1, cmp_k_w2, cmp_v_w1, cmp_v_b1, cmp_v_w2, rwkv_mu, rwkv_w0, rwkv_w_up, rwkv_a0, rwkv_a_up, rwkv_g_up, rwkv_k_k, rwkv_k_a, rwkv_r_k, rwkv_lnx_w, rwkv_lnx_b, w_out, norm_x_g, norm_mem_g, w_q_x, w_kv_x, w_o_x, norm_ffn_g, w_gate, w_up, w_down):
    stacked = (norm_mix_g, w_in, nsa_gate_b, cmp_pe_k, cmp_pe_v, cmp_k_w1, cmp_k_b1, cmp_k_w2, cmp_v_w1,
               cmp_v_b1, cmp_v_w2, rwkv_mu, rwkv_w0, rwkv_w_up, rwkv_a0, rwkv_a_up, rwkv_g_up, rwkv_k_k,
               rwkv_k_a, rwkv_r_k, rwkv_lnx_w, rwkv_lnx_b, w_out, norm_x_g, norm_mem_g, w_q_x, w_kv_x, w_o_x,
               norm_ffn_g, w_gate, w_up, w_down)
    depth = w_in.shape[0]
    for l in range(depth):
        x = _layer(x, mem, rel_bias, norm_f_g, l == depth - 1, *[p[l] for p in stacked])
    return x
```

```python
import functools
import math

import numpy as np
import jax
import jax.numpy as jnp
from jax import lax
from jax.experimental import pallas as pl
from jax.experimental.pallas import tpu as pltpu

F32 = jnp.float32
BF16 = jnp.bfloat16

LANES = 128
SUBLANES = 8
BF16_ROWS = 16
VMEM_LIMIT_BYTES = 56 * 1024 * 1024

HEAD_DIM = 64
NSA_HEADS = 8
NSA_KV_HEADS = 2
NSA_GROUP = NSA_HEADS // NSA_KV_HEADS
NSA_WIDTH = NSA_HEADS * HEAD_DIM
KV_WIDTH = NSA_KV_HEADS * HEAD_DIM
RWKV_HEADS = 8
RWKV_WIDTH = RWKV_HEADS * HEAD_DIM
CMP_LEN = 32
CMP_STRIDE = 16
SEL_BLOCK = 64
SEL_SHIFT = 6
SEL_TOP = 16
WINDOW = 512
Q_BLOCK = 128
DECAY_LORA = 64
AAA_LORA = 64
GATE_LORA = 128
N_BUCKETS = 32
MAX_DISTANCE = 2048
XATTN_HEADS = 4
RMS_EPS = 1e-6
LNX_EPS = 64e-5
FORCE_SCORE = 1e4
NEG_SCORE = -1e9
MASK_SCORE = -1e30
LOG2E = math.log2(math.e)
RWKV_COLS = 3 * RWKV_WIDTH + DECAY_LORA + AAA_LORA + GATE_LORA
NSA_COLS = NSA_WIDTH + 6 * KV_WIDTH + 3 * NSA_HEADS

KEY_CHUNK = 128
RWKV_CHUNK = 64
RWKV_SCAN_STEP_CHUNKS = 8
RWKV_GROUP_HEADS = 2
RWKV_STEP_CHUNKS = 8
ROW_TILE = 512
FFN_ROW_TILE = 512


def _t5_thresholds():
    d = np.arange(0, 2 * MAX_DISTANCE, dtype=np.int64)
    max_exact = N_BUCKETS // 2
    nf = np.maximum(d, 1).astype(np.float32)
    large = max_exact + (np.log(nf / np.float32(max_exact)) / np.float32(math.log(MAX_DISTANCE / max_exact))
                         * np.float32(N_BUCKETS - max_exact)).astype(np.int32)
    bucket = np.where(d < max_exact, d, np.minimum(large, N_BUCKETS - 1))
    return [int(np.argmax(bucket >= k)) for k in range(N_BUCKETS)]


T5_THRESHOLDS = _t5_thresholds()
N_BIAS_TILES = -(-(T5_THRESHOLDS[-1] + KEY_CHUNK) // KEY_CHUNK) + 1
TILE_MASKED = N_BIAS_TILES
TILE_WINDOW_EDGE = N_BIAS_TILES + 1
N_ALL_TILES = N_BIAS_TILES + 2
SEL_STEP_BLOCKS = 16
SEL_STAGE_TILES = 4
SEL_GROUP_STAGES = SEL_STEP_BLOCKS * SEL_BLOCK // (SEL_STAGE_TILES * KEY_CHUNK)
NSA_STEP_QBLOCKS = 4
V_ROWS = HEAD_DIM + BF16_ROWS
GATE_ROWS = 16


def _params(*semantics):
    return pltpu.CompilerParams(dimension_semantics=semantics, vmem_limit_bytes=VMEM_LIMIT_BYTES)


def _rms(x, g):
    return x * lax.rsqrt(jnp.mean(x * x, axis=-1, keepdims=True) + RMS_EPS) * g


def _dot(a, b, **kw):
    return jnp.dot(a, b, preferred_element_type=F32, **kw)


def _split3(x):
    hi = x.astype(BF16)
    r1 = x - hi.astype(F32)
    mid = r1.astype(BF16)
    lo = (r1 - mid.astype(F32)).astype(BF16)
    return hi, mid, lo


def _dot_f32_lhs(x, w01, pieces=3):
    w = w01.astype(BF16)
    hi, mid, lo = _split3(x)
    return _dot(hi, w) + (_dot(mid, w) + _dot(lo, w) if pieces == 3 else _dot(mid, w))


def _dot_f32_rhs(w01, x):
    w = w01.astype(BF16)
    hi, mid, lo = _split3(x)
    return _dot(w, hi) + (_dot(w, mid) + _dot(w, lo))


def _dot_hi_lo(a, b):
    a_hi = a.astype(BF16)
    a_lo = (a - a_hi.astype(F32)).astype(BF16)
    b_hi = b.astype(BF16)
    b_lo = (b - b_hi.astype(F32)).astype(BF16)
    return _dot(a_hi, b_hi) + (_dot(a_hi, b_lo) + _dot(a_lo, b_hi))


def _dot_nt(a, b, **kw):
    return lax.dot_general(a, b, (((1,), (1,)), ((), ())), preferred_element_type=F32, **kw)


def _norm_matmul_kernel(x_ref, g_ref, *refs, nseg, bias_flags):
    nb = sum(bias_flags)
    w_refs, b_refs, o_refs = refs[:nseg], refs[nseg:nseg + nb], refs[nseg + nb:]
    xn = _rms(x_ref[...], g_ref[...]).astype(BF16)
    bi = 0
    for s in range(nseg):
        y = _dot(xn, w_refs[s][...])
        if bias_flags[s]:
            y = y + b_refs[bi][...]
            bi += 1
        o_refs[s][...] = y.astype(o_refs[s].dtype)


def _norm_matmul(x, g, weights, biases, out_dtypes, row_tile):
    R, D = x.shape
    tm = min(row_tile, R)
    assert R % tm == 0
    nseg = len(weights)
    bias_flags = tuple(b is not None for b in biases)
    const = lambda i: (0, 0)
    in_specs = [pl.BlockSpec((tm, D), lambda i: (i, 0)), pl.BlockSpec((1, D), const)]
    in_specs += [pl.BlockSpec(w.shape, const) for w in weights]
    in_specs += [pl.BlockSpec((1, b.shape[-1]), const) for b in biases if b is not None]
    out_specs = [pl.BlockSpec((tm, w.shape[1]), lambda i: (i, 0)) for w in weights]
    out_shape = [jax.ShapeDtypeStruct((R, w.shape[1]), dt) for w, dt in zip(weights, out_dtypes)]
    return pl.pallas_call(
        functools.partial(_norm_matmul_kernel, nseg=nseg, bias_flags=bias_flags),
        grid=(R // tm,), in_specs=in_specs, out_specs=out_specs, out_shape=out_shape,
        compiler_params=_params("parallel"), name="norm_matmul",
    )(x, g.reshape(1, D), *weights, *[b.reshape(1, -1) for b in biases if b is not None])


def _proj_in_kernel(x_ref, g_ref, wq_ref, wc_ref, wk_ref, wvT_ref, wgT_ref, bg_ref, wr_ref,
                    q_ref, kvc_ref, kaug_ref, vsT_ref, vwT_ref, gate_ref, rw_ref, *, seq_len):
    tm = x_ref.shape[0]
    xn = _rms(x_ref[...], g_ref[...]).astype(BF16)
    qT = (_dot_nt(wq_ref[...], xn) * (HEAD_DIM ** -0.5 * LOG2E)).astype(BF16)
    for j in range(tm // Q_BLOCK):
        for hg in range(NSA_HEADS):
            h, g = divmod(hg, NSA_GROUP)
            q_ref[0, j, h, :, g * Q_BLOCK:(g + 1) * Q_BLOCK] = qT[hg * HEAD_DIM:(hg + 1) * HEAD_DIM,
                                                                  j * Q_BLOCK:(j + 1) * Q_BLOCK]
    kvc_ref[0] = _dot(xn, wc_ref[...])
    rw_ref[0] = _dot(xn, wr_ref[...])
    k_all = _dot(xn, wk_ref[...])
    tok = lax.rem(pl.program_id(0) * tm, seq_len) + lax.broadcasted_iota(jnp.int32, k_all.shape, 0)
    lane = lax.broadcasted_iota(jnp.int32, k_all.shape, 1)
    blk = jnp.bitwise_and(jnp.right_shift(tok, SEL_SHIFT), SEL_STEP_BLOCKS - 1)
    hot = (jnp.bitwise_and(lane, LANES - 1) == HEAD_DIM + blk) & (lane < NSA_KV_HEADS * LANES)
    kaug_ref[0] = jnp.where(hot, 1.0, k_all).astype(BF16)
    vT = _dot_nt(wvT_ref[...], xn)
    row = lax.broadcasted_iota(jnp.int32, vT.shape, 0)
    ones_row = row == HEAD_DIM
    for grp in range(1, 2 * NSA_KV_HEADS):
        ones_row = ones_row | (row == grp * V_ROWS + HEAD_DIM)
    vT = jnp.where(ones_row, 1.0, vT).astype(BF16)
    half = NSA_KV_HEADS * V_ROWS
    stage_keys = vsT_ref.shape[3]
    for c in range(tm // stage_keys):
        vsT_ref[0, c] = vT[0:half, c * stage_keys:(c + 1) * stage_keys]
    for c in range(tm // KEY_CHUNK):
        vwT_ref[0, c] = vT[half:, c * KEY_CHUNK:(c + 1) * KEY_CHUNK]
    gT = _dot_nt(wgT_ref[...], xn) + bg_ref[...]
    for j in range(tm // Q_BLOCK):
        for h in range(NSA_KV_HEADS):
            gate_ref[0, j, h] = gT[h * GATE_ROWS:(h + 1) * GATE_ROWS, j * Q_BLOCK:(j + 1) * Q_BLOCK]


def _proj_in(x, norm_g, w_in, gate_b):
    B, T, D = x.shape
    Hkv, G, dh = NSA_KV_HEADS, NSA_GROUP, HEAD_DIM
    tm = ROW_TILE
    stage_keys = SEL_STAGE_TILES * KEY_CHUNK
    assert T % tm == 0 and tm % Q_BLOCK == 0 and tm % stage_keys == 0
    kv0 = NSA_WIDTH
    g0 = kv0 + 6 * KV_WIDTH
    stream = lambda s: w_in[:, kv0 + s * KV_WIDTH:kv0 + (s + 1) * KV_WIDTH].reshape(D, Hkv, dh)
    pad_cols = lambda w: jnp.pad(w, ((0, 0), (0, 0), (0, LANES - dh))).reshape(D, Hkv * LANES)
    pad_rows = lambda w: jnp.pad(w.transpose(1, 2, 0), ((0, 0), (0, V_ROWS - dh), (0, 0))).reshape(Hkv * V_ROWS, D)
    w_k = jnp.concatenate([pad_cols(stream(2)), pad_cols(stream(4))], axis=1)
    w_vT = jnp.concatenate([pad_rows(stream(3)), pad_rows(stream(5))], axis=0)
    reorder = lambda a: a.reshape(-1, Hkv, G, 3).transpose(1, 3, 2, 0).reshape(Hkv, 3 * G, -1)
    pad_gate = lambda a: jnp.pad(a, ((0, 0), (0, GATE_ROWS - 3 * G), (0, 0))).reshape(Hkv * GATE_ROWS, -1)
    w_gT = pad_gate(reorder(w_in[:, g0:NSA_COLS]))
    b_g = pad_gate(reorder(gate_b.reshape(1, -1)))
    weights = [w_in[:, :kv0].T, w_in[:, kv0:kv0 + 2 * KV_WIDTH], w_k, w_vT, w_gT]
    weights = [w.astype(BF16) for w in weights] + [b_g, w_in[:, NSA_COLS:].astype(BF16)]
    nt = T // tm
    rows = lambda n: pl.BlockSpec((1, tm, n), lambda i: (i // nt, i % nt, 0))
    const = lambda i: (0, 0)
    return pl.pallas_call(
        functools.partial(_proj_in_kernel, seq_len=T),
        grid=(B * nt,),
        in_specs=[pl.BlockSpec((tm, D), lambda i: (i, 0)), pl.BlockSpec((1, D), const)]
        + [pl.BlockSpec(w.shape, const) for w in weights],
        out_specs=[pl.BlockSpec((1, tm // Q_BLOCK, Hkv, dh, G * Q_BLOCK), lambda i: (i // nt, i % nt, 0, 0, 0)),
                   rows(2 * KV_WIDTH), rows(2 * Hkv * LANES),
                   pl.BlockSpec((1, tm // stage_keys, Hkv * V_ROWS, stage_keys), lambda i: (i // nt, i % nt, 0, 0)),
                   pl.BlockSpec((1, tm // KEY_CHUNK, Hkv * V_ROWS, KEY_CHUNK), lambda i: (i // nt, i % nt, 0, 0)),
                   pl.BlockSpec((1, tm // Q_BLOCK, Hkv, GATE_ROWS, Q_BLOCK), lambda i: (i // nt, i % nt, 0, 0, 0)),
                   rows(RWKV_COLS)],
        out_shape=[jax.ShapeDtypeStruct((B, T // Q_BLOCK, Hkv, dh, G * Q_BLOCK), BF16),
                   jax.ShapeDtypeStruct((B, T, 2 * KV_WIDTH), F32),
                   jax.ShapeDtypeStruct((B, T, 2 * Hkv * LANES), BF16),
                   jax.ShapeDtypeStruct((B, T // stage_keys, Hkv * V_ROWS, stage_keys), BF16),
                   jax.ShapeDtypeStruct((B, T // KEY_CHUNK, Hkv * V_ROWS, KEY_CHUNK), BF16),
                   jax.ShapeDtypeStruct((B, T // Q_BLOCK, Hkv, GATE_ROWS, Q_BLOCK), F32),
                   jax.ShapeDtypeStruct((B, T, RWKV_COLS), F32)],
        compiler_params=_params("parallel"), name="proj_in",
    )(x.reshape(B * T, D), norm_g.reshape(1, D), *weights)


def _compress_kernel(x_ref, pe_ref, w1_ref, b1_ref, w2_ref, o_ref, *, transpose_out):
    n16 = x_ref.shape[1] // CMP_STRIDE
    hidden = w1_ref.shape[2] // NSA_KV_HEADS
    lo = jnp.zeros((n16, w1_ref.shape[2]), F32)
    hi = jnp.zeros((n16, w1_ref.shape[2]), F32)
    for l in range(CMP_STRIDE):
        rows = x_ref[0, pl.ds(l, n16, stride=CMP_STRIDE), :]
        lo = lo + _dot((rows + pe_ref[l:l + 1, :]).astype(BF16), w1_ref[l])
        hi = hi + _dot((rows + pe_ref[CMP_STRIDE + l:CMP_STRIDE + l + 1, :]).astype(BF16), w1_ref[CMP_STRIDE + l])
    h = lo + pltpu.roll(hi, n16 - 1, axis=0) + b1_ref[...]
    h = jax.nn.gelu(h).astype(BF16)
    for hkv in range(NSA_KV_HEADS):
        hh = h[:, hkv * hidden:(hkv + 1) * hidden]
        if transpose_out:
            o_ref[0, hkv] = _dot_nt(w2_ref[...], hh).astype(o_ref.dtype)
        else:
            o_ref[0, hkv] = _dot(hh, w2_ref[...]).astype(o_ref.dtype)


def _compress(kvc, stream, pe, w1, b1, w2, transpose_out):
    B, T, _ = kvc.shape
    H, dh = NSA_KV_HEADS, HEAD_DIM
    n16 = T // CMP_STRIDE
    hidden = w1.shape[1]
    w1r = w1.astype(BF16).reshape(CMP_LEN, dh, hidden)
    w1_bd = jnp.concatenate([jnp.pad(w1r, ((0, 0), (0, 0), (h * hidden, (H - 1 - h) * hidden))) for h in range(H)],
                            axis=1)
    w2b = (w2.T if transpose_out else w2).astype(BF16)
    oshape = (B, H, dh, n16) if transpose_out else (B, H, n16, dh)
    return pl.pallas_call(
        functools.partial(_compress_kernel, transpose_out=transpose_out),
        grid=(B,),
        in_specs=[pl.BlockSpec((1, T, H * dh), lambda b: (b, 0, stream)),
                  pl.BlockSpec((CMP_LEN, H * dh), lambda b: (0, 0)),
                  pl.BlockSpec(w1_bd.shape, lambda b: (0, 0, 0)),
                  pl.BlockSpec((1, H * hidden), lambda b: (0, 0)),
                  pl.BlockSpec(w2b.shape, lambda b: (0, 0))],
        out_specs=pl.BlockSpec((1,) + oshape[1:], lambda b: (b, 0, 0, 0)),
        out_shape=jax.ShapeDtypeStruct(oshape, BF16),
        compiler_params=_params("parallel"), name="nsa_compress",
    )(kvc, jnp.tile(pe, (1, H)), w1_bd, jnp.tile(b1.reshape(1, hidden), (1, H)), w2b)


def _bias_of_distance(tab_ref, h, d):
    val = jnp.full(d.shape, tab_ref[h, 0], F32)
    for k in range(1, N_BUCKETS):
        val = jnp.where(d >= T5_THRESHOLDS[k], tab_ref[h, k], val)
    return val * LOG2E


def _bias_tiles_kernel(tab_ref, bt_ref, cb_ref, *, n_cmp_pad):
    hkv = pl.program_id(0)
    j = lax.broadcasted_iota(jnp.int32, (KEY_CHUNK, Q_BLOCK), 0)
    i = lax.broadcasted_iota(jnp.int32, (KEY_CHUNK, Q_BLOCK), 1)
    r2 = lax.broadcasted_iota(jnp.int32, (2 * n_cmp_pad, Q_BLOCK), 0)
    i2 = lax.broadcasted_iota(jnp.int32, (2 * n_cmp_pad, Q_BLOCK), 1)
    l2 = r2 - (n_cmp_pad - KEY_CHUNK)
    d2 = i2 - CMP_STRIDE * l2 + (CMP_STRIDE * KEY_CHUNK - Q_BLOCK - (CMP_LEN - 1))
    hidden2 = (l2 >= KEY_CHUNK) | ((l2 >= 0) & (d2 < 0))
    d2 = jnp.where((l2 >= 0) & (l2 < KEY_CHUNK), d2, 2 * MAX_DISTANCE)
    for g in range(NSA_GROUP):
        h = hkv * NSA_GROUP + g
        lanes = slice(g * Q_BLOCK, (g + 1) * Q_BLOCK)
        for m in range(N_BIAS_TILES):
            tile = _bias_of_distance(tab_ref, h, m * KEY_CHUNK + i - j)
            if m == 0:
                tile = jnp.where(j <= i, tile, MASK_SCORE)
            bt_ref[0, m, :, lanes] = tile
        bt_ref[0, TILE_MASKED, :, lanes] = jnp.full((KEY_CHUNK, Q_BLOCK), MASK_SCORE, F32)
        edge = _bias_of_distance(tab_ref, h, WINDOW + i - j)
        bt_ref[0, TILE_WINDOW_EDGE, :, lanes] = jnp.where(j > i, edge, MASK_SCORE)
        cb_ref[0, :, lanes] = jnp.where(hidden2, MASK_SCORE, _bias_of_distance(tab_ref, h, d2))


def _bias_tiles(rel_bias, n_cmp_pad):
    assert CMP_STRIDE * KEY_CHUNK - Q_BLOCK - (CMP_LEN - 1) >= T5_THRESHOLDS[-1]
    GQ = NSA_GROUP * Q_BLOCK
    return pl.pallas_call(
        functools.partial(_bias_tiles_kernel, n_cmp_pad=n_cmp_pad),
        grid=(NSA_KV_HEADS,),
        in_specs=[pl.BlockSpec(memory_space=pltpu.SMEM)],
        out_specs=[pl.BlockSpec((1, N_ALL_TILES, KEY_CHUNK, GQ), lambda h: (h, 0, 0, 0)),
                   pl.BlockSpec((1, 2 * n_cmp_pad, GQ), lambda h: (h, 0, 0))],
        out_shape=[jax.ShapeDtypeStruct((NSA_KV_HEADS, N_ALL_TILES, KEY_CHUNK, GQ), F32),
                   jax.ShapeDtypeStruct((NSA_KV_HEADS, 2 * n_cmp_pad, GQ), F32)],
        compiler_params=_params("parallel"), name="t5_bias_tiles",
    )(rel_bias.T)


def _nsa_kernel(q_ref, kc_ref, vcT_ref, ks_ref, vsT_ref, kw_ref, vwT_ref, gate_ref, ovT_ref, bt_ref, cb_ref,
                o_ref, acc_sc, qaug_sc, seladd_sc, s0_sc, s1_sc, p0_sc, p1_sc, *, n_sel, n_cmp_pad):
    G, NQ = NSA_GROUP, NSA_STEP_QBLOCKS
    GQ = G * Q_BLOCK
    qbs = [pl.program_id(2) * NQ + x for x in range(NQ)]
    per_block = lambda fn: jnp.concatenate([fn(x) for x in range(NQ)], axis=1)
    tile_g = lambda a: jnp.concatenate([a] * G, axis=1)
    qT = per_block(lambda x: q_ref[0, x, 0])
    qaug_sc[0:HEAD_DIM, :] = qT
    qaug_sc[HEAD_DIM:, :] = jnp.zeros((qaug_sc.shape[0] - HEAD_DIM, NQ * GQ), BF16)
    lane_q = lax.broadcasted_iota(jnp.int32, (1, Q_BLOCK), 1)
    t = per_block(lambda x: qbs[x] * Q_BLOCK + lane_q)

    def bias_tile(dist_of_block):
        def one(x):
            dist = dist_of_block(x)
            return bt_ref[0, jnp.where(dist < 0, TILE_MASKED, jnp.minimum(dist, N_BIAS_TILES - 1))]
        return per_block(one)

    n_back = WINDOW // KEY_CHUNK
    q_win = qaug_sc[...]
    win = {}

    def win_scores(x, back):
        kc = jnp.maximum(qbs[x] - back, 0)
        edge = TILE_WINDOW_EDGE if back == n_back else back
        tile = jnp.where(qbs[x] >= back, edge, TILE_MASKED)
        k_chunk = kw_ref[0, pl.ds(pl.multiple_of(kc * KEY_CHUNK, KEY_CHUNK), KEY_CHUNK), :]
        win["s", x, back] = _dot(k_chunk, q_win[:, x * GQ:(x + 1) * GQ]) + bt_ref[0, tile]
        col_max = jnp.max(win["s", x, back], axis=0, keepdims=True)
        win["m", x] = jnp.maximum(win["m", x], col_max) if ("m", x) in win else col_max

    def win_weights(x, back):
        win["p", x, back] = jnp.exp2(win["s", x, back] - win["m", x]).astype(BF16)

    def win_values(x, back):
        pv = _dot(vwT_ref[0, jnp.maximum(qbs[x] - back, 0)], win["p", x, back])
        win["acc", x] = win["acc", x] + pv if ("acc", x) in win else pv

    backs = list(range(n_back, -1, -1))
    window_work = [functools.partial(fn, x, b) for x in range(NQ)
                   for fn in (win_scores, win_weights, win_values) for b in backs]

    def cmp_branch(n_rows):
        def cmp_bias(x):
            start = pl.multiple_of(n_cmp_pad - (Q_BLOCK // CMP_STRIDE) * (qbs[x] + 1), SUBLANES)
            return cb_ref[0, pl.ds(start, n_rows), :]
        s = _dot(kc_ref[0, 0, 0:n_rows, :], qT) + per_block(cmp_bias)
        e = jnp.exp2(s - jnp.maximum(jnp.max(s, axis=0, keepdims=True), 0.1 * MASK_SCORE))
        p = e * (1.0 / jnp.maximum(jnp.sum(e, axis=0, keepdims=True), 1e-30))
        o_cmp = _dot(vcT_ref[0, 0, :, 0:n_rows], p.astype(BF16))
        psum = per_block(lambda x: sum(p[:, x * GQ + g * Q_BLOCK:x * GQ + (g + 1) * Q_BLOCK] for g in range(G)))
        return o_cmp, _dot_f32_rhs(ovT_ref[:, 0:n_rows], psum)

    half_rows = n_cmp_pad // 2
    n_visible = (Q_BLOCK // CMP_STRIDE) * (qbs[-1] + 1)
    if half_rows % LANES == 0:
        o_c, imp = lax.cond(n_visible <= half_rows, lambda: cmp_branch(half_rows), lambda: cmp_branch(n_cmp_pad))
    else:
        o_c, imp = cmp_branch(n_cmp_pad)
    bj = lax.broadcasted_iota(jnp.int32, (n_sel, NQ * Q_BLOCK), 0)
    cur = jnp.right_shift(t, SEL_SHIFT)
    forced = (bj == 0) | (bj == cur) | (bj == cur - 1)
    valid = bj * SEL_BLOCK <= t
    score = jnp.where(forced, -jnp.inf, jnp.where(valid, imp, NEG_SCORE))
    bjf = bj.astype(F32)
    n_rounds = max(min(SEL_TOP, n_sel) - 3, 0)
    for rnd in range(n_rounds):
        mx = jnp.max(score, axis=0, keepdims=True)
        first = jnp.min(jnp.where(score == mx, bjf, float(n_sel)), axis=0, keepdims=True)
        score = jnp.where(bjf == first, -jnp.inf, score)
        take = -(-len(window_work) // (n_rounds - rnd))
        for piece in window_work[:take]:
            piece()
        window_work = window_work[take:]
    for piece in window_work:
        piece()
    o_w = per_block(lambda x: win["acc", x][0:HEAD_DIM] / jnp.maximum(win["acc", x][HEAD_DIM:HEAD_DIM + 1], 1e-30))
    sel_add = jnp.where(score == -jnp.inf, 0.0, MASK_SCORE)
    seladd_sc[...] = per_block(lambda x: tile_g(sel_add[:, x * Q_BLOCK:(x + 1) * Q_BLOCK]))

    stage_keys = SEL_STAGE_TILES * KEY_CHUNK
    n_stages = qbs[-1] // SEL_STAGE_TILES + 1
    last_stage = ks_ref.shape[1] // stage_keys - 1

    def scores(k, s_buf):
        kk = jnp.minimum(k, last_stage)
        blk0 = pl.multiple_of(kk // SEL_GROUP_STAGES * SEL_STEP_BLOCKS, SEL_STEP_BLOCKS)
        qaug_sc[HEAD_DIM:HEAD_DIM + SEL_STEP_BLOCKS, :] = seladd_sc[pl.ds(blk0, SEL_STEP_BLOCKS), :].astype(BF16)
        k0 = pl.multiple_of(kk * stage_keys, stage_keys)
        bias = jnp.concatenate([bias_tile(lambda x, c=c: qbs[x] - (k * SEL_STAGE_TILES + c))
                                for c in range(SEL_STAGE_TILES)], axis=0)
        s = _dot(ks_ref[0, pl.ds(k0, stage_keys), :], qaug_sc[...]) + bias
        s_buf[...] = s
        return jnp.max(s, axis=0, keepdims=True)

    def weights(m, col_max, s_buf, p_buf):
        m_new = jnp.maximum(m, col_max)
        p_buf[...] = jnp.exp2(s_buf[...] - m_new).astype(BF16)
        return m_new, jnp.exp2(m - m_new)

    def accumulate(k, alpha, p_buf):
        acc_sc[...] = alpha * acc_sc[...] + _dot(vsT_ref[0, jnp.clip(k, 0, last_stage)], p_buf[...])

    def pair(j, carry):
        m, col_max, alpha = carry
        k = 2 * j
        col_max1 = scores(k + 1, s1_sc)
        m, alpha0 = weights(m, col_max, s0_sc, p0_sc)
        accumulate(k - 1, alpha, p1_sc)
        col_max2 = scores(k + 2, s0_sc)
        m, alpha1 = weights(m, col_max1, s1_sc, p1_sc)
        accumulate(k, alpha0, p0_sc)
        return m, col_max2, alpha1

    acc_sc[...] = jnp.zeros(acc_sc.shape, F32)
    p1_sc[...] = jnp.zeros(p1_sc.shape, BF16)
    m_init = jnp.full((1, NQ * GQ), 0.1 * MASK_SCORE, F32)
    carry = (m_init, scores(0, s0_sc), jnp.ones((1, NQ * GQ), F32))
    n_pairs = n_stages // 2
    m, col_max, alpha = lax.fori_loop(0, n_pairs, pair, carry)
    last = 2 * n_pairs

    @pl.when(n_stages % 2 == 1)
    def _():
        _, alpha_last = weights(m, col_max, s0_sc, p0_sc)
        accumulate(last - 1, alpha, p1_sc)
        accumulate(last, alpha_last, p0_sc)

    @pl.when(n_stages % 2 == 0)
    def _():
        accumulate(last - 1, alpha, p1_sc)
    o_s = acc_sc[0:HEAD_DIM, :] / jnp.maximum(acc_sc[HEAD_DIM:HEAD_DIM + 1, :], 1e-30)

    def gate_row(br):
        def one(x):
            gt = jax.nn.sigmoid(gate_ref[0, x, 0])
            return jnp.concatenate([gt[br * G + g:br * G + g + 1, :] for g in range(G)], axis=1)
        return per_block(one)
    o = gate_row(0) * o_c + gate_row(1) * o_s + gate_row(2) * o_w
    for x in range(NQ):
        o_ref[0, x * Q_BLOCK:(x + 1) * Q_BLOCK, :] = jnp.concatenate(
            [o[:, x * GQ + g * Q_BLOCK:x * GQ + (g + 1) * Q_BLOCK].T for g in range(G)], axis=1)


def _nsa_attention(q, kc, vcT, kaug, vsT, vwT, gates, overlapT, bias_tiles, cmp_bias):
    B, T, _ = kaug.shape
    Hkv, G = NSA_KV_HEADS, NSA_GROUP
    GQ = G * Q_BLOCK
    nqb = T // Q_BLOCK
    n_cmp_pad = kc.shape[2]
    n_sel = T // SEL_BLOCK
    half_keys = SEL_STAGE_TILES * KEY_CHUNK
    NQ = NSA_STEP_QBLOCKS
    assert T % (SEL_GROUP_STAGES * half_keys) == 0 and nqb % NQ == 0
    assert FORCE_SCORE > NSA_GROUP
    per_head = lambda b, h, i: (b, h, 0, 0)
    v_rows = V_ROWS
    chunked = lambda w: pl.BlockSpec((1, T // w, v_rows, w), lambda b, h, i: (b, 0, h, 0))
    step_lanes = NQ * GQ
    return pl.pallas_call(
        functools.partial(_nsa_kernel, n_sel=n_sel, n_cmp_pad=n_cmp_pad),
        grid=(B, Hkv, nqb // NQ),
        in_specs=[pl.BlockSpec((1, NQ, 1, HEAD_DIM, GQ), lambda b, h, i: (b, i, h, 0, 0)),
                  pl.BlockSpec((1, 1, n_cmp_pad, HEAD_DIM), per_head),
                  pl.BlockSpec((1, 1, HEAD_DIM, n_cmp_pad), per_head),
                  pl.BlockSpec((1, T, LANES), lambda b, h, i: (b, 0, h)), chunked(half_keys),
                  pl.BlockSpec((1, T, LANES), lambda b, h, i: (b, 0, Hkv + h)), chunked(KEY_CHUNK),
                  pl.BlockSpec((1, NQ, 1, GATE_ROWS, Q_BLOCK), lambda b, h, i: (b, i, h, 0, 0)),
                  pl.BlockSpec(overlapT.shape, lambda b, h, i: (0, 0)),
                  pl.BlockSpec((1, N_ALL_TILES, KEY_CHUNK, GQ), lambda b, h, i: (h, 0, 0, 0)),
                  pl.BlockSpec((1, 2 * n_cmp_pad, GQ), lambda b, h, i: (h, 0, 0))],
        out_specs=pl.BlockSpec((1, NQ * Q_BLOCK, G * HEAD_DIM), lambda b, h, i: (b, i, h)),
        out_shape=jax.ShapeDtypeStruct((B, T, NSA_WIDTH), F32),
        scratch_shapes=[pltpu.VMEM((v_rows, step_lanes), F32), pltpu.VMEM((LANES, step_lanes), BF16),
                        pltpu.VMEM((n_sel, step_lanes), F32),
                        pltpu.VMEM((half_keys, step_lanes), F32), pltpu.VMEM((half_keys, step_lanes), F32),
                        pltpu.VMEM((half_keys, step_lanes), BF16), pltpu.VMEM((half_keys, step_lanes), BF16)],
        compiler_params=_params("parallel", "parallel", "arbitrary"), name="nsa_attention",
    )(q, kc, vcT, kaug, vsT, kaug, vwT, gates, overlapT, bias_tiles, cmp_bias)


def _rwkv_chunk_kernel(rw_ref, prev_ref, mu_ref, w0_ref, wup_ref, a0_ref, aup_ref, gup_ref, kk_ref, ka_ref,
                       rk_ref, seg_ref, tri_ref, q_ref, y0_ref, a_ref, d_ref, g_ref, bonus_ref):
    C, W, N = RWKV_CHUNK, RWKV_WIDTH, HEAD_DIM
    c = pl.program_id(1)
    x = rw_ref[0]
    R = x.shape[0]
    chunk_rows = [slice(ck * C, (ck + 1) * C) for ck in range(R // C)]
    row = lax.broadcasted_iota(jnp.int32, (R, 1), 0)
    last_prev = jnp.where(c == 0, 0.0, prev_ref[0, SUBLANES - 1:SUBLANES, :])
    x_prev = jnp.where(row == 0, last_prev, pltpu.roll(x, 1, axis=0))
    xs = x + (x_prev - x) * mu_ref[...]
    r, k, v = xs[:, 0:W], xs[:, W:2 * W], xs[:, 2 * W:3 * W]
    o = 3 * W
    wd, ad, gd = xs[:, o:o + DECAY_LORA], xs[:, o + DECAY_LORA:o + DECAY_LORA + AAA_LORA], \
        xs[:, o + DECAY_LORA + AAA_LORA:]
    w_log = -jax.nn.softplus(-(w0_ref[...] + _dot(jnp.tanh(wd).astype(BF16), wup_ref[...]))) - 0.5
    lw = -jnp.exp(w_log)
    lr = jax.nn.sigmoid(a0_ref[...] + _dot(ad.astype(BF16), aup_ref[...]))
    g_ref[0] = _dot(jax.nn.sigmoid(gd).astype(BF16), gup_ref[...])
    kk = k * kk_ref[...]
    def head_sums(z):
        return jnp.concatenate([_dot_f32_lhs(z[:, t * LANES:(t + 1) * LANES], seg_ref[...])
                                for t in range(W // LANES)], axis=1)
    kk = kk * lax.rsqrt(jnp.maximum(head_sums(kk * kk), 1e-24))
    k = k * (1.0 + (lr - 1.0) * ka_ref[...])
    bonus_ref[0] = head_sums(r * k * rk_ref[...]) * v
    a_vec, b_vec = -kk, kk * lr

    L = jnp.concatenate([_dot_f32_rhs(tri_ref[...], lw[rs]) for rs in chunk_rows], axis=0)
    L_end = jnp.concatenate([jnp.broadcast_to(L[rs.stop - 1:rs.stop, :], (C, W)) for rs in chunk_rows], axis=0)
    e_neg = jnp.exp(-L)
    e_rem = jnp.exp(L_end - L)
    At, Bt, Kt, Rt = a_vec * jnp.exp(L - lw), b_vec * e_neg, k * e_neg, r * jnp.exp(L)
    Bg, Kg = b_vec * e_rem, k * e_rem
    decay_end = [jnp.exp(L[rs.stop - 1:rs.stop, :]) for rs in chunk_rows]

    HG = RWKV_GROUP_HEADS
    GL = HG * N
    assert C == N and C & (C - 1) == 0
    blk_of = lambda idx: jnp.right_shift(idx, int(math.log2(C)))
    same_head = (blk_of(lax.broadcasted_iota(jnp.int32, (HG * C, GL), 0))
                 == blk_of(lax.broadcasted_iota(jnp.int32, (HG * C, GL), 1)))
    bf = lambda z: z.astype(BF16)
    block_diag = lambda y: jnp.where(same_head, jnp.concatenate([bf(y)] * HG, axis=0), 0.0)
    ri = lax.broadcasted_iota(jnp.int32, (C, GL), 0)
    cj = jnp.bitwise_and(lax.broadcasted_iota(jnp.int32, (C, GL), 1), C - 1)
    strict, incl = ri > cj, ri >= cj
    eye_c = (ri == cj).astype(F32)
    pr = lax.broadcasted_iota(jnp.int32, (N, 2 * N), 0)
    pc = lax.broadcasted_iota(jnp.int32, (N, 2 * N), 1)
    first_head, pair_eye = pc < N, (pr == jnp.bitwise_and(pc, N - 1)).astype(F32)

    def same_head_blocks(z):
        return jnp.where(first_head, z[0:N], z[N:2 * N])
    items = [(ck, gp) for ck in range(len(chunk_rows)) for gp in range(RWKV_HEADS // HG)]
    sl = [(chunk_rows[ck], slice(gp * GL, (gp + 1) * GL)) for ck, gp in items]
    ar = [bf(jnp.concatenate([At[s], Rt[s]], axis=0)) for s in sl]
    zb = [_dot_nt(a, block_diag(Bt[s])) for a, s in zip(ar, sl)]
    zk = [_dot_nt(a, block_diag(Kt[s])) for a, s in zip(ar, sl)]
    n_mat = [jnp.where(strict, z[0:C], 0.0) for z in zb]
    m_mat = [jnp.where(strict, z[0:C], 0.0) for z in zk]
    v_bd = [block_diag(v[s]) for s in sl]
    mv = [_dot(bf(m), vb) for m, vb in zip(m_mat, v_bd)]
    t_inv, n_pow = [eye_c + n for n in n_mat], n_mat
    for _ in range(int(math.log2(C)) - 1):
        n_pow = [_dot(bf(n), block_diag(n)) for n in n_pow]
        t_inv = [t + _dot(bf(t), block_diag(n)) for t, n in zip(t_inv, n_pow)]
    t_bf = [bf(t) for t in t_inv]
    ta = [_dot(t, block_diag(At[s])) for t, s in zip(t_bf, sl)]
    g0 = [_dot(t, block_diag(m)) for t, m in zip(t_bf, mv)]
    p_mat = [bf(jnp.where(incl, z[C:], 0.0)) for z in zb]
    pk_mat = [bf(jnp.where(incl, z[C:], 0.0)) for z in zk]
    q_out = [Rt[s] + _dot(p, block_diag(t)) for p, t, s in zip(p_mat, ta, sl)]
    y0_out = [_dot(p, block_diag(g)) + _dot(pk, vb) for p, g, pk, vb in zip(p_mat, g0, pk_mat, v_bd)]
    bgT = [bf(Bg[s].T) for s in sl]
    a_full = [_dot(b, bf(t)) for b, t in zip(bgT, ta)]
    d_full = [_dot(b, bf(g)) + _dot(bf(Kg[s].T), bf(v[s])) for b, g, s in zip(bgT, g0, sl)]
    for n, (ck, gp) in enumerate(items):
        rows, lanes = sl[n]
        q_ref[0, rows, lanes] = q_out[n].astype(q_ref.dtype)
        y0_ref[0, rows, lanes] = y0_out[n]
        for pp in range(HG // 2):
            blk = slice(pp * 2 * N, (pp + 1) * 2 * N)
            pair_lanes = slice(lanes.start + blk.start, lanes.start + blk.stop)
            a_ref[0, ck, gp * (HG // 2) + pp] = (same_head_blocks(a_full[n][blk, blk])
                                                 + pair_eye * decay_end[ck][:, pair_lanes])
            d_ref[0, ck, gp * (HG // 2) + pp] = same_head_blocks(d_full[n][blk, blk])


def _rwkv_chunks(rw, mu, w0, w_up, a0, a_up, g_up, k_k, k_a, r_k):
    B, T, cols = rw.shape
    C, W, H, N = RWKV_CHUNK, RWKV_WIDTH, RWKV_HEADS, HEAD_DIM
    nc = T // C
    S = RWKV_STEP_CHUNKS
    R = S * C
    assert nc % S == 0
    seg = jnp.asarray(np.kron(np.eye(LANES // N), np.ones((N, N))), BF16)
    tri = jnp.asarray(np.tril(np.ones((C, C))), BF16)
    row = lambda z: z.reshape(1, -1)
    const = lambda b, c: (0, 0)
    vec = pl.BlockSpec((1, W), const)
    tok = pl.BlockSpec((1, R, W), lambda b, c: (b, c, 0))
    mat = pl.BlockSpec((1, S, H // 2, N, 2 * N), lambda b, c: (b, c, 0, 0, 0))
    return pl.pallas_call(
        _rwkv_chunk_kernel,
        grid=(B, nc // S),
        in_specs=[pl.BlockSpec((1, R, cols), lambda b, c: (b, c, 0)),
                  pl.BlockSpec((1, SUBLANES, cols), lambda b, c: (b, jnp.maximum(c * (R // SUBLANES) - 1, 0), 0)),
                  pl.BlockSpec((1, cols), const), vec,
                  pl.BlockSpec((DECAY_LORA, W), const), vec,
                  pl.BlockSpec((AAA_LORA, W), const),
                  pl.BlockSpec((GATE_LORA, W), const), vec, vec, vec,
                  pl.BlockSpec((LANES, LANES), const), pl.BlockSpec((C, C), const)],
        out_specs=[tok, tok, mat, mat, tok, tok],
        out_shape=[jax.ShapeDtypeStruct((B, T, W), BF16), jax.ShapeDtypeStruct((B, T, W), F32),
                   jax.ShapeDtypeStruct((B, nc, H // 2, N, 2 * N), F32),
                   jax.ShapeDtypeStruct((B, nc, H // 2, N, 2 * N), F32),
                   jax.ShapeDtypeStruct((B, T, W), F32), jax.ShapeDtypeStruct((B, T, W), F32)],
        compiler_params=_params("parallel", "parallel"), name="rwkv_chunks",
    )(rw, rw, row(mu), row(w0), w_up.astype(BF16), row(a0), a_up.astype(BF16), g_up.astype(BF16),
      row(k_k), row(k_a), row(r_k), seg, tri)


def _pair_block_diag(x):
    first_head = lax.broadcasted_iota(jnp.int32, x.shape, 1) < x.shape[0]
    zero = jnp.zeros_like(x)
    return jnp.concatenate([jnp.where(first_head, x, zero), jnp.where(first_head, zero, x)], axis=0)


def _pair_side_by_side(z):
    n = z.shape[0] // 2
    return jnp.where(lax.broadcasted_iota(jnp.int32, (n, 2 * n), 1) < n, z[0:n], z[n:])


def _rwkv_state_kernel(a_ref, d_ref, h_ref, h_sc):
    @pl.when(pl.program_id(0) == 0)
    def _():
        h_sc[...] = jnp.zeros(h_sc.shape, F32)

    items = [(b, pair) for b in range(a_ref.shape[0]) for pair in range(a_ref.shape[2])]
    states = [h_sc[b, pair] for b, pair in items]
    for ck in range(a_ref.shape[1]):
        for (b, pair), st in zip(items, states):
            h_ref[b, ck, pair] = _pair_side_by_side(st).astype(h_ref.dtype)
        states = [_dot_hi_lo(_pair_block_diag(a_ref[b, ck, pair]), st) + _pair_block_diag(d_ref[b, ck, pair])
                  for (b, pair), st in zip(items, states)]
    for (b, pair), st in zip(items, states):
        h_sc[b, pair] = st


def _rwkv_readout(h_ref, q_ref, y0_ref, g_ref, bonus_ref, lw_ref, lb_ref, seg_ref):
    C = RWKV_CHUNK
    inv_n = 1.0 / HEAD_DIM
    n_chunks, n_pairs = h_ref.shape[1], h_ref.shape[2]
    items = [(ck, pair, slice(ck * C, (ck + 1) * C), slice(pair * LANES, (pair + 1) * LANES))
             for ck in range(n_chunks) for pair in range(n_pairs)]
    ys = [_dot(q_ref[0, rows, lanes], _pair_block_diag(h_ref[0, ck, pair])) + y0_ref[0, rows, lanes]
          for ck, pair, rows, lanes in items]
    means = [_dot_f32_lhs(y, seg_ref[...], 2) * inv_n for y in ys]
    cen = [y - mean for y, mean in zip(ys, means)]
    var = [_dot_f32_lhs(jnp.square(c), seg_ref[...], 2) * inv_n for c in cen]
    out = []
    for (_, _, rows, lanes), c, v in zip(items, cen, var):
        yn = c * lax.rsqrt(v + LNX_EPS) * lw_ref[:, lanes] + lb_ref[:, lanes]
        out.append((yn + bonus_ref[0, rows, lanes]) * g_ref[0, rows, lanes])
    return jnp.concatenate([jnp.concatenate(out[ck * n_pairs:(ck + 1) * n_pairs], axis=1) for ck in range(n_chunks)],
                           axis=0)


def _rwkv_states(A, D):
    B, nc, P, N, N2 = A.shape
    S = RWKV_SCAN_STEP_CHUNKS
    assert nc % S == 0
    mat = pl.BlockSpec((B, S, P, N, N2), lambda c: (0, c, 0, 0, 0))
    return pl.pallas_call(
        _rwkv_state_kernel,
        grid=(nc // S,),
        in_specs=[mat, mat],
        out_specs=mat,
        out_shape=jax.ShapeDtypeStruct((B, nc, P, N, N2), BF16),
        scratch_shapes=[pltpu.VMEM((B, P, N2, N2), F32)],
        compiler_params=_params("arbitrary"), name="rwkv_state",
    )(A, D)


def _mix_xattn_kernel(x_ref, on_ref, h_ref, qr_ref, y0_ref, g_ref, bonus_ref, lw_ref, lb_ref, seg_ref,
                      wo1_ref, wo2_ref, gx_ref, wq_ref, k_ref, v_ref, wo_ref, o_ref):
    o_rwkv = _rwkv_readout(h_ref, qr_ref, y0_ref, g_ref, bonus_ref, lw_ref, lb_ref, seg_ref)
    x1 = x_ref[0] + _dot(on_ref[0].astype(BF16), wo1_ref[...]) + _dot(o_rwkv.astype(BF16), wo2_ref[...])
    q = _dot(_rms(x1, gx_ref[...]).astype(BF16), wq_ref[...])
    dh = q.shape[-1] // XATTN_HEADS
    qbf = (q * (dh ** -0.5)).astype(BF16)
    outs = []
    for h in range(XATTN_HEADS):
        hs = slice(h * dh, (h + 1) * dh)
        s = _dot_nt(qbf[:, hs], k_ref[0, :, hs])
        e = jnp.exp(s - jnp.max(s, axis=-1, keepdims=True))
        p = e / jnp.sum(e, axis=-1, keepdims=True)
        outs.append(_dot(p.astype(BF16), v_ref[0, :, hs]))
    o = jnp.concatenate(outs, axis=1).astype(BF16)
    o_ref[0] = x1 + _dot(o, wo_ref[...])


def _mix_xattn(x, o_nsa, rwkv_parts, w_out, norm_x_g, w_q, mem_k, mem_v, w_o):
    B, T, D = x.shape
    M = mem_k.shape[1]
    tm = min(ROW_TILE, T)
    assert tm % RWKV_CHUNK == 0
    h_start, Q, Y0, g, bonus, lnx_w, lnx_b = rwkv_parts
    _, _, P, N, N2 = h_start.shape
    W = RWKV_WIDTH
    seg = jnp.asarray(np.kron(np.eye(2), np.ones((HEAD_DIM, HEAD_DIM))), BF16)
    wo1, wo2 = w_out[:NSA_WIDTH].astype(BF16), w_out[NSA_WIDTH:].astype(BF16)
    const = lambda b, i: (0, 0)
    tile = lambda w: pl.BlockSpec((1, tm, w), lambda b, i: (b, i, 0))
    vec = pl.BlockSpec((1, W), const)
    return pl.pallas_call(
        _mix_xattn_kernel,
        grid=(B, T // tm),
        in_specs=[tile(D), tile(NSA_WIDTH),
                  pl.BlockSpec((1, tm // RWKV_CHUNK, P, N, N2), lambda b, i: (b, i, 0, 0, 0)),
                  tile(W), tile(W), tile(W), tile(W), vec, vec, pl.BlockSpec((N2, N2), const),
                  pl.BlockSpec(wo1.shape, const), pl.BlockSpec(wo2.shape, const),
                  pl.BlockSpec((1, D), const), pl.BlockSpec((D, D), const),
                  pl.BlockSpec((1, M, D), lambda b, i: (b, 0, 0)), pl.BlockSpec((1, M, D), lambda b, i: (b, 0, 0)),
                  pl.BlockSpec((D, D), const)],
        out_specs=tile(D),
        out_shape=jax.ShapeDtypeStruct((B, T, D), F32),
        compiler_params=_params("parallel", "parallel"), name="mix_xattn",
    )(x, o_nsa, h_start, Q, Y0, g, bonus, lnx_w.reshape(1, W), lnx_b.reshape(1, W), seg,
      wo1, wo2, norm_x_g.reshape(1, D), w_q.astype(BF16), mem_k, mem_v, w_o.astype(BF16))


def _ffn_kernel(x_ref, g_ref, wg_ref, wu_ref, wd_ref, gf_ref, o_ref, *, final_norm):
    x = x_ref[...]
    h = _rms(x, g_ref[...]).astype(BF16)
    act = (jax.nn.silu(_dot(h, wg_ref[...])) * _dot(h, wu_ref[...])).astype(BF16)
    y = x + _dot(act, wd_ref[...])
    o_ref[...] = _rms(y, gf_ref[...]) if final_norm else y


def _ffn(x, norm_g, w_gate, w_up, w_down, final_g, final_norm):
    R, D = x.shape
    F = w_gate.shape[1]
    tm = min(FFN_ROW_TILE, R)
    const = lambda i: (0, 0)
    resident = lambda shape: pl.BlockSpec(shape, const, pipeline_mode=pl.Buffered(1))
    return pl.pallas_call(
        functools.partial(_ffn_kernel, final_norm=final_norm),
        grid=(R // tm,),
        in_specs=[pl.BlockSpec((tm, D), lambda i: (i, 0)), pl.BlockSpec((1, D), const),
                  resident((D, F)), resident((D, F)), resident((F, D)), pl.BlockSpec((1, D), const)],
        out_specs=pl.BlockSpec((tm, D), lambda i: (i, 0)),
        out_shape=jax.ShapeDtypeStruct((R, D), F32),
        compiler_params=_params("parallel"), name="ffn",
    )(x, norm_g.reshape(1, D), w_gate.astype(BF16), w_up.astype(BF16), w_down.astype(BF16),
      final_g.reshape(1, D))


def _overlap_matrix(n_cmp_pad, n_sel):
    c = np.arange(n_cmp_pad)[:, None] * CMP_STRIDE
    s = np.arange(n_sel)[None, :] * SEL_BLOCK
    return ((c <= s + SEL_BLOCK - 1) & (c + CMP_LEN - 1 >= s)).astype(np.float32)


def _layer(x, mem, rel_bias, final_g, is_last, norm_mix_g, w_in, nsa_gate_b, cmp_pe_k, cmp_pe_v,
           cmp_k_w1, cmp_k_b1, cmp_k_w2, cmp_v_w1, cmp_v_b1, cmp_v_w2,
           rwkv_mu, rwkv_w0, rwkv_w_up, rwkv_a0, rwkv_a_up, rwkv_g_up,
           rwkv_k_k, rwkv_k_a, rwkv_r_k, rwkv_lnx_w, rwkv_lnx_b, w_out,
           norm_x_g, norm_mem_g, w_q_x, w_kv_x, w_o_x, norm_ffn_g, w_gate, w_up, w_down):
    B, T, D = x.shape
    q, kvc, kaug, vsT, vwT, gates, rw = _proj_in(x, norm_mix_g, w_in, nsa_gate_b)

    n16 = T // CMP_STRIDE
    kc = _compress(kvc, 0, cmp_pe_k, cmp_k_w1, cmp_k_b1, cmp_k_w2, False)
    vcT = _compress(kvc, 1, cmp_pe_v, cmp_v_w1, cmp_v_b1, cmp_v_w2, True)
    bias_tiles, cmp_bias = _bias_tiles(rel_bias, n16)
    o_nsa = _nsa_attention(q, kc, vcT, kaug, vsT, vwT,
                           gates, jnp.asarray(_overlap_matrix(n16, T // SEL_BLOCK).T, BF16), bias_tiles, cmp_bias)

    Q, Y0, A, Dm, g, bonus = _rwkv_chunks(rw, rwkv_mu, rwkv_w0, rwkv_w_up, rwkv_a0,
                                          rwkv_a_up, rwkv_g_up, rwkv_k_k, rwkv_k_a, rwkv_r_k.reshape(-1))
    rwkv_parts = (_rwkv_states(A, Dm), Q, Y0, g, bonus, rwkv_lnx_w, rwkv_lnx_b)

    M = mem.shape[1]
    (kv_mem,) = _norm_matmul(mem.reshape(B * M, D), norm_mem_g, [w_kv_x.astype(BF16)], [None], [BF16], ROW_TILE)
    kv_mem = kv_mem.reshape(B, M, 2 * D)
    x = _mix_xattn(x, o_nsa, rwkv_parts, w_out, norm_x_g, w_q_x, kv_mem[..., :D], kv_mem[..., D:], w_o_x)
    x = _ffn(x.reshape(B * T, D), norm_ffn_g, w_gate, w_up, w_down, final_g, is_last)
    return x.reshape(B, T, D)


def kernel(x, mem, rel_bias, norm_f_g, norm_mix_g, w_in, nsa_gate_b, cmp_pe_k, cmp_pe_v, cmp_k_w1, cmp_k_b1, cmp_k_w2, cmp_v_w1, cmp_v_b1, cmp_v_w2, rwkv_mu, rwkv_w0, rwkv_w_up, rwkv_a0, rwkv_a_up, rwkv_g_up, rwkv_k_k, rwkv_k_a, rwkv_r_k, rwkv_lnx_w, rwkv_lnx_b, w_out, norm_x_g, norm_mem_g, w_q_x, w_kv_x, w_o_x, norm_ffn_g, w_gate, w_up, w_down):
    stacked = (norm_mix_g, w_in, nsa_gate_b, cmp_pe_k, cmp_pe_v, cmp_k_w1, cmp_k_b1, cmp_k_w2, cmp_v_w1,
               cmp_v_b1, cmp_v_w2, rwkv_mu, rwkv_w0, rwkv_w_up, rwkv_a0, rwkv_a_up, rwkv_g_up, rwkv_k_k,
               rwkv_k_a, rwkv_r_k, rwkv_lnx_w, rwkv_lnx_b, w_out, norm_x_g, norm_mem_g, w_q_x, w_kv_x, w_o_x,
               norm_ffn_g, w_gate, w_up, w_down)
    depth = w_in.shape[0]
    for l in range(depth):
        x = _layer(x, mem, rel_bias, norm_f_g, l == depth - 1, *[p[l] for p in stacked])
    return x
```

```python
import functools
import math

import numpy as np
import jax
import jax.numpy as jnp
from jax import lax
from jax.experimental import pallas as pl
from jax.experimental.pallas import tpu as pltpu

F32 = jnp.float32
BF16 = jnp.bfloat16

LANES = 128
SUBLANES = 8
BF16_ROWS = 16
VMEM_LIMIT_BYTES = 56 * 1024 * 1024

HEAD_DIM = 64
NSA_HEADS = 8
NSA_KV_HEADS = 2
NSA_GROUP = NSA_HEADS // NSA_KV_HEADS
NSA_WIDTH = NSA_HEADS * HEAD_DIM
KV_WIDTH = NSA_KV_HEADS * HEAD_DIM
RWKV_HEADS = 8
RWKV_WIDTH = RWKV_HEADS * HEAD_DIM
CMP_LEN = 32
CMP_STRIDE = 16
SEL_BLOCK = 64
SEL_SHIFT = 6
SEL_TOP = 16
WINDOW = 512
Q_BLOCK = 128
DECAY_LORA = 64
AAA_LORA = 64
GATE_LORA = 128
N_BUCKETS = 32
MAX_DISTANCE = 2048
XATTN_HEADS = 4
RMS_EPS = 1e-6
LNX_EPS = 64e-5
FORCE_SCORE = 1e4
NEG_SCORE = -1e9
MASK_SCORE = -1e30
LOG2E = math.log2(math.e)
RWKV_COLS = 3 * RWKV_WIDTH + DECAY_LORA + AAA_LORA + GATE_LORA
NSA_COLS = NSA_WIDTH + 6 * KV_WIDTH + 3 * NSA_HEADS

KEY_CHUNK = 128
RWKV_CHUNK = 64
RWKV_SCAN_STEP_CHUNKS = 8
RWKV_GROUP_HEADS = 2
RWKV_STEP_CHUNKS = 8
ROW_TILE = 512
FFN_ROW_TILE = 512


def _t5_thresholds():
    d = np.arange(0, 2 * MAX_DISTANCE, dtype=np.int64)
    max_exact = N_BUCKETS // 2
    nf = np.maximum(d, 1).astype(np.float32)
    large = max_exact + (np.log(nf / np.float32(max_exact)) / np.float32(math.log(MAX_DISTANCE / max_exact))
                         * np.float32(N_BUCKETS - max_exact)).astype(np.int32)
    bucket = np.where(d < max_exact, d, np.minimum(large, N_BUCKETS - 1))
    return [int(np.argmax(bucket >= k)) for k in range(N_BUCKETS)]


T5_THRESHOLDS = _t5_thresholds()
N_BIAS_TILES = -(-(T5_THRESHOLDS[-1] + KEY_CHUNK) // KEY_CHUNK) + 1
TILE_MASKED = N_BIAS_TILES
TILE_WINDOW_EDGE = N_BIAS_TILES + 1
N_ALL_TILES = N_BIAS_TILES + 2
SEL_STEP_BLOCKS = 16
SEL_STAGE_TILES = 4
SEL_GROUP_STAGES = SEL_STEP_BLOCKS * SEL_BLOCK // (SEL_STAGE_TILES * KEY_CHUNK)
NSA_STEP_QBLOCKS = 4
V_ROWS = HEAD_DIM + BF16_ROWS
GATE_ROWS = 16


def _params(*semantics):
    return pltpu.CompilerParams(dimension_semantics=semantics, vmem_limit_bytes=VMEM_LIMIT_BYTES)


def _rms(x, g):
    return x * lax.rsqrt(jnp.mean(x * x, axis=-1, keepdims=True) + RMS_EPS) * g


def _dot(a, b, **kw):
    return jnp.dot(a, b, preferred_element_type=F32, **kw)


def _split3(x):
    hi = x.astype(BF16)
    r1 = x - hi.astype(F32)
    mid = r1.astype(BF16)
    lo = (r1 - mid.astype(F32)).astype(BF16)
    return hi, mid, lo


def _dot_f32_lhs(x, w01, pieces=3):
    w = w01.astype(BF16)
    hi, mid, lo = _split3(x)
    return _dot(hi, w) + (_dot(mid, w) + _dot(lo, w) if pieces == 3 else _dot(mid, w))


def _dot_f32_rhs(w01, x):
    w = w01.astype(BF16)
    hi, mid, lo = _split3(x)
    return _dot(w, hi) + (_dot(w, mid) + _dot(w, lo))


def _dot_hi_lo(a, b):
    a_hi = a.astype(BF16)
    a_lo = (a - a_hi.astype(F32)).astype(BF16)
    b_hi = b.astype(BF16)
    b_lo = (b - b_hi.astype(F32)).astype(BF16)
    return _dot(a_hi, b_hi) + (_dot(a_hi, b_lo) + _dot(a_lo, b_hi))


def _dot_nt(a, b, **kw):
    return lax.dot_general(a, b, (((1,), (1,)), ((), ())), preferred_element_type=F32, **kw)


def _norm_matmul_kernel(x_ref, g_ref, *refs, nseg, bias_flags):
    nb = sum(bias_flags)
    w_refs, b_refs, o_refs = refs[:nseg], refs[nseg:nseg + nb], refs[nseg + nb:]
    xn = _rms(x_ref[...], g_ref[...]).astype(BF16)
    bi = 0
    for s in range(nseg):
        y = _dot(xn, w_refs[s][...])
        if bias_flags[s]:
            y = y + b_refs[bi][...]
            bi += 1
        o_refs[s][...] = y.astype(o_refs[s].dtype)


def _norm_matmul(x, g, weights, biases, out_dtypes, row_tile):
    R, D = x.shape
    tm = min(row_tile, R)
    assert R % tm == 0
    nseg = len(weights)
    bias_flags = tuple(b is not None for b in biases)
    const = lambda i: (0, 0)
    in_specs = [pl.BlockSpec((tm, D), lambda i: (i, 0)), pl.BlockSpec((1, D), const)]
    in_specs += [pl.BlockSpec(w.shape, const) for w in weights]
    in_specs += [pl.BlockSpec((1, b.shape[-1]), const) for b in biases if b is not None]
    out_specs = [pl.BlockSpec((tm, w.shape[1]), lambda i: (i, 0)) for w in weights]
    out_shape = [jax.ShapeDtypeStruct((R, w.shape[1]), dt) for w, dt in zip(weights, out_dtypes)]
    return pl.pallas_call(
        functools.partial(_norm_matmul_kernel, nseg=nseg, bias_flags=bias_flags),
        grid=(R // tm,), in_specs=in_specs, out_specs=out_specs, out_shape=out_shape,
        compiler_params=_params("parallel"), name="norm_matmul",
    )(x, g.reshape(1, D), *weights, *[b.reshape(1, -1) for b in biases if b is not None])


def _proj_in_kernel(x_ref, g_ref, wq_ref, wc_ref, wk_ref, wvT_ref, wgT_ref, bg_ref, wr_ref,
                    q_ref, kvc_ref, kaug_ref, vsT_ref, vwT_ref, gate_ref, rw_ref, *, seq_len):
    tm = x_ref.shape[0]
    xn = _rms(x_ref[...], g_ref[...]).astype(BF16)
    qT = (_dot_nt(wq_ref[...], xn) * (HEAD_DIM ** -0.5 * LOG2E)).astype(BF16)
    for j in range(tm // Q_BLOCK):
        for hg in range(NSA_HEADS):
            h, g = divmod(hg, NSA_GROUP)
            q_ref[0, j, h, :, g * Q_BLOCK:(g + 1) * Q_BLOCK] = qT[hg * HEAD_DIM:(hg + 1) * HEAD_DIM,
                                                                  j * Q_BLOCK:(j + 1) * Q_BLOCK]
    kvc_ref[0] = _dot(xn, wc_ref[...])
    rw_ref[0] = _dot(xn, wr_ref[...])
    k_all = _dot(xn, wk_ref[...])
    tok = lax.rem(pl.program_id(0) * tm, seq_len) + lax.broadcasted_iota(jnp.int32, k_all.shape, 0)
    lane = lax.broadcasted_iota(jnp.int32, k_all.shape, 1)
    blk = jnp.bitwise_and(jnp.right_shift(tok, SEL_SHIFT), SEL_STEP_BLOCKS - 1)
    hot = (jnp.bitwise_and(lane, LANES - 1) == HEAD_DIM + blk) & (lane < NSA_KV_HEADS * LANES)
    kaug_ref[0] = jnp.where(hot, 1.0, k_all).astype(BF16)
    vT = _dot_nt(wvT_ref[...], xn)
    row = lax.broadcasted_iota(jnp.int32, vT.shape, 0)
    ones_row = row == HEAD_DIM
    for grp in range(1, 2 * NSA_KV_HEADS):
        ones_row = ones_row | (row == grp * V_ROWS + HEAD_DIM)
    vT = jnp.where(ones_row, 1.0, vT).astype(BF16)
    half = NSA_KV_HEADS * V_ROWS
    stage_keys = vsT_ref.shape[3]
    for c in range(tm // stage_keys):
        vsT_ref[0, c] = vT[0:half, c * stage_keys:(c + 1) * stage_keys]
    for c in range(tm // KEY_CHUNK):
        vwT_ref[0, c] = vT[half:, c * KEY_CHUNK:(c + 1) * KEY_CHUNK]
    gT = _dot_nt(wgT_ref[...], xn) + bg_ref[...]
    for j in range(tm // Q_BLOCK):
        for h in range(NSA_KV_HEADS):
            gate_ref[0, j, h] = gT[h * GATE_ROWS:(h + 1) * GATE_ROWS, j * Q_BLOCK:(j + 1) * Q_BLOCK]


def _proj_in(x, norm_g, w_in, gate_b):
    B, T, D = x.shape
    Hkv, G, dh = NSA_KV_HEADS, NSA_GROUP, HEAD_DIM
    tm = ROW_TILE
    stage_keys = SEL_STAGE_TILES * KEY_CHUNK
    assert T % tm == 0 and tm % Q_BLOCK == 0 and tm % stage_keys == 0
    kv0 = NSA_WIDTH
    g0 = kv0 + 6 * KV_WIDTH
    stream = lambda s: w_in[:, kv0 + s * KV_WIDTH:kv0 + (s + 1) * KV_WIDTH].reshape(D, Hkv, dh)
    pad_cols = lambda w: jnp.pad(w, ((0, 0), (0, 0), (0, LANES - dh))).reshape(D, Hkv * LANES)
    pad_rows = lambda w: jnp.pad(w.transpose(1, 2, 0), ((0, 0), (0, V_ROWS - dh), (0, 0))).reshape(Hkv * V_ROWS, D)
    w_k = jnp.concatenate([pad_cols(stream(2)), pad_cols(stream(4))], axis=1)
    w_vT = jnp.concatenate([pad_rows(stream(3)), pad_rows(stream(5))], axis=0)
    reorder = lambda a: a.reshape(-1, Hkv, G, 3).transpose(1, 3, 2, 0).reshape(Hkv, 3 * G, -1)
    pad_gate = lambda a: jnp.pad(a, ((0, 0), (0, GATE_ROWS - 3 * G), (0, 0))).reshape(Hkv * GATE_ROWS, -1)
    w_gT = pad_gate(reorder(w_in[:, g0:NSA_COLS]))
    b_g = pad_gate(reorder(gate_b.reshape(1, -1)))
    weights = [w_in[:, :kv0].T, w_in[:, kv0:kv0 + 2 * KV_WIDTH], w_k, w_vT, w_gT]
    weights = [w.astype(BF16) for w in weights] + [b_g, w_in[:, NSA_COLS:].astype(BF16)]
    nt = T // tm
    rows = lambda n: pl.BlockSpec((1, tm, n), lambda i: (i // nt, i % nt, 0))
    const = lambda i: (0, 0)
    return pl.pallas_call(
        functools.partial(_proj_in_kernel, seq_len=T),
        grid=(B * nt,),
        in_specs=[pl.BlockSpec((tm, D), lambda i: (i, 0)), pl.BlockSpec((1, D), const)]
        + [pl.BlockSpec(w.shape, const) for w in weights],
        out_specs=[pl.BlockSpec((1, tm // Q_BLOCK, Hkv, dh, G * Q_BLOCK), lambda i: (i // nt, i % nt, 0, 0, 0)),
                   rows(2 * KV_WIDTH), rows(2 * Hkv * LANES),
                   pl.BlockSpec((1, tm // stage_keys, Hkv * V_ROWS, stage_keys), lambda i: (i // nt, i % nt, 0, 0)),
                   pl.BlockSpec((1, tm // KEY_CHUNK, Hkv * V_ROWS, KEY_CHUNK), lambda i: (i // nt, i % nt, 0, 0)),
                   pl.BlockSpec((1, tm // Q_BLOCK, Hkv, GATE_ROWS, Q_BLOCK), lambda i: (i // nt, i % nt, 0, 0, 0)),
                   rows(RWKV_COLS)],
        out_shape=[jax.ShapeDtypeStruct((B, T // Q_BLOCK, Hkv, dh, G * Q_BLOCK), BF16),
                   jax.ShapeDtypeStruct((B, T, 2 * KV_WIDTH), F32),
                   jax.ShapeDtypeStruct((B, T, 2 * Hkv * LANES), BF16),
                   jax.ShapeDtypeStruct((B, T // stage_keys, Hkv * V_ROWS, stage_keys), BF16),
                   jax.ShapeDtypeStruct((B, T // KEY_CHUNK, Hkv * V_ROWS, KEY_CHUNK), BF16),
                   jax.ShapeDtypeStruct((B, T // Q_BLOCK, Hkv, GATE_ROWS, Q_BLOCK), F32),
                   jax.ShapeDtypeStruct((B, T, RWKV_COLS), F32)],
        compiler_params=_params("parallel"), name="proj_in",
    )(x.reshape(B * T, D), norm_g.reshape(1, D), *weights)


def _compress_kernel(x_ref, pe_ref, w1_ref, b1_ref, w2_ref, o_ref, *, transpose_out):
    n16 = x_ref.shape[1] // CMP_STRIDE
    hidden = w1_ref.shape[2] // NSA_KV_HEADS
    lo = jnp.zeros((n16, w1_ref.shape[2]), F32)
    hi = jnp.zeros((n16, w1_ref.shape[2]), F32)
    for l in range(CMP_STRIDE):
        rows = x_ref[0, pl.ds(l, n16, stride=CMP_STRIDE), :]
        lo = lo + _dot((rows + pe_ref[l:l + 1, :]).astype(BF16), w1_ref[l])
        hi = hi + _dot((rows + pe_ref[CMP_STRIDE + l:CMP_STRIDE + l + 1, :]).astype(BF16), w1_ref[CMP_STRIDE + l])
    h = lo + pltpu.roll(hi, n16 - 1, axis=0) + b1_ref[...]
    h = jax.nn.gelu(h).astype(BF16)
    for hkv in range(NSA_KV_HEADS):
        hh = h[:, hkv * hidden:(hkv + 1) * hidden]
        if transpose_out:
            o_ref[0, hkv] = _dot_nt(w2_ref[...], hh).astype(o_ref.dtype)
        else:
            o_ref[0, hkv] = _dot(hh, w2_ref[...]).astype(o_ref.dtype)


def _compress(kvc, stream, pe, w1, b1, w2, transpose_out):
    B, T, _ = kvc.shape
    H, dh = NSA_KV_HEADS, HEAD_DIM
    n16 = T // CMP_STRIDE
    hidden = w1.shape[1]
    w1r = w1.astype(BF16).reshape(CMP_LEN, dh, hidden)
    w1_bd = jnp.concatenate([jnp.pad(w1r, ((0, 0), (0, 0), (h * hidden, (H - 1 - h) * hidden))) for h in range(H)],
                            axis=1)
    w2b = (w2.T if transpose_out else w2).astype(BF16)
    oshape = (B, H, dh, n16) if transpose_out else (B, H, n16, dh)
    return pl.pallas_call(
        functools.partial(_compress_kernel, transpose_out=transpose_out),
        grid=(B,),
        in_specs=[pl.BlockSpec((1, T, H * dh), lambda b: (b, 0, stream)),
                  pl.BlockSpec((CMP_LEN, H * dh), lambda b: (0, 0)),
                  pl.BlockSpec(w1_bd.shape, lambda b: (0, 0, 0)),
                  pl.BlockSpec((1, H * hidden), lambda b: (0, 0)),
                  pl.BlockSpec(w2b.shape, lambda b: (0, 0))],
        out_specs=pl.BlockSpec((1,) + oshape[1:], lambda b: (b, 0, 0, 0)),
        out_shape=jax.ShapeDtypeStruct(oshape, BF16),
        compiler_params=_params("parallel"), name="nsa_compress",
    )(kvc, jnp.tile(pe, (1, H)), w1_bd, jnp.tile(b1.reshape(1, hidden), (1, H)), w2b)


def _bias_of_distance(tab_ref, h, d):
    val = jnp.full(d.shape, tab_ref[h, 0], F32)
    for k in range(1, N_BUCKETS):
        val = jnp.where(d >= T5_THRESHOLDS[k], tab_ref[h, k], val)
    return val * LOG2E


def _bias_tiles_kernel(tab_ref, bt_ref, cb_ref, *, n_cmp_pad):
    hkv = pl.program_id(0)
    j = lax.broadcasted_iota(jnp.int32, (KEY_CHUNK, Q_BLOCK), 0)
    i = lax.broadcasted_iota(jnp.int32, (KEY_CHUNK, Q_BLOCK), 1)
    r2 = lax.broadcasted_iota(jnp.int32, (2 * n_cmp_pad, Q_BLOCK), 0)
    i2 = lax.broadcasted_iota(jnp.int32, (2 * n_cmp_pad, Q_BLOCK), 1)
    l2 = r2 - (n_cmp_pad - KEY_CHUNK)
    d2 = i2 - CMP_STRIDE * l2 + (CMP_STRIDE * KEY_CHUNK - Q_BLOCK - (CMP_LEN - 1))
    hidden2 = (l2 >= KEY_CHUNK) | ((l2 >= 0) & (d2 < 0))
    d2 = jnp.where((l2 >= 0) & (l2 < KEY_CHUNK), d2, 2 * MAX_DISTANCE)
    for g in range(NSA_GROUP):
        h = hkv * NSA_GROUP + g
        lanes = slice(g * Q_BLOCK, (g + 1) * Q_BLOCK)
        for m in range(N_BIAS_TILES):
            tile = _bias_of_distance(tab_ref, h, m * KEY_CHUNK + i - j)
            if m == 0:
                tile = jnp.where(j <= i, tile, MASK_SCORE)
            bt_ref[0, m, :, lanes] = tile
        bt_ref[0, TILE_MASKED, :, lanes] = jnp.full((KEY_CHUNK, Q_BLOCK), MASK_SCORE, F32)
        edge = _bias_of_distance(tab_ref, h, WINDOW + i - j)
        bt_ref[0, TILE_WINDOW_EDGE, :, lanes] = jnp.where(j > i, edge, MASK_SCORE)
        cb_ref[0, :, lanes] = jnp.where(hidden2, MASK_SCORE, _bias_of_distance(tab_ref, h, d2))


def _bias_tiles(rel_bias, n_cmp_pad):
    assert CMP_STRIDE * KEY_CHUNK - Q_BLOCK - (CMP_LEN - 1) >= T5_THRESHOLDS[-1]
    GQ = NSA_GROUP * Q_BLOCK
    return pl.pallas_call(
        functools.partial(_bias_tiles_kernel, n_cmp_pad=n_cmp_pad),
        grid=(NSA_KV_HEADS,),
        in_specs=[pl.BlockSpec(memory_space=pltpu.SMEM)],
        out_specs=[pl.BlockSpec((1, N_ALL_TILES, KEY_CHUNK, GQ), lambda h: (h, 0, 0, 0)),
                   pl.BlockSpec((1, 2 * n_cmp_pad, GQ), lambda h: (h, 0, 0))],
        out_shape=[jax.ShapeDtypeStruct((NSA_KV_HEADS, N_ALL_TILES, KEY_CHUNK, GQ), F32),
                   jax.ShapeDtypeStruct((NSA_KV_HEADS, 2 * n_cmp_pad, GQ), F32)],
        compiler_params=_params("parallel"), name="t5_bias_tiles",
    )(rel_bias.T)


def _nsa_kernel(q_ref, kc_ref, vcT_ref, ks_ref, vsT_ref, kw_ref, vwT_ref, gate_ref, ovT_ref, bt_ref, cb_ref,
                o_ref, acc_sc, qaug_sc, seladd_sc, s0_sc, s1_sc, p0_sc, p1_sc, *, n_sel, n_cmp_pad):
    G, NQ = NSA_GROUP, NSA_STEP_QBLOCKS
    GQ = G * Q_BLOCK
    qbs = [pl.program_id(2) * NQ + x for x in range(NQ)]
    per_block = lambda fn: jnp.concatenate([fn(x) for x in range(NQ)], axis=1)
    tile_g = lambda a: jnp.concatenate([a] * G, axis=1)
    qT = per_block(lambda x: q_ref[0, x, 0])
    qaug_sc[0:HEAD_DIM, :] = qT
    qaug_sc[HEAD_DIM:, :] = jnp.zeros((qaug_sc.shape[0] - HEAD_DIM, NQ * GQ), BF16)
    lane_q = lax.broadcasted_iota(jnp.int32, (1, Q_BLOCK), 1)
    t = per_block(lambda x: qbs[x] * Q_BLOCK + lane_q)

    def bias_tile(dist_of_block):
        def one(x):
            dist = dist_of_block(x)
            return bt_ref[0, jnp.where(dist < 0, TILE_MASKED, jnp.minimum(dist, N_BIAS_TILES - 1))]
        return per_block(one)

    n_back = WINDOW // KEY_CHUNK
    q_win = qaug_sc[...]
    win = {}

    def win_scores(x, back):
        kc = jnp.maximum(qbs[x] - back, 0)
        edge = TILE_WINDOW_EDGE if back == n_back else back
        tile = jnp.where(qbs[x] >= back, edge, TILE_MASKED)
        k_chunk = kw_ref[0, pl.ds(pl.multiple_of(kc * KEY_CHUNK, KEY_CHUNK), KEY_CHUNK), :]
        win["s", x, back] = _dot(k_chunk, q_win[:, x * GQ:(x + 1) * GQ]) + bt_ref[0, tile]
        col_max = jnp.max(win["s", x, back], axis=0, keepdims=True)
        win["m", x] = jnp.maximum(win["m", x], col_max) if ("m", x) in win else col_max

    def win_weights(x, back):
        win["p", x, back] = jnp.exp2(win["s", x, back] - win["m", x]).astype(BF16)

    def win_values(x, back):
        pv = _dot(vwT_ref[0, jnp.maximum(qbs[x] - back, 0)], win["p", x, back])
        win["acc", x] = win["acc", x] + pv if ("acc", x) in win else pv

    backs = list(range(n_back, -1, -1))
    window_work = [functools.partial(fn, x, b) for x in range(NQ)
                   for fn in (win_scores, win_weights, win_values) for b in backs]

    def cmp_branch(n_rows):
        def cmp_bias(x):
            start = pl.multiple_of(n_cmp_pad - (Q_BLOCK // CMP_STRIDE) * (qbs[x] + 1), SUBLANES)
            return cb_ref[0, pl.ds(start, n_rows), :]
        s = _dot(kc_ref[0, 0, 0:n_rows, :], qT) + per_block(cmp_bias)
        e = jnp.exp2(s - jnp.maximum(jnp.max(s, axis=0, keepdims=True), 0.1 * MASK_SCORE))
        p = e * (1.0 / jnp.maximum(jnp.sum(e, axis=0, keepdims=True), 1e-30))
        o_cmp = _dot(vcT_ref[0, 0, :, 0:n_rows], p.astype(BF16))
        psum = per_block(lambda x: sum(p[:, x * GQ + g * Q_BLOCK:x * GQ + (g + 1) * Q_BLOCK] for g in range(G)))
        return o_cmp, _dot_f32_rhs(ovT_ref[:, 0:n_rows], psum)

    half_rows = n_cmp_pad // 2
    n_visible = (Q_BLOCK // CMP_STRIDE) * (qbs[-1] + 1)
    if half_rows % LANES == 0:
        o_c, imp = lax.cond(n_visible <= half_rows, lambda: cmp_branch(half_rows), lambda: cmp_branch(n_cmp_pad))
    else:
        o_c, imp = cmp_branch(n_cmp_pad)
    bj = lax.broadcasted_iota(jnp.int32, (n_sel, NQ * Q_BLOCK), 0)
    cur = jnp.right_shift(t, SEL_SHIFT)
    forced = (bj == 0) | (bj == cur) | (bj == cur - 1)
    valid = bj * SEL_BLOCK <= t
    score = jnp.where(forced, -jnp.inf, jnp.where(valid, imp, NEG_SCORE))
    bjf = bj.astype(F32)
    n_rounds = max(min(SEL_TOP, n_sel) - 3, 0)
    for rnd in range(n_rounds):
        mx = jnp.max(score, axis=0, keepdims=True)
        first = jnp.min(jnp.where(score == mx, bjf, float(n_sel)), axis=0, keepdims=True)
        score = jnp.where(bjf == first, -jnp.inf, score)
        take = -(-len(window_work) // (n_rounds - rnd))
        for piece in window_work[:take]:
            piece()
        window_work = window_work[take:]
    for piece in window_work:
        piece()
    o_w = per_block(lambda x: win["acc", x][0:HEAD_DIM] / jnp.maximum(win["acc", x][HEAD_DIM:HEAD_DIM + 1], 1e-30))
    sel_add = jnp.where(score == -jnp.inf, 0.0, MASK_SCORE)
    seladd_sc[...] = per_block(lambda x: tile_g(sel_add[:, x * Q_BLOCK:(x + 1) * Q_BLOCK]))

    stage_keys = SEL_STAGE_TILES * KEY_CHUNK
    n_stages = qbs[-1] // SEL_STAGE_TILES + 1
    last_stage = ks_ref.shape[1] // stage_keys - 1

    def scores(k, s_buf):
        kk = jnp.minimum(k, last_stage)
        blk0 = pl.multiple_of(kk // SEL_GROUP_STAGES * SEL_STEP_BLOCKS, SEL_STEP_BLOCKS)
        qaug_sc[HEAD_DIM:HEAD_DIM + SEL_STEP_BLOCKS, :] = seladd_sc[pl.ds(blk0, SEL_STEP_BLOCKS), :].astype(BF16)
        k0 = pl.multiple_of(kk * stage_keys, stage_keys)
        bias = jnp.concatenate([bias_tile(lambda x, c=c: qbs[x] - (k * SEL_STAGE_TILES + c))
                                for c in range(SEL_STAGE_TILES)], axis=0)
        s = _dot(ks_ref[0, pl.ds(k0, stage_keys), :], qaug_sc[...]) + bias
        s_buf[...] = s
        return jnp.max(s, axis=0, keepdims=True)

    def weights(m, col_max, s_buf, p_buf):
        m_new = jnp.maximum(m, col_max)
        p_buf[...] = jnp.exp2(s_buf[...] - m_new).astype(BF16)
        return m_new, jnp.exp2(m - m_new)

    def accumulate(k, alpha, p_buf):
        acc_sc[...] = alpha * acc_sc[...] + _dot(vsT_ref[0, jnp.clip(k, 0, last_stage)], p_buf[...])

    def pair(j, carry):
        m, col_max, alpha = carry
        k = 2 * j
        col_max1 = scores(k + 1, s1_sc)
        m, alpha0 = weights(m, col_max, s0_sc, p0_sc)
        accumulate(k - 1, alpha, p1_sc)
        col_max2 = scores(k + 2, s0_sc)
        m, alpha1 = weights(m, col_max1, s1_sc, p1_sc)
        accumulate(k, alpha0, p0_sc)
        return m, col_max2, alpha1

    acc_sc[...] = jnp.zeros(acc_sc.shape, F32)
    p1_sc[...] = jnp.zeros(p1_sc.shape, BF16)
    m_init = jnp.full((1, NQ * GQ), 0.1 * MASK_SCORE, F32)
    carry = (m_init, scores(0, s0_sc), jnp.ones((1, NQ * GQ), F32))
    n_pairs = n_stages // 2
    m, col_max, alpha = lax.fori_loop(0, n_pairs, pair, carry)
    last = 2 * n_pairs

    @pl.when(n_stages % 2 == 1)
    def _():
        _, alpha_last = weights(m, col_max, s0_sc, p0_sc)
        accumulate(last - 1, alpha, p1_sc)
        accumulate(last, alpha_last, p0_sc)

    @pl.when(n_stages % 2 == 0)
    def _():
        accumulate(last - 1, alpha, p1_sc)
    o_s = acc_sc[0:HEAD_DIM, :] / jnp.maximum(acc_sc[HEAD_DIM:HEAD_DIM + 1, :], 1e-30)

    def gate_row(br):
        def one(x):
            gt = jax.nn.sigmoid(gate_ref[0, x, 0])
            return jnp.concatenate([gt[br * G + g:br * G + g + 1, :] for g in range(G)], axis=1)
        return per_block(one)
    o = gate_row(0) * o_c + gate_row(1) * o_s + gate_row(2) * o_w
    for x in range(NQ):
        o_ref[0, x * Q_BLOCK:(x + 1) * Q_BLOCK, :] = jnp.concatenate(
            [o[:, x * GQ + g * Q_BLOCK:x * GQ + (g + 1) * Q_BLOCK].T for g in range(G)], axis=1)


def _nsa_attention(q, kc, vcT, kaug, vsT, vwT, gates, overlapT, bias_tiles, cmp_bias):
    B, T, _ = kaug.shape
    Hkv, G = NSA_KV_HEADS, NSA_GROUP
    GQ = G * Q_BLOCK
    nqb = T // Q_BLOCK
    n_cmp_pad = kc.shape[2]
    n_sel = T // SEL_BLOCK
    half_keys = SEL_STAGE_TILES * KEY_CHUNK
    NQ = NSA_STEP_QBLOCKS
    assert T % (SEL_GROUP_STAGES * half_keys) == 0 and nqb % NQ == 0
    assert FORCE_SCORE > NSA_GROUP
    per_head = lambda b, h, i: (b, h, 0, 0)
    v_rows = V_ROWS
    chunked = lambda w: pl.BlockSpec((1, T // w, v_rows, w), lambda b, h, i: (b, 0, h, 0))
    step_lanes = NQ * GQ
    return pl.pallas_call(
        functools.partial(_nsa_kernel, n_sel=n_sel, n_cmp_pad=n_cmp_pad),
        grid=(B, Hkv, nqb // NQ),
        in_specs=[pl.BlockSpec((1, NQ, 1, HEAD_DIM, GQ), lambda b, h, i: (b, i, h, 0, 0)),
                  pl.BlockSpec((1, 1, n_cmp_pad, HEAD_DIM), per_head),
                  pl.BlockSpec((1, 1, HEAD_DIM, n_cmp_pad), per_head),
                  pl.BlockSpec((1, T, LANES), lambda b, h, i: (b, 0, h)), chunked(half_keys),
                  pl.BlockSpec((1, T, LANES), lambda b, h, i: (b, 0, Hkv + h)), chunked(KEY_CHUNK),
                  pl.BlockSpec((1, NQ, 1, GATE_ROWS, Q_BLOCK), lambda b, h, i: (b, i, h, 0, 0)),
                  pl.BlockSpec(overlapT.shape, lambda b, h, i: (0, 0)),
                  pl.BlockSpec((1, N_ALL_TILES, KEY_CHUNK, GQ), lambda b, h, i: (h, 0, 0, 0)),
                  pl.BlockSpec((1, 2 * n_cmp_pad, GQ), lambda b, h, i: (h, 0, 0))],
        out_specs=pl.BlockSpec((1, NQ * Q_BLOCK, G * HEAD_DIM), lambda b, h, i: (b, i, h)),
        out_shape=jax.ShapeDtypeStruct((B, T, NSA_WIDTH), F32),
        scratch_shapes=[pltpu.VMEM((v_rows, step_lanes), F32), pltpu.VMEM((LANES, step_lanes), BF16),
                        pltpu.VMEM((n_sel, step_lanes), F32),
                        pltpu.VMEM((half_keys, step_lanes), F32), pltpu.VMEM((half_keys, step_lanes), F32),
                        pltpu.VMEM((half_keys, step_lanes), BF16), pltpu.VMEM((half_keys, step_lanes), BF16)],
        compiler_params=_params("parallel", "parallel", "arbitrary"), name="nsa_attention",
    )(q, kc, vcT, kaug, vsT, kaug, vwT, gates, overlapT, bias_tiles, cmp_bias)


def _rwkv_chunk_kernel(rw_ref, prev_ref, mu_ref, w0_ref, wup_ref, a0_ref, aup_ref, gup_ref, kk_ref, ka_ref,
                       rk_ref, seg_ref, tri_ref, q_ref, y0_ref, a_ref, d_ref, g_ref, bonus_ref):
    C, W, N = RWKV_CHUNK, RWKV_WIDTH, HEAD_DIM
    c = pl.program_id(1)
    x = rw_ref[0]
    R = x.shape[0]
    chunk_rows = [slice(ck * C, (ck + 1) * C) for ck in range(R // C)]
    row = lax.broadcasted_iota(jnp.int32, (R, 1), 0)
    last_prev = jnp.where(c == 0, 0.0, prev_ref[0, SUBLANES - 1:SUBLANES, :])
    x_prev = jnp.where(row == 0, last_prev, pltpu.roll(x, 1, axis=0))
    xs = x + (x_prev - x) * mu_ref[...]
    r, k, v = xs[:, 0:W], xs[:, W:2 * W], xs[:, 2 * W:3 * W]
    o = 3 * W
    wd, ad, gd = xs[:, o:o + DECAY_LORA], xs[:, o + DECAY_LORA:o + DECAY_LORA + AAA_LORA], \
        xs[:, o + DECAY_LORA + AAA_LORA:]
    w_log = -jax.nn.softplus(-(w0_ref[...] + _dot(jnp.tanh(wd).astype(BF16), wup_ref[...]))) - 0.5
    lw = -jnp.exp(w_log)
    lr = jax.nn.sigmoid(a0_ref[...] + _dot(ad.astype(BF16), aup_ref[...]))
    g_ref[0] = _dot(jax.nn.sigmoid(gd).astype(BF16), gup_ref[...])
    kk = k * kk_ref[...]
    def head_sums(z):
        return jnp.concatenate([_dot_f32_lhs(z[:, t * LANES:(t + 1) * LANES], seg_ref[...])
                                for t in range(W // LANES)], axis=1)
    kk = kk * lax.rsqrt(jnp.maximum(head_sums(kk * kk), 1e-24))
    k = k * (1.0 + (lr - 1.0) * ka_ref[...])
    bonus_ref[0] = head_sums(r * k * rk_ref[...]) * v
    a_vec, b_vec = -kk, kk * lr

    L = jnp.concatenate([_dot_f32_rhs(tri_ref[...], lw[rs]) for rs in chunk_rows], axis=0)
    L_end = jnp.concatenate([jnp.broadcast_to(L[rs.stop - 1:rs.stop, :], (C, W)) for rs in chunk_rows], axis=0)
    e_neg = jnp.exp(-L)
    e_rem = jnp.exp(L_end - L)
    At, Bt, Kt, Rt = a_vec * jnp.exp(L - lw), b_vec * e_neg, k * e_neg, r * jnp.exp(L)
    Bg, Kg = b_vec * e_rem, k * e_rem
    decay_end = [jnp.exp(L[rs.stop - 1:rs.stop, :]) for rs in chunk_rows]

    HG = RWKV_GROUP_HEADS
    GL = HG * N
    assert C == N and C & (C - 1) == 0
    blk_of = lambda idx: jnp.right_shift(idx, int(math.log2(C)))
    same_head = (blk_of(lax.broadcasted_iota(jnp.int32, (HG * C, GL), 0))
                 == blk_of(lax.broadcasted_iota(jnp.int32, (HG * C, GL), 1)))
    bf = lambda z: z.astype(BF16)
    block_diag = lambda y: jnp.where(same_head, jnp.concatenate([bf(y)] * HG, axis=0), 0.0)
    ri = lax.broadcasted_iota(jnp.int32, (C, GL), 0)
    cj = jnp.bitwise_and(lax.broadcasted_iota(jnp.int32, (C, GL), 1), C - 1)
    strict, incl = ri > cj, ri >= cj
    eye_c = (ri == cj).astype(F32)
    pr = lax.broadcasted_iota(jnp.int32, (N, 2 * N), 0)
    pc = lax.broadcasted_iota(jnp.int32, (N, 2 * N), 1)
    first_head, pair_eye = pc < N, (pr == jnp.bitwise_and(pc, N - 1)).astype(F32)

    def same_head_blocks(z):
        return jnp.where(first_head, z[0:N], z[N:2 * N])
    items = [(ck, gp) for ck in range(len(chunk_rows)) for gp in range(RWKV_HEADS // HG)]
    sl = [(chunk_rows[ck], slice(gp * GL, (gp + 1) * GL)) for ck, gp in items]
    ar = [bf(jnp.concatenate([At[s], Rt[s]], axis=0)) for s in sl]
    zb = [_dot_nt(a, block_diag(Bt[s])) for a, s in zip(ar, sl)]
    zk = [_dot_nt(a, block_diag(Kt[s])) for a, s in zip(ar, sl)]
    n_mat = [jnp.where(strict, z[0:C], 0.0) for z in zb]
    m_mat = [jnp.where(strict, z[0:C], 0.0) for z in zk]
    v_bd = [block_diag(v[s]) for s in sl]
    mv = [_dot(bf(m), vb) for m, vb in zip(m_mat, v_bd)]
    t_inv, n_pow = [eye_c + n for n in n_mat], n_mat
    for _ in range(int(math.log2(C)) - 1):
        n_pow = [_dot(bf(n), block_diag(n)) for n in n_pow]
        t_inv = [t + _dot(bf(t), block_diag(n)) for t, n in zip(t_inv, n_pow)]
    t_bf = [bf(t) for t in t_inv]
    ta = [_dot(t, block_diag(At[s])) for t, s in zip(t_bf, sl)]
    g0 = [_dot(t, block_diag(m)) for t, m in zip(t_bf, mv)]
    p_mat = [bf(jnp.where(incl, z[C:], 0.0)) for z in zb]
    pk_mat = [bf(jnp.where(incl, z[C:], 0.0)) for z in zk]
    q_out = [Rt[s] + _dot(p, block_diag(t)) for p, t, s in zip(p_mat, ta, sl)]
    y0_out = [_dot(p, block_diag(g)) + _dot(pk, vb) for p, g, pk, vb in zip(p_mat, g0, pk_mat, v_bd)]
    bgT = [bf(Bg[s].T) for s in sl]
    a_full = [_dot(b, bf(t)) for b, t in zip(bgT, ta)]
    d_full = [_dot(b, bf(g)) + _dot(bf(Kg[s].T), bf(v[s])) for b, g, s in zip(bgT, g0, sl)]
    for n, (ck, gp) in enumerate(items):
        rows, lanes = sl[n]
        q_ref[0, rows, lanes] = q_out[n].astype(q_ref.dtype)
        y0_ref[0, rows, lanes] = y0_out[n]
        for pp in range(HG // 2):
            blk = slice(pp * 2 * N, (pp + 1) * 2 * N)
            pair_lanes = slice(lanes.start + blk.start, lanes.start + blk.stop)
            a_ref[0, ck, gp * (HG // 2) + pp] = (same_head_blocks(a_full[n][blk, blk])
                                                 + pair_eye * decay_end[ck][:, pair_lanes])
            d_ref[0, ck, gp * (HG // 2) + pp] = same_head_blocks(d_full[n][blk, blk])


def _rwkv_chunks(rw, mu, w0, w_up, a0, a_up, g_up, k_k, k_a, r_k):
    B, T, cols = rw.shape
    C, W, H, N = RWKV_CHUNK, RWKV_WIDTH, RWKV_HEADS, HEAD_DIM
    nc = T // C
    S = RWKV_STEP_CHUNKS
    R = S * C
    assert nc % S == 0
    seg = jnp.asarray(np.kron(np.eye(LANES // N), np.ones((N, N))), BF16)
    tri = jnp.asarray(np.tril(np.ones((C, C))), BF16)
    row = lambda z: z.reshape(1, -1)
    const = lambda b, c: (0, 0)
    vec = pl.BlockSpec((1, W), const)
    tok = pl.BlockSpec((1, R, W), lambda b, c: (b, c, 0))
    mat = pl.BlockSpec((1, S, H // 2, N, 2 * N), lambda b, c: (b, c, 0, 0, 0))
    return pl.pallas_call(
        _rwkv_chunk_kernel,
        grid=(B, nc // S),
        in_specs=[pl.BlockSpec((1, R, cols), lambda b, c: (b, c, 0)),
                  pl.BlockSpec((1, SUBLANES, cols), lambda b, c: (b, jnp.maximum(c * (R // SUBLANES) - 1, 0), 0)),
                  pl.BlockSpec((1, cols), const), vec,
                  pl.BlockSpec((DECAY_LORA, W), const), vec,
                  pl.BlockSpec((AAA_LORA, W), const),
                  pl.BlockSpec((GATE_LORA, W), const), vec, vec, vec,
                  pl.BlockSpec((LANES, LANES), const), pl.BlockSpec((C, C), const)],
        out_specs=[tok, tok, mat, mat, tok, tok],
        out_shape=[jax.ShapeDtypeStruct((B, T, W), BF16), jax.ShapeDtypeStruct((B, T, W), F32),
                   jax.ShapeDtypeStruct((B, nc, H // 2, N, 2 * N), F32),
                   jax.ShapeDtypeStruct((B, nc, H // 2, N, 2 * N), F32),
                   jax.ShapeDtypeStruct((B, T, W), F32), jax.ShapeDtypeStruct((B, T, W), F32)],
        compiler_params=_params("parallel", "parallel"), name="rwkv_chunks",
    )(rw, rw, row(mu), row(w0), w_up.astype(BF16), row(a0), a_up.astype(BF16), g_up.astype(BF16),
      row(k_k), row(k_a), row(r_k), seg, tri)


def _pair_block_diag(x):
    first_head = lax.broadcasted_iota(jnp.int32, x.shape, 1) < x.shape[0]
    zero = jnp.zeros_like(x)
    return jnp.concatenate([jnp.where(first_head, x, zero), jnp.where(first_head, zero, x)], axis=0)


def _pair_side_by_side(z):
    n = z.shape[0] // 2
    return jnp.where(lax.broadcasted_iota(jnp.int32, (n, 2 * n), 1) < n, z[0:n], z[n:])


def _rwkv_state_kernel(a_ref, d_ref, h_ref, h_sc):
    @pl.when(pl.program_id(0) == 0)
    def _():
        h_sc[...] = jnp.zeros(h_sc.shape, F32)

    items = [(b, pair) for b in range(a_ref.shape[0]) for pair in range(a_ref.shape[2])]
    states = [h_sc[b, pair] for b, pair in items]
    for ck in range(a_ref.shape[1]):
        for (b, pair), st in zip(items, states):
            h_ref[b, ck, pair] = _pair_side_by_side(st).astype(h_ref.dtype)
        states = [_dot_hi_lo(_pair_block_diag(a_ref[b, ck, pair]), st) + _pair_block_diag(d_ref[b, ck, pair])
                  for (b, pair), st in zip(items, states)]
    for (b, pair), st in zip(items, states):
        h_sc[b, pair] = st


def _rwkv_readout(h_ref, q_ref, y0_ref, g_ref, bonus_ref, lw_ref, lb_ref, seg_ref):
    C = RWKV_CHUNK
    inv_n = 1.0 / HEAD_DIM
    n_chunks, n_pairs = h_ref.shape[1], h_ref.shape[2]
    items = [(ck, pair, slice(ck * C, (ck + 1) * C), slice(pair * LANES, (pair + 1) * LANES))
             for ck in range(n_chunks) for pair in range(n_pairs)]
    ys = [_dot(q_ref[0, rows, lanes], _pair_block_diag(h_ref[0, ck, pair])) + y0_ref[0, rows, lanes]
          for ck, pair, rows, lanes in items]
    first_head = lax.broadcasted_iota(jnp.int32, (C, LANES), 1) < HEAD_DIM

    def head_mean(z):
        s_a = jnp.sum(jnp.where(first_head, z, 0.0), axis=-1, keepdims=True)
        s_b = jnp.sum(jnp.where(first_head, 0.0, z), axis=-1, keepdims=True)
        return jnp.where(first_head, s_a, s_b) * inv_n
    means = [head_mean(y) for y in ys]
    cen = [y - mean for y, mean in zip(ys, means)]
    var = [head_mean(jnp.square(c)) for c in cen]
    out = []
    for (_, _, rows, lanes), c, v in zip(items, cen, var):
        yn = c * lax.rsqrt(v + LNX_EPS) * lw_ref[:, lanes] + lb_ref[:, lanes]
        out.append((yn + bonus_ref[0, rows, lanes]) * g_ref[0, rows, lanes])
    return jnp.concatenate([jnp.concatenate(out[ck * n_pairs:(ck + 1) * n_pairs], axis=1) for ck in range(n_chunks)],
                           axis=0)


def _rwkv_states(A, D):
    B, nc, P, N, N2 = A.shape
    S = RWKV_SCAN_STEP_CHUNKS
    assert nc % S == 0
    mat = pl.BlockSpec((B, S, P, N, N2), lambda c: (0, c, 0, 0, 0))
    return pl.pallas_call(
        _rwkv_state_kernel,
        grid=(nc // S,),
        in_specs=[mat, mat],
        out_specs=mat,
        out_shape=jax.ShapeDtypeStruct((B, nc, P, N, N2), BF16),
        scratch_shapes=[pltpu.VMEM((B, P, N2, N2), F32)],
        compiler_params=_params("arbitrary"), name="rwkv_state",
    )(A, D)


def _mix_xattn_kernel(x_ref, on_ref, h_ref, qr_ref, y0_ref, g_ref, bonus_ref, lw_ref, lb_ref, seg_ref,
                      wo1_ref, wo2_ref, gx_ref, wq_ref, k_ref, v_ref, wo_ref, o_ref):
    o_rwkv = _rwkv_readout(h_ref, qr_ref, y0_ref, g_ref, bonus_ref, lw_ref, lb_ref, seg_ref)
    x1 = x_ref[0] + _dot(on_ref[0].astype(BF16), wo1_ref[...]) + _dot(o_rwkv.astype(BF16), wo2_ref[...])
    q = _dot(_rms(x1, gx_ref[...]).astype(BF16), wq_ref[...])
    dh = q.shape[-1] // XATTN_HEADS
    qbf = (q * (dh ** -0.5)).astype(BF16)
    outs = []
    for h in range(XATTN_HEADS):
        hs = slice(h * dh, (h + 1) * dh)
        s = _dot_nt(qbf[:, hs], k_ref[0, :, hs])
        e = jnp.exp(s - jnp.max(s, axis=-1, keepdims=True))
        p = e / jnp.sum(e, axis=-1, keepdims=True)
        outs.append(_dot(p.astype(BF16), v_ref[0, :, hs]))
    o = jnp.concatenate(outs, axis=1).astype(BF16)
    o_ref[0] = x1 + _dot(o, wo_ref[...])


def _mix_xattn(x, o_nsa, rwkv_parts, w_out, norm_x_g, w_q, mem_k, mem_v, w_o):
    B, T, D = x.shape
    M = mem_k.shape[1]
    tm = min(ROW_TILE, T)
    assert tm % RWKV_CHUNK == 0
    h_start, Q, Y0, g, bonus, lnx_w, lnx_b = rwkv_parts
    _, _, P, N, N2 = h_start.shape
    W = RWKV_WIDTH
    seg = jnp.asarray(np.kron(np.eye(2), np.ones((HEAD_DIM, HEAD_DIM))), BF16)
    wo1, wo2 = w_out[:NSA_WIDTH].astype(BF16), w_out[NSA_WIDTH:].astype(BF16)
    const = lambda b, i: (0, 0)
    tile = lambda w: pl.BlockSpec((1, tm, w), lambda b, i: (b, i, 0))
    vec = pl.BlockSpec((1, W), const)
    return pl.pallas_call(
        _mix_xattn_kernel,
        grid=(B, T // tm),
        in_specs=[tile(D), tile(NSA_WIDTH),
                  pl.BlockSpec((1, tm // RWKV_CHUNK, P, N, N2), lambda b, i: (b, i, 0, 0, 0)),
                  tile(W), tile(W), tile(W), tile(W), vec, vec, pl.BlockSpec((N2, N2), const),
                  pl.BlockSpec(wo1.shape, const), pl.BlockSpec(wo2.shape, const),
                  pl.BlockSpec((1, D), const), pl.BlockSpec((D, D), const),
                  pl.BlockSpec((1, M, D), lambda b, i: (b, 0, 0)), pl.BlockSpec((1, M, D), lambda b, i: (b, 0, 0)),
                  pl.BlockSpec((D, D), const)],
        out_specs=tile(D),
        out_shape=jax.ShapeDtypeStruct((B, T, D), F32),
        compiler_params=_params("parallel", "parallel"), name="mix_xattn",
    )(x, o_nsa, h_start, Q, Y0, g, bonus, lnx_w.reshape(1, W), lnx_b.reshape(1, W), seg,
      wo1, wo2, norm_x_g.reshape(1, D), w_q.astype(BF16), mem_k, mem_v, w_o.astype(BF16))


def _ffn_kernel(x_ref, g_ref, wg_ref, wu_ref, wd_ref, gf_ref, o_ref, *, final_norm):
    x = x_ref[...]
    h = _rms(x, g_ref[...]).astype(BF16)
    act = (jax.nn.silu(_dot(h, wg_ref[...])) * _dot(h, wu_ref[...])).astype(BF16)
    y = x + _dot(act, wd_ref[...])
    o_ref[...] = _rms(y, gf_ref[...]) if final_norm else y


def _ffn(x, norm_g, w_gate, w_up, w_down, final_g, final_norm):
    R, D = x.shape
    F = w_gate.shape[1]
    tm = min(FFN_ROW_TILE, R)
    const = lambda i: (0, 0)
    resident = lambda shape: pl.BlockSpec(shape, const, pipeline_mode=pl.Buffered(1))
    return pl.pallas_call(
        functools.partial(_ffn_kernel, final_norm=final_norm),
        grid=(R // tm,),
        in_specs=[pl.BlockSpec((tm, D), lambda i: (i, 0)), pl.BlockSpec((1, D), const),
                  resident((D, F)), resident((D, F)), resident((F, D)), pl.BlockSpec((1, D), const)],
        out_specs=pl.BlockSpec((tm, D), lambda i: (i, 0)),
        out_shape=jax.ShapeDtypeStruct((R, D), F32),
        compiler_params=_params("parallel"), name="ffn",
    )(x, norm_g.reshape(1, D), w_gate.astype(BF16), w_up.astype(BF16), w_down.astype(BF16),
      final_g.reshape(1, D))


def _overlap_matrix(n_cmp_pad, n_sel):
    c = np.arange(n_cmp_pad)[:, None] * CMP_STRIDE
    s = np.arange(n_sel)[None, :] * SEL_BLOCK
    return ((c <= s + SEL_BLOCK - 1) & (c + CMP_LEN - 1 >= s)).astype(np.float32)


def _layer(x, mem, rel_bias, final_g, is_last, norm_mix_g, w_in, nsa_gate_b, cmp_pe_k, cmp_pe_v,
           cmp_k_w1, cmp_k_b1, cmp_k_w2, cmp_v_w1, cmp_v_b1, cmp_v_w2,
           rwkv_mu, rwkv_w0, rwkv_w_up, rwkv_a0, rwkv_a_up, rwkv_g_up,
           rwkv_k_k, rwkv_k_a, rwkv_r_k, rwkv_lnx_w, rwkv_lnx_b, w_out,
           norm_x_g, norm_mem_g, w_q_x, w_kv_x, w_o_x, norm_ffn_g, w_gate, w_up, w_down):
    B, T, D = x.shape
    q, kvc, kaug, vsT, vwT, gates, rw = _proj_in(x, norm_mix_g, w_in, nsa_gate_b)

    n16 = T // CMP_STRIDE
    kc = _compress(kvc, 0, cmp_pe_k, cmp_k_w1, cmp_k_b1, cmp_k_w2, False)
    vcT = _compress(kvc, 1, cmp_pe_v, cmp_v_w1, cmp_v_b1, cmp_v_w2, True)
    bias_tiles, cmp_bias = _bias_tiles(rel_bias, n16)
    o_nsa = _nsa_attention(q, kc, vcT, kaug, vsT, vwT,
                           gates, jnp.asarray(_overlap_matrix(n16, T // SEL_BLOCK).T, BF16), bias_tiles, cmp_bias)

    Q, Y0, A, Dm, g, bonus = _rwkv_chunks(rw, rwkv_mu, rwkv_w0, rwkv_w_up, rwkv_a0,
                                          rwkv_a_up, rwkv_g_up, rwkv_k_k, rwkv_k_a, rwkv_r_k.reshape(-1))
    rwkv_parts = (_rwkv_states(A, Dm), Q, Y0, g, bonus, rwkv_lnx_w, rwkv_lnx_b)

    M = mem.shape[1]
    (kv_mem,) = _norm_matmul(mem.reshape(B * M, D), norm_mem_g, [w_kv_x.astype(BF16)], [None], [BF16], ROW_TILE)
    kv_mem = kv_mem.reshape(B, M, 2 * D)
    x = _mix_xattn(x, o_nsa, rwkv_parts, w_out, norm_x_g, w_q_x, kv_mem[..., :D], kv_mem[..., D:], w_o_x)
    x = _ffn(x.reshape(B * T, D), norm_ffn_g, w_gate, w_up, w_down, final_g, is_last)
    return x.reshape(B, T, D)


def kernel(x, mem, rel_bias, norm_f_g, norm_mix_g, w_in, nsa_gate_b, cmp_pe_k, cmp_pe_v, cmp_k_w1, cmp_k_b1, cmp_k_w2, cmp_v_w1, cmp_v_b1, cmp_v_w2, rwkv_mu, rwkv_w0, rwkv_w_up, rwkv_a0, rwkv_a_up, rwkv_g_up, rwkv_k_k, rwkv_k_a, rwkv_r_k, rwkv_lnx_w, rwkv_lnx_b, w_out, norm_x_g, norm_mem_g, w_q_x, w_kv_x, w_o_x, norm_ffn_g, w_gate, w_up, w_down):
    stacked = (norm_mix_g, w_in, nsa_gate_b, cmp_pe_k, cmp_pe_v, cmp_k_w1, cmp_k_b1, cmp_k_w2, cmp_v_w1,
               cmp_v_b1, cmp_v_w2, rwkv_mu, rwkv_w0, rwkv_w_up, rwkv_a0, rwkv_a_up, rwkv_g_up, rwkv_k_k,
               rwkv_k_a, rwkv_r_k, rwkv_lnx_w, rwkv_lnx_b, w_out, norm_x_g, norm_mem_g, w_q_x, w_kv_x, w_o_x,
               norm_ffn_g, w_gate, w_up, w_down)
    depth = w_in.shape[0]
    for l in range(depth):
        x = _layer(x, mem, rel_bias, norm_f_g, l == depth - 1, *[p[l] for p in stacked])
    return x
```
